```python
import jax, jax.numpy as jnp
from jax import lax
import numpy as np

D_MODEL = 2048
BATCH = 8
SEQ = 2048
DEPTH = 4

NORM_EPS = 1e-5
ROPE_THETA = 500000.0
ROPE_FRACTION = 4
BAND_BLOCK = 128

A_DIM = 128
A_HEADS = D_MODEL // 2 // A_DIM
A_BRANCHES = ((128, 1), (512, 4), (2048, 16))
A_WIDTH = A_HEADS * A_DIM
B_KDIM = 128
B_VDIM = 128
B_HEADS = D_MODEL // 2 // B_VDIM
B_KWIDTH = B_HEADS * B_KDIM
B_WIDTH = B_HEADS * B_VDIM
B_CHUNK = 64
EVEN_SPLITS = (A_WIDTH, A_WIDTH, A_WIDTH, B_KWIDTH, B_KWIDTH, B_WIDTH, B_WIDTH)
EVEN_IN = sum(EVEN_SPLITS)
EVEN_MIX = A_WIDTH + B_WIDTH

C_DIM = 64
C_Q_HEADS = D_MODEL // C_DIM
C_KV_HEADS = C_Q_HEADS // 8
C_GROUP = C_Q_HEADS // C_KV_HEADS
C_WINDOW = 128
C_QKV = (C_Q_HEADS + 2 * C_KV_HEADS) * C_DIM

D_FF = 4 * D_MODEL
N_EVEN = (DEPTH + 1) // 2
N_ODD = DEPTH // 2

kernel_name = "hybrid_dilated_hgrn2_swa_sink_trunk"


def rmsnorm(x, g):
    xf = x.astype(jnp.float32)
    y = xf * lax.rsqrt(jnp.mean(jnp.square(xf), axis=-1, keepdims=True) + NORM_EPS)
    return (y * g.astype(jnp.float32)).astype(x.dtype)


def rope_tables(seq, head_dim):
    rot = head_dim // ROPE_FRACTION
    inv_freq = 1.0 / (ROPE_THETA ** (jnp.arange(0, rot, 2, dtype=jnp.float32) / rot))
    ang = jnp.arange(seq, dtype=jnp.float32)[:, None] * inv_freq[None, :]
    return jnp.cos(ang), jnp.sin(ang)


def apply_partial_rope(x, cos, sin):
    half = cos.shape[-1]
    xf = x.astype(jnp.float32)
    x1, x2, rest = xf[..., :half], xf[..., half:2 * half], xf[..., 2 * half:]
    out = jnp.concatenate([x1 * cos - x2 * sin, x2 * cos + x1 * sin, rest], axis=-1)
    return out.astype(x.dtype)


def banded_attention(q, k, v, max_dist, sink=None):
    L, D = q.shape[-2], q.shape[-1]
    nb = -(-L // BAND_BLOCK)
    pad_r = nb * BAND_BLOCK - L

    def pad_seq(t, left):
        return jnp.pad(t, [(0, 0)] * (t.ndim - 2) + [(left, pad_r), (0, 0)])

    qb = pad_seq(q, 0).reshape(q.shape[:-2] + (nb, BAND_BLOCK, D))
    kp = pad_seq(k, BAND_BLOCK).reshape(k.shape[:-2] + (nb + 1, BAND_BLOCK, D))
    vp = pad_seq(v, BAND_BLOCK).reshape(v.shape[:-2] + (nb + 1, BAND_BLOCK, D))
    kb = jnp.concatenate([kp[..., :-1, :, :], kp[..., 1:, :, :]], axis=-2)
    vb = jnp.concatenate([vp[..., :-1, :, :], vp[..., 1:, :, :]], axis=-2)
    s = jnp.einsum('...gnid,...njd->...gnij', qb, kb,
                   preferred_element_type=jnp.float32) * (D ** -0.5)
    qi = jnp.arange(BAND_BLOCK)[:, None]
    kj = jnp.arange(2 * BAND_BLOCK)[None, :]
    dist = qi + BAND_BLOCK - kj
    kpos = jnp.arange(nb)[:, None, None] * BAND_BLOCK - BAND_BLOCK + kj
    mask = (dist >= 0) & (dist <= max_dist) & (kpos >= 0)
    s = jnp.where(mask, s, -jnp.inf)
    m = jnp.max(s, axis=-1)
    if sink is not None:
        sink = sink.astype(jnp.float32)[..., None, None]
        m = jnp.maximum(m, sink)
    p = jnp.exp(s - m[..., None])
    l = jnp.sum(p, axis=-1)
    if sink is not None:
        l = l + jnp.exp(sink - m)
    num = jnp.einsum('...gnij,...njd->...gnid', p.astype(v.dtype), vb,
                     preferred_element_type=jnp.float32)
    num = num.reshape(num.shape[:-3] + (nb * BAND_BLOCK, D))[..., :L, :]
    m = m.reshape(m.shape[:-2] + (nb * BAND_BLOCK,))[..., :L]
    l = l.reshape(l.shape[:-2] + (nb * BAND_BLOCK,))[..., :L]
    return num, m, l


def dilated_attention(q, k, v):
    B, H, S, Dh = q.shape
    nums, ms, ls = [], [], []
    for window, dil in A_BRANCHES:
        L = S // dil

        def to_res(t):
            return jnp.swapaxes(t.reshape(B, H, L, dil, Dh), 2, 3)

        num, m, l = banded_attention(to_res(q)[..., None, :, :], to_res(k), to_res(v),
                                     window // dil)
        nums.append(jnp.swapaxes(num[..., 0, :, :], 2, 3).reshape(B, H, S, Dh))
        ms.append(jnp.swapaxes(m[..., 0, :], 2, 3).reshape(B, H, S))
        ls.append(jnp.swapaxes(l[..., 0, :], 2, 3).reshape(B, H, S))
    m_all = jnp.stack(ms)
    w = jnp.exp(m_all - jnp.max(m_all, axis=0, keepdims=True))
    num = jnp.sum(w[..., None] * jnp.stack(nums), axis=0)
    den = jnp.sum(w * jnp.stack(ls), axis=0)
    return num / den[..., None]


def hgrn2_chunkwise(q, k, v, log_f):
    B, H, S, K = q.shape
    V = v.shape[-1]
    n = S // B_CHUNK

    def chunks(t):
        return jnp.moveaxis(t.reshape(B, H, n, B_CHUNK, t.shape[-1]), 2, 0)

    causal = jnp.tril(jnp.ones((B_CHUNK, B_CHUNK), bool))

    def step(state, xs):
        qc, kc, vc, gc = xs
        b = jnp.cumsum(gc, axis=-2)
        diff = b[..., :, None, :] - b[..., None, :, :]
        decay = jnp.exp(jnp.where(causal[..., None], diff, -jnp.inf))
        attn = jnp.einsum('bhtk,bhsk,bhtsk->bhts', qc, kc, decay)
        o = (jnp.einsum('bhts,bhsv->bhtv', attn, vc)
             + jnp.einsum('bhtk,bhkv->bhtv', qc * jnp.exp(b), state))
        b_last = b[..., -1:, :]
        new_state = (jnp.exp(b_last[..., 0, :])[..., None] * state
                     + jnp.einsum('bhsk,bhsv->bhkv', kc * jnp.exp(b_last - b), vc))
        return new_state, o

    state0 = jnp.zeros((B, H, K, V), jnp.float32)
    _, o = lax.scan(step, state0, (chunks(q), chunks(k), chunks(v), chunks(log_f)))
    return jnp.moveaxis(o, 0, 2).reshape(B, H, S, V)


def split_heads(t, n_heads):
    B, S, _ = t.shape
    return t.reshape(B, S, n_heads, -1).transpose(0, 2, 1, 3)


def merge_heads(t):
    B, H, S, Dh = t.shape
    return t.transpose(0, 2, 1, 3).reshape(B, S, H * Dh)


def even_mixer(h, w_in, w_out, lower_bound, out_norm_g, cos_a, sin_a):
    proj = h @ w_in
    qa, ka, va, qb, fb, ib, gb = jnp.split(proj, np.cumsum(EVEN_SPLITS)[:-1].tolist(), axis=-1)
    qa = apply_partial_rope(split_heads(qa, A_HEADS), cos_a, sin_a)
    ka = apply_partial_rope(split_heads(ka, A_HEADS), cos_a, sin_a)
    oa = dilated_attention(qa, ka, split_heads(va, A_HEADS))
    lb = lower_bound.astype(jnp.float32).reshape(B_HEADS, 1, B_KDIM)
    gate = lb + (1.0 - lb) * jax.nn.sigmoid(split_heads(fb, B_HEADS).astype(jnp.float32))
    q_b = jax.nn.silu(split_heads(qb, B_HEADS).astype(jnp.float32)) * (B_KDIM ** -0.5)
    ob = hgrn2_chunkwise(q_b, 1.0 - gate, split_heads(ib, B_HEADS).astype(jnp.float32),
                         jnp.log(gate))
    ob = rmsnorm(ob, out_norm_g) * jax.nn.silu(split_heads(gb, B_HEADS).astype(jnp.float32))
    mixed = jnp.concatenate([merge_heads(oa), merge_heads(ob)], axis=-1)
    return mixed.astype(h.dtype) @ w_out


def odd_mixer(h, w_qkv, b_qkv, sinks, w_o, b_o, cos_c, sin_c):
    B, S, _ = h.shape
    proj = h @ w_qkv + b_qkv
    q, k, v = jnp.split(proj, [C_Q_HEADS * C_DIM, (C_Q_HEADS + C_KV_HEADS) * C_DIM], axis=-1)
    q = apply_partial_rope(split_heads(q, C_Q_HEADS), cos_c, sin_c)
    q = q.reshape(B, C_KV_HEADS, C_GROUP, S, C_DIM)
    k = apply_partial_rope(split_heads(k, C_KV_HEADS), cos_c, sin_c)
    v = split_heads(v, C_KV_HEADS)
    num, _, l = banded_attention(q, k, v, C_WINDOW - 1,
                                 sink=sinks.reshape(C_KV_HEADS, C_GROUP))
    o = (num / l[..., None]).reshape(B, C_Q_HEADS, S, C_DIM)
    return merge_heads(o).astype(h.dtype) @ w_o + b_o


def squared_relu_mlp(h, w1, w2):
    return jnp.square(jax.nn.relu(h @ w1)) @ w2


def _fwd_setup_inputs(seed: int = 0) -> dict:
    key = jax.random.key(seed)
    ks = jax.random.split(key, 15)

    def nrm(k, shape, scale):
        return scale * jax.random.normal(k, shape, jnp.float32)

    return {
        "x": nrm(ks[0], (BATCH, SEQ, D_MODEL), 1.0),
        "norm_mix_g": 1.0 + nrm(ks[1], (DEPTH, D_MODEL), 0.02),
        "norm_mlp_g": 1.0 + nrm(ks[2], (DEPTH, D_MODEL), 0.02),
        "final_norm_g": 1.0 + nrm(ks[3], (D_MODEL,), 0.02),
        "even_w_in": nrm(ks[4], (N_EVEN, D_MODEL, EVEN_IN), D_MODEL ** -0.5),
        "even_w_out": nrm(ks[5], (N_EVEN, EVEN_MIX, D_MODEL), EVEN_MIX ** -0.5),
        "hgrn_lb_raw": 1.0 + nrm(ks[6], (N_EVEN, B_KWIDTH), 0.1),
        "hgrn_norm_g": 1.0 + nrm(ks[7], (N_EVEN, B_VDIM), 0.02),
        "odd_w_qkv": nrm(ks[8], (N_ODD, D_MODEL, C_QKV), D_MODEL ** -0.5),
        "odd_b_qkv": nrm(ks[9], (N_ODD, C_QKV), 0.02),
        "odd_sinks": nrm(ks[10], (N_ODD, C_Q_HEADS), 1.0),
        "odd_w_o": nrm(ks[11], (N_ODD, C_Q_HEADS * C_DIM, D_MODEL), (C_Q_HEADS * C_DIM) ** -0.5),
        "odd_b_o": nrm(ks[12], (N_ODD, D_MODEL), 0.02),
        "mlp_w1": nrm(ks[13], (DEPTH, D_MODEL, D_FF), D_MODEL ** -0.5),
        "mlp_w2": nrm(ks[14], (DEPTH, D_FF, D_MODEL), D_FF ** -0.5),
    }


def _fwd_reference(x, norm_mix_g, norm_mlp_g, final_norm_g, even_w_in, even_w_out, hgrn_lb_raw,
              hgrn_norm_g, odd_w_qkv, odd_b_qkv, odd_sinks, odd_w_o, odd_b_o, mlp_w1, mlp_w2):
    S = x.shape[1]
    cos_a, sin_a = rope_tables(S, A_DIM)
    cos_c, sin_c = rope_tables(S, C_DIM)
    lb_soft = jax.nn.softmax(hgrn_lb_raw.astype(jnp.float32), axis=0)
    lower_bounds = jnp.cumsum(lb_soft, axis=0) - lb_soft[0:1]
    for layer in range(DEPTH):
        h = rmsnorm(x, norm_mix_g[layer])
        if layer % 2 == 0:
            e = layer // 2
            mix = even_mixer(h, even_w_in[e], even_w_out[e], lower_bounds[e], hgrn_norm_g[e],
                             cos_a, sin_a)
        else:
            o = layer // 2
            mix = odd_mixer(h, odd_w_qkv[o], odd_b_qkv[o], odd_sinks[o], odd_w_o[o], odd_b_o[o],
                            cos_c, sin_c)
        x = x + mix.astype(x.dtype)
        h = rmsnorm(x, norm_mlp_g[layer])
        x = x + squared_relu_mlp(h, mlp_w1[layer], mlp_w2[layer]).astype(x.dtype)
    return rmsnorm(x, final_norm_g)


import jax as _jax
import jax.numpy as _jnp

TWIN_FORMAT = 'train_step'
FWD_PARAMS = ['x', 'norm_mix_g', 'norm_mlp_g', 'final_norm_g', 'even_w_in', 'even_w_out', 'hgrn_lb_raw', 'hgrn_norm_g', 'odd_w_qkv', 'odd_b_qkv', 'odd_sinks', 'odd_w_o', 'odd_b_o', 'mlp_w1', 'mlp_w2']
TWIN_WEIGHTS = ['norm_mix_g', 'norm_mlp_g', 'final_norm_g', 'even_w_in', 'even_w_out', 'hgrn_lb_raw', 'hgrn_norm_g', 'odd_w_qkv', 'odd_b_qkv', 'odd_sinks', 'odd_w_o', 'odd_b_o', 'mlp_w1', 'mlp_w2']
TWIN_DIFF_INPUT = 'x'
TWIN_INPUTS = ['x', 'norm_mix_g', 'norm_mlp_g', 'final_norm_g', 'even_w_in', 'even_w_out', 'hgrn_lb_raw', 'hgrn_norm_g', 'odd_w_qkv', 'odd_b_qkv', 'odd_sinks', 'odd_w_o', 'odd_b_o', 'mlp_w1', 'mlp_w2', 'loss_target', 'm_norm_mix_g', 'm_norm_mlp_g', 'm_final_norm_g', 'm_even_w_in', 'm_even_w_out', 'm_hgrn_lb_raw', 'm_hgrn_norm_g', 'm_odd_w_qkv', 'm_odd_b_qkv', 'm_odd_sinks', 'm_odd_w_o', 'm_odd_b_o', 'm_mlp_w1', 'm_mlp_w2', 'v_norm_mix_g', 'v_norm_mlp_g', 'v_final_norm_g', 'v_even_w_in', 'v_even_w_out', 'v_hgrn_lb_raw', 'v_hgrn_norm_g', 'v_odd_w_qkv', 'v_odd_b_qkv', 'v_odd_sinks', 'v_odd_w_o', 'v_odd_b_o', 'v_mlp_w1', 'v_mlp_w2']
TWIN_OUTPUTS = ['loss', 'grad_x', 'grad_norm_mix_g', 'grad_norm_mlp_g', 'grad_final_norm_g', 'grad_even_w_in', 'grad_even_w_out', 'grad_hgrn_lb_raw', 'grad_hgrn_norm_g', 'grad_odd_w_qkv', 'grad_odd_b_qkv', 'grad_odd_sinks', 'grad_odd_w_o', 'grad_odd_b_o', 'grad_mlp_w1', 'grad_mlp_w2', 'delta_norm_mix_g', 'delta_norm_mlp_g', 'delta_final_norm_g', 'delta_even_w_in', 'delta_even_w_out', 'delta_hgrn_lb_raw', 'delta_hgrn_norm_g', 'delta_odd_w_qkv', 'delta_odd_b_qkv', 'delta_odd_sinks', 'delta_odd_w_o', 'delta_odd_b_o', 'delta_mlp_w1', 'delta_mlp_w2', 'new_m_norm_mix_g', 'new_m_norm_mlp_g', 'new_m_final_norm_g', 'new_m_even_w_in', 'new_m_even_w_out', 'new_m_hgrn_lb_raw', 'new_m_hgrn_norm_g', 'new_m_odd_w_qkv', 'new_m_odd_b_qkv', 'new_m_odd_sinks', 'new_m_odd_w_o', 'new_m_odd_b_o', 'new_m_mlp_w1', 'new_m_mlp_w2', 'new_v_norm_mix_g', 'new_v_norm_mlp_g', 'new_v_final_norm_g', 'new_v_even_w_in', 'new_v_even_w_out', 'new_v_hgrn_lb_raw', 'new_v_hgrn_norm_g', 'new_v_odd_w_qkv', 'new_v_odd_b_qkv', 'new_v_odd_sinks', 'new_v_odd_w_o', 'new_v_odd_b_o', 'new_v_mlp_w1', 'new_v_mlp_w2']
TWIN_LEAF_KINDS = {'loss': 'loss', 'grad_x': 'grad_x', 'grad_norm_mix_g': 'grad_w', 'grad_norm_mlp_g': 'grad_w', 'grad_final_norm_g': 'grad_w', 'grad_even_w_in': 'grad_w', 'grad_even_w_out': 'grad_w', 'grad_hgrn_lb_raw': 'grad_w', 'grad_hgrn_norm_g': 'grad_w', 'grad_odd_w_qkv': 'grad_w', 'grad_odd_b_qkv': 'grad_w', 'grad_odd_sinks': 'grad_w', 'grad_odd_w_o': 'grad_w', 'grad_odd_b_o': 'grad_w', 'grad_mlp_w1': 'grad_w', 'grad_mlp_w2': 'grad_w', 'delta_norm_mix_g': 'delta_w', 'delta_norm_mlp_g': 'delta_w', 'delta_final_norm_g': 'delta_w', 'delta_even_w_in': 'delta_w', 'delta_even_w_out': 'delta_w', 'delta_hgrn_lb_raw': 'delta_w', 'delta_hgrn_norm_g': 'delta_w', 'delta_odd_w_qkv': 'delta_w', 'delta_odd_b_qkv': 'delta_w', 'delta_odd_sinks': 'delta_w', 'delta_odd_w_o': 'delta_w', 'delta_odd_b_o': 'delta_w', 'delta_mlp_w1': 'delta_w', 'delta_mlp_w2': 'delta_w', 'new_m_norm_mix_g': 'new_m', 'new_m_norm_mlp_g': 'new_m', 'new_m_final_norm_g': 'new_m', 'new_m_even_w_in': 'new_m', 'new_m_even_w_out': 'new_m', 'new_m_hgrn_lb_raw': 'new_m', 'new_m_hgrn_norm_g': 'new_m', 'new_m_odd_w_qkv': 'new_m', 'new_m_odd_b_qkv': 'new_m', 'new_m_odd_sinks': 'new_m', 'new_m_odd_w_o': 'new_m', 'new_m_odd_b_o': 'new_m', 'new_m_mlp_w1': 'new_m', 'new_m_mlp_w2': 'new_m', 'new_v_norm_mix_g': 'new_v', 'new_v_norm_mlp_g': 'new_v', 'new_v_final_norm_g': 'new_v', 'new_v_even_w_in': 'new_v', 'new_v_even_w_out': 'new_v', 'new_v_hgrn_lb_raw': 'new_v', 'new_v_hgrn_norm_g': 'new_v', 'new_v_odd_w_qkv': 'new_v', 'new_v_odd_b_qkv': 'new_v', 'new_v_odd_sinks': 'new_v', 'new_v_odd_w_o': 'new_v', 'new_v_odd_b_o': 'new_v', 'new_v_mlp_w1': 'new_v', 'new_v_mlp_w2': 'new_v'}


def _forward(args):
    return _fwd_reference(*[args[k] for k in FWD_PARAMS])


def _output_shape():
    out = _jax.eval_shape(lambda: _forward(_fwd_setup_inputs(0)))
    return out.shape, out.dtype

N_MICROBATCH = 1
ADAM_LR = 0.001
ADAM_B1 = 0.9
ADAM_B2 = 0.999
ADAM_EPS = 1e-08
ADAM_WD = 0.01
ADAM_STEP = 10
PER_EXAMPLE_BATCH_AXIS = {'x': 0, 'loss_target': 0}
SHARED_INPUTS = []
_WEIGHT_DTYPES = {'norm_mix_g': _jnp.float32, 'norm_mlp_g': _jnp.float32, 'final_norm_g': _jnp.float32, 'even_w_in': _jnp.float32, 'even_w_out': _jnp.float32, 'hgrn_lb_raw': _jnp.float32, 'hgrn_norm_g': _jnp.float32, 'odd_w_qkv': _jnp.float32, 'odd_b_qkv': _jnp.float32, 'odd_sinks': _jnp.float32, 'odd_w_o': _jnp.float32, 'odd_b_o': _jnp.float32, 'mlp_w1': _jnp.float32, 'mlp_w2': _jnp.float32}
MOMENT_SCALE = {'norm_mix_g': 2.919232e-02, 'norm_mlp_g': 4.689417e-02, 'final_norm_g': 8.437283e+00, 'even_w_in': 1.923455e-02, 'even_w_out': 2.550737e-02, 'hgrn_lb_raw': 1.262548e-03, 'hgrn_norm_g': 1.009622e-01, 'odd_w_qkv': 2.053890e-02, 'odd_b_qkv': 6.095450e-02, 'odd_sinks': 6.942603e-03, 'odd_w_o': 2.337661e-02, 'odd_b_o': 6.530527e-02, 'mlp_w1': 2.331770e-02, 'mlp_w2': 4.914072e-02}


def _to_microbatches(a, axis):
    t = _jnp.moveaxis(a, axis, 0)
    t = t.reshape((N_MICROBATCH, t.shape[0] // N_MICROBATCH) + t.shape[1:])
    return _jnp.moveaxis(t, 1, axis + 1)


def setup_inputs(seed: int = 0) -> dict:
    inp = _fwd_setup_inputs(seed)
    key = _jax.random.fold_in(_jax.random.key(seed), 7919)
    shape, _ = _output_shape()
    out = dict(inp)
    out["loss_target"] = _jax.random.normal(_jax.random.fold_in(key, 0), shape, _jnp.float32)
    for i, name in enumerate(TWIN_WEIGHTS):
        w = inp[name].astype(_jnp.float32)
        if MOMENT_SCALE is None:
            s = _jnp.sqrt(_jnp.mean(_jnp.square(w)) + 1e-30)
        else:
            s = MOMENT_SCALE[name]
        km, kv = _jax.random.split(_jax.random.fold_in(key, i + 1))
        out[name] = w
        out["m_" + name] = s * _jax.random.normal(km, w.shape, _jnp.float32)
        out["v_" + name] = (s * s) * _jax.random.uniform(kv, w.shape, _jnp.float32, 0.5, 1.5)
    if N_MICROBATCH > 1:
        for name, axis in PER_EXAMPLE_BATCH_AXIS.items():
            out[name] = _to_microbatches(out[name], axis)
    return {'x': out['x'], 'norm_mix_g': out['norm_mix_g'], 'norm_mlp_g': out['norm_mlp_g'], 'final_norm_g': out['final_norm_g'], 'even_w_in': out['even_w_in'], 'even_w_out': out['even_w_out'], 'hgrn_lb_raw': out['hgrn_lb_raw'], 'hgrn_norm_g': out['hgrn_norm_g'], 'odd_w_qkv': out['odd_w_qkv'], 'odd_b_qkv': out['odd_b_qkv'], 'odd_sinks': out['odd_sinks'], 'odd_w_o': out['odd_w_o'], 'odd_b_o': out['odd_b_o'], 'mlp_w1': out['mlp_w1'], 'mlp_w2': out['mlp_w2'], 'loss_target': out['loss_target'], 'm_norm_mix_g': out['m_norm_mix_g'], 'm_norm_mlp_g': out['m_norm_mlp_g'], 'm_final_norm_g': out['m_final_norm_g'], 'm_even_w_in': out['m_even_w_in'], 'm_even_w_out': out['m_even_w_out'], 'm_hgrn_lb_raw': out['m_hgrn_lb_raw'], 'm_hgrn_norm_g': out['m_hgrn_norm_g'], 'm_odd_w_qkv': out['m_odd_w_qkv'], 'm_odd_b_qkv': out['m_odd_b_qkv'], 'm_odd_sinks': out['m_odd_sinks'], 'm_odd_w_o': out['m_odd_w_o'], 'm_odd_b_o': out['m_odd_b_o'], 'm_mlp_w1': out['m_mlp_w1'], 'm_mlp_w2': out['m_mlp_w2'], 'v_norm_mix_g': out['v_norm_mix_g'], 'v_norm_mlp_g': out['v_norm_mlp_g'], 'v_final_norm_g': out['v_final_norm_g'], 'v_even_w_in': out['v_even_w_in'], 'v_even_w_out': out['v_even_w_out'], 'v_hgrn_lb_raw': out['v_hgrn_lb_raw'], 'v_hgrn_norm_g': out['v_hgrn_norm_g'], 'v_odd_w_qkv': out['v_odd_w_qkv'], 'v_odd_b_qkv': out['v_odd_b_qkv'], 'v_odd_sinks': out['v_odd_sinks'], 'v_odd_w_o': out['v_odd_w_o'], 'v_odd_b_o': out['v_odd_b_o'], 'v_mlp_w1': out['v_mlp_w1'], 'v_mlp_w2': out['v_mlp_w2']}


def _loss(weights, diff, rest, loss_target):
    with _jax.named_scope("forward"):
        args = {**rest, TWIN_DIFF_INPUT: diff, **{k: w.astype(_WEIGHT_DTYPES[k]) for k, w in weights.items()}}
        y = _forward(args)
    with _jax.named_scope("loss_head"):
        err = _jnp.square(y.astype(_jnp.float32) - loss_target)
        return 0.5 * _jnp.sum(_jnp.mean(err, axis=-1)) if err.ndim else 0.5 * err


def _adamw(w, g, m, v):
    m = ADAM_B1 * m + (1.0 - ADAM_B1) * g
    v = ADAM_B2 * v + (1.0 - ADAM_B2) * _jnp.square(g)
    m_hat = m / (1.0 - ADAM_B1 ** ADAM_STEP)
    v_hat = v / (1.0 - ADAM_B2 ** ADAM_STEP)
    delta = -ADAM_LR * (m_hat / (_jnp.sqrt(v_hat) + ADAM_EPS) + ADAM_WD * w)
    return delta, m, v


def reference(x, norm_mix_g, norm_mlp_g, final_norm_g, even_w_in, even_w_out, hgrn_lb_raw, hgrn_norm_g, odd_w_qkv, odd_b_qkv, odd_sinks, odd_w_o, odd_b_o, mlp_w1, mlp_w2, loss_target, m_norm_mix_g, m_norm_mlp_g, m_final_norm_g, m_even_w_in, m_even_w_out, m_hgrn_lb_raw, m_hgrn_norm_g, m_odd_w_qkv, m_odd_b_qkv, m_odd_sinks, m_odd_w_o, m_odd_b_o, m_mlp_w1, m_mlp_w2, v_norm_mix_g, v_norm_mlp_g, v_final_norm_g, v_even_w_in, v_even_w_out, v_hgrn_lb_raw, v_hgrn_norm_g, v_odd_w_qkv, v_odd_b_qkv, v_odd_sinks, v_odd_w_o, v_odd_b_o, v_mlp_w1, v_mlp_w2):
    given = dict(x=x, norm_mix_g=norm_mix_g, norm_mlp_g=norm_mlp_g, final_norm_g=final_norm_g, even_w_in=even_w_in, even_w_out=even_w_out, hgrn_lb_raw=hgrn_lb_raw, hgrn_norm_g=hgrn_norm_g, odd_w_qkv=odd_w_qkv, odd_b_qkv=odd_b_qkv, odd_sinks=odd_sinks, odd_w_o=odd_w_o, odd_b_o=odd_b_o, mlp_w1=mlp_w1, mlp_w2=mlp_w2, loss_target=loss_target, m_norm_mix_g=m_norm_mix_g, m_norm_mlp_g=m_norm_mlp_g, m_final_norm_g=m_final_norm_g, m_even_w_in=m_even_w_in, m_even_w_out=m_even_w_out, m_hgrn_lb_raw=m_hgrn_lb_raw, m_hgrn_norm_g=m_hgrn_norm_g, m_odd_w_qkv=m_odd_w_qkv, m_odd_b_qkv=m_odd_b_qkv, m_odd_sinks=m_odd_sinks, m_odd_w_o=m_odd_w_o, m_odd_b_o=m_odd_b_o, m_mlp_w1=m_mlp_w1, m_mlp_w2=m_mlp_w2, v_norm_mix_g=v_norm_mix_g, v_norm_mlp_g=v_norm_mlp_g, v_final_norm_g=v_final_norm_g, v_even_w_in=v_even_w_in, v_even_w_out=v_even_w_out, v_hgrn_lb_raw=v_hgrn_lb_raw, v_hgrn_norm_g=v_hgrn_norm_g, v_odd_w_qkv=v_odd_w_qkv, v_odd_b_qkv=v_odd_b_qkv, v_odd_sinks=v_odd_sinks, v_odd_w_o=v_odd_w_o, v_odd_b_o=v_odd_b_o, v_mlp_w1=v_mlp_w1, v_mlp_w2=v_mlp_w2)
    weights = {n: given[n] for n in TWIN_WEIGHTS}
    shared = {n: given[n] for n in SHARED_INPUTS}
    per_example = {n: given[n] for n in ['x']}
    grad_fn = _jax.value_and_grad(_loss, argnums=(0, 1))

    def one_microbatch(ex, loss_target):
        ex = dict(ex)
        diff = ex.pop(TWIN_DIFF_INPUT)
        return grad_fn(weights, diff, {**shared, **ex}, loss_target)

    if N_MICROBATCH == 1:
        loss, (grad_w, grad_x) = one_microbatch(per_example, given["loss_target"])
    else:
        def body(carry, xs):
            loss_sum, grad_sum = carry
            l_k, (gw_k, gx_k) = one_microbatch(xs[0], xs[1])
            with _jax.named_scope("update"):
                return (loss_sum + l_k, _jax.tree.map(_jnp.add, grad_sum, gw_k)), gx_k

        init = (_jnp.zeros((), _jnp.float32), _jax.tree.map(_jnp.zeros_like, weights))
        (loss, grad_w), grad_x = _jax.lax.scan(body, init, (per_example, given["loss_target"]))
    with _jax.named_scope("update"):
        delta_w, new_m, new_v = {}, {}, {}
        for n in TWIN_WEIGHTS:
            delta_w[n], new_m[n], new_v[n] = _adamw(weights[n], grad_w[n], given["m_" + n], given["v_" + n])
    return (loss, grad_x, *[grad_w[n] for n in TWIN_WEIGHTS], *[delta_w[n] for n in TWIN_WEIGHTS],
            *[new_m[n] for n in TWIN_WEIGHTS], *[new_v[n] for n in TWIN_WEIGHTS])
```

```python
import functools
import math

import jax
import jax.numpy as jnp
from jax import lax
from jax.experimental import pallas as pl
from jax.experimental.pallas import tpu as pltpu

F32 = jnp.float32
BF16 = jnp.bfloat16
MESH = pl.DeviceIdType.MESH

N_DEV = 8
NORM_EPS = 1e-5
ROPE_THETA = 500000.0
BLK = 128
A_DIM = 128
A_BRANCHES = ((128, 1), (512, 4), (2048, 16))
B_DIM = 128
B_CHUNK = 64
C_DIM = 64
C_GROUP = 8
C_WINDOW = 128
LANES = 128

ADAM_LR = 0.001
ADAM_B1 = 0.9
ADAM_B2 = 0.999
ADAM_EPS = 1e-08
ADAM_WD = 0.01
ADAM_STEP = 10

NN = (((1,), (0,)), ((), ()))
NT = (((1,), (1,)), ((), ()))
TN = (((0,), (0,)), ((), ()))


def _params(*sem):
    return pltpu.CompilerParams(dimension_semantics=sem)


def _sigmoid(x):
    return 1.0 / (1.0 + jnp.exp(-x))


def _rows_call(name, body, row_ins, full_ins, row_outs, acc_outs, tm):
    t = row_ins[0].shape[0]
    n_ri, n_fi, n_ro = len(row_ins), len(full_ins), len(row_outs)

    def kern(*refs):
        i = pl.program_id(0)
        body(i, refs[:n_ri], refs[n_ri:n_ri + n_fi],
             refs[n_ri + n_fi:n_ri + n_fi + n_ro], refs[n_ri + n_fi + n_ro:])

    def row_spec(shape):
        return pl.BlockSpec((tm,) + tuple(shape[1:]), lambda i: (i,) + (0,) * (len(shape) - 1))

    def full_spec(shape):
        return pl.BlockSpec(tuple(shape), lambda i: (0,) * len(shape))

    outs = pl.pallas_call(
        kern, name=name, grid=(t // tm,),
        in_specs=[row_spec(a.shape) for a in row_ins] + [full_spec(a.shape) for a in full_ins],
        out_specs=[row_spec(s.shape) for s in row_outs] + [full_spec(s.shape) for s in acc_outs],
        out_shape=list(row_outs) + list(acc_outs),
        compiler_params=_params("arbitrary" if acc_outs else "parallel"),
    )(*row_ins, *full_ins)
    return outs


def _sds(shape, dtype):
    return jax.ShapeDtypeStruct(tuple(shape), dtype)


def rms_fwd(x, g, name):
    t, d = x.shape

    def body(i, ri, fi, ro, ao):
        xv = ri[0][...]
        r = lax.rsqrt(jnp.mean(xv * xv, axis=-1, keepdims=True) + NORM_EPS)
        ro[0][...] = (xv * r * fi[0][...]).astype(BF16)

    return _rows_call(name, body, [x], [g.reshape(1, d)], [_sds((t, d), BF16)], [], 256)[0]


def rms_bwd(x, g, dh, dx_res, name):
    t, d = x.shape

    def body(i, ri, fi, ro, ao):
        xv, dhv, res = ri[0][...], ri[1][...], ri[2][...]
        gv = fi[0][...]
        r = lax.rsqrt(jnp.mean(xv * xv, axis=-1, keepdims=True) + NORM_EPS)
        gd = gv * dhv
        dx = res + r * gd - xv * (r * r * r) * jnp.mean(xv * gd, axis=-1, keepdims=True)
        ro[0][...] = dx
        ro[1][...] = dx.astype(BF16)

        @pl.when(i == 0)
        def _():
            ao[0][...] = jnp.zeros_like(ao[0])
            ao[1][...] = jnp.zeros_like(ao[1])

        ao[0][...] += jnp.sum(dhv * xv * r, axis=0, keepdims=True)
        ao[1][...] += jnp.sum(dx, axis=0, keepdims=True)

    return _rows_call(name, body, [x, dh, dx_res], [g.reshape(1, d)],
                      [_sds((t, d), F32), _sds((t, d), BF16)],
                      [_sds((1, d), F32), _sds((1, d), F32)], 256)


def loss_head(x, g, target, name):
    t, d = x.shape

    def body(i, ri, fi, ro, ao):
        xv, tg = ri[0][...], ri[1][...]
        gv = fi[0][...]
        r = lax.rsqrt(jnp.mean(xv * xv, axis=-1, keepdims=True) + NORM_EPS)
        e = xv * r * gv - tg
        dy = e * (1.0 / d)
        gd = gv * dy
        dx = r * gd - xv * (r * r * r) * jnp.mean(xv * gd, axis=-1, keepdims=True)
        ro[0][...] = dx
        ro[1][...] = dx.astype(BF16)

        @pl.when(i == 0)
        def _():
            ao[0][...] = jnp.zeros_like(ao[0])
            ao[1][...] = jnp.zeros_like(ao[1])

        ao[0][...] += jnp.sum(dy * xv * r, axis=0, keepdims=True)
        part = 0.5 * jnp.sum(jnp.mean(e * e, axis=-1, keepdims=True), axis=0, keepdims=True)
        ao[1][...] += jnp.broadcast_to(part, (1, LANES))

    return _rows_call(name, body, [x, target], [g.reshape(1, d)],
                      [_sds((t, d), F32), _sds((t, d), BF16)],
                      [_sds((1, d), F32), _sds((1, LANES), F32)], 256)


def rope_tables(seq, head_dim):
    rot = head_dim // 4
    half = rot // 2
    inv_freq = 1.0 / (ROPE_THETA ** (jnp.arange(0, rot, 2, dtype=F32) / rot))
    ang = jnp.arange(seq, dtype=F32)[:, None] * inv_freq[None, :]
    cos, sin = jnp.cos(ang), jnp.sin(ang)
    zeros = jnp.zeros((seq, head_dim - rot), F32)
    zh = jnp.zeros((seq, half), F32)
    c = jnp.concatenate([cos, cos, jnp.ones((seq, head_dim - rot), F32)], axis=-1)
    sp = jnp.concatenate([zh, sin, zeros], axis=-1)
    sm = jnp.concatenate([-sin, zh, zeros], axis=-1)
    rep = LANES // head_dim
    return jnp.tile(c, (1, rep)), jnp.tile(sp, (1, rep)), jnp.tile(sm, (1, rep)), half


def rope_call(x, tabs, width, n_rope, inverse, name, col_sum=False):
    c, sp, sm, half = tabs
    t = x.shape[0]
    tm = 256
    n_slab = width // LANES

    def kern(x_ref, c_ref, sp_ref, sm_ref, o_ref, *acc):
        cv, spv, smv = c_ref[...], sp_ref[...], sm_ref[...]
        for j in range(n_slab):
            xs = x_ref[:, j * LANES:(j + 1) * LANES].astype(F32)
            if j < n_rope:
                if inverse:
                    ys = (xs * cv + pltpu.roll(xs * spv, LANES - half, 1)
                          + pltpu.roll(xs * smv, half, 1))
                else:
                    ys = (xs * cv + pltpu.roll(xs, half, 1) * spv
                          + pltpu.roll(xs, LANES - half, 1) * smv)
            else:
                ys = xs
            o_ref[:, j * LANES:(j + 1) * LANES] = ys.astype(BF16)
            if col_sum:
                @pl.when(pl.program_id(0) == 0)
                def _():
                    acc[0][:, j * LANES:(j + 1) * LANES] = jnp.zeros((1, LANES), F32)
                acc[0][:, j * LANES:(j + 1) * LANES] += jnp.sum(ys, axis=0, keepdims=True)

    tab_spec = pl.BlockSpec((tm, LANES), lambda i: (i, 0))
    out_shape = [_sds((t, width), BF16)]
    out_specs = [pl.BlockSpec((tm, width), lambda i: (i, 0))]
    if col_sum:
        out_shape.append(_sds((1, width), F32))
        out_specs.append(pl.BlockSpec((1, width), lambda i: (0, 0)))
    return pl.pallas_call(
        kern, name=name, grid=(t // tm,),
        in_specs=[pl.BlockSpec((tm, width), lambda i: (i, 0)), tab_spec, tab_spec, tab_spec],
        out_specs=out_specs, out_shape=out_shape,
        compiler_params=_params("arbitrary" if col_sum else "parallel"),
    )(x, c, sp, sm)


def _mm_call(name, a, b, extras, out_shapes, grid, a_spec, b_spec, extra_specs, out_specs,
             acc_shape, dims, epilogue):
    n_ex, n_out = len(extras), len(out_shapes)
    nk = grid[2]

    def kern(*refs):
        a_ref, b_ref = refs[0], refs[1]
        ex = refs[2:2 + n_ex]
        outs = refs[2 + n_ex:2 + n_ex + n_out]
        acc = refs[-1]
        k = pl.program_id(2)

        @pl.when(k == 0)
        def _():
            acc[...] = jnp.zeros_like(acc)

        acc[...] += lax.dot_general(a_ref[...].astype(BF16), b_ref[...].astype(BF16), dims,
                                    preferred_element_type=F32)

        @pl.when(k == nk - 1)
        def _():
            epilogue(acc[...], ex, outs)

    return pl.pallas_call(
        kern, name=name, grid=grid,
        in_specs=[a_spec, b_spec, *extra_specs], out_specs=out_specs, out_shape=out_shapes,
        scratch_shapes=[pltpu.VMEM(acc_shape, F32)],
        compiler_params=_params("parallel", "parallel", "arbitrary"),
    )(a, b, *extras)


def _ep_store(dtype):
    def ep(acc, ex, outs):
        outs[0][...] = acc.astype(dtype)
    return ep


def _ep_residual(acc, ex, outs):
    outs[0][...] = acc + ex[0][...]


def _ep_bias(acc, ex, outs):
    outs[0][...] = acc + ex[0][...]


def _ep_bias_residual(acc, ex, outs):
    outs[0][...] = acc + ex[0][...] + ex[1][...]


def _ep_relu2(acc, ex, outs):
    outs[0][...] = acc
    rl = jnp.maximum(acc, 0.0)
    outs[1][...] = (rl * rl).astype(BF16)


def _ep_relu2_bwd(acc, ex, outs):
    outs[0][...] = (acc * (2.0 * jnp.maximum(ex[0][...], 0.0))).astype(BF16)


def mm_cols_sharded(a, wg, layer, name, epilogue=None, n_out=1, tm=1024, tk=512):
    m, kdim = a.shape
    n = wg.shape[-1]
    if epilogue is None:
        epilogue, outs = _ep_store(F32), [_sds((m, N_DEV * n), F32)]
    else:
        outs = [_sds((m, N_DEV * n), F32), _sds((m, N_DEV * n), BF16)][:n_out]
    return _mm_call(
        name, a, wg, [], outs, (m // tm, N_DEV, kdim // tk),
        pl.BlockSpec((tm, tk), lambda i, j, k: (i, k)),
        pl.BlockSpec((None, None, tk, n), lambda i, j, k: (j, layer, k, 0)),
        [], [pl.BlockSpec((tm, n), lambda i, j, k: (i, j))] * len(outs),
        (tm, n), NN, epilogue)


def mm_rows_sharded(a, wg, layer, name, extras, extra_kinds, epilogue, tm=1024, tn=1024, tk=None):
    m, kdim = a.shape
    ks, n = wg.shape[-2], wg.shape[-1]
    tk = ks if tk is None else tk
    r = ks // tk
    tn = min(tn, n)
    specs = []
    for kind in extra_kinds:
        if kind == "row":
            specs.append(pl.BlockSpec((1, tn), lambda i, j, k: (0, j)))
        else:
            specs.append(pl.BlockSpec((tm, tn), lambda i, j, k: (i, j)))
    return _mm_call(
        name, a, wg, extras, [_sds((m, n), F32)], (m // tm, n // tn, kdim // tk),
        pl.BlockSpec((tm, tk), lambda i, j, k: (i, k)),
        pl.BlockSpec((None, None, tk, tn), lambda i, j, k: (k // r, layer, k % r, j)),
        specs, [pl.BlockSpec((tm, tn), lambda i, j, k: (i, j))],
        (tm, tn), NN, epilogue)[0]


def mm_plain(a, w, name, extras, extra_kinds, epilogue, tm=1024, tn=512, tk=512):
    m, kdim = a.shape
    n = w.shape[1]
    specs = []
    for kind in extra_kinds:
        if kind == "row":
            specs.append(pl.BlockSpec((1, tn), lambda i, j, k: (0, j)))
        else:
            specs.append(pl.BlockSpec((tm, tn), lambda i, j, k: (i, j)))
    return _mm_call(
        name, a, w, extras, [_sds((m, n), F32)], (m // tm, n // tn, kdim // tk),
        pl.BlockSpec((tm, tk), lambda i, j, k: (i, k)),
        pl.BlockSpec((tk, tn), lambda i, j, k: (k, j)),
        specs, [pl.BlockSpec((tm, tn), lambda i, j, k: (i, j))],
        (tm, tn), NN, epilogue)[0]


def mm_nt_cols_sharded(dy, wg, layer, name, tm=1024, tn=1024):
    m = dy.shape[0]
    kdim, n = wg.shape[-2], wg.shape[-1]
    return _mm_call(
        name, dy, wg, [], [_sds((m, kdim), F32)], (m // tm, kdim // tn, N_DEV),
        pl.BlockSpec((tm, n), lambda i, j, k: (i, k)),
        pl.BlockSpec((None, None, tn, n), lambda i, j, k: (k, layer, j, 0)),
        [], [pl.BlockSpec((tm, tn), lambda i, j, k: (i, j))],
        (tm, tn), NT, _ep_store(F32))[0]


def mm_nt_rows_sharded(dy, wg, layer, name, extras=(), epilogue=None, out_dtype=F32,
                       tm=1024, tk=512):
    m, n = dy.shape
    ks = wg.shape[-2]
    epilogue = _ep_store(out_dtype) if epilogue is None else epilogue
    return _mm_call(
        name, dy, wg, list(extras), [_sds((m, N_DEV * ks), out_dtype)], (m // tm, N_DEV, n // tk),
        pl.BlockSpec((tm, tk), lambda i, j, k: (i, k)),
        pl.BlockSpec((None, None, ks, tk), lambda i, j, k: (j, layer, 0, k)),
        [pl.BlockSpec((tm, ks), lambda i, j, k: (i, j))] * len(extras),
        [pl.BlockSpec((tm, ks), lambda i, j, k: (i, j))],
        (tm, ks), NT, epilogue)[0]


def mm_nt_plain(dy, w, name, tm=1024, tn=1024, tk=512):
    m, n = dy.shape
    kdim = w.shape[0]
    return _mm_call(
        name, dy, w, [], [_sds((m, kdim), F32)], (m // tm, kdim // tn, n // tk),
        pl.BlockSpec((tm, tk), lambda i, j, k: (i, k)),
        pl.BlockSpec((tn, tk), lambda i, j, k: (j, k)),
        [], [pl.BlockSpec((tm, tn), lambda i, j, k: (i, j))],
        (tm, tn), NT, _ep_store(F32))[0]


def mm_tn(a, dy, name, shard_cols=None, tm=1024, tn=1024, tk=512):
    t, kdim = a.shape
    n = dy.shape[1]
    tm = min(tm, kdim)
    if shard_cols is None:
        tn = min(tn, n)
        out = _sds((kdim, n), BF16)
        o_spec = pl.BlockSpec((tm, tn), lambda i, j, k: (i, j))
    else:
        tn = shard_cols
        out = _sds((n // tn, kdim, tn), BF16)
        o_spec = pl.BlockSpec((None, tm, tn), lambda i, j, k: (j, i, 0))
    return _mm_call(
        name, a, dy, [], [out], (kdim // tm, n // tn, t // tk),
        pl.BlockSpec((tk, tm), lambda i, j, k: (k, i)),
        pl.BlockSpec((tk, tn), lambda i, j, k: (k, j)),
        [], [o_spec], (tm, tn), TN, _ep_store(BF16))[0]


def _band_mask(g, nk_prev_valid, max_dist):
    rows = lax.broadcasted_iota(jnp.int32, (g * BLK, 2 * BLK), 0) % BLK
    cols = lax.broadcasted_iota(jnp.int32, (g * BLK, 2 * BLK), 1)
    dist = rows + BLK - cols
    ok = (dist >= 0) & (dist <= max_dist)
    return ok & ((cols >= BLK) | nk_prev_valid)


def band_fwd(qkv, q0, k0, v0, hk, g, seg, max_dist, name, sink_rows=None):
    t, dh = qkv.shape[1], qkv.shape[2]
    nb = t // BLK
    scale = dh ** -0.5
    has_sink = sink_rows is not None

    def kern(*refs):
        if has_sink:
            q_ref, k_ref, v_ref, s_ref, num_ref, m_ref, l_ref = refs
            sink = s_ref[...]
        else:
            q_ref, k_ref, v_ref, num_ref, m_ref, l_ref = refs
        b = pl.program_id(1)
        cur = pl.multiple_of(b * BLK, BLK)
        prev = pl.multiple_of(jnp.maximum(b - 1, 0) * BLK, BLK)
        q = q_ref[...].reshape(g * BLK, dh)
        kk = jnp.concatenate([k_ref[pl.ds(prev, BLK), :], k_ref[pl.ds(cur, BLK), :]], axis=0)
        vv = jnp.concatenate([v_ref[pl.ds(prev, BLK), :], v_ref[pl.ds(cur, BLK), :]], axis=0)
        s = lax.dot_general(q, kk, NT, preferred_element_type=F32) * scale
        s = jnp.where(_band_mask(g, (b % seg) != 0, max_dist), s, -jnp.inf)
        m = jnp.max(s, axis=-1, keepdims=True)
        if has_sink:
            m = jnp.maximum(m, sink)
        p = jnp.exp(s - m)
        l = jnp.sum(p, axis=-1, keepdims=True)
        if has_sink:
            l = l + jnp.exp(sink - m)
        num = jnp.dot(p.astype(BF16), vv, preferred_element_type=F32)
        num_ref[...] = num.reshape(g, BLK, dh)
        m_ref[...] = m.reshape(g, BLK, 1)
        l_ref[...] = l.reshape(g, BLK, 1)

    in_specs = [pl.BlockSpec((g, BLK, dh), lambda h, b: (q0 // g + h, b, 0)),
                pl.BlockSpec((None, t, dh), lambda h, b: (k0 + h, 0, 0)),
                pl.BlockSpec((None, t, dh), lambda h, b: (v0 + h, 0, 0))]
    args = [qkv, qkv, qkv]
    if has_sink:
        in_specs.append(pl.BlockSpec((None, g * BLK, 1), lambda h, b: (h, 0, 0)))
        args.append(sink_rows)
    hq = hk * g
    return pl.pallas_call(
        kern, name=name, grid=(hk, nb), in_specs=in_specs,
        out_specs=[pl.BlockSpec((g, BLK, dh), lambda h, b: (h, b, 0)),
                   pl.BlockSpec((g, BLK, 1), lambda h, b: (h, b, 0)),
                   pl.BlockSpec((g, BLK, 1), lambda h, b: (h, b, 0))],
        out_shape=[_sds((hq, t, dh), F32), _sds((hq, t, 1), F32), _sds((hq, t, 1), F32)],
        compiler_params=_params("parallel", "parallel"),
    )(*args)


def band_bwd(qkv, q0, k0, v0, do, lse, delta, hk, g, seg, max_dist, name, sink_rows=None):
    t, dh = qkv.shape[1], qkv.shape[2]
    nb = t // BLK
    scale = dh ** -0.5
    has_sink = sink_rows is not None

    def kern(*refs):
        if has_sink:
            (q_ref, k_ref, v_ref, do_ref, lse_ref, dl_ref, s_ref,
             dq_ref, dk_ref, dv_ref, ds_ref, sacc) = refs
            sink = s_ref[...]
        else:
            q_ref, k_ref, v_ref, do_ref, lse_ref, dl_ref, dq_ref, dk_ref, dv_ref = refs
        b = pl.program_id(1)

        @pl.when(b == 0)
        def _():
            dk_ref[...] = jnp.zeros_like(dk_ref)
            dv_ref[...] = jnp.zeros_like(dv_ref)
            if has_sink:
                sacc[...] = jnp.zeros_like(sacc)

        cur = pl.multiple_of(b * BLK, BLK)
        prev = pl.multiple_of(jnp.maximum(b - 1, 0) * BLK, BLK)
        q = q_ref[...].reshape(g * BLK, dh)
        dout = do_ref[...].reshape(g * BLK, dh)
        lse_b = lse_ref[...].reshape(g * BLK, 1)
        dl_b = dl_ref[...].reshape(g * BLK, 1)
        kk = jnp.concatenate([k_ref[pl.ds(prev, BLK), :], k_ref[pl.ds(cur, BLK), :]], axis=0)
        vv = jnp.concatenate([v_ref[pl.ds(prev, BLK), :], v_ref[pl.ds(cur, BLK), :]], axis=0)
        s = lax.dot_general(q, kk, NT, preferred_element_type=F32) * scale
        s = jnp.where(_band_mask(g, (b % seg) != 0, max_dist), s, -jnp.inf)
        p = jnp.exp(s - lse_b)
        dp = lax.dot_general(dout, vv, NT, preferred_element_type=F32)
        ds = (p * (dp - dl_b) * scale).astype(BF16)
        dq = jnp.dot(ds, kk, preferred_element_type=F32)
        dq_ref[...] = dq.reshape(g, BLK, dh)
        dkk = lax.dot_general(ds, q, TN, preferred_element_type=F32)
        dvv = lax.dot_general(p.astype(BF16), dout, TN, preferred_element_type=F32)
        dk_ref[pl.ds(prev, BLK), :] += dkk[:BLK]
        dk_ref[pl.ds(cur, BLK), :] += dkk[BLK:]
        dv_ref[pl.ds(prev, BLK), :] += dvv[:BLK]
        dv_ref[pl.ds(cur, BLK), :] += dvv[BLK:]
        if has_sink:
            sacc[...] += -jnp.exp(sink - lse_b) * dl_b

            @pl.when(b == nb - 1)
            def _():
                for gi in range(g):
                    ds_ref[gi:gi + 1, :] = jnp.sum(sacc[gi * BLK:(gi + 1) * BLK, :], axis=0,
                                                   keepdims=True)

    in_specs = [pl.BlockSpec((g, BLK, dh), lambda h, b: (q0 // g + h, b, 0)),
                pl.BlockSpec((None, t, dh), lambda h, b: (k0 + h, 0, 0)),
                pl.BlockSpec((None, t, dh), lambda h, b: (v0 + h, 0, 0)),
                pl.BlockSpec((g, BLK, dh), lambda h, b: (h, b, 0)),
                pl.BlockSpec((g, BLK, 1), lambda h, b: (h, b, 0)),
                pl.BlockSpec((g, BLK, 1), lambda h, b: (h, b, 0))]
    args = [qkv, qkv, qkv, do, lse, delta]
    hq = hk * g
    out_specs = [pl.BlockSpec((g, BLK, dh), lambda h, b: (h, b, 0)),
                 pl.BlockSpec((None, t, dh), lambda h, b: (h, 0, 0)),
                 pl.BlockSpec((None, t, dh), lambda h, b: (h, 0, 0))]
    out_shape = [_sds((hq, t, dh), F32), _sds((hk, t, dh), F32), _sds((hk, t, dh), F32)]
    scratch = []
    if has_sink:
        in_specs.append(pl.BlockSpec((None, g * BLK, 1), lambda h, b: (h, 0, 0)))
        args.append(sink_rows)
        out_specs.append(pl.BlockSpec((None, g, 1), lambda h, b: (h, 0, 0)))
        out_shape.append(_sds((hk, g, 1), F32))
        scratch.append(pltpu.VMEM((g * BLK, 1), F32))
    return pl.pallas_call(
        kern, name=name, grid=(hk, nb), in_specs=in_specs, out_specs=out_specs, out_shape=out_shape,
        scratch_shapes=scratch, compiler_params=_params("parallel", "arbitrary"),
    )(*args)


def merge_branches(nums, ms, ls, name):
    h, t, dh = nums[0].shape
    nbr = len(nums)

    def kern(*refs):
        num_refs, m_refs, l_refs = refs[:nbr], refs[nbr:2 * nbr], refs[2 * nbr:3 * nbr]
        o_ref, lse_ref = refs[3 * nbr], refs[3 * nbr + 1]
        mall = m_refs[0][...]
        for i in range(1, nbr):
            mall = jnp.maximum(mall, m_refs[i][...])
        num = jnp.zeros((t, dh), F32)
        den = jnp.zeros((t, 1), F32)
        for i in range(nbr):
            w = jnp.exp(m_refs[i][...] - mall)
            num = num + w * num_refs[i][...]
            den = den + w * l_refs[i][...]
        o_ref[...] = num / den
        lse_ref[...] = mall + jnp.log(den)

    big = pl.BlockSpec((None, t, dh), lambda i: (i, 0, 0))
    col = pl.BlockSpec((None, t, 1), lambda i: (i, 0, 0))
    return pl.pallas_call(
        kern, name=name, grid=(h,), in_specs=[big] * nbr + [col] * (2 * nbr),
        out_specs=[big, col], out_shape=[_sds((h, t, dh), F32), _sds((h, t, 1), F32)],
        compiler_params=_params("parallel"),
    )(*nums, *ms, *ls)


def normalise_heads(num, m, l, name):
    h, t, dh = num.shape

    def kern(num_ref, m_ref, l_ref, o_ref, lse_ref):
        lv = l_ref[...]
        o_ref[...] = num_ref[...] / lv
        lse_ref[...] = m_ref[...] + jnp.log(lv)

    big = pl.BlockSpec((None, t, dh), lambda i: (i, 0, 0))
    col = pl.BlockSpec((None, t, 1), lambda i: (i, 0, 0))
    return pl.pallas_call(
        kern, name=name, grid=(h,), in_specs=[big, col, col], out_specs=[big, col],
        out_shape=[_sds((h, t, dh), F32), _sds((h, t, 1), F32)],
        compiler_params=_params("parallel"),
    )(num, m, l)


def head_delta(o, do, name):
    h, t, dh = o.shape

    def kern(o_ref, do_ref, d_ref):
        d_ref[...] = jnp.sum(o_ref[...] * do_ref[...], axis=-1, keepdims=True)

    big = pl.BlockSpec((None, t, dh), lambda i: (i, 0, 0))
    return pl.pallas_call(
        kern, name=name, grid=(h,), in_specs=[big, big],
        out_specs=pl.BlockSpec((None, t, 1), lambda i: (i, 0, 0)),
        out_shape=_sds((h, t, 1), F32), compiler_params=_params("parallel"),
    )(o, do)


def _cumsum_rows(x, n, reverse=False):
    rows = lax.broadcasted_iota(jnp.int32, x.shape, 0)
    shift = 1
    while shift < n:
        if reverse:
            x = x + jnp.where(rows < n - shift, pltpu.roll(x, n - shift, 0), 0.0)
        else:
            x = x + jnp.where(rows >= shift, pltpu.roll(x, shift, 0), 0.0)
        shift *= 2
    return x


def _hgrn_gates(f, lb):
    sig = _sigmoid(f)
    gate = lb + (1.0 - lb) * sig
    return sig, gate


def hgrn_fwd(proj, col0, nh, lb, gn, name):
    t = proj.shape[0]
    c = B_CHUNK
    nc = t // c
    scale = B_DIM ** -0.5

    def kern(q_ref, f_ref, i_ref, g_ref, lb_ref, gn_ref, out_ref, opre_ref, st_ref,
             state, bsc, qsc, osc):
        lbv = lb_ref[...]
        gnv = gn_ref[...]
        state[...] = jnp.zeros_like(state)
        srow = lax.broadcasted_iota(jnp.int32, (c, B_DIM), 0)

        def chunk(ci, carry):
            rows = pl.ds(pl.multiple_of(ci * c, c), c)
            _, gate = _hgrn_gates(f_ref[rows, :], lbv)
            kk = 1.0 - gate
            qb = q_ref[rows, :]
            qq = qb * _sigmoid(qb) * scale
            v = i_ref[rows, :]
            b = _cumsum_rows(jnp.log(gate), c)
            st = state[...]
            st_ref[ci] = st
            o_inter = lax.dot_general((qq * jnp.exp(b)).astype(BF16), st.astype(BF16), NT,
                                      preferred_element_type=F32)
            bsc[...] = b
            qsc[...] = qq

            def tstep(ti, carry2):
                bt = bsc[pl.ds(ti, 1), :]
                qt = qsc[pl.ds(ti, 1), :]
                dec = jnp.exp(jnp.where(srow <= ti, bt - b, -jnp.inf))
                a = jnp.sum(qt * kk * dec, axis=1, keepdims=True)
                osc[pl.ds(ti, 1), :] = jnp.sum(a * v, axis=0, keepdims=True)
                return carry2

            lax.fori_loop(0, c, tstep, 0)
            o = osc[...] + o_inter
            opre_ref[rows, :] = o
            bl = b[c - 1:c, :]
            state[...] = st * jnp.exp(bl) + lax.dot_general(
                v.astype(BF16), (kk * jnp.exp(bl - b)).astype(BF16), TN, preferred_element_type=F32)
            r = lax.rsqrt(jnp.mean(o * o, axis=-1, keepdims=True) + NORM_EPS)
            gb = g_ref[rows, :]
            out_ref[rows, :] = (o * r * gnv * (gb * _sigmoid(gb))).astype(BF16)
            return carry

        lax.fori_loop(0, nc, chunk, 0)

    def col(off):
        return pl.BlockSpec((t, B_DIM), lambda h: (0, col0 + off * nh + h))

    return pl.pallas_call(
        kern, name=name, grid=(nh,),
        in_specs=[col(0), col(1), col(2), col(3),
                  pl.BlockSpec((None, 1, B_DIM), lambda h: (h, 0, 0)),
                  pl.BlockSpec((1, B_DIM), lambda h: (0, 0))],
        out_specs=[pl.BlockSpec((t, B_DIM), lambda h: (0, h)),
                   pl.BlockSpec((t, B_DIM), lambda h: (0, h)),
                   pl.BlockSpec((None, nc, B_DIM, B_DIM), lambda h: (h, 0, 0, 0))],
        out_shape=[_sds((t, nh * B_DIM), BF16), _sds((t, nh * B_DIM), F32),
                   _sds((nh, nc, B_DIM, B_DIM), F32)],
        scratch_shapes=[pltpu.VMEM((B_DIM, B_DIM), F32), pltpu.VMEM((c, B_DIM), F32),
                        pltpu.VMEM((c, B_DIM), F32), pltpu.VMEM((c, B_DIM), F32)],
        compiler_params=_params("parallel"),
    )(proj, proj, proj, proj, lb, gn)


def hgrn_bwd(proj, col0, nh, lb, gn, opre, states, dout, dcol0, name):
    t = proj.shape[0]
    c = B_CHUNK
    nc = t // c
    scale = B_DIM ** -0.5

    def kern(q_ref, f_ref, i_ref, g_ref, lb_ref, gn_ref, opre_ref, st_ref, dout_ref,
             dq_ref, df_ref, di_ref, dg_ref, dgn_ref, dlb_ref,
             dstate, bsc, qsc, dosc, dqsc):
        lbv = lb_ref[...]
        gnv = gn_ref[...]
        dstate[...] = jnp.zeros_like(dstate)
        dlb_ref[...] = jnp.zeros_like(dlb_ref)

        @pl.when(pl.program_id(0) == 0)
        def _():
            dgn_ref[...] = jnp.zeros_like(dgn_ref)

        srow = lax.broadcasted_iota(jnp.int32, (c, B_DIM), 0)

        def chunk(cj, carry):
            ci = nc - 1 - cj
            rows = pl.ds(pl.multiple_of(ci * c, c), c)
            f = f_ref[rows, :]
            sig, gate = _hgrn_gates(f, lbv)
            kk = 1.0 - gate
            qb = q_ref[rows, :]
            sq = _sigmoid(qb)
            qq = qb * sq * scale
            v = i_ref[rows, :]
            b = _cumsum_rows(jnp.log(gate), c)
            st0 = st_ref[ci]
            dst = dstate[...]
            o = opre_ref[rows, :]
            gb = g_ref[rows, :]
            sg = _sigmoid(gb)
            silu_g = gb * sg
            d_out = dout_ref[rows, :]
            r = lax.rsqrt(jnp.mean(o * o, axis=-1, keepdims=True) + NORM_EPS)
            y = o * r
            dg_ref[rows, :] = (d_out * y * gnv * (sg * (1.0 + gb * (1.0 - sg)))).astype(BF16)
            dyn = d_out * silu_g
            dgn_ref[...] += jnp.sum(dyn * y, axis=0, keepdims=True)
            dy = dyn * gnv
            do = r * (dy - y * jnp.mean(dy * y, axis=-1, keepdims=True))
            eb = jnp.exp(b)
            bl = b[c - 1:c, :]
            ebl = jnp.exp(bl - b)
            ebl_last = jnp.exp(bl)
            do_b = do.astype(BF16)
            dst_b = dst.astype(BF16)
            dq_inter = jnp.dot(do_b, st0.astype(BF16), preferred_element_type=F32) * eb
            dst0 = lax.dot_general(do_b, (qq * eb).astype(BF16), TN,
                                   preferred_element_type=F32) + dst * ebl_last
            dv_inter = lax.dot_general((kk * ebl).astype(BF16), dst_b, NT, preferred_element_type=F32)
            dk_inter = jnp.dot(v.astype(BF16), dst_b, preferred_element_type=F32) * ebl
            bsc[...] = b
            qsc[...] = qq
            dosc[...] = do

            def tstep(ti, carry2):
                dk_acc, dv_acc = carry2
                bt = bsc[pl.ds(ti, 1), :]
                qt = qsc[pl.ds(ti, 1), :]
                dot_ = dosc[pl.ds(ti, 1), :]
                dec = jnp.exp(jnp.where(srow <= ti, bt - b, -jnp.inf))
                da = jnp.sum(dot_ * v, axis=1, keepdims=True)
                w = da * dec
                dqsc[pl.ds(ti, 1), :] = jnp.sum(w * kk, axis=0, keepdims=True)
                a = jnp.sum(qt * kk * dec, axis=1, keepdims=True)
                return dk_acc + w * qt, dv_acc + a * dot_

            zero = jnp.zeros((c, B_DIM), F32)
            dk_intra, dv_intra = lax.fori_loop(0, c, tstep, (zero, zero))
            dq = dqsc[...] + dq_inter
            dk = dk_intra + dk_inter
            dv = dv_intra + dv_inter
            db = qq * dq - kk * dk
            extra = (jnp.sum(kk * dk_inter, axis=0, keepdims=True)
                     + ebl_last * jnp.sum(st0 * dst, axis=0, keepdims=True))
            db = db + jnp.where(srow == c - 1, extra, 0.0)
            dlog = _cumsum_rows(db, c, reverse=True)
            dgate = dlog / gate - dk
            df_ref[rows, :] = (dgate * (1.0 - lbv) * sig * (1.0 - sig)).astype(BF16)
            dlb_ref[...] += jnp.sum(dgate * (1.0 - sig), axis=0, keepdims=True)
            dq_ref[rows, :] = (dq * scale * (sq * (1.0 + qb * (1.0 - sq)))).astype(BF16)
            di_ref[rows, :] = dv.astype(BF16)
            dstate[...] = dst0
            return carry

        lax.fori_loop(0, nc, chunk, 0)

    def col(off):
        return pl.BlockSpec((t, B_DIM), lambda h: (0, col0 + off * nh + h))

    hcol = pl.BlockSpec((t, B_DIM), lambda h: (0, h))
    vec = pl.BlockSpec((None, 1, B_DIM), lambda h: (h, 0, 0))
    wide = _sds((t, nh * B_DIM), BF16)
    return pl.pallas_call(
        kern, name=name, grid=(nh,),
        in_specs=[col(0), col(1), col(2), col(3), vec,
                  pl.BlockSpec((1, B_DIM), lambda h: (0, 0)), hcol,
                  pl.BlockSpec((None, nc, B_DIM, B_DIM), lambda h: (h, 0, 0, 0)),
                  pl.BlockSpec((t, B_DIM), lambda h: (0, dcol0 + h))],
        out_specs=[hcol, hcol, hcol, hcol, pl.BlockSpec((1, B_DIM), lambda h: (0, 0)), vec],
        out_shape=[wide, wide, wide, wide, _sds((1, B_DIM), F32), _sds((nh, 1, B_DIM), F32)],
        scratch_shapes=[pltpu.VMEM((B_DIM, B_DIM), F32)] + [pltpu.VMEM((c, B_DIM), F32)] * 4,
        compiler_params=_params("arbitrary"),
    )(proj, proj, proj, proj, lb, gn, opre, states, dout)


def lower_bounds_fwd(raw, name):
    n, w = raw.shape

    def kern(raw_ref, lb_ref, soft_ref):
        r = raw_ref[...]
        mx = r[0:1]
        for i in range(1, n):
            mx = jnp.maximum(mx, r[i:i + 1])
        e = jnp.exp(r - mx)
        den = e[0:1]
        for i in range(1, n):
            den = den + e[i:i + 1]
        soft = e / den
        soft_ref[...] = soft
        run = soft[0:1]
        lb_ref[0:1, :] = run - soft[0:1]
        for i in range(1, n):
            run = run + soft[i:i + 1]
            lb_ref[i:i + 1, :] = run - soft[0:1]

    return pl.pallas_call(kern, name=name, out_shape=[_sds((n, w), F32), _sds((n, w), F32)])(raw)


def lower_bounds_bwd(soft, dlb, name):
    n, w = soft.shape

    def kern(soft_ref, dlb_ref, out_ref):
        s = soft_ref[...]
        d = dlb_ref[...]
        total = d[0:1]
        for i in range(1, n):
            total = total + d[i:i + 1]
        us = []
        tail = total
        for i in range(n):
            us.append(tail - total if i == 0 else tail)
            tail = tail - d[i:i + 1]
        dot = s[0:1] * us[0]
        for i in range(1, n):
            dot = dot + s[i:i + 1] * us[i]
        for i in range(n):
            out_ref[i:i + 1, :] = s[i:i + 1] * (us[i] - dot)

    return pl.pallas_call(kern, name=name, out_shape=_sds((n, w), F32))(soft, dlb)


def _row_tile(kdim, n):
    tk = 512
    while tk > 8 and tk * n > 256 * 1024:
        tk //= 2
    return min(kdim, tk)


def _adam_update(w, g, m, v):
    m2 = ADAM_B1 * m + (1.0 - ADAM_B1) * g
    v2 = ADAM_B2 * v + (1.0 - ADAM_B2) * (g * g)
    m_hat = m2 / (1.0 - ADAM_B1 ** ADAM_STEP)
    v_hat = v2 / (1.0 - ADAM_B2 ** ADAM_STEP)
    delta = -ADAM_LR * (m_hat / (jnp.sqrt(v_hat) + ADAM_EPS) + ADAM_WD * w)
    return delta, m2, v2


def adamw_small(w, g, m, v, name):
    def kern(w_ref, g_ref, m_ref, v_ref, d_ref, m2_ref, v2_ref):
        d, m2, v2 = _adam_update(w_ref[...], g_ref[...], m_ref[...], v_ref[...])
        d_ref[...] = d
        m2_ref[...] = m2
        v2_ref[...] = v2

    return pl.pallas_call(kern, name=name, out_shape=[_sds(w.shape, F32)] * 3)(w, g, m, v)


def adamw_big(parts, w, m, v, name):
    nl, kdim, n = w.shape
    tk = _row_tile(kdim, n)

    def kern(p_ref, w_ref, m_ref, v_ref, g_ref, d_ref, m2_ref, v2_ref):
        g = p_ref[0].astype(F32)
        for q in range(1, 4):
            g = g + p_ref[q].astype(F32)
        d, m2, v2 = _adam_update(w_ref[...], g, m_ref[...], v_ref[...])
        g_ref[...] = g
        d_ref[...] = d
        m2_ref[...] = m2
        v2_ref[...] = v2

    blk = pl.BlockSpec((None, tk, n), lambda l, i: (l, i, 0))
    return pl.pallas_call(
        kern, name=name, grid=(nl, kdim // tk),
        in_specs=[pl.BlockSpec((None, 4, tk, n), lambda l, i: (l, 0, i, 0)), blk, blk, blk],
        out_specs=[blk] * 4, out_shape=[_sds(w.shape, F32)] * 4,
        compiler_params=_params("parallel", "parallel"),
    )(parts, w, m, v)


def cast_bf16(w, name):
    nl, kdim, n = w.shape
    tk = _row_tile(kdim, n)

    def kern(w_ref, o_ref):
        o_ref[...] = w_ref[...].astype(BF16)

    blk = pl.BlockSpec((None, tk, n), lambda l, i: (l, i, 0))
    return pl.pallas_call(
        kern, name=name, grid=(nl, kdim // tk), in_specs=[blk], out_specs=blk,
        out_shape=_sds(w.shape, BF16), compiler_params=_params("parallel", "parallel"),
    )(w)


def pair_add(dw, r1, core, name):
    kdim, n = dw.shape[1], dw.shape[2]
    tk = _row_tile(kdim, n)

    def kern(c_ref, a_ref, b_ref, o_ref):
        o_ref[...] = (a_ref[...].astype(F32) + b_ref[...].astype(F32)).astype(BF16)

    grid_spec = pltpu.PrefetchScalarGridSpec(
        num_scalar_prefetch=1, grid=(4, kdim // tk),
        in_specs=[pl.BlockSpec((None, tk, n), lambda p, i, c: (2 * p + c[0], i, 0)),
                  pl.BlockSpec((None, tk, n), lambda p, i, c: (p, i, 0))],
        out_specs=pl.BlockSpec((None, tk, n), lambda p, i, c: (p, i, 0)))
    return pl.pallas_call(
        kern, name=name, grid_spec=grid_spec, out_shape=_sds((4, kdim, n), BF16),
        compiler_params=_params("parallel", "parallel"),
    )(core, dw, r1)


ANY = pl.BlockSpec(memory_space=pl.ANY)


def _place():
    x, y, c = lax.axis_index("x"), lax.axis_index("y"), lax.axis_index("c")
    chips = [(1 - x, y), (x, 1 - y), (1 - x, 1 - y)]
    return x, y, c, chips


def all_gather(shards, name):
    n = len(shards)

    def kern(*refs):
        ins, outs = refs[:n], refs[n:2 * n]
        send_sems, recv_sems, local_sems = refs[2 * n:]
        x, y, c, chips = _place()
        me, sib = (x, y, c), (x, y, 1 - c)

        def copy(t, k, block, to, src=None):
            px, py, pc = block
            dst = outs[t].at[4 * px + 2 * py + pc]
            return pltpu.make_async_remote_copy(
                src_ref=dst if src is None else src, dst_ref=dst,
                send_sem=send_sems.at[7 * t + k], recv_sem=recv_sems.at[7 * t + k],
                device_id=to, device_id_type=MESH)

        mine = [pltpu.make_async_copy(ins[t], outs[t].at[4 * x + 2 * y + c], local_sems.at[t])
                for t in range(n)]
        for cp in mine:
            cp.start()
        first = []
        for t in range(n):
            first.append(copy(t, 0, me, sib, src=ins[t]))
            first += [copy(t, 1 + j, me, (*chip, c), src=ins[t]) for j, chip in enumerate(chips)]
        for cp in first:
            cp.start()
        passed = []
        for t in range(n):
            for j, chip in enumerate(chips):
                copy(t, 1 + j, (*chip, c), me).wait_recv()
                fwd = copy(t, 4 + j, (*chip, c), sib)
                fwd.start()
                passed.append(fwd)
        for t in range(n):
            copy(t, 0, sib, me).wait_recv()
            for j, chip in enumerate(chips):
                copy(t, 4 + j, (*chip, 1 - c), me).wait_recv()
        for cp in first + passed:
            cp.wait_send()
        for cp in mine:
            cp.wait()

    return pl.pallas_call(
        kern, name=name, in_specs=[ANY] * n, out_specs=[ANY] * n,
        out_shape=[_sds((N_DEV,) + s.shape, s.dtype) for s in shards],
        scratch_shapes=[pltpu.SemaphoreType.DMA((7 * n,)), pltpu.SemaphoreType.DMA((7 * n,)),
                        pltpu.SemaphoreType.DMA((n,))],
    )(*shards)


def all_reduce_small(vec, name):
    r = vec.shape[0]

    def kern(v_ref, o_ref, buf, send_sems, recv_sems):
        x, y, c, _ = _place()
        me = 4 * x + 2 * y + c
        peers = [(x, y, 1 - c), (1 - x, y, c), (x, 1 - y, c), (1 - x, 1 - y, c),
                 (1 - x, y, 1 - c), (x, 1 - y, 1 - c), (1 - x, 1 - y, 1 - c)]
        buf[me] = v_ref[...]
        copies = []
        for k, peer in enumerate(peers):
            cp = pltpu.make_async_remote_copy(
                src_ref=v_ref, dst_ref=buf.at[me], send_sem=send_sems.at[k],
                recv_sem=recv_sems.at[k], device_id=peer, device_id_type=MESH)
            cp.start()
            copies.append(cp)
        for cp in copies:
            cp.wait_recv()
        for cp in copies:
            cp.wait_send()
        total = buf[0]
        for d in range(1, N_DEV):
            total = total + buf[d]
        o_ref[...] = total

    vm = pl.BlockSpec(memory_space=pltpu.VMEM)
    return pl.pallas_call(
        kern, name=name, in_specs=[vm], out_specs=vm, out_shape=_sds(vec.shape, F32),
        scratch_shapes=[pltpu.VMEM((N_DEV, r, LANES), F32), pltpu.SemaphoreType.DMA((7,)),
                        pltpu.SemaphoreType.DMA((7,))],
    )(vec)


def exchange_with_sibling(grads, name):
    n = len(grads)

    def kern(*refs):
        ins, outs = refs[:n], refs[n:2 * n]
        send_sems, recv_sems = refs[2 * n:]
        x, y, c, _ = _place()
        copies = []
        for t in range(n):
            for p in range(4):
                cp = pltpu.make_async_remote_copy(
                    src_ref=ins[t].at[2 * p + 1 - c], dst_ref=outs[t].at[p],
                    send_sem=send_sems.at[4 * t + p], recv_sem=recv_sems.at[4 * t + p],
                    device_id=(x, y, 1 - c), device_id_type=MESH)
                cp.start()
                copies.append(cp)
        for cp in copies:
            cp.wait_recv()
        for cp in copies:
            cp.wait_send()

    return pl.pallas_call(
        kern, name=name, in_specs=[ANY] * n, out_specs=[ANY] * n,
        out_shape=[_sds((4,) + g.shape[1:], g.dtype) for g in grads],
        scratch_shapes=[pltpu.SemaphoreType.DMA((4 * n,)), pltpu.SemaphoreType.DMA((4 * n,))],
    )(*grads)


def exchange_between_chips(partials, layers, kinds, name):
    n = len(partials)
    n_kind = max(kinds) + 1
    shapes = []
    for kd in range(n_kind):
        idx = [i for i in range(n) if kinds[i] == kd]
        nl = max(layers[i] for i in idx) + 1
        shapes.append(_sds((nl,) + partials[idx[0]].shape, partials[idx[0]].dtype))

    def kern(*refs):
        ins, outs = refs[:n], refs[n:n + n_kind]
        send_sems, recv_sems, local_sems = refs[n + n_kind:]
        x, y, c, chips = _place()
        mine = 2 * x + y
        local = []
        copies = []
        for t in range(n):
            dst = outs[kinds[t]].at[layers[t], mine]
            lc = pltpu.make_async_copy(ins[t].at[mine], dst, local_sems.at[t])
            lc.start()
            local.append(lc)
            for j, (qx, qy) in enumerate(chips):
                cp = pltpu.make_async_remote_copy(
                    src_ref=ins[t].at[2 * qx + qy], dst_ref=dst,
                    send_sem=send_sems.at[3 * t + j], recv_sem=recv_sems.at[3 * t + j],
                    device_id=(qx, qy, c), device_id_type=MESH)
                cp.start()
                copies.append(cp)
        for cp in copies:
            cp.wait_recv()
        for cp in copies:
            cp.wait_send()
        for lc in local:
            lc.wait()

    return pl.pallas_call(
        kern, name=name, in_specs=[ANY] * n, out_specs=[ANY] * n_kind, out_shape=shapes,
        scratch_shapes=[pltpu.SemaphoreType.DMA((3 * n,)), pltpu.SemaphoreType.DMA((3 * n,)),
                        pltpu.SemaphoreType.DMA((n,))],
    )(*partials)


def _pack(arrays):
    flat = jnp.concatenate([a.reshape(-1).astype(F32) for a in arrays])
    pad = (-flat.shape[0]) % (8 * LANES)
    return jnp.pad(flat, (0, pad)).reshape(-1, LANES)


def _unpack(packed, shapes):
    flat = packed.reshape(-1)
    out, off = [], 0
    for s in shapes:
        n = math.prod(s)
        out.append(flat[off:off + n].reshape(s))
        off += n
    return out


def _to_heads(x2d, dil, n_heads, dh):
    t = x2d.shape[0]
    return x2d.reshape(t // dil, dil, n_heads, dh).transpose(2, 1, 0, 3).reshape(n_heads, t, dh)


def _from_heads(xh, dil):
    h, t, w = xh.shape
    return xh.reshape(h, dil, t // dil, w).transpose(2, 1, 0, 3).reshape(t, h * w)


def _unperm(xh, dil):
    h, t, w = xh.shape
    return xh.reshape(h, dil, t // dil, w).transpose(0, 2, 1, 3).reshape(h, t, w)


def _perm(xh, dil):
    h, t, w = xh.shape
    return xh.reshape(h, t // dil, dil, w).transpose(0, 2, 1, 3).reshape(h, t, w)


def local_step(x, target, norm_mix_g, norm_mlp_g, final_norm_g, lbs, hgrn_norm_g, sinks,
               bq_full, bo_full, win_g, wout_g, wqkv_g, wo_g, w1_g, w2_g):
    t, d = x.shape
    depth = norm_mix_g.shape[0]
    na = d // 2 // A_DIM
    nbh = d // 2 // B_DIM
    nq = d // C_DIM
    nkv = nq // C_GROUP
    a_w = 3 * na * A_DIM
    c_w = (nq + 2 * nkv) * C_DIM
    tabs_a = rope_tables(t, A_DIM)
    tabs_c = rope_tables(t, C_DIM)
    saved = []
    for l in range(depth):
        s = {"x_in": x}
        h = rms_fwd(x, norm_mix_g[l], "norm_mix_fwd")
        s["h"] = h
        if l % 2 == 0:
            e = l // 2
            proj = mm_cols_sharded(h, win_g, e, "even_in_proj")[0]
            qkv_r = rope_call(proj, tabs_a, a_w, 2 * na, False, "rope_a")[0]
            nums, ms, ls, hms = [], [], [], []
            for window, dil in A_BRANCHES:
                hm = _to_heads(qkv_r, dil, 3 * na, A_DIM)
                num, m, lsum = band_fwd(hm, 0, na, 2 * na, na, 1, t // dil // BLK, window // dil,
                                        f"dilated_fwd_{dil}")
                hms.append(hm)
                nums.append(_unperm(num, dil))
                ms.append(_unperm(m, dil))
                ls.append(_unperm(lsum, dil))
            oa, lse = merge_branches(nums, ms, ls, "dilated_merge")
            lb_e = lbs[e].reshape(nbh, 1, B_DIM)
            gn_e = hgrn_norm_g[e].reshape(1, B_DIM)
            ob, opre, states = hgrn_fwd(proj, 3 * na, nbh, lb_e, gn_e, "hgrn_fwd")
            mixed = jnp.concatenate([_from_heads(oa, 1).astype(BF16), ob], axis=1)
            x = mm_rows_sharded(mixed, wout_g, e, "even_out_proj", [x], ["tile"], _ep_residual)
            s.update(proj=proj, hms=hms, oa=oa, lse=lse, opre=opre, states=states, mixed=mixed,
                     lb=lb_e, gn=gn_e)
        else:
            o = l // 2
            wq = wqkv_g[:, o].transpose(1, 0, 2).reshape(d, c_w)
            proj = mm_plain(h, wq, "odd_qkv_proj", [bq_full[o].reshape(1, c_w)], ["row"], _ep_bias)
            qkv_r = rope_call(proj, tabs_c, c_w, (nq + nkv) * C_DIM // LANES, False, "rope_c")[0]
            hm = _to_heads(qkv_r, 1, nq + 2 * nkv, C_DIM)
            sink_rows = jnp.repeat(sinks[o].reshape(nkv, C_GROUP), BLK, axis=1).reshape(
                nkv, C_GROUP * BLK, 1)
            num, m, lsum = band_fwd(hm, 0, nq, nq + nkv, nkv, C_GROUP, t // BLK, C_WINDOW - 1,
                                    "swa_fwd", sink_rows=sink_rows)
            o_hm, lse = normalise_heads(num, m, lsum, "swa_normalise")
            attn = _from_heads(o_hm, 1).astype(BF16)
            x = mm_rows_sharded(attn, wo_g, o, "odd_out_proj", [bo_full[o].reshape(1, d), x],
                                ["row", "tile"], _ep_bias_residual)
            s.update(wq=wq, hm=hm, sink_rows=sink_rows, o_hm=o_hm, lse=lse, attn=attn)
        s["x_mid"] = x
        h2 = rms_fwd(x, norm_mlp_g[l], "norm_mlp_fwd")
        u, act = mm_cols_sharded(h2, w1_g, l, "mlp_up", epilogue=_ep_relu2, n_out=2)
        x = mm_rows_sharded(act, w2_g, l, "mlp_down", [x], ["tile"], _ep_residual, tk=512)
        s.update(h2=h2, u=u, act=act)
        saved.append(s)

    dx, dxb, dg_final, loss_part = loss_head(x, final_norm_g, target, "loss_head")
    big = []
    small = {"final": dg_final, "loss": loss_part, "mix": [None] * depth, "mlp": [None] * depth,
             "lb": {}, "gn": {}, "sinks": {}, "bq": {}, "bo": {}}
    for l in reversed(range(depth)):
        s = saved[l]
        big.append(("w2", l, mm_tn(s["act"], dxb, "mlp_down_dw").reshape(N_DEV, -1, d)))
        du = mm_nt_rows_sharded(dxb, w2_g, l, "mlp_down_dx", extras=[s["u"]],
                                epilogue=_ep_relu2_bwd, out_dtype=BF16)
        big.append(("w1", l, mm_tn(s["h2"], du, "mlp_up_dw", shard_cols=w1_g.shape[-1])))
        dh2 = mm_nt_cols_sharded(du, w1_g, l, "mlp_up_dx")
        dx, dxb, dg, col_dx = rms_bwd(s["x_mid"], norm_mlp_g[l], dh2, dx, "norm_mlp_bwd")
        small["mlp"][l] = dg
        if l % 2 == 0:
            e = l // 2
            big.append(("wout", e, mm_tn(s["mixed"], dxb, "even_out_dw").reshape(N_DEV, -1, d)))
            dmixed = mm_nt_rows_sharded(dxb, wout_g, e, "even_out_dx")
            do_hm = _to_heads(dmixed[:, :na * A_DIM], 1, na, A_DIM)
            delta = head_delta(s["oa"], do_hm, "dilated_delta")
            dsum = None
            for (window, dil), hm in zip(A_BRANCHES, s["hms"]):
                dq, dk, dv = band_bwd(hm, 0, na, 2 * na, _perm(do_hm, dil).astype(BF16),
                                      _perm(s["lse"], dil), _perm(delta, dil), na, 1,
                                      t // dil // BLK, window // dil, f"dilated_bwd_{dil}")
                part = _from_heads(jnp.concatenate([dq, dk, dv], axis=0), dil)
                dsum = part if dsum is None else dsum + part
            dqkv_a = rope_call(dsum, tabs_a, a_w, 2 * na, True, "rope_a_bwd")[0]
            dqb, dfb, dib, dgb, dgn, dlb = hgrn_bwd(s["proj"], 3 * na, nbh, s["lb"], s["gn"],
                                                    s["opre"], s["states"], dmixed, na, "hgrn_bwd")
            small["gn"][e] = dgn
            small["lb"][e] = dlb
            dproj = jnp.concatenate([dqkv_a, dqb, dfb, dib, dgb], axis=1)
            big.append(("win", e, mm_tn(s["h"], dproj, "even_in_dw", shard_cols=win_g.shape[-1])))
            dh = mm_nt_cols_sharded(dproj, win_g, e, "even_in_dx")
        else:
            o = l // 2
            small["bo"][o] = col_dx
            big.append(("wo", o, mm_tn(s["attn"], dxb, "odd_out_dw").reshape(N_DEV, -1, d)))
            dattn = mm_nt_rows_sharded(dxb, wo_g, o, "odd_out_dx")
            do_hm = _to_heads(dattn, 1, nq, C_DIM)
            delta = head_delta(s["o_hm"], do_hm, "swa_delta")
            dq, dk, dv, dsink = band_bwd(s["hm"], 0, nq, nq + nkv, do_hm.astype(BF16), s["lse"],
                                         delta, nkv, C_GROUP, t // BLK, C_WINDOW - 1, "swa_bwd",
                                         sink_rows=s["sink_rows"])
            small["sinks"][o] = dsink
            dqkv = _from_heads(jnp.concatenate([dq, dk, dv], axis=0), 1)
            dproj, dbq = rope_call(dqkv, tabs_c, c_w, (nq + nkv) * C_DIM // LANES, True,
                                   "rope_c_bwd", col_sum=True)
            small["bq"][o] = dbq
            dwq = mm_tn(s["h"], dproj, "odd_qkv_dw", tn=512)
            big.append(("wqkv", o, dwq.reshape(d, N_DEV, -1).transpose(1, 0, 2)))
            dh = mm_nt_plain(dproj, s["wq"], "odd_qkv_dx")
        dx, dxb, dg, _ = rms_bwd(s["x_in"], norm_mix_g[l], dh, dx, "norm_mix_bwd")
        small["mix"][l] = dg
    return dx, big, small


def kernel(x, norm_mix_g, norm_mlp_g, final_norm_g, even_w_in, even_w_out, hgrn_lb_raw, hgrn_norm_g, odd_w_qkv, odd_b_qkv, odd_sinks, odd_w_o, odd_b_o, mlp_w1, mlp_w2, loss_target, m_norm_mix_g, m_norm_mlp_g, m_final_norm_g, m_even_w_in, m_even_w_out, m_hgrn_lb_raw, m_hgrn_norm_g, m_odd_w_qkv, m_odd_b_qkv, m_odd_sinks, m_odd_w_o, m_odd_b_o, m_mlp_w1, m_mlp_w2, v_norm_mix_g, v_norm_mlp_g, v_final_norm_g, v_even_w_in, v_even_w_out, v_hgrn_lb_raw, v_hgrn_norm_g, v_odd_w_qkv, v_odd_b_qkv, v_odd_sinks, v_odd_w_o, v_odd_b_o, v_mlp_w1, v_mlp_w2):
    d = x.shape[2]
    depth = norm_mix_g.shape[0]
    n_even, n_odd = even_w_in.shape[0], odd_w_qkv.shape[0]
    xi, yi, ci = lax.axis_index("x"), lax.axis_index("y"), lax.axis_index("c")
    dev = 4 * xi + 2 * yi + ci
    core = ci.astype(jnp.int32).reshape(1)

    big_w = {"win": even_w_in, "wout": even_w_out, "wqkv": odd_w_qkv, "wo": odd_w_o,
             "w1": mlp_w1, "w2": mlp_w2}
    big_m = {"win": m_even_w_in, "wout": m_even_w_out, "wqkv": m_odd_w_qkv, "wo": m_odd_w_o,
             "w1": m_mlp_w1, "w2": m_mlp_w2}
    big_v = {"win": v_even_w_in, "wout": v_even_w_out, "wqkv": v_odd_w_qkv, "wo": v_odd_w_o,
             "w1": v_mlp_w1, "w2": v_mlp_w2}
    kinds = list(big_w)
    gathered = all_gather([cast_bf16(big_w[k], f"cast_{k}") for k in kinds], "gather_weights")
    wg = dict(zip(kinds, gathered))

    bq_w, bo_w = odd_b_qkv.shape[1], odd_b_o.shape[1]
    bq_mine = lax.dynamic_update_slice(jnp.zeros((n_odd, N_DEV * bq_w), F32), odd_b_qkv,
                                       (0, dev * bq_w))
    bo_mine = lax.dynamic_update_slice(jnp.zeros((n_odd, N_DEV * bo_w), F32), odd_b_o,
                                       (0, dev * bo_w))
    bq_full, bo_full = _unpack(all_reduce_small(_pack([bq_mine, bo_mine]), "gather_biases"),
                               [bq_mine.shape, bo_mine.shape])

    lbs, soft = lower_bounds_fwd(hgrn_lb_raw, "lower_bounds")

    dx, big, small = local_step(x[0], loss_target[0], norm_mix_g, norm_mlp_g, final_norm_g, lbs,
                                hgrn_norm_g, odd_sinks, bq_full, bo_full, wg["win"], wg["wout"],
                                wg["wqkv"], wg["wo"], wg["w1"], wg["w2"])

    parts = ([small["mix"][l] for l in range(depth)] + [small["mlp"][l] for l in range(depth)]
             + [small["final"]] + [small["lb"][e] for e in range(n_even)]
             + [small["gn"][e] for e in range(n_even)] + [small["sinks"][o] for o in range(n_odd)]
             + [small["bq"][o] for o in range(n_odd)] + [small["bo"][o] for o in range(n_odd)]
             + [small["loss"]])
    shapes = ([(depth, d)] * 2 + [(d,), hgrn_lb_raw.shape, hgrn_norm_g.shape, odd_sinks.shape,
              (n_odd, N_DEV * bq_w), (n_odd, N_DEV * bo_w), (1, LANES)])
    g_mix, g_mlp, g_final, d_lbs, g_gn, g_sinks, g_bq_full, g_bo_full, loss_v = _unpack(
        all_reduce_small(_pack(parts), "reduce_small"), shapes)
    g_lb = lower_bounds_bwd(soft, d_lbs, "lower_bounds_bwd")
    g_bq = lax.dynamic_slice(g_bq_full, (0, dev * bq_w), (n_odd, bq_w))
    g_bo = lax.dynamic_slice(g_bo_full, (0, dev * bo_w), (n_odd, bo_w))
    loss = loss_v[0, 0]

    small_names = ["norm_mix_g", "norm_mlp_g", "final_norm_g", "hgrn_lb_raw", "hgrn_norm_g",
                   "odd_b_qkv", "odd_sinks", "odd_b_o"]
    small_w = [norm_mix_g, norm_mlp_g, final_norm_g, hgrn_lb_raw, hgrn_norm_g, odd_b_qkv,
               odd_sinks, odd_b_o]
    small_m = [m_norm_mix_g, m_norm_mlp_g, m_final_norm_g, m_hgrn_lb_raw, m_hgrn_norm_g,
               m_odd_b_qkv, m_odd_sinks, m_odd_b_o]
    small_v = [v_norm_mix_g, v_norm_mlp_g, v_final_norm_g, v_hgrn_lb_raw, v_hgrn_norm_g,
               v_odd_b_qkv, v_odd_sinks, v_odd_b_o]
    small_g = [g_mix, g_mlp, g_final, g_lb, g_gn, g_bq, g_sinks, g_bo]
    sshapes = [w.shape for w in small_w]
    sd, sm, sv = adamw_small(_pack(small_w), _pack(small_g), _pack(small_m), _pack(small_v),
                             "adamw_small")
    res = {}
    for name, g, dl, m2, v2 in zip(small_names, small_g, _unpack(sd, sshapes),
                                   _unpack(sm, sshapes), _unpack(sv, sshapes)):
        res[name] = (g.reshape(dl.shape), dl, m2, v2)

    grads = [g for _, _, g in big]
    received = exchange_with_sibling(grads, "scatter_grads_d2d")
    partials = [pair_add(g, r, core, f"pair_add_{k}")
                for (k, _, _), g, r in zip(big, grads, received)]
    stacked = exchange_between_chips(partials, [l for _, l, _ in big],
                                     [kinds.index(k) for k, _, _ in big], "scatter_grads_ici")
    long_names = {"win": "even_w_in", "wout": "even_w_out", "wqkv": "odd_w_qkv", "wo": "odd_w_o",
                  "w1": "mlp_w1", "w2": "mlp_w2"}
    for k, parts4 in zip(kinds, stacked):
        res[long_names[k]] = tuple(adamw_big(parts4, big_w[k], big_m[k], big_v[k], f"adamw_{k}"))

    order = ["norm_mix_g", "norm_mlp_g", "final_norm_g", "even_w_in", "even_w_out", "hgrn_lb_raw",
             "hgrn_norm_g", "odd_w_qkv", "odd_b_qkv", "odd_sinks", "odd_w_o", "odd_b_o", "mlp_w1",
             "mlp_w2"]
    outs = [loss, dx[None]]
    for j in range(4):
        outs += [res[n][j] for n in order]
    return tuple(outs)
```

```python
import functools
import math

import jax
import jax.numpy as jnp
from jax import lax
from jax.experimental import pallas as pl
from jax.experimental.pallas import tpu as pltpu

F32 = jnp.float32
BF16 = jnp.bfloat16
MESH = pl.DeviceIdType.MESH

N_DEV = 8
NORM_EPS = 1e-5
ROPE_THETA = 500000.0
BLK = 128
A_DIM = 128
A_BRANCHES = ((128, 1), (512, 4), (2048, 16))
B_DIM = 128
B_CHUNK = 64
C_DIM = 64
C_GROUP = 8
C_WINDOW = 128
LANES = 128

ADAM_LR = 0.001
ADAM_B1 = 0.9
ADAM_B2 = 0.999
ADAM_EPS = 1e-08
ADAM_WD = 0.01
ADAM_STEP = 10

NN = (((1,), (0,)), ((), ()))
NT = (((1,), (1,)), ((), ()))
TN = (((0,), (0,)), ((), ()))


def _params(*sem):
    return pltpu.CompilerParams(dimension_semantics=sem)


def _sigmoid(x):
    return 1.0 / (1.0 + jnp.exp(-x))


def _rows_call(name, body, row_ins, full_ins, row_outs, acc_outs, tm):
    t = row_ins[0].shape[0]
    n_ri, n_fi, n_ro = len(row_ins), len(full_ins), len(row_outs)

    def kern(*refs):
        i = pl.program_id(0)
        body(i, refs[:n_ri], refs[n_ri:n_ri + n_fi],
             refs[n_ri + n_fi:n_ri + n_fi + n_ro], refs[n_ri + n_fi + n_ro:])

    def row_spec(shape):
        return pl.BlockSpec((tm,) + tuple(shape[1:]), lambda i: (i,) + (0,) * (len(shape) - 1))

    def full_spec(shape):
        return pl.BlockSpec(tuple(shape), lambda i: (0,) * len(shape))

    outs = pl.pallas_call(
        kern, name=name, grid=(t // tm,),
        in_specs=[row_spec(a.shape) for a in row_ins] + [full_spec(a.shape) for a in full_ins],
        out_specs=[row_spec(s.shape) for s in row_outs] + [full_spec(s.shape) for s in acc_outs],
        out_shape=list(row_outs) + list(acc_outs),
        compiler_params=_params("arbitrary" if acc_outs else "parallel"),
    )(*row_ins, *full_ins)
    return outs


def _sds(shape, dtype):
    return jax.ShapeDtypeStruct(tuple(shape), dtype)


def rms_fwd(x, g, name):
    t, d = x.shape

    def body(i, ri, fi, ro, ao):
        xv = ri[0][...]
        r = lax.rsqrt(jnp.mean(xv * xv, axis=-1, keepdims=True) + NORM_EPS)
        ro[0][...] = (xv * r * fi[0][...]).astype(BF16)

    return _rows_call(name, body, [x], [g.reshape(1, d)], [_sds((t, d), BF16)], [], 256)[0]


def rms_bwd(x, g, dh, dx_res, name):
    t, d = x.shape

    def body(i, ri, fi, ro, ao):
        xv, dhv, res = ri[0][...], ri[1][...], ri[2][...]
        gv = fi[0][...]
        r = lax.rsqrt(jnp.mean(xv * xv, axis=-1, keepdims=True) + NORM_EPS)
        gd = gv * dhv
        dx = res + r * gd - xv * (r * r * r) * jnp.mean(xv * gd, axis=-1, keepdims=True)
        ro[0][...] = dx
        ro[1][...] = dx.astype(BF16)

        @pl.when(i == 0)
        def _():
            ao[0][...] = jnp.zeros_like(ao[0])
            ao[1][...] = jnp.zeros_like(ao[1])

        ao[0][...] += jnp.sum(dhv * xv * r, axis=0, keepdims=True)
        ao[1][...] += jnp.sum(dx, axis=0, keepdims=True)

    return _rows_call(name, body, [x, dh, dx_res], [g.reshape(1, d)],
                      [_sds((t, d), F32), _sds((t, d), BF16)],
                      [_sds((1, d), F32), _sds((1, d), F32)], 256)


def loss_head(x, g, target, name):
    t, d = x.shape

    def body(i, ri, fi, ro, ao):
        xv, tg = ri[0][...], ri[1][...]
        gv = fi[0][...]
        r = lax.rsqrt(jnp.mean(xv * xv, axis=-1, keepdims=True) + NORM_EPS)
        e = xv * r * gv - tg
        dy = e * (1.0 / d)
        gd = gv * dy
        dx = r * gd - xv * (r * r * r) * jnp.mean(xv * gd, axis=-1, keepdims=True)
        ro[0][...] = dx
        ro[1][...] = dx.astype(BF16)

        @pl.when(i == 0)
        def _():
            ao[0][...] = jnp.zeros_like(ao[0])
            ao[1][...] = jnp.zeros_like(ao[1])

        ao[0][...] += jnp.sum(dy * xv * r, axis=0, keepdims=True)
        part = 0.5 * jnp.sum(jnp.mean(e * e, axis=-1, keepdims=True), axis=0, keepdims=True)
        ao[1][...] += jnp.broadcast_to(part, (1, LANES))

    return _rows_call(name, body, [x, target], [g.reshape(1, d)],
                      [_sds((t, d), F32), _sds((t, d), BF16)],
                      [_sds((1, d), F32), _sds((1, LANES), F32)], 256)


def rope_tables(seq, head_dim):
    rot = head_dim // 4
    half = rot // 2
    inv_freq = 1.0 / (ROPE_THETA ** (jnp.arange(0, rot, 2, dtype=F32) / rot))
    ang = jnp.arange(seq, dtype=F32)[:, None] * inv_freq[None, :]
    cos, sin = jnp.cos(ang), jnp.sin(ang)
    zeros = jnp.zeros((seq, head_dim - rot), F32)
    zh = jnp.zeros((seq, half), F32)
    c = jnp.concatenate([cos, cos, jnp.ones((seq, head_dim - rot), F32)], axis=-1)
    sp = jnp.concatenate([zh, sin, zeros], axis=-1)
    sm = jnp.concatenate([-sin, zh, zeros], axis=-1)
    rep = LANES // head_dim
    return jnp.tile(c, (1, rep)), jnp.tile(sp, (1, rep)), jnp.tile(sm, (1, rep)), half


def rope_call(x, tabs, width, n_rope, inverse, name, col_sum=False):
    c, sp, sm, half = tabs
    t = x.shape[0]
    tm = 256
    n_slab = width // LANES

    def kern(x_ref, c_ref, sp_ref, sm_ref, o_ref, *acc):
        cv, spv, smv = c_ref[...], sp_ref[...], sm_ref[...]
        for j in range(n_slab):
            xs = x_ref[:, j * LANES:(j + 1) * LANES].astype(F32)
            if j < n_rope:
                if inverse:
                    ys = (xs * cv + pltpu.roll(xs * spv, LANES - half, 1)
                          + pltpu.roll(xs * smv, half, 1))
                else:
                    ys = (xs * cv + pltpu.roll(xs, half, 1) * spv
                          + pltpu.roll(xs, LANES - half, 1) * smv)
            else:
                ys = xs
            o_ref[:, j * LANES:(j + 1) * LANES] = ys.astype(BF16)
            if col_sum:
                @pl.when(pl.program_id(0) == 0)
                def _():
                    acc[0][:, j * LANES:(j + 1) * LANES] = jnp.zeros((1, LANES), F32)
                acc[0][:, j * LANES:(j + 1) * LANES] += jnp.sum(ys, axis=0, keepdims=True)

    tab_spec = pl.BlockSpec((tm, LANES), lambda i: (i, 0))
    out_shape = [_sds((t, width), BF16)]
    out_specs = [pl.BlockSpec((tm, width), lambda i: (i, 0))]
    if col_sum:
        out_shape.append(_sds((1, width), F32))
        out_specs.append(pl.BlockSpec((1, width), lambda i: (0, 0)))
    return pl.pallas_call(
        kern, name=name, grid=(t // tm,),
        in_specs=[pl.BlockSpec((tm, width), lambda i: (i, 0)), tab_spec, tab_spec, tab_spec],
        out_specs=out_specs, out_shape=out_shape,
        compiler_params=_params("arbitrary" if col_sum else "parallel"),
    )(x, c, sp, sm)


def _mm_call(name, a, b, extras, out_shapes, grid, a_spec, b_spec, extra_specs, out_specs,
             acc_shape, dims, epilogue):
    n_ex, n_out = len(extras), len(out_shapes)
    nk = grid[2]

    def product(a_ref, b_ref):
        bv = b_ref[...]
        if bv.ndim == 3:
            bv = bv.reshape(bv.shape[0] * bv.shape[1], bv.shape[2])
        return lax.dot_general(a_ref[...].astype(BF16), bv.astype(BF16), dims,
                               preferred_element_type=F32)

    def kern(*refs):
        a_ref, b_ref = refs[0], refs[1]
        ex = refs[2:2 + n_ex]
        outs = refs[2 + n_ex:2 + n_ex + n_out]
        if nk == 1:
            epilogue(product(a_ref, b_ref), ex, outs)
            return
        acc = refs[-1]
        k = pl.program_id(2)

        @pl.when(k == 0)
        def _():
            acc[...] = product(a_ref, b_ref)

        @pl.when(k > 0)
        def _():
            acc[...] += product(a_ref, b_ref)

        @pl.when(k == nk - 1)
        def _():
            epilogue(acc[...], ex, outs)

    return pl.pallas_call(
        kern, name=name, grid=grid,
        in_specs=[a_spec, b_spec, *extra_specs], out_specs=out_specs, out_shape=out_shapes,
        scratch_shapes=[pltpu.VMEM(acc_shape, F32)] if nk > 1 else [],
        compiler_params=_params("parallel", "parallel", "arbitrary"),
    )(a, b, *extras)


def _ep_store(dtype):
    def ep(acc, ex, outs):
        outs[0][...] = acc.astype(dtype)
    return ep


def _ep_residual(acc, ex, outs):
    outs[0][...] = acc + ex[0][...]


def _ep_bias(acc, ex, outs):
    outs[0][...] = acc + ex[0][...]


def _ep_bias_residual(acc, ex, outs):
    outs[0][...] = acc + ex[0][...] + ex[1][...]


def _ep_relu2(acc, ex, outs):
    outs[0][...] = acc
    rl = jnp.maximum(acc, 0.0)
    outs[1][...] = (rl * rl).astype(BF16)


def _ep_relu2_bwd(acc, ex, outs):
    outs[0][...] = (acc * (2.0 * jnp.maximum(ex[0][...], 0.0))).astype(BF16)


MM_TM = 1024
MM_TN = 1024
MM_TK = 2048


def mm_cols_sharded(a, wg, layer, name, epilogue=None, n_out=1):
    m, kdim = a.shape
    n = wg.shape[-1]
    tm, tk = min(m, MM_TM), min(kdim, MM_TK)
    if epilogue is None:
        epilogue, outs = _ep_store(F32), [_sds((m, N_DEV * n), F32)]
    else:
        outs = [_sds((m, N_DEV * n), F32), _sds((m, N_DEV * n), BF16)][:n_out]
    return _mm_call(
        name, a, wg, [], outs, (m // tm, N_DEV, kdim // tk),
        pl.BlockSpec((tm, tk), lambda i, j, k: (i, k)),
        pl.BlockSpec((None, None, tk, n), lambda i, j, k: (j, layer, k, 0)),
        [], [pl.BlockSpec((tm, n), lambda i, j, k: (i, j))] * len(outs),
        (tm, n), NN, epilogue)


def _extra_specs(extra_kinds, tm, tn):
    specs = []
    for kind in extra_kinds:
        if kind == "row":
            specs.append(pl.BlockSpec((1, tn), lambda i, j, k: (0, j)))
        else:
            specs.append(pl.BlockSpec((tm, tn), lambda i, j, k: (i, j)))
    return specs


def mm_rows_sharded(a, wg, layer, name, extras, extra_kinds, epilogue):
    m, kdim = a.shape
    ks, n = wg.shape[-2], wg.shape[-1]
    tm, tn = min(m, MM_TM), min(n, MM_TN)
    gps = max(1, min(kdim, MM_TK) // ks)
    return _mm_call(
        name, a, wg, extras, [_sds((m, n), F32)], (m // tm, n // tn, N_DEV // gps),
        pl.BlockSpec((tm, gps * ks), lambda i, j, k: (i, k)),
        pl.BlockSpec((gps, None, ks, tn), lambda i, j, k: (k, layer, 0, j)),
        _extra_specs(extra_kinds, tm, tn), [pl.BlockSpec((tm, tn), lambda i, j, k: (i, j))],
        (tm, tn), NN, epilogue)[0]


def mm_plain(a, w, name, extras, extra_kinds, epilogue, tn=512):
    m, kdim = a.shape
    n = w.shape[1]
    tm, tk = min(m, MM_TM), min(kdim, MM_TK)
    return _mm_call(
        name, a, w, extras, [_sds((m, n), F32)], (m // tm, n // tn, kdim // tk),
        pl.BlockSpec((tm, tk), lambda i, j, k: (i, k)),
        pl.BlockSpec((tk, tn), lambda i, j, k: (k, j)),
        _extra_specs(extra_kinds, tm, tn), [pl.BlockSpec((tm, tn), lambda i, j, k: (i, j))],
        (tm, tn), NN, epilogue)[0]


def mm_nt_cols_sharded(dy, wg, layer, name):
    m = dy.shape[0]
    kdim, n = wg.shape[-2], wg.shape[-1]
    tm, tn = min(m, MM_TM), min(kdim, MM_TN)
    return _mm_call(
        name, dy, wg, [], [_sds((m, kdim), F32)], (m // tm, kdim // tn, N_DEV),
        pl.BlockSpec((tm, n), lambda i, j, k: (i, k)),
        pl.BlockSpec((None, None, tn, n), lambda i, j, k: (k, layer, j, 0)),
        [], [pl.BlockSpec((tm, tn), lambda i, j, k: (i, j))],
        (tm, tn), NT, _ep_store(F32))[0]


def mm_nt_rows_sharded(dy, wg, layer, name, extras=(), epilogue=None, out_dtype=F32):
    m, n = dy.shape
    ks = wg.shape[-2]
    tm, tk = min(m, MM_TM), min(n, MM_TK)
    gps = max(1, MM_TN // ks)
    tn = gps * ks
    epilogue = _ep_store(out_dtype) if epilogue is None else epilogue
    return _mm_call(
        name, dy, wg, list(extras), [_sds((m, N_DEV * ks), out_dtype)],
        (m // tm, N_DEV // gps, n // tk),
        pl.BlockSpec((tm, tk), lambda i, j, k: (i, k)),
        pl.BlockSpec((gps, None, ks, tk), lambda i, j, k: (j, layer, 0, k)),
        [pl.BlockSpec((tm, tn), lambda i, j, k: (i, j))] * len(extras),
        [pl.BlockSpec((tm, tn), lambda i, j, k: (i, j))],
        (tm, tn), NT, epilogue)[0]


def mm_nt_plain(dy, w, name, tk):
    m, n = dy.shape
    kdim = w.shape[0]
    tm, tn = min(m, MM_TM), min(kdim, MM_TN)
    return _mm_call(
        name, dy, w, [], [_sds((m, kdim), F32)], (m // tm, kdim // tn, n // tk),
        pl.BlockSpec((tm, tk), lambda i, j, k: (i, k)),
        pl.BlockSpec((tn, tk), lambda i, j, k: (j, k)),
        [], [pl.BlockSpec((tm, tn), lambda i, j, k: (i, j))],
        (tm, tn), NT, _ep_store(F32))[0]


def mm_tn(a, dy, name, shard_cols=None, tn=MM_TN):
    t, kdim = a.shape
    n = dy.shape[1]
    tm, tk = min(kdim, MM_TM), min(t, MM_TK)
    if shard_cols is None:
        tn = min(tn, n)
        out = _sds((kdim, n), BF16)
        o_spec = pl.BlockSpec((tm, tn), lambda i, j, k: (i, j))
    else:
        tn = shard_cols
        out = _sds((n // tn, kdim, tn), BF16)
        o_spec = pl.BlockSpec((None, tm, tn), lambda i, j, k: (j, i, 0))
    return _mm_call(
        name, a, dy, [], [out], (kdim // tm, n // tn, t // tk),
        pl.BlockSpec((tk, tm), lambda i, j, k: (k, i)),
        pl.BlockSpec((tk, tn), lambda i, j, k: (k, j)),
        [], [o_spec], (tm, tn), TN, _ep_store(BF16))[0]


def _band_mask(g, nk_prev_valid, max_dist):
    rows = lax.broadcasted_iota(jnp.int32, (g * BLK, 2 * BLK), 0) % BLK
    cols = lax.broadcasted_iota(jnp.int32, (g * BLK, 2 * BLK), 1)
    dist = rows + BLK - cols
    ok = (dist >= 0) & (dist <= max_dist)
    return ok & ((cols >= BLK) | nk_prev_valid)


def band_fwd(qkv, q0, k0, v0, hk, g, seg, max_dist, name, sink_rows=None):
    t, dh = qkv.shape[1], qkv.shape[2]
    nb = t // BLK
    scale = dh ** -0.5
    has_sink = sink_rows is not None

    def kern(*refs):
        if has_sink:
            q_ref, k_ref, v_ref, s_ref, num_ref, m_ref, l_ref = refs
            sink = s_ref[...]
        else:
            q_ref, k_ref, v_ref, num_ref, m_ref, l_ref = refs
        b = pl.program_id(1)
        cur = pl.multiple_of(b * BLK, BLK)
        prev = pl.multiple_of(jnp.maximum(b - 1, 0) * BLK, BLK)
        q = q_ref[...].reshape(g * BLK, dh)
        kk = jnp.concatenate([k_ref[pl.ds(prev, BLK), :], k_ref[pl.ds(cur, BLK), :]], axis=0)
        vv = jnp.concatenate([v_ref[pl.ds(prev, BLK), :], v_ref[pl.ds(cur, BLK), :]], axis=0)
        s = lax.dot_general(q, kk, NT, preferred_element_type=F32) * scale
        s = jnp.where(_band_mask(g, (b % seg) != 0, max_dist), s, -jnp.inf)
        m = jnp.max(s, axis=-1, keepdims=True)
        if has_sink:
            m = jnp.maximum(m, sink)
        p = jnp.exp(s - m)
        l = jnp.sum(p, axis=-1, keepdims=True)
        if has_sink:
            l = l + jnp.exp(sink - m)
        num = jnp.dot(p.astype(BF16), vv, preferred_element_type=F32)
        num_ref[...] = num.reshape(g, BLK, dh)
        m_ref[...] = m.reshape(g, BLK, 1)
        l_ref[...] = l.reshape(g, BLK, 1)

    in_specs = [pl.BlockSpec((g, BLK, dh), lambda h, b: (q0 // g + h, b, 0)),
                pl.BlockSpec((None, t, dh), lambda h, b: (k0 + h, 0, 0)),
                pl.BlockSpec((None, t, dh), lambda h, b: (v0 + h, 0, 0))]
    args = [qkv, qkv, qkv]
    if has_sink:
        in_specs.append(pl.BlockSpec((None, g * BLK, 1), lambda h, b: (h, 0, 0)))
        args.append(sink_rows)
    hq = hk * g
    return pl.pallas_call(
        kern, name=name, grid=(hk, nb), in_specs=in_specs,
        out_specs=[pl.BlockSpec((g, BLK, dh), lambda h, b: (h, b, 0)),
                   pl.BlockSpec((g, BLK, 1), lambda h, b: (h, b, 0)),
                   pl.BlockSpec((g, BLK, 1), lambda h, b: (h, b, 0))],
        out_shape=[_sds((hq, t, dh), F32), _sds((hq, t, 1), F32), _sds((hq, t, 1), F32)],
        compiler_params=_params("parallel", "parallel"),
    )(*args)


def band_bwd(qkv, q0, k0, v0, do, lse, delta, hk, g, seg, max_dist, name, sink_rows=None):
    t, dh = qkv.shape[1], qkv.shape[2]
    nb = t // BLK
    scale = dh ** -0.5
    has_sink = sink_rows is not None

    def kern(*refs):
        if has_sink:
            (q_ref, k_ref, v_ref, do_ref, lse_ref, dl_ref, s_ref,
             dq_ref, dk_ref, dv_ref, ds_ref, sacc) = refs
            sink = s_ref[...]
        else:
            q_ref, k_ref, v_ref, do_ref, lse_ref, dl_ref, dq_ref, dk_ref, dv_ref = refs
        b = pl.program_id(1)

        @pl.when(b == 0)
        def _():
            dk_ref[...] = jnp.zeros_like(dk_ref)
            dv_ref[...] = jnp.zeros_like(dv_ref)
            if has_sink:
                sacc[...] = jnp.zeros_like(sacc)

        cur = pl.multiple_of(b * BLK, BLK)
        prev = pl.multiple_of(jnp.maximum(b - 1, 0) * BLK, BLK)
        q = q_ref[...].reshape(g * BLK, dh)
        dout = do_ref[...].reshape(g * BLK, dh)
        lse_b = lse_ref[...].reshape(g * BLK, 1)
        dl_b = dl_ref[...].reshape(g * BLK, 1)
        kk = jnp.concatenate([k_ref[pl.ds(prev, BLK), :], k_ref[pl.ds(cur, BLK), :]], axis=0)
        vv = jnp.concatenate([v_ref[pl.ds(prev, BLK), :], v_ref[pl.ds(cur, BLK), :]], axis=0)
        s = lax.dot_general(q, kk, NT, preferred_element_type=F32) * scale
        s = jnp.where(_band_mask(g, (b % seg) != 0, max_dist), s, -jnp.inf)
        p = jnp.exp(s - lse_b)
        dp = lax.dot_general(dout, vv, NT, preferred_element_type=F32)
        ds = (p * (dp - dl_b) * scale).astype(BF16)
        dq = jnp.dot(ds, kk, preferred_element_type=F32)
        dq_ref[...] = dq.reshape(g, BLK, dh)
        dkk = lax.dot_general(ds, q, TN, preferred_element_type=F32)
        dvv = lax.dot_general(p.astype(BF16), dout, TN, preferred_element_type=F32)
        dk_ref[pl.ds(prev, BLK), :] += dkk[:BLK]
        dk_ref[pl.ds(cur, BLK), :] += dkk[BLK:]
        dv_ref[pl.ds(prev, BLK), :] += dvv[:BLK]
        dv_ref[pl.ds(cur, BLK), :] += dvv[BLK:]
        if has_sink:
            sacc[...] += -jnp.exp(sink - lse_b) * dl_b

            @pl.when(b == nb - 1)
            def _():
                for gi in range(g):
                    ds_ref[gi:gi + 1, :] = jnp.sum(sacc[gi * BLK:(gi + 1) * BLK, :], axis=0,
                                                   keepdims=True)

    in_specs = [pl.BlockSpec((g, BLK, dh), lambda h, b: (q0 // g + h, b, 0)),
                pl.BlockSpec((None, t, dh), lambda h, b: (k0 + h, 0, 0)),
                pl.BlockSpec((None, t, dh), lambda h, b: (v0 + h, 0, 0)),
                pl.BlockSpec((g, BLK, dh), lambda h, b: (h, b, 0)),
                pl.BlockSpec((g, BLK, 1), lambda h, b: (h, b, 0)),
                pl.BlockSpec((g, BLK, 1), lambda h, b: (h, b, 0))]
    args = [qkv, qkv, qkv, do, lse, delta]
    hq = hk * g
    out_specs = [pl.BlockSpec((g, BLK, dh), lambda h, b: (h, b, 0)),
                 pl.BlockSpec((None, t, dh), lambda h, b: (h, 0, 0)),
                 pl.BlockSpec((None, t, dh), lambda h, b: (h, 0, 0))]
    out_shape = [_sds((hq, t, dh), F32), _sds((hk, t, dh), F32), _sds((hk, t, dh), F32)]
    scratch = []
    if has_sink:
        in_specs.append(pl.BlockSpec((None, g * BLK, 1), lambda h, b: (h, 0, 0)))
        args.append(sink_rows)
        out_specs.append(pl.BlockSpec((None, g, 1), lambda h, b: (h, 0, 0)))
        out_shape.append(_sds((hk, g, 1), F32))
        scratch.append(pltpu.VMEM((g * BLK, 1), F32))
    return pl.pallas_call(
        kern, name=name, grid=(hk, nb), in_specs=in_specs, out_specs=out_specs, out_shape=out_shape,
        scratch_shapes=scratch, compiler_params=_params("parallel", "arbitrary"),
    )(*args)


def merge_branches(nums, ms, ls, name):
    h, t, dh = nums[0].shape
    nbr = len(nums)

    def kern(*refs):
        num_refs, m_refs, l_refs = refs[:nbr], refs[nbr:2 * nbr], refs[2 * nbr:3 * nbr]
        o_ref, lse_ref = refs[3 * nbr], refs[3 * nbr + 1]
        mall = m_refs[0][...]
        for i in range(1, nbr):
            mall = jnp.maximum(mall, m_refs[i][...])
        num = jnp.zeros((t, dh), F32)
        den = jnp.zeros((t, 1), F32)
        for i in range(nbr):
            w = jnp.exp(m_refs[i][...] - mall)
            num = num + w * num_refs[i][...]
            den = den + w * l_refs[i][...]
        o_ref[...] = num / den
        lse_ref[...] = mall + jnp.log(den)

    big = pl.BlockSpec((None, t, dh), lambda i: (i, 0, 0))
    col = pl.BlockSpec((None, t, 1), lambda i: (i, 0, 0))
    return pl.pallas_call(
        kern, name=name, grid=(h,), in_specs=[big] * nbr + [col] * (2 * nbr),
        out_specs=[big, col], out_shape=[_sds((h, t, dh), F32), _sds((h, t, 1), F32)],
        compiler_params=_params("parallel"),
    )(*nums, *ms, *ls)


def normalise_heads(num, m, l, name):
    h, t, dh = num.shape

    def kern(num_ref, m_ref, l_ref, o_ref, lse_ref):
        lv = l_ref[...]
        o_ref[...] = num_ref[...] / lv
        lse_ref[...] = m_ref[...] + jnp.log(lv)

    big = pl.BlockSpec((None, t, dh), lambda i: (i, 0, 0))
    col = pl.BlockSpec((None, t, 1), lambda i: (i, 0, 0))
    return pl.pallas_call(
        kern, name=name, grid=(h,), in_specs=[big, col, col], out_specs=[big, col],
        out_shape=[_sds((h, t, dh), F32), _sds((h, t, 1), F32)],
        compiler_params=_params("parallel"),
    )(num, m, l)


def head_delta(o, do, name):
    h, t, dh = o.shape

    def kern(o_ref, do_ref, d_ref):
        d_ref[...] = jnp.sum(o_ref[...] * do_ref[...], axis=-1, keepdims=True)

    big = pl.BlockSpec((None, t, dh), lambda i: (i, 0, 0))
    return pl.pallas_call(
        kern, name=name, grid=(h,), in_specs=[big, big],
        out_specs=pl.BlockSpec((None, t, 1), lambda i: (i, 0, 0)),
        out_shape=_sds((h, t, 1), F32), compiler_params=_params("parallel"),
    )(o, do)


def _cumsum_rows(x, n, reverse=False):
    rows = lax.broadcasted_iota(jnp.int32, x.shape, 0)
    shift = 1
    while shift < n:
        if reverse:
            x = x + jnp.where(rows < n - shift, pltpu.roll(x, n - shift, 0), 0.0)
        else:
            x = x + jnp.where(rows >= shift, pltpu.roll(x, shift, 0), 0.0)
        shift *= 2
    return x


def _hgrn_gates(f, lb):
    sig = _sigmoid(f)
    gate = lb + (1.0 - lb) * sig
    return sig, gate


B_SUB = 16


def _dot3(a, b, dims):
    ah, bh = a.astype(BF16), b.astype(BF16)
    al = (a - ah.astype(F32)).astype(BF16)
    bl = (b - bh.astype(F32)).astype(BF16)
    dot = functools.partial(lax.dot_general, dimension_numbers=dims, preferred_element_type=F32)
    return dot(ah, bh) + dot(al, bh) + dot(ah, bl)


def _sub_scales(b, i):
    r0 = i * B_SUB
    beta = b[r0 - 1:r0, :]
    return jnp.exp(b[r0:r0 + B_SUB, :] - beta), jnp.exp(jnp.minimum(beta - b, 0.0))


def _hgrn_intra_attn(qq, kk, b):
    c = qq.shape[0]
    lane = lax.broadcasted_iota(jnp.int32, (B_SUB, c), 1)
    trow = lax.broadcasted_iota(jnp.int32, (B_SUB, B_DIM), 0)
    blocks = []
    for i in range(c // B_SUB):
        r0 = i * B_SUB
        qi, bi = qq[r0:r0 + B_SUB, :], b[r0:r0 + B_SUB, :]
        if i == 0:
            a_i = jnp.zeros((B_SUB, c), F32)
        else:
            eq, ek = _sub_scales(b, i)
            a_i = jnp.where(lane < r0, _dot3(qi * eq, kk * ek, NT), 0.0)
        for sl in range(B_SUB):
            s = r0 + sl
            e = jnp.exp(jnp.where(trow >= sl, bi - b[s:s + 1, :], -jnp.inf))
            col = jnp.sum(qi * kk[s:s + 1, :] * e, axis=1, keepdims=True)
            a_i = jnp.where(lane == s, col, a_i)
        blocks.append(a_i)
    return jnp.concatenate(blocks, axis=0)


def hgrn_fwd(proj, col0, nh, lb, gn, name):
    t = proj.shape[0]
    c = B_CHUNK
    nc = t // c
    scale = B_DIM ** -0.5

    def kern(q_ref, f_ref, i_ref, g_ref, lb_ref, gn_ref, out_ref, opre_ref, st_ref, a_ref, state):
        lbv = lb_ref[...]
        gnv = gn_ref[...]
        state[...] = jnp.zeros_like(state)

        def chunk(ci, carry):
            rows = pl.ds(pl.multiple_of(ci * c, c), c)
            _, gate = _hgrn_gates(f_ref[rows, :], lbv)
            kk = 1.0 - gate
            qb = q_ref[rows, :]
            qq = qb * _sigmoid(qb) * scale
            v = i_ref[rows, :]
            b = _cumsum_rows(jnp.log(gate), c)
            st = state[...]
            st_ref[ci] = st
            o_inter = lax.dot_general((qq * jnp.exp(b)).astype(BF16), st.astype(BF16), NT,
                                      preferred_element_type=F32)
            amat = _hgrn_intra_attn(qq, kk, b)
            a_ref[ci] = amat
            o = jnp.dot(amat.astype(BF16), v.astype(BF16), preferred_element_type=F32) + o_inter
            opre_ref[rows, :] = o
            bl = b[c - 1:c, :]
            state[...] = st * jnp.exp(bl) + lax.dot_general(
                v.astype(BF16), (kk * jnp.exp(bl - b)).astype(BF16), TN, preferred_element_type=F32)
            r = lax.rsqrt(jnp.mean(o * o, axis=-1, keepdims=True) + NORM_EPS)
            gb = g_ref[rows, :]
            out_ref[rows, :] = (o * r * gnv * (gb * _sigmoid(gb))).astype(BF16)
            return carry

        lax.fori_loop(0, nc, chunk, 0)

    def col(off):
        return pl.BlockSpec((t, B_DIM), lambda h: (0, col0 + off * nh + h))

    return pl.pallas_call(
        kern, name=name, grid=(nh,),
        in_specs=[col(0), col(1), col(2), col(3),
                  pl.BlockSpec((None, 1, B_DIM), lambda h: (h, 0, 0)),
                  pl.BlockSpec((1, B_DIM), lambda h: (0, 0))],
        out_specs=[pl.BlockSpec((t, B_DIM), lambda h: (0, h)),
                   pl.BlockSpec((t, B_DIM), lambda h: (0, h)),
                   pl.BlockSpec((None, nc, B_DIM, B_DIM), lambda h: (h, 0, 0, 0)),
                   pl.BlockSpec((None, nc, c, c), lambda h: (h, 0, 0, 0))],
        out_shape=[_sds((t, nh * B_DIM), BF16), _sds((t, nh * B_DIM), F32),
                   _sds((nh, nc, B_DIM, B_DIM), F32), _sds((nh, nc, c, c), F32)],
        scratch_shapes=[pltpu.VMEM((B_DIM, B_DIM), F32)],
        compiler_params=_params("parallel"),
    )(proj, proj, proj, proj, lb, gn)


def hgrn_bwd(proj, col0, nh, lb, gn, opre, states, amats, dout, dcol0, name):
    t = proj.shape[0]
    c = B_CHUNK
    nc = t // c
    scale = B_DIM ** -0.5
    nsub = c // B_SUB

    def kern(q_ref, f_ref, i_ref, g_ref, lb_ref, gn_ref, opre_ref, st_ref, a_ref, dout_ref,
             dq_ref, df_ref, di_ref, dg_ref, dgn_ref, dlb_ref, dstate, dksc):
        lbv = lb_ref[...]
        gnv = gn_ref[...]
        dstate[...] = jnp.zeros_like(dstate)
        dlb_ref[...] = jnp.zeros_like(dlb_ref)

        @pl.when(pl.program_id(0) == 0)
        def _():
            dgn_ref[...] = jnp.zeros_like(dgn_ref)

        srow = lax.broadcasted_iota(jnp.int32, (c, B_DIM), 0)
        lane = lax.broadcasted_iota(jnp.int32, (B_SUB, c), 1)
        trow = lax.broadcasted_iota(jnp.int32, (B_SUB, B_DIM), 0)
        arow = lax.broadcasted_iota(jnp.int32, (c, c), 0)
        alane = lax.broadcasted_iota(jnp.int32, (c, c), 1)

        def chunk(cj, carry):
            ci = nc - 1 - cj
            rows = pl.ds(pl.multiple_of(ci * c, c), c)
            f = f_ref[rows, :]
            sig, gate = _hgrn_gates(f, lbv)
            kk = 1.0 - gate
            qb = q_ref[rows, :]
            sq = _sigmoid(qb)
            qq = qb * sq * scale
            v = i_ref[rows, :]
            b = _cumsum_rows(jnp.log(gate), c)
            st0 = st_ref[ci]
            dst = dstate[...]
            o = opre_ref[rows, :]
            gb = g_ref[rows, :]
            sg = _sigmoid(gb)
            silu_g = gb * sg
            d_out = dout_ref[rows, :]
            r = lax.rsqrt(jnp.mean(o * o, axis=-1, keepdims=True) + NORM_EPS)
            y = o * r
            dg_ref[rows, :] = (d_out * y * gnv * (sg * (1.0 + gb * (1.0 - sg)))).astype(BF16)
            dyn = d_out * silu_g
            dgn_ref[...] += jnp.sum(dyn * y, axis=0, keepdims=True)
            dy = dyn * gnv
            do = r * (dy - y * jnp.mean(dy * y, axis=-1, keepdims=True))
            eb = jnp.exp(b)
            bl = b[c - 1:c, :]
            ebl = jnp.exp(bl - b)
            ebl_last = jnp.exp(bl)
            do_b = do.astype(BF16)
            dst_b = dst.astype(BF16)
            dq_inter = jnp.dot(do_b, st0.astype(BF16), preferred_element_type=F32) * eb
            dst0 = lax.dot_general(do_b, (qq * eb).astype(BF16), TN,
                                   preferred_element_type=F32) + dst * ebl_last
            dv_inter = lax.dot_general((kk * ebl).astype(BF16), dst_b, NT, preferred_element_type=F32)
            dk_inter = jnp.dot(v.astype(BF16), dst_b, preferred_element_type=F32) * ebl
            amat = a_ref[ci]
            v_b = v.astype(BF16)
            d_a = lax.dot_general(do_b, v_b, NT, preferred_element_type=F32)
            d_a = jnp.where(arow >= alane, d_a, 0.0)
            dv_intra = lax.dot_general(amat.astype(BF16), do_b, TN, preferred_element_type=F32)
            dk_pairs = jnp.zeros((c, B_DIM), F32)
            dq_blocks = []
            for i in range(nsub):
                r0 = i * B_SUB
                qi, bi = qq[r0:r0 + B_SUB, :], b[r0:r0 + B_SUB, :]
                da_i = d_a[r0:r0 + B_SUB, :]
                if i == 0:
                    dq_i = jnp.zeros((B_SUB, B_DIM), F32)
                else:
                    eq, ek = _sub_scales(b, i)
                    da_m = jnp.where(lane < r0, da_i, 0.0)
                    dq_i = _dot3(da_m, kk * ek, NN) * eq
                    dk_pairs = dk_pairs + _dot3(da_m, qi * eq, TN) * ek
                for sl in range(B_SUB):
                    s = r0 + sl
                    e = jnp.exp(jnp.where(trow >= sl, bi - b[s:s + 1, :], -jnp.inf))
                    dacol = jnp.sum(jnp.where(lane == s, da_i, 0.0), axis=1, keepdims=True)
                    w = dacol * e
                    dq_i = dq_i + w * kk[s:s + 1, :]
                    dksc[s:s + 1, :] = jnp.sum(w * qi, axis=0, keepdims=True)
                dq_blocks.append(dq_i)
            dq = jnp.concatenate(dq_blocks, axis=0) + dq_inter
            dk = dk_pairs + dksc[...] + dk_inter
            dv = dv_intra + dv_inter
            db = qq * dq - kk * dk
            extra = (jnp.sum(kk * dk_inter, axis=0, keepdims=True)
                     + ebl_last * jnp.sum(st0 * dst, axis=0, keepdims=True))
            db = db + jnp.where(srow == c - 1, extra, 0.0)
            dlog = _cumsum_rows(db, c, reverse=True)
            dgate = dlog / gate - dk
            df_ref[rows, :] = (dgate * (1.0 - lbv) * sig * (1.0 - sig)).astype(BF16)
            dlb_ref[...] += jnp.sum(dgate * (1.0 - sig), axis=0, keepdims=True)
            dq_ref[rows, :] = (dq * scale * (sq * (1.0 + qb * (1.0 - sq)))).astype(BF16)
            di_ref[rows, :] = dv.astype(BF16)
            dstate[...] = dst0
            return carry

        lax.fori_loop(0, nc, chunk, 0)

    def col(off):
        return pl.BlockSpec((t, B_DIM), lambda h: (0, col0 + off * nh + h))

    hcol = pl.BlockSpec((t, B_DIM), lambda h: (0, h))
    vec = pl.BlockSpec((None, 1, B_DIM), lambda h: (h, 0, 0))
    wide = _sds((t, nh * B_DIM), BF16)
    return pl.pallas_call(
        kern, name=name, grid=(nh,),
        in_specs=[col(0), col(1), col(2), col(3), vec,
                  pl.BlockSpec((1, B_DIM), lambda h: (0, 0)), hcol,
                  pl.BlockSpec((None, nc, B_DIM, B_DIM), lambda h: (h, 0, 0, 0)),
                  pl.BlockSpec((None, nc, c, c), lambda h: (h, 0, 0, 0)),
                  pl.BlockSpec((t, B_DIM), lambda h: (0, dcol0 + h))],
        out_specs=[hcol, hcol, hcol, hcol, pl.BlockSpec((1, B_DIM), lambda h: (0, 0)), vec],
        out_shape=[wide, wide, wide, wide, _sds((1, B_DIM), F32), _sds((nh, 1, B_DIM), F32)],
        scratch_shapes=[pltpu.VMEM((B_DIM, B_DIM), F32), pltpu.VMEM((c, B_DIM), F32)],
        compiler_params=_params("arbitrary"),
    )(proj, proj, proj, proj, lb, gn, opre, states, amats, dout)


def lower_bounds_fwd(raw, name):
    n, w = raw.shape

    def kern(raw_ref, lb_ref, soft_ref):
        r = raw_ref[...]
        mx = r[0:1]
        for i in range(1, n):
            mx = jnp.maximum(mx, r[i:i + 1])
        e = jnp.exp(r - mx)
        den = e[0:1]
        for i in range(1, n):
            den = den + e[i:i + 1]
        soft = e / den
        soft_ref[...] = soft
        run = soft[0:1]
        lb_ref[0:1, :] = run - soft[0:1]
        for i in range(1, n):
            run = run + soft[i:i + 1]
            lb_ref[i:i + 1, :] = run - soft[0:1]

    return pl.pallas_call(kern, name=name, out_shape=[_sds((n, w), F32), _sds((n, w), F32)])(raw)


def lower_bounds_bwd(soft, dlb, name):
    n, w = soft.shape

    def kern(soft_ref, dlb_ref, out_ref):
        s = soft_ref[...]
        d = dlb_ref[...]
        total = d[0:1]
        for i in range(1, n):
            total = total + d[i:i + 1]
        us = []
        tail = total
        for i in range(n):
            us.append(tail - total if i == 0 else tail)
            tail = tail - d[i:i + 1]
        dot = s[0:1] * us[0]
        for i in range(1, n):
            dot = dot + s[i:i + 1] * us[i]
        for i in range(n):
            out_ref[i:i + 1, :] = s[i:i + 1] * (us[i] - dot)

    return pl.pallas_call(kern, name=name, out_shape=_sds((n, w), F32))(soft, dlb)


def _row_tile(kdim, n):
    tk = 512
    while tk > 8 and tk * n > 256 * 1024:
        tk //= 2
    return min(kdim, tk)


def _adam_update(w, g, m, v):
    m2 = ADAM_B1 * m + (1.0 - ADAM_B1) * g
    v2 = ADAM_B2 * v + (1.0 - ADAM_B2) * (g * g)
    m_hat = m2 / (1.0 - ADAM_B1 ** ADAM_STEP)
    v_hat = v2 / (1.0 - ADAM_B2 ** ADAM_STEP)
    delta = -ADAM_LR * (m_hat / (jnp.sqrt(v_hat) + ADAM_EPS) + ADAM_WD * w)
    return delta, m2, v2


def adamw_small(w, g, m, v, name):
    def kern(w_ref, g_ref, m_ref, v_ref, d_ref, m2_ref, v2_ref):
        d, m2, v2 = _adam_update(w_ref[...], g_ref[...], m_ref[...], v_ref[...])
        d_ref[...] = d
        m2_ref[...] = m2
        v2_ref[...] = v2

    return pl.pallas_call(kern, name=name, out_shape=[_sds(w.shape, F32)] * 3)(w, g, m, v)


def adamw_big(parts, w, m, v, name):
    nl, kdim, n = w.shape
    tk = _row_tile(kdim, n)

    def kern(p_ref, w_ref, m_ref, v_ref, g_ref, d_ref, m2_ref, v2_ref):
        g = p_ref[0].astype(F32)
        for q in range(1, 4):
            g = g + p_ref[q].astype(F32)
        d, m2, v2 = _adam_update(w_ref[...], g, m_ref[...], v_ref[...])
        g_ref[...] = g
        d_ref[...] = d
        m2_ref[...] = m2
        v2_ref[...] = v2

    blk = pl.BlockSpec((None, tk, n), lambda l, i: (l, i, 0))
    return pl.pallas_call(
        kern, name=name, grid=(nl, kdim // tk),
        in_specs=[pl.BlockSpec((None, 4, tk, n), lambda l, i: (l, 0, i, 0)), blk, blk, blk],
        out_specs=[blk] * 4, out_shape=[_sds(w.shape, F32)] * 4,
        compiler_params=_params("parallel", "parallel"),
    )(parts, w, m, v)


def cast_bf16(w, name):
    nl, kdim, n = w.shape
    tk = _row_tile(kdim, n)

    def kern(w_ref, o_ref):
        o_ref[...] = w_ref[...].astype(BF16)

    blk = pl.BlockSpec((None, tk, n), lambda l, i: (l, i, 0))
    return pl.pallas_call(
        kern, name=name, grid=(nl, kdim // tk), in_specs=[blk], out_specs=blk,
        out_shape=_sds(w.shape, BF16), compiler_params=_params("parallel", "parallel"),
    )(w)


def pair_add(dw, r1, core, name):
    kdim, n = dw.shape[1], dw.shape[2]
    tk = _row_tile(kdim, n)

    def kern(c_ref, a_ref, b_ref, o_ref):
        o_ref[...] = (a_ref[...].astype(F32) + b_ref[...].astype(F32)).astype(BF16)

    grid_spec = pltpu.PrefetchScalarGridSpec(
        num_scalar_prefetch=1, grid=(4, kdim // tk),
        in_specs=[pl.BlockSpec((None, tk, n), lambda p, i, c: (2 * p + c[0], i, 0)),
                  pl.BlockSpec((None, tk, n), lambda p, i, c: (p, i, 0))],
        out_specs=pl.BlockSpec((None, tk, n), lambda p, i, c: (p, i, 0)))
    return pl.pallas_call(
        kern, name=name, grid_spec=grid_spec, out_shape=_sds((4, kdim, n), BF16),
        compiler_params=_params("parallel", "parallel"),
    )(core, dw, r1)


ANY = pl.BlockSpec(memory_space=pl.ANY)


def _place():
    x, y, c = lax.axis_index("x"), lax.axis_index("y"), lax.axis_index("c")
    chips = [(1 - x, y), (x, 1 - y), (1 - x, 1 - y)]
    return x, y, c, chips


def all_gather(shards, name):
    n = len(shards)

    def kern(*refs):
        ins, outs = refs[:n], refs[n:2 * n]
        send_sems, recv_sems, local_sems = refs[2 * n:]
        x, y, c, chips = _place()
        me, sib = (x, y, c), (x, y, 1 - c)

        def copy(t, k, block, to, src=None):
            px, py, pc = block
            dst = outs[t].at[4 * px + 2 * py + pc]
            return pltpu.make_async_remote_copy(
                src_ref=dst if src is None else src, dst_ref=dst,
                send_sem=send_sems.at[7 * t + k], recv_sem=recv_sems.at[7 * t + k],
                device_id=to, device_id_type=MESH)

        mine = [pltpu.make_async_copy(ins[t], outs[t].at[4 * x + 2 * y + c], local_sems.at[t])
                for t in range(n)]
        for cp in mine:
            cp.start()
        first = []
        for t in range(n):
            first.append(copy(t, 0, me, sib, src=ins[t]))
            first += [copy(t, 1 + j, me, (*chip, c), src=ins[t]) for j, chip in enumerate(chips)]
        for cp in first:
            cp.start()
        passed = []
        for t in range(n):
            for j, chip in enumerate(chips):
                copy(t, 1 + j, (*chip, c), me).wait_recv()
                fwd = copy(t, 4 + j, (*chip, c), sib)
                fwd.start()
                passed.append(fwd)
        for t in range(n):
            copy(t, 0, sib, me).wait_recv()
            for j, chip in enumerate(chips):
                copy(t, 4 + j, (*chip, 1 - c), me).wait_recv()
        for cp in first + passed:
            cp.wait_send()
        for cp in mine:
            cp.wait()

    return pl.pallas_call(
        kern, name=name, in_specs=[ANY] * n, out_specs=[ANY] * n,
        out_shape=[_sds((N_DEV,) + s.shape, s.dtype) for s in shards],
        scratch_shapes=[pltpu.SemaphoreType.DMA((7 * n,)), pltpu.SemaphoreType.DMA((7 * n,)),
                        pltpu.SemaphoreType.DMA((n,))],
    )(*shards)


def all_reduce_small(vec, name):
    r = vec.shape[0]

    def kern(v_ref, o_ref, buf, send_sems, recv_sems):
        x, y, c, _ = _place()
        me = 4 * x + 2 * y + c
        peers = [(x, y, 1 - c), (1 - x, y, c), (x, 1 - y, c), (1 - x, 1 - y, c),
                 (1 - x, y, 1 - c), (x, 1 - y, 1 - c), (1 - x, 1 - y, 1 - c)]
        buf[me] = v_ref[...]
        copies = []
        for k, peer in enumerate(peers):
            cp = pltpu.make_async_remote_copy(
                src_ref=v_ref, dst_ref=buf.at[me], send_sem=send_sems.at[k],
                recv_sem=recv_sems.at[k], device_id=peer, device_id_type=MESH)
            cp.start()
            copies.append(cp)
        for cp in copies:
            cp.wait_recv()
        for cp in copies:
            cp.wait_send()
        total = buf[0]
        for d in range(1, N_DEV):
            total = total + buf[d]
        o_ref[...] = total

    vm = pl.BlockSpec(memory_space=pltpu.VMEM)
    return pl.pallas_call(
        kern, name=name, in_specs=[vm], out_specs=vm, out_shape=_sds(vec.shape, F32),
        scratch_shapes=[pltpu.VMEM((N_DEV, r, LANES), F32), pltpu.SemaphoreType.DMA((7,)),
                        pltpu.SemaphoreType.DMA((7,))],
    )(vec)


def exchange_with_sibling(grads, name):
    n = len(grads)

    def kern(*refs):
        ins, outs = refs[:n], refs[n:2 * n]
        send_sems, recv_sems = refs[2 * n:]
        x, y, c, _ = _place()
        copies = []
        for t in range(n):
            for p in range(4):
                cp = pltpu.make_async_remote_copy(
                    src_ref=ins[t].at[2 * p + 1 - c], dst_ref=outs[t].at[p],
                    send_sem=send_sems.at[4 * t + p], recv_sem=recv_sems.at[4 * t + p],
                    device_id=(x, y, 1 - c), device_id_type=MESH)
                cp.start()
                copies.append(cp)
        for cp in copies:
            cp.wait_recv()
        for cp in copies:
            cp.wait_send()

    return pl.pallas_call(
        kern, name=name, in_specs=[ANY] * n, out_specs=[ANY] * n,
        out_shape=[_sds((4,) + g.shape[1:], g.dtype) for g in grads],
        scratch_shapes=[pltpu.SemaphoreType.DMA((4 * n,)), pltpu.SemaphoreType.DMA((4 * n,))],
    )(*grads)


def exchange_between_chips(partials, layers, kinds, name):
    n = len(partials)
    n_kind = max(kinds) + 1
    shapes = []
    for kd in range(n_kind):
        idx = [i for i in range(n) if kinds[i] == kd]
        nl = max(layers[i] for i in idx) + 1
        shapes.append(_sds((nl,) + partials[idx[0]].shape, partials[idx[0]].dtype))

    def kern(*refs):
        ins, outs = refs[:n], refs[n:n + n_kind]
        send_sems, recv_sems, local_sems = refs[n + n_kind:]
        x, y, c, chips = _place()
        mine = 2 * x + y
        local = []
        copies = []
        for t in range(n):
            dst = outs[kinds[t]].at[layers[t], mine]
            lc = pltpu.make_async_copy(ins[t].at[mine], dst, local_sems.at[t])
            lc.start()
            local.append(lc)
            for j, (qx, qy) in enumerate(chips):
                cp = pltpu.make_async_remote_copy(
                    src_ref=ins[t].at[2 * qx + qy], dst_ref=dst,
                    send_sem=send_sems.at[3 * t + j], recv_sem=recv_sems.at[3 * t + j],
                    device_id=(qx, qy, c), device_id_type=MESH)
                cp.start()
                copies.append(cp)
        for cp in copies:
            cp.wait_recv()
        for cp in copies:
            cp.wait_send()
        for lc in local:
            lc.wait()

    return pl.pallas_call(
        kern, name=name, in_specs=[ANY] * n, out_specs=[ANY] * n_kind, out_shape=shapes,
        scratch_shapes=[pltpu.SemaphoreType.DMA((3 * n,)), pltpu.SemaphoreType.DMA((3 * n,)),
                        pltpu.SemaphoreType.DMA((n,))],
    )(*partials)


def _pack(arrays):
    flat = jnp.concatenate([a.reshape(-1).astype(F32) for a in arrays])
    pad = (-flat.shape[0]) % (8 * LANES)
    return jnp.pad(flat, (0, pad)).reshape(-1, LANES)


def _unpack(packed, shapes):
    flat = packed.reshape(-1)
    out, off = [], 0
    for s in shapes:
        n = math.prod(s)
        out.append(flat[off:off + n].reshape(s))
        off += n
    return out


def _to_heads(x2d, dil, n_heads, dh):
    t = x2d.shape[0]
    return x2d.reshape(t // dil, dil, n_heads, dh).transpose(2, 1, 0, 3).reshape(n_heads, t, dh)


def _from_heads(xh, dil):
    h, t, w = xh.shape
    return xh.reshape(h, dil, t // dil, w).transpose(2, 1, 0, 3).reshape(t, h * w)


def _unperm(xh, dil):
    h, t, w = xh.shape
    return xh.reshape(h, dil, t // dil, w).transpose(0, 2, 1, 3).reshape(h, t, w)


def _perm(xh, dil):
    h, t, w = xh.shape
    return xh.reshape(h, t // dil, dil, w).transpose(0, 2, 1, 3).reshape(h, t, w)


def local_step(x, target, norm_mix_g, norm_mlp_g, final_norm_g, lbs, hgrn_norm_g, sinks,
               bq_full, bo_full, win_g, wout_g, wqkv_g, wo_g, w1_g, w2_g):
    t, d = x.shape
    depth = norm_mix_g.shape[0]
    na = d // 2 // A_DIM
    nbh = d // 2 // B_DIM
    nq = d // C_DIM
    nkv = nq // C_GROUP
    a_w = 3 * na * A_DIM
    c_w = (nq + 2 * nkv) * C_DIM
    tabs_a = rope_tables(t, A_DIM)
    tabs_c = rope_tables(t, C_DIM)
    saved = []
    for l in range(depth):
        s = {"x_in": x}
        h = rms_fwd(x, norm_mix_g[l], "norm_mix_fwd")
        s["h"] = h
        if l % 2 == 0:
            e = l // 2
            proj = mm_cols_sharded(h, win_g, e, "even_in_proj")[0]
            qkv_r = rope_call(proj, tabs_a, a_w, 2 * na, False, "rope_a")[0]
            nums, ms, ls, hms = [], [], [], []
            for window, dil in A_BRANCHES:
                hm = _to_heads(qkv_r, dil, 3 * na, A_DIM)
                num, m, lsum = band_fwd(hm, 0, na, 2 * na, na, 1, t // dil // BLK, window // dil,
                                        f"dilated_fwd_{dil}")
                hms.append(hm)
                nums.append(_unperm(num, dil))
                ms.append(_unperm(m, dil))
                ls.append(_unperm(lsum, dil))
            oa, lse = merge_branches(nums, ms, ls, "dilated_merge")
            lb_e = lbs[e].reshape(nbh, 1, B_DIM)
            gn_e = hgrn_norm_g[e].reshape(1, B_DIM)
            ob, opre, states, amats = hgrn_fwd(proj, 3 * na, nbh, lb_e, gn_e, "hgrn_fwd")
            mixed = jnp.concatenate([_from_heads(oa, 1).astype(BF16), ob], axis=1)
            x = mm_rows_sharded(mixed, wout_g, e, "even_out_proj", [x], ["tile"], _ep_residual)
            s.update(proj=proj, hms=hms, oa=oa, lse=lse, opre=opre, states=states, amats=amats,
                     mixed=mixed, lb=lb_e, gn=gn_e)
        else:
            o = l // 2
            wq = wqkv_g[:, o].transpose(1, 0, 2).reshape(d, c_w)
            proj = mm_plain(h, wq, "odd_qkv_proj", [bq_full[o].reshape(1, c_w)], ["row"], _ep_bias)
            qkv_r = rope_call(proj, tabs_c, c_w, (nq + nkv) * C_DIM // LANES, False, "rope_c")[0]
            hm = _to_heads(qkv_r, 1, nq + 2 * nkv, C_DIM)
            sink_rows = jnp.repeat(sinks[o].reshape(nkv, C_GROUP), BLK, axis=1).reshape(
                nkv, C_GROUP * BLK, 1)
            num, m, lsum = band_fwd(hm, 0, nq, nq + nkv, nkv, C_GROUP, t // BLK, C_WINDOW - 1,
                                    "swa_fwd", sink_rows=sink_rows)
            o_hm, lse = normalise_heads(num, m, lsum, "swa_normalise")
            attn = _from_heads(o_hm, 1).astype(BF16)
            x = mm_rows_sharded(attn, wo_g, o, "odd_out_proj", [bo_full[o].reshape(1, d), x],
                                ["row", "tile"], _ep_bias_residual)
            s.update(wq=wq, hm=hm, sink_rows=sink_rows, o_hm=o_hm, lse=lse, attn=attn)
        s["x_mid"] = x
        h2 = rms_fwd(x, norm_mlp_g[l], "norm_mlp_fwd")
        u, act = mm_cols_sharded(h2, w1_g, l, "mlp_up", epilogue=_ep_relu2, n_out=2)
        x = mm_rows_sharded(act, w2_g, l, "mlp_down", [x], ["tile"], _ep_residual)
        s.update(h2=h2, u=u, act=act)
        saved.append(s)

    dx, dxb, dg_final, loss_part = loss_head(x, final_norm_g, target, "loss_head")
    big = []
    small = {"final": dg_final, "loss": loss_part, "mix": [None] * depth, "mlp": [None] * depth,
             "lb": {}, "gn": {}, "sinks": {}, "bq": {}, "bo": {}}
    for l in reversed(range(depth)):
        s = saved[l]
        big.append(("w2", l, mm_tn(s["act"], dxb, "mlp_down_dw").reshape(N_DEV, -1, d)))
        du = mm_nt_rows_sharded(dxb, w2_g, l, "mlp_down_dx", extras=[s["u"]],
                                epilogue=_ep_relu2_bwd, out_dtype=BF16)
        big.append(("w1", l, mm_tn(s["h2"], du, "mlp_up_dw", shard_cols=w1_g.shape[-1])))
        dh2 = mm_nt_cols_sharded(du, w1_g, l, "mlp_up_dx")
        dx, dxb, dg, col_dx = rms_bwd(s["x_mid"], norm_mlp_g[l], dh2, dx, "norm_mlp_bwd")
        small["mlp"][l] = dg
        if l % 2 == 0:
            e = l // 2
            big.append(("wout", e, mm_tn(s["mixed"], dxb, "even_out_dw").reshape(N_DEV, -1, d)))
            dmixed = mm_nt_rows_sharded(dxb, wout_g, e, "even_out_dx")
            do_hm = _to_heads(dmixed[:, :na * A_DIM], 1, na, A_DIM)
            delta = head_delta(s["oa"], do_hm, "dilated_delta")
            dsum = None
            for (window, dil), hm in zip(A_BRANCHES, s["hms"]):
                dq, dk, dv = band_bwd(hm, 0, na, 2 * na, _perm(do_hm, dil).astype(BF16),
                                      _perm(s["lse"], dil), _perm(delta, dil), na, 1,
                                      t // dil // BLK, window // dil, f"dilated_bwd_{dil}")
                part = _from_heads(jnp.concatenate([dq, dk, dv], axis=0), dil)
                dsum = part if dsum is None else dsum + part
            dqkv_a = rope_call(dsum, tabs_a, a_w, 2 * na, True, "rope_a_bwd")[0]
            dqb, dfb, dib, dgb, dgn, dlb = hgrn_bwd(s["proj"], 3 * na, nbh, s["lb"], s["gn"],
                                                    s["opre"], s["states"], s["amats"], dmixed, na,
                                                    "hgrn_bwd")
            small["gn"][e] = dgn
            small["lb"][e] = dlb
            dproj = jnp.concatenate([dqkv_a, dqb, dfb, dib, dgb], axis=1)
            big.append(("win", e, mm_tn(s["h"], dproj, "even_in_dw", shard_cols=win_g.shape[-1])))
            dh = mm_nt_cols_sharded(dproj, win_g, e, "even_in_dx")
        else:
            o = l // 2
            small["bo"][o] = col_dx
            big.append(("wo", o, mm_tn(s["attn"], dxb, "odd_out_dw").reshape(N_DEV, -1, d)))
            dattn = mm_nt_rows_sharded(dxb, wo_g, o, "odd_out_dx")
            do_hm = _to_heads(dattn, 1, nq, C_DIM)
            delta = head_delta(s["o_hm"], do_hm, "swa_delta")
            dq, dk, dv, dsink = band_bwd(s["hm"], 0, nq, nq + nkv, do_hm.astype(BF16), s["lse"],
                                         delta, nkv, C_GROUP, t // BLK, C_WINDOW - 1, "swa_bwd",
                                         sink_rows=s["sink_rows"])
            small["sinks"][o] = dsink
            dqkv = _from_heads(jnp.concatenate([dq, dk, dv], axis=0), 1)
            dproj, dbq = rope_call(dqkv, tabs_c, c_w, (nq + nkv) * C_DIM // LANES, True,
                                   "rope_c_bwd", col_sum=True)
            small["bq"][o] = dbq
            dwq = mm_tn(s["h"], dproj, "odd_qkv_dw", tn=512)
            big.append(("wqkv", o, dwq.reshape(d, N_DEV, -1).transpose(1, 0, 2)))
            dh = mm_nt_plain(dproj, s["wq"], "odd_qkv_dx", tk=c_w)
        dx, dxb, dg, _ = rms_bwd(s["x_in"], norm_mix_g[l], dh, dx, "norm_mix_bwd")
        small["mix"][l] = dg
    return dx, big, small


def kernel(x, norm_mix_g, norm_mlp_g, final_norm_g, even_w_in, even_w_out, hgrn_lb_raw, hgrn_norm_g, odd_w_qkv, odd_b_qkv, odd_sinks, odd_w_o, odd_b_o, mlp_w1, mlp_w2, loss_target, m_norm_mix_g, m_norm_mlp_g, m_final_norm_g, m_even_w_in, m_even_w_out, m_hgrn_lb_raw, m_hgrn_norm_g, m_odd_w_qkv, m_odd_b_qkv, m_odd_sinks, m_odd_w_o, m_odd_b_o, m_mlp_w1, m_mlp_w2, v_norm_mix_g, v_norm_mlp_g, v_final_norm_g, v_even_w_in, v_even_w_out, v_hgrn_lb_raw, v_hgrn_norm_g, v_odd_w_qkv, v_odd_b_qkv, v_odd_sinks, v_odd_w_o, v_odd_b_o, v_mlp_w1, v_mlp_w2):
    d = x.shape[2]
    depth = norm_mix_g.shape[0]
    n_even, n_odd = even_w_in.shape[0], odd_w_qkv.shape[0]
    xi, yi, ci = lax.axis_index("x"), lax.axis_index("y"), lax.axis_index("c")
    dev = 4 * xi + 2 * yi + ci
    core = ci.astype(jnp.int32).reshape(1)

    big_w = {"win": even_w_in, "wout": even_w_out, "wqkv": odd_w_qkv, "wo": odd_w_o,
             "w1": mlp_w1, "w2": mlp_w2}
    big_m = {"win": m_even_w_in, "wout": m_even_w_out, "wqkv": m_odd_w_qkv, "wo": m_odd_w_o,
             "w1": m_mlp_w1, "w2": m_mlp_w2}
    big_v = {"win": v_even_w_in, "wout": v_even_w_out, "wqkv": v_odd_w_qkv, "wo": v_odd_w_o,
             "w1": v_mlp_w1, "w2": v_mlp_w2}
    kinds = list(big_w)
    gathered = all_gather([cast_bf16(big_w[k], f"cast_{k}") for k in kinds], "gather_weights")
    wg = dict(zip(kinds, gathered))

    bq_w, bo_w = odd_b_qkv.shape[1], odd_b_o.shape[1]
    bq_mine = lax.dynamic_update_slice(jnp.zeros((n_odd, N_DEV * bq_w), F32), odd_b_qkv,
                                       (0, dev * bq_w))
    bo_mine = lax.dynamic_update_slice(jnp.zeros((n_odd, N_DEV * bo_w), F32), odd_b_o,
                                       (0, dev * bo_w))
    bq_full, bo_full = _unpack(all_reduce_small(_pack([bq_mine, bo_mine]), "gather_biases"),
                               [bq_mine.shape, bo_mine.shape])

    lbs, soft = lower_bounds_fwd(hgrn_lb_raw, "lower_bounds")

    dx, big, small = local_step(x[0], loss_target[0], norm_mix_g, norm_mlp_g, final_norm_g, lbs,
                                hgrn_norm_g, odd_sinks, bq_full, bo_full, wg["win"], wg["wout"],
                                wg["wqkv"], wg["wo"], wg["w1"], wg["w2"])

    parts = ([small["mix"][l] for l in range(depth)] + [small["mlp"][l] for l in range(depth)]
             + [small["final"]] + [small["lb"][e] for e in range(n_even)]
             + [small["gn"][e] for e in range(n_even)] + [small["sinks"][o] for o in range(n_odd)]
             + [small["bq"][o] for o in range(n_odd)] + [small["bo"][o] for o in range(n_odd)]
             + [small["loss"]])
    shapes = ([(depth, d)] * 2 + [(d,), hgrn_lb_raw.shape, hgrn_norm_g.shape, odd_sinks.shape,
              (n_odd, N_DEV * bq_w), (n_odd, N_DEV * bo_w), (1, LANES)])
    g_mix, g_mlp, g_final, d_lbs, g_gn, g_sinks, g_bq_full, g_bo_full, loss_v = _unpack(
        all_reduce_small(_pack(parts), "reduce_small"), shapes)
    g_lb = lower_bounds_bwd(soft, d_lbs, "lower_bounds_bwd")
    g_bq = lax.dynamic_slice(g_bq_full, (0, dev * bq_w), (n_odd, bq_w))
    g_bo = lax.dynamic_slice(g_bo_full, (0, dev * bo_w), (n_odd, bo_w))
    loss = loss_v[0, 0]

    small_names = ["norm_mix_g", "norm_mlp_g", "final_norm_g", "hgrn_lb_raw", "hgrn_norm_g",
                   "odd_b_qkv", "odd_sinks", "odd_b_o"]
    small_w = [norm_mix_g, norm_mlp_g, final_norm_g, hgrn_lb_raw, hgrn_norm_g, odd_b_qkv,
               odd_sinks, odd_b_o]
    small_m = [m_norm_mix_g, m_norm_mlp_g, m_final_norm_g, m_hgrn_lb_raw, m_hgrn_norm_g,
               m_odd_b_qkv, m_odd_sinks, m_odd_b_o]
    small_v = [v_norm_mix_g, v_norm_mlp_g, v_final_norm_g, v_hgrn_lb_raw, v_hgrn_norm_g,
               v_odd_b_qkv, v_odd_sinks, v_odd_b_o]
    small_g = [g_mix, g_mlp, g_final, g_lb, g_gn, g_bq, g_sinks, g_bo]
    sshapes = [w.shape for w in small_w]
    sd, sm, sv = adamw_small(_pack(small_w), _pack(small_g), _pack(small_m), _pack(small_v),
                             "adamw_small")
    res = {}
    for name, g, dl, m2, v2 in zip(small_names, small_g, _unpack(sd, sshapes),
                                   _unpack(sm, sshapes), _unpack(sv, sshapes)):
        res[name] = (g.reshape(dl.shape), dl, m2, v2)

    grads = [g for _, _, g in big]
    received = exchange_with_sibling(grads, "scatter_grads_d2d")
    partials = [pair_add(g, r, core, f"pair_add_{k}")
                for (k, _, _), g, r in zip(big, grads, received)]
    stacked = exchange_between_chips(partials, [l for _, l, _ in big],
                                     [kinds.index(k) for k, _, _ in big], "scatter_grads_ici")
    long_names = {"win": "even_w_in", "wout": "even_w_out", "wqkv": "odd_w_qkv", "wo": "odd_w_o",
                  "w1": "mlp_w1", "w2": "mlp_w2"}
    for k, parts4 in zip(kinds, stacked):
        res[long_names[k]] = tuple(adamw_big(parts4, big_w[k], big_m[k], big_v[k], f"adamw_{k}"))

    order = ["norm_mix_g", "norm_mlp_g", "final_norm_g", "even_w_in", "even_w_out", "hgrn_lb_raw",
             "hgrn_norm_g", "odd_w_qkv", "odd_b_qkv", "odd_sinks", "odd_w_o", "odd_b_o", "mlp_w1",
             "mlp_w2"]
    outs = [loss, dx[None]]
    for j in range(4):
        outs += [res[n][j] for n in order]
    return tuple(outs)
```

```python
import functools
import math

import jax
import jax.numpy as jnp
from jax import lax
from jax.experimental import pallas as pl
from jax.experimental.pallas import tpu as pltpu

F32 = jnp.float32
BF16 = jnp.bfloat16
MESH = pl.DeviceIdType.MESH

N_DEV = 8
NORM_EPS = 1e-5
ROPE_THETA = 500000.0
BLK = 128
A_DIM = 128
A_BRANCHES = ((128, 1), (512, 4), (2048, 16))
B_DIM = 128
B_CHUNK = 64
C_DIM = 64
C_GROUP = 8
C_WINDOW = 128
LANES = 128

ADAM_LR = 0.001
ADAM_B1 = 0.9
ADAM_B2 = 0.999
ADAM_EPS = 1e-08
ADAM_WD = 0.01
ADAM_STEP = 10

NN = (((1,), (0,)), ((), ()))
NT = (((1,), (1,)), ((), ()))
TN = (((0,), (0,)), ((), ()))


def _params(*sem):
    return pltpu.CompilerParams(dimension_semantics=sem)


def _sigmoid(x):
    return 1.0 / (1.0 + jnp.exp(-x))


def _rows_call(name, body, row_ins, full_ins, row_outs, acc_outs, tm):
    t = row_ins[0].shape[0]
    n_ri, n_fi, n_ro = len(row_ins), len(full_ins), len(row_outs)

    def kern(*refs):
        i = pl.program_id(0)
        body(i, refs[:n_ri], refs[n_ri:n_ri + n_fi],
             refs[n_ri + n_fi:n_ri + n_fi + n_ro], refs[n_ri + n_fi + n_ro:])

    def row_spec(shape):
        return pl.BlockSpec((tm,) + tuple(shape[1:]), lambda i: (i,) + (0,) * (len(shape) - 1))

    def full_spec(shape):
        return pl.BlockSpec(tuple(shape), lambda i: (0,) * len(shape))

    outs = pl.pallas_call(
        kern, name=name, grid=(t // tm,),
        in_specs=[row_spec(a.shape) for a in row_ins] + [full_spec(a.shape) for a in full_ins],
        out_specs=[row_spec(s.shape) for s in row_outs] + [full_spec(s.shape) for s in acc_outs],
        out_shape=list(row_outs) + list(acc_outs),
        compiler_params=_params("arbitrary" if acc_outs else "parallel"),
    )(*row_ins, *full_ins)
    return outs


def _sds(shape, dtype):
    return jax.ShapeDtypeStruct(tuple(shape), dtype)


def rms_fwd(x, g, name):
    t, d = x.shape

    def body(i, ri, fi, ro, ao):
        xv = ri[0][...]
        r = lax.rsqrt(jnp.mean(xv * xv, axis=-1, keepdims=True) + NORM_EPS)
        ro[0][...] = (xv * r * fi[0][...]).astype(BF16)

    return _rows_call(name, body, [x], [g.reshape(1, d)], [_sds((t, d), BF16)], [], 256)[0]


def rms_bwd(x, g, dh, dx_res, name):
    t, d = x.shape

    def body(i, ri, fi, ro, ao):
        xv, dhv, res = ri[0][...], ri[1][...], ri[2][...]
        gv = fi[0][...]
        r = lax.rsqrt(jnp.mean(xv * xv, axis=-1, keepdims=True) + NORM_EPS)
        gd = gv * dhv
        dx = res + r * gd - xv * (r * r * r) * jnp.mean(xv * gd, axis=-1, keepdims=True)
        ro[0][...] = dx
        ro[1][...] = dx.astype(BF16)

        @pl.when(i == 0)
        def _():
            ao[0][...] = jnp.zeros_like(ao[0])
            ao[1][...] = jnp.zeros_like(ao[1])

        ao[0][...] += jnp.sum(dhv * xv * r, axis=0, keepdims=True)
        ao[1][...] += jnp.sum(dx, axis=0, keepdims=True)

    return _rows_call(name, body, [x, dh, dx_res], [g.reshape(1, d)],
                      [_sds((t, d), F32), _sds((t, d), BF16)],
                      [_sds((1, d), F32), _sds((1, d), F32)], 256)


def loss_head(x, g, target, name):
    t, d = x.shape

    def body(i, ri, fi, ro, ao):
        xv, tg = ri[0][...], ri[1][...]
        gv = fi[0][...]
        r = lax.rsqrt(jnp.mean(xv * xv, axis=-1, keepdims=True) + NORM_EPS)
        e = xv * r * gv - tg
        dy = e * (1.0 / d)
        gd = gv * dy
        dx = r * gd - xv * (r * r * r) * jnp.mean(xv * gd, axis=-1, keepdims=True)
        ro[0][...] = dx
        ro[1][...] = dx.astype(BF16)

        @pl.when(i == 0)
        def _():
            ao[0][...] = jnp.zeros_like(ao[0])
            ao[1][...] = jnp.zeros_like(ao[1])

        ao[0][...] += jnp.sum(dy * xv * r, axis=0, keepdims=True)
        part = 0.5 * jnp.sum(jnp.mean(e * e, axis=-1, keepdims=True), axis=0, keepdims=True)
        ao[1][...] += jnp.broadcast_to(part, (1, LANES))

    return _rows_call(name, body, [x, target], [g.reshape(1, d)],
                      [_sds((t, d), F32), _sds((t, d), BF16)],
                      [_sds((1, d), F32), _sds((1, LANES), F32)], 256)


def rope_tables(seq, head_dim):
    rot = head_dim // 4
    half = rot // 2
    inv_freq = 1.0 / (ROPE_THETA ** (jnp.arange(0, rot, 2, dtype=F32) / rot))
    ang = jnp.arange(seq, dtype=F32)[:, None] * inv_freq[None, :]
    cos, sin = jnp.cos(ang), jnp.sin(ang)
    zeros = jnp.zeros((seq, head_dim - rot), F32)
    zh = jnp.zeros((seq, half), F32)
    c = jnp.concatenate([cos, cos, jnp.ones((seq, head_dim - rot), F32)], axis=-1)
    sp = jnp.concatenate([zh, sin, zeros], axis=-1)
    sm = jnp.concatenate([-sin, zh, zeros], axis=-1)
    rep = LANES // head_dim
    return jnp.tile(c, (1, rep)), jnp.tile(sp, (1, rep)), jnp.tile(sm, (1, rep)), half


def rope_call(x, tabs, width, n_rope, inverse, name, col_sum=False):
    c, sp, sm, half = tabs
    t = x.shape[0]
    tm = 256
    n_slab = width // LANES

    def kern(x_ref, c_ref, sp_ref, sm_ref, o_ref, *acc):
        cv, spv, smv = c_ref[...], sp_ref[...], sm_ref[...]
        for j in range(n_slab):
            xs = x_ref[:, j * LANES:(j + 1) * LANES].astype(F32)
            if j < n_rope:
                if inverse:
                    ys = (xs * cv + pltpu.roll(xs * spv, LANES - half, 1)
                          + pltpu.roll(xs * smv, half, 1))
                else:
                    ys = (xs * cv + pltpu.roll(xs, half, 1) * spv
                          + pltpu.roll(xs, LANES - half, 1) * smv)
            else:
                ys = xs
            o_ref[:, j * LANES:(j + 1) * LANES] = ys.astype(BF16)
            if col_sum:
                @pl.when(pl.program_id(0) == 0)
                def _():
                    acc[0][:, j * LANES:(j + 1) * LANES] = jnp.zeros((1, LANES), F32)
                acc[0][:, j * LANES:(j + 1) * LANES] += jnp.sum(ys, axis=0, keepdims=True)

    tab_spec = pl.BlockSpec((tm, LANES), lambda i: (i, 0))
    out_shape = [_sds((t, width), BF16)]
    out_specs = [pl.BlockSpec((tm, width), lambda i: (i, 0))]
    if col_sum:
        out_shape.append(_sds((1, width), F32))
        out_specs.append(pl.BlockSpec((1, width), lambda i: (0, 0)))
    return pl.pallas_call(
        kern, name=name, grid=(t // tm,),
        in_specs=[pl.BlockSpec((tm, width), lambda i: (i, 0)), tab_spec, tab_spec, tab_spec],
        out_specs=out_specs, out_shape=out_shape,
        compiler_params=_params("arbitrary" if col_sum else "parallel"),
    )(x, c, sp, sm)


def _mm_call(name, a, b, extras, out_shapes, grid, a_spec, b_spec, extra_specs, out_specs,
             acc_shape, dims, epilogue):
    n_ex, n_out = len(extras), len(out_shapes)
    nk = grid[2]

    def product(a_ref, b_ref):
        bv = b_ref[...]
        if bv.ndim == 3:
            bv = bv.reshape(bv.shape[0] * bv.shape[1], bv.shape[2])
        return lax.dot_general(a_ref[...].astype(BF16), bv.astype(BF16), dims,
                               preferred_element_type=F32)

    def kern(*refs):
        a_ref, b_ref = refs[0], refs[1]
        ex = refs[2:2 + n_ex]
        outs = refs[2 + n_ex:2 + n_ex + n_out]
        if nk == 1:
            epilogue(product(a_ref, b_ref), ex, outs)
            return
        acc = refs[-1]
        k = pl.program_id(2)

        @pl.when(k == 0)
        def _():
            acc[...] = product(a_ref, b_ref)

        @pl.when(k > 0)
        def _():
            acc[...] += product(a_ref, b_ref)

        @pl.when(k == nk - 1)
        def _():
            epilogue(acc[...], ex, outs)

    return pl.pallas_call(
        kern, name=name, grid=grid,
        in_specs=[a_spec, b_spec, *extra_specs], out_specs=out_specs, out_shape=out_shapes,
        scratch_shapes=[pltpu.VMEM(acc_shape, F32)] if nk > 1 else [],
        compiler_params=_params("parallel", "parallel", "arbitrary"),
    )(a, b, *extras)


def _ep_store(dtype):
    def ep(acc, ex, outs):
        outs[0][...] = acc.astype(dtype)
    return ep


def _ep_residual(acc, ex, outs):
    outs[0][...] = acc + ex[0][...]


def _ep_bias(acc, ex, outs):
    outs[0][...] = acc + ex[0][...]


def _ep_bias_residual(acc, ex, outs):
    outs[0][...] = acc + ex[0][...] + ex[1][...]


def _ep_relu2(acc, ex, outs):
    outs[0][...] = acc
    rl = jnp.maximum(acc, 0.0)
    outs[1][...] = (rl * rl).astype(BF16)


def _ep_relu2_bwd(acc, ex, outs):
    outs[0][...] = (acc * (2.0 * jnp.maximum(ex[0][...], 0.0))).astype(BF16)


MM_TM = 1024
MM_TN = 1024
MM_TK = 2048


def mm_cols_sharded(a, wg, layer, name, epilogue=None, n_out=1):
    m, kdim = a.shape
    n = wg.shape[-1]
    tm, tk = min(m, MM_TM), min(kdim, MM_TK)
    if epilogue is None:
        epilogue, outs = _ep_store(F32), [_sds((m, N_DEV * n), F32)]
    else:
        outs = [_sds((m, N_DEV * n), F32), _sds((m, N_DEV * n), BF16)][:n_out]
    return _mm_call(
        name, a, wg, [], outs, (m // tm, N_DEV, kdim // tk),
        pl.BlockSpec((tm, tk), lambda i, j, k: (i, k)),
        pl.BlockSpec((None, None, tk, n), lambda i, j, k: (j, layer, k, 0)),
        [], [pl.BlockSpec((tm, n), lambda i, j, k: (i, j))] * len(outs),
        (tm, n), NN, epilogue)


def _extra_specs(extra_kinds, tm, tn):
    specs = []
    for kind in extra_kinds:
        if kind == "row":
            specs.append(pl.BlockSpec((1, tn), lambda i, j, k: (0, j)))
        else:
            specs.append(pl.BlockSpec((tm, tn), lambda i, j, k: (i, j)))
    return specs


def mm_rows_sharded(a, wg, layer, name, extras, extra_kinds, epilogue):
    m, kdim = a.shape
    ks, n = wg.shape[-2], wg.shape[-1]
    tm, tn = min(m, MM_TM), min(n, MM_TN)
    gps = max(1, min(kdim, MM_TK) // ks)
    return _mm_call(
        name, a, wg, extras, [_sds((m, n), F32)], (m // tm, n // tn, N_DEV // gps),
        pl.BlockSpec((tm, gps * ks), lambda i, j, k: (i, k)),
        pl.BlockSpec((gps, None, ks, tn), lambda i, j, k: (k, layer, 0, j)),
        _extra_specs(extra_kinds, tm, tn), [pl.BlockSpec((tm, tn), lambda i, j, k: (i, j))],
        (tm, tn), NN, epilogue)[0]


def mm_plain(a, w, name, extras, extra_kinds, epilogue, tn=512):
    m, kdim = a.shape
    n = w.shape[1]
    tm, tk = min(m, MM_TM), min(kdim, MM_TK)
    return _mm_call(
        name, a, w, extras, [_sds((m, n), F32)], (m // tm, n // tn, kdim // tk),
        pl.BlockSpec((tm, tk), lambda i, j, k: (i, k)),
        pl.BlockSpec((tk, tn), lambda i, j, k: (k, j)),
        _extra_specs(extra_kinds, tm, tn), [pl.BlockSpec((tm, tn), lambda i, j, k: (i, j))],
        (tm, tn), NN, epilogue)[0]


def mm_nt_cols_sharded(dy, wg, layer, name):
    m = dy.shape[0]
    kdim, n = wg.shape[-2], wg.shape[-1]
    tm, tn = min(m, MM_TM), min(kdim, MM_TN)
    return _mm_call(
        name, dy, wg, [], [_sds((m, kdim), F32)], (m // tm, kdim // tn, N_DEV),
        pl.BlockSpec((tm, n), lambda i, j, k: (i, k)),
        pl.BlockSpec((None, None, tn, n), lambda i, j, k: (k, layer, j, 0)),
        [], [pl.BlockSpec((tm, tn), lambda i, j, k: (i, j))],
        (tm, tn), NT, _ep_store(F32))[0]


def mm_nt_rows_sharded(dy, wg, layer, name, extras=(), epilogue=None, out_dtype=F32):
    m, n = dy.shape
    ks = wg.shape[-2]
    tm, tk = min(m, MM_TM), min(n, MM_TK)
    gps = max(1, MM_TN // ks)
    tn = gps * ks
    epilogue = _ep_store(out_dtype) if epilogue is None else epilogue
    return _mm_call(
        name, dy, wg, list(extras), [_sds((m, N_DEV * ks), out_dtype)],
        (m // tm, N_DEV // gps, n // tk),
        pl.BlockSpec((tm, tk), lambda i, j, k: (i, k)),
        pl.BlockSpec((gps, None, ks, tk), lambda i, j, k: (j, layer, 0, k)),
        [pl.BlockSpec((tm, tn), lambda i, j, k: (i, j))] * len(extras),
        [pl.BlockSpec((tm, tn), lambda i, j, k: (i, j))],
        (tm, tn), NT, epilogue)[0]


def mm_nt_plain(dy, w, name, tk):
    m, n = dy.shape
    kdim = w.shape[0]
    tm, tn = min(m, MM_TM), min(kdim, MM_TN)
    return _mm_call(
        name, dy, w, [], [_sds((m, kdim), F32)], (m // tm, kdim // tn, n // tk),
        pl.BlockSpec((tm, tk), lambda i, j, k: (i, k)),
        pl.BlockSpec((tn, tk), lambda i, j, k: (j, k)),
        [], [pl.BlockSpec((tm, tn), lambda i, j, k: (i, j))],
        (tm, tn), NT, _ep_store(F32))[0]


def mm_tn(a, dy, name, shard_cols=None, tn=MM_TN):
    t, kdim = a.shape
    n = dy.shape[1]
    tm, tk = min(kdim, MM_TM), min(t, MM_TK)
    if shard_cols is None:
        tn = min(tn, n)
        out = _sds((kdim, n), BF16)
        o_spec = pl.BlockSpec((tm, tn), lambda i, j, k: (i, j))
    else:
        tn = shard_cols
        out = _sds((n // tn, kdim, tn), BF16)
        o_spec = pl.BlockSpec((None, tm, tn), lambda i, j, k: (j, i, 0))
    return _mm_call(
        name, a, dy, [], [out], (kdim // tm, n // tn, t // tk),
        pl.BlockSpec((tk, tm), lambda i, j, k: (k, i)),
        pl.BlockSpec((tk, tn), lambda i, j, k: (k, j)),
        [], [o_spec], (tm, tn), TN, _ep_store(BF16))[0]


def _band_mask(g, nk_prev_valid, max_dist):
    rows = lax.broadcasted_iota(jnp.int32, (g * BLK, 2 * BLK), 0) % BLK
    cols = lax.broadcasted_iota(jnp.int32, (g * BLK, 2 * BLK), 1)
    dist = rows + BLK - cols
    ok = (dist >= 0) & (dist <= max_dist)
    return ok & ((cols >= BLK) | nk_prev_valid)


def band_fwd(qkv, q0, k0, v0, hk, g, seg, max_dist, name, sink_rows=None):
    t, dh = qkv.shape[1], qkv.shape[2]
    nb = t // BLK
    scale = dh ** -0.5
    has_sink = sink_rows is not None

    def kern(*refs):
        if has_sink:
            q_ref, k_ref, v_ref, s_ref, num_ref, m_ref, l_ref = refs
            sink = s_ref[...]
        else:
            q_ref, k_ref, v_ref, num_ref, m_ref, l_ref = refs
        b = pl.program_id(1)
        cur = pl.multiple_of(b * BLK, BLK)
        prev = pl.multiple_of(jnp.maximum(b - 1, 0) * BLK, BLK)
        q = q_ref[...].reshape(g * BLK, dh)
        kk = jnp.concatenate([k_ref[pl.ds(prev, BLK), :], k_ref[pl.ds(cur, BLK), :]], axis=0)
        vv = jnp.concatenate([v_ref[pl.ds(prev, BLK), :], v_ref[pl.ds(cur, BLK), :]], axis=0)
        s = lax.dot_general(q, kk, NT, preferred_element_type=F32) * scale
        s = jnp.where(_band_mask(g, (b % seg) != 0, max_dist), s, -jnp.inf)
        m = jnp.max(s, axis=-1, keepdims=True)
        if has_sink:
            m = jnp.maximum(m, sink)
        p = jnp.exp(s - m)
        l = jnp.sum(p, axis=-1, keepdims=True)
        if has_sink:
            l = l + jnp.exp(sink - m)
        num = jnp.dot(p.astype(BF16), vv, preferred_element_type=F32)
        num_ref[...] = num.reshape(g, BLK, dh)
        m_ref[...] = m.reshape(g, BLK, 1)
        l_ref[...] = l.reshape(g, BLK, 1)

    in_specs = [pl.BlockSpec((g, BLK, dh), lambda h, b: (q0 // g + h, b, 0)),
                pl.BlockSpec((None, t, dh), lambda h, b: (k0 + h, 0, 0)),
                pl.BlockSpec((None, t, dh), lambda h, b: (v0 + h, 0, 0))]
    args = [qkv, qkv, qkv]
    if has_sink:
        in_specs.append(pl.BlockSpec((None, g * BLK, 1), lambda h, b: (h, 0, 0)))
        args.append(sink_rows)
    hq = hk * g
    return pl.pallas_call(
        kern, name=name, grid=(hk, nb), in_specs=in_specs,
        out_specs=[pl.BlockSpec((g, BLK, dh), lambda h, b: (h, b, 0)),
                   pl.BlockSpec((g, BLK, 1), lambda h, b: (h, b, 0)),
                   pl.BlockSpec((g, BLK, 1), lambda h, b: (h, b, 0))],
        out_shape=[_sds((hq, t, dh), F32), _sds((hq, t, 1), F32), _sds((hq, t, 1), F32)],
        compiler_params=_params("parallel", "parallel"),
    )(*args)


def band_bwd(qkv, q0, k0, v0, do, lse, delta, hk, g, seg, max_dist, name, sink_rows=None):
    t, dh = qkv.shape[1], qkv.shape[2]
    nb = t // BLK
    scale = dh ** -0.5
    has_sink = sink_rows is not None

    def kern(*refs):
        if has_sink:
            (q_ref, k_ref, v_ref, do_ref, lse_ref, dl_ref, s_ref,
             dq_ref, dk_ref, dv_ref, ds_ref, sacc) = refs
            sink = s_ref[...]
        else:
            q_ref, k_ref, v_ref, do_ref, lse_ref, dl_ref, dq_ref, dk_ref, dv_ref = refs
        b = pl.program_id(1)

        @pl.when(b == 0)
        def _():
            dk_ref[...] = jnp.zeros_like(dk_ref)
            dv_ref[...] = jnp.zeros_like(dv_ref)
            if has_sink:
                sacc[...] = jnp.zeros_like(sacc)

        cur = pl.multiple_of(b * BLK, BLK)
        prev = pl.multiple_of(jnp.maximum(b - 1, 0) * BLK, BLK)
        q = q_ref[...].reshape(g * BLK, dh)
        dout = do_ref[...].reshape(g * BLK, dh)
        lse_b = lse_ref[...].reshape(g * BLK, 1)
        dl_b = dl_ref[...].reshape(g * BLK, 1)
        kk = jnp.concatenate([k_ref[pl.ds(prev, BLK), :], k_ref[pl.ds(cur, BLK), :]], axis=0)
        vv = jnp.concatenate([v_ref[pl.ds(prev, BLK), :], v_ref[pl.ds(cur, BLK), :]], axis=0)
        s = lax.dot_general(q, kk, NT, preferred_element_type=F32) * scale
        s = jnp.where(_band_mask(g, (b % seg) != 0, max_dist), s, -jnp.inf)
        p = jnp.exp(s - lse_b)
        dp = lax.dot_general(dout, vv, NT, preferred_element_type=F32)
        ds = (p * (dp - dl_b) * scale).astype(BF16)
        dq = jnp.dot(ds, kk, preferred_element_type=F32)
        dq_ref[...] = dq.reshape(g, BLK, dh)
        dkk = lax.dot_general(ds, q, TN, preferred_element_type=F32)
        dvv = lax.dot_general(p.astype(BF16), dout, TN, preferred_element_type=F32)
        dk_ref[pl.ds(prev, BLK), :] += dkk[:BLK]
        dk_ref[pl.ds(cur, BLK), :] += dkk[BLK:]
        dv_ref[pl.ds(prev, BLK), :] += dvv[:BLK]
        dv_ref[pl.ds(cur, BLK), :] += dvv[BLK:]
        if has_sink:
            sacc[...] += -jnp.exp(sink - lse_b) * dl_b

            @pl.when(b == nb - 1)
            def _():
                for gi in range(g):
                    ds_ref[gi:gi + 1, :] = jnp.sum(sacc[gi * BLK:(gi + 1) * BLK, :], axis=0,
                                                   keepdims=True)

    in_specs = [pl.BlockSpec((g, BLK, dh), lambda h, b: (q0 // g + h, b, 0)),
                pl.BlockSpec((None, t, dh), lambda h, b: (k0 + h, 0, 0)),
                pl.BlockSpec((None, t, dh), lambda h, b: (v0 + h, 0, 0)),
                pl.BlockSpec((g, BLK, dh), lambda h, b: (h, b, 0)),
                pl.BlockSpec((g, BLK, 1), lambda h, b: (h, b, 0)),
                pl.BlockSpec((g, BLK, 1), lambda h, b: (h, b, 0))]
    args = [qkv, qkv, qkv, do, lse, delta]
    hq = hk * g
    out_specs = [pl.BlockSpec((g, BLK, dh), lambda h, b: (h, b, 0)),
                 pl.BlockSpec((None, t, dh), lambda h, b: (h, 0, 0)),
                 pl.BlockSpec((None, t, dh), lambda h, b: (h, 0, 0))]
    out_shape = [_sds((hq, t, dh), F32), _sds((hk, t, dh), F32), _sds((hk, t, dh), F32)]
    scratch = []
    if has_sink:
        in_specs.append(pl.BlockSpec((None, g * BLK, 1), lambda h, b: (h, 0, 0)))
        args.append(sink_rows)
        out_specs.append(pl.BlockSpec((None, g, 1), lambda h, b: (h, 0, 0)))
        out_shape.append(_sds((hk, g, 1), F32))
        scratch.append(pltpu.VMEM((g * BLK, 1), F32))
    return pl.pallas_call(
        kern, name=name, grid=(hk, nb), in_specs=in_specs, out_specs=out_specs, out_shape=out_shape,
        scratch_shapes=scratch, compiler_params=_params("parallel", "arbitrary"),
    )(*args)


def merge_branches(nums, ms, ls, name):
    h, t, dh = nums[0].shape
    nbr = len(nums)

    def kern(*refs):
        num_refs, m_refs, l_refs = refs[:nbr], refs[nbr:2 * nbr], refs[2 * nbr:3 * nbr]
        o_ref, lse_ref = refs[3 * nbr], refs[3 * nbr + 1]
        mall = m_refs[0][...]
        for i in range(1, nbr):
            mall = jnp.maximum(mall, m_refs[i][...])
        num = jnp.zeros((t, dh), F32)
        den = jnp.zeros((t, 1), F32)
        for i in range(nbr):
            w = jnp.exp(m_refs[i][...] - mall)
            num = num + w * num_refs[i][...]
            den = den + w * l_refs[i][...]
        o_ref[...] = num / den
        lse_ref[...] = mall + jnp.log(den)

    big = pl.BlockSpec((None, t, dh), lambda i: (i, 0, 0))
    col = pl.BlockSpec((None, t, 1), lambda i: (i, 0, 0))
    return pl.pallas_call(
        kern, name=name, grid=(h,), in_specs=[big] * nbr + [col] * (2 * nbr),
        out_specs=[big, col], out_shape=[_sds((h, t, dh), F32), _sds((h, t, 1), F32)],
        compiler_params=_params("parallel"),
    )(*nums, *ms, *ls)


def normalise_heads(num, m, l, name):
    h, t, dh = num.shape

    def kern(num_ref, m_ref, l_ref, o_ref, lse_ref):
        lv = l_ref[...]
        o_ref[...] = num_ref[...] / lv
        lse_ref[...] = m_ref[...] + jnp.log(lv)

    big = pl.BlockSpec((None, t, dh), lambda i: (i, 0, 0))
    col = pl.BlockSpec((None, t, 1), lambda i: (i, 0, 0))
    return pl.pallas_call(
        kern, name=name, grid=(h,), in_specs=[big, col, col], out_specs=[big, col],
        out_shape=[_sds((h, t, dh), F32), _sds((h, t, 1), F32)],
        compiler_params=_params("parallel"),
    )(num, m, l)


def head_delta(o, do, name):
    h, t, dh = o.shape

    def kern(o_ref, do_ref, d_ref):
        d_ref[...] = jnp.sum(o_ref[...] * do_ref[...], axis=-1, keepdims=True)

    big = pl.BlockSpec((None, t, dh), lambda i: (i, 0, 0))
    return pl.pallas_call(
        kern, name=name, grid=(h,), in_specs=[big, big],
        out_specs=pl.BlockSpec((None, t, 1), lambda i: (i, 0, 0)),
        out_shape=_sds((h, t, 1), F32), compiler_params=_params("parallel"),
    )(o, do)


def _cumsum_rows(x, n, reverse=False):
    rows = lax.broadcasted_iota(jnp.int32, x.shape, 0)
    shift = 1
    while shift < n:
        if reverse:
            x = x + jnp.where(rows < n - shift, pltpu.roll(x, n - shift, 0), 0.0)
        else:
            x = x + jnp.where(rows >= shift, pltpu.roll(x, shift, 0), 0.0)
        shift *= 2
    return x


def _hgrn_gates(f, lb):
    sig = _sigmoid(f)
    gate = lb + (1.0 - lb) * sig
    return sig, gate


B_SUB = 16


def _dot3(a, b, dims):
    ah, bh = a.astype(BF16), b.astype(BF16)
    al = (a - ah.astype(F32)).astype(BF16)
    bl = (b - bh.astype(F32)).astype(BF16)
    dot = functools.partial(lax.dot_general, dimension_numbers=dims, preferred_element_type=F32)
    return dot(ah, bh) + dot(al, bh) + dot(ah, bl)


def _sub_scales(b, i):
    r0 = i * B_SUB
    beta = b[r0 - 1:r0, :]
    return jnp.exp(b[r0:r0 + B_SUB, :] - beta), jnp.exp(jnp.minimum(beta - b, 0.0))


def _hgrn_intra_attn(qq, kk, b):
    c = qq.shape[0]
    lane = lax.broadcasted_iota(jnp.int32, (B_SUB, c), 1)
    trow = lax.broadcasted_iota(jnp.int32, (B_SUB, B_DIM), 0)
    blocks = []
    for i in range(c // B_SUB):
        r0 = i * B_SUB
        qi, bi = qq[r0:r0 + B_SUB, :], b[r0:r0 + B_SUB, :]
        if i == 0:
            a_i = jnp.zeros((B_SUB, c), F32)
        else:
            eq, ek = _sub_scales(b, i)
            a_i = jnp.where(lane < r0, _dot3(qi * eq, kk * ek, NT), 0.0)
        for sl in range(B_SUB):
            s = r0 + sl
            e = jnp.exp(jnp.where(trow >= sl, bi - b[s:s + 1, :], -jnp.inf))
            col = jnp.sum(qi * kk[s:s + 1, :] * e, axis=1, keepdims=True)
            a_i = jnp.where(lane == s, col, a_i)
        blocks.append(a_i)
    return jnp.concatenate(blocks, axis=0)


def hgrn_fwd(proj, col0, nh, lb, gn, name):
    t = proj.shape[0]
    c = B_CHUNK
    nc = t // c
    scale = B_DIM ** -0.5

    def kern(q_ref, f_ref, i_ref, g_ref, lb_ref, gn_ref, out_ref, opre_ref, st_ref, a_ref, state):
        lbv = lb_ref[...]
        gnv = gn_ref[...]
        state[...] = jnp.zeros_like(state)

        def chunk(ci, carry):
            rows = pl.ds(pl.multiple_of(ci * c, c), c)
            _, gate = _hgrn_gates(f_ref[rows, :], lbv)
            kk = 1.0 - gate
            qb = q_ref[rows, :]
            qq = qb * _sigmoid(qb) * scale
            v = i_ref[rows, :]
            b = _cumsum_rows(jnp.log(gate), c)
            st = state[...]
            st_ref[ci] = st
            o_inter = lax.dot_general((qq * jnp.exp(b)).astype(BF16), st.astype(BF16), NT,
                                      preferred_element_type=F32)
            amat = _hgrn_intra_attn(qq, kk, b)
            a_ref[ci] = amat
            o = jnp.dot(amat.astype(BF16), v.astype(BF16), preferred_element_type=F32) + o_inter
            opre_ref[rows, :] = o
            bl = b[c - 1:c, :]
            state[...] = st * jnp.exp(bl) + lax.dot_general(
                v.astype(BF16), (kk * jnp.exp(bl - b)).astype(BF16), TN, preferred_element_type=F32)
            r = lax.rsqrt(jnp.mean(o * o, axis=-1, keepdims=True) + NORM_EPS)
            gb = g_ref[rows, :]
            out_ref[rows, :] = (o * r * gnv * (gb * _sigmoid(gb))).astype(BF16)
            return carry

        lax.fori_loop(0, nc, chunk, 0)

    def col(off):
        return pl.BlockSpec((t, B_DIM), lambda h: (0, col0 + off * nh + h))

    return pl.pallas_call(
        kern, name=name, grid=(nh,),
        in_specs=[col(0), col(1), col(2), col(3),
                  pl.BlockSpec((None, 1, B_DIM), lambda h: (h, 0, 0)),
                  pl.BlockSpec((1, B_DIM), lambda h: (0, 0))],
        out_specs=[pl.BlockSpec((t, B_DIM), lambda h: (0, h)),
                   pl.BlockSpec((t, B_DIM), lambda h: (0, h)),
                   pl.BlockSpec((None, nc, B_DIM, B_DIM), lambda h: (h, 0, 0, 0)),
                   pl.BlockSpec((None, nc, c, c), lambda h: (h, 0, 0, 0))],
        out_shape=[_sds((t, nh * B_DIM), BF16), _sds((t, nh * B_DIM), F32),
                   _sds((nh, nc, B_DIM, B_DIM), F32), _sds((nh, nc, c, c), F32)],
        scratch_shapes=[pltpu.VMEM((B_DIM, B_DIM), F32)],
        compiler_params=_params("parallel"),
    )(proj, proj, proj, proj, lb, gn)


def hgrn_bwd(proj, col0, nh, lb, gn, opre, states, amats, dout, dcol0, name):
    t = proj.shape[0]
    c = B_CHUNK
    nc = t // c
    scale = B_DIM ** -0.5
    nsub = c // B_SUB

    def kern(q_ref, f_ref, i_ref, g_ref, lb_ref, gn_ref, opre_ref, st_ref, a_ref, dout_ref,
             dq_ref, df_ref, di_ref, dg_ref, dgn_ref, dlb_ref, dstate, dksc):
        lbv = lb_ref[...]
        gnv = gn_ref[...]
        dstate[...] = jnp.zeros_like(dstate)
        dlb_ref[...] = jnp.zeros_like(dlb_ref)

        @pl.when(pl.program_id(0) == 0)
        def _():
            dgn_ref[...] = jnp.zeros_like(dgn_ref)

        srow = lax.broadcasted_iota(jnp.int32, (c, B_DIM), 0)
        lane = lax.broadcasted_iota(jnp.int32, (B_SUB, c), 1)
        trow = lax.broadcasted_iota(jnp.int32, (B_SUB, B_DIM), 0)
        arow = lax.broadcasted_iota(jnp.int32, (c, c), 0)
        alane = lax.broadcasted_iota(jnp.int32, (c, c), 1)

        def chunk(cj, carry):
            ci = nc - 1 - cj
            rows = pl.ds(pl.multiple_of(ci * c, c), c)
            f = f_ref[rows, :]
            sig, gate = _hgrn_gates(f, lbv)
            kk = 1.0 - gate
            qb = q_ref[rows, :]
            sq = _sigmoid(qb)
            qq = qb * sq * scale
            v = i_ref[rows, :]
            b = _cumsum_rows(jnp.log(gate), c)
            st0 = st_ref[ci]
            dst = dstate[...]
            o = opre_ref[rows, :]
            gb = g_ref[rows, :]
            sg = _sigmoid(gb)
            silu_g = gb * sg
            d_out = dout_ref[rows, :]
            r = lax.rsqrt(jnp.mean(o * o, axis=-1, keepdims=True) + NORM_EPS)
            y = o * r
            dg_ref[rows, :] = (d_out * y * gnv * (sg * (1.0 + gb * (1.0 - sg)))).astype(BF16)
            dyn = d_out * silu_g
            dgn_ref[...] += jnp.sum(dyn * y, axis=0, keepdims=True)
            dy = dyn * gnv
            do = r * (dy - y * jnp.mean(dy * y, axis=-1, keepdims=True))
            eb = jnp.exp(b)
            bl = b[c - 1:c, :]
            ebl = jnp.exp(bl - b)
            ebl_last = jnp.exp(bl)
            do_b = do.astype(BF16)
            dst_b = dst.astype(BF16)
            dq_inter = jnp.dot(do_b, st0.astype(BF16), preferred_element_type=F32) * eb
            dst0 = lax.dot_general(do_b, (qq * eb).astype(BF16), TN,
                                   preferred_element_type=F32) + dst * ebl_last
            dv_inter = lax.dot_general((kk * ebl).astype(BF16), dst_b, NT, preferred_element_type=F32)
            dk_inter = jnp.dot(v.astype(BF16), dst_b, preferred_element_type=F32) * ebl
            amat = a_ref[ci]
            v_b = v.astype(BF16)
            d_a = lax.dot_general(do_b, v_b, NT, preferred_element_type=F32)
            d_a = jnp.where(arow >= alane, d_a, 0.0)
            dv_intra = lax.dot_general(amat.astype(BF16), do_b, TN, preferred_element_type=F32)
            dk_pairs = jnp.zeros((c, B_DIM), F32)
            dq_blocks = []
            for i in range(nsub):
                r0 = i * B_SUB
                qi, bi = qq[r0:r0 + B_SUB, :], b[r0:r0 + B_SUB, :]
                da_i = d_a[r0:r0 + B_SUB, :]
                if i == 0:
                    dq_i = jnp.zeros((B_SUB, B_DIM), F32)
                else:
                    eq, ek = _sub_scales(b, i)
                    da_m = jnp.where(lane < r0, da_i, 0.0)
                    dq_i = _dot3(da_m, kk * ek, NN) * eq
                    dk_pairs = dk_pairs + _dot3(da_m, qi * eq, TN) * ek
                for sl in range(B_SUB):
                    s = r0 + sl
                    e = jnp.exp(jnp.where(trow >= sl, bi - b[s:s + 1, :], -jnp.inf))
                    dacol = jnp.sum(jnp.where(lane == s, da_i, 0.0), axis=1, keepdims=True)
                    w = dacol * e
                    dq_i = dq_i + w * kk[s:s + 1, :]
                    dksc[s:s + 1, :] = jnp.sum(w * qi, axis=0, keepdims=True)
                dq_blocks.append(dq_i)
            dq = jnp.concatenate(dq_blocks, axis=0) + dq_inter
            dk = dk_pairs + dksc[...] + dk_inter
            dv = dv_intra + dv_inter
            db = qq * dq - kk * dk
            extra = (jnp.sum(kk * dk_inter, axis=0, keepdims=True)
                     + ebl_last * jnp.sum(st0 * dst, axis=0, keepdims=True))
            db = db + jnp.where(srow == c - 1, extra, 0.0)
            dlog = _cumsum_rows(db, c, reverse=True)
            dgate = dlog / gate - dk
            df_ref[rows, :] = (dgate * (1.0 - lbv) * sig * (1.0 - sig)).astype(BF16)
            dlb_ref[...] += jnp.sum(dgate * (1.0 - sig), axis=0, keepdims=True)
            dq_ref[rows, :] = (dq * scale * (sq * (1.0 + qb * (1.0 - sq)))).astype(BF16)
            di_ref[rows, :] = dv.astype(BF16)
            dstate[...] = dst0
            return carry

        lax.fori_loop(0, nc, chunk, 0)

    def col(off):
        return pl.BlockSpec((t, B_DIM), lambda h: (0, col0 + off * nh + h))

    hcol = pl.BlockSpec((t, B_DIM), lambda h: (0, h))
    vec = pl.BlockSpec((None, 1, B_DIM), lambda h: (h, 0, 0))
    wide = _sds((t, nh * B_DIM), BF16)
    return pl.pallas_call(
        kern, name=name, grid=(nh,),
        in_specs=[col(0), col(1), col(2), col(3), vec,
                  pl.BlockSpec((1, B_DIM), lambda h: (0, 0)), hcol,
                  pl.BlockSpec((None, nc, B_DIM, B_DIM), lambda h: (h, 0, 0, 0)),
                  pl.BlockSpec((None, nc, c, c), lambda h: (h, 0, 0, 0)),
                  pl.BlockSpec((t, B_DIM), lambda h: (0, dcol0 + h))],
        out_specs=[hcol, hcol, hcol, hcol, pl.BlockSpec((1, B_DIM), lambda h: (0, 0)), vec],
        out_shape=[wide, wide, wide, wide, _sds((1, B_DIM), F32), _sds((nh, 1, B_DIM), F32)],
        scratch_shapes=[pltpu.VMEM((B_DIM, B_DIM), F32), pltpu.VMEM((c, B_DIM), F32)],
        compiler_params=_params("arbitrary"),
    )(proj, proj, proj, proj, lb, gn, opre, states, amats, dout)


def lower_bounds_fwd(raw, name):
    n, w = raw.shape

    def kern(raw_ref, lb_ref, soft_ref):
        r = raw_ref[...]
        mx = r[0:1]
        for i in range(1, n):
            mx = jnp.maximum(mx, r[i:i + 1])
        e = jnp.exp(r - mx)
        den = e[0:1]
        for i in range(1, n):
            den = den + e[i:i + 1]
        soft = e / den
        soft_ref[...] = soft
        run = soft[0:1]
        lb_ref[0:1, :] = run - soft[0:1]
        for i in range(1, n):
            run = run + soft[i:i + 1]
            lb_ref[i:i + 1, :] = run - soft[0:1]

    return pl.pallas_call(kern, name=name, out_shape=[_sds((n, w), F32), _sds((n, w), F32)])(raw)


def lower_bounds_bwd(soft, dlb, name):
    n, w = soft.shape

    def kern(soft_ref, dlb_ref, out_ref):
        s = soft_ref[...]
        d = dlb_ref[...]
        total = d[0:1]
        for i in range(1, n):
            total = total + d[i:i + 1]
        us = []
        tail = total
        for i in range(n):
            us.append(tail - total if i == 0 else tail)
            tail = tail - d[i:i + 1]
        dot = s[0:1] * us[0]
        for i in range(1, n):
            dot = dot + s[i:i + 1] * us[i]
        for i in range(n):
            out_ref[i:i + 1, :] = s[i:i + 1] * (us[i] - dot)

    return pl.pallas_call(kern, name=name, out_shape=_sds((n, w), F32))(soft, dlb)


def _row_tile(kdim, n):
    tk = 512
    while tk > 8 and tk * n > 256 * 1024:
        tk //= 2
    return min(kdim, tk)


def _adam_update(w, g, m, v):
    m2 = ADAM_B1 * m + (1.0 - ADAM_B1) * g
    v2 = ADAM_B2 * v + (1.0 - ADAM_B2) * (g * g)
    m_hat = m2 / (1.0 - ADAM_B1 ** ADAM_STEP)
    v_hat = v2 / (1.0 - ADAM_B2 ** ADAM_STEP)
    delta = -ADAM_LR * (m_hat / (jnp.sqrt(v_hat) + ADAM_EPS) + ADAM_WD * w)
    return delta, m2, v2


def adamw_small(w, g, m, v, name):
    def kern(w_ref, g_ref, m_ref, v_ref, d_ref, m2_ref, v2_ref):
        d, m2, v2 = _adam_update(w_ref[...], g_ref[...], m_ref[...], v_ref[...])
        d_ref[...] = d
        m2_ref[...] = m2
        v2_ref[...] = v2

    return pl.pallas_call(kern, name=name, out_shape=[_sds(w.shape, F32)] * 3)(w, g, m, v)


def adamw_big(parts, w, m, v, name):
    nl, kdim, n = w.shape
    tk = _row_tile(kdim, n)

    def kern(p_ref, w_ref, m_ref, v_ref, g_ref, d_ref, m2_ref, v2_ref):
        g = p_ref[0].astype(F32)
        for q in range(1, 4):
            g = g + p_ref[q].astype(F32)
        d, m2, v2 = _adam_update(w_ref[...], g, m_ref[...], v_ref[...])
        g_ref[...] = g
        d_ref[...] = d
        m2_ref[...] = m2
        v2_ref[...] = v2

    blk = pl.BlockSpec((None, tk, n), lambda l, i: (l, i, 0))
    return pl.pallas_call(
        kern, name=name, grid=(nl, kdim // tk),
        in_specs=[pl.BlockSpec((None, 4, tk, n), lambda l, i: (l, 0, i, 0)), blk, blk, blk],
        out_specs=[blk] * 4, out_shape=[_sds(w.shape, F32)] * 4,
        compiler_params=_params("parallel", "parallel"),
    )(parts, w, m, v)


def cast_bf16(w, name):
    nl, kdim, n = w.shape
    tk = _row_tile(kdim, n)

    def kern(w_ref, o_ref):
        o_ref[...] = w_ref[...].astype(BF16)

    blk = pl.BlockSpec((None, tk, n), lambda l, i: (l, i, 0))
    return pl.pallas_call(
        kern, name=name, grid=(nl, kdim // tk), in_specs=[blk], out_specs=blk,
        out_shape=_sds(w.shape, BF16), compiler_params=_params("parallel", "parallel"),
    )(w)


def pair_add(dw, r1, core, name):
    kdim, n = dw.shape[1], dw.shape[2]
    tk = _row_tile(kdim, n)

    def kern(c_ref, a_ref, b_ref, o_ref):
        o_ref[...] = (a_ref[...].astype(F32) + b_ref[...].astype(F32)).astype(BF16)

    grid_spec = pltpu.PrefetchScalarGridSpec(
        num_scalar_prefetch=1, grid=(4, kdim // tk),
        in_specs=[pl.BlockSpec((None, tk, n), lambda p, i, c: (2 * p + c[0], i, 0)),
                  pl.BlockSpec((None, tk, n), lambda p, i, c: (p, i, 0))],
        out_specs=pl.BlockSpec((None, tk, n), lambda p, i, c: (p, i, 0)))
    return pl.pallas_call(
        kern, name=name, grid_spec=grid_spec, out_shape=_sds((4, kdim, n), BF16),
        compiler_params=_params("parallel", "parallel"),
    )(core, dw, r1)


ANY = pl.BlockSpec(memory_space=pl.ANY)


def _place():
    x, y, c = lax.axis_index("x"), lax.axis_index("y"), lax.axis_index("c")
    chips = [(1 - x, y), (x, 1 - y), (1 - x, 1 - y)]
    return x, y, c, chips


def all_gather(shards, name):
    n = len(shards)

    def kern(*refs):
        ins, outs = refs[:n], refs[n:2 * n]
        send_sems, recv_sems, local_sems = refs[2 * n:]
        x, y, c, chips = _place()
        me, sib = (x, y, c), (x, y, 1 - c)

        def copy(t, k, block, to, src=None):
            px, py, pc = block
            dst = outs[t].at[4 * px + 2 * py + pc]
            return pltpu.make_async_remote_copy(
                src_ref=dst if src is None else src, dst_ref=dst,
                send_sem=send_sems.at[7 * t + k], recv_sem=recv_sems.at[7 * t + k],
                device_id=to, device_id_type=MESH)

        mine = [pltpu.make_async_copy(ins[t], outs[t].at[4 * x + 2 * y + c], local_sems.at[t])
                for t in range(n)]
        for cp in mine:
            cp.start()
        first = []
        for t in range(n):
            first.append(copy(t, 0, me, sib, src=ins[t]))
            first += [copy(t, 1 + j, me, (*chip, c), src=ins[t]) for j, chip in enumerate(chips)]
        for cp in first:
            cp.start()
        passed = []
        for t in range(n):
            for j, chip in enumerate(chips):
                copy(t, 1 + j, (*chip, c), me).wait_recv()
                fwd = copy(t, 4 + j, (*chip, c), sib)
                fwd.start()
                passed.append(fwd)
        for t in range(n):
            copy(t, 0, sib, me).wait_recv()
            for j, chip in enumerate(chips):
                copy(t, 4 + j, (*chip, 1 - c), me).wait_recv()
        for cp in first + passed:
            cp.wait_send()
        for cp in mine:
            cp.wait()

    return pl.pallas_call(
        kern, name=name, in_specs=[ANY] * n, out_specs=[ANY] * n,
        out_shape=[_sds((N_DEV,) + s.shape, s.dtype) for s in shards],
        scratch_shapes=[pltpu.SemaphoreType.DMA((7 * n,)), pltpu.SemaphoreType.DMA((7 * n,)),
                        pltpu.SemaphoreType.DMA((n,))],
    )(*shards)


HBM = pl.BlockSpec(memory_space=pltpu.HBM)
SEM = pl.BlockSpec(memory_space=pltpu.SEMAPHORE)
DATAFLOW = pltpu.SideEffectType.DATAFLOW_SIDE_EFFECTING


def _first_level_targets():
    x, y, c, chips = _place()
    return 4 * x + 2 * y + c, [(x, y, 1 - c)] + [(*chip, c) for chip in chips]


def gather_start(shards, name):
    n = len(shards)
    lands = [lax.empty((N_DEV,) + s.shape, s.dtype) for s in shards]

    def kern(*refs):
        ins, lnd = refs[:n], refs[n:2 * n]
        send_sems, recv_sems = refs[2 * n], refs[2 * n + 1]
        token = refs[-1]
        me, targets = _first_level_targets()
        for t in range(n):
            for k, to in enumerate(targets):
                pltpu.make_async_remote_copy(
                    src_ref=ins[t], dst_ref=lnd[t].at[me], send_sem=send_sems.at[4 * t + k],
                    recv_sem=recv_sems.at[4 * t + k], device_id=to, device_id_type=MESH).start()
        token[...] = jnp.zeros_like(token)

    args = [pltpu.with_memory_space_constraint(a, pltpu.HBM) for a in list(shards) + lands]
    return pl.pallas_call(
        kern, name=name,
        out_shape=(pltpu.SemaphoreType.DMA((4 * n,)), pltpu.SemaphoreType.DMA((4 * n,)),
                   *[pltpu.HBM(a.shape, a.dtype) for a in args], _sds((8, LANES), F32)),
        in_specs=[HBM] * (2 * n),
        out_specs=(SEM, SEM, *[HBM] * (2 * n), pl.BlockSpec(memory_space=pltpu.VMEM)),
        input_output_aliases={i: 2 + i for i in range(2 * n)},
        compiler_params=pltpu.CompilerParams(has_side_effects=DATAFLOW),
    )(*args)


def gather_wait(send_sems, recv_sems, shards, lands, after, name):
    n = len(shards)

    def kern(*refs):
        ins, lnd = refs[:n], refs[n:2 * n]
        send_sems, recv_sems = refs[2 * n], refs[2 * n + 1]
        me, targets = _first_level_targets()
        for t in range(n):
            for k, to in enumerate(targets):
                cp = pltpu.make_async_remote_copy(
                    src_ref=ins[t], dst_ref=lnd[t].at[me], send_sem=send_sems.at[4 * t + k],
                    recv_sem=recv_sems.at[4 * t + k], device_id=to, device_id_type=MESH)
                cp.wait_send()
                cp.wait_recv()

    bufs = list(shards) + list(lands)
    return pl.pallas_call(
        kern, name=name, out_shape=tuple(pltpu.HBM(a.shape, a.dtype) for a in bufs),
        in_specs=[HBM] * (2 * n) + [SEM, SEM, ANY], out_specs=[HBM] * (2 * n),
        input_output_aliases={i: i for i in range(2 * n)},
        compiler_params=pltpu.CompilerParams(has_side_effects=DATAFLOW),
    )(*bufs, send_sems, recv_sems, after)


def gather_forward(shards, lands, name):
    n = len(shards)

    def kern(*refs):
        ins, lnd = refs[:n], refs[2 * n:3 * n]
        send_sems, recv_sems, local_sems = refs[3 * n:]
        x, y, c, chips = _place()
        mine = [pltpu.make_async_copy(ins[t], lnd[t].at[4 * x + 2 * y + c], local_sems.at[t])
                for t in range(n)]
        for cp in mine:
            cp.start()
        passed = []
        for t in range(n):
            for j, (qx, qy) in enumerate(chips):
                block = lnd[t].at[4 * qx + 2 * qy + c]
                cp = pltpu.make_async_remote_copy(
                    src_ref=block, dst_ref=block, send_sem=send_sems.at[3 * t + j],
                    recv_sem=recv_sems.at[3 * t + j], device_id=(x, y, 1 - c), device_id_type=MESH)
                cp.start()
                passed.append(cp)
        for cp in passed:
            cp.wait_recv()
        for cp in passed:
            cp.wait_send()
        for cp in mine:
            cp.wait()

    return pl.pallas_call(
        kern, name=name, in_specs=[ANY] * (2 * n), out_specs=[ANY] * n,
        out_shape=[_sds(a.shape, a.dtype) for a in lands],
        input_output_aliases={n + i: i for i in range(n)},
        scratch_shapes=[pltpu.SemaphoreType.DMA((3 * n,)), pltpu.SemaphoreType.DMA((3 * n,)),
                        pltpu.SemaphoreType.DMA((n,))],
    )(*shards, *lands)


def all_reduce_small(vec, name):
    r = vec.shape[0]

    def kern(v_ref, o_ref, buf, send_sems, recv_sems):
        x, y, c, _ = _place()
        me = 4 * x + 2 * y + c
        peers = [(x, y, 1 - c), (1 - x, y, c), (x, 1 - y, c), (1 - x, 1 - y, c),
                 (1 - x, y, 1 - c), (x, 1 - y, 1 - c), (1 - x, 1 - y, 1 - c)]
        buf[me] = v_ref[...]
        copies = []
        for k, peer in enumerate(peers):
            cp = pltpu.make_async_remote_copy(
                src_ref=v_ref, dst_ref=buf.at[me], send_sem=send_sems.at[k],
                recv_sem=recv_sems.at[k], device_id=peer, device_id_type=MESH)
            cp.start()
            copies.append(cp)
        for cp in copies:
            cp.wait_recv()
        for cp in copies:
            cp.wait_send()
        total = buf[0]
        for d in range(1, N_DEV):
            total = total + buf[d]
        o_ref[...] = total

    vm = pl.BlockSpec(memory_space=pltpu.VMEM)
    return pl.pallas_call(
        kern, name=name, in_specs=[vm], out_specs=vm, out_shape=_sds(vec.shape, F32),
        scratch_shapes=[pltpu.VMEM((N_DEV, r, LANES), F32), pltpu.SemaphoreType.DMA((7,)),
                        pltpu.SemaphoreType.DMA((7,))],
    )(vec)


def exchange_with_sibling(grads, name):
    n = len(grads)

    def kern(*refs):
        ins, outs = refs[:n], refs[n:2 * n]
        send_sems, recv_sems = refs[2 * n:]
        x, y, c, _ = _place()
        copies = []
        for t in range(n):
            for p in range(4):
                cp = pltpu.make_async_remote_copy(
                    src_ref=ins[t].at[2 * p + 1 - c], dst_ref=outs[t].at[p],
                    send_sem=send_sems.at[4 * t + p], recv_sem=recv_sems.at[4 * t + p],
                    device_id=(x, y, 1 - c), device_id_type=MESH)
                cp.start()
                copies.append(cp)
        for cp in copies:
            cp.wait_recv()
        for cp in copies:
            cp.wait_send()

    return pl.pallas_call(
        kern, name=name, in_specs=[ANY] * n, out_specs=[ANY] * n,
        out_shape=[_sds((4,) + g.shape[1:], g.dtype) for g in grads],
        scratch_shapes=[pltpu.SemaphoreType.DMA((4 * n,)), pltpu.SemaphoreType.DMA((4 * n,))],
    )(*grads)


def exchange_between_chips(partials, layers, kinds, name):
    n = len(partials)
    n_kind = max(kinds) + 1
    shapes = []
    for kd in range(n_kind):
        idx = [i for i in range(n) if kinds[i] == kd]
        nl = max(layers[i] for i in idx) + 1
        shapes.append(_sds((nl,) + partials[idx[0]].shape, partials[idx[0]].dtype))

    def kern(*refs):
        ins, outs = refs[:n], refs[n:n + n_kind]
        send_sems, recv_sems, local_sems = refs[n + n_kind:]
        x, y, c, chips = _place()
        mine = 2 * x + y
        local = []
        copies = []
        for t in range(n):
            dst = outs[kinds[t]].at[layers[t], mine]
            lc = pltpu.make_async_copy(ins[t].at[mine], dst, local_sems.at[t])
            lc.start()
            local.append(lc)
            for j, (qx, qy) in enumerate(chips):
                cp = pltpu.make_async_remote_copy(
                    src_ref=ins[t].at[2 * qx + qy], dst_ref=dst,
                    send_sem=send_sems.at[3 * t + j], recv_sem=recv_sems.at[3 * t + j],
                    device_id=(qx, qy, c), device_id_type=MESH)
                cp.start()
                copies.append(cp)
        for cp in copies:
            cp.wait_recv()
        for cp in copies:
            cp.wait_send()
        for lc in local:
            lc.wait()

    return pl.pallas_call(
        kern, name=name, in_specs=[ANY] * n, out_specs=[ANY] * n_kind, out_shape=shapes,
        scratch_shapes=[pltpu.SemaphoreType.DMA((3 * n,)), pltpu.SemaphoreType.DMA((3 * n,)),
                        pltpu.SemaphoreType.DMA((n,))],
    )(*partials)


def _pack(arrays):
    flat = jnp.concatenate([a.reshape(-1).astype(F32) for a in arrays])
    pad = (-flat.shape[0]) % (8 * LANES)
    return jnp.pad(flat, (0, pad)).reshape(-1, LANES)


def _unpack(packed, shapes):
    flat = packed.reshape(-1)
    out, off = [], 0
    for s in shapes:
        n = math.prod(s)
        out.append(flat[off:off + n].reshape(s))
        off += n
    return out


def _to_heads(x2d, dil, n_heads, dh):
    t = x2d.shape[0]
    return x2d.reshape(t // dil, dil, n_heads, dh).transpose(2, 1, 0, 3).reshape(n_heads, t, dh)


def _from_heads(xh, dil):
    h, t, w = xh.shape
    return xh.reshape(h, dil, t // dil, w).transpose(2, 1, 0, 3).reshape(t, h * w)


def _unperm(xh, dil):
    h, t, w = xh.shape
    return xh.reshape(h, dil, t // dil, w).transpose(0, 2, 1, 3).reshape(h, t, w)


def _perm(xh, dil):
    h, t, w = xh.shape
    return xh.reshape(h, t // dil, dil, w).transpose(0, 2, 1, 3).reshape(h, t, w)


def local_step(x, target, norm_mix_g, norm_mlp_g, final_norm_g, lbs, hgrn_norm_g, sinks,
               bq_full, bo_full, weights_get, weights_prefetch):
    t, d = x.shape
    depth = norm_mix_g.shape[0]
    na = d // 2 // A_DIM
    nbh = d // 2 // B_DIM
    nq = d // C_DIM
    nkv = nq // C_GROUP
    a_w = 3 * na * A_DIM
    c_w = (nq + 2 * nkv) * C_DIM
    tabs_a = rope_tables(t, A_DIM)
    tabs_c = rope_tables(t, C_DIM)
    saved = []
    for l in range(depth):
        s = {"x_in": x}
        win_g, wout_g, w1_g, w2_g = weights_get(l, x)
        gain = norm_mix_g[l]
        if l + 1 < depth:
            gain = gain + weights_prefetch(l + 1)
        s.update(win=win_g, wout=wout_g, w1=w1_g, w2=w2_g)
        h = rms_fwd(x, gain, "norm_mix_fwd")
        s["h"] = h
        if l % 2 == 0:
            e = l // 2
            proj = mm_cols_sharded(h, win_g, 0, "even_in_proj")[0]
            qkv_r = rope_call(proj, tabs_a, a_w, 2 * na, False, "rope_a")[0]
            nums, ms, ls, hms = [], [], [], []
            for window, dil in A_BRANCHES:
                hm = _to_heads(qkv_r, dil, 3 * na, A_DIM)
                num, m, lsum = band_fwd(hm, 0, na, 2 * na, na, 1, t // dil // BLK, window // dil,
                                        f"dilated_fwd_{dil}")
                hms.append(hm)
                nums.append(_unperm(num, dil))
                ms.append(_unperm(m, dil))
                ls.append(_unperm(lsum, dil))
            oa, lse = merge_branches(nums, ms, ls, "dilated_merge")
            lb_e = lbs[e].reshape(nbh, 1, B_DIM)
            gn_e = hgrn_norm_g[e].reshape(1, B_DIM)
            ob, opre, states, amats = hgrn_fwd(proj, 3 * na, nbh, lb_e, gn_e, "hgrn_fwd")
            mixed = jnp.concatenate([_from_heads(oa, 1).astype(BF16), ob], axis=1)
            x = mm_rows_sharded(mixed, wout_g, 0, "even_out_proj", [x], ["tile"], _ep_residual)
            s.update(proj=proj, hms=hms, oa=oa, lse=lse, opre=opre, states=states, amats=amats,
                     mixed=mixed, lb=lb_e, gn=gn_e)
        else:
            o = l // 2
            wq = win_g[:, 0].transpose(1, 0, 2).reshape(d, c_w)
            proj = mm_plain(h, wq, "odd_qkv_proj", [bq_full[o].reshape(1, c_w)], ["row"], _ep_bias)
            qkv_r = rope_call(proj, tabs_c, c_w, (nq + nkv) * C_DIM // LANES, False, "rope_c")[0]
            hm = _to_heads(qkv_r, 1, nq + 2 * nkv, C_DIM)
            sink_rows = jnp.repeat(sinks[o].reshape(nkv, C_GROUP), BLK, axis=1).reshape(
                nkv, C_GROUP * BLK, 1)
            num, m, lsum = band_fwd(hm, 0, nq, nq + nkv, nkv, C_GROUP, t // BLK, C_WINDOW - 1,
                                    "swa_fwd", sink_rows=sink_rows)
            o_hm, lse = normalise_heads(num, m, lsum, "swa_normalise")
            attn = _from_heads(o_hm, 1).astype(BF16)
            x = mm_rows_sharded(attn, wout_g, 0, "odd_out_proj", [bo_full[o].reshape(1, d), x],
                                ["row", "tile"], _ep_bias_residual)
            s.update(wq=wq, hm=hm, sink_rows=sink_rows, o_hm=o_hm, lse=lse, attn=attn)
        s["x_mid"] = x
        h2 = rms_fwd(x, norm_mlp_g[l], "norm_mlp_fwd")
        u, act = mm_cols_sharded(h2, w1_g, 0, "mlp_up", epilogue=_ep_relu2, n_out=2)
        x = mm_rows_sharded(act, w2_g, 0, "mlp_down", [x], ["tile"], _ep_residual)
        s.update(h2=h2, u=u, act=act)
        saved.append(s)

    dx, dxb, dg_final, loss_part = loss_head(x, final_norm_g, target, "loss_head")
    big = []
    small = {"final": dg_final, "loss": loss_part, "mix": [None] * depth, "mlp": [None] * depth,
             "lb": {}, "gn": {}, "sinks": {}, "bq": {}, "bo": {}}
    for l in reversed(range(depth)):
        s = saved[l]
        win_g, wout_g, w1_g, w2_g = s["win"], s["wout"], s["w1"], s["w2"]
        big.append(("w2", l, mm_tn(s["act"], dxb, "mlp_down_dw").reshape(N_DEV, -1, d)))
        du = mm_nt_rows_sharded(dxb, w2_g, 0, "mlp_down_dx", extras=[s["u"]],
                                epilogue=_ep_relu2_bwd, out_dtype=BF16)
        big.append(("w1", l, mm_tn(s["h2"], du, "mlp_up_dw", shard_cols=w1_g.shape[-1])))
        dh2 = mm_nt_cols_sharded(du, w1_g, 0, "mlp_up_dx")
        dx, dxb, dg, col_dx = rms_bwd(s["x_mid"], norm_mlp_g[l], dh2, dx, "norm_mlp_bwd")
        small["mlp"][l] = dg
        if l % 2 == 0:
            e = l // 2
            big.append(("wout", e, mm_tn(s["mixed"], dxb, "even_out_dw").reshape(N_DEV, -1, d)))
            dmixed = mm_nt_rows_sharded(dxb, wout_g, 0, "even_out_dx")
            do_hm = _to_heads(dmixed[:, :na * A_DIM], 1, na, A_DIM)
            delta = head_delta(s["oa"], do_hm, "dilated_delta")
            dsum = None
            for (window, dil), hm in zip(A_BRANCHES, s["hms"]):
                dq, dk, dv = band_bwd(hm, 0, na, 2 * na, _perm(do_hm, dil).astype(BF16),
                                      _perm(s["lse"], dil), _perm(delta, dil), na, 1,
                                      t // dil // BLK, window // dil, f"dilated_bwd_{dil}")
                part = _from_heads(jnp.concatenate([dq, dk, dv], axis=0), dil)
                dsum = part if dsum is None else dsum + part
            dqkv_a = rope_call(dsum, tabs_a, a_w, 2 * na, True, "rope_a_bwd")[0]
            dqb, dfb, dib, dgb, dgn, dlb = hgrn_bwd(s["proj"], 3 * na, nbh, s["lb"], s["gn"],
                                                    s["opre"], s["states"], s["amats"], dmixed, na,
                                                    "hgrn_bwd")
            small["gn"][e] = dgn
            small["lb"][e] = dlb
            dproj = jnp.concatenate([dqkv_a, dqb, dfb, dib, dgb], axis=1)
            big.append(("win", e, mm_tn(s["h"], dproj, "even_in_dw", shard_cols=win_g.shape[-1])))
            dh = mm_nt_cols_sharded(dproj, win_g, 0, "even_in_dx")
        else:
            o = l // 2
            small["bo"][o] = col_dx
            big.append(("wo", o, mm_tn(s["attn"], dxb, "odd_out_dw").reshape(N_DEV, -1, d)))
            dattn = mm_nt_rows_sharded(dxb, wout_g, 0, "odd_out_dx")
            do_hm = _to_heads(dattn, 1, nq, C_DIM)
            delta = head_delta(s["o_hm"], do_hm, "swa_delta")
            dq, dk, dv, dsink = band_bwd(s["hm"], 0, nq, nq + nkv, do_hm.astype(BF16), s["lse"],
                                         delta, nkv, C_GROUP, t // BLK, C_WINDOW - 1, "swa_bwd",
                                         sink_rows=s["sink_rows"])
            small["sinks"][o] = dsink
            dqkv = _from_heads(jnp.concatenate([dq, dk, dv], axis=0), 1)
            dproj, dbq = rope_call(dqkv, tabs_c, c_w, (nq + nkv) * C_DIM // LANES, True,
                                   "rope_c_bwd", col_sum=True)
            small["bq"][o] = dbq
            dwq = mm_tn(s["h"], dproj, "odd_qkv_dw", tn=512)
            big.append(("wqkv", o, dwq.reshape(d, N_DEV, -1).transpose(1, 0, 2)))
            dh = mm_nt_plain(dproj, s["wq"], "odd_qkv_dx", tk=c_w)
        dx, dxb, dg, _ = rms_bwd(s["x_in"], norm_mix_g[l], dh, dx, "norm_mix_bwd")
        small["mix"][l] = dg
    return dx, big, small


def kernel(x, norm_mix_g, norm_mlp_g, final_norm_g, even_w_in, even_w_out, hgrn_lb_raw, hgrn_norm_g, odd_w_qkv, odd_b_qkv, odd_sinks, odd_w_o, odd_b_o, mlp_w1, mlp_w2, loss_target, m_norm_mix_g, m_norm_mlp_g, m_final_norm_g, m_even_w_in, m_even_w_out, m_hgrn_lb_raw, m_hgrn_norm_g, m_odd_w_qkv, m_odd_b_qkv, m_odd_sinks, m_odd_w_o, m_odd_b_o, m_mlp_w1, m_mlp_w2, v_norm_mix_g, v_norm_mlp_g, v_final_norm_g, v_even_w_in, v_even_w_out, v_hgrn_lb_raw, v_hgrn_norm_g, v_odd_w_qkv, v_odd_b_qkv, v_odd_sinks, v_odd_w_o, v_odd_b_o, v_mlp_w1, v_mlp_w2):
    d = x.shape[2]
    depth = norm_mix_g.shape[0]
    n_even, n_odd = even_w_in.shape[0], odd_w_qkv.shape[0]
    xi, yi, ci = lax.axis_index("x"), lax.axis_index("y"), lax.axis_index("c")
    dev = 4 * xi + 2 * yi + ci
    core = ci.astype(jnp.int32).reshape(1)

    big_w = {"win": even_w_in, "wout": even_w_out, "wqkv": odd_w_qkv, "wo": odd_w_o,
             "w1": mlp_w1, "w2": mlp_w2}
    big_m = {"win": m_even_w_in, "wout": m_even_w_out, "wqkv": m_odd_w_qkv, "wo": m_odd_w_o,
             "w1": m_mlp_w1, "w2": m_mlp_w2}
    big_v = {"win": v_even_w_in, "wout": v_even_w_out, "wqkv": v_odd_w_qkv, "wo": v_odd_w_o,
             "w1": v_mlp_w1, "w2": v_mlp_w2}
    kinds = list(big_w)
    casts = {k: cast_bf16(big_w[k], f"cast_{k}") for k in kinds}

    def layer_shards(l):
        a, b = ("win", "wout") if l % 2 == 0 else ("wqkv", "wo")
        return [casts[a][l // 2], casts[b][l // 2], casts["w1"][l], casts["w2"][l]]

    in_flight = {}

    def weights_prefetch(l):
        shards = layer_shards(l)
        sems_and_bufs = gather_start(shards, f"gather_start_{l}")
        in_flight[l] = sems_and_bufs[:-1]
        return sems_and_bufs[-1][0, 0]

    def weights_get(l, after):
        if l == 0:
            gathered = all_gather(layer_shards(0), "gather_layer_0")
        else:
            n = len(layer_shards(l))
            send_sems, recv_sems, *bufs = in_flight.pop(l)
            bufs = gather_wait(send_sems, recv_sems, bufs[:n], bufs[n:], after, f"gather_wait_{l}")
            gathered = gather_forward(bufs[:n], bufs[n:], f"gather_forward_{l}")
        return [g[:, None] for g in gathered]

    bq_w, bo_w = odd_b_qkv.shape[1], odd_b_o.shape[1]
    bq_mine = lax.dynamic_update_slice(jnp.zeros((n_odd, N_DEV * bq_w), F32), odd_b_qkv,
                                       (0, dev * bq_w))
    bo_mine = lax.dynamic_update_slice(jnp.zeros((n_odd, N_DEV * bo_w), F32), odd_b_o,
                                       (0, dev * bo_w))
    bq_full, bo_full = _unpack(all_reduce_small(_pack([bq_mine, bo_mine]), "gather_biases"),
                               [bq_mine.shape, bo_mine.shape])

    lbs, soft = lower_bounds_fwd(hgrn_lb_raw, "lower_bounds")

    dx, big, small = local_step(x[0], loss_target[0], norm_mix_g, norm_mlp_g, final_norm_g, lbs,
                                hgrn_norm_g, odd_sinks, bq_full, bo_full, weights_get,
                                weights_prefetch)

    parts = ([small["mix"][l] for l in range(depth)] + [small["mlp"][l] for l in range(depth)]
             + [small["final"]] + [small["lb"][e] for e in range(n_even)]
             + [small["gn"][e] for e in range(n_even)] + [small["sinks"][o] for o in range(n_odd)]
             + [small["bq"][o] for o in range(n_odd)] + [small["bo"][o] for o in range(n_odd)]
             + [small["loss"]])
    shapes = ([(depth, d)] * 2 + [(d,), hgrn_lb_raw.shape, hgrn_norm_g.shape, odd_sinks.shape,
              (n_odd, N_DEV * bq_w), (n_odd, N_DEV * bo_w), (1, LANES)])
    g_mix, g_mlp, g_final, d_lbs, g_gn, g_sinks, g_bq_full, g_bo_full, loss_v = _unpack(
        all_reduce_small(_pack(parts), "reduce_small"), shapes)
    g_lb = lower_bounds_bwd(soft, d_lbs, "lower_bounds_bwd")
    g_bq = lax.dynamic_slice(g_bq_full, (0, dev * bq_w), (n_odd, bq_w))
    g_bo = lax.dynamic_slice(g_bo_full, (0, dev * bo_w), (n_odd, bo_w))
    loss = loss_v[0, 0]

    small_names = ["norm_mix_g", "norm_mlp_g", "final_norm_g", "hgrn_lb_raw", "hgrn_norm_g",
                   "odd_b_qkv", "odd_sinks", "odd_b_o"]
    small_w = [norm_mix_g, norm_mlp_g, final_norm_g, hgrn_lb_raw, hgrn_norm_g, odd_b_qkv,
               odd_sinks, odd_b_o]
    small_m = [m_norm_mix_g, m_norm_mlp_g, m_final_norm_g, m_hgrn_lb_raw, m_hgrn_norm_g,
               m_odd_b_qkv, m_odd_sinks, m_odd_b_o]
    small_v = [v_norm_mix_g, v_norm_mlp_g, v_final_norm_g, v_hgrn_lb_raw, v_hgrn_norm_g,
               v_odd_b_qkv, v_odd_sinks, v_odd_b_o]
    small_g = [g_mix, g_mlp, g_final, g_lb, g_gn, g_bq, g_sinks, g_bo]
    sshapes = [w.shape for w in small_w]
    sd, sm, sv = adamw_small(_pack(small_w), _pack(small_g), _pack(small_m), _pack(small_v),
                             "adamw_small")
    res = {}
    for name, g, dl, m2, v2 in zip(small_names, small_g, _unpack(sd, sshapes),
                                   _unpack(sm, sshapes), _unpack(sv, sshapes)):
        res[name] = (g.reshape(dl.shape), dl, m2, v2)

    grads = [g for _, _, g in big]
    received = exchange_with_sibling(grads, "scatter_grads_d2d")
    partials = [pair_add(g, r, core, f"pair_add_{k}")
                for (k, _, _), g, r in zip(big, grads, received)]
    stacked = exchange_between_chips(partials, [l for _, l, _ in big],
                                     [kinds.index(k) for k, _, _ in big], "scatter_grads_ici")
    long_names = {"win": "even_w_in", "wout": "even_w_out", "wqkv": "odd_w_qkv", "wo": "odd_w_o",
                  "w1": "mlp_w1", "w2": "mlp_w2"}
    for k, parts4 in zip(kinds, stacked):
        res[long_names[k]] = tuple(adamw_big(parts4, big_w[k], big_m[k], big_v[k], f"adamw_{k}"))

    order = ["norm_mix_g", "norm_mlp_g", "final_norm_g", "even_w_in", "even_w_out", "hgrn_lb_raw",
             "hgrn_norm_g", "odd_w_qkv", "odd_b_qkv", "odd_sinks", "odd_w_o", "odd_b_o", "mlp_w1",
             "mlp_w2"]
    outs = [loss, dx[None]]
    for j in range(4):
        outs += [res[n][j] for n in order]
    return tuple(outs)
```

```python
import functools
import math

import jax
import jax.numpy as jnp
from jax import lax
from jax.experimental import pallas as pl
from jax.experimental.pallas import tpu as pltpu

F32 = jnp.float32
BF16 = jnp.bfloat16
MESH = pl.DeviceIdType.MESH

N_DEV = 8
NORM_EPS = 1e-5
ROPE_THETA = 500000.0
BLK = 128
A_DIM = 128
A_BRANCHES = ((128, 1), (512, 4), (2048, 16))
B_DIM = 128
B_CHUNK = 64
C_DIM = 64
C_GROUP = 8
C_WINDOW = 128
LANES = 128

ADAM_LR = 0.001
ADAM_B1 = 0.9
ADAM_B2 = 0.999
ADAM_EPS = 1e-08
ADAM_WD = 0.01
ADAM_STEP = 10

NN = (((1,), (0,)), ((), ()))
NT = (((1,), (1,)), ((), ()))
TN = (((0,), (0,)), ((), ()))


def _params(*sem):
    return pltpu.CompilerParams(dimension_semantics=sem)


def _sigmoid(x):
    return 1.0 / (1.0 + jnp.exp(-x))


def _rows_call(name, body, row_ins, full_ins, row_outs, acc_outs, tm):
    t = row_ins[0].shape[0]
    n_ri, n_fi, n_ro = len(row_ins), len(full_ins), len(row_outs)

    def kern(*refs):
        i = pl.program_id(0)
        body(i, refs[:n_ri], refs[n_ri:n_ri + n_fi],
             refs[n_ri + n_fi:n_ri + n_fi + n_ro], refs[n_ri + n_fi + n_ro:])

    def row_spec(shape):
        return pl.BlockSpec((tm,) + tuple(shape[1:]), lambda i: (i,) + (0,) * (len(shape) - 1))

    def full_spec(shape):
        return pl.BlockSpec(tuple(shape), lambda i: (0,) * len(shape))

    outs = pl.pallas_call(
        kern, name=name, grid=(t // tm,),
        in_specs=[row_spec(a.shape) for a in row_ins] + [full_spec(a.shape) for a in full_ins],
        out_specs=[row_spec(s.shape) for s in row_outs] + [full_spec(s.shape) for s in acc_outs],
        out_shape=list(row_outs) + list(acc_outs),
        compiler_params=_params("arbitrary" if acc_outs else "parallel"),
    )(*row_ins, *full_ins)
    return outs


def _sds(shape, dtype):
    return jax.ShapeDtypeStruct(tuple(shape), dtype)


def rms_fwd(x, g, name):
    t, d = x.shape

    def body(i, ri, fi, ro, ao):
        xv = ri[0][...]
        r = lax.rsqrt(jnp.mean(xv * xv, axis=-1, keepdims=True) + NORM_EPS)
        ro[0][...] = (xv * r * fi[0][...]).astype(BF16)

    return _rows_call(name, body, [x], [g.reshape(1, d)], [_sds((t, d), BF16)], [], 256)[0]


def rms_bwd(x, g, dh, dx_res, name):
    t, d = x.shape

    def body(i, ri, fi, ro, ao):
        xv, dhv, res = ri[0][...], ri[1][...], ri[2][...]
        gv = fi[0][...]
        r = lax.rsqrt(jnp.mean(xv * xv, axis=-1, keepdims=True) + NORM_EPS)
        gd = gv * dhv
        dx = res + r * gd - xv * (r * r * r) * jnp.mean(xv * gd, axis=-1, keepdims=True)
        ro[0][...] = dx
        ro[1][...] = dx.astype(BF16)

        @pl.when(i == 0)
        def _():
            ao[0][...] = jnp.zeros_like(ao[0])
            ao[1][...] = jnp.zeros_like(ao[1])

        ao[0][...] += jnp.sum(dhv * xv * r, axis=0, keepdims=True)
        ao[1][...] += jnp.sum(dx, axis=0, keepdims=True)

    return _rows_call(name, body, [x, dh, dx_res], [g.reshape(1, d)],
                      [_sds((t, d), F32), _sds((t, d), BF16)],
                      [_sds((1, d), F32), _sds((1, d), F32)], 256)


def loss_head(x, g, target, name):
    t, d = x.shape

    def body(i, ri, fi, ro, ao):
        xv, tg = ri[0][...], ri[1][...]
        gv = fi[0][...]
        r = lax.rsqrt(jnp.mean(xv * xv, axis=-1, keepdims=True) + NORM_EPS)
        e = xv * r * gv - tg
        dy = e * (1.0 / d)
        gd = gv * dy
        dx = r * gd - xv * (r * r * r) * jnp.mean(xv * gd, axis=-1, keepdims=True)
        ro[0][...] = dx
        ro[1][...] = dx.astype(BF16)

        @pl.when(i == 0)
        def _():
            ao[0][...] = jnp.zeros_like(ao[0])
            ao[1][...] = jnp.zeros_like(ao[1])

        ao[0][...] += jnp.sum(dy * xv * r, axis=0, keepdims=True)
        part = 0.5 * jnp.sum(jnp.mean(e * e, axis=-1, keepdims=True), axis=0, keepdims=True)
        ao[1][...] += jnp.broadcast_to(part, (1, LANES))

    return _rows_call(name, body, [x, target], [g.reshape(1, d)],
                      [_sds((t, d), F32), _sds((t, d), BF16)],
                      [_sds((1, d), F32), _sds((1, LANES), F32)], 256)


def rope_tables(seq, head_dim):
    rot = head_dim // 4
    half = rot // 2
    inv_freq = 1.0 / (ROPE_THETA ** (jnp.arange(0, rot, 2, dtype=F32) / rot))
    ang = jnp.arange(seq, dtype=F32)[:, None] * inv_freq[None, :]
    cos, sin = jnp.cos(ang), jnp.sin(ang)
    zeros = jnp.zeros((seq, head_dim - rot), F32)
    zh = jnp.zeros((seq, half), F32)
    c = jnp.concatenate([cos, cos, jnp.ones((seq, head_dim - rot), F32)], axis=-1)
    sp = jnp.concatenate([zh, sin, zeros], axis=-1)
    sm = jnp.concatenate([-sin, zh, zeros], axis=-1)
    rep = LANES // head_dim
    return jnp.tile(c, (1, rep)), jnp.tile(sp, (1, rep)), jnp.tile(sm, (1, rep)), half


def rope_call(x, tabs, width, n_rope, inverse, name, col_sum=False):
    c, sp, sm, half = tabs
    t = x.shape[0]
    tm = 256
    n_slab = width // LANES

    def kern(x_ref, c_ref, sp_ref, sm_ref, o_ref, *acc):
        cv, spv, smv = c_ref[...], sp_ref[...], sm_ref[...]
        for j in range(n_slab):
            xs = x_ref[:, j * LANES:(j + 1) * LANES].astype(F32)
            if j < n_rope:
                if inverse:
                    ys = (xs * cv + pltpu.roll(xs * spv, LANES - half, 1)
                          + pltpu.roll(xs * smv, half, 1))
                else:
                    ys = (xs * cv + pltpu.roll(xs, half, 1) * spv
                          + pltpu.roll(xs, LANES - half, 1) * smv)
            else:
                ys = xs
            o_ref[:, j * LANES:(j + 1) * LANES] = ys.astype(BF16)
            if col_sum:
                @pl.when(pl.program_id(0) == 0)
                def _():
                    acc[0][:, j * LANES:(j + 1) * LANES] = jnp.zeros((1, LANES), F32)
                acc[0][:, j * LANES:(j + 1) * LANES] += jnp.sum(ys, axis=0, keepdims=True)

    tab_spec = pl.BlockSpec((tm, LANES), lambda i: (i, 0))
    out_shape = [_sds((t, width), BF16)]
    out_specs = [pl.BlockSpec((tm, width), lambda i: (i, 0))]
    if col_sum:
        out_shape.append(_sds((1, width), F32))
        out_specs.append(pl.BlockSpec((1, width), lambda i: (0, 0)))
    return pl.pallas_call(
        kern, name=name, grid=(t // tm,),
        in_specs=[pl.BlockSpec((tm, width), lambda i: (i, 0)), tab_spec, tab_spec, tab_spec],
        out_specs=out_specs, out_shape=out_shape,
        compiler_params=_params("arbitrary" if col_sum else "parallel"),
    )(x, c, sp, sm)


def _mm_call(name, a, b, extras, out_shapes, grid, a_spec, b_spec, extra_specs, out_specs,
             acc_shape, dims, epilogue):
    n_ex, n_out = len(extras), len(out_shapes)
    nk = grid[2]

    def product(a_ref, b_ref):
        bv = b_ref[...]
        if bv.ndim == 3:
            bv = bv.reshape(bv.shape[0] * bv.shape[1], bv.shape[2])
        return lax.dot_general(a_ref[...].astype(BF16), bv.astype(BF16), dims,
                               preferred_element_type=F32)

    def kern(*refs):
        a_ref, b_ref = refs[0], refs[1]
        ex = refs[2:2 + n_ex]
        outs = refs[2 + n_ex:2 + n_ex + n_out]
        if nk == 1:
            epilogue(product(a_ref, b_ref), ex, outs)
            return
        acc = refs[-1]
        k = pl.program_id(2)

        @pl.when(k == 0)
        def _():
            acc[...] = product(a_ref, b_ref)

        @pl.when(k > 0)
        def _():
            acc[...] += product(a_ref, b_ref)

        @pl.when(k == nk - 1)
        def _():
            epilogue(acc[...], ex, outs)

    return pl.pallas_call(
        kern, name=name, grid=grid,
        in_specs=[a_spec, b_spec, *extra_specs], out_specs=out_specs, out_shape=out_shapes,
        scratch_shapes=[pltpu.VMEM(acc_shape, F32)] if nk > 1 else [],
        compiler_params=_params("parallel", "parallel", "arbitrary"),
    )(a, b, *extras)


def _ep_store(dtype):
    def ep(acc, ex, outs):
        outs[0][...] = acc.astype(dtype)
    return ep


def _ep_residual(acc, ex, outs):
    outs[0][...] = acc + ex[0][...]


def _ep_bias(acc, ex, outs):
    outs[0][...] = acc + ex[0][...]


def _ep_bias_residual(acc, ex, outs):
    outs[0][...] = acc + ex[0][...] + ex[1][...]


def _ep_relu2(acc, ex, outs):
    outs[0][...] = acc
    rl = jnp.maximum(acc, 0.0)
    outs[1][...] = (rl * rl).astype(BF16)


def _ep_relu2_bwd(acc, ex, outs):
    outs[0][...] = (acc * (2.0 * jnp.maximum(ex[0][...], 0.0))).astype(BF16)


MM_TM = 1024
MM_TN = 1024
MM_TK = 2048


def mm_cols_sharded(a, wg, layer, name, epilogue=None, n_out=1):
    m, kdim = a.shape
    n = wg.shape[-1]
    tm, tk = min(m, MM_TM), min(kdim, MM_TK)
    if epilogue is None:
        epilogue, outs = _ep_store(F32), [_sds((m, N_DEV * n), F32)]
    else:
        outs = [_sds((m, N_DEV * n), F32), _sds((m, N_DEV * n), BF16)][:n_out]
    return _mm_call(
        name, a, wg, [], outs, (m // tm, N_DEV, kdim // tk),
        pl.BlockSpec((tm, tk), lambda i, j, k: (i, k)),
        pl.BlockSpec((None, None, tk, n), lambda i, j, k: (j, layer, k, 0)),
        [], [pl.BlockSpec((tm, n), lambda i, j, k: (i, j))] * len(outs),
        (tm, n), NN, epilogue)


def _extra_specs(extra_kinds, tm, tn):
    specs = []
    for kind in extra_kinds:
        if kind == "row":
            specs.append(pl.BlockSpec((1, tn), lambda i, j, k: (0, j)))
        else:
            specs.append(pl.BlockSpec((tm, tn), lambda i, j, k: (i, j)))
    return specs


def mm_rows_sharded(a, wg, layer, name, extras, extra_kinds, epilogue):
    m, kdim = a.shape
    ks, n = wg.shape[-2], wg.shape[-1]
    tm, tn = min(m, MM_TM), min(n, MM_TN)
    gps = max(1, min(kdim, MM_TK) // ks)
    return _mm_call(
        name, a, wg, extras, [_sds((m, n), F32)], (m // tm, n // tn, N_DEV // gps),
        pl.BlockSpec((tm, gps * ks), lambda i, j, k: (i, k)),
        pl.BlockSpec((gps, None, ks, tn), lambda i, j, k: (k, layer, 0, j)),
        _extra_specs(extra_kinds, tm, tn), [pl.BlockSpec((tm, tn), lambda i, j, k: (i, j))],
        (tm, tn), NN, epilogue)[0]


def mm_plain(a, w, name, extras, extra_kinds, epilogue, tn=512):
    m, kdim = a.shape
    n = w.shape[1]
    tm, tk = min(m, MM_TM), min(kdim, MM_TK)
    return _mm_call(
        name, a, w, extras, [_sds((m, n), F32)], (m // tm, n // tn, kdim // tk),
        pl.BlockSpec((tm, tk), lambda i, j, k: (i, k)),
        pl.BlockSpec((tk, tn), lambda i, j, k: (k, j)),
        _extra_specs(extra_kinds, tm, tn), [pl.BlockSpec((tm, tn), lambda i, j, k: (i, j))],
        (tm, tn), NN, epilogue)[0]


def mm_nt_cols_sharded(dy, wg, layer, name):
    m = dy.shape[0]
    kdim, n = wg.shape[-2], wg.shape[-1]
    tm, tn = min(m, MM_TM), min(kdim, MM_TN)
    return _mm_call(
        name, dy, wg, [], [_sds((m, kdim), F32)], (m // tm, kdim // tn, N_DEV),
        pl.BlockSpec((tm, n), lambda i, j, k: (i, k)),
        pl.BlockSpec((None, None, tn, n), lambda i, j, k: (k, layer, j, 0)),
        [], [pl.BlockSpec((tm, tn), lambda i, j, k: (i, j))],
        (tm, tn), NT, _ep_store(F32))[0]


def mm_nt_rows_sharded(dy, wg, layer, name, extras=(), epilogue=None, out_dtype=F32):
    m, n = dy.shape
    ks = wg.shape[-2]
    tm, tk = min(m, MM_TM), min(n, MM_TK)
    gps = max(1, MM_TN // ks)
    tn = gps * ks
    epilogue = _ep_store(out_dtype) if epilogue is None else epilogue
    return _mm_call(
        name, dy, wg, list(extras), [_sds((m, N_DEV * ks), out_dtype)],
        (m // tm, N_DEV // gps, n // tk),
        pl.BlockSpec((tm, tk), lambda i, j, k: (i, k)),
        pl.BlockSpec((gps, None, ks, tk), lambda i, j, k: (j, layer, 0, k)),
        [pl.BlockSpec((tm, tn), lambda i, j, k: (i, j))] * len(extras),
        [pl.BlockSpec((tm, tn), lambda i, j, k: (i, j))],
        (tm, tn), NT, epilogue)[0]


def mm_nt_plain(dy, w, name, tk):
    m, n = dy.shape
    kdim = w.shape[0]
    tm, tn = min(m, MM_TM), min(kdim, MM_TN)
    return _mm_call(
        name, dy, w, [], [_sds((m, kdim), F32)], (m // tm, kdim // tn, n // tk),
        pl.BlockSpec((tm, tk), lambda i, j, k: (i, k)),
        pl.BlockSpec((tn, tk), lambda i, j, k: (j, k)),
        [], [pl.BlockSpec((tm, tn), lambda i, j, k: (i, j))],
        (tm, tn), NT, _ep_store(F32))[0]


def mm_tn(a, dy, name, shard_cols=None, tn=MM_TN):
    t, kdim = a.shape
    n = dy.shape[1]
    tm, tk = min(kdim, MM_TM), min(t, MM_TK)
    if shard_cols is None:
        tn = min(tn, n)
        out = _sds((kdim, n), BF16)
        o_spec = pl.BlockSpec((tm, tn), lambda i, j, k: (i, j))
    else:
        tn = shard_cols
        out = _sds((n // tn, kdim, tn), BF16)
        o_spec = pl.BlockSpec((None, tm, tn), lambda i, j, k: (j, i, 0))
    return _mm_call(
        name, a, dy, [], [out], (kdim // tm, n // tn, t // tk),
        pl.BlockSpec((tk, tm), lambda i, j, k: (k, i)),
        pl.BlockSpec((tk, tn), lambda i, j, k: (k, j)),
        [], [o_spec], (tm, tn), TN, _ep_store(BF16))[0]


def _band_mask(g, nk_prev_valid, max_dist):
    rows = lax.broadcasted_iota(jnp.int32, (g * BLK, 2 * BLK), 0) % BLK
    cols = lax.broadcasted_iota(jnp.int32, (g * BLK, 2 * BLK), 1)
    dist = rows + BLK - cols
    ok = (dist >= 0) & (dist <= max_dist)
    return ok & ((cols >= BLK) | nk_prev_valid)


def band_fwd(qkv, q0, k0, v0, hk, g, seg, max_dist, name, sink_rows=None):
    t, dh = qkv.shape[1], qkv.shape[2]
    nb = t // BLK
    scale = dh ** -0.5
    has_sink = sink_rows is not None

    def kern(*refs):
        if has_sink:
            q_ref, k_ref, v_ref, s_ref, num_ref, m_ref, l_ref = refs
            sink = s_ref[...]
        else:
            q_ref, k_ref, v_ref, num_ref, m_ref, l_ref = refs
        b = pl.program_id(1)
        cur = pl.multiple_of(b * BLK, BLK)
        prev = pl.multiple_of(jnp.maximum(b - 1, 0) * BLK, BLK)
        q = q_ref[...].reshape(g * BLK, dh)
        kk = jnp.concatenate([k_ref[pl.ds(prev, BLK), :], k_ref[pl.ds(cur, BLK), :]], axis=0)
        vv = jnp.concatenate([v_ref[pl.ds(prev, BLK), :], v_ref[pl.ds(cur, BLK), :]], axis=0)
        s = lax.dot_general(q, kk, NT, preferred_element_type=F32) * scale
        s = jnp.where(_band_mask(g, (b % seg) != 0, max_dist), s, -jnp.inf)
        m = jnp.max(s, axis=-1, keepdims=True)
        if has_sink:
            m = jnp.maximum(m, sink)
        p = jnp.exp(s - m)
        l = jnp.sum(p, axis=-1, keepdims=True)
        if has_sink:
            l = l + jnp.exp(sink - m)
        num = jnp.dot(p.astype(BF16), vv, preferred_element_type=F32)
        num_ref[...] = num.reshape(g, BLK, dh)
        m_ref[...] = m.reshape(g, BLK, 1)
        l_ref[...] = l.reshape(g, BLK, 1)

    in_specs = [pl.BlockSpec((g, BLK, dh), lambda h, b: (q0 // g + h, b, 0)),
                pl.BlockSpec((None, t, dh), lambda h, b: (k0 + h, 0, 0)),
                pl.BlockSpec((None, t, dh), lambda h, b: (v0 + h, 0, 0))]
    args = [qkv, qkv, qkv]
    if has_sink:
        in_specs.append(pl.BlockSpec((None, g * BLK, 1), lambda h, b: (h, 0, 0)))
        args.append(sink_rows)
    hq = hk * g
    return pl.pallas_call(
        kern, name=name, grid=(hk, nb), in_specs=in_specs,
        out_specs=[pl.BlockSpec((g, BLK, dh), lambda h, b: (h, b, 0)),
                   pl.BlockSpec((g, BLK, 1), lambda h, b: (h, b, 0)),
                   pl.BlockSpec((g, BLK, 1), lambda h, b: (h, b, 0))],
        out_shape=[_sds((hq, t, dh), F32), _sds((hq, t, 1), F32), _sds((hq, t, 1), F32)],
        compiler_params=_params("parallel", "parallel"),
    )(*args)


def band_bwd(qkv, q0, k0, v0, do, lse, delta, hk, g, seg, max_dist, name, sink_rows=None):
    t, dh = qkv.shape[1], qkv.shape[2]
    nb = t // BLK
    scale = dh ** -0.5
    has_sink = sink_rows is not None

    def kern(*refs):
        if has_sink:
            (q_ref, k_ref, v_ref, do_ref, lse_ref, dl_ref, s_ref,
             dq_ref, dk_ref, dv_ref, ds_ref, sacc) = refs
            sink = s_ref[...]
        else:
            q_ref, k_ref, v_ref, do_ref, lse_ref, dl_ref, dq_ref, dk_ref, dv_ref = refs
        b = pl.program_id(1)

        @pl.when(b == 0)
        def _():
            dk_ref[...] = jnp.zeros_like(dk_ref)
            dv_ref[...] = jnp.zeros_like(dv_ref)
            if has_sink:
                sacc[...] = jnp.zeros_like(sacc)

        cur = pl.multiple_of(b * BLK, BLK)
        prev = pl.multiple_of(jnp.maximum(b - 1, 0) * BLK, BLK)
        q = q_ref[...].reshape(g * BLK, dh)
        dout = do_ref[...].reshape(g * BLK, dh)
        lse_b = lse_ref[...].reshape(g * BLK, 1)
        dl_b = dl_ref[...].reshape(g * BLK, 1)
        kk = jnp.concatenate([k_ref[pl.ds(prev, BLK), :], k_ref[pl.ds(cur, BLK), :]], axis=0)
        vv = jnp.concatenate([v_ref[pl.ds(prev, BLK), :], v_ref[pl.ds(cur, BLK), :]], axis=0)
        s = lax.dot_general(q, kk, NT, preferred_element_type=F32) * scale
        s = jnp.where(_band_mask(g, (b % seg) != 0, max_dist), s, -jnp.inf)
        p = jnp.exp(s - lse_b)
        dp = lax.dot_general(dout, vv, NT, preferred_element_type=F32)
        ds = (p * (dp - dl_b) * scale).astype(BF16)
        dq = jnp.dot(ds, kk, preferred_element_type=F32)
        dq_ref[...] = dq.reshape(g, BLK, dh)
        dkk = lax.dot_general(ds, q, TN, preferred_element_type=F32)
        dvv = lax.dot_general(p.astype(BF16), dout, TN, preferred_element_type=F32)
        dk_ref[pl.ds(prev, BLK), :] += dkk[:BLK]
        dk_ref[pl.ds(cur, BLK), :] += dkk[BLK:]
        dv_ref[pl.ds(prev, BLK), :] += dvv[:BLK]
        dv_ref[pl.ds(cur, BLK), :] += dvv[BLK:]
        if has_sink:
            sacc[...] += -jnp.exp(sink - lse_b) * dl_b

            @pl.when(b == nb - 1)
            def _():
                for gi in range(g):
                    ds_ref[gi:gi + 1, :] = jnp.sum(sacc[gi * BLK:(gi + 1) * BLK, :], axis=0,
                                                   keepdims=True)

    in_specs = [pl.BlockSpec((g, BLK, dh), lambda h, b: (q0 // g + h, b, 0)),
                pl.BlockSpec((None, t, dh), lambda h, b: (k0 + h, 0, 0)),
                pl.BlockSpec((None, t, dh), lambda h, b: (v0 + h, 0, 0)),
                pl.BlockSpec((g, BLK, dh), lambda h, b: (h, b, 0)),
                pl.BlockSpec((g, BLK, 1), lambda h, b: (h, b, 0)),
                pl.BlockSpec((g, BLK, 1), lambda h, b: (h, b, 0))]
    args = [qkv, qkv, qkv, do, lse, delta]
    hq = hk * g
    out_specs = [pl.BlockSpec((g, BLK, dh), lambda h, b: (h, b, 0)),
                 pl.BlockSpec((None, t, dh), lambda h, b: (h, 0, 0)),
                 pl.BlockSpec((None, t, dh), lambda h, b: (h, 0, 0))]
    out_shape = [_sds((hq, t, dh), F32), _sds((hk, t, dh), F32), _sds((hk, t, dh), F32)]
    scratch = []
    if has_sink:
        in_specs.append(pl.BlockSpec((None, g * BLK, 1), lambda h, b: (h, 0, 0)))
        args.append(sink_rows)
        out_specs.append(pl.BlockSpec((None, g, 1), lambda h, b: (h, 0, 0)))
        out_shape.append(_sds((hk, g, 1), F32))
        scratch.append(pltpu.VMEM((g * BLK, 1), F32))
    return pl.pallas_call(
        kern, name=name, grid=(hk, nb), in_specs=in_specs, out_specs=out_specs, out_shape=out_shape,
        scratch_shapes=scratch, compiler_params=_params("parallel", "arbitrary"),
    )(*args)


def merge_branches(nums, ms, ls, name):
    h, t, dh = nums[0].shape
    nbr = len(nums)

    def kern(*refs):
        num_refs, m_refs, l_refs = refs[:nbr], refs[nbr:2 * nbr], refs[2 * nbr:3 * nbr]
        o_ref, lse_ref = refs[3 * nbr], refs[3 * nbr + 1]
        mall = m_refs[0][...]
        for i in range(1, nbr):
            mall = jnp.maximum(mall, m_refs[i][...])
        num = jnp.zeros((t, dh), F32)
        den = jnp.zeros((t, 1), F32)
        for i in range(nbr):
            w = jnp.exp(m_refs[i][...] - mall)
            num = num + w * num_refs[i][...]
            den = den + w * l_refs[i][...]
        o_ref[...] = num / den
        lse_ref[...] = mall + jnp.log(den)

    big = pl.BlockSpec((None, t, dh), lambda i: (i, 0, 0))
    col = pl.BlockSpec((None, t, 1), lambda i: (i, 0, 0))
    return pl.pallas_call(
        kern, name=name, grid=(h,), in_specs=[big] * nbr + [col] * (2 * nbr),
        out_specs=[big, col], out_shape=[_sds((h, t, dh), F32), _sds((h, t, 1), F32)],
        compiler_params=_params("parallel"),
    )(*nums, *ms, *ls)


def normalise_heads(num, m, l, name):
    h, t, dh = num.shape

    def kern(num_ref, m_ref, l_ref, o_ref, lse_ref):
        lv = l_ref[...]
        o_ref[...] = num_ref[...] / lv
        lse_ref[...] = m_ref[...] + jnp.log(lv)

    big = pl.BlockSpec((None, t, dh), lambda i: (i, 0, 0))
    col = pl.BlockSpec((None, t, 1), lambda i: (i, 0, 0))
    return pl.pallas_call(
        kern, name=name, grid=(h,), in_specs=[big, col, col], out_specs=[big, col],
        out_shape=[_sds((h, t, dh), F32), _sds((h, t, 1), F32)],
        compiler_params=_params("parallel"),
    )(num, m, l)


def head_delta(o, do, name):
    h, t, dh = o.shape

    def kern(o_ref, do_ref, d_ref):
        d_ref[...] = jnp.sum(o_ref[...] * do_ref[...], axis=-1, keepdims=True)

    big = pl.BlockSpec((None, t, dh), lambda i: (i, 0, 0))
    return pl.pallas_call(
        kern, name=name, grid=(h,), in_specs=[big, big],
        out_specs=pl.BlockSpec((None, t, 1), lambda i: (i, 0, 0)),
        out_shape=_sds((h, t, 1), F32), compiler_params=_params("parallel"),
    )(o, do)


def _cumsum_rows(x, n, reverse=False):
    rows = lax.broadcasted_iota(jnp.int32, x.shape, 0)
    shift = 1
    while shift < n:
        if reverse:
            x = x + jnp.where(rows < n - shift, pltpu.roll(x, n - shift, 0), 0.0)
        else:
            x = x + jnp.where(rows >= shift, pltpu.roll(x, shift, 0), 0.0)
        shift *= 2
    return x


def _hgrn_gates(f, lb):
    sig = _sigmoid(f)
    gate = lb + (1.0 - lb) * sig
    return sig, gate


B_SUB = 16


def _dot3(a, b, dims):
    ah, bh = a.astype(BF16), b.astype(BF16)
    al = (a - ah.astype(F32)).astype(BF16)
    bl = (b - bh.astype(F32)).astype(BF16)
    dot = functools.partial(lax.dot_general, dimension_numbers=dims, preferred_element_type=F32)
    return dot(ah, bh) + dot(al, bh) + dot(ah, bl)


def _sub_scales(b, i):
    r0 = i * B_SUB
    beta = b[r0 - 1:r0, :]
    return jnp.exp(b[r0:r0 + B_SUB, :] - beta), jnp.exp(jnp.minimum(beta - b, 0.0))


def _hgrn_intra_attn(qq, kk, b):
    c = qq.shape[0]
    lane = lax.broadcasted_iota(jnp.int32, (B_SUB, c), 1)
    trow = lax.broadcasted_iota(jnp.int32, (B_SUB, B_DIM), 0)
    blocks = []
    for i in range(c // B_SUB):
        r0 = i * B_SUB
        qi, bi = qq[r0:r0 + B_SUB, :], b[r0:r0 + B_SUB, :]
        if i == 0:
            a_i = jnp.zeros((B_SUB, c), F32)
        else:
            eq, ek = _sub_scales(b, i)
            a_i = jnp.where(lane < r0, _dot3(qi * eq, kk * ek, NT), 0.0)
        for sl in range(B_SUB):
            s = r0 + sl
            e = jnp.exp(jnp.where(trow >= sl, bi - b[s:s + 1, :], -jnp.inf))
            col = jnp.sum(qi * kk[s:s + 1, :] * e, axis=1, keepdims=True)
            a_i = jnp.where(lane == s, col, a_i)
        blocks.append(a_i)
    return jnp.concatenate(blocks, axis=0)


def hgrn_fwd(proj, col0, nh, lb, gn, name):
    t = proj.shape[0]
    c = B_CHUNK
    nc = t // c
    scale = B_DIM ** -0.5

    def kern(q_ref, f_ref, i_ref, g_ref, lb_ref, gn_ref, out_ref, opre_ref, st_ref, a_ref, state):
        lbv = lb_ref[...]
        gnv = gn_ref[...]
        state[...] = jnp.zeros_like(state)

        def chunk(ci, carry):
            rows = pl.ds(pl.multiple_of(ci * c, c), c)
            _, gate = _hgrn_gates(f_ref[rows, :], lbv)
            kk = 1.0 - gate
            qb = q_ref[rows, :]
            qq = qb * _sigmoid(qb) * scale
            v = i_ref[rows, :]
            b = _cumsum_rows(jnp.log(gate), c)
            st = state[...]
            st_ref[ci] = st
            o_inter = lax.dot_general((qq * jnp.exp(b)).astype(BF16), st.astype(BF16), NT,
                                      preferred_element_type=F32)
            amat = _hgrn_intra_attn(qq, kk, b)
            a_ref[ci] = amat
            o = jnp.dot(amat.astype(BF16), v.astype(BF16), preferred_element_type=F32) + o_inter
            opre_ref[rows, :] = o
            bl = b[c - 1:c, :]
            state[...] = st * jnp.exp(bl) + lax.dot_general(
                v.astype(BF16), (kk * jnp.exp(bl - b)).astype(BF16), TN, preferred_element_type=F32)
            r = lax.rsqrt(jnp.mean(o * o, axis=-1, keepdims=True) + NORM_EPS)
            gb = g_ref[rows, :]
            out_ref[rows, :] = (o * r * gnv * (gb * _sigmoid(gb))).astype(BF16)
            return carry

        lax.fori_loop(0, nc, chunk, 0)

    def col(off):
        return pl.BlockSpec((t, B_DIM), lambda h: (0, col0 + off * nh + h))

    return pl.pallas_call(
        kern, name=name, grid=(nh,),
        in_specs=[col(0), col(1), col(2), col(3),
                  pl.BlockSpec((None, 1, B_DIM), lambda h: (h, 0, 0)),
                  pl.BlockSpec((1, B_DIM), lambda h: (0, 0))],
        out_specs=[pl.BlockSpec((t, B_DIM), lambda h: (0, h)),
                   pl.BlockSpec((t, B_DIM), lambda h: (0, h)),
                   pl.BlockSpec((None, nc, B_DIM, B_DIM), lambda h: (h, 0, 0, 0)),
                   pl.BlockSpec((None, nc, c, c), lambda h: (h, 0, 0, 0))],
        out_shape=[_sds((t, nh * B_DIM), BF16), _sds((t, nh * B_DIM), F32),
                   _sds((nh, nc, B_DIM, B_DIM), F32), _sds((nh, nc, c, c), F32)],
        scratch_shapes=[pltpu.VMEM((B_DIM, B_DIM), F32)],
        compiler_params=_params("parallel"),
    )(proj, proj, proj, proj, lb, gn)


def hgrn_bwd(proj, col0, nh, lb, gn, opre, states, amats, dout, dcol0, name):
    t = proj.shape[0]
    c = B_CHUNK
    nc = t // c
    scale = B_DIM ** -0.5
    nsub = c // B_SUB

    def kern(q_ref, f_ref, i_ref, g_ref, lb_ref, gn_ref, opre_ref, st_ref, a_ref, dout_ref,
             dq_ref, df_ref, di_ref, dg_ref, dgn_ref, dlb_ref, dstate, dksc):
        lbv = lb_ref[...]
        gnv = gn_ref[...]
        dstate[...] = jnp.zeros_like(dstate)
        dlb_ref[...] = jnp.zeros_like(dlb_ref)

        @pl.when(pl.program_id(0) == 0)
        def _():
            dgn_ref[...] = jnp.zeros_like(dgn_ref)

        srow = lax.broadcasted_iota(jnp.int32, (c, B_DIM), 0)
        lane = lax.broadcasted_iota(jnp.int32, (B_SUB, c), 1)
        trow = lax.broadcasted_iota(jnp.int32, (B_SUB, B_DIM), 0)
        arow = lax.broadcasted_iota(jnp.int32, (c, c), 0)
        alane = lax.broadcasted_iota(jnp.int32, (c, c), 1)

        def chunk(cj, carry):
            ci = nc - 1 - cj
            rows = pl.ds(pl.multiple_of(ci * c, c), c)
            f = f_ref[rows, :]
            sig, gate = _hgrn_gates(f, lbv)
            kk = 1.0 - gate
            qb = q_ref[rows, :]
            sq = _sigmoid(qb)
            qq = qb * sq * scale
            v = i_ref[rows, :]
            b = _cumsum_rows(jnp.log(gate), c)
            st0 = st_ref[ci]
            dst = dstate[...]
            o = opre_ref[rows, :]
            gb = g_ref[rows, :]
            sg = _sigmoid(gb)
            silu_g = gb * sg
            d_out = dout_ref[rows, :]
            r = lax.rsqrt(jnp.mean(o * o, axis=-1, keepdims=True) + NORM_EPS)
            y = o * r
            dg_ref[rows, :] = (d_out * y * gnv * (sg * (1.0 + gb * (1.0 - sg)))).astype(BF16)
            dyn = d_out * silu_g
            dgn_ref[...] += jnp.sum(dyn * y, axis=0, keepdims=True)
            dy = dyn * gnv
            do = r * (dy - y * jnp.mean(dy * y, axis=-1, keepdims=True))
            eb = jnp.exp(b)
            bl = b[c - 1:c, :]
            ebl = jnp.exp(bl - b)
            ebl_last = jnp.exp(bl)
            do_b = do.astype(BF16)
            dst_b = dst.astype(BF16)
            dq_inter = jnp.dot(do_b, st0.astype(BF16), preferred_element_type=F32) * eb
            dst0 = lax.dot_general(do_b, (qq * eb).astype(BF16), TN,
                                   preferred_element_type=F32) + dst * ebl_last
            dv_inter = lax.dot_general((kk * ebl).astype(BF16), dst_b, NT, preferred_element_type=F32)
            dk_inter = jnp.dot(v.astype(BF16), dst_b, preferred_element_type=F32) * ebl
            amat = a_ref[ci]
            v_b = v.astype(BF16)
            d_a = lax.dot_general(do_b, v_b, NT, preferred_element_type=F32)
            d_a = jnp.where(arow >= alane, d_a, 0.0)
            dv_intra = lax.dot_general(amat.astype(BF16), do_b, TN, preferred_element_type=F32)
            dk_pairs = jnp.zeros((c, B_DIM), F32)
            dq_blocks = []
            for i in range(nsub):
                r0 = i * B_SUB
                qi, bi = qq[r0:r0 + B_SUB, :], b[r0:r0 + B_SUB, :]
                da_i = d_a[r0:r0 + B_SUB, :]
                if i == 0:
                    dq_i = jnp.zeros((B_SUB, B_DIM), F32)
                else:
                    eq, ek = _sub_scales(b, i)
                    da_m = jnp.where(lane < r0, da_i, 0.0)
                    dq_i = _dot3(da_m, kk * ek, NN) * eq
                    dk_pairs = dk_pairs + _dot3(da_m, qi * eq, TN) * ek
                for sl in range(B_SUB):
                    s = r0 + sl
                    e = jnp.exp(jnp.where(trow >= sl, bi - b[s:s + 1, :], -jnp.inf))
                    dacol = jnp.sum(jnp.where(lane == s, da_i, 0.0), axis=1, keepdims=True)
                    w = dacol * e
                    dq_i = dq_i + w * kk[s:s + 1, :]
                    dksc[s:s + 1, :] = jnp.sum(w * qi, axis=0, keepdims=True)
                dq_blocks.append(dq_i)
            dq = jnp.concatenate(dq_blocks, axis=0) + dq_inter
            dk = dk_pairs + dksc[...] + dk_inter
            dv = dv_intra + dv_inter
            db = qq * dq - kk * dk
            extra = (jnp.sum(kk * dk_inter, axis=0, keepdims=True)
                     + ebl_last * jnp.sum(st0 * dst, axis=0, keepdims=True))
            db = db + jnp.where(srow == c - 1, extra, 0.0)
            dlog = _cumsum_rows(db, c, reverse=True)
            dgate = dlog / gate - dk
            df_ref[rows, :] = (dgate * (1.0 - lbv) * sig * (1.0 - sig)).astype(BF16)
            dlb_ref[...] += jnp.sum(dgate * (1.0 - sig), axis=0, keepdims=True)
            dq_ref[rows, :] = (dq * scale * (sq * (1.0 + qb * (1.0 - sq)))).astype(BF16)
            di_ref[rows, :] = dv.astype(BF16)
            dstate[...] = dst0
            return carry

        lax.fori_loop(0, nc, chunk, 0)

    def col(off):
        return pl.BlockSpec((t, B_DIM), lambda h: (0, col0 + off * nh + h))

    hcol = pl.BlockSpec((t, B_DIM), lambda h: (0, h))
    vec = pl.BlockSpec((None, 1, B_DIM), lambda h: (h, 0, 0))
    wide = _sds((t, nh * B_DIM), BF16)
    return pl.pallas_call(
        kern, name=name, grid=(nh,),
        in_specs=[col(0), col(1), col(2), col(3), vec,
                  pl.BlockSpec((1, B_DIM), lambda h: (0, 0)), hcol,
                  pl.BlockSpec((None, nc, B_DIM, B_DIM), lambda h: (h, 0, 0, 0)),
                  pl.BlockSpec((None, nc, c, c), lambda h: (h, 0, 0, 0)),
                  pl.BlockSpec((t, B_DIM), lambda h: (0, dcol0 + h))],
        out_specs=[hcol, hcol, hcol, hcol, pl.BlockSpec((1, B_DIM), lambda h: (0, 0)), vec],
        out_shape=[wide, wide, wide, wide, _sds((1, B_DIM), F32), _sds((nh, 1, B_DIM), F32)],
        scratch_shapes=[pltpu.VMEM((B_DIM, B_DIM), F32), pltpu.VMEM((c, B_DIM), F32)],
        compiler_params=_params("arbitrary"),
    )(proj, proj, proj, proj, lb, gn, opre, states, amats, dout)


def lower_bounds_fwd(raw, name):
    n, w = raw.shape

    def kern(raw_ref, lb_ref, soft_ref):
        r = raw_ref[...]
        mx = r[0:1]
        for i in range(1, n):
            mx = jnp.maximum(mx, r[i:i + 1])
        e = jnp.exp(r - mx)
        den = e[0:1]
        for i in range(1, n):
            den = den + e[i:i + 1]
        soft = e / den
        soft_ref[...] = soft
        run = soft[0:1]
        lb_ref[0:1, :] = run - soft[0:1]
        for i in range(1, n):
            run = run + soft[i:i + 1]
            lb_ref[i:i + 1, :] = run - soft[0:1]

    return pl.pallas_call(kern, name=name, out_shape=[_sds((n, w), F32), _sds((n, w), F32)])(raw)


def lower_bounds_bwd(soft, dlb, name):
    n, w = soft.shape

    def kern(soft_ref, dlb_ref, out_ref):
        s = soft_ref[...]
        d = dlb_ref[...]
        total = d[0:1]
        for i in range(1, n):
            total = total + d[i:i + 1]
        us = []
        tail = total
        for i in range(n):
            us.append(tail - total if i == 0 else tail)
            tail = tail - d[i:i + 1]
        dot = s[0:1] * us[0]
        for i in range(1, n):
            dot = dot + s[i:i + 1] * us[i]
        for i in range(n):
            out_ref[i:i + 1, :] = s[i:i + 1] * (us[i] - dot)

    return pl.pallas_call(kern, name=name, out_shape=_sds((n, w), F32))(soft, dlb)


def _row_tile(kdim, n):
    tk = 512
    while tk > 8 and tk * n > 256 * 1024:
        tk //= 2
    return min(kdim, tk)


def _adam_update(w, g, m, v):
    m2 = ADAM_B1 * m + (1.0 - ADAM_B1) * g
    v2 = ADAM_B2 * v + (1.0 - ADAM_B2) * (g * g)
    m_hat = m2 / (1.0 - ADAM_B1 ** ADAM_STEP)
    v_hat = v2 / (1.0 - ADAM_B2 ** ADAM_STEP)
    delta = -ADAM_LR * (m_hat / (jnp.sqrt(v_hat) + ADAM_EPS) + ADAM_WD * w)
    return delta, m2, v2


def adamw_small(w, g, m, v, name):
    def kern(w_ref, g_ref, m_ref, v_ref, d_ref, m2_ref, v2_ref):
        d, m2, v2 = _adam_update(w_ref[...], g_ref[...], m_ref[...], v_ref[...])
        d_ref[...] = d
        m2_ref[...] = m2
        v2_ref[...] = v2

    return pl.pallas_call(kern, name=name, out_shape=[_sds(w.shape, F32)] * 3)(w, g, m, v)


def adamw_big(parts, w, m, v, name):
    nl, kdim, n = w.shape
    tk = _row_tile(kdim, n)

    def kern(p_ref, w_ref, m_ref, v_ref, g_ref, d_ref, m2_ref, v2_ref):
        g = p_ref[0].astype(F32)
        for q in range(1, 4):
            g = g + p_ref[q].astype(F32)
        d, m2, v2 = _adam_update(w_ref[...], g, m_ref[...], v_ref[...])
        g_ref[...] = g
        d_ref[...] = d
        m2_ref[...] = m2
        v2_ref[...] = v2

    blk = pl.BlockSpec((None, tk, n), lambda l, i: (l, i, 0))
    return pl.pallas_call(
        kern, name=name, grid=(nl, kdim // tk),
        in_specs=[pl.BlockSpec((None, 4, tk, n), lambda l, i: (l, 0, i, 0)), blk, blk, blk],
        out_specs=[blk] * 4, out_shape=[_sds(w.shape, F32)] * 4,
        compiler_params=_params("parallel", "parallel"),
    )(parts, w, m, v)


def cast_bf16(w, name):
    nl, kdim, n = w.shape
    tk = _row_tile(kdim, n)

    def kern(w_ref, o_ref):
        o_ref[...] = w_ref[...].astype(BF16)

    blk = pl.BlockSpec((None, tk, n), lambda l, i: (l, i, 0))
    return pl.pallas_call(
        kern, name=name, grid=(nl, kdim // tk), in_specs=[blk], out_specs=blk,
        out_shape=_sds(w.shape, BF16), compiler_params=_params("parallel", "parallel"),
    )(w)


def pair_add(dw, r1, core, name):
    kdim, n = dw.shape[1], dw.shape[2]
    tk = _row_tile(kdim, n)

    def kern(c_ref, a_ref, b_ref, o_ref):
        o_ref[...] = (a_ref[...].astype(F32) + b_ref[...].astype(F32)).astype(BF16)

    grid_spec = pltpu.PrefetchScalarGridSpec(
        num_scalar_prefetch=1, grid=(4, kdim // tk),
        in_specs=[pl.BlockSpec((None, tk, n), lambda p, i, c: (2 * p + c[0], i, 0)),
                  pl.BlockSpec((None, tk, n), lambda p, i, c: (p, i, 0))],
        out_specs=pl.BlockSpec((None, tk, n), lambda p, i, c: (p, i, 0)))
    return pl.pallas_call(
        kern, name=name, grid_spec=grid_spec, out_shape=_sds((4, kdim, n), BF16),
        compiler_params=_params("parallel", "parallel"),
    )(core, dw, r1)


ANY = pl.BlockSpec(memory_space=pl.ANY)


def _place():
    x, y, c = lax.axis_index("x"), lax.axis_index("y"), lax.axis_index("c")
    chips = [(1 - x, y), (x, 1 - y), (1 - x, 1 - y)]
    return x, y, c, chips


def all_gather(shards, name):
    n = len(shards)

    def kern(*refs):
        ins, outs = refs[:n], refs[n:2 * n]
        send_sems, recv_sems, local_sems = refs[2 * n:]
        x, y, c, chips = _place()
        me, sib = (x, y, c), (x, y, 1 - c)

        def copy(t, k, block, to, src=None):
            px, py, pc = block
            dst = outs[t].at[4 * px + 2 * py + pc]
            return pltpu.make_async_remote_copy(
                src_ref=dst if src is None else src, dst_ref=dst,
                send_sem=send_sems.at[7 * t + k], recv_sem=recv_sems.at[7 * t + k],
                device_id=to, device_id_type=MESH)

        mine = [pltpu.make_async_copy(ins[t], outs[t].at[4 * x + 2 * y + c], local_sems.at[t])
                for t in range(n)]
        for cp in mine:
            cp.start()
        first = []
        for t in range(n):
            first.append(copy(t, 0, me, sib, src=ins[t]))
            first += [copy(t, 1 + j, me, (*chip, c), src=ins[t]) for j, chip in enumerate(chips)]
        for cp in first:
            cp.start()
        passed = []
        for t in range(n):
            for j, chip in enumerate(chips):
                copy(t, 1 + j, (*chip, c), me).wait_recv()
                fwd = copy(t, 4 + j, (*chip, c), sib)
                fwd.start()
                passed.append(fwd)
        for t in range(n):
            copy(t, 0, sib, me).wait_recv()
            for j, chip in enumerate(chips):
                copy(t, 4 + j, (*chip, 1 - c), me).wait_recv()
        for cp in first + passed:
            cp.wait_send()
        for cp in mine:
            cp.wait()

    return pl.pallas_call(
        kern, name=name, in_specs=[ANY] * n, out_specs=[ANY] * n,
        out_shape=[_sds((N_DEV,) + s.shape, s.dtype) for s in shards],
        scratch_shapes=[pltpu.SemaphoreType.DMA((7 * n,)), pltpu.SemaphoreType.DMA((7 * n,)),
                        pltpu.SemaphoreType.DMA((n,))],
    )(*shards)


HBM = pl.BlockSpec(memory_space=pltpu.HBM)
SEM = pl.BlockSpec(memory_space=pltpu.SEMAPHORE)
DATAFLOW = pltpu.SideEffectType.DATAFLOW_SIDE_EFFECTING


def _first_level_targets():
    x, y, c, chips = _place()
    return 4 * x + 2 * y + c, [(x, y, 1 - c)] + [(*chip, c) for chip in chips]


def gather_start(shards, after, name):
    n = len(shards)
    lands = [lax.empty((N_DEV,) + s.shape, s.dtype) for s in shards]

    def kern(*refs):
        ins, lnd = refs[:n], refs[n:2 * n]
        send_sems, recv_sems = refs[2 * n + 1], refs[2 * n + 2]
        token = refs[-1]
        me, targets = _first_level_targets()
        for t in range(n):
            for k, to in enumerate(targets):
                pltpu.make_async_remote_copy(
                    src_ref=ins[t], dst_ref=lnd[t].at[me], send_sem=send_sems.at[4 * t + k],
                    recv_sem=recv_sems.at[4 * t + k], device_id=to, device_id_type=MESH).start()
        token[...] = jnp.zeros_like(token)

    args = [pltpu.with_memory_space_constraint(a, pltpu.HBM) for a in list(shards) + lands]
    return pl.pallas_call(
        kern, name=name,
        out_shape=(pltpu.SemaphoreType.DMA((4 * n,)), pltpu.SemaphoreType.DMA((4 * n,)),
                   *[pltpu.HBM(a.shape, a.dtype) for a in args], _sds((8, LANES), F32)),
        in_specs=[HBM] * (2 * n) + [ANY],
        out_specs=(SEM, SEM, *[HBM] * (2 * n), pl.BlockSpec(memory_space=pltpu.VMEM)),
        input_output_aliases={i: 2 + i for i in range(2 * n)},
        compiler_params=pltpu.CompilerParams(has_side_effects=DATAFLOW),
    )(*args, after)


def gather_wait(send_sems, recv_sems, shards, lands, after, name):
    n = len(shards)

    def kern(*refs):
        ins, lnd = refs[:n], refs[n:2 * n]
        send_sems, recv_sems = refs[2 * n], refs[2 * n + 1]
        me, targets = _first_level_targets()
        for t in range(n):
            for k, to in enumerate(targets):
                cp = pltpu.make_async_remote_copy(
                    src_ref=ins[t], dst_ref=lnd[t].at[me], send_sem=send_sems.at[4 * t + k],
                    recv_sem=recv_sems.at[4 * t + k], device_id=to, device_id_type=MESH)
                cp.wait_send()
                cp.wait_recv()

    bufs = list(shards) + list(lands)
    return pl.pallas_call(
        kern, name=name, out_shape=tuple(pltpu.HBM(a.shape, a.dtype) for a in bufs),
        in_specs=[HBM] * (2 * n) + [SEM, SEM, ANY], out_specs=[HBM] * (2 * n),
        input_output_aliases={i: i for i in range(2 * n)},
        compiler_params=pltpu.CompilerParams(has_side_effects=DATAFLOW),
    )(*bufs, send_sems, recv_sems, after)


def gather_forward(shards, lands, name):
    n = len(shards)

    def kern(*refs):
        ins, lnd = refs[:n], refs[2 * n:3 * n]
        send_sems, recv_sems, local_sems = refs[3 * n:]
        x, y, c, chips = _place()
        mine = [pltpu.make_async_copy(ins[t], lnd[t].at[4 * x + 2 * y + c], local_sems.at[t])
                for t in range(n)]
        for cp in mine:
            cp.start()
        passed = []
        for t in range(n):
            for j, (qx, qy) in enumerate(chips):
                block = lnd[t].at[4 * qx + 2 * qy + c]
                cp = pltpu.make_async_remote_copy(
                    src_ref=block, dst_ref=block, send_sem=send_sems.at[3 * t + j],
                    recv_sem=recv_sems.at[3 * t + j], device_id=(x, y, 1 - c), device_id_type=MESH)
                cp.start()
                passed.append(cp)
        for cp in passed:
            cp.wait_recv()
        for cp in passed:
            cp.wait_send()
        for cp in mine:
            cp.wait()

    return pl.pallas_call(
        kern, name=name, in_specs=[ANY] * (2 * n), out_specs=[ANY] * n,
        out_shape=[_sds(a.shape, a.dtype) for a in lands],
        input_output_aliases={n + i: i for i in range(n)},
        scratch_shapes=[pltpu.SemaphoreType.DMA((3 * n,)), pltpu.SemaphoreType.DMA((3 * n,)),
                        pltpu.SemaphoreType.DMA((n,))],
    )(*shards, *lands)


def all_reduce_small(vec, name):
    r = vec.shape[0]

    def kern(v_ref, o_ref, buf, send_sems, recv_sems):
        x, y, c, _ = _place()
        me = 4 * x + 2 * y + c
        peers = [(x, y, 1 - c), (1 - x, y, c), (x, 1 - y, c), (1 - x, 1 - y, c),
                 (1 - x, y, 1 - c), (x, 1 - y, 1 - c), (1 - x, 1 - y, 1 - c)]
        buf[me] = v_ref[...]
        copies = []
        for k, peer in enumerate(peers):
            cp = pltpu.make_async_remote_copy(
                src_ref=v_ref, dst_ref=buf.at[me], send_sem=send_sems.at[k],
                recv_sem=recv_sems.at[k], device_id=peer, device_id_type=MESH)
            cp.start()
            copies.append(cp)
        for cp in copies:
            cp.wait_recv()
        for cp in copies:
            cp.wait_send()
        total = buf[0]
        for d in range(1, N_DEV):
            total = total + buf[d]
        o_ref[...] = total

    vm = pl.BlockSpec(memory_space=pltpu.VMEM)
    return pl.pallas_call(
        kern, name=name, in_specs=[vm], out_specs=vm, out_shape=_sds(vec.shape, F32),
        scratch_shapes=[pltpu.VMEM((N_DEV, r, LANES), F32), pltpu.SemaphoreType.DMA((7,)),
                        pltpu.SemaphoreType.DMA((7,))],
    )(vec)


def exchange_with_sibling(grads, name):
    n = len(grads)

    def kern(*refs):
        ins, outs = refs[:n], refs[n:2 * n]
        send_sems, recv_sems = refs[2 * n:]
        x, y, c, _ = _place()
        copies = []
        for t in range(n):
            for p in range(4):
                cp = pltpu.make_async_remote_copy(
                    src_ref=ins[t].at[2 * p + 1 - c], dst_ref=outs[t].at[p],
                    send_sem=send_sems.at[4 * t + p], recv_sem=recv_sems.at[4 * t + p],
                    device_id=(x, y, 1 - c), device_id_type=MESH)
                cp.start()
                copies.append(cp)
        for cp in copies:
            cp.wait_recv()
        for cp in copies:
            cp.wait_send()

    return pl.pallas_call(
        kern, name=name, in_specs=[ANY] * n, out_specs=[ANY] * n,
        out_shape=[_sds((4,) + g.shape[1:], g.dtype) for g in grads],
        scratch_shapes=[pltpu.SemaphoreType.DMA((4 * n,)), pltpu.SemaphoreType.DMA((4 * n,))],
    )(*grads)


def exchange_between_chips(partials, layers, kinds, name):
    n = len(partials)
    n_kind = max(kinds) + 1
    shapes = []
    for kd in range(n_kind):
        idx = [i for i in range(n) if kinds[i] == kd]
        nl = max(layers[i] for i in idx) + 1
        shapes.append(_sds((nl,) + partials[idx[0]].shape, partials[idx[0]].dtype))

    def kern(*refs):
        ins, outs = refs[:n], refs[n:n + n_kind]
        send_sems, recv_sems, local_sems = refs[n + n_kind:]
        x, y, c, chips = _place()
        mine = 2 * x + y
        local = []
        copies = []
        for t in range(n):
            dst = outs[kinds[t]].at[layers[t], mine]
            lc = pltpu.make_async_copy(ins[t].at[mine], dst, local_sems.at[t])
            lc.start()
            local.append(lc)
            for j, (qx, qy) in enumerate(chips):
                cp = pltpu.make_async_remote_copy(
                    src_ref=ins[t].at[2 * qx + qy], dst_ref=dst,
                    send_sem=send_sems.at[3 * t + j], recv_sem=recv_sems.at[3 * t + j],
                    device_id=(qx, qy, c), device_id_type=MESH)
                cp.start()
                copies.append(cp)
        for cp in copies:
            cp.wait_recv()
        for cp in copies:
            cp.wait_send()
        for lc in local:
            lc.wait()

    return pl.pallas_call(
        kern, name=name, in_specs=[ANY] * n, out_specs=[ANY] * n_kind, out_shape=shapes,
        scratch_shapes=[pltpu.SemaphoreType.DMA((3 * n,)), pltpu.SemaphoreType.DMA((3 * n,)),
                        pltpu.SemaphoreType.DMA((n,))],
    )(*partials)


def scatter_start(partials, name):
    n = len(partials)
    lands = [lax.empty(p.shape, p.dtype) for p in partials]

    def kern(*refs):
        ins, lnd = refs[:n], refs[n:2 * n]
        send_sems, recv_sems, local_sems = refs[2 * n:2 * n + 3]
        token = refs[-1]
        x, y, c, chips = _place()
        mine = 2 * x + y
        for t in range(n):
            pltpu.make_async_copy(ins[t].at[mine], lnd[t].at[mine], local_sems.at[t]).start()
            for j, (qx, qy) in enumerate(chips):
                pltpu.make_async_remote_copy(
                    src_ref=ins[t].at[2 * qx + qy], dst_ref=lnd[t].at[mine],
                    send_sem=send_sems.at[3 * t + j], recv_sem=recv_sems.at[3 * t + j],
                    device_id=(qx, qy, c), device_id_type=MESH).start()
        token[...] = jnp.zeros_like(token)

    args = [pltpu.with_memory_space_constraint(a, pltpu.HBM) for a in list(partials) + lands]
    return pl.pallas_call(
        kern, name=name,
        out_shape=(pltpu.SemaphoreType.DMA((3 * n,)), pltpu.SemaphoreType.DMA((3 * n,)),
                   pltpu.SemaphoreType.DMA((n,)),
                   *[pltpu.HBM(a.shape, a.dtype) for a in args], _sds((8, LANES), F32)),
        in_specs=[HBM] * (2 * n),
        out_specs=(SEM, SEM, SEM, *[HBM] * (2 * n), pl.BlockSpec(memory_space=pltpu.VMEM)),
        input_output_aliases={i: 3 + i for i in range(2 * n)},
        compiler_params=pltpu.CompilerParams(has_side_effects=DATAFLOW),
    )(*args)


def scatter_wait(send_sems, recv_sems, local_sems, partials, lands, after, name):
    n = len(partials)

    def kern(*refs):
        ins, lnd = refs[:n], refs[n:2 * n]
        send_sems, recv_sems, local_sems = refs[2 * n:2 * n + 3]
        x, y, c, chips = _place()
        mine = 2 * x + y
        for t in range(n):
            pltpu.make_async_copy(ins[t].at[mine], lnd[t].at[mine], local_sems.at[t]).wait()
            for j, (qx, qy) in enumerate(chips):
                cp = pltpu.make_async_remote_copy(
                    src_ref=ins[t].at[2 * qx + qy], dst_ref=lnd[t].at[mine],
                    send_sem=send_sems.at[3 * t + j], recv_sem=recv_sems.at[3 * t + j],
                    device_id=(qx, qy, c), device_id_type=MESH)
                cp.wait_send()
                cp.wait_recv()

    bufs = list(partials) + list(lands)
    outs = pl.pallas_call(
        kern, name=name, out_shape=tuple(pltpu.HBM(a.shape, a.dtype) for a in bufs),
        in_specs=[HBM] * (2 * n) + [SEM, SEM, SEM, ANY], out_specs=[HBM] * (2 * n),
        input_output_aliases={i: i for i in range(2 * n)},
        compiler_params=pltpu.CompilerParams(has_side_effects=DATAFLOW),
    )(*bufs, send_sems, recv_sems, local_sems, after)
    return outs[n:]


def adamw_layers(parts, w, m, v, name):
    nl, kdim, n = w.shape
    tk = _row_tile(kdim, n)

    def kern(*refs):
        p_refs = refs[:nl]
        w_ref, m_ref, v_ref, g_ref, d_ref, m2_ref, v2_ref = refs[nl:]
        for l in range(nl):
            @pl.when(pl.program_id(0) == l)
            def _():
                g = p_refs[l][0].astype(F32)
                for q in range(1, 4):
                    g = g + p_refs[l][q].astype(F32)
                d, m2, v2 = _adam_update(w_ref[...], g, m_ref[...], v_ref[...])
                g_ref[...] = g
                d_ref[...] = d
                m2_ref[...] = m2
                v2_ref[...] = v2

    def part_spec(l):
        return pl.BlockSpec((4, tk, n), lambda li, i: (0, jnp.where(li == l, i, 0), 0))

    blk = pl.BlockSpec((None, tk, n), lambda li, i: (li, i, 0))
    return pl.pallas_call(
        kern, name=name, grid=(nl, kdim // tk),
        in_specs=[part_spec(l) for l in range(nl)] + [blk, blk, blk],
        out_specs=[blk] * 4, out_shape=[_sds(w.shape, F32)] * 4,
        compiler_params=_params("arbitrary", "arbitrary"),
    )(*parts, w, m, v)


def _pack(arrays):
    flat = jnp.concatenate([a.reshape(-1).astype(F32) for a in arrays])
    pad = (-flat.shape[0]) % (8 * LANES)
    return jnp.pad(flat, (0, pad)).reshape(-1, LANES)


def _unpack(packed, shapes):
    flat = packed.reshape(-1)
    out, off = [], 0
    for s in shapes:
        n = math.prod(s)
        out.append(flat[off:off + n].reshape(s))
        off += n
    return out


def _to_heads(x2d, dil, n_heads, dh):
    t = x2d.shape[0]
    return x2d.reshape(t // dil, dil, n_heads, dh).transpose(2, 1, 0, 3).reshape(n_heads, t, dh)


def _from_heads(xh, dil):
    h, t, w = xh.shape
    return xh.reshape(h, dil, t // dil, w).transpose(2, 1, 0, 3).reshape(t, h * w)


def _unperm(xh, dil):
    h, t, w = xh.shape
    return xh.reshape(h, dil, t // dil, w).transpose(0, 2, 1, 3).reshape(h, t, w)


def _perm(xh, dil):
    h, t, w = xh.shape
    return xh.reshape(h, t // dil, dil, w).transpose(0, 2, 1, 3).reshape(h, t, w)


def local_step(x, target, norm_mix_g, norm_mlp_g, final_norm_g, lbs, hgrn_norm_g, sinks,
               bq_full, bo_full, weights_get, weights_prefetch, grads_ready):
    t, d = x.shape
    depth = norm_mix_g.shape[0]
    na = d // 2 // A_DIM
    nbh = d // 2 // B_DIM
    nq = d // C_DIM
    nkv = nq // C_GROUP
    a_w = 3 * na * A_DIM
    c_w = (nq + 2 * nkv) * C_DIM
    tabs_a = rope_tables(t, A_DIM)
    tabs_c = rope_tables(t, C_DIM)
    saved = []
    for l in range(depth):
        s = {"x_in": x}
        win_g, wout_g, w1_g, w2_g = weights_get(l, x)
        gain = norm_mix_g[l]
        if l + 1 < depth:
            gain = gain + weights_prefetch(l + 1)
        s.update(win=win_g, wout=wout_g, w1=w1_g, w2=w2_g)
        h = rms_fwd(x, gain, "norm_mix_fwd")
        s["h"] = h
        if l % 2 == 0:
            e = l // 2
            proj = mm_cols_sharded(h, win_g, 0, "even_in_proj")[0]
            qkv_r = rope_call(proj, tabs_a, a_w, 2 * na, False, "rope_a")[0]
            nums, ms, ls, hms = [], [], [], []
            for window, dil in A_BRANCHES:
                hm = _to_heads(qkv_r, dil, 3 * na, A_DIM)
                num, m, lsum = band_fwd(hm, 0, na, 2 * na, na, 1, t // dil // BLK, window // dil,
                                        f"dilated_fwd_{dil}")
                hms.append(hm)
                nums.append(_unperm(num, dil))
                ms.append(_unperm(m, dil))
                ls.append(_unperm(lsum, dil))
            oa, lse = merge_branches(nums, ms, ls, "dilated_merge")
            lb_e = lbs[e].reshape(nbh, 1, B_DIM)
            gn_e = hgrn_norm_g[e].reshape(1, B_DIM)
            ob, opre, states, amats = hgrn_fwd(proj, 3 * na, nbh, lb_e, gn_e, "hgrn_fwd")
            mixed = jnp.concatenate([_from_heads(oa, 1).astype(BF16), ob], axis=1)
            x = mm_rows_sharded(mixed, wout_g, 0, "even_out_proj", [x], ["tile"], _ep_residual)
            s.update(proj=proj, hms=hms, oa=oa, lse=lse, opre=opre, states=states, amats=amats,
                     mixed=mixed, lb=lb_e, gn=gn_e)
        else:
            o = l // 2
            wq = win_g[:, 0].transpose(1, 0, 2).reshape(d, c_w)
            proj = mm_plain(h, wq, "odd_qkv_proj", [bq_full[o].reshape(1, c_w)], ["row"], _ep_bias)
            qkv_r = rope_call(proj, tabs_c, c_w, (nq + nkv) * C_DIM // LANES, False, "rope_c")[0]
            hm = _to_heads(qkv_r, 1, nq + 2 * nkv, C_DIM)
            sink_rows = jnp.repeat(sinks[o].reshape(nkv, C_GROUP), BLK, axis=1).reshape(
                nkv, C_GROUP * BLK, 1)
            num, m, lsum = band_fwd(hm, 0, nq, nq + nkv, nkv, C_GROUP, t // BLK, C_WINDOW - 1,
                                    "swa_fwd", sink_rows=sink_rows)
            o_hm, lse = normalise_heads(num, m, lsum, "swa_normalise")
            attn = _from_heads(o_hm, 1).astype(BF16)
            x = mm_rows_sharded(attn, wout_g, 0, "odd_out_proj", [bo_full[o].reshape(1, d), x],
                                ["row", "tile"], _ep_bias_residual)
            s.update(wq=wq, hm=hm, sink_rows=sink_rows, o_hm=o_hm, lse=lse, attn=attn)
        s["x_mid"] = x
        h2 = rms_fwd(x, norm_mlp_g[l], "norm_mlp_fwd")
        u, act = mm_cols_sharded(h2, w1_g, 0, "mlp_up", epilogue=_ep_relu2, n_out=2)
        x = mm_rows_sharded(act, w2_g, 0, "mlp_down", [x], ["tile"], _ep_residual)
        s.update(h2=h2, u=u, act=act)
        saved.append(s)

    dx, dxb, dg_final, loss_part = loss_head(x, final_norm_g, target, "loss_head")
    big = []
    small = {"final": dg_final, "loss": loss_part, "mix": [None] * depth, "mlp": [None] * depth,
             "lb": {}, "gn": {}, "sinks": {}, "bq": {}, "bo": {}}
    for l in reversed(range(depth)):
        s = saved[l]
        win_g, wout_g, w1_g, w2_g = s["win"], s["wout"], s["w1"], s["w2"]
        big.append(("w2", l, mm_tn(s["act"], dxb, "mlp_down_dw").reshape(N_DEV, -1, d)))
        du = mm_nt_rows_sharded(dxb, w2_g, 0, "mlp_down_dx", extras=[s["u"]],
                                epilogue=_ep_relu2_bwd, out_dtype=BF16)
        big.append(("w1", l, mm_tn(s["h2"], du, "mlp_up_dw", shard_cols=w1_g.shape[-1])))
        dh2 = mm_nt_cols_sharded(du, w1_g, 0, "mlp_up_dx")
        dx, dxb, dg, col_dx = rms_bwd(s["x_mid"], norm_mlp_g[l], dh2, dx, "norm_mlp_bwd")
        small["mlp"][l] = dg
        if l % 2 == 0:
            e = l // 2
            big.append(("wout", e, mm_tn(s["mixed"], dxb, "even_out_dw").reshape(N_DEV, -1, d)))
            dmixed = mm_nt_rows_sharded(dxb, wout_g, 0, "even_out_dx")
            do_hm = _to_heads(dmixed[:, :na * A_DIM], 1, na, A_DIM)
            delta = head_delta(s["oa"], do_hm, "dilated_delta")
            dsum = None
            for (window, dil), hm in zip(A_BRANCHES, s["hms"]):
                dq, dk, dv = band_bwd(hm, 0, na, 2 * na, _perm(do_hm, dil).astype(BF16),
                                      _perm(s["lse"], dil), _perm(delta, dil), na, 1,
                                      t // dil // BLK, window // dil, f"dilated_bwd_{dil}")
                part = _from_heads(jnp.concatenate([dq, dk, dv], axis=0), dil)
                dsum = part if dsum is None else dsum + part
            dqkv_a = rope_call(dsum, tabs_a, a_w, 2 * na, True, "rope_a_bwd")[0]
            dqb, dfb, dib, dgb, dgn, dlb = hgrn_bwd(s["proj"], 3 * na, nbh, s["lb"], s["gn"],
                                                    s["opre"], s["states"], s["amats"], dmixed, na,
                                                    "hgrn_bwd")
            small["gn"][e] = dgn
            small["lb"][e] = dlb
            dproj = jnp.concatenate([dqkv_a, dqb, dfb, dib, dgb], axis=1)
            big.append(("win", e, mm_tn(s["h"], dproj, "even_in_dw", shard_cols=win_g.shape[-1])))
            dh = mm_nt_cols_sharded(dproj, win_g, 0, "even_in_dx")
        else:
            o = l // 2
            small["bo"][o] = col_dx
            big.append(("wo", o, mm_tn(s["attn"], dxb, "odd_out_dw").reshape(N_DEV, -1, d)))
            dattn = mm_nt_rows_sharded(dxb, wout_g, 0, "odd_out_dx")
            do_hm = _to_heads(dattn, 1, nq, C_DIM)
            delta = head_delta(s["o_hm"], do_hm, "swa_delta")
            dq, dk, dv, dsink = band_bwd(s["hm"], 0, nq, nq + nkv, do_hm.astype(BF16), s["lse"],
                                         delta, nkv, C_GROUP, t // BLK, C_WINDOW - 1, "swa_bwd",
                                         sink_rows=s["sink_rows"])
            small["sinks"][o] = dsink
            dqkv = _from_heads(jnp.concatenate([dq, dk, dv], axis=0), 1)
            dproj, dbq = rope_call(dqkv, tabs_c, c_w, (nq + nkv) * C_DIM // LANES, True,
                                   "rope_c_bwd", col_sum=True)
            small["bq"][o] = dbq
            dwq = mm_tn(s["h"], dproj, "odd_qkv_dw", tn=512)
            big.append(("wqkv", o, dwq.reshape(d, N_DEV, -1).transpose(1, 0, 2)))
            dh = mm_nt_plain(dproj, s["wq"], "odd_qkv_dx", tk=c_w)
        token = grads_ready(l, big[-4:])
        dx, dxb, dg, _ = rms_bwd(s["x_in"], norm_mix_g[l] + token, dh, dx, "norm_mix_bwd")
        small["mix"][l] = dg
    return dx, small


def kernel(x, norm_mix_g, norm_mlp_g, final_norm_g, even_w_in, even_w_out, hgrn_lb_raw, hgrn_norm_g, odd_w_qkv, odd_b_qkv, odd_sinks, odd_w_o, odd_b_o, mlp_w1, mlp_w2, loss_target, m_norm_mix_g, m_norm_mlp_g, m_final_norm_g, m_even_w_in, m_even_w_out, m_hgrn_lb_raw, m_hgrn_norm_g, m_odd_w_qkv, m_odd_b_qkv, m_odd_sinks, m_odd_w_o, m_odd_b_o, m_mlp_w1, m_mlp_w2, v_norm_mix_g, v_norm_mlp_g, v_final_norm_g, v_even_w_in, v_even_w_out, v_hgrn_lb_raw, v_hgrn_norm_g, v_odd_w_qkv, v_odd_b_qkv, v_odd_sinks, v_odd_w_o, v_odd_b_o, v_mlp_w1, v_mlp_w2):
    d = x.shape[2]
    depth = norm_mix_g.shape[0]
    n_even, n_odd = even_w_in.shape[0], odd_w_qkv.shape[0]
    xi, yi, ci = lax.axis_index("x"), lax.axis_index("y"), lax.axis_index("c")
    dev = 4 * xi + 2 * yi + ci
    core = ci.astype(jnp.int32).reshape(1)

    big_w = {"win": even_w_in, "wout": even_w_out, "wqkv": odd_w_qkv, "wo": odd_w_o,
             "w1": mlp_w1, "w2": mlp_w2}
    big_m = {"win": m_even_w_in, "wout": m_even_w_out, "wqkv": m_odd_w_qkv, "wo": m_odd_w_o,
             "w1": m_mlp_w1, "w2": m_mlp_w2}
    big_v = {"win": v_even_w_in, "wout": v_even_w_out, "wqkv": v_odd_w_qkv, "wo": v_odd_w_o,
             "w1": v_mlp_w1, "w2": v_mlp_w2}
    kinds = list(big_w)
    casts = {k: cast_bf16(big_w[k], f"cast_{k}") for k in kinds}

    def layer_shards(l):
        a, b = ("win", "wout") if l % 2 == 0 else ("wqkv", "wo")
        return [casts[a][l // 2], casts[b][l // 2], casts["w1"][l], casts["w2"][l]]

    in_flight = {}
    ready = {}

    def weights_prefetch(l):
        shards = layer_shards(l)
        sems_and_bufs = gather_start(shards, ready[l - 1][0], f"gather_start_{l}")
        in_flight[l] = sems_and_bufs[:-1]
        return sems_and_bufs[-1][0, 0]

    def weights_get(l, after):
        if l == 0:
            gathered = all_gather(layer_shards(0), "gather_layer_0")
        else:
            n = len(layer_shards(l))
            send_sems, recv_sems, *bufs = in_flight.pop(l)
            bufs = gather_wait(send_sems, recv_sems, bufs[:n], bufs[n:], after, f"gather_wait_{l}")
            gathered = gather_forward(bufs[:n], bufs[n:], f"gather_forward_{l}")
        ready[l] = gathered
        return [g[:, None] for g in gathered]

    scattering = {}

    def grads_ready(l, layer_grads):
        names = [k for k, _, _ in layer_grads]
        grads = [g for _, _, g in layer_grads]
        received = exchange_with_sibling(grads, f"scatter_d2d_{l}")
        partials = [pair_add(g, r, core, f"pair_add_{k}")
                    for k, g, r in zip(names, grads, received)]
        started = scatter_start(partials, f"scatter_start_{l}")
        scattering[l] = (names, [li for _, li, _ in layer_grads], started[:-1])
        return started[-1][0, 0]

    bq_w, bo_w = odd_b_qkv.shape[1], odd_b_o.shape[1]
    bq_mine = lax.dynamic_update_slice(jnp.zeros((n_odd, N_DEV * bq_w), F32), odd_b_qkv,
                                       (0, dev * bq_w))
    bo_mine = lax.dynamic_update_slice(jnp.zeros((n_odd, N_DEV * bo_w), F32), odd_b_o,
                                       (0, dev * bo_w))
    bq_full, bo_full = _unpack(all_reduce_small(_pack([bq_mine, bo_mine]), "gather_biases"),
                               [bq_mine.shape, bo_mine.shape])

    lbs, soft = lower_bounds_fwd(hgrn_lb_raw, "lower_bounds")

    dx, small = local_step(x[0], loss_target[0], norm_mix_g, norm_mlp_g, final_norm_g, lbs,
                           hgrn_norm_g, odd_sinks, bq_full, bo_full, weights_get,
                           weights_prefetch, grads_ready)

    parts = ([small["mix"][l] for l in range(depth)] + [small["mlp"][l] for l in range(depth)]
             + [small["final"]] + [small["lb"][e] for e in range(n_even)]
             + [small["gn"][e] for e in range(n_even)] + [small["sinks"][o] for o in range(n_odd)]
             + [small["bq"][o] for o in range(n_odd)] + [small["bo"][o] for o in range(n_odd)]
             + [small["loss"]])
    shapes = ([(depth, d)] * 2 + [(d,), hgrn_lb_raw.shape, hgrn_norm_g.shape, odd_sinks.shape,
              (n_odd, N_DEV * bq_w), (n_odd, N_DEV * bo_w), (1, LANES)])
    g_mix, g_mlp, g_final, d_lbs, g_gn, g_sinks, g_bq_full, g_bo_full, loss_v = _unpack(
        all_reduce_small(_pack(parts), "reduce_small"), shapes)
    g_lb = lower_bounds_bwd(soft, d_lbs, "lower_bounds_bwd")
    g_bq = lax.dynamic_slice(g_bq_full, (0, dev * bq_w), (n_odd, bq_w))
    g_bo = lax.dynamic_slice(g_bo_full, (0, dev * bo_w), (n_odd, bo_w))
    loss = loss_v[0, 0]

    small_names = ["norm_mix_g", "norm_mlp_g", "final_norm_g", "hgrn_lb_raw", "hgrn_norm_g",
                   "odd_b_qkv", "odd_sinks", "odd_b_o"]
    small_w = [norm_mix_g, norm_mlp_g, final_norm_g, hgrn_lb_raw, hgrn_norm_g, odd_b_qkv,
               odd_sinks, odd_b_o]
    small_m = [m_norm_mix_g, m_norm_mlp_g, m_final_norm_g, m_hgrn_lb_raw, m_hgrn_norm_g,
               m_odd_b_qkv, m_odd_sinks, m_odd_b_o]
    small_v = [v_norm_mix_g, v_norm_mlp_g, v_final_norm_g, v_hgrn_lb_raw, v_hgrn_norm_g,
               v_odd_b_qkv, v_odd_sinks, v_odd_b_o]
    small_g = [g_mix, g_mlp, g_final, g_lb, g_gn, g_bq, g_sinks, g_bo]
    sshapes = [w.shape for w in small_w]
    sd, sm, sv = adamw_small(_pack(small_w), _pack(small_g), _pack(small_m), _pack(small_v),
                             "adamw_small")
    res = {}
    for name, g, dl, m2, v2 in zip(small_names, small_g, _unpack(sd, sshapes),
                                   _unpack(sm, sshapes), _unpack(sv, sshapes)):
        res[name] = (g.reshape(dl.shape), dl, m2, v2)

    landed = {k: [None] * big_w[k].shape[0] for k in kinds}
    for l in range(depth):
        names, layer_idx, (send_sems, recv_sems, local_sems, *bufs) = scattering[l]
        n = len(names)
        lands = scatter_wait(send_sems, recv_sems, local_sems, bufs[:n], bufs[n:], dx,
                             f"scatter_wait_{l}")
        for k, li, land in zip(names, layer_idx, lands):
            landed[k][li] = land
    long_names = {"win": "even_w_in", "wout": "even_w_out", "wqkv": "odd_w_qkv", "wo": "odd_w_o",
                  "w1": "mlp_w1", "w2": "mlp_w2"}
    for k in kinds:
        res[long_names[k]] = tuple(adamw_layers(landed[k], big_w[k], big_m[k], big_v[k],
                                                f"adamw_{k}"))

    order = ["norm_mix_g", "norm_mlp_g", "final_norm_g", "even_w_in", "even_w_out", "hgrn_lb_raw",
             "hgrn_norm_g", "odd_w_qkv", "odd_b_qkv", "odd_sinks", "odd_w_o", "odd_b_o", "mlp_w1",
             "mlp_w2"]
    outs = [loss, dx[None]]
    for j in range(4):
        outs += [res[n][j] for n in order]
    return tuple(outs)
```

```python
import functools
import math

import jax
import jax.numpy as jnp
from jax import lax
from jax.experimental import pallas as pl
from jax.experimental.pallas import tpu as pltpu

F32 = jnp.float32
BF16 = jnp.bfloat16
MESH = pl.DeviceIdType.MESH

N_DEV = 8
NORM_EPS = 1e-5
ROPE_THETA = 500000.0
BLK = 128
A_DIM = 128
A_BRANCHES = ((128, 1), (512, 4), (2048, 16))
B_DIM = 128
B_CHUNK = 64
C_DIM = 64
C_GROUP = 8
C_WINDOW = 128
LANES = 128

ADAM_LR = 0.001
ADAM_B1 = 0.9
ADAM_B2 = 0.999
ADAM_EPS = 1e-08
ADAM_WD = 0.01
ADAM_STEP = 10

NN = (((1,), (0,)), ((), ()))
NT = (((1,), (1,)), ((), ()))
TN = (((0,), (0,)), ((), ()))


def _params(*sem):
    return pltpu.CompilerParams(dimension_semantics=sem)


def _sigmoid(x):
    return 1.0 / (1.0 + jnp.exp(-x))


def _rows_call(name, body, row_ins, full_ins, row_outs, acc_outs, tm):
    t = row_ins[0].shape[0]
    n_ri, n_fi, n_ro = len(row_ins), len(full_ins), len(row_outs)

    def kern(*refs):
        i = pl.program_id(0)
        body(i, refs[:n_ri], refs[n_ri:n_ri + n_fi],
             refs[n_ri + n_fi:n_ri + n_fi + n_ro], refs[n_ri + n_fi + n_ro:])

    def row_spec(shape):
        return pl.BlockSpec((tm,) + tuple(shape[1:]), lambda i: (i,) + (0,) * (len(shape) - 1))

    def full_spec(shape):
        return pl.BlockSpec(tuple(shape), lambda i: (0,) * len(shape))

    outs = pl.pallas_call(
        kern, name=name, grid=(t // tm,),
        in_specs=[row_spec(a.shape) for a in row_ins] + [full_spec(a.shape) for a in full_ins],
        out_specs=[row_spec(s.shape) for s in row_outs] + [full_spec(s.shape) for s in acc_outs],
        out_shape=list(row_outs) + list(acc_outs),
        compiler_params=_params("arbitrary" if acc_outs else "parallel"),
    )(*row_ins, *full_ins)
    return outs


def _sds(shape, dtype):
    return jax.ShapeDtypeStruct(tuple(shape), dtype)


def rms_fwd(x, g, name):
    t, d = x.shape

    def body(i, ri, fi, ro, ao):
        xv = ri[0][...]
        r = lax.rsqrt(jnp.mean(xv * xv, axis=-1, keepdims=True) + NORM_EPS)
        ro[0][...] = (xv * r * fi[0][...]).astype(BF16)

    return _rows_call(name, body, [x], [g.reshape(1, d)], [_sds((t, d), BF16)], [], 256)[0]


def rms_bwd(x, g, dh, dx_res, name):
    t, d = x.shape

    def body(i, ri, fi, ro, ao):
        xv, dhv, res = ri[0][...], ri[1][...], ri[2][...]
        gv = fi[0][...]
        r = lax.rsqrt(jnp.mean(xv * xv, axis=-1, keepdims=True) + NORM_EPS)
        gd = gv * dhv
        dx = res + r * gd - xv * (r * r * r) * jnp.mean(xv * gd, axis=-1, keepdims=True)
        ro[0][...] = dx
        ro[1][...] = dx.astype(BF16)

        @pl.when(i == 0)
        def _():
            ao[0][...] = jnp.zeros_like(ao[0])
            ao[1][...] = jnp.zeros_like(ao[1])

        ao[0][...] += jnp.sum(dhv * xv * r, axis=0, keepdims=True)
        ao[1][...] += jnp.sum(dx, axis=0, keepdims=True)

    return _rows_call(name, body, [x, dh, dx_res], [g.reshape(1, d)],
                      [_sds((t, d), F32), _sds((t, d), BF16)],
                      [_sds((1, d), F32), _sds((1, d), F32)], 256)


def loss_head(x, g, target, name):
    t, d = x.shape

    def body(i, ri, fi, ro, ao):
        xv, tg = ri[0][...], ri[1][...]
        gv = fi[0][...]
        r = lax.rsqrt(jnp.mean(xv * xv, axis=-1, keepdims=True) + NORM_EPS)
        e = xv * r * gv - tg
        dy = e * (1.0 / d)
        gd = gv * dy
        dx = r * gd - xv * (r * r * r) * jnp.mean(xv * gd, axis=-1, keepdims=True)
        ro[0][...] = dx
        ro[1][...] = dx.astype(BF16)

        @pl.when(i == 0)
        def _():
            ao[0][...] = jnp.zeros_like(ao[0])
            ao[1][...] = jnp.zeros_like(ao[1])

        ao[0][...] += jnp.sum(dy * xv * r, axis=0, keepdims=True)
        part = 0.5 * jnp.sum(jnp.mean(e * e, axis=-1, keepdims=True), axis=0, keepdims=True)
        ao[1][...] += jnp.broadcast_to(part, (1, LANES))

    return _rows_call(name, body, [x, target], [g.reshape(1, d)],
                      [_sds((t, d), F32), _sds((t, d), BF16)],
                      [_sds((1, d), F32), _sds((1, LANES), F32)], 256)


def rope_tables(seq, head_dim):
    rot = head_dim // 4
    half = rot // 2
    inv_freq = 1.0 / (ROPE_THETA ** (jnp.arange(0, rot, 2, dtype=F32) / rot))
    ang = jnp.arange(seq, dtype=F32)[:, None] * inv_freq[None, :]
    cos, sin = jnp.cos(ang), jnp.sin(ang)
    zeros = jnp.zeros((seq, head_dim - rot), F32)
    zh = jnp.zeros((seq, half), F32)
    c = jnp.concatenate([cos, cos, jnp.ones((seq, head_dim - rot), F32)], axis=-1)
    sp = jnp.concatenate([zh, sin, zeros], axis=-1)
    sm = jnp.concatenate([-sin, zh, zeros], axis=-1)
    rep = LANES // head_dim
    return jnp.tile(c, (1, rep)), jnp.tile(sp, (1, rep)), jnp.tile(sm, (1, rep)), half


def rope_call(x, tabs, width, n_rope, inverse, name, col_sum=False):
    c, sp, sm, half = tabs
    t = x.shape[0]
    tm = 256
    n_slab = width // LANES

    def kern(x_ref, c_ref, sp_ref, sm_ref, o_ref, *acc):
        cv, spv, smv = c_ref[...], sp_ref[...], sm_ref[...]
        for j in range(n_slab):
            xs = x_ref[:, j * LANES:(j + 1) * LANES].astype(F32)
            if j < n_rope:
                if inverse:
                    ys = (xs * cv + pltpu.roll(xs * spv, LANES - half, 1)
                          + pltpu.roll(xs * smv, half, 1))
                else:
                    ys = (xs * cv + pltpu.roll(xs, half, 1) * spv
                          + pltpu.roll(xs, LANES - half, 1) * smv)
            else:
                ys = xs
            o_ref[:, j * LANES:(j + 1) * LANES] = ys.astype(BF16)
            if col_sum:
                @pl.when(pl.program_id(0) == 0)
                def _():
                    acc[0][:, j * LANES:(j + 1) * LANES] = jnp.zeros((1, LANES), F32)
                acc[0][:, j * LANES:(j + 1) * LANES] += jnp.sum(ys, axis=0, keepdims=True)

    tab_spec = pl.BlockSpec((tm, LANES), lambda i: (i, 0))
    out_shape = [_sds((t, width), BF16)]
    out_specs = [pl.BlockSpec((tm, width), lambda i: (i, 0))]
    if col_sum:
        out_shape.append(_sds((1, width), F32))
        out_specs.append(pl.BlockSpec((1, width), lambda i: (0, 0)))
    return pl.pallas_call(
        kern, name=name, grid=(t // tm,),
        in_specs=[pl.BlockSpec((tm, width), lambda i: (i, 0)), tab_spec, tab_spec, tab_spec],
        out_specs=out_specs, out_shape=out_shape,
        compiler_params=_params("arbitrary" if col_sum else "parallel"),
    )(x, c, sp, sm)


def _mm_call(name, a, b, extras, out_shapes, grid, a_spec, b_spec, extra_specs, out_specs,
             acc_shape, dims, epilogue):
    n_ex, n_out = len(extras), len(out_shapes)
    nk = grid[2]

    def product(a_ref, b_ref):
        bv = b_ref[...]
        if bv.ndim == 3:
            bv = bv.reshape(bv.shape[0] * bv.shape[1], bv.shape[2])
        return lax.dot_general(a_ref[...].astype(BF16), bv.astype(BF16), dims,
                               preferred_element_type=F32)

    def kern(*refs):
        a_ref, b_ref = refs[0], refs[1]
        ex = refs[2:2 + n_ex]
        outs = refs[2 + n_ex:2 + n_ex + n_out]
        if nk == 1:
            epilogue(product(a_ref, b_ref), ex, outs)
            return
        acc = refs[-1]
        k = pl.program_id(2)

        @pl.when(k == 0)
        def _():
            acc[...] = product(a_ref, b_ref)

        @pl.when(k > 0)
        def _():
            acc[...] += product(a_ref, b_ref)

        @pl.when(k == nk - 1)
        def _():
            epilogue(acc[...], ex, outs)

    return pl.pallas_call(
        kern, name=name, grid=grid,
        in_specs=[a_spec, b_spec, *extra_specs], out_specs=out_specs, out_shape=out_shapes,
        scratch_shapes=[pltpu.VMEM(acc_shape, F32)] if nk > 1 else [],
        compiler_params=_params("parallel", "parallel", "arbitrary"),
    )(a, b, *extras)


def _ep_store(dtype):
    def ep(acc, ex, outs):
        outs[0][...] = acc.astype(dtype)
    return ep


def _ep_residual(acc, ex, outs):
    outs[0][...] = acc + ex[0][...]


def _ep_bias(acc, ex, outs):
    outs[0][...] = acc + ex[0][...]


def _ep_bias_residual(acc, ex, outs):
    outs[0][...] = acc + ex[0][...] + ex[1][...]


def _ep_relu2(acc, ex, outs):
    outs[0][...] = acc
    rl = jnp.maximum(acc, 0.0)
    outs[1][...] = (rl * rl).astype(BF16)


def _ep_relu2_bwd(acc, ex, outs):
    outs[0][...] = (acc * (2.0 * jnp.maximum(ex[0][...], 0.0))).astype(BF16)


MM_TM = 1024
MM_TN = 1024
MM_TK = 2048


def mm_cols_sharded(a, wg, layer, name, epilogue=None, n_out=1):
    m, kdim = a.shape
    n = wg.shape[-1]
    tm, tk = min(m, MM_TM), min(kdim, MM_TK)
    if epilogue is None:
        epilogue, outs = _ep_store(F32), [_sds((m, N_DEV * n), F32)]
    else:
        outs = [_sds((m, N_DEV * n), F32), _sds((m, N_DEV * n), BF16)][:n_out]
    return _mm_call(
        name, a, wg, [], outs, (m // tm, N_DEV, kdim // tk),
        pl.BlockSpec((tm, tk), lambda i, j, k: (i, k)),
        pl.BlockSpec((None, None, tk, n), lambda i, j, k: (j, layer, k, 0)),
        [], [pl.BlockSpec((tm, n), lambda i, j, k: (i, j))] * len(outs),
        (tm, n), NN, epilogue)


def _extra_specs(extra_kinds, tm, tn):
    specs = []
    for kind in extra_kinds:
        if kind == "row":
            specs.append(pl.BlockSpec((1, tn), lambda i, j, k: (0, j)))
        else:
            specs.append(pl.BlockSpec((tm, tn), lambda i, j, k: (i, j)))
    return specs


def mm_rows_sharded(a, wg, layer, name, extras, extra_kinds, epilogue):
    m, kdim = a.shape
    ks, n = wg.shape[-2], wg.shape[-1]
    tm, tn = min(m, MM_TM), min(n, MM_TN)
    gps = max(1, min(kdim, MM_TK) // ks)
    return _mm_call(
        name, a, wg, extras, [_sds((m, n), F32)], (m // tm, n // tn, N_DEV // gps),
        pl.BlockSpec((tm, gps * ks), lambda i, j, k: (i, k)),
        pl.BlockSpec((gps, None, ks, tn), lambda i, j, k: (k, layer, 0, j)),
        _extra_specs(extra_kinds, tm, tn), [pl.BlockSpec((tm, tn), lambda i, j, k: (i, j))],
        (tm, tn), NN, epilogue)[0]


def mm_plain(a, w, name, extras, extra_kinds, epilogue, tn=512):
    m, kdim = a.shape
    n = w.shape[1]
    tm, tk = min(m, MM_TM), min(kdim, MM_TK)
    return _mm_call(
        name, a, w, extras, [_sds((m, n), F32)], (m // tm, n // tn, kdim // tk),
        pl.BlockSpec((tm, tk), lambda i, j, k: (i, k)),
        pl.BlockSpec((tk, tn), lambda i, j, k: (k, j)),
        _extra_specs(extra_kinds, tm, tn), [pl.BlockSpec((tm, tn), lambda i, j, k: (i, j))],
        (tm, tn), NN, epilogue)[0]


def mm_nt_cols_sharded(dy, wg, layer, name):
    m = dy.shape[0]
    kdim, n = wg.shape[-2], wg.shape[-1]
    tm, tn = min(m, MM_TM), min(kdim, MM_TN)
    return _mm_call(
        name, dy, wg, [], [_sds((m, kdim), F32)], (m // tm, kdim // tn, N_DEV),
        pl.BlockSpec((tm, n), lambda i, j, k: (i, k)),
        pl.BlockSpec((None, None, tn, n), lambda i, j, k: (k, layer, j, 0)),
        [], [pl.BlockSpec((tm, tn), lambda i, j, k: (i, j))],
        (tm, tn), NT, _ep_store(F32))[0]


def mm_nt_rows_sharded(dy, wg, layer, name, extras=(), epilogue=None, out_dtype=F32):
    m, n = dy.shape
    ks = wg.shape[-2]
    tm, tk = min(m, MM_TM), min(n, MM_TK)
    gps = max(1, MM_TN // ks)
    tn = gps * ks
    epilogue = _ep_store(out_dtype) if epilogue is None else epilogue
    return _mm_call(
        name, dy, wg, list(extras), [_sds((m, N_DEV * ks), out_dtype)],
        (m // tm, N_DEV // gps, n // tk),
        pl.BlockSpec((tm, tk), lambda i, j, k: (i, k)),
        pl.BlockSpec((gps, None, ks, tk), lambda i, j, k: (j, layer, 0, k)),
        [pl.BlockSpec((tm, tn), lambda i, j, k: (i, j))] * len(extras),
        [pl.BlockSpec((tm, tn), lambda i, j, k: (i, j))],
        (tm, tn), NT, epilogue)[0]


def mm_nt_plain(dy, w, name, tk):
    m, n = dy.shape
    kdim = w.shape[0]
    tm, tn = min(m, MM_TM), min(kdim, MM_TN)
    return _mm_call(
        name, dy, w, [], [_sds((m, kdim), F32)], (m // tm, kdim // tn, n // tk),
        pl.BlockSpec((tm, tk), lambda i, j, k: (i, k)),
        pl.BlockSpec((tn, tk), lambda i, j, k: (j, k)),
        [], [pl.BlockSpec((tm, tn), lambda i, j, k: (i, j))],
        (tm, tn), NT, _ep_store(F32))[0]


def mm_tn(a, dy, name, shard_cols=None, tn=MM_TN):
    t, kdim = a.shape
    n = dy.shape[1]
    tm, tk = min(kdim, MM_TM), min(t, MM_TK)
    if shard_cols is None:
        tn = min(tn, n)
        out = _sds((kdim, n), BF16)
        o_spec = pl.BlockSpec((tm, tn), lambda i, j, k: (i, j))
    else:
        tn = shard_cols
        out = _sds((n // tn, kdim, tn), BF16)
        o_spec = pl.BlockSpec((None, tm, tn), lambda i, j, k: (j, i, 0))
    return _mm_call(
        name, a, dy, [], [out], (kdim // tm, n // tn, t // tk),
        pl.BlockSpec((tk, tm), lambda i, j, k: (k, i)),
        pl.BlockSpec((tk, tn), lambda i, j, k: (k, j)),
        [], [o_spec], (tm, tn), TN, _ep_store(BF16))[0]


def _band_mask(g, nk_prev_valid, max_dist):
    rows = lax.broadcasted_iota(jnp.int32, (g * BLK, 2 * BLK), 0) % BLK
    cols = lax.broadcasted_iota(jnp.int32, (g * BLK, 2 * BLK), 1)
    dist = rows + BLK - cols
    ok = (dist >= 0) & (dist <= max_dist)
    return ok & ((cols >= BLK) | nk_prev_valid)


def band_fwd(qkv, q0, k0, v0, hk, g, seg, max_dist, name, sink_rows=None):
    t, dh = qkv.shape[1], qkv.shape[2]
    nb = t // BLK
    scale = dh ** -0.5
    has_sink = sink_rows is not None

    def kern(*refs):
        if has_sink:
            q_ref, k_ref, v_ref, s_ref, num_ref, m_ref, l_ref = refs
            sink = s_ref[...]
        else:
            q_ref, k_ref, v_ref, num_ref, m_ref, l_ref = refs
        b = pl.program_id(1)
        cur = pl.multiple_of(b * BLK, BLK)
        prev = pl.multiple_of(jnp.maximum(b - 1, 0) * BLK, BLK)
        q = q_ref[...].reshape(g * BLK, dh)
        kk = jnp.concatenate([k_ref[pl.ds(prev, BLK), :], k_ref[pl.ds(cur, BLK), :]], axis=0)
        vv = jnp.concatenate([v_ref[pl.ds(prev, BLK), :], v_ref[pl.ds(cur, BLK), :]], axis=0)
        s = lax.dot_general(q, kk, NT, preferred_element_type=F32) * scale
        s = jnp.where(_band_mask(g, (b % seg) != 0, max_dist), s, -jnp.inf)
        m = jnp.max(s, axis=-1, keepdims=True)
        if has_sink:
            m = jnp.maximum(m, sink)
        p = jnp.exp(s - m)
        l = jnp.sum(p, axis=-1, keepdims=True)
        if has_sink:
            l = l + jnp.exp(sink - m)
        num = jnp.dot(p.astype(BF16), vv, preferred_element_type=F32)
        num_ref[...] = num.reshape(g, BLK, dh)
        m_ref[...] = m.reshape(g, BLK, 1)
        l_ref[...] = l.reshape(g, BLK, 1)

    in_specs = [pl.BlockSpec((g, BLK, dh), lambda h, b: (q0 // g + h, b, 0)),
                pl.BlockSpec((None, t, dh), lambda h, b: (k0 + h, 0, 0)),
                pl.BlockSpec((None, t, dh), lambda h, b: (v0 + h, 0, 0))]
    args = [qkv, qkv, qkv]
    if has_sink:
        in_specs.append(pl.BlockSpec((None, g * BLK, 1), lambda h, b: (h, 0, 0)))
        args.append(sink_rows)
    hq = hk * g
    return pl.pallas_call(
        kern, name=name, grid=(hk, nb), in_specs=in_specs,
        out_specs=[pl.BlockSpec((g, BLK, dh), lambda h, b: (h, b, 0)),
                   pl.BlockSpec((g, BLK, 1), lambda h, b: (h, b, 0)),
                   pl.BlockSpec((g, BLK, 1), lambda h, b: (h, b, 0))],
        out_shape=[_sds((hq, t, dh), F32), _sds((hq, t, 1), F32), _sds((hq, t, 1), F32)],
        compiler_params=_params("parallel", "parallel"),
    )(*args)


def band_bwd(qkv, q0, k0, v0, do, lse, delta, hk, g, seg, max_dist, name, sink_rows=None):
    t, dh = qkv.shape[1], qkv.shape[2]
    nb = t // BLK
    scale = dh ** -0.5
    has_sink = sink_rows is not None

    def kern(*refs):
        if has_sink:
            (q_ref, k_ref, v_ref, do_ref, lse_ref, dl_ref, s_ref,
             dq_ref, dk_ref, dv_ref, ds_ref, sacc) = refs
            sink = s_ref[...]
        else:
            q_ref, k_ref, v_ref, do_ref, lse_ref, dl_ref, dq_ref, dk_ref, dv_ref = refs
        b = pl.program_id(1)

        @pl.when(b == 0)
        def _():
            dk_ref[...] = jnp.zeros_like(dk_ref)
            dv_ref[...] = jnp.zeros_like(dv_ref)
            if has_sink:
                sacc[...] = jnp.zeros_like(sacc)

        cur = pl.multiple_of(b * BLK, BLK)
        prev = pl.multiple_of(jnp.maximum(b - 1, 0) * BLK, BLK)
        q = q_ref[...].reshape(g * BLK, dh)
        dout = do_ref[...].reshape(g * BLK, dh)
        lse_b = lse_ref[...].reshape(g * BLK, 1)
        dl_b = dl_ref[...].reshape(g * BLK, 1)
        kk = jnp.concatenate([k_ref[pl.ds(prev, BLK), :], k_ref[pl.ds(cur, BLK), :]], axis=0)
        vv = jnp.concatenate([v_ref[pl.ds(prev, BLK), :], v_ref[pl.ds(cur, BLK), :]], axis=0)
        s = lax.dot_general(q, kk, NT, preferred_element_type=F32) * scale
        s = jnp.where(_band_mask(g, (b % seg) != 0, max_dist), s, -jnp.inf)
        p = jnp.exp(s - lse_b)
        dp = lax.dot_general(dout, vv, NT, preferred_element_type=F32)
        ds = (p * (dp - dl_b) * scale).astype(BF16)
        dq = jnp.dot(ds, kk, preferred_element_type=F32)
        dq_ref[...] = dq.reshape(g, BLK, dh)
        dkk = lax.dot_general(ds, q, TN, preferred_element_type=F32)
        dvv = lax.dot_general(p.astype(BF16), dout, TN, preferred_element_type=F32)
        dk_ref[pl.ds(prev, BLK), :] += dkk[:BLK]
        dk_ref[pl.ds(cur, BLK), :] += dkk[BLK:]
        dv_ref[pl.ds(prev, BLK), :] += dvv[:BLK]
        dv_ref[pl.ds(cur, BLK), :] += dvv[BLK:]
        if has_sink:
            sacc[...] += -jnp.exp(sink - lse_b) * dl_b

            @pl.when(b == nb - 1)
            def _():
                for gi in range(g):
                    ds_ref[gi:gi + 1, :] = jnp.sum(sacc[gi * BLK:(gi + 1) * BLK, :], axis=0,
                                                   keepdims=True)

    in_specs = [pl.BlockSpec((g, BLK, dh), lambda h, b: (q0 // g + h, b, 0)),
                pl.BlockSpec((None, t, dh), lambda h, b: (k0 + h, 0, 0)),
                pl.BlockSpec((None, t, dh), lambda h, b: (v0 + h, 0, 0)),
                pl.BlockSpec((g, BLK, dh), lambda h, b: (h, b, 0)),
                pl.BlockSpec((g, BLK, 1), lambda h, b: (h, b, 0)),
                pl.BlockSpec((g, BLK, 1), lambda h, b: (h, b, 0))]
    args = [qkv, qkv, qkv, do, lse, delta]
    hq = hk * g
    out_specs = [pl.BlockSpec((g, BLK, dh), lambda h, b: (h, b, 0)),
                 pl.BlockSpec((None, t, dh), lambda h, b: (h, 0, 0)),
                 pl.BlockSpec((None, t, dh), lambda h, b: (h, 0, 0))]
    out_shape = [_sds((hq, t, dh), F32), _sds((hk, t, dh), F32), _sds((hk, t, dh), F32)]
    scratch = []
    if has_sink:
        in_specs.append(pl.BlockSpec((None, g * BLK, 1), lambda h, b: (h, 0, 0)))
        args.append(sink_rows)
        out_specs.append(pl.BlockSpec((None, g, 1), lambda h, b: (h, 0, 0)))
        out_shape.append(_sds((hk, g, 1), F32))
        scratch.append(pltpu.VMEM((g * BLK, 1), F32))
    return pl.pallas_call(
        kern, name=name, grid=(hk, nb), in_specs=in_specs, out_specs=out_specs, out_shape=out_shape,
        scratch_shapes=scratch, compiler_params=_params("parallel", "arbitrary"),
    )(*args)


def merge_branches(nums, ms, ls, name):
    h, t, dh = nums[0].shape
    nbr = len(nums)

    def kern(*refs):
        num_refs, m_refs, l_refs = refs[:nbr], refs[nbr:2 * nbr], refs[2 * nbr:3 * nbr]
        o_ref, lse_ref = refs[3 * nbr], refs[3 * nbr + 1]
        mall = m_refs[0][...]
        for i in range(1, nbr):
            mall = jnp.maximum(mall, m_refs[i][...])
        num = jnp.zeros((t, dh), F32)
        den = jnp.zeros((t, 1), F32)
        for i in range(nbr):
            w = jnp.exp(m_refs[i][...] - mall)
            num = num + w * num_refs[i][...]
            den = den + w * l_refs[i][...]
        o_ref[...] = num / den
        lse_ref[...] = mall + jnp.log(den)

    big = pl.BlockSpec((None, t, dh), lambda i: (i, 0, 0))
    col = pl.BlockSpec((None, t, 1), lambda i: (i, 0, 0))
    return pl.pallas_call(
        kern, name=name, grid=(h,), in_specs=[big] * nbr + [col] * (2 * nbr),
        out_specs=[big, col], out_shape=[_sds((h, t, dh), F32), _sds((h, t, 1), F32)],
        compiler_params=_params("parallel"),
    )(*nums, *ms, *ls)


def normalise_heads(num, m, l, name):
    h, t, dh = num.shape

    def kern(num_ref, m_ref, l_ref, o_ref, lse_ref):
        lv = l_ref[...]
        o_ref[...] = num_ref[...] / lv
        lse_ref[...] = m_ref[...] + jnp.log(lv)

    big = pl.BlockSpec((None, t, dh), lambda i: (i, 0, 0))
    col = pl.BlockSpec((None, t, 1), lambda i: (i, 0, 0))
    return pl.pallas_call(
        kern, name=name, grid=(h,), in_specs=[big, col, col], out_specs=[big, col],
        out_shape=[_sds((h, t, dh), F32), _sds((h, t, 1), F32)],
        compiler_params=_params("parallel"),
    )(num, m, l)


def head_delta(o, do, name):
    h, t, dh = o.shape

    def kern(o_ref, do_ref, d_ref):
        d_ref[...] = jnp.sum(o_ref[...] * do_ref[...], axis=-1, keepdims=True)

    big = pl.BlockSpec((None, t, dh), lambda i: (i, 0, 0))
    return pl.pallas_call(
        kern, name=name, grid=(h,), in_specs=[big, big],
        out_specs=pl.BlockSpec((None, t, 1), lambda i: (i, 0, 0)),
        out_shape=_sds((h, t, 1), F32), compiler_params=_params("parallel"),
    )(o, do)


def _cumsum_rows(x, n, reverse=False):
    rows = lax.broadcasted_iota(jnp.int32, x.shape, 0)
    shift = 1
    while shift < n:
        if reverse:
            x = x + jnp.where(rows < n - shift, pltpu.roll(x, n - shift, 0), 0.0)
        else:
            x = x + jnp.where(rows >= shift, pltpu.roll(x, shift, 0), 0.0)
        shift *= 2
    return x


def _hgrn_gates(f, lb):
    sig = _sigmoid(f)
    gate = lb + (1.0 - lb) * sig
    return sig, gate


B_SUB = 16


def _dot3(a, b, dims):
    ah, bh = a.astype(BF16), b.astype(BF16)
    al = (a - ah.astype(F32)).astype(BF16)
    bl = (b - bh.astype(F32)).astype(BF16)
    dot = functools.partial(lax.dot_general, dimension_numbers=dims, preferred_element_type=F32)
    return dot(ah, bh) + dot(al, bh) + dot(ah, bl)


def _sub_scales(b, i):
    r0 = i * B_SUB
    beta = b[r0 - 1:r0, :]
    return jnp.exp(b[r0:r0 + B_SUB, :] - beta), jnp.exp(jnp.minimum(beta - b, 0.0))


def _hgrn_intra_attn(qq, kk, b):
    c = qq.shape[0]
    lane = lax.broadcasted_iota(jnp.int32, (B_SUB, c), 1)
    trow = lax.broadcasted_iota(jnp.int32, (B_SUB, B_DIM), 0)
    blocks = []
    for i in range(c // B_SUB):
        r0 = i * B_SUB
        qi, bi = qq[r0:r0 + B_SUB, :], b[r0:r0 + B_SUB, :]
        if i == 0:
            a_i = jnp.zeros((B_SUB, c), F32)
        else:
            eq, ek = _sub_scales(b, i)
            a_i = jnp.where(lane < r0, _dot3(qi * eq, kk * ek, NT), 0.0)
        for sl in range(B_SUB):
            s = r0 + sl
            e = jnp.exp(jnp.where(trow >= sl, bi - b[s:s + 1, :], -jnp.inf))
            col = jnp.sum(qi * kk[s:s + 1, :] * e, axis=1, keepdims=True)
            a_i = jnp.where(lane == s, col, a_i)
        blocks.append(a_i)
    return jnp.concatenate(blocks, axis=0)


def hgrn_fwd(proj, col0, nh, lb, gn, name):
    t = proj.shape[0]
    c = B_CHUNK
    nc = t // c
    scale = B_DIM ** -0.5

    def kern(q_ref, f_ref, i_ref, g_ref, lb_ref, gn_ref, out_ref, opre_ref, st_ref, a_ref, state):
        lbv = lb_ref[...]
        gnv = gn_ref[...]
        state[...] = jnp.zeros_like(state)

        def chunk(ci, carry):
            rows = pl.ds(pl.multiple_of(ci * c, c), c)
            _, gate = _hgrn_gates(f_ref[rows, :], lbv)
            kk = 1.0 - gate
            qb = q_ref[rows, :]
            qq = qb * _sigmoid(qb) * scale
            v = i_ref[rows, :]
            b = _cumsum_rows(jnp.log(gate), c)
            st = state[...]
            st_ref[ci] = st
            o_inter = lax.dot_general((qq * jnp.exp(b)).astype(BF16), st.astype(BF16), NT,
                                      preferred_element_type=F32)
            amat = _hgrn_intra_attn(qq, kk, b)
            a_ref[ci] = amat
            o = jnp.dot(amat.astype(BF16), v.astype(BF16), preferred_element_type=F32) + o_inter
            opre_ref[rows, :] = o
            bl = b[c - 1:c, :]
            state[...] = st * jnp.exp(bl) + lax.dot_general(
                v.astype(BF16), (kk * jnp.exp(bl - b)).astype(BF16), TN, preferred_element_type=F32)
            r = lax.rsqrt(jnp.mean(o * o, axis=-1, keepdims=True) + NORM_EPS)
            gb = g_ref[rows, :]
            out_ref[rows, :] = (o * r * gnv * (gb * _sigmoid(gb))).astype(BF16)
            return carry

        lax.fori_loop(0, nc, chunk, 0)

    def col(off):
        return pl.BlockSpec((t, B_DIM), lambda h: (0, col0 + off * nh + h))

    return pl.pallas_call(
        kern, name=name, grid=(nh,),
        in_specs=[col(0), col(1), col(2), col(3),
                  pl.BlockSpec((None, 1, B_DIM), lambda h: (h, 0, 0)),
                  pl.BlockSpec((1, B_DIM), lambda h: (0, 0))],
        out_specs=[pl.BlockSpec((t, B_DIM), lambda h: (0, h)),
                   pl.BlockSpec((t, B_DIM), lambda h: (0, h)),
                   pl.BlockSpec((None, nc, B_DIM, B_DIM), lambda h: (h, 0, 0, 0)),
                   pl.BlockSpec((None, nc, c, c), lambda h: (h, 0, 0, 0))],
        out_shape=[_sds((t, nh * B_DIM), BF16), _sds((t, nh * B_DIM), F32),
                   _sds((nh, nc, B_DIM, B_DIM), F32), _sds((nh, nc, c, c), F32)],
        scratch_shapes=[pltpu.VMEM((B_DIM, B_DIM), F32)],
        compiler_params=_params("parallel"),
    )(proj, proj, proj, proj, lb, gn)


def hgrn_bwd(proj, col0, nh, lb, gn, opre, states, amats, dout, dcol0, name):
    t = proj.shape[0]
    c = B_CHUNK
    nc = t // c
    scale = B_DIM ** -0.5
    nsub = c // B_SUB

    def kern(q_ref, f_ref, i_ref, g_ref, lb_ref, gn_ref, opre_ref, st_ref, a_ref, dout_ref,
             dq_ref, df_ref, di_ref, dg_ref, dgn_ref, dlb_ref, dstate, dksc):
        lbv = lb_ref[...]
        gnv = gn_ref[...]
        dstate[...] = jnp.zeros_like(dstate)
        dlb_ref[...] = jnp.zeros_like(dlb_ref)

        @pl.when(pl.program_id(0) == 0)
        def _():
            dgn_ref[...] = jnp.zeros_like(dgn_ref)

        srow = lax.broadcasted_iota(jnp.int32, (c, B_DIM), 0)
        lane = lax.broadcasted_iota(jnp.int32, (B_SUB, c), 1)
        trow = lax.broadcasted_iota(jnp.int32, (B_SUB, B_DIM), 0)
        arow = lax.broadcasted_iota(jnp.int32, (c, c), 0)
        alane = lax.broadcasted_iota(jnp.int32, (c, c), 1)

        def chunk(cj, carry):
            ci = nc - 1 - cj
            rows = pl.ds(pl.multiple_of(ci * c, c), c)
            f = f_ref[rows, :]
            sig, gate = _hgrn_gates(f, lbv)
            kk = 1.0 - gate
            qb = q_ref[rows, :]
            sq = _sigmoid(qb)
            qq = qb * sq * scale
            v = i_ref[rows, :]
            b = _cumsum_rows(jnp.log(gate), c)
            st0 = st_ref[ci]
            dst = dstate[...]
            o = opre_ref[rows, :]
            gb = g_ref[rows, :]
            sg = _sigmoid(gb)
            silu_g = gb * sg
            d_out = dout_ref[rows, :]
            r = lax.rsqrt(jnp.mean(o * o, axis=-1, keepdims=True) + NORM_EPS)
            y = o * r
            dg_ref[rows, :] = (d_out * y * gnv * (sg * (1.0 + gb * (1.0 - sg)))).astype(BF16)
            dyn = d_out * silu_g
            dgn_ref[...] += jnp.sum(dyn * y, axis=0, keepdims=True)
            dy = dyn * gnv
            do = r * (dy - y * jnp.mean(dy * y, axis=-1, keepdims=True))
            eb = jnp.exp(b)
            bl = b[c - 1:c, :]
            ebl = jnp.exp(bl - b)
            ebl_last = jnp.exp(bl)
            do_b = do.astype(BF16)
            dst_b = dst.astype(BF16)
            dq_inter = jnp.dot(do_b, st0.astype(BF16), preferred_element_type=F32) * eb
            dst0 = lax.dot_general(do_b, (qq * eb).astype(BF16), TN,
                                   preferred_element_type=F32) + dst * ebl_last
            dv_inter = lax.dot_general((kk * ebl).astype(BF16), dst_b, NT, preferred_element_type=F32)
            dk_inter = jnp.dot(v.astype(BF16), dst_b, preferred_element_type=F32) * ebl
            amat = a_ref[ci]
            v_b = v.astype(BF16)
            d_a = lax.dot_general(do_b, v_b, NT, preferred_element_type=F32)
            d_a = jnp.where(arow >= alane, d_a, 0.0)
            dv_intra = lax.dot_general(amat.astype(BF16), do_b, TN, preferred_element_type=F32)
            dk_pairs = jnp.zeros((c, B_DIM), F32)
            dq_blocks = []
            for i in range(nsub):
                r0 = i * B_SUB
                qi, bi = qq[r0:r0 + B_SUB, :], b[r0:r0 + B_SUB, :]
                da_i = d_a[r0:r0 + B_SUB, :]
                if i == 0:
                    dq_i = jnp.zeros((B_SUB, B_DIM), F32)
                else:
                    eq, ek = _sub_scales(b, i)
                    da_m = jnp.where(lane < r0, da_i, 0.0)
                    dq_i = _dot3(da_m, kk * ek, NN) * eq
                    dk_pairs = dk_pairs + _dot3(da_m, qi * eq, TN) * ek
                for sl in range(B_SUB):
                    s = r0 + sl
                    e = jnp.exp(jnp.where(trow >= sl, bi - b[s:s + 1, :], -jnp.inf))
                    dacol = jnp.sum(jnp.where(lane == s, da_i, 0.0), axis=1, keepdims=True)
                    w = dacol * e
                    dq_i = dq_i + w * kk[s:s + 1, :]
                    dksc[s:s + 1, :] = jnp.sum(w * qi, axis=0, keepdims=True)
                dq_blocks.append(dq_i)
            dq = jnp.concatenate(dq_blocks, axis=0) + dq_inter
            dk = dk_pairs + dksc[...] + dk_inter
            dv = dv_intra + dv_inter
            db = qq * dq - kk * dk
            extra = (jnp.sum(kk * dk_inter, axis=0, keepdims=True)
                     + ebl_last * jnp.sum(st0 * dst, axis=0, keepdims=True))
            db = db + jnp.where(srow == c - 1, extra, 0.0)
            dlog = _cumsum_rows(db, c, reverse=True)
            dgate = dlog / gate - dk
            df_ref[rows, :] = (dgate * (1.0 - lbv) * sig * (1.0 - sig)).astype(BF16)
            dlb_ref[...] += jnp.sum(dgate * (1.0 - sig), axis=0, keepdims=True)
            dq_ref[rows, :] = (dq * scale * (sq * (1.0 + qb * (1.0 - sq)))).astype(BF16)
            di_ref[rows, :] = dv.astype(BF16)
            dstate[...] = dst0
            return carry

        lax.fori_loop(0, nc, chunk, 0)

    def col(off):
        return pl.BlockSpec((t, B_DIM), lambda h: (0, col0 + off * nh + h))

    hcol = pl.BlockSpec((t, B_DIM), lambda h: (0, h))
    vec = pl.BlockSpec((None, 1, B_DIM), lambda h: (h, 0, 0))
    wide = _sds((t, nh * B_DIM), BF16)
    return pl.pallas_call(
        kern, name=name, grid=(nh,),
        in_specs=[col(0), col(1), col(2), col(3), vec,
                  pl.BlockSpec((1, B_DIM), lambda h: (0, 0)), hcol,
                  pl.BlockSpec((None, nc, B_DIM, B_DIM), lambda h: (h, 0, 0, 0)),
                  pl.BlockSpec((None, nc, c, c), lambda h: (h, 0, 0, 0)),
                  pl.BlockSpec((t, B_DIM), lambda h: (0, dcol0 + h))],
        out_specs=[hcol, hcol, hcol, hcol, pl.BlockSpec((1, B_DIM), lambda h: (0, 0)), vec],
        out_shape=[wide, wide, wide, wide, _sds((1, B_DIM), F32), _sds((nh, 1, B_DIM), F32)],
        scratch_shapes=[pltpu.VMEM((B_DIM, B_DIM), F32), pltpu.VMEM((c, B_DIM), F32)],
        compiler_params=_params("arbitrary"),
    )(proj, proj, proj, proj, lb, gn, opre, states, amats, dout)


def lower_bounds_fwd(raw, name):
    n, w = raw.shape

    def kern(raw_ref, lb_ref, soft_ref):
        r = raw_ref[...]
        mx = r[0:1]
        for i in range(1, n):
            mx = jnp.maximum(mx, r[i:i + 1])
        e = jnp.exp(r - mx)
        den = e[0:1]
        for i in range(1, n):
            den = den + e[i:i + 1]
        soft = e / den
        soft_ref[...] = soft
        run = soft[0:1]
        lb_ref[0:1, :] = run - soft[0:1]
        for i in range(1, n):
            run = run + soft[i:i + 1]
            lb_ref[i:i + 1, :] = run - soft[0:1]

    return pl.pallas_call(kern, name=name, out_shape=[_sds((n, w), F32), _sds((n, w), F32)])(raw)


def lower_bounds_bwd(soft, dlb, name):
    n, w = soft.shape

    def kern(soft_ref, dlb_ref, out_ref):
        s = soft_ref[...]
        d = dlb_ref[...]
        total = d[0:1]
        for i in range(1, n):
            total = total + d[i:i + 1]
        us = []
        tail = total
        for i in range(n):
            us.append(tail - total if i == 0 else tail)
            tail = tail - d[i:i + 1]
        dot = s[0:1] * us[0]
        for i in range(1, n):
            dot = dot + s[i:i + 1] * us[i]
        for i in range(n):
            out_ref[i:i + 1, :] = s[i:i + 1] * (us[i] - dot)

    return pl.pallas_call(kern, name=name, out_shape=_sds((n, w), F32))(soft, dlb)


def _row_tile(kdim, n):
    tk = 512
    while tk > 8 and tk * n > 256 * 1024:
        tk //= 2
    return min(kdim, tk)


def _adam_update(w, g, m, v):
    m2 = ADAM_B1 * m + (1.0 - ADAM_B1) * g
    v2 = ADAM_B2 * v + (1.0 - ADAM_B2) * (g * g)
    m_hat = m2 / (1.0 - ADAM_B1 ** ADAM_STEP)
    v_hat = v2 / (1.0 - ADAM_B2 ** ADAM_STEP)
    delta = -ADAM_LR * (m_hat / (jnp.sqrt(v_hat) + ADAM_EPS) + ADAM_WD * w)
    return delta, m2, v2


def adamw_small(w, g, m, v, name):
    def kern(w_ref, g_ref, m_ref, v_ref, d_ref, m2_ref, v2_ref):
        d, m2, v2 = _adam_update(w_ref[...], g_ref[...], m_ref[...], v_ref[...])
        d_ref[...] = d
        m2_ref[...] = m2
        v2_ref[...] = v2

    return pl.pallas_call(kern, name=name, out_shape=[_sds(w.shape, F32)] * 3)(w, g, m, v)


def adamw_big(parts, w, m, v, name):
    nl, kdim, n = w.shape
    tk = _row_tile(kdim, n)

    def kern(p_ref, w_ref, m_ref, v_ref, g_ref, d_ref, m2_ref, v2_ref):
        g = p_ref[0].astype(F32)
        for q in range(1, 4):
            g = g + p_ref[q].astype(F32)
        d, m2, v2 = _adam_update(w_ref[...], g, m_ref[...], v_ref[...])
        g_ref[...] = g
        d_ref[...] = d
        m2_ref[...] = m2
        v2_ref[...] = v2

    blk = pl.BlockSpec((None, tk, n), lambda l, i: (l, i, 0))
    return pl.pallas_call(
        kern, name=name, grid=(nl, kdim // tk),
        in_specs=[pl.BlockSpec((None, 4, tk, n), lambda l, i: (l, 0, i, 0)), blk, blk, blk],
        out_specs=[blk] * 4, out_shape=[_sds(w.shape, F32)] * 4,
        compiler_params=_params("parallel", "parallel"),
    )(parts, w, m, v)


def cast_bf16(w, name):
    nl, kdim, n = w.shape
    tk = _row_tile(kdim, n)

    def kern(w_ref, o_ref):
        o_ref[...] = w_ref[...].astype(BF16)

    blk = pl.BlockSpec((None, tk, n), lambda l, i: (l, i, 0))
    return pl.pallas_call(
        kern, name=name, grid=(nl, kdim // tk), in_specs=[blk], out_specs=blk,
        out_shape=_sds(w.shape, BF16), compiler_params=_params("parallel", "parallel"),
    )(w)


def pair_add(dw, r1, core, name):
    kdim, n = dw.shape[1], dw.shape[2]
    tk = _row_tile(kdim, n)

    def kern(c_ref, a_ref, b_ref, o_ref):
        o_ref[...] = (a_ref[...].astype(F32) + b_ref[...].astype(F32)).astype(BF16)

    grid_spec = pltpu.PrefetchScalarGridSpec(
        num_scalar_prefetch=1, grid=(4, kdim // tk),
        in_specs=[pl.BlockSpec((None, tk, n), lambda p, i, c: (2 * p + c[0], i, 0)),
                  pl.BlockSpec((None, tk, n), lambda p, i, c: (p, i, 0))],
        out_specs=pl.BlockSpec((None, tk, n), lambda p, i, c: (p, i, 0)))
    return pl.pallas_call(
        kern, name=name, grid_spec=grid_spec, out_shape=_sds((4, kdim, n), BF16),
        compiler_params=_params("parallel", "parallel"),
    )(core, dw, r1)


ANY = pl.BlockSpec(memory_space=pl.ANY)


def _place():
    x, y, c = lax.axis_index("x"), lax.axis_index("y"), lax.axis_index("c")
    chips = [(1 - x, y), (x, 1 - y), (1 - x, 1 - y)]
    return x, y, c, chips


def all_gather(shards, name):
    n = len(shards)

    def kern(*refs):
        ins, outs = refs[:n], refs[n:2 * n]
        send_sems, recv_sems, local_sems = refs[2 * n:]
        x, y, c, chips = _place()
        me, sib = (x, y, c), (x, y, 1 - c)

        def copy(t, k, block, to, src=None):
            px, py, pc = block
            dst = outs[t].at[4 * px + 2 * py + pc]
            return pltpu.make_async_remote_copy(
                src_ref=dst if src is None else src, dst_ref=dst,
                send_sem=send_sems.at[7 * t + k], recv_sem=recv_sems.at[7 * t + k],
                device_id=to, device_id_type=MESH)

        mine = [pltpu.make_async_copy(ins[t], outs[t].at[4 * x + 2 * y + c], local_sems.at[t])
                for t in range(n)]
        for cp in mine:
            cp.start()
        first = []
        for t in range(n):
            first.append(copy(t, 0, me, sib, src=ins[t]))
            first += [copy(t, 1 + j, me, (*chip, c), src=ins[t]) for j, chip in enumerate(chips)]
        for cp in first:
            cp.start()
        passed = []
        for t in range(n):
            for j, chip in enumerate(chips):
                copy(t, 1 + j, (*chip, c), me).wait_recv()
                fwd = copy(t, 4 + j, (*chip, c), sib)
                fwd.start()
                passed.append(fwd)
        for t in range(n):
            copy(t, 0, sib, me).wait_recv()
            for j, chip in enumerate(chips):
                copy(t, 4 + j, (*chip, 1 - c), me).wait_recv()
        for cp in first + passed:
            cp.wait_send()
        for cp in mine:
            cp.wait()

    return pl.pallas_call(
        kern, name=name, in_specs=[ANY] * n, out_specs=[ANY] * n,
        out_shape=[_sds((N_DEV,) + s.shape, s.dtype) for s in shards],
        scratch_shapes=[pltpu.SemaphoreType.DMA((7 * n,)), pltpu.SemaphoreType.DMA((7 * n,)),
                        pltpu.SemaphoreType.DMA((n,))],
    )(*shards)


HBM = pl.BlockSpec(memory_space=pltpu.HBM)
SEM = pl.BlockSpec(memory_space=pltpu.SEMAPHORE)
DATAFLOW = pltpu.SideEffectType.DATAFLOW_SIDE_EFFECTING


def _first_level_targets():
    x, y, c, chips = _place()
    return 4 * x + 2 * y + c, [(x, y, 1 - c)] + [(*chip, c) for chip in chips]


def gather_start(shards, after, name):
    n = len(shards)
    lands = [lax.empty((N_DEV,) + s.shape, s.dtype) for s in shards]

    def kern(*refs):
        ins, lnd = refs[:n], refs[n:2 * n]
        send_sems, recv_sems = refs[2 * n + len(after)], refs[2 * n + len(after) + 1]
        token = refs[-1]
        me, targets = _first_level_targets()
        for t in range(n):
            for k, to in enumerate(targets):
                pltpu.make_async_remote_copy(
                    src_ref=ins[t], dst_ref=lnd[t].at[me], send_sem=send_sems.at[4 * t + k],
                    recv_sem=recv_sems.at[4 * t + k], device_id=to, device_id_type=MESH).start()
        token[...] = jnp.zeros_like(token)

    args = [pltpu.with_memory_space_constraint(a, pltpu.HBM) for a in list(shards) + lands]
    return pl.pallas_call(
        kern, name=name,
        out_shape=(pltpu.SemaphoreType.DMA((4 * n,)), pltpu.SemaphoreType.DMA((4 * n,)),
                   *[pltpu.HBM(a.shape, a.dtype) for a in args], _sds((8, LANES), F32)),
        in_specs=[HBM] * (2 * n) + [ANY] * len(after),
        out_specs=(SEM, SEM, *[HBM] * (2 * n), pl.BlockSpec(memory_space=pltpu.VMEM)),
        input_output_aliases={i: 2 + i for i in range(2 * n)},
        compiler_params=pltpu.CompilerParams(has_side_effects=DATAFLOW),
    )(*args, *after)


def gather_wait(send_sems, recv_sems, shards, lands, after, name):
    n = len(shards)

    def kern(*refs):
        ins, lnd = refs[:n], refs[n:2 * n]
        send_sems, recv_sems = refs[2 * n], refs[2 * n + 1]
        me, targets = _first_level_targets()
        for t in range(n):
            for k, to in enumerate(targets):
                cp = pltpu.make_async_remote_copy(
                    src_ref=ins[t], dst_ref=lnd[t].at[me], send_sem=send_sems.at[4 * t + k],
                    recv_sem=recv_sems.at[4 * t + k], device_id=to, device_id_type=MESH)
                cp.wait_send()
                cp.wait_recv()

    bufs = list(shards) + list(lands)
    return pl.pallas_call(
        kern, name=name, out_shape=tuple(pltpu.HBM(a.shape, a.dtype) for a in bufs),
        in_specs=[HBM] * (2 * n) + [SEM, SEM, ANY], out_specs=[HBM] * (2 * n),
        input_output_aliases={i: i for i in range(2 * n)},
        compiler_params=pltpu.CompilerParams(has_side_effects=DATAFLOW),
    )(*bufs, send_sems, recv_sems, after)


def _forward_copies(ins, lnd, send_sems, recv_sems, local_sems):
    x, y, c, chips = _place()
    n = len(ins)
    mine = [pltpu.make_async_copy(ins[t], lnd[t].at[4 * x + 2 * y + c], local_sems.at[t])
            for t in range(n)]
    passed = []
    for t in range(n):
        for j, (qx, qy) in enumerate(chips):
            block = lnd[t].at[4 * qx + 2 * qy + c]
            passed.append(pltpu.make_async_remote_copy(
                src_ref=block, dst_ref=block, send_sem=send_sems.at[3 * t + j],
                recv_sem=recv_sems.at[3 * t + j], device_id=(x, y, 1 - c), device_id_type=MESH))
    return mine, passed


def forward_start(shards, lands, name):
    n = len(shards)

    def kern(*refs):
        ins, lnd = refs[:n], refs[n:2 * n]
        mine, passed = _forward_copies(ins, lnd, *refs[2 * n:2 * n + 3])
        for cp in mine + passed:
            cp.start()
        refs[-1][...] = jnp.zeros_like(refs[-1])

    bufs = list(shards) + list(lands)
    return pl.pallas_call(
        kern, name=name,
        out_shape=(pltpu.SemaphoreType.DMA((3 * n,)), pltpu.SemaphoreType.DMA((3 * n,)),
                   pltpu.SemaphoreType.DMA((n,)),
                   *[pltpu.HBM(a.shape, a.dtype) for a in bufs], _sds((8, LANES), F32)),
        in_specs=[HBM] * (2 * n),
        out_specs=(SEM, SEM, SEM, *[HBM] * (2 * n), pl.BlockSpec(memory_space=pltpu.VMEM)),
        input_output_aliases={i: 3 + i for i in range(2 * n)},
        compiler_params=pltpu.CompilerParams(has_side_effects=DATAFLOW),
    )(*bufs)


def forward_wait(send_sems, recv_sems, local_sems, shards, lands, after, name):
    n = len(shards)

    def kern(*refs):
        ins, lnd = refs[:n], refs[n:2 * n]
        mine, passed = _forward_copies(ins, lnd, *refs[2 * n:2 * n + 3])
        for cp in passed:
            cp.wait_send()
            cp.wait_recv()
        for cp in mine:
            cp.wait()

    bufs = list(shards) + list(lands)
    outs = pl.pallas_call(
        kern, name=name, out_shape=tuple(pltpu.HBM(a.shape, a.dtype) for a in bufs),
        in_specs=[HBM] * (2 * n) + [SEM, SEM, SEM, ANY], out_specs=[HBM] * (2 * n),
        input_output_aliases={i: i for i in range(2 * n)},
        compiler_params=pltpu.CompilerParams(has_side_effects=DATAFLOW),
    )(*bufs, send_sems, recv_sems, local_sems, after)
    return outs[n:]


def all_reduce_small(vec, name):
    r = vec.shape[0]

    def kern(v_ref, o_ref, buf, send_sems, recv_sems):
        x, y, c, _ = _place()
        me = 4 * x + 2 * y + c
        peers = [(x, y, 1 - c), (1 - x, y, c), (x, 1 - y, c), (1 - x, 1 - y, c),
                 (1 - x, y, 1 - c), (x, 1 - y, 1 - c), (1 - x, 1 - y, 1 - c)]
        buf[me] = v_ref[...]
        copies = []
        for k, peer in enumerate(peers):
            cp = pltpu.make_async_remote_copy(
                src_ref=v_ref, dst_ref=buf.at[me], send_sem=send_sems.at[k],
                recv_sem=recv_sems.at[k], device_id=peer, device_id_type=MESH)
            cp.start()
            copies.append(cp)
        for cp in copies:
            cp.wait_recv()
        for cp in copies:
            cp.wait_send()
        total = buf[0]
        for d in range(1, N_DEV):
            total = total + buf[d]
        o_ref[...] = total

    vm = pl.BlockSpec(memory_space=pltpu.VMEM)
    return pl.pallas_call(
        kern, name=name, in_specs=[vm], out_specs=vm, out_shape=_sds(vec.shape, F32),
        scratch_shapes=[pltpu.VMEM((N_DEV, r, LANES), F32), pltpu.SemaphoreType.DMA((7,)),
                        pltpu.SemaphoreType.DMA((7,))],
    )(vec)


def exchange_with_sibling(grads, name):
    n = len(grads)

    def kern(*refs):
        ins, outs = refs[:n], refs[n:2 * n]
        send_sems, recv_sems = refs[2 * n:]
        x, y, c, _ = _place()
        copies = []
        for t in range(n):
            for p in range(4):
                cp = pltpu.make_async_remote_copy(
                    src_ref=ins[t].at[2 * p + 1 - c], dst_ref=outs[t].at[p],
                    send_sem=send_sems.at[4 * t + p], recv_sem=recv_sems.at[4 * t + p],
                    device_id=(x, y, 1 - c), device_id_type=MESH)
                cp.start()
                copies.append(cp)
        for cp in copies:
            cp.wait_recv()
        for cp in copies:
            cp.wait_send()

    return pl.pallas_call(
        kern, name=name, in_specs=[ANY] * n, out_specs=[ANY] * n,
        out_shape=[_sds((4,) + g.shape[1:], g.dtype) for g in grads],
        scratch_shapes=[pltpu.SemaphoreType.DMA((4 * n,)), pltpu.SemaphoreType.DMA((4 * n,))],
    )(*grads)


def exchange_between_chips(partials, layers, kinds, name):
    n = len(partials)
    n_kind = max(kinds) + 1
    shapes = []
    for kd in range(n_kind):
        idx = [i for i in range(n) if kinds[i] == kd]
        nl = max(layers[i] for i in idx) + 1
        shapes.append(_sds((nl,) + partials[idx[0]].shape, partials[idx[0]].dtype))

    def kern(*refs):
        ins, outs = refs[:n], refs[n:n + n_kind]
        send_sems, recv_sems, local_sems = refs[n + n_kind:]
        x, y, c, chips = _place()
        mine = 2 * x + y
        local = []
        copies = []
        for t in range(n):
            dst = outs[kinds[t]].at[layers[t], mine]
            lc = pltpu.make_async_copy(ins[t].at[mine], dst, local_sems.at[t])
            lc.start()
            local.append(lc)
            for j, (qx, qy) in enumerate(chips):
                cp = pltpu.make_async_remote_copy(
                    src_ref=ins[t].at[2 * qx + qy], dst_ref=dst,
                    send_sem=send_sems.at[3 * t + j], recv_sem=recv_sems.at[3 * t + j],
                    device_id=(qx, qy, c), device_id_type=MESH)
                cp.start()
                copies.append(cp)
        for cp in copies:
            cp.wait_recv()
        for cp in copies:
            cp.wait_send()
        for lc in local:
            lc.wait()

    return pl.pallas_call(
        kern, name=name, in_specs=[ANY] * n, out_specs=[ANY] * n_kind, out_shape=shapes,
        scratch_shapes=[pltpu.SemaphoreType.DMA((3 * n,)), pltpu.SemaphoreType.DMA((3 * n,)),
                        pltpu.SemaphoreType.DMA((n,))],
    )(*partials)


def scatter_start(partials, name):
    n = len(partials)
    lands = [lax.empty(p.shape, p.dtype) for p in partials]

    def kern(*refs):
        ins, lnd = refs[:n], refs[n:2 * n]
        send_sems, recv_sems, local_sems = refs[2 * n:2 * n + 3]
        token = refs[-1]
        x, y, c, chips = _place()
        mine = 2 * x + y
        for t in range(n):
            pltpu.make_async_copy(ins[t].at[mine], lnd[t].at[mine], local_sems.at[t]).start()
            for j, (qx, qy) in enumerate(chips):
                pltpu.make_async_remote_copy(
                    src_ref=ins[t].at[2 * qx + qy], dst_ref=lnd[t].at[mine],
                    send_sem=send_sems.at[3 * t + j], recv_sem=recv_sems.at[3 * t + j],
                    device_id=(qx, qy, c), device_id_type=MESH).start()
        token[...] = jnp.zeros_like(token)

    args = [pltpu.with_memory_space_constraint(a, pltpu.HBM) for a in list(partials) + lands]
    return pl.pallas_call(
        kern, name=name,
        out_shape=(pltpu.SemaphoreType.DMA((3 * n,)), pltpu.SemaphoreType.DMA((3 * n,)),
                   pltpu.SemaphoreType.DMA((n,)),
                   *[pltpu.HBM(a.shape, a.dtype) for a in args], _sds((8, LANES), F32)),
        in_specs=[HBM] * (2 * n),
        out_specs=(SEM, SEM, SEM, *[HBM] * (2 * n), pl.BlockSpec(memory_space=pltpu.VMEM)),
        input_output_aliases={i: 3 + i for i in range(2 * n)},
        compiler_params=pltpu.CompilerParams(has_side_effects=DATAFLOW),
    )(*args)


def scatter_wait(send_sems, recv_sems, local_sems, partials, lands, after, name):
    n = len(partials)

    def kern(*refs):
        ins, lnd = refs[:n], refs[n:2 * n]
        send_sems, recv_sems, local_sems = refs[2 * n:2 * n + 3]
        x, y, c, chips = _place()
        mine = 2 * x + y
        for t in range(n):
            pltpu.make_async_copy(ins[t].at[mine], lnd[t].at[mine], local_sems.at[t]).wait()
            for j, (qx, qy) in enumerate(chips):
                cp = pltpu.make_async_remote_copy(
                    src_ref=ins[t].at[2 * qx + qy], dst_ref=lnd[t].at[mine],
                    send_sem=send_sems.at[3 * t + j], recv_sem=recv_sems.at[3 * t + j],
                    device_id=(qx, qy, c), device_id_type=MESH)
                cp.wait_send()
                cp.wait_recv()

    bufs = list(partials) + list(lands)
    outs = pl.pallas_call(
        kern, name=name, out_shape=tuple(pltpu.HBM(a.shape, a.dtype) for a in bufs),
        in_specs=[HBM] * (2 * n) + [SEM, SEM, SEM, ANY], out_specs=[HBM] * (2 * n),
        input_output_aliases={i: i for i in range(2 * n)},
        compiler_params=pltpu.CompilerParams(has_side_effects=DATAFLOW),
    )(*bufs, send_sems, recv_sems, local_sems, after)
    return outs[n:]


def adamw_layers(parts, w, m, v, name):
    nl, kdim, n = w.shape
    tk = _row_tile(kdim, n)

    def kern(*refs):
        p_refs = refs[:nl]
        w_ref, m_ref, v_ref, g_ref, d_ref, m2_ref, v2_ref = refs[nl:]
        for l in range(nl):
            @pl.when(pl.program_id(0) == l)
            def _():
                g = p_refs[l][0].astype(F32)
                for q in range(1, 4):
                    g = g + p_refs[l][q].astype(F32)
                d, m2, v2 = _adam_update(w_ref[...], g, m_ref[...], v_ref[...])
                g_ref[...] = g
                d_ref[...] = d
                m2_ref[...] = m2
                v2_ref[...] = v2

    def part_spec(l):
        return pl.BlockSpec((4, tk, n), lambda li, i: (0, jnp.where(li == l, i, 0), 0))

    blk = pl.BlockSpec((None, tk, n), lambda li, i: (li, i, 0))
    return pl.pallas_call(
        kern, name=name, grid=(nl, kdim // tk),
        in_specs=[part_spec(l) for l in range(nl)] + [blk, blk, blk],
        out_specs=[blk] * 4, out_shape=[_sds(w.shape, F32)] * 4,
        compiler_params=_params("arbitrary", "arbitrary"),
    )(*parts, w, m, v)


def _pack(arrays):
    flat = jnp.concatenate([a.reshape(-1).astype(F32) for a in arrays])
    pad = (-flat.shape[0]) % (8 * LANES)
    return jnp.pad(flat, (0, pad)).reshape(-1, LANES)


def _unpack(packed, shapes):
    flat = packed.reshape(-1)
    out, off = [], 0
    for s in shapes:
        n = math.prod(s)
        out.append(flat[off:off + n].reshape(s))
        off += n
    return out


def _to_heads(x2d, dil, n_heads, dh):
    t = x2d.shape[0]
    return x2d.reshape(t // dil, dil, n_heads, dh).transpose(2, 1, 0, 3).reshape(n_heads, t, dh)


def _from_heads(xh, dil):
    h, t, w = xh.shape
    return xh.reshape(h, dil, t // dil, w).transpose(2, 1, 0, 3).reshape(t, h * w)


def _unperm(xh, dil):
    h, t, w = xh.shape
    return xh.reshape(h, dil, t // dil, w).transpose(0, 2, 1, 3).reshape(h, t, w)


def _perm(xh, dil):
    h, t, w = xh.shape
    return xh.reshape(h, t // dil, dil, w).transpose(0, 2, 1, 3).reshape(h, t, w)


def local_step(x, target, norm_mix_g, norm_mlp_g, final_norm_g, lbs, hgrn_norm_g, sinks,
               bq_full, bo_full, weights_get, weights_mid, grads_ready):
    t, d = x.shape
    depth = norm_mix_g.shape[0]
    na = d // 2 // A_DIM
    nbh = d // 2 // B_DIM
    nq = d // C_DIM
    nkv = nq // C_GROUP
    a_w = 3 * na * A_DIM
    c_w = (nq + 2 * nkv) * C_DIM
    tabs_a = rope_tables(t, A_DIM)
    tabs_c = rope_tables(t, C_DIM)
    saved = []
    for l in range(depth):
        s = {"x_in": x}
        (win_g, wout_g, w1_g, w2_g), token = weights_get(l, x)
        s.update(win=win_g, wout=wout_g, w1=w1_g, w2=w2_g)
        h = rms_fwd(x, norm_mix_g[l] + token, "norm_mix_fwd")
        s["h"] = h
        if l % 2 == 0:
            e = l // 2
            proj = mm_cols_sharded(h, win_g, 0, "even_in_proj")[0]
            qkv_r = rope_call(proj, tabs_a, a_w, 2 * na, False, "rope_a")[0]
            nums, ms, ls, hms = [], [], [], []
            for window, dil in A_BRANCHES:
                hm = _to_heads(qkv_r, dil, 3 * na, A_DIM)
                num, m, lsum = band_fwd(hm, 0, na, 2 * na, na, 1, t // dil // BLK, window // dil,
                                        f"dilated_fwd_{dil}")
                hms.append(hm)
                nums.append(_unperm(num, dil))
                ms.append(_unperm(m, dil))
                ls.append(_unperm(lsum, dil))
            oa, lse = merge_branches(nums, ms, ls, "dilated_merge")
            lb_e = lbs[e].reshape(nbh, 1, B_DIM)
            gn_e = hgrn_norm_g[e].reshape(1, B_DIM)
            ob, opre, states, amats = hgrn_fwd(proj, 3 * na, nbh, lb_e, gn_e, "hgrn_fwd")
            mixed = jnp.concatenate([_from_heads(oa, 1).astype(BF16), ob], axis=1)
            x = mm_rows_sharded(mixed, wout_g, 0, "even_out_proj", [x], ["tile"], _ep_residual)
            s.update(proj=proj, hms=hms, oa=oa, lse=lse, opre=opre, states=states, amats=amats,
                     mixed=mixed, lb=lb_e, gn=gn_e)
        else:
            o = l // 2
            wq = win_g[:, 0].transpose(1, 0, 2).reshape(d, c_w)
            proj = mm_plain(h, wq, "odd_qkv_proj", [bq_full[o].reshape(1, c_w)], ["row"], _ep_bias)
            qkv_r = rope_call(proj, tabs_c, c_w, (nq + nkv) * C_DIM // LANES, False, "rope_c")[0]
            hm = _to_heads(qkv_r, 1, nq + 2 * nkv, C_DIM)
            sink_rows = jnp.repeat(sinks[o].reshape(nkv, C_GROUP), BLK, axis=1).reshape(
                nkv, C_GROUP * BLK, 1)
            num, m, lsum = band_fwd(hm, 0, nq, nq + nkv, nkv, C_GROUP, t // BLK, C_WINDOW - 1,
                                    "swa_fwd", sink_rows=sink_rows)
            o_hm, lse = normalise_heads(num, m, lsum, "swa_normalise")
            attn = _from_heads(o_hm, 1).astype(BF16)
            x = mm_rows_sharded(attn, wout_g, 0, "odd_out_proj", [bo_full[o].reshape(1, d), x],
                                ["row", "tile"], _ep_bias_residual)
            s.update(wq=wq, hm=hm, sink_rows=sink_rows, o_hm=o_hm, lse=lse, attn=attn)
        s["x_mid"] = x
        h2 = rms_fwd(x, norm_mlp_g[l] + weights_mid(l, x), "norm_mlp_fwd")
        u, act = mm_cols_sharded(h2, w1_g, 0, "mlp_up", epilogue=_ep_relu2, n_out=2)
        x = mm_rows_sharded(act, w2_g, 0, "mlp_down", [x], ["tile"], _ep_residual)
        s.update(h2=h2, u=u, act=act)
        saved.append(s)

    dx, dxb, dg_final, loss_part = loss_head(x, final_norm_g, target, "loss_head")
    big = []
    small = {"final": dg_final, "loss": loss_part, "mix": [None] * depth, "mlp": [None] * depth,
             "lb": {}, "gn": {}, "sinks": {}, "bq": {}, "bo": {}}
    for l in reversed(range(depth)):
        s = saved[l]
        win_g, wout_g, w1_g, w2_g = s["win"], s["wout"], s["w1"], s["w2"]
        big.append(("w2", l, mm_tn(s["act"], dxb, "mlp_down_dw").reshape(N_DEV, -1, d)))
        du = mm_nt_rows_sharded(dxb, w2_g, 0, "mlp_down_dx", extras=[s["u"]],
                                epilogue=_ep_relu2_bwd, out_dtype=BF16)
        big.append(("w1", l, mm_tn(s["h2"], du, "mlp_up_dw", shard_cols=w1_g.shape[-1])))
        dh2 = mm_nt_cols_sharded(du, w1_g, 0, "mlp_up_dx")
        token = grads_ready(l, big[-2:])
        dx, dxb, dg, col_dx = rms_bwd(s["x_mid"], norm_mlp_g[l] + token, dh2, dx, "norm_mlp_bwd")
        small["mlp"][l] = dg
        if l % 2 == 0:
            e = l // 2
            big.append(("wout", e, mm_tn(s["mixed"], dxb, "even_out_dw").reshape(N_DEV, -1, d)))
            dmixed = mm_nt_rows_sharded(dxb, wout_g, 0, "even_out_dx")
            do_hm = _to_heads(dmixed[:, :na * A_DIM], 1, na, A_DIM)
            delta = head_delta(s["oa"], do_hm, "dilated_delta")
            dsum = None
            for (window, dil), hm in zip(A_BRANCHES, s["hms"]):
                dq, dk, dv = band_bwd(hm, 0, na, 2 * na, _perm(do_hm, dil).astype(BF16),
                                      _perm(s["lse"], dil), _perm(delta, dil), na, 1,
                                      t // dil // BLK, window // dil, f"dilated_bwd_{dil}")
                part = _from_heads(jnp.concatenate([dq, dk, dv], axis=0), dil)
                dsum = part if dsum is None else dsum + part
            dqkv_a = rope_call(dsum, tabs_a, a_w, 2 * na, True, "rope_a_bwd")[0]
            dqb, dfb, dib, dgb, dgn, dlb = hgrn_bwd(s["proj"], 3 * na, nbh, s["lb"], s["gn"],
                                                    s["opre"], s["states"], s["amats"], dmixed, na,
                                                    "hgrn_bwd")
            small["gn"][e] = dgn
            small["lb"][e] = dlb
            dproj = jnp.concatenate([dqkv_a, dqb, dfb, dib, dgb], axis=1)
            big.append(("win", e, mm_tn(s["h"], dproj, "even_in_dw", shard_cols=win_g.shape[-1])))
            dh = mm_nt_cols_sharded(dproj, win_g, 0, "even_in_dx")
        else:
            o = l // 2
            small["bo"][o] = col_dx
            big.append(("wo", o, mm_tn(s["attn"], dxb, "odd_out_dw").reshape(N_DEV, -1, d)))
            dattn = mm_nt_rows_sharded(dxb, wout_g, 0, "odd_out_dx")
            do_hm = _to_heads(dattn, 1, nq, C_DIM)
            delta = head_delta(s["o_hm"], do_hm, "swa_delta")
            dq, dk, dv, dsink = band_bwd(s["hm"], 0, nq, nq + nkv, do_hm.astype(BF16), s["lse"],
                                         delta, nkv, C_GROUP, t // BLK, C_WINDOW - 1, "swa_bwd",
                                         sink_rows=s["sink_rows"])
            small["sinks"][o] = dsink
            dqkv = _from_heads(jnp.concatenate([dq, dk, dv], axis=0), 1)
            dproj, dbq = rope_call(dqkv, tabs_c, c_w, (nq + nkv) * C_DIM // LANES, True,
                                   "rope_c_bwd", col_sum=True)
            small["bq"][o] = dbq
            dwq = mm_tn(s["h"], dproj, "odd_qkv_dw", tn=512)
            big.append(("wqkv", o, dwq.reshape(d, N_DEV, -1).transpose(1, 0, 2)))
            dh = mm_nt_plain(dproj, s["wq"], "odd_qkv_dx", tk=c_w)
        token = grads_ready(l, big[-2:])
        dx, dxb, dg, _ = rms_bwd(s["x_in"], norm_mix_g[l] + token, dh, dx, "norm_mix_bwd")
        small["mix"][l] = dg
    return dx, small


def kernel(x, norm_mix_g, norm_mlp_g, final_norm_g, even_w_in, even_w_out, hgrn_lb_raw, hgrn_norm_g, odd_w_qkv, odd_b_qkv, odd_sinks, odd_w_o, odd_b_o, mlp_w1, mlp_w2, loss_target, m_norm_mix_g, m_norm_mlp_g, m_final_norm_g, m_even_w_in, m_even_w_out, m_hgrn_lb_raw, m_hgrn_norm_g, m_odd_w_qkv, m_odd_b_qkv, m_odd_sinks, m_odd_w_o, m_odd_b_o, m_mlp_w1, m_mlp_w2, v_norm_mix_g, v_norm_mlp_g, v_final_norm_g, v_even_w_in, v_even_w_out, v_hgrn_lb_raw, v_hgrn_norm_g, v_odd_w_qkv, v_odd_b_qkv, v_odd_sinks, v_odd_w_o, v_odd_b_o, v_mlp_w1, v_mlp_w2):
    d = x.shape[2]
    depth = norm_mix_g.shape[0]
    n_even, n_odd = even_w_in.shape[0], odd_w_qkv.shape[0]
    xi, yi, ci = lax.axis_index("x"), lax.axis_index("y"), lax.axis_index("c")
    dev = 4 * xi + 2 * yi + ci
    core = ci.astype(jnp.int32).reshape(1)

    big_w = {"win": even_w_in, "wout": even_w_out, "wqkv": odd_w_qkv, "wo": odd_w_o,
             "w1": mlp_w1, "w2": mlp_w2}
    big_m = {"win": m_even_w_in, "wout": m_even_w_out, "wqkv": m_odd_w_qkv, "wo": m_odd_w_o,
             "w1": m_mlp_w1, "w2": m_mlp_w2}
    big_v = {"win": v_even_w_in, "wout": v_even_w_out, "wqkv": v_odd_w_qkv, "wo": v_odd_w_o,
             "w1": v_mlp_w1, "w2": v_mlp_w2}
    kinds = list(big_w)
    casts = {k: cast_bf16(big_w[k], f"cast_{k}") for k in kinds}

    def layer_shards(l):
        a, b = ("win", "wout") if l % 2 == 0 else ("wqkv", "wo")
        return [casts[a][l // 2], casts[b][l // 2], casts["w1"][l], casts["w2"][l]]

    bq_w, bo_w = odd_b_qkv.shape[1], odd_b_o.shape[1]
    bq_mine = lax.dynamic_update_slice(jnp.zeros((n_odd, N_DEV * bq_w), F32), odd_b_qkv,
                                       (0, dev * bq_w))
    bo_mine = lax.dynamic_update_slice(jnp.zeros((n_odd, N_DEV * bo_w), F32), odd_b_o,
                                       (0, dev * bo_w))
    biases = all_reduce_small(_pack([bq_mine, bo_mine]), "gather_biases")
    bq_full, bo_full = _unpack(biases, [bq_mine.shape, bo_mine.shape])

    first_level = {}
    second_level = {}
    zero = jnp.zeros((), F32)

    def start_first_level(l, after):
        started = gather_start(layer_shards(l), after, f"gather_start_{l}")
        first_level[l] = started[:-1]
        return started[-1][0, 0]

    def weights_get(l, after):
        if l == 0:
            gathered = all_gather(layer_shards(0), "gather_layer_0")
            token = start_first_level(1, [gathered[0], biases]) if depth > 1 else zero
        else:
            n = len(layer_shards(l))
            send_sems, recv_sems, local_sems, *bufs = second_level.pop(l)
            gathered = forward_wait(send_sems, recv_sems, local_sems, bufs[:n], bufs[n:], after,
                                    f"gather_forward_wait_{l}")
            token = zero
        return [g[:, None] for g in gathered], token

    def weights_mid(l, after):
        if l + 1 >= depth:
            return zero
        n = len(layer_shards(l + 1))
        send_sems, recv_sems, *bufs = first_level.pop(l + 1)
        bufs = gather_wait(send_sems, recv_sems, bufs[:n], bufs[n:], after, f"gather_wait_{l + 1}")
        started = forward_start(bufs[:n], bufs[n:], f"gather_forward_start_{l + 1}")
        second_level[l + 1] = started[:-1]
        token = started[-1][0, 0]
        if l + 2 < depth:
            token = token + start_first_level(l + 2, [started[-1]])
        return token

    scattering = []

    def grads_ready(l, group):
        names = [k for k, _, _ in group]
        grads = [g for _, _, g in group]
        tag = f"{l}_{names[0]}"
        received = exchange_with_sibling(grads, f"scatter_d2d_{tag}")
        partials = [pair_add(g, r, core, f"pair_add_{k}")
                    for k, g, r in zip(names, grads, received)]
        started = scatter_start(partials, f"scatter_start_{tag}")
        scattering.append((tag, names, [li for _, li, _ in group], started[:-1]))
        return started[-1][0, 0]

    lbs, soft = lower_bounds_fwd(hgrn_lb_raw, "lower_bounds")

    dx, small = local_step(x[0], loss_target[0], norm_mix_g, norm_mlp_g, final_norm_g, lbs,
                           hgrn_norm_g, odd_sinks, bq_full, bo_full, weights_get, weights_mid,
                           grads_ready)

    parts = ([small["mix"][l] for l in range(depth)] + [small["mlp"][l] for l in range(depth)]
             + [small["final"]] + [small["lb"][e] for e in range(n_even)]
             + [small["gn"][e] for e in range(n_even)] + [small["sinks"][o] for o in range(n_odd)]
             + [small["bq"][o] for o in range(n_odd)] + [small["bo"][o] for o in range(n_odd)]
             + [small["loss"]])
    shapes = ([(depth, d)] * 2 + [(d,), hgrn_lb_raw.shape, hgrn_norm_g.shape, odd_sinks.shape,
              (n_odd, N_DEV * bq_w), (n_odd, N_DEV * bo_w), (1, LANES)])
    g_mix, g_mlp, g_final, d_lbs, g_gn, g_sinks, g_bq_full, g_bo_full, loss_v = _unpack(
        all_reduce_small(_pack(parts), "reduce_small"), shapes)
    g_lb = lower_bounds_bwd(soft, d_lbs, "lower_bounds_bwd")
    g_bq = lax.dynamic_slice(g_bq_full, (0, dev * bq_w), (n_odd, bq_w))
    g_bo = lax.dynamic_slice(g_bo_full, (0, dev * bo_w), (n_odd, bo_w))
    loss = loss_v[0, 0]

    small_names = ["norm_mix_g", "norm_mlp_g", "final_norm_g", "hgrn_lb_raw", "hgrn_norm_g",
                   "odd_b_qkv", "odd_sinks", "odd_b_o"]
    small_w = [norm_mix_g, norm_mlp_g, final_norm_g, hgrn_lb_raw, hgrn_norm_g, odd_b_qkv,
               odd_sinks, odd_b_o]
    small_m = [m_norm_mix_g, m_norm_mlp_g, m_final_norm_g, m_hgrn_lb_raw, m_hgrn_norm_g,
               m_odd_b_qkv, m_odd_sinks, m_odd_b_o]
    small_v = [v_norm_mix_g, v_norm_mlp_g, v_final_norm_g, v_hgrn_lb_raw, v_hgrn_norm_g,
               v_odd_b_qkv, v_odd_sinks, v_odd_b_o]
    small_g = [g_mix, g_mlp, g_final, g_lb, g_gn, g_bq, g_sinks, g_bo]
    sshapes = [w.shape for w in small_w]
    sd, sm, sv = adamw_small(_pack(small_w), _pack(small_g), _pack(small_m), _pack(small_v),
                             "adamw_small")
    res = {}
    for name, g, dl, m2, v2 in zip(small_names, small_g, _unpack(sd, sshapes),
                                   _unpack(sm, sshapes), _unpack(sv, sshapes)):
        res[name] = (g.reshape(dl.shape), dl, m2, v2)

    landed = {k: [None] * big_w[k].shape[0] for k in kinds}
    for tag, names, layer_idx, (send_sems, recv_sems, local_sems, *bufs) in scattering:
        n = len(names)
        lands = scatter_wait(send_sems, recv_sems, local_sems, bufs[:n], bufs[n:], dx,
                             f"scatter_wait_{tag}")
        for k, li, land in zip(names, layer_idx, lands):
            landed[k][li] = land
    long_names = {"win": "even_w_in", "wout": "even_w_out", "wqkv": "odd_w_qkv", "wo": "odd_w_o",
                  "w1": "mlp_w1", "w2": "mlp_w2"}
    for k in kinds:
        res[long_names[k]] = tuple(adamw_layers(landed[k], big_w[k], big_m[k], big_v[k],
                                                f"adamw_{k}"))

    order = ["norm_mix_g", "norm_mlp_g", "final_norm_g", "even_w_in", "even_w_out", "hgrn_lb_raw",
             "hgrn_norm_g", "odd_w_qkv", "odd_b_qkv", "odd_sinks", "odd_w_o", "odd_b_o", "mlp_w1",
             "mlp_w2"]
    outs = [loss, dx[None]]
    for j in range(4):
        outs += [res[n][j] for n in order]
    return tuple(outs)
```

```python
import functools
import math

import jax
import jax.numpy as jnp
from jax import lax
from jax.experimental import pallas as pl
from jax.experimental.pallas import tpu as pltpu

F32 = jnp.float32
BF16 = jnp.bfloat16
MESH = pl.DeviceIdType.MESH

N_DEV = 8
NORM_EPS = 1e-5
ROPE_THETA = 500000.0
BLK = 128
A_DIM = 128
A_BRANCHES = ((128, 1), (512, 4), (2048, 16))
B_DIM = 128
B_CHUNK = 64
C_DIM = 64
C_GROUP = 8
C_WINDOW = 128
LANES = 128

ADAM_LR = 0.001
ADAM_B1 = 0.9
ADAM_B2 = 0.999
ADAM_EPS = 1e-08
ADAM_WD = 0.01
ADAM_STEP = 10

NN = (((1,), (0,)), ((), ()))
NT = (((1,), (1,)), ((), ()))
TN = (((0,), (0,)), ((), ()))


def _params(*sem):
    return pltpu.CompilerParams(dimension_semantics=sem)


def _sigmoid(x):
    return 1.0 / (1.0 + jnp.exp(-x))


def _rows_call(name, body, row_ins, full_ins, row_outs, acc_outs, tm):
    t = row_ins[0].shape[0]
    n_ri, n_fi, n_ro = len(row_ins), len(full_ins), len(row_outs)

    def kern(*refs):
        i = pl.program_id(0)
        body(i, refs[:n_ri], refs[n_ri:n_ri + n_fi],
             refs[n_ri + n_fi:n_ri + n_fi + n_ro], refs[n_ri + n_fi + n_ro:])

    def row_spec(shape):
        return pl.BlockSpec((tm,) + tuple(shape[1:]), lambda i: (i,) + (0,) * (len(shape) - 1))

    def full_spec(shape):
        return pl.BlockSpec(tuple(shape), lambda i: (0,) * len(shape))

    outs = pl.pallas_call(
        kern, name=name, grid=(t // tm,),
        in_specs=[row_spec(a.shape) for a in row_ins] + [full_spec(a.shape) for a in full_ins],
        out_specs=[row_spec(s.shape) for s in row_outs] + [full_spec(s.shape) for s in acc_outs],
        out_shape=list(row_outs) + list(acc_outs),
        compiler_params=_params("arbitrary" if acc_outs else "parallel"),
    )(*row_ins, *full_ins)
    return outs


def _sds(shape, dtype):
    return jax.ShapeDtypeStruct(tuple(shape), dtype)


def rms_fwd(x, g, name):
    t, d = x.shape

    def body(i, ri, fi, ro, ao):
        xv = ri[0][...]
        r = lax.rsqrt(jnp.mean(xv * xv, axis=-1, keepdims=True) + NORM_EPS)
        ro[0][...] = (xv * r * fi[0][...]).astype(BF16)

    return _rows_call(name, body, [x], [g.reshape(1, d)], [_sds((t, d), BF16)], [], 256)[0]


def rms_bwd(x, g, dh, dx_res, name):
    t, d = x.shape

    def body(i, ri, fi, ro, ao):
        xv, dhv, res = ri[0][...], ri[1][...], ri[2][...]
        gv = fi[0][...]
        r = lax.rsqrt(jnp.mean(xv * xv, axis=-1, keepdims=True) + NORM_EPS)
        gd = gv * dhv
        dx = res + r * gd - xv * (r * r * r) * jnp.mean(xv * gd, axis=-1, keepdims=True)
        ro[0][...] = dx
        ro[1][...] = dx.astype(BF16)

        @pl.when(i == 0)
        def _():
            ao[0][...] = jnp.zeros_like(ao[0])
            ao[1][...] = jnp.zeros_like(ao[1])

        ao[0][...] += jnp.sum(dhv * xv * r, axis=0, keepdims=True)
        ao[1][...] += jnp.sum(dx, axis=0, keepdims=True)

    return _rows_call(name, body, [x, dh, dx_res], [g.reshape(1, d)],
                      [_sds((t, d), F32), _sds((t, d), BF16)],
                      [_sds((1, d), F32), _sds((1, d), F32)], 256)


def loss_head(x, g, target, name):
    t, d = x.shape

    def body(i, ri, fi, ro, ao):
        xv, tg = ri[0][...], ri[1][...]
        gv = fi[0][...]
        r = lax.rsqrt(jnp.mean(xv * xv, axis=-1, keepdims=True) + NORM_EPS)
        e = xv * r * gv - tg
        dy = e * (1.0 / d)
        gd = gv * dy
        dx = r * gd - xv * (r * r * r) * jnp.mean(xv * gd, axis=-1, keepdims=True)
        ro[0][...] = dx
        ro[1][...] = dx.astype(BF16)

        @pl.when(i == 0)
        def _():
            ao[0][...] = jnp.zeros_like(ao[0])
            ao[1][...] = jnp.zeros_like(ao[1])

        ao[0][...] += jnp.sum(dy * xv * r, axis=0, keepdims=True)
        part = 0.5 * jnp.sum(jnp.mean(e * e, axis=-1, keepdims=True), axis=0, keepdims=True)
        ao[1][...] += jnp.broadcast_to(part, (1, LANES))

    return _rows_call(name, body, [x, target], [g.reshape(1, d)],
                      [_sds((t, d), F32), _sds((t, d), BF16)],
                      [_sds((1, d), F32), _sds((1, LANES), F32)], 256)


def rope_tables(seq, head_dim):
    rot = head_dim // 4
    half = rot // 2
    inv_freq = 1.0 / (ROPE_THETA ** (jnp.arange(0, rot, 2, dtype=F32) / rot))
    ang = jnp.arange(seq, dtype=F32)[:, None] * inv_freq[None, :]
    cos, sin = jnp.cos(ang), jnp.sin(ang)
    zeros = jnp.zeros((seq, head_dim - rot), F32)
    zh = jnp.zeros((seq, half), F32)
    c = jnp.concatenate([cos, cos, jnp.ones((seq, head_dim - rot), F32)], axis=-1)
    sp = jnp.concatenate([zh, sin, zeros], axis=-1)
    sm = jnp.concatenate([-sin, zh, zeros], axis=-1)
    rep = LANES // head_dim
    return jnp.tile(c, (1, rep)), jnp.tile(sp, (1, rep)), jnp.tile(sm, (1, rep)), half


def rope_call(x, tabs, width, n_rope, inverse, name, col_sum=False):
    c, sp, sm, half = tabs
    t = x.shape[0]
    tm = 256
    n_slab = width // LANES

    def kern(x_ref, c_ref, sp_ref, sm_ref, o_ref, *acc):
        cv, spv, smv = c_ref[...], sp_ref[...], sm_ref[...]
        for j in range(n_slab):
            xs = x_ref[:, j * LANES:(j + 1) * LANES].astype(F32)
            if j < n_rope:
                if inverse:
                    ys = (xs * cv + pltpu.roll(xs * spv, LANES - half, 1)
                          + pltpu.roll(xs * smv, half, 1))
                else:
                    ys = (xs * cv + pltpu.roll(xs, half, 1) * spv
                          + pltpu.roll(xs, LANES - half, 1) * smv)
            else:
                ys = xs
            o_ref[:, j * LANES:(j + 1) * LANES] = ys.astype(BF16)
            if col_sum:
                @pl.when(pl.program_id(0) == 0)
                def _():
                    acc[0][:, j * LANES:(j + 1) * LANES] = jnp.zeros((1, LANES), F32)
                acc[0][:, j * LANES:(j + 1) * LANES] += jnp.sum(ys, axis=0, keepdims=True)

    tab_spec = pl.BlockSpec((tm, LANES), lambda i: (i, 0))
    out_shape = [_sds((t, width), BF16)]
    out_specs = [pl.BlockSpec((tm, width), lambda i: (i, 0))]
    if col_sum:
        out_shape.append(_sds((1, width), F32))
        out_specs.append(pl.BlockSpec((1, width), lambda i: (0, 0)))
    return pl.pallas_call(
        kern, name=name, grid=(t // tm,),
        in_specs=[pl.BlockSpec((tm, width), lambda i: (i, 0)), tab_spec, tab_spec, tab_spec],
        out_specs=out_specs, out_shape=out_shape,
        compiler_params=_params("arbitrary" if col_sum else "parallel"),
    )(x, c, sp, sm)


def _mm_call(name, a, b, extras, out_shapes, grid, a_spec, b_spec, extra_specs, out_specs,
             acc_shape, dims, epilogue):
    n_ex, n_out = len(extras), len(out_shapes)
    nk = grid[2]

    def product(a_ref, b_ref):
        bv = b_ref[...]
        if bv.ndim == 3:
            bv = bv.reshape(bv.shape[0] * bv.shape[1], bv.shape[2])
        return lax.dot_general(a_ref[...].astype(BF16), bv.astype(BF16), dims,
                               preferred_element_type=F32)

    def kern(*refs):
        a_ref, b_ref = refs[0], refs[1]
        ex = refs[2:2 + n_ex]
        outs = refs[2 + n_ex:2 + n_ex + n_out]
        if nk == 1:
            epilogue(product(a_ref, b_ref), ex, outs)
            return
        acc = refs[-1]
        k = pl.program_id(2)

        @pl.when(k == 0)
        def _():
            acc[...] = product(a_ref, b_ref)

        @pl.when(k > 0)
        def _():
            acc[...] += product(a_ref, b_ref)

        @pl.when(k == nk - 1)
        def _():
            epilogue(acc[...], ex, outs)

    return pl.pallas_call(
        kern, name=name, grid=grid,
        in_specs=[a_spec, b_spec, *extra_specs], out_specs=out_specs, out_shape=out_shapes,
        scratch_shapes=[pltpu.VMEM(acc_shape, F32)] if nk > 1 else [],
        compiler_params=_params("parallel", "parallel", "arbitrary"),
    )(a, b, *extras)


def _ep_store(dtype):
    def ep(acc, ex, outs):
        outs[0][...] = acc.astype(dtype)
    return ep


def _ep_residual(acc, ex, outs):
    outs[0][...] = acc + ex[0][...]


def _ep_bias(acc, ex, outs):
    outs[0][...] = acc + ex[0][...]


def _ep_bias_residual(acc, ex, outs):
    outs[0][...] = acc + ex[0][...] + ex[1][...]


def _ep_relu2(acc, ex, outs):
    outs[0][...] = acc
    rl = jnp.maximum(acc, 0.0)
    outs[1][...] = (rl * rl).astype(BF16)


def _ep_relu2_bwd(acc, ex, outs):
    outs[0][...] = (acc * (2.0 * jnp.maximum(ex[0][...], 0.0))).astype(BF16)


MM_TM = 1024
MM_TN = 1024
MM_TK = 2048


def mm_cols_sharded(a, wg, layer, name, epilogue=None, n_out=1):
    m, kdim = a.shape
    n = wg.shape[-1]
    tm, tk = min(m, MM_TM), min(kdim, MM_TK)
    if epilogue is None:
        epilogue, outs = _ep_store(F32), [_sds((m, N_DEV * n), F32)]
    else:
        outs = [_sds((m, N_DEV * n), F32), _sds((m, N_DEV * n), BF16)][:n_out]
    return _mm_call(
        name, a, wg, [], outs, (m // tm, N_DEV, kdim // tk),
        pl.BlockSpec((tm, tk), lambda i, j, k: (i, k)),
        pl.BlockSpec((None, None, tk, n), lambda i, j, k: (j, layer, k, 0)),
        [], [pl.BlockSpec((tm, n), lambda i, j, k: (i, j))] * len(outs),
        (tm, n), NN, epilogue)


def _extra_specs(extra_kinds, tm, tn):
    specs = []
    for kind in extra_kinds:
        if kind == "row":
            specs.append(pl.BlockSpec((1, tn), lambda i, j, k: (0, j)))
        else:
            specs.append(pl.BlockSpec((tm, tn), lambda i, j, k: (i, j)))
    return specs


def mm_rows_sharded(a, wg, layer, name, extras, extra_kinds, epilogue):
    m, kdim = a.shape
    ks, n = wg.shape[-2], wg.shape[-1]
    tm, tn = min(m, MM_TM), min(n, MM_TN)
    gps = max(1, min(kdim, MM_TK) // ks)
    return _mm_call(
        name, a, wg, extras, [_sds((m, n), F32)], (m // tm, n // tn, N_DEV // gps),
        pl.BlockSpec((tm, gps * ks), lambda i, j, k: (i, k)),
        pl.BlockSpec((gps, None, ks, tn), lambda i, j, k: (k, layer, 0, j)),
        _extra_specs(extra_kinds, tm, tn), [pl.BlockSpec((tm, tn), lambda i, j, k: (i, j))],
        (tm, tn), NN, epilogue)[0]


def mm_plain(a, w, name, extras, extra_kinds, epilogue, tn=512):
    m, kdim = a.shape
    n = w.shape[1]
    tm, tk = min(m, MM_TM), min(kdim, MM_TK)
    return _mm_call(
        name, a, w, extras, [_sds((m, n), F32)], (m // tm, n // tn, kdim // tk),
        pl.BlockSpec((tm, tk), lambda i, j, k: (i, k)),
        pl.BlockSpec((tk, tn), lambda i, j, k: (k, j)),
        _extra_specs(extra_kinds, tm, tn), [pl.BlockSpec((tm, tn), lambda i, j, k: (i, j))],
        (tm, tn), NN, epilogue)[0]


def mm_nt_cols_sharded(dy, wg, layer, name):
    m = dy.shape[0]
    kdim, n = wg.shape[-2], wg.shape[-1]
    tm, tn = min(m, MM_TM), min(kdim, MM_TN)
    return _mm_call(
        name, dy, wg, [], [_sds((m, kdim), F32)], (m // tm, kdim // tn, N_DEV),
        pl.BlockSpec((tm, n), lambda i, j, k: (i, k)),
        pl.BlockSpec((None, None, tn, n), lambda i, j, k: (k, layer, j, 0)),
        [], [pl.BlockSpec((tm, tn), lambda i, j, k: (i, j))],
        (tm, tn), NT, _ep_store(F32))[0]


def mm_nt_rows_sharded(dy, wg, layer, name, extras=(), epilogue=None, out_dtype=F32):
    m, n = dy.shape
    ks = wg.shape[-2]
    tm, tk = min(m, MM_TM), min(n, MM_TK)
    gps = max(1, MM_TN // ks)
    tn = gps * ks
    epilogue = _ep_store(out_dtype) if epilogue is None else epilogue
    return _mm_call(
        name, dy, wg, list(extras), [_sds((m, N_DEV * ks), out_dtype)],
        (m // tm, N_DEV // gps, n // tk),
        pl.BlockSpec((tm, tk), lambda i, j, k: (i, k)),
        pl.BlockSpec((gps, None, ks, tk), lambda i, j, k: (j, layer, 0, k)),
        [pl.BlockSpec((tm, tn), lambda i, j, k: (i, j))] * len(extras),
        [pl.BlockSpec((tm, tn), lambda i, j, k: (i, j))],
        (tm, tn), NT, epilogue)[0]


def mm_nt_plain(dy, w, name, tk):
    m, n = dy.shape
    kdim = w.shape[0]
    tm, tn = min(m, MM_TM), min(kdim, MM_TN)
    return _mm_call(
        name, dy, w, [], [_sds((m, kdim), F32)], (m // tm, kdim // tn, n // tk),
        pl.BlockSpec((tm, tk), lambda i, j, k: (i, k)),
        pl.BlockSpec((tn, tk), lambda i, j, k: (j, k)),
        [], [pl.BlockSpec((tm, tn), lambda i, j, k: (i, j))],
        (tm, tn), NT, _ep_store(F32))[0]


def mm_tn(a, dy, name, shard_cols=None, tn=MM_TN):
    t, kdim = a.shape
    n = dy.shape[1]
    tm, tk = min(kdim, MM_TM), min(t, MM_TK)
    if shard_cols is None:
        tn = min(tn, n)
        out = _sds((kdim, n), BF16)
        o_spec = pl.BlockSpec((tm, tn), lambda i, j, k: (i, j))
    else:
        tn = shard_cols
        out = _sds((n // tn, kdim, tn), BF16)
        o_spec = pl.BlockSpec((None, tm, tn), lambda i, j, k: (j, i, 0))
    return _mm_call(
        name, a, dy, [], [out], (kdim // tm, n // tn, t // tk),
        pl.BlockSpec((tk, tm), lambda i, j, k: (k, i)),
        pl.BlockSpec((tk, tn), lambda i, j, k: (k, j)),
        [], [o_spec], (tm, tn), TN, _ep_store(BF16))[0]


BAND_BLOCKS_PER_STEP = 4


def _band_mask(g, nk_prev_valid, max_dist):
    rows = lax.broadcasted_iota(jnp.int32, (g * BLK, 2 * BLK), 0) % BLK
    cols = lax.broadcasted_iota(jnp.int32, (g * BLK, 2 * BLK), 1)
    dist = rows + BLK - cols
    ok = (dist >= 0) & (dist <= max_dist)
    return ok & ((cols >= BLK) | nk_prev_valid)


def band_fwd(qkv, q0, k0, v0, hk, g, seg, max_dist, name, sink_rows=None):
    t, dh = qkv.shape[1], qkv.shape[2]
    nb = t // BLK
    rb = BAND_BLOCKS_PER_STEP // g if g < BAND_BLOCKS_PER_STEP else 1
    rows = rb * BLK
    scale = dh ** -0.5
    has_sink = sink_rows is not None

    def kern(*refs):
        if has_sink:
            q_ref, k_ref, v_ref, s_ref, num_ref, m_ref, l_ref = refs
            sink = s_ref[...]
        else:
            q_ref, k_ref, v_ref, num_ref, m_ref, l_ref = refs
        for r in range(rb):
            b = pl.program_id(1) * rb + r
            cur = pl.multiple_of(b * BLK, BLK)
            prev = pl.multiple_of(jnp.maximum(b - 1, 0) * BLK, BLK)
            here = slice(r * BLK, (r + 1) * BLK)
            q = q_ref[:, here, :].reshape(g * BLK, dh)
            kk = jnp.concatenate([k_ref[pl.ds(prev, BLK), :], k_ref[pl.ds(cur, BLK), :]], axis=0)
            vv = jnp.concatenate([v_ref[pl.ds(prev, BLK), :], v_ref[pl.ds(cur, BLK), :]], axis=0)
            s = lax.dot_general(q, kk, NT, preferred_element_type=F32) * scale
            s = jnp.where(_band_mask(g, (b % seg) != 0, max_dist), s, -jnp.inf)
            m = jnp.max(s, axis=-1, keepdims=True)
            if has_sink:
                m = jnp.maximum(m, sink)
            p = jnp.exp(s - m)
            l = jnp.sum(p, axis=-1, keepdims=True)
            if has_sink:
                l = l + jnp.exp(sink - m)
            num = jnp.dot(p.astype(BF16), vv, preferred_element_type=F32)
            num_ref[:, here, :] = num.reshape(g, BLK, dh)
            m_ref[:, here, :] = m.reshape(g, BLK, 1)
            l_ref[:, here, :] = l.reshape(g, BLK, 1)

    in_specs = [pl.BlockSpec((g, rows, dh), lambda h, b: (q0 // g + h, b, 0)),
                pl.BlockSpec((None, t, dh), lambda h, b: (k0 + h, 0, 0)),
                pl.BlockSpec((None, t, dh), lambda h, b: (v0 + h, 0, 0))]
    args = [qkv, qkv, qkv]
    if has_sink:
        in_specs.append(pl.BlockSpec((None, g * BLK, 1), lambda h, b: (h, 0, 0)))
        args.append(sink_rows)
    hq = hk * g
    return pl.pallas_call(
        kern, name=name, grid=(hk, nb // rb), in_specs=in_specs,
        out_specs=[pl.BlockSpec((g, rows, dh), lambda h, b: (h, b, 0)),
                   pl.BlockSpec((g, rows, 1), lambda h, b: (h, b, 0)),
                   pl.BlockSpec((g, rows, 1), lambda h, b: (h, b, 0))],
        out_shape=[_sds((hq, t, dh), F32), _sds((hq, t, 1), F32), _sds((hq, t, 1), F32)],
        compiler_params=_params("parallel", "parallel"),
    )(*args)


def band_bwd(qkv, q0, k0, v0, do, lse, delta, hk, g, seg, max_dist, name, sink_rows=None):
    t, dh = qkv.shape[1], qkv.shape[2]
    nb = t // BLK
    rb = BAND_BLOCKS_PER_STEP // g if g < BAND_BLOCKS_PER_STEP else 1
    scale = dh ** -0.5
    has_sink = sink_rows is not None

    def kern(*refs):
        if has_sink:
            (q_ref, k_ref, v_ref, do_ref, lse_ref, dl_ref, s_ref,
             dq_ref, dk_ref, dv_ref, ds_ref, sacc) = refs
            sink = s_ref[...]
        else:
            q_ref, k_ref, v_ref, do_ref, lse_ref, dl_ref, dq_ref, dk_ref, dv_ref = refs
        step = pl.program_id(1)

        @pl.when(step == 0)
        def _():
            dk_ref[...] = jnp.zeros_like(dk_ref)
            dv_ref[...] = jnp.zeros_like(dv_ref)
            if has_sink:
                sacc[...] = jnp.zeros_like(sacc)

        for r in range(rb):
            b = step * rb + r
            cur = pl.multiple_of(b * BLK, BLK)
            prev = pl.multiple_of(jnp.maximum(b - 1, 0) * BLK, BLK)
            here = slice(r * BLK, (r + 1) * BLK)
            q = q_ref[:, here, :].reshape(g * BLK, dh)
            dout = do_ref[:, here, :].reshape(g * BLK, dh)
            lse_b = lse_ref[:, here, :].reshape(g * BLK, 1)
            dl_b = dl_ref[:, here, :].reshape(g * BLK, 1)
            kk = jnp.concatenate([k_ref[pl.ds(prev, BLK), :], k_ref[pl.ds(cur, BLK), :]], axis=0)
            vv = jnp.concatenate([v_ref[pl.ds(prev, BLK), :], v_ref[pl.ds(cur, BLK), :]], axis=0)
            s = lax.dot_general(q, kk, NT, preferred_element_type=F32) * scale
            s = jnp.where(_band_mask(g, (b % seg) != 0, max_dist), s, -jnp.inf)
            p = jnp.exp(s - lse_b)
            dp = lax.dot_general(dout, vv, NT, preferred_element_type=F32)
            ds = (p * (dp - dl_b) * scale).astype(BF16)
            dq = jnp.dot(ds, kk, preferred_element_type=F32)
            dq_ref[:, here, :] = dq.reshape(g, BLK, dh)
            dkk = lax.dot_general(ds, q, TN, preferred_element_type=F32)
            dvv = lax.dot_general(p.astype(BF16), dout, TN, preferred_element_type=F32)
            dk_ref[pl.ds(prev, BLK), :] += dkk[:BLK]
            dk_ref[pl.ds(cur, BLK), :] += dkk[BLK:]
            dv_ref[pl.ds(prev, BLK), :] += dvv[:BLK]
            dv_ref[pl.ds(cur, BLK), :] += dvv[BLK:]
            if has_sink:
                sacc[...] += -jnp.exp(sink - lse_b) * dl_b

        if has_sink:
            @pl.when(step == nb // rb - 1)
            def _():
                for gi in range(g):
                    ds_ref[gi:gi + 1, :] = jnp.sum(sacc[gi * BLK:(gi + 1) * BLK, :], axis=0,
                                                   keepdims=True)

    rows = rb * BLK
    in_specs = [pl.BlockSpec((g, rows, dh), lambda h, b: (q0 // g + h, b, 0)),
                pl.BlockSpec((None, t, dh), lambda h, b: (k0 + h, 0, 0)),
                pl.BlockSpec((None, t, dh), lambda h, b: (v0 + h, 0, 0)),
                pl.BlockSpec((g, rows, dh), lambda h, b: (h, b, 0)),
                pl.BlockSpec((g, rows, 1), lambda h, b: (h, b, 0)),
                pl.BlockSpec((g, rows, 1), lambda h, b: (h, b, 0))]
    args = [qkv, qkv, qkv, do, lse, delta]
    hq = hk * g
    out_specs = [pl.BlockSpec((g, rows, dh), lambda h, b: (h, b, 0)),
                 pl.BlockSpec((None, t, dh), lambda h, b: (h, 0, 0)),
                 pl.BlockSpec((None, t, dh), lambda h, b: (h, 0, 0))]
    out_shape = [_sds((hq, t, dh), F32), _sds((hk, t, dh), F32), _sds((hk, t, dh), F32)]
    scratch = []
    if has_sink:
        in_specs.append(pl.BlockSpec((None, g * BLK, 1), lambda h, b: (h, 0, 0)))
        args.append(sink_rows)
        out_specs.append(pl.BlockSpec((None, g, 1), lambda h, b: (h, 0, 0)))
        out_shape.append(_sds((hk, g, 1), F32))
        scratch.append(pltpu.VMEM((g * BLK, 1), F32))
    return pl.pallas_call(
        kern, name=name, grid=(hk, nb // rb), in_specs=in_specs, out_specs=out_specs,
        out_shape=out_shape,
        scratch_shapes=scratch, compiler_params=_params("parallel", "arbitrary"),
    )(*args)


def merge_branches(nums, ms, ls, name):
    h, t, dh = nums[0].shape
    nbr = len(nums)

    def kern(*refs):
        num_refs, m_refs, l_refs = refs[:nbr], refs[nbr:2 * nbr], refs[2 * nbr:3 * nbr]
        o_ref, lse_ref = refs[3 * nbr], refs[3 * nbr + 1]
        mall = m_refs[0][...]
        for i in range(1, nbr):
            mall = jnp.maximum(mall, m_refs[i][...])
        num = jnp.zeros((t, dh), F32)
        den = jnp.zeros((t, 1), F32)
        for i in range(nbr):
            w = jnp.exp(m_refs[i][...] - mall)
            num = num + w * num_refs[i][...]
            den = den + w * l_refs[i][...]
        o_ref[...] = num / den
        lse_ref[...] = mall + jnp.log(den)

    big = pl.BlockSpec((None, t, dh), lambda i: (i, 0, 0))
    col = pl.BlockSpec((None, t, 1), lambda i: (i, 0, 0))
    return pl.pallas_call(
        kern, name=name, grid=(h,), in_specs=[big] * nbr + [col] * (2 * nbr),
        out_specs=[big, col], out_shape=[_sds((h, t, dh), F32), _sds((h, t, 1), F32)],
        compiler_params=_params("parallel"),
    )(*nums, *ms, *ls)


def normalise_heads(num, m, l, name):
    h, t, dh = num.shape

    def kern(num_ref, m_ref, l_ref, o_ref, lse_ref):
        lv = l_ref[...]
        o_ref[...] = num_ref[...] / lv
        lse_ref[...] = m_ref[...] + jnp.log(lv)

    big = pl.BlockSpec((None, t, dh), lambda i: (i, 0, 0))
    col = pl.BlockSpec((None, t, 1), lambda i: (i, 0, 0))
    return pl.pallas_call(
        kern, name=name, grid=(h,), in_specs=[big, col, col], out_specs=[big, col],
        out_shape=[_sds((h, t, dh), F32), _sds((h, t, 1), F32)],
        compiler_params=_params("parallel"),
    )(num, m, l)


def head_delta(o, do, name):
    h, t, dh = o.shape

    def kern(o_ref, do_ref, d_ref):
        d_ref[...] = jnp.sum(o_ref[...] * do_ref[...], axis=-1, keepdims=True)

    big = pl.BlockSpec((None, t, dh), lambda i: (i, 0, 0))
    return pl.pallas_call(
        kern, name=name, grid=(h,), in_specs=[big, big],
        out_specs=pl.BlockSpec((None, t, 1), lambda i: (i, 0, 0)),
        out_shape=_sds((h, t, 1), F32), compiler_params=_params("parallel"),
    )(o, do)


def _cumsum_rows(x, n, reverse=False):
    rows = lax.broadcasted_iota(jnp.int32, x.shape, 0)
    shift = 1
    while shift < n:
        if reverse:
            x = x + jnp.where(rows < n - shift, pltpu.roll(x, n - shift, 0), 0.0)
        else:
            x = x + jnp.where(rows >= shift, pltpu.roll(x, shift, 0), 0.0)
        shift *= 2
    return x


def _hgrn_gates(f, lb):
    sig = _sigmoid(f)
    gate = lb + (1.0 - lb) * sig
    return sig, gate


B_SUB = 16


def _dot3(a, b, dims):
    ah, bh = a.astype(BF16), b.astype(BF16)
    al = (a - ah.astype(F32)).astype(BF16)
    bl = (b - bh.astype(F32)).astype(BF16)
    dot = functools.partial(lax.dot_general, dimension_numbers=dims, preferred_element_type=F32)
    return dot(ah, bh) + dot(al, bh) + dot(ah, bl)


def _sub_scales(b, i):
    r0 = i * B_SUB
    beta = b[r0 - 1:r0, :]
    return jnp.exp(b[r0:r0 + B_SUB, :] - beta), jnp.exp(jnp.minimum(beta - b, 0.0))


def _hgrn_intra_attn(qq, kk, b):
    c = qq.shape[0]
    lane = lax.broadcasted_iota(jnp.int32, (B_SUB, c), 1)
    trow = lax.broadcasted_iota(jnp.int32, (B_SUB, B_DIM), 0)
    blocks = []
    for i in range(c // B_SUB):
        r0 = i * B_SUB
        qi, bi = qq[r0:r0 + B_SUB, :], b[r0:r0 + B_SUB, :]
        if i == 0:
            a_i = jnp.zeros((B_SUB, c), F32)
        else:
            eq, ek = _sub_scales(b, i)
            a_i = jnp.where(lane < r0, _dot3(qi * eq, kk * ek, NT), 0.0)
        for sl in range(B_SUB):
            s = r0 + sl
            e = jnp.exp(jnp.where(trow >= sl, bi - b[s:s + 1, :], -jnp.inf))
            col = jnp.sum(qi * kk[s:s + 1, :] * e, axis=1, keepdims=True)
            a_i = jnp.where(lane == s, col, a_i)
        blocks.append(a_i)
    return jnp.concatenate(blocks, axis=0)


def hgrn_fwd(proj, col0, nh, lb, gn, name):
    t = proj.shape[0]
    c = B_CHUNK
    nc = t // c
    scale = B_DIM ** -0.5

    def kern(q_ref, f_ref, i_ref, g_ref, lb_ref, gn_ref, out_ref, opre_ref, st_ref, a_ref, state):
        lbv = lb_ref[...]
        gnv = gn_ref[...]
        state[...] = jnp.zeros_like(state)

        def chunk(ci, carry):
            rows = pl.ds(pl.multiple_of(ci * c, c), c)
            _, gate = _hgrn_gates(f_ref[rows, :], lbv)
            kk = 1.0 - gate
            qb = q_ref[rows, :]
            qq = qb * _sigmoid(qb) * scale
            v = i_ref[rows, :]
            b = _cumsum_rows(jnp.log(gate), c)
            st = state[...]
            st_ref[ci] = st
            o_inter = lax.dot_general((qq * jnp.exp(b)).astype(BF16), st.astype(BF16), NT,
                                      preferred_element_type=F32)
            amat = _hgrn_intra_attn(qq, kk, b)
            a_ref[ci] = amat
            o = jnp.dot(amat.astype(BF16), v.astype(BF16), preferred_element_type=F32) + o_inter
            opre_ref[rows, :] = o
            bl = b[c - 1:c, :]
            state[...] = st * jnp.exp(bl) + lax.dot_general(
                v.astype(BF16), (kk * jnp.exp(bl - b)).astype(BF16), TN, preferred_element_type=F32)
            r = lax.rsqrt(jnp.mean(o * o, axis=-1, keepdims=True) + NORM_EPS)
            gb = g_ref[rows, :]
            out_ref[rows, :] = (o * r * gnv * (gb * _sigmoid(gb))).astype(BF16)
            return carry

        lax.fori_loop(0, nc, chunk, 0)

    def col(off):
        return pl.BlockSpec((t, B_DIM), lambda h: (0, col0 + off * nh + h))

    return pl.pallas_call(
        kern, name=name, grid=(nh,),
        in_specs=[col(0), col(1), col(2), col(3),
                  pl.BlockSpec((None, 1, B_DIM), lambda h: (h, 0, 0)),
                  pl.BlockSpec((1, B_DIM), lambda h: (0, 0))],
        out_specs=[pl.BlockSpec((t, B_DIM), lambda h: (0, h)),
                   pl.BlockSpec((t, B_DIM), lambda h: (0, h)),
                   pl.BlockSpec((None, nc, B_DIM, B_DIM), lambda h: (h, 0, 0, 0)),
                   pl.BlockSpec((None, nc, c, c), lambda h: (h, 0, 0, 0))],
        out_shape=[_sds((t, nh * B_DIM), BF16), _sds((t, nh * B_DIM), F32),
                   _sds((nh, nc, B_DIM, B_DIM), F32), _sds((nh, nc, c, c), F32)],
        scratch_shapes=[pltpu.VMEM((B_DIM, B_DIM), F32)],
        compiler_params=_params("parallel"),
    )(proj, proj, proj, proj, lb, gn)


def hgrn_bwd(proj, col0, nh, lb, gn, opre, states, amats, dout, dcol0, name):
    t = proj.shape[0]
    c = B_CHUNK
    nc = t // c
    scale = B_DIM ** -0.5
    nsub = c // B_SUB

    def kern(q_ref, f_ref, i_ref, g_ref, lb_ref, gn_ref, opre_ref, st_ref, a_ref, dout_ref,
             dq_ref, df_ref, di_ref, dg_ref, dgn_ref, dlb_ref, dstate, dksc):
        lbv = lb_ref[...]
        gnv = gn_ref[...]
        dstate[...] = jnp.zeros_like(dstate)
        dlb_ref[...] = jnp.zeros_like(dlb_ref)

        @pl.when(pl.program_id(0) == 0)
        def _():
            dgn_ref[...] = jnp.zeros_like(dgn_ref)

        srow = lax.broadcasted_iota(jnp.int32, (c, B_DIM), 0)
        lane = lax.broadcasted_iota(jnp.int32, (B_SUB, c), 1)
        trow = lax.broadcasted_iota(jnp.int32, (B_SUB, B_DIM), 0)
        arow = lax.broadcasted_iota(jnp.int32, (c, c), 0)
        alane = lax.broadcasted_iota(jnp.int32, (c, c), 1)

        def chunk(cj, carry):
            ci = nc - 1 - cj
            rows = pl.ds(pl.multiple_of(ci * c, c), c)
            f = f_ref[rows, :]
            sig, gate = _hgrn_gates(f, lbv)
            kk = 1.0 - gate
            qb = q_ref[rows, :]
            sq = _sigmoid(qb)
            qq = qb * sq * scale
            v = i_ref[rows, :]
            b = _cumsum_rows(jnp.log(gate), c)
            st0 = st_ref[ci]
            dst = dstate[...]
            o = opre_ref[rows, :]
            gb = g_ref[rows, :]
            sg = _sigmoid(gb)
            silu_g = gb * sg
            d_out = dout_ref[rows, :]
            r = lax.rsqrt(jnp.mean(o * o, axis=-1, keepdims=True) + NORM_EPS)
            y = o * r
            dg_ref[rows, :] = (d_out * y * gnv * (sg * (1.0 + gb * (1.0 - sg)))).astype(BF16)
            dyn = d_out * silu_g
            dgn_ref[...] += jnp.sum(dyn * y, axis=0, keepdims=True)
            dy = dyn * gnv
            do = r * (dy - y * jnp.mean(dy * y, axis=-1, keepdims=True))
            eb = jnp.exp(b)
            bl = b[c - 1:c, :]
            ebl = jnp.exp(bl - b)
            ebl_last = jnp.exp(bl)
            do_b = do.astype(BF16)
            dst_b = dst.astype(BF16)
            dq_inter = jnp.dot(do_b, st0.astype(BF16), preferred_element_type=F32) * eb
            dst0 = lax.dot_general(do_b, (qq * eb).astype(BF16), TN,
                                   preferred_element_type=F32) + dst * ebl_last
            dv_inter = lax.dot_general((kk * ebl).astype(BF16), dst_b, NT, preferred_element_type=F32)
            dk_inter = jnp.dot(v.astype(BF16), dst_b, preferred_element_type=F32) * ebl
            amat = a_ref[ci]
            v_b = v.astype(BF16)
            d_a = lax.dot_general(do_b, v_b, NT, preferred_element_type=F32)
            d_a = jnp.where(arow >= alane, d_a, 0.0)
            dv_intra = lax.dot_general(amat.astype(BF16), do_b, TN, preferred_element_type=F32)
            dk_pairs = jnp.zeros((c, B_DIM), F32)
            dq_blocks = []
            for i in range(nsub):
                r0 = i * B_SUB
                qi, bi = qq[r0:r0 + B_SUB, :], b[r0:r0 + B_SUB, :]
                da_i = d_a[r0:r0 + B_SUB, :]
                if i == 0:
                    dq_i = jnp.zeros((B_SUB, B_DIM), F32)
                else:
                    eq, ek = _sub_scales(b, i)
                    da_m = jnp.where(lane < r0, da_i, 0.0)
                    dq_i = _dot3(da_m, kk * ek, NN) * eq
                    dk_pairs = dk_pairs + _dot3(da_m, qi * eq, TN) * ek
                for sl in range(B_SUB):
                    s = r0 + sl
                    e = jnp.exp(jnp.where(trow >= sl, bi - b[s:s + 1, :], -jnp.inf))
                    dacol = jnp.sum(jnp.where(lane == s, da_i, 0.0), axis=1, keepdims=True)
                    w = dacol * e
                    dq_i = dq_i + w * kk[s:s + 1, :]
                    dksc[s:s + 1, :] = jnp.sum(w * qi, axis=0, keepdims=True)
                dq_blocks.append(dq_i)
            dq = jnp.concatenate(dq_blocks, axis=0) + dq_inter
            dk = dk_pairs + dksc[...] + dk_inter
            dv = dv_intra + dv_inter
            db = qq * dq - kk * dk
            extra = (jnp.sum(kk * dk_inter, axis=0, keepdims=True)
                     + ebl_last * jnp.sum(st0 * dst, axis=0, keepdims=True))
            db = db + jnp.where(srow == c - 1, extra, 0.0)
            dlog = _cumsum_rows(db, c, reverse=True)
            dgate = dlog / gate - dk
            df_ref[rows, :] = (dgate * (1.0 - lbv) * sig * (1.0 - sig)).astype(BF16)
            dlb_ref[...] += jnp.sum(dgate * (1.0 - sig), axis=0, keepdims=True)
            dq_ref[rows, :] = (dq * scale * (sq * (1.0 + qb * (1.0 - sq)))).astype(BF16)
            di_ref[rows, :] = dv.astype(BF16)
            dstate[...] = dst0
            return carry

        lax.fori_loop(0, nc, chunk, 0)

    def col(off):
        return pl.BlockSpec((t, B_DIM), lambda h: (0, col0 + off * nh + h))

    hcol = pl.BlockSpec((t, B_DIM), lambda h: (0, h))
    vec = pl.BlockSpec((None, 1, B_DIM), lambda h: (h, 0, 0))
    wide = _sds((t, nh * B_DIM), BF16)
    return pl.pallas_call(
        kern, name=name, grid=(nh,),
        in_specs=[col(0), col(1), col(2), col(3), vec,
                  pl.BlockSpec((1, B_DIM), lambda h: (0, 0)), hcol,
                  pl.BlockSpec((None, nc, B_DIM, B_DIM), lambda h: (h, 0, 0, 0)),
                  pl.BlockSpec((None, nc, c, c), lambda h: (h, 0, 0, 0)),
                  pl.BlockSpec((t, B_DIM), lambda h: (0, dcol0 + h))],
        out_specs=[hcol, hcol, hcol, hcol, pl.BlockSpec((1, B_DIM), lambda h: (0, 0)), vec],
        out_shape=[wide, wide, wide, wide, _sds((1, B_DIM), F32), _sds((nh, 1, B_DIM), F32)],
        scratch_shapes=[pltpu.VMEM((B_DIM, B_DIM), F32), pltpu.VMEM((c, B_DIM), F32)],
        compiler_params=_params("arbitrary"),
    )(proj, proj, proj, proj, lb, gn, opre, states, amats, dout)


def lower_bounds_fwd(raw, name):
    n, w = raw.shape

    def kern(raw_ref, lb_ref, soft_ref):
        r = raw_ref[...]
        mx = r[0:1]
        for i in range(1, n):
            mx = jnp.maximum(mx, r[i:i + 1])
        e = jnp.exp(r - mx)
        den = e[0:1]
        for i in range(1, n):
            den = den + e[i:i + 1]
        soft = e / den
        soft_ref[...] = soft
        run = soft[0:1]
        lb_ref[0:1, :] = run - soft[0:1]
        for i in range(1, n):
            run = run + soft[i:i + 1]
            lb_ref[i:i + 1, :] = run - soft[0:1]

    return pl.pallas_call(kern, name=name, out_shape=[_sds((n, w), F32), _sds((n, w), F32)])(raw)


def lower_bounds_bwd(soft, dlb, name):
    n, w = soft.shape

    def kern(soft_ref, dlb_ref, out_ref):
        s = soft_ref[...]
        d = dlb_ref[...]
        total = d[0:1]
        for i in range(1, n):
            total = total + d[i:i + 1]
        us = []
        tail = total
        for i in range(n):
            us.append(tail - total if i == 0 else tail)
            tail = tail - d[i:i + 1]
        dot = s[0:1] * us[0]
        for i in range(1, n):
            dot = dot + s[i:i + 1] * us[i]
        for i in range(n):
            out_ref[i:i + 1, :] = s[i:i + 1] * (us[i] - dot)

    return pl.pallas_call(kern, name=name, out_shape=_sds((n, w), F32))(soft, dlb)


def _row_tile(kdim, n):
    tk = 512
    while tk > 8 and tk * n > 256 * 1024:
        tk //= 2
    return min(kdim, tk)


def _adam_update(w, g, m, v):
    m2 = ADAM_B1 * m + (1.0 - ADAM_B1) * g
    v2 = ADAM_B2 * v + (1.0 - ADAM_B2) * (g * g)
    m_hat = m2 / (1.0 - ADAM_B1 ** ADAM_STEP)
    v_hat = v2 / (1.0 - ADAM_B2 ** ADAM_STEP)
    delta = -ADAM_LR * (m_hat / (jnp.sqrt(v_hat) + ADAM_EPS) + ADAM_WD * w)
    return delta, m2, v2


def adamw_small(w, g, m, v, name):
    def kern(w_ref, g_ref, m_ref, v_ref, d_ref, m2_ref, v2_ref):
        d, m2, v2 = _adam_update(w_ref[...], g_ref[...], m_ref[...], v_ref[...])
        d_ref[...] = d
        m2_ref[...] = m2
        v2_ref[...] = v2

    return pl.pallas_call(kern, name=name, out_shape=[_sds(w.shape, F32)] * 3)(w, g, m, v)


def adamw_big(parts, w, m, v, name):
    nl, kdim, n = w.shape
    tk = _row_tile(kdim, n)

    def kern(p_ref, w_ref, m_ref, v_ref, g_ref, d_ref, m2_ref, v2_ref):
        g = p_ref[0].astype(F32)
        for q in range(1, 4):
            g = g + p_ref[q].astype(F32)
        d, m2, v2 = _adam_update(w_ref[...], g, m_ref[...], v_ref[...])
        g_ref[...] = g
        d_ref[...] = d
        m2_ref[...] = m2
        v2_ref[...] = v2

    blk = pl.BlockSpec((None, tk, n), lambda l, i: (l, i, 0))
    return pl.pallas_call(
        kern, name=name, grid=(nl, kdim // tk),
        in_specs=[pl.BlockSpec((None, 4, tk, n), lambda l, i: (l, 0, i, 0)), blk, blk, blk],
        out_specs=[blk] * 4, out_shape=[_sds(w.shape, F32)] * 4,
        compiler_params=_params("parallel", "parallel"),
    )(parts, w, m, v)


def cast_bf16(w, name):
    nl, kdim, n = w.shape
    tk = _row_tile(kdim, n)

    def kern(w_ref, o_ref):
        o_ref[...] = w_ref[...].astype(BF16)

    blk = pl.BlockSpec((None, tk, n), lambda l, i: (l, i, 0))
    return pl.pallas_call(
        kern, name=name, grid=(nl, kdim // tk), in_specs=[blk], out_specs=blk,
        out_shape=_sds(w.shape, BF16), compiler_params=_params("parallel", "parallel"),
    )(w)


def pair_add(dw, r1, core, name):
    kdim, n = dw.shape[1], dw.shape[2]
    tk = _row_tile(kdim, n)

    def kern(c_ref, a_ref, b_ref, o_ref):
        o_ref[...] = (a_ref[...].astype(F32) + b_ref[...].astype(F32)).astype(BF16)

    grid_spec = pltpu.PrefetchScalarGridSpec(
        num_scalar_prefetch=1, grid=(4, kdim // tk),
        in_specs=[pl.BlockSpec((None, tk, n), lambda p, i, c: (2 * p + c[0], i, 0)),
                  pl.BlockSpec((None, tk, n), lambda p, i, c: (p, i, 0))],
        out_specs=pl.BlockSpec((None, tk, n), lambda p, i, c: (p, i, 0)))
    return pl.pallas_call(
        kern, name=name, grid_spec=grid_spec, out_shape=_sds((4, kdim, n), BF16),
        compiler_params=_params("parallel", "parallel"),
    )(core, dw, r1)


ANY = pl.BlockSpec(memory_space=pl.ANY)


def _place():
    x, y, c = lax.axis_index("x"), lax.axis_index("y"), lax.axis_index("c")
    chips = [(1 - x, y), (x, 1 - y), (1 - x, 1 - y)]
    return x, y, c, chips


def all_gather(shards, name):
    n = len(shards)

    def kern(*refs):
        ins, outs = refs[:n], refs[n:2 * n]
        send_sems, recv_sems, local_sems = refs[2 * n:]
        x, y, c, chips = _place()
        me, sib = (x, y, c), (x, y, 1 - c)

        def copy(t, k, block, to, src=None):
            px, py, pc = block
            dst = outs[t].at[4 * px + 2 * py + pc]
            return pltpu.make_async_remote_copy(
                src_ref=dst if src is None else src, dst_ref=dst,
                send_sem=send_sems.at[7 * t + k], recv_sem=recv_sems.at[7 * t + k],
                device_id=to, device_id_type=MESH)

        mine = [pltpu.make_async_copy(ins[t], outs[t].at[4 * x + 2 * y + c], local_sems.at[t])
                for t in range(n)]
        for cp in mine:
            cp.start()
        first = []
        for t in range(n):
            first.append(copy(t, 0, me, sib, src=ins[t]))
            first += [copy(t, 1 + j, me, (*chip, c), src=ins[t]) for j, chip in enumerate(chips)]
        for cp in first:
            cp.start()
        passed = []
        for t in range(n):
            for j, chip in enumerate(chips):
                copy(t, 1 + j, (*chip, c), me).wait_recv()
                fwd = copy(t, 4 + j, (*chip, c), sib)
                fwd.start()
                passed.append(fwd)
        for t in range(n):
            copy(t, 0, sib, me).wait_recv()
            for j, chip in enumerate(chips):
                copy(t, 4 + j, (*chip, 1 - c), me).wait_recv()
        for cp in first + passed:
            cp.wait_send()
        for cp in mine:
            cp.wait()

    return pl.pallas_call(
        kern, name=name, in_specs=[ANY] * n, out_specs=[ANY] * n,
        out_shape=[_sds((N_DEV,) + s.shape, s.dtype) for s in shards],
        scratch_shapes=[pltpu.SemaphoreType.DMA((7 * n,)), pltpu.SemaphoreType.DMA((7 * n,)),
                        pltpu.SemaphoreType.DMA((n,))],
    )(*shards)


HBM = pl.BlockSpec(memory_space=pltpu.HBM)
SEM = pl.BlockSpec(memory_space=pltpu.SEMAPHORE)
DATAFLOW = pltpu.SideEffectType.DATAFLOW_SIDE_EFFECTING


def _first_level_targets():
    x, y, c, chips = _place()
    return 4 * x + 2 * y + c, [(x, y, 1 - c)] + [(*chip, c) for chip in chips]


def gather_start(shards, after, name):
    n = len(shards)
    lands = [lax.empty((N_DEV,) + s.shape, s.dtype) for s in shards]

    def kern(*refs):
        ins, lnd = refs[:n], refs[n:2 * n]
        send_sems, recv_sems, local_sems = refs[2 * n + len(after):2 * n + len(after) + 3]
        token = refs[-1]
        me, targets = _first_level_targets()
        for t in range(n):
            pltpu.make_async_copy(ins[t], lnd[t].at[me], local_sems.at[t]).start()
            for k, to in enumerate(targets):
                pltpu.make_async_remote_copy(
                    src_ref=ins[t], dst_ref=lnd[t].at[me], send_sem=send_sems.at[4 * t + k],
                    recv_sem=recv_sems.at[4 * t + k], device_id=to, device_id_type=MESH).start()
        token[...] = jnp.zeros_like(token)

    args = [pltpu.with_memory_space_constraint(a, pltpu.HBM) for a in list(shards) + lands]
    return pl.pallas_call(
        kern, name=name,
        out_shape=(pltpu.SemaphoreType.DMA((4 * n,)), pltpu.SemaphoreType.DMA((4 * n,)),
                   pltpu.SemaphoreType.DMA((n,)),
                   *[pltpu.HBM(a.shape, a.dtype) for a in args], _sds((8, LANES), F32)),
        in_specs=[HBM] * (2 * n) + [ANY] * len(after),
        out_specs=(SEM, SEM, SEM, *[HBM] * (2 * n), pl.BlockSpec(memory_space=pltpu.VMEM)),
        input_output_aliases={i: 3 + i for i in range(2 * n)},
        compiler_params=pltpu.CompilerParams(has_side_effects=DATAFLOW),
    )(*args, *after)


def gather_wait(send_sems, recv_sems, local_sems, shards, lands, after, name):
    n = len(shards)

    def kern(*refs):
        ins, lnd = refs[:n], refs[n:2 * n]
        send_sems, recv_sems, local_sems = refs[2 * n:2 * n + 3]
        me, targets = _first_level_targets()
        for t in range(n):
            pltpu.make_async_copy(ins[t], lnd[t].at[me], local_sems.at[t]).wait()
            for k, to in enumerate(targets):
                cp = pltpu.make_async_remote_copy(
                    src_ref=ins[t], dst_ref=lnd[t].at[me], send_sem=send_sems.at[4 * t + k],
                    recv_sem=recv_sems.at[4 * t + k], device_id=to, device_id_type=MESH)
                cp.wait_send()
                cp.wait_recv()

    bufs = list(shards) + list(lands)
    return pl.pallas_call(
        kern, name=name, out_shape=tuple(pltpu.HBM(a.shape, a.dtype) for a in bufs),
        in_specs=[HBM] * (2 * n) + [SEM, SEM, SEM, ANY], out_specs=[HBM] * (2 * n),
        input_output_aliases={i: i for i in range(2 * n)},
        compiler_params=pltpu.CompilerParams(has_side_effects=DATAFLOW),
    )(*bufs, send_sems, recv_sems, local_sems, after)


def _forward_copies(lnd, send_sems, recv_sems):
    x, y, c, chips = _place()
    passed = []
    for t in range(len(lnd)):
        for j, (qx, qy) in enumerate(chips):
            block = lnd[t].at[4 * qx + 2 * qy + c]
            passed.append(pltpu.make_async_remote_copy(
                src_ref=block, dst_ref=block, send_sem=send_sems.at[3 * t + j],
                recv_sem=recv_sems.at[3 * t + j], device_id=(x, y, 1 - c), device_id_type=MESH))
    return passed


def forward_start(lands, name):
    n = len(lands)

    def kern(*refs):
        for cp in _forward_copies(refs[:n], refs[n], refs[n + 1]):
            cp.start()
        refs[-1][...] = jnp.zeros_like(refs[-1])

    return pl.pallas_call(
        kern, name=name,
        out_shape=(pltpu.SemaphoreType.DMA((3 * n,)), pltpu.SemaphoreType.DMA((3 * n,)),
                   *[pltpu.HBM(a.shape, a.dtype) for a in lands], _sds((8, LANES), F32)),
        in_specs=[HBM] * n,
        out_specs=(SEM, SEM, *[HBM] * n, pl.BlockSpec(memory_space=pltpu.VMEM)),
        input_output_aliases={i: 2 + i for i in range(n)},
        compiler_params=pltpu.CompilerParams(has_side_effects=DATAFLOW),
    )(*lands)


def forward_wait(send_sems, recv_sems, lands, after, name):
    n = len(lands)

    def kern(*refs):
        for cp in _forward_copies(refs[:n], refs[n], refs[n + 1]):
            cp.wait_send()
            cp.wait_recv()

    return pl.pallas_call(
        kern, name=name, out_shape=tuple(pltpu.HBM(a.shape, a.dtype) for a in lands),
        in_specs=[HBM] * n + [SEM, SEM, ANY], out_specs=[HBM] * n,
        input_output_aliases={i: i for i in range(n)},
        compiler_params=pltpu.CompilerParams(has_side_effects=DATAFLOW),
    )(*lands, send_sems, recv_sems, after)


def all_reduce_small(vec, name):
    r = vec.shape[0]

    def kern(v_ref, o_ref, buf, send_sems, recv_sems):
        x, y, c, _ = _place()
        me = 4 * x + 2 * y + c
        peers = [(x, y, 1 - c), (1 - x, y, c), (x, 1 - y, c), (1 - x, 1 - y, c),
                 (1 - x, y, 1 - c), (x, 1 - y, 1 - c), (1 - x, 1 - y, 1 - c)]
        buf[me] = v_ref[...]
        copies = []
        for k, peer in enumerate(peers):
            cp = pltpu.make_async_remote_copy(
                src_ref=v_ref, dst_ref=buf.at[me], send_sem=send_sems.at[k],
                recv_sem=recv_sems.at[k], device_id=peer, device_id_type=MESH)
            cp.start()
            copies.append(cp)
        for cp in copies:
            cp.wait_recv()
        for cp in copies:
            cp.wait_send()
        total = buf[0]
        for d in range(1, N_DEV):
            total = total + buf[d]
        o_ref[...] = total

    vm = pl.BlockSpec(memory_space=pltpu.VMEM)
    return pl.pallas_call(
        kern, name=name, in_specs=[vm], out_specs=vm, out_shape=_sds(vec.shape, F32),
        scratch_shapes=[pltpu.VMEM((N_DEV, r, LANES), F32), pltpu.SemaphoreType.DMA((7,)),
                        pltpu.SemaphoreType.DMA((7,))],
    )(vec)


def exchange_with_sibling(grads, name):
    n = len(grads)

    def kern(*refs):
        ins, outs = refs[:n], refs[n:2 * n]
        send_sems, recv_sems = refs[2 * n:]
        x, y, c, _ = _place()
        copies = []
        for t in range(n):
            for p in range(4):
                cp = pltpu.make_async_remote_copy(
                    src_ref=ins[t].at[2 * p + 1 - c], dst_ref=outs[t].at[p],
                    send_sem=send_sems.at[4 * t + p], recv_sem=recv_sems.at[4 * t + p],
                    device_id=(x, y, 1 - c), device_id_type=MESH)
                cp.start()
                copies.append(cp)
        for cp in copies:
            cp.wait_recv()
        for cp in copies:
            cp.wait_send()

    return pl.pallas_call(
        kern, name=name, in_specs=[ANY] * n, out_specs=[ANY] * n,
        out_shape=[_sds((4,) + g.shape[1:], g.dtype) for g in grads],
        scratch_shapes=[pltpu.SemaphoreType.DMA((4 * n,)), pltpu.SemaphoreType.DMA((4 * n,))],
    )(*grads)


def exchange_between_chips(partials, layers, kinds, name):
    n = len(partials)
    n_kind = max(kinds) + 1
    shapes = []
    for kd in range(n_kind):
        idx = [i for i in range(n) if kinds[i] == kd]
        nl = max(layers[i] for i in idx) + 1
        shapes.append(_sds((nl,) + partials[idx[0]].shape, partials[idx[0]].dtype))

    def kern(*refs):
        ins, outs = refs[:n], refs[n:n + n_kind]
        send_sems, recv_sems, local_sems = refs[n + n_kind:]
        x, y, c, chips = _place()
        mine = 2 * x + y
        local = []
        copies = []
        for t in range(n):
            dst = outs[kinds[t]].at[layers[t], mine]
            lc = pltpu.make_async_copy(ins[t].at[mine], dst, local_sems.at[t])
            lc.start()
            local.append(lc)
            for j, (qx, qy) in enumerate(chips):
                cp = pltpu.make_async_remote_copy(
                    src_ref=ins[t].at[2 * qx + qy], dst_ref=dst,
                    send_sem=send_sems.at[3 * t + j], recv_sem=recv_sems.at[3 * t + j],
                    device_id=(qx, qy, c), device_id_type=MESH)
                cp.start()
                copies.append(cp)
        for cp in copies:
            cp.wait_recv()
        for cp in copies:
            cp.wait_send()
        for lc in local:
            lc.wait()

    return pl.pallas_call(
        kern, name=name, in_specs=[ANY] * n, out_specs=[ANY] * n_kind, out_shape=shapes,
        scratch_shapes=[pltpu.SemaphoreType.DMA((3 * n,)), pltpu.SemaphoreType.DMA((3 * n,)),
                        pltpu.SemaphoreType.DMA((n,))],
    )(*partials)


def scatter_start(partials, name):
    n = len(partials)
    lands = [lax.empty(p.shape, p.dtype) for p in partials]

    def kern(*refs):
        ins, lnd = refs[:n], refs[n:2 * n]
        send_sems, recv_sems, local_sems = refs[2 * n:2 * n + 3]
        token = refs[-1]
        x, y, c, chips = _place()
        mine = 2 * x + y
        for t in range(n):
            pltpu.make_async_copy(ins[t].at[mine], lnd[t].at[mine], local_sems.at[t]).start()
            for j, (qx, qy) in enumerate(chips):
                pltpu.make_async_remote_copy(
                    src_ref=ins[t].at[2 * qx + qy], dst_ref=lnd[t].at[mine],
                    send_sem=send_sems.at[3 * t + j], recv_sem=recv_sems.at[3 * t + j],
                    device_id=(qx, qy, c), device_id_type=MESH).start()
        token[...] = jnp.zeros_like(token)

    args = [pltpu.with_memory_space_constraint(a, pltpu.HBM) for a in list(partials) + lands]
    return pl.pallas_call(
        kern, name=name,
        out_shape=(pltpu.SemaphoreType.DMA((3 * n,)), pltpu.SemaphoreType.DMA((3 * n,)),
                   pltpu.SemaphoreType.DMA((n,)),
                   *[pltpu.HBM(a.shape, a.dtype) for a in args], _sds((8, LANES), F32)),
        in_specs=[HBM] * (2 * n),
        out_specs=(SEM, SEM, SEM, *[HBM] * (2 * n), pl.BlockSpec(memory_space=pltpu.VMEM)),
        input_output_aliases={i: 3 + i for i in range(2 * n)},
        compiler_params=pltpu.CompilerParams(has_side_effects=DATAFLOW),
    )(*args)


def scatter_wait(send_sems, recv_sems, local_sems, partials, lands, after, name):
    n = len(partials)

    def kern(*refs):
        ins, lnd = refs[:n], refs[n:2 * n]
        send_sems, recv_sems, local_sems = refs[2 * n:2 * n + 3]
        x, y, c, chips = _place()
        mine = 2 * x + y
        for t in range(n):
            pltpu.make_async_copy(ins[t].at[mine], lnd[t].at[mine], local_sems.at[t]).wait()
            for j, (qx, qy) in enumerate(chips):
                cp = pltpu.make_async_remote_copy(
                    src_ref=ins[t].at[2 * qx + qy], dst_ref=lnd[t].at[mine],
                    send_sem=send_sems.at[3 * t + j], recv_sem=recv_sems.at[3 * t + j],
                    device_id=(qx, qy, c), device_id_type=MESH)
                cp.wait_send()
                cp.wait_recv()

    bufs = list(partials) + list(lands)
    outs = pl.pallas_call(
        kern, name=name, out_shape=tuple(pltpu.HBM(a.shape, a.dtype) for a in bufs),
        in_specs=[HBM] * (2 * n) + [SEM, SEM, SEM, ANY], out_specs=[HBM] * (2 * n),
        input_output_aliases={i: i for i in range(2 * n)},
        compiler_params=pltpu.CompilerParams(has_side_effects=DATAFLOW),
    )(*bufs, send_sems, recv_sems, local_sems, after)
    return outs[n:]


def adamw_layers(parts, w, m, v, name):
    nl, kdim, n = w.shape
    tk = _row_tile(kdim, n)

    def kern(*refs):
        p_refs = refs[:nl]
        w_ref, m_ref, v_ref, g_ref, d_ref, m2_ref, v2_ref = refs[nl:]
        for l in range(nl):
            @pl.when(pl.program_id(0) == l)
            def _():
                g = p_refs[l][0].astype(F32)
                for q in range(1, 4):
                    g = g + p_refs[l][q].astype(F32)
                d, m2, v2 = _adam_update(w_ref[...], g, m_ref[...], v_ref[...])
                g_ref[...] = g
                d_ref[...] = d
                m2_ref[...] = m2
                v2_ref[...] = v2

    def part_spec(l):
        return pl.BlockSpec((4, tk, n), lambda li, i: (0, jnp.where(li == l, i, 0), 0))

    blk = pl.BlockSpec((None, tk, n), lambda li, i: (li, i, 0))
    return pl.pallas_call(
        kern, name=name, grid=(nl, kdim // tk),
        in_specs=[part_spec(l) for l in range(nl)] + [blk, blk, blk],
        out_specs=[blk] * 4, out_shape=[_sds(w.shape, F32)] * 4,
        compiler_params=_params("arbitrary", "arbitrary"),
    )(*parts, w, m, v)


def _pack(arrays):
    flat = jnp.concatenate([a.reshape(-1).astype(F32) for a in arrays])
    pad = (-flat.shape[0]) % (8 * LANES)
    return jnp.pad(flat, (0, pad)).reshape(-1, LANES)


def _unpack(packed, shapes):
    flat = packed.reshape(-1)
    out, off = [], 0
    for s in shapes:
        n = math.prod(s)
        out.append(flat[off:off + n].reshape(s))
        off += n
    return out


def _to_heads(x2d, dil, n_heads, dh):
    t = x2d.shape[0]
    return x2d.reshape(t // dil, dil, n_heads, dh).transpose(2, 1, 0, 3).reshape(n_heads, t, dh)


def _from_heads(xh, dil):
    h, t, w = xh.shape
    return xh.reshape(h, dil, t // dil, w).transpose(2, 1, 0, 3).reshape(t, h * w)


def _unperm(xh, dil):
    h, t, w = xh.shape
    return xh.reshape(h, dil, t // dil, w).transpose(0, 2, 1, 3).reshape(h, t, w)


def _perm(xh, dil):
    h, t, w = xh.shape
    return xh.reshape(h, t // dil, dil, w).transpose(0, 2, 1, 3).reshape(h, t, w)


def local_step(x, target, norm_mix_g, norm_mlp_g, final_norm_g, lbs, hgrn_norm_g, sinks,
               bq_full, bo_full, weights_get, weights_mid, grads_ready):
    t, d = x.shape
    depth = norm_mix_g.shape[0]
    na = d // 2 // A_DIM
    nbh = d // 2 // B_DIM
    nq = d // C_DIM
    nkv = nq // C_GROUP
    a_w = 3 * na * A_DIM
    c_w = (nq + 2 * nkv) * C_DIM
    tabs_a = rope_tables(t, A_DIM)
    tabs_c = rope_tables(t, C_DIM)
    saved = []
    for l in range(depth):
        s = {"x_in": x}
        (win_g, wout_g, w1_g, w2_g), token = weights_get(l, x)
        s.update(win=win_g, wout=wout_g, w1=w1_g, w2=w2_g)
        h = rms_fwd(x, norm_mix_g[l] + token, "norm_mix_fwd")
        s["h"] = h
        if l % 2 == 0:
            e = l // 2
            proj = mm_cols_sharded(h, win_g, 0, "even_in_proj")[0]
            qkv_r = rope_call(proj, tabs_a, a_w, 2 * na, False, "rope_a")[0]
            nums, ms, ls, hms = [], [], [], []
            for window, dil in A_BRANCHES:
                hm = _to_heads(qkv_r, dil, 3 * na, A_DIM)
                num, m, lsum = band_fwd(hm, 0, na, 2 * na, na, 1, t // dil // BLK, window // dil,
                                        f"dilated_fwd_{dil}")
                hms.append(hm)
                nums.append(_unperm(num, dil))
                ms.append(_unperm(m, dil))
                ls.append(_unperm(lsum, dil))
            oa, lse = merge_branches(nums, ms, ls, "dilated_merge")
            lb_e = lbs[e].reshape(nbh, 1, B_DIM)
            gn_e = hgrn_norm_g[e].reshape(1, B_DIM)
            ob, opre, states, amats = hgrn_fwd(proj, 3 * na, nbh, lb_e, gn_e, "hgrn_fwd")
            mixed = jnp.concatenate([_from_heads(oa, 1).astype(BF16), ob], axis=1)
            x = mm_rows_sharded(mixed, wout_g, 0, "even_out_proj", [x], ["tile"], _ep_residual)
            s.update(proj=proj, hms=hms, oa=oa, lse=lse, opre=opre, states=states, amats=amats,
                     mixed=mixed, lb=lb_e, gn=gn_e)
        else:
            o = l // 2
            wq = win_g[:, 0].transpose(1, 0, 2).reshape(d, c_w)
            proj = mm_plain(h, wq, "odd_qkv_proj", [bq_full[o].reshape(1, c_w)], ["row"], _ep_bias)
            qkv_r = rope_call(proj, tabs_c, c_w, (nq + nkv) * C_DIM // LANES, False, "rope_c")[0]
            hm = _to_heads(qkv_r, 1, nq + 2 * nkv, C_DIM)
            sink_rows = jnp.repeat(sinks[o].reshape(nkv, C_GROUP), BLK, axis=1).reshape(
                nkv, C_GROUP * BLK, 1)
            num, m, lsum = band_fwd(hm, 0, nq, nq + nkv, nkv, C_GROUP, t // BLK, C_WINDOW - 1,
                                    "swa_fwd", sink_rows=sink_rows)
            o_hm, lse = normalise_heads(num, m, lsum, "swa_normalise")
            attn = _from_heads(o_hm, 1).astype(BF16)
            x = mm_rows_sharded(attn, wout_g, 0, "odd_out_proj", [bo_full[o].reshape(1, d), x],
                                ["row", "tile"], _ep_bias_residual)
            s.update(wq=wq, hm=hm, sink_rows=sink_rows, o_hm=o_hm, lse=lse, attn=attn)
        s["x_mid"] = x
        h2 = rms_fwd(x, norm_mlp_g[l] + weights_mid(l, x), "norm_mlp_fwd")
        u, act = mm_cols_sharded(h2, w1_g, 0, "mlp_up", epilogue=_ep_relu2, n_out=2)
        x = mm_rows_sharded(act, w2_g, 0, "mlp_down", [x], ["tile"], _ep_residual)
        s.update(h2=h2, u=u, act=act)
        saved.append(s)

    dx, dxb, dg_final, loss_part = loss_head(x, final_norm_g, target, "loss_head")
    big = []
    small = {"final": dg_final, "loss": loss_part, "mix": [None] * depth, "mlp": [None] * depth,
             "lb": {}, "gn": {}, "sinks": {}, "bq": {}, "bo": {}}
    for l in reversed(range(depth)):
        s = saved[l]
        win_g, wout_g, w1_g, w2_g = s["win"], s["wout"], s["w1"], s["w2"]
        big.append(("w2", l, mm_tn(s["act"], dxb, "mlp_down_dw").reshape(N_DEV, -1, d)))
        du = mm_nt_rows_sharded(dxb, w2_g, 0, "mlp_down_dx", extras=[s["u"]],
                                epilogue=_ep_relu2_bwd, out_dtype=BF16)
        big.append(("w1", l, mm_tn(s["h2"], du, "mlp_up_dw", shard_cols=w1_g.shape[-1])))
        dh2 = mm_nt_cols_sharded(du, w1_g, 0, "mlp_up_dx")
        token = grads_ready(l, big[-2:])
        dx, dxb, dg, col_dx = rms_bwd(s["x_mid"], norm_mlp_g[l] + token, dh2, dx, "norm_mlp_bwd")
        small["mlp"][l] = dg
        if l % 2 == 0:
            e = l // 2
            big.append(("wout", e, mm_tn(s["mixed"], dxb, "even_out_dw").reshape(N_DEV, -1, d)))
            dmixed = mm_nt_rows_sharded(dxb, wout_g, 0, "even_out_dx")
            do_hm = _to_heads(dmixed[:, :na * A_DIM], 1, na, A_DIM)
            delta = head_delta(s["oa"], do_hm, "dilated_delta")
            dsum = None
            for (window, dil), hm in zip(A_BRANCHES, s["hms"]):
                dq, dk, dv = band_bwd(hm, 0, na, 2 * na, _perm(do_hm, dil).astype(BF16),
                                      _perm(s["lse"], dil), _perm(delta, dil), na, 1,
                                      t // dil // BLK, window // dil, f"dilated_bwd_{dil}")
                part = _from_heads(jnp.concatenate([dq, dk, dv], axis=0), dil)
                dsum = part if dsum is None else dsum + part
            dqkv_a = rope_call(dsum, tabs_a, a_w, 2 * na, True, "rope_a_bwd")[0]
            dqb, dfb, dib, dgb, dgn, dlb = hgrn_bwd(s["proj"], 3 * na, nbh, s["lb"], s["gn"],
                                                    s["opre"], s["states"], s["amats"], dmixed, na,
                                                    "hgrn_bwd")
            small["gn"][e] = dgn
            small["lb"][e] = dlb
            dproj = jnp.concatenate([dqkv_a, dqb, dfb, dib, dgb], axis=1)
            big.append(("win", e, mm_tn(s["h"], dproj, "even_in_dw", shard_cols=win_g.shape[-1])))
            dh = mm_nt_cols_sharded(dproj, win_g, 0, "even_in_dx")
        else:
            o = l // 2
            small["bo"][o] = col_dx
            big.append(("wo", o, mm_tn(s["attn"], dxb, "odd_out_dw").reshape(N_DEV, -1, d)))
            dattn = mm_nt_rows_sharded(dxb, wout_g, 0, "odd_out_dx")
            do_hm = _to_heads(dattn, 1, nq, C_DIM)
            delta = head_delta(s["o_hm"], do_hm, "swa_delta")
            dq, dk, dv, dsink = band_bwd(s["hm"], 0, nq, nq + nkv, do_hm.astype(BF16), s["lse"],
                                         delta, nkv, C_GROUP, t // BLK, C_WINDOW - 1, "swa_bwd",
                                         sink_rows=s["sink_rows"])
            small["sinks"][o] = dsink
            dqkv = _from_heads(jnp.concatenate([dq, dk, dv], axis=0), 1)
            dproj, dbq = rope_call(dqkv, tabs_c, c_w, (nq + nkv) * C_DIM // LANES, True,
                                   "rope_c_bwd", col_sum=True)
            small["bq"][o] = dbq
            dwq = mm_tn(s["h"], dproj, "odd_qkv_dw", tn=512)
            big.append(("wqkv", o, dwq.reshape(d, N_DEV, -1).transpose(1, 0, 2)))
            dh = mm_nt_plain(dproj, s["wq"], "odd_qkv_dx", tk=c_w)
        token = grads_ready(l, big[-2:])
        dx, dxb, dg, _ = rms_bwd(s["x_in"], norm_mix_g[l] + token, dh, dx, "norm_mix_bwd")
        small["mix"][l] = dg
    return dx, small


def kernel(x, norm_mix_g, norm_mlp_g, final_norm_g, even_w_in, even_w_out, hgrn_lb_raw, hgrn_norm_g, odd_w_qkv, odd_b_qkv, odd_sinks, odd_w_o, odd_b_o, mlp_w1, mlp_w2, loss_target, m_norm_mix_g, m_norm_mlp_g, m_final_norm_g, m_even_w_in, m_even_w_out, m_hgrn_lb_raw, m_hgrn_norm_g, m_odd_w_qkv, m_odd_b_qkv, m_odd_sinks, m_odd_w_o, m_odd_b_o, m_mlp_w1, m_mlp_w2, v_norm_mix_g, v_norm_mlp_g, v_final_norm_g, v_even_w_in, v_even_w_out, v_hgrn_lb_raw, v_hgrn_norm_g, v_odd_w_qkv, v_odd_b_qkv, v_odd_sinks, v_odd_w_o, v_odd_b_o, v_mlp_w1, v_mlp_w2):
    d = x.shape[2]
    depth = norm_mix_g.shape[0]
    n_even, n_odd = even_w_in.shape[0], odd_w_qkv.shape[0]
    xi, yi, ci = lax.axis_index("x"), lax.axis_index("y"), lax.axis_index("c")
    dev = 4 * xi + 2 * yi + ci
    core = ci.astype(jnp.int32).reshape(1)

    big_w = {"win": even_w_in, "wout": even_w_out, "wqkv": odd_w_qkv, "wo": odd_w_o,
             "w1": mlp_w1, "w2": mlp_w2}
    big_m = {"win": m_even_w_in, "wout": m_even_w_out, "wqkv": m_odd_w_qkv, "wo": m_odd_w_o,
             "w1": m_mlp_w1, "w2": m_mlp_w2}
    big_v = {"win": v_even_w_in, "wout": v_even_w_out, "wqkv": v_odd_w_qkv, "wo": v_odd_w_o,
             "w1": v_mlp_w1, "w2": v_mlp_w2}
    kinds = list(big_w)
    casts = {k: cast_bf16(big_w[k], f"cast_{k}") for k in kinds}

    def layer_shards(l):
        a, b = ("win", "wout") if l % 2 == 0 else ("wqkv", "wo")
        return [casts[a][l // 2], casts[b][l // 2], casts["w1"][l], casts["w2"][l]]

    bq_w, bo_w = odd_b_qkv.shape[1], odd_b_o.shape[1]
    bq_mine = lax.dynamic_update_slice(jnp.zeros((n_odd, N_DEV * bq_w), F32), odd_b_qkv,
                                       (0, dev * bq_w))
    bo_mine = lax.dynamic_update_slice(jnp.zeros((n_odd, N_DEV * bo_w), F32), odd_b_o,
                                       (0, dev * bo_w))
    biases = all_reduce_small(_pack([bq_mine, bo_mine]), "gather_biases")
    bq_full, bo_full = _unpack(biases, [bq_mine.shape, bo_mine.shape])

    first_level = {}
    second_level = {}
    zero = jnp.zeros((), F32)

    def start_first_level(l, after):
        started = gather_start(layer_shards(l), after, f"gather_start_{l}")
        first_level[l] = started[:-1]
        return started[-1][0, 0]

    def weights_get(l, after):
        if l == 0:
            gathered = all_gather(layer_shards(0), "gather_layer_0")
            token = start_first_level(1, [gathered[0], biases]) if depth > 1 else zero
        else:
            send_sems, recv_sems, *lands = second_level.pop(l)
            gathered = forward_wait(send_sems, recv_sems, lands, after,
                                    f"gather_forward_wait_{l}")
            token = zero
        return [g[:, None] for g in gathered], token

    def weights_mid(l, after):
        if l + 1 >= depth:
            return zero
        n = len(layer_shards(l + 1))
        send_sems, recv_sems, local_sems, *bufs = first_level.pop(l + 1)
        bufs = gather_wait(send_sems, recv_sems, local_sems, bufs[:n], bufs[n:], after,
                           f"gather_wait_{l + 1}")
        started = forward_start(bufs[n:], f"gather_forward_start_{l + 1}")
        second_level[l + 1] = started[:-1]
        token = started[-1][0, 0]
        if l + 2 < depth:
            token = token + start_first_level(l + 2, [started[-1]])
        return token

    scattering = []

    def grads_ready(l, group):
        names = [k for k, _, _ in group]
        grads = [g for _, _, g in group]
        tag = f"{l}_{names[0]}"
        received = exchange_with_sibling(grads, f"scatter_d2d_{tag}")
        partials = [pair_add(g, r, core, f"pair_add_{k}")
                    for k, g, r in zip(names, grads, received)]
        started = scatter_start(partials, f"scatter_start_{tag}")
        scattering.append((tag, names, [li for _, li, _ in group], started[:-1]))
        return started[-1][0, 0]

    lbs, soft = lower_bounds_fwd(hgrn_lb_raw, "lower_bounds")

    dx, small = local_step(x[0], loss_target[0], norm_mix_g, norm_mlp_g, final_norm_g, lbs,
                           hgrn_norm_g, odd_sinks, bq_full, bo_full, weights_get, weights_mid,
                           grads_ready)

    parts = ([small["mix"][l] for l in range(depth)] + [small["mlp"][l] for l in range(depth)]
             + [small["final"]] + [small["lb"][e] for e in range(n_even)]
             + [small["gn"][e] for e in range(n_even)] + [small["sinks"][o] for o in range(n_odd)]
             + [small["bq"][o] for o in range(n_odd)] + [small["bo"][o] for o in range(n_odd)]
             + [small["loss"]])
    shapes = ([(depth, d)] * 2 + [(d,), hgrn_lb_raw.shape, hgrn_norm_g.shape, odd_sinks.shape,
              (n_odd, N_DEV * bq_w), (n_odd, N_DEV * bo_w), (1, LANES)])
    g_mix, g_mlp, g_final, d_lbs, g_gn, g_sinks, g_bq_full, g_bo_full, loss_v = _unpack(
        all_reduce_small(_pack(parts), "reduce_small"), shapes)
    g_lb = lower_bounds_bwd(soft, d_lbs, "lower_bounds_bwd")
    g_bq = lax.dynamic_slice(g_bq_full, (0, dev * bq_w), (n_odd, bq_w))
    g_bo = lax.dynamic_slice(g_bo_full, (0, dev * bo_w), (n_odd, bo_w))
    loss = loss_v[0, 0]

    small_names = ["norm_mix_g", "norm_mlp_g", "final_norm_g", "hgrn_lb_raw", "hgrn_norm_g",
                   "odd_b_qkv", "odd_sinks", "odd_b_o"]
    small_w = [norm_mix_g, norm_mlp_g, final_norm_g, hgrn_lb_raw, hgrn_norm_g, odd_b_qkv,
               odd_sinks, odd_b_o]
    small_m = [m_norm_mix_g, m_norm_mlp_g, m_final_norm_g, m_hgrn_lb_raw, m_hgrn_norm_g,
               m_odd_b_qkv, m_odd_sinks, m_odd_b_o]
    small_v = [v_norm_mix_g, v_norm_mlp_g, v_final_norm_g, v_hgrn_lb_raw, v_hgrn_norm_g,
               v_odd_b_qkv, v_odd_sinks, v_odd_b_o]
    small_g = [g_mix, g_mlp, g_final, g_lb, g_gn, g_bq, g_sinks, g_bo]
    sshapes = [w.shape for w in small_w]
    sd, sm, sv = adamw_small(_pack(small_w), _pack(small_g), _pack(small_m), _pack(small_v),
                             "adamw_small")
    res = {}
    for name, g, dl, m2, v2 in zip(small_names, small_g, _unpack(sd, sshapes),
                                   _unpack(sm, sshapes), _unpack(sv, sshapes)):
        res[name] = (g.reshape(dl.shape), dl, m2, v2)

    landed = {k: [None] * big_w[k].shape[0] for k in kinds}
    for tag, names, layer_idx, (send_sems, recv_sems, local_sems, *bufs) in scattering:
        n = len(names)
        lands = scatter_wait(send_sems, recv_sems, local_sems, bufs[:n], bufs[n:], dx,
                             f"scatter_wait_{tag}")
        for k, li, land in zip(names, layer_idx, lands):
            landed[k][li] = land
    long_names = {"win": "even_w_in", "wout": "even_w_out", "wqkv": "odd_w_qkv", "wo": "odd_w_o",
                  "w1": "mlp_w1", "w2": "mlp_w2"}
    for k in kinds:
        res[long_names[k]] = tuple(adamw_layers(landed[k], big_w[k], big_m[k], big_v[k],
                                                f"adamw_{k}"))

    order = ["norm_mix_g", "norm_mlp_g", "final_norm_g", "even_w_in", "even_w_out", "hgrn_lb_raw",
             "hgrn_norm_g", "odd_w_qkv", "odd_b_qkv", "odd_sinks", "odd_w_o", "odd_b_o", "mlp_w1",
             "mlp_w2"]
    outs = [loss, dx[None]]
    for j in range(4):
        outs += [res[n][j] for n in order]
    return tuple(outs)
```

```python
import functools
import math

import jax
import jax.numpy as jnp
from jax import lax
from jax.experimental import pallas as pl
from jax.experimental.pallas import tpu as pltpu

F32 = jnp.float32
BF16 = jnp.bfloat16
MESH = pl.DeviceIdType.MESH

N_DEV = 8
NORM_EPS = 1e-5
ROPE_THETA = 500000.0
BLK = 128
A_DIM = 128
A_BRANCHES = ((128, 1), (512, 4), (2048, 16))
B_DIM = 128
B_CHUNK = 64
C_DIM = 64
C_GROUP = 8
C_WINDOW = 128
LANES = 128

ADAM_LR = 0.001
ADAM_B1 = 0.9
ADAM_B2 = 0.999
ADAM_EPS = 1e-08
ADAM_WD = 0.01
ADAM_STEP = 10

NN = (((1,), (0,)), ((), ()))
NT = (((1,), (1,)), ((), ()))
TN = (((0,), (0,)), ((), ()))


def _params(*sem):
    return pltpu.CompilerParams(dimension_semantics=sem)


def _sigmoid(x):
    return 1.0 / (1.0 + jnp.exp(-x))


def _rows_call(name, body, row_ins, full_ins, row_outs, acc_outs, tm):
    t = row_ins[0].shape[0]
    n_ri, n_fi, n_ro = len(row_ins), len(full_ins), len(row_outs)

    def kern(*refs):
        i = pl.program_id(0)
        body(i, refs[:n_ri], refs[n_ri:n_ri + n_fi],
             refs[n_ri + n_fi:n_ri + n_fi + n_ro], refs[n_ri + n_fi + n_ro:])

    def row_spec(shape):
        return pl.BlockSpec((tm,) + tuple(shape[1:]), lambda i: (i,) + (0,) * (len(shape) - 1))

    def full_spec(shape):
        return pl.BlockSpec(tuple(shape), lambda i: (0,) * len(shape))

    outs = pl.pallas_call(
        kern, name=name, grid=(t // tm,),
        in_specs=[row_spec(a.shape) for a in row_ins] + [full_spec(a.shape) for a in full_ins],
        out_specs=[row_spec(s.shape) for s in row_outs] + [full_spec(s.shape) for s in acc_outs],
        out_shape=list(row_outs) + list(acc_outs),
        compiler_params=_params("arbitrary" if acc_outs else "parallel"),
    )(*row_ins, *full_ins)
    return outs


def _sds(shape, dtype):
    return jax.ShapeDtypeStruct(tuple(shape), dtype)


def rms_fwd(x, g, name):
    t, d = x.shape

    def body(i, ri, fi, ro, ao):
        xv = ri[0][...]
        r = lax.rsqrt(jnp.mean(xv * xv, axis=-1, keepdims=True) + NORM_EPS)
        ro[0][...] = (xv * r * fi[0][...]).astype(BF16)

    return _rows_call(name, body, [x], [g.reshape(1, d)], [_sds((t, d), BF16)], [], 256)[0]


def rms_bwd(x, g, dh, dx_res, name):
    t, d = x.shape

    def body(i, ri, fi, ro, ao):
        xv, dhv, res = ri[0][...], ri[1][...], ri[2][...]
        gv = fi[0][...]
        r = lax.rsqrt(jnp.mean(xv * xv, axis=-1, keepdims=True) + NORM_EPS)
        gd = gv * dhv
        dx = res + r * gd - xv * (r * r * r) * jnp.mean(xv * gd, axis=-1, keepdims=True)
        ro[0][...] = dx
        ro[1][...] = dx.astype(BF16)

        @pl.when(i == 0)
        def _():
            ao[0][...] = jnp.zeros_like(ao[0])
            ao[1][...] = jnp.zeros_like(ao[1])

        ao[0][...] += jnp.sum(dhv * xv * r, axis=0, keepdims=True)
        ao[1][...] += jnp.sum(dx, axis=0, keepdims=True)

    return _rows_call(name, body, [x, dh, dx_res], [g.reshape(1, d)],
                      [_sds((t, d), F32), _sds((t, d), BF16)],
                      [_sds((1, d), F32), _sds((1, d), F32)], 256)


def loss_head(x, g, target, name):
    t, d = x.shape

    def body(i, ri, fi, ro, ao):
        xv, tg = ri[0][...], ri[1][...]
        gv = fi[0][...]
        r = lax.rsqrt(jnp.mean(xv * xv, axis=-1, keepdims=True) + NORM_EPS)
        e = xv * r * gv - tg
        dy = e * (1.0 / d)
        gd = gv * dy
        dx = r * gd - xv * (r * r * r) * jnp.mean(xv * gd, axis=-1, keepdims=True)
        ro[0][...] = dx
        ro[1][...] = dx.astype(BF16)

        @pl.when(i == 0)
        def _():
            ao[0][...] = jnp.zeros_like(ao[0])
            ao[1][...] = jnp.zeros_like(ao[1])

        ao[0][...] += jnp.sum(dy * xv * r, axis=0, keepdims=True)
        part = 0.5 * jnp.sum(jnp.mean(e * e, axis=-1, keepdims=True), axis=0, keepdims=True)
        ao[1][...] += jnp.broadcast_to(part, (1, LANES))

    return _rows_call(name, body, [x, target], [g.reshape(1, d)],
                      [_sds((t, d), F32), _sds((t, d), BF16)],
                      [_sds((1, d), F32), _sds((1, LANES), F32)], 256)


def rope_tables(seq, head_dim):
    rot = head_dim // 4
    half = rot // 2
    inv_freq = 1.0 / (ROPE_THETA ** (jnp.arange(0, rot, 2, dtype=F32) / rot))
    ang = jnp.arange(seq, dtype=F32)[:, None] * inv_freq[None, :]
    cos, sin = jnp.cos(ang), jnp.sin(ang)
    zeros = jnp.zeros((seq, head_dim - rot), F32)
    zh = jnp.zeros((seq, half), F32)
    c = jnp.concatenate([cos, cos, jnp.ones((seq, head_dim - rot), F32)], axis=-1)
    sp = jnp.concatenate([zh, sin, zeros], axis=-1)
    sm = jnp.concatenate([-sin, zh, zeros], axis=-1)
    rep = LANES // head_dim
    return jnp.tile(c, (1, rep)), jnp.tile(sp, (1, rep)), jnp.tile(sm, (1, rep)), half


def rope_call(x, tabs, width, n_rope, inverse, name, col_sum=False):
    c, sp, sm, half = tabs
    t = x.shape[0]
    tm = 256
    n_slab = width // LANES

    def kern(x_ref, c_ref, sp_ref, sm_ref, o_ref, *acc):
        cv, spv, smv = c_ref[...], sp_ref[...], sm_ref[...]
        for j in range(n_slab):
            xs = x_ref[:, j * LANES:(j + 1) * LANES].astype(F32)
            if j < n_rope:
                if inverse:
                    ys = (xs * cv + pltpu.roll(xs * spv, LANES - half, 1)
                          + pltpu.roll(xs * smv, half, 1))
                else:
                    ys = (xs * cv + pltpu.roll(xs, half, 1) * spv
                          + pltpu.roll(xs, LANES - half, 1) * smv)
            else:
                ys = xs
            o_ref[:, j * LANES:(j + 1) * LANES] = ys.astype(BF16)
            if col_sum:
                @pl.when(pl.program_id(0) == 0)
                def _():
                    acc[0][:, j * LANES:(j + 1) * LANES] = jnp.zeros((1, LANES), F32)
                acc[0][:, j * LANES:(j + 1) * LANES] += jnp.sum(ys, axis=0, keepdims=True)

    tab_spec = pl.BlockSpec((tm, LANES), lambda i: (i, 0))
    out_shape = [_sds((t, width), BF16)]
    out_specs = [pl.BlockSpec((tm, width), lambda i: (i, 0))]
    if col_sum:
        out_shape.append(_sds((1, width), F32))
        out_specs.append(pl.BlockSpec((1, width), lambda i: (0, 0)))
    return pl.pallas_call(
        kern, name=name, grid=(t // tm,),
        in_specs=[pl.BlockSpec((tm, width), lambda i: (i, 0)), tab_spec, tab_spec, tab_spec],
        out_specs=out_specs, out_shape=out_shape,
        compiler_params=_params("arbitrary" if col_sum else "parallel"),
    )(x, c, sp, sm)


def _mm_call(name, a, b, extras, out_shapes, grid, a_spec, b_spec, extra_specs, out_specs,
             acc_shape, dims, epilogue):
    n_ex, n_out = len(extras), len(out_shapes)
    nk = grid[2]

    def product(a_ref, b_ref):
        bv = b_ref[...]
        if bv.ndim == 3:
            bv = bv.reshape(bv.shape[0] * bv.shape[1], bv.shape[2])
        return lax.dot_general(a_ref[...].astype(BF16), bv.astype(BF16), dims,
                               preferred_element_type=F32)

    def kern(*refs):
        a_ref, b_ref = refs[0], refs[1]
        ex = refs[2:2 + n_ex]
        outs = refs[2 + n_ex:2 + n_ex + n_out]
        if nk == 1:
            epilogue(product(a_ref, b_ref), ex, outs)
            return
        acc = refs[-1]
        k = pl.program_id(2)

        @pl.when(k == 0)
        def _():
            acc[...] = product(a_ref, b_ref)

        @pl.when(k > 0)
        def _():
            acc[...] += product(a_ref, b_ref)

        @pl.when(k == nk - 1)
        def _():
            epilogue(acc[...], ex, outs)

    return pl.pallas_call(
        kern, name=name, grid=grid,
        in_specs=[a_spec, b_spec, *extra_specs], out_specs=out_specs, out_shape=out_shapes,
        scratch_shapes=[pltpu.VMEM(acc_shape, F32)] if nk > 1 else [],
        compiler_params=_params("parallel", "parallel", "arbitrary"),
    )(a, b, *extras)


def _ep_store(dtype):
    def ep(acc, ex, outs):
        outs[0][...] = acc.astype(dtype)
    return ep


def _ep_residual(acc, ex, outs):
    outs[0][...] = acc + ex[0][...]


def _ep_bias(acc, ex, outs):
    outs[0][...] = acc + ex[0][...]


def _ep_bias_residual(acc, ex, outs):
    outs[0][...] = acc + ex[0][...] + ex[1][...]


def _ep_relu2(acc, ex, outs):
    outs[0][...] = acc
    rl = jnp.maximum(acc, 0.0)
    outs[1][...] = (rl * rl).astype(BF16)


def _ep_relu2_bwd(acc, ex, outs):
    outs[0][...] = (acc * (2.0 * jnp.maximum(ex[0][...], 0.0))).astype(BF16)


MM_TM = 1024
MM_TN = 1024
MM_TK = 2048


def mm_cols_sharded(a, wg, layer, name, epilogue=None, n_out=1):
    m, kdim = a.shape
    n = wg.shape[-1]
    tm, tk = min(m, MM_TM), min(kdim, MM_TK)
    if epilogue is None:
        epilogue, outs = _ep_store(F32), [_sds((m, N_DEV * n), F32)]
    else:
        outs = [_sds((m, N_DEV * n), F32), _sds((m, N_DEV * n), BF16)][:n_out]
    return _mm_call(
        name, a, wg, [], outs, (m // tm, N_DEV, kdim // tk),
        pl.BlockSpec((tm, tk), lambda i, j, k: (i, k)),
        pl.BlockSpec((None, None, tk, n), lambda i, j, k: (j, layer, k, 0)),
        [], [pl.BlockSpec((tm, n), lambda i, j, k: (i, j))] * len(outs),
        (tm, n), NN, epilogue)


def _extra_specs(extra_kinds, tm, tn):
    specs = []
    for kind in extra_kinds:
        if kind == "row":
            specs.append(pl.BlockSpec((1, tn), lambda i, j, k: (0, j)))
        else:
            specs.append(pl.BlockSpec((tm, tn), lambda i, j, k: (i, j)))
    return specs


def mm_rows_sharded(a, wg, layer, name, extras, extra_kinds, epilogue):
    m, kdim = a.shape
    ks, n = wg.shape[-2], wg.shape[-1]
    tm, tn = min(m, MM_TM), min(n, MM_TN)
    gps = max(1, min(kdim, MM_TK) // ks)
    return _mm_call(
        name, a, wg, extras, [_sds((m, n), F32)], (m // tm, n // tn, N_DEV // gps),
        pl.BlockSpec((tm, gps * ks), lambda i, j, k: (i, k)),
        pl.BlockSpec((gps, None, ks, tn), lambda i, j, k: (k, layer, 0, j)),
        _extra_specs(extra_kinds, tm, tn), [pl.BlockSpec((tm, tn), lambda i, j, k: (i, j))],
        (tm, tn), NN, epilogue)[0]


def mm_plain(a, w, name, extras, extra_kinds, epilogue, tn=512):
    m, kdim = a.shape
    n = w.shape[1]
    tm, tk = min(m, MM_TM), min(kdim, MM_TK)
    return _mm_call(
        name, a, w, extras, [_sds((m, n), F32)], (m // tm, n // tn, kdim // tk),
        pl.BlockSpec((tm, tk), lambda i, j, k: (i, k)),
        pl.BlockSpec((tk, tn), lambda i, j, k: (k, j)),
        _extra_specs(extra_kinds, tm, tn), [pl.BlockSpec((tm, tn), lambda i, j, k: (i, j))],
        (tm, tn), NN, epilogue)[0]


def mm_nt_cols_sharded(dy, wg, layer, name):
    m = dy.shape[0]
    kdim, n = wg.shape[-2], wg.shape[-1]
    tm, tn = min(m, MM_TM), min(kdim, MM_TN)
    return _mm_call(
        name, dy, wg, [], [_sds((m, kdim), F32)], (m // tm, kdim // tn, N_DEV),
        pl.BlockSpec((tm, n), lambda i, j, k: (i, k)),
        pl.BlockSpec((None, None, tn, n), lambda i, j, k: (k, layer, j, 0)),
        [], [pl.BlockSpec((tm, tn), lambda i, j, k: (i, j))],
        (tm, tn), NT, _ep_store(F32))[0]


def mm_nt_rows_sharded(dy, wg, layer, name, extras=(), epilogue=None, out_dtype=F32):
    m, n = dy.shape
    ks = wg.shape[-2]
    tm, tk = min(m, MM_TM), min(n, MM_TK)
    gps = max(1, MM_TN // ks)
    tn = gps * ks
    epilogue = _ep_store(out_dtype) if epilogue is None else epilogue
    return _mm_call(
        name, dy, wg, list(extras), [_sds((m, N_DEV * ks), out_dtype)],
        (m // tm, N_DEV // gps, n // tk),
        pl.BlockSpec((tm, tk), lambda i, j, k: (i, k)),
        pl.BlockSpec((gps, None, ks, tk), lambda i, j, k: (j, layer, 0, k)),
        [pl.BlockSpec((tm, tn), lambda i, j, k: (i, j))] * len(extras),
        [pl.BlockSpec((tm, tn), lambda i, j, k: (i, j))],
        (tm, tn), NT, epilogue)[0]


def mm_nt_plain(dy, w, name, tk):
    m, n = dy.shape
    kdim = w.shape[0]
    tm, tn = min(m, MM_TM), min(kdim, MM_TN)
    return _mm_call(
        name, dy, w, [], [_sds((m, kdim), F32)], (m // tm, kdim // tn, n // tk),
        pl.BlockSpec((tm, tk), lambda i, j, k: (i, k)),
        pl.BlockSpec((tn, tk), lambda i, j, k: (j, k)),
        [], [pl.BlockSpec((tm, tn), lambda i, j, k: (i, j))],
        (tm, tn), NT, _ep_store(F32))[0]


def mm_tn(a, dy, name, shard_cols=None, tn=MM_TN):
    t, kdim = a.shape
    n = dy.shape[1]
    tm, tk = min(kdim, MM_TM), min(t, MM_TK)
    if shard_cols is None:
        tn = min(tn, n)
        out = _sds((kdim, n), BF16)
        o_spec = pl.BlockSpec((tm, tn), lambda i, j, k: (i, j))
    else:
        tn = shard_cols
        out = _sds((n // tn, kdim, tn), BF16)
        o_spec = pl.BlockSpec((None, tm, tn), lambda i, j, k: (j, i, 0))
    return _mm_call(
        name, a, dy, [], [out], (kdim // tm, n // tn, t // tk),
        pl.BlockSpec((tk, tm), lambda i, j, k: (k, i)),
        pl.BlockSpec((tk, tn), lambda i, j, k: (k, j)),
        [], [o_spec], (tm, tn), TN, _ep_store(BF16))[0]


BAND_BLOCKS_PER_STEP = 4


def _band_mask(g, nk_prev_valid, max_dist):
    rows = lax.broadcasted_iota(jnp.int32, (g * BLK, 2 * BLK), 0) % BLK
    cols = lax.broadcasted_iota(jnp.int32, (g * BLK, 2 * BLK), 1)
    dist = rows + BLK - cols
    ok = (dist >= 0) & (dist <= max_dist)
    return ok & ((cols >= BLK) | nk_prev_valid)


def band_fwd(qkv, q0, k0, v0, hk, g, seg, max_dist, name, sink_rows=None, normalise=False):
    t, dh = qkv.shape[1], qkv.shape[2]
    nb = t // BLK
    rb = BAND_BLOCKS_PER_STEP // g if g < BAND_BLOCKS_PER_STEP else 1
    rows = rb * BLK
    scale = dh ** -0.5
    has_sink = sink_rows is not None

    def kern(*refs):
        if has_sink:
            q_ref, k_ref, v_ref, s_ref, num_ref, m_ref, *l_ref = refs
            sink = s_ref[...]
        else:
            q_ref, k_ref, v_ref, num_ref, m_ref, *l_ref = refs
        for r in range(rb):
            b = pl.program_id(1) * rb + r
            cur = pl.multiple_of(b * BLK, BLK)
            prev = pl.multiple_of(jnp.maximum(b - 1, 0) * BLK, BLK)
            here = slice(r * BLK, (r + 1) * BLK)
            q = q_ref[:, here, :].reshape(g * BLK, dh)
            kk = jnp.concatenate([k_ref[pl.ds(prev, BLK), :], k_ref[pl.ds(cur, BLK), :]], axis=0)
            vv = jnp.concatenate([v_ref[pl.ds(prev, BLK), :], v_ref[pl.ds(cur, BLK), :]], axis=0)
            s = lax.dot_general(q, kk, NT, preferred_element_type=F32) * scale
            s = jnp.where(_band_mask(g, (b % seg) != 0, max_dist), s, -jnp.inf)
            m = jnp.max(s, axis=-1, keepdims=True)
            if has_sink:
                m = jnp.maximum(m, sink)
            p = jnp.exp(s - m)
            l = jnp.sum(p, axis=-1, keepdims=True)
            if has_sink:
                l = l + jnp.exp(sink - m)
            num = jnp.dot(p.astype(BF16), vv, preferred_element_type=F32)
            if normalise:
                num_ref[:, here, :] = (num / l).reshape(g, BLK, dh)
                m_ref[:, here, :] = (m + jnp.log(l)).reshape(g, BLK, 1)
            else:
                num_ref[:, here, :] = num.reshape(g, BLK, dh)
                m_ref[:, here, :] = m.reshape(g, BLK, 1)
                l_ref[0][:, here, :] = l.reshape(g, BLK, 1)

    in_specs = [pl.BlockSpec((g, rows, dh), lambda h, b: (q0 // g + h, b, 0)),
                pl.BlockSpec((None, t, dh), lambda h, b: (k0 + h, 0, 0)),
                pl.BlockSpec((None, t, dh), lambda h, b: (v0 + h, 0, 0))]
    args = [qkv, qkv, qkv]
    if has_sink:
        in_specs.append(pl.BlockSpec((None, g * BLK, 1), lambda h, b: (h, 0, 0)))
        args.append(sink_rows)
    hq = hk * g
    n_col = 1 if normalise else 2
    return pl.pallas_call(
        kern, name=name, grid=(hk, nb // rb), in_specs=in_specs,
        out_specs=[pl.BlockSpec((g, rows, dh), lambda h, b: (h, b, 0))]
        + [pl.BlockSpec((g, rows, 1), lambda h, b: (h, b, 0))] * n_col,
        out_shape=[_sds((hq, t, dh), F32)] + [_sds((hq, t, 1), F32)] * n_col,
        compiler_params=_params("parallel", "parallel"),
    )(*args)


def band_bwd(qkv, q0, k0, v0, do, lse, delta, hk, g, seg, max_dist, name, sink_rows=None,
             delta_from_o=False):
    t, dh = qkv.shape[1], qkv.shape[2]
    nb = t // BLK
    rb = BAND_BLOCKS_PER_STEP // g if g < BAND_BLOCKS_PER_STEP else 1
    scale = dh ** -0.5
    has_sink = sink_rows is not None

    def kern(*refs):
        if has_sink:
            (q_ref, k_ref, v_ref, do_ref, lse_ref, dl_ref, s_ref,
             dq_ref, dk_ref, dv_ref, ds_ref, sacc) = refs
            sink = s_ref[...]
        else:
            q_ref, k_ref, v_ref, do_ref, lse_ref, dl_ref, dq_ref, dk_ref, dv_ref = refs
        step = pl.program_id(1)

        @pl.when(step == 0)
        def _():
            dk_ref[...] = jnp.zeros_like(dk_ref)
            dv_ref[...] = jnp.zeros_like(dv_ref)
            if has_sink:
                sacc[...] = jnp.zeros_like(sacc)

        for r in range(rb):
            b = step * rb + r
            cur = pl.multiple_of(b * BLK, BLK)
            prev = pl.multiple_of(jnp.maximum(b - 1, 0) * BLK, BLK)
            here = slice(r * BLK, (r + 1) * BLK)
            q = q_ref[:, here, :].reshape(g * BLK, dh)
            dout = do_ref[:, here, :].reshape(g * BLK, dh)
            lse_b = lse_ref[:, here, :].reshape(g * BLK, 1)
            if delta_from_o:
                dl_b = jnp.sum(dl_ref[:, here, :].reshape(g * BLK, dh) * dout, axis=-1,
                               keepdims=True)
                dout = dout.astype(BF16)
            else:
                dl_b = dl_ref[:, here, :].reshape(g * BLK, 1)
            kk = jnp.concatenate([k_ref[pl.ds(prev, BLK), :], k_ref[pl.ds(cur, BLK), :]], axis=0)
            vv = jnp.concatenate([v_ref[pl.ds(prev, BLK), :], v_ref[pl.ds(cur, BLK), :]], axis=0)
            s = lax.dot_general(q, kk, NT, preferred_element_type=F32) * scale
            s = jnp.where(_band_mask(g, (b % seg) != 0, max_dist), s, -jnp.inf)
            p = jnp.exp(s - lse_b)
            dp = lax.dot_general(dout, vv, NT, preferred_element_type=F32)
            ds = (p * (dp - dl_b) * scale).astype(BF16)
            dq = jnp.dot(ds, kk, preferred_element_type=F32)
            dq_ref[:, here, :] = dq.reshape(g, BLK, dh)
            dkk = lax.dot_general(ds, q, TN, preferred_element_type=F32)
            dvv = lax.dot_general(p.astype(BF16), dout, TN, preferred_element_type=F32)
            dk_ref[pl.ds(prev, BLK), :] += dkk[:BLK]
            dk_ref[pl.ds(cur, BLK), :] += dkk[BLK:]
            dv_ref[pl.ds(prev, BLK), :] += dvv[:BLK]
            dv_ref[pl.ds(cur, BLK), :] += dvv[BLK:]
            if has_sink:
                sacc[...] += -jnp.exp(sink - lse_b) * dl_b

        if has_sink:
            @pl.when(step == nb // rb - 1)
            def _():
                for gi in range(g):
                    ds_ref[gi:gi + 1, :] = jnp.sum(sacc[gi * BLK:(gi + 1) * BLK, :], axis=0,
                                                   keepdims=True)

    rows = rb * BLK
    in_specs = [pl.BlockSpec((g, rows, dh), lambda h, b: (q0 // g + h, b, 0)),
                pl.BlockSpec((None, t, dh), lambda h, b: (k0 + h, 0, 0)),
                pl.BlockSpec((None, t, dh), lambda h, b: (v0 + h, 0, 0)),
                pl.BlockSpec((g, rows, dh), lambda h, b: (h, b, 0)),
                pl.BlockSpec((g, rows, 1), lambda h, b: (h, b, 0)),
                pl.BlockSpec((g, rows, dh if delta_from_o else 1), lambda h, b: (h, b, 0))]
    args = [qkv, qkv, qkv, do, lse, delta]
    hq = hk * g
    out_specs = [pl.BlockSpec((g, rows, dh), lambda h, b: (h, b, 0)),
                 pl.BlockSpec((None, t, dh), lambda h, b: (h, 0, 0)),
                 pl.BlockSpec((None, t, dh), lambda h, b: (h, 0, 0))]
    out_shape = [_sds((hq, t, dh), F32), _sds((hk, t, dh), F32), _sds((hk, t, dh), F32)]
    scratch = []
    if has_sink:
        in_specs.append(pl.BlockSpec((None, g * BLK, 1), lambda h, b: (h, 0, 0)))
        args.append(sink_rows)
        out_specs.append(pl.BlockSpec((None, g, 1), lambda h, b: (h, 0, 0)))
        out_shape.append(_sds((hk, g, 1), F32))
        scratch.append(pltpu.VMEM((g * BLK, 1), F32))
    return pl.pallas_call(
        kern, name=name, grid=(hk, nb // rb), in_specs=in_specs, out_specs=out_specs,
        out_shape=out_shape,
        scratch_shapes=scratch, compiler_params=_params("parallel", "arbitrary"),
    )(*args)


def merge_branches(nums, ms, ls, name):
    h, t, dh = nums[0].shape
    nbr = len(nums)

    def kern(*refs):
        num_refs, m_refs, l_refs = refs[:nbr], refs[nbr:2 * nbr], refs[2 * nbr:3 * nbr]
        o_ref, lse_ref = refs[3 * nbr], refs[3 * nbr + 1]
        mall = m_refs[0][...]
        for i in range(1, nbr):
            mall = jnp.maximum(mall, m_refs[i][...])
        num = jnp.zeros((t, dh), F32)
        den = jnp.zeros((t, 1), F32)
        for i in range(nbr):
            w = jnp.exp(m_refs[i][...] - mall)
            num = num + w * num_refs[i][...]
            den = den + w * l_refs[i][...]
        o_ref[...] = num / den
        lse_ref[...] = mall + jnp.log(den)

    big = pl.BlockSpec((None, t, dh), lambda i: (i, 0, 0))
    col = pl.BlockSpec((None, t, 1), lambda i: (i, 0, 0))
    return pl.pallas_call(
        kern, name=name, grid=(h,), in_specs=[big] * nbr + [col] * (2 * nbr),
        out_specs=[big, col], out_shape=[_sds((h, t, dh), F32), _sds((h, t, 1), F32)],
        compiler_params=_params("parallel"),
    )(*nums, *ms, *ls)


def normalise_heads(num, m, l, name):
    h, t, dh = num.shape

    def kern(num_ref, m_ref, l_ref, o_ref, lse_ref):
        lv = l_ref[...]
        o_ref[...] = num_ref[...] / lv
        lse_ref[...] = m_ref[...] + jnp.log(lv)

    big = pl.BlockSpec((None, t, dh), lambda i: (i, 0, 0))
    col = pl.BlockSpec((None, t, 1), lambda i: (i, 0, 0))
    return pl.pallas_call(
        kern, name=name, grid=(h,), in_specs=[big, col, col], out_specs=[big, col],
        out_shape=[_sds((h, t, dh), F32), _sds((h, t, 1), F32)],
        compiler_params=_params("parallel"),
    )(num, m, l)


def head_delta(o, do, name):
    h, t, dh = o.shape

    def kern(o_ref, do_ref, d_ref):
        d_ref[...] = jnp.sum(o_ref[...] * do_ref[...], axis=-1, keepdims=True)

    big = pl.BlockSpec((None, t, dh), lambda i: (i, 0, 0))
    return pl.pallas_call(
        kern, name=name, grid=(h,), in_specs=[big, big],
        out_specs=pl.BlockSpec((None, t, 1), lambda i: (i, 0, 0)),
        out_shape=_sds((h, t, 1), F32), compiler_params=_params("parallel"),
    )(o, do)


def _cumsum_rows(x, n, reverse=False):
    rows = lax.broadcasted_iota(jnp.int32, x.shape, 0)
    shift = 1
    while shift < n:
        if reverse:
            x = x + jnp.where(rows < n - shift, pltpu.roll(x, n - shift, 0), 0.0)
        else:
            x = x + jnp.where(rows >= shift, pltpu.roll(x, shift, 0), 0.0)
        shift *= 2
    return x


def _hgrn_gates(f, lb):
    sig = _sigmoid(f)
    gate = lb + (1.0 - lb) * sig
    return sig, gate


B_SUB = 16


def _dot3(a, b, dims):
    ah, bh = a.astype(BF16), b.astype(BF16)
    al = (a - ah.astype(F32)).astype(BF16)
    bl = (b - bh.astype(F32)).astype(BF16)
    dot = functools.partial(lax.dot_general, dimension_numbers=dims, preferred_element_type=F32)
    return dot(ah, bh) + dot(al, bh) + dot(ah, bl)


def _sub_scales(b, i):
    r0 = i * B_SUB
    beta = b[r0 - 1:r0, :]
    return jnp.exp(b[r0:r0 + B_SUB, :] - beta), jnp.exp(jnp.minimum(beta - b, 0.0))


def _hgrn_intra_attn(qq, kk, b):
    c = qq.shape[0]
    lane = lax.broadcasted_iota(jnp.int32, (B_SUB, c), 1)
    trow = lax.broadcasted_iota(jnp.int32, (B_SUB, B_DIM), 0)
    blocks = []
    for i in range(c // B_SUB):
        r0 = i * B_SUB
        qi, bi = qq[r0:r0 + B_SUB, :], b[r0:r0 + B_SUB, :]
        if i == 0:
            a_i = jnp.zeros((B_SUB, c), F32)
        else:
            eq, ek = _sub_scales(b, i)
            a_i = jnp.where(lane < r0, _dot3(qi * eq, kk * ek, NT), 0.0)
        for sl in range(B_SUB):
            s = r0 + sl
            e = jnp.exp(jnp.where(trow >= sl, bi - b[s:s + 1, :], -jnp.inf))
            col = jnp.sum(qi * kk[s:s + 1, :] * e, axis=1, keepdims=True)
            a_i = jnp.where(lane == s, col, a_i)
        blocks.append(a_i)
    return jnp.concatenate(blocks, axis=0)


def hgrn_fwd(proj, col0, nh, lb, gn, name):
    t = proj.shape[0]
    c = B_CHUNK
    nc = t // c
    scale = B_DIM ** -0.5

    def kern(q_ref, f_ref, i_ref, g_ref, lb_ref, gn_ref, out_ref, opre_ref, st_ref, a_ref, state):
        lbv = lb_ref[...]
        gnv = gn_ref[...]
        state[...] = jnp.zeros_like(state)

        def chunk(ci, carry):
            rows = pl.ds(pl.multiple_of(ci * c, c), c)
            _, gate = _hgrn_gates(f_ref[rows, :], lbv)
            kk = 1.0 - gate
            qb = q_ref[rows, :]
            qq = qb * _sigmoid(qb) * scale
            v = i_ref[rows, :]
            b = _cumsum_rows(jnp.log(gate), c)
            st = state[...]
            st_ref[ci] = st
            o_inter = lax.dot_general((qq * jnp.exp(b)).astype(BF16), st.astype(BF16), NT,
                                      preferred_element_type=F32)
            amat = _hgrn_intra_attn(qq, kk, b)
            a_ref[ci] = amat
            o = jnp.dot(amat.astype(BF16), v.astype(BF16), preferred_element_type=F32) + o_inter
            opre_ref[rows, :] = o
            bl = b[c - 1:c, :]
            state[...] = st * jnp.exp(bl) + lax.dot_general(
                v.astype(BF16), (kk * jnp.exp(bl - b)).astype(BF16), TN, preferred_element_type=F32)
            r = lax.rsqrt(jnp.mean(o * o, axis=-1, keepdims=True) + NORM_EPS)
            gb = g_ref[rows, :]
            out_ref[rows, :] = (o * r * gnv * (gb * _sigmoid(gb))).astype(BF16)
            return carry

        lax.fori_loop(0, nc, chunk, 0)

    def col(off):
        return pl.BlockSpec((t, B_DIM), lambda h: (0, col0 + off * nh + h))

    return pl.pallas_call(
        kern, name=name, grid=(nh,),
        in_specs=[col(0), col(1), col(2), col(3),
                  pl.BlockSpec((None, 1, B_DIM), lambda h: (h, 0, 0)),
                  pl.BlockSpec((1, B_DIM), lambda h: (0, 0))],
        out_specs=[pl.BlockSpec((t, B_DIM), lambda h: (0, h)),
                   pl.BlockSpec((t, B_DIM), lambda h: (0, h)),
                   pl.BlockSpec((None, nc, B_DIM, B_DIM), lambda h: (h, 0, 0, 0)),
                   pl.BlockSpec((None, nc, c, c), lambda h: (h, 0, 0, 0))],
        out_shape=[_sds((t, nh * B_DIM), BF16), _sds((t, nh * B_DIM), F32),
                   _sds((nh, nc, B_DIM, B_DIM), F32), _sds((nh, nc, c, c), F32)],
        scratch_shapes=[pltpu.VMEM((B_DIM, B_DIM), F32)],
        compiler_params=_params("parallel"),
    )(proj, proj, proj, proj, lb, gn)


def hgrn_bwd(proj, col0, nh, lb, gn, opre, states, amats, dout, dcol0, name):
    t = proj.shape[0]
    c = B_CHUNK
    nc = t // c
    scale = B_DIM ** -0.5
    nsub = c // B_SUB

    def kern(q_ref, f_ref, i_ref, g_ref, lb_ref, gn_ref, opre_ref, st_ref, a_ref, dout_ref,
             dq_ref, df_ref, di_ref, dg_ref, dgn_ref, dlb_ref, dstate, dksc):
        lbv = lb_ref[...]
        gnv = gn_ref[...]
        dstate[...] = jnp.zeros_like(dstate)
        dlb_ref[...] = jnp.zeros_like(dlb_ref)

        @pl.when(pl.program_id(0) == 0)
        def _():
            dgn_ref[...] = jnp.zeros_like(dgn_ref)

        srow = lax.broadcasted_iota(jnp.int32, (c, B_DIM), 0)
        lane = lax.broadcasted_iota(jnp.int32, (B_SUB, c), 1)
        trow = lax.broadcasted_iota(jnp.int32, (B_SUB, B_DIM), 0)
        arow = lax.broadcasted_iota(jnp.int32, (c, c), 0)
        alane = lax.broadcasted_iota(jnp.int32, (c, c), 1)

        def chunk(cj, carry):
            ci = nc - 1 - cj
            rows = pl.ds(pl.multiple_of(ci * c, c), c)
            f = f_ref[rows, :]
            sig, gate = _hgrn_gates(f, lbv)
            kk = 1.0 - gate
            qb = q_ref[rows, :]
            sq = _sigmoid(qb)
            qq = qb * sq * scale
            v = i_ref[rows, :]
            b = _cumsum_rows(jnp.log(gate), c)
            st0 = st_ref[ci]
            dst = dstate[...]
            o = opre_ref[rows, :]
            gb = g_ref[rows, :]
            sg = _sigmoid(gb)
            silu_g = gb * sg
            d_out = dout_ref[rows, :]
            r = lax.rsqrt(jnp.mean(o * o, axis=-1, keepdims=True) + NORM_EPS)
            y = o * r
            dg_ref[rows, :] = (d_out * y * gnv * (sg * (1.0 + gb * (1.0 - sg)))).astype(BF16)
            dyn = d_out * silu_g
            dgn_ref[...] += jnp.sum(dyn * y, axis=0, keepdims=True)
            dy = dyn * gnv
            do = r * (dy - y * jnp.mean(dy * y, axis=-1, keepdims=True))
            eb = jnp.exp(b)
            bl = b[c - 1:c, :]
            ebl = jnp.exp(bl - b)
            ebl_last = jnp.exp(bl)
            do_b = do.astype(BF16)
            dst_b = dst.astype(BF16)
            dq_inter = jnp.dot(do_b, st0.astype(BF16), preferred_element_type=F32) * eb
            dst0 = lax.dot_general(do_b, (qq * eb).astype(BF16), TN,
                                   preferred_element_type=F32) + dst * ebl_last
            dv_inter = lax.dot_general((kk * ebl).astype(BF16), dst_b, NT, preferred_element_type=F32)
            dk_inter = jnp.dot(v.astype(BF16), dst_b, preferred_element_type=F32) * ebl
            amat = a_ref[ci]
            v_b = v.astype(BF16)
            d_a = lax.dot_general(do_b, v_b, NT, preferred_element_type=F32)
            d_a = jnp.where(arow >= alane, d_a, 0.0)
            dv_intra = lax.dot_general(amat.astype(BF16), do_b, TN, preferred_element_type=F32)
            dk_pairs = jnp.zeros((c, B_DIM), F32)
            dq_blocks = []
            for i in range(nsub):
                r0 = i * B_SUB
                qi, bi = qq[r0:r0 + B_SUB, :], b[r0:r0 + B_SUB, :]
                da_i = d_a[r0:r0 + B_SUB, :]
                if i == 0:
                    dq_i = jnp.zeros((B_SUB, B_DIM), F32)
                else:
                    eq, ek = _sub_scales(b, i)
                    da_m = jnp.where(lane < r0, da_i, 0.0)
                    dq_i = _dot3(da_m, kk * ek, NN) * eq
                    dk_pairs = dk_pairs + _dot3(da_m, qi * eq, TN) * ek
                for sl in range(B_SUB):
                    s = r0 + sl
                    e = jnp.exp(jnp.where(trow >= sl, bi - b[s:s + 1, :], -jnp.inf))
                    dacol = jnp.sum(jnp.where(lane == s, da_i, 0.0), axis=1, keepdims=True)
                    w = dacol * e
                    dq_i = dq_i + w * kk[s:s + 1, :]
                    dksc[s:s + 1, :] = jnp.sum(w * qi, axis=0, keepdims=True)
                dq_blocks.append(dq_i)
            dq = jnp.concatenate(dq_blocks, axis=0) + dq_inter
            dk = dk_pairs + dksc[...] + dk_inter
            dv = dv_intra + dv_inter
            db = qq * dq - kk * dk
            extra = (jnp.sum(kk * dk_inter, axis=0, keepdims=True)
                     + ebl_last * jnp.sum(st0 * dst, axis=0, keepdims=True))
            db = db + jnp.where(srow == c - 1, extra, 0.0)
            dlog = _cumsum_rows(db, c, reverse=True)
            dgate = dlog / gate - dk
            df_ref[rows, :] = (dgate * (1.0 - lbv) * sig * (1.0 - sig)).astype(BF16)
            dlb_ref[...] += jnp.sum(dgate * (1.0 - sig), axis=0, keepdims=True)
            dq_ref[rows, :] = (dq * scale * (sq * (1.0 + qb * (1.0 - sq)))).astype(BF16)
            di_ref[rows, :] = dv.astype(BF16)
            dstate[...] = dst0
            return carry

        lax.fori_loop(0, nc, chunk, 0)

    def col(off):
        return pl.BlockSpec((t, B_DIM), lambda h: (0, col0 + off * nh + h))

    hcol = pl.BlockSpec((t, B_DIM), lambda h: (0, h))
    vec = pl.BlockSpec((None, 1, B_DIM), lambda h: (h, 0, 0))
    wide = _sds((t, nh * B_DIM), BF16)
    return pl.pallas_call(
        kern, name=name, grid=(nh,),
        in_specs=[col(0), col(1), col(2), col(3), vec,
                  pl.BlockSpec((1, B_DIM), lambda h: (0, 0)), hcol,
                  pl.BlockSpec((None, nc, B_DIM, B_DIM), lambda h: (h, 0, 0, 0)),
                  pl.BlockSpec((None, nc, c, c), lambda h: (h, 0, 0, 0)),
                  pl.BlockSpec((t, B_DIM), lambda h: (0, dcol0 + h))],
        out_specs=[hcol, hcol, hcol, hcol, pl.BlockSpec((1, B_DIM), lambda h: (0, 0)), vec],
        out_shape=[wide, wide, wide, wide, _sds((1, B_DIM), F32), _sds((nh, 1, B_DIM), F32)],
        scratch_shapes=[pltpu.VMEM((B_DIM, B_DIM), F32), pltpu.VMEM((c, B_DIM), F32)],
        compiler_params=_params("arbitrary"),
    )(proj, proj, proj, proj, lb, gn, opre, states, amats, dout)


def lower_bounds_fwd(raw, name):
    n, w = raw.shape

    def kern(raw_ref, lb_ref, soft_ref):
        r = raw_ref[...]
        mx = r[0:1]
        for i in range(1, n):
            mx = jnp.maximum(mx, r[i:i + 1])
        e = jnp.exp(r - mx)
        den = e[0:1]
        for i in range(1, n):
            den = den + e[i:i + 1]
        soft = e / den
        soft_ref[...] = soft
        run = soft[0:1]
        lb_ref[0:1, :] = run - soft[0:1]
        for i in range(1, n):
            run = run + soft[i:i + 1]
            lb_ref[i:i + 1, :] = run - soft[0:1]

    return pl.pallas_call(kern, name=name, out_shape=[_sds((n, w), F32), _sds((n, w), F32)])(raw)


def lower_bounds_bwd(soft, dlb, name):
    n, w = soft.shape

    def kern(soft_ref, dlb_ref, out_ref):
        s = soft_ref[...]
        d = dlb_ref[...]
        total = d[0:1]
        for i in range(1, n):
            total = total + d[i:i + 1]
        us = []
        tail = total
        for i in range(n):
            us.append(tail - total if i == 0 else tail)
            tail = tail - d[i:i + 1]
        dot = s[0:1] * us[0]
        for i in range(1, n):
            dot = dot + s[i:i + 1] * us[i]
        for i in range(n):
            out_ref[i:i + 1, :] = s[i:i + 1] * (us[i] - dot)

    return pl.pallas_call(kern, name=name, out_shape=_sds((n, w), F32))(soft, dlb)


def _row_tile(kdim, n):
    tk = 512
    while tk > 8 and tk * n > 256 * 1024:
        tk //= 2
    return min(kdim, tk)


def _adam_update(w, g, m, v):
    m2 = ADAM_B1 * m + (1.0 - ADAM_B1) * g
    v2 = ADAM_B2 * v + (1.0 - ADAM_B2) * (g * g)
    m_hat = m2 / (1.0 - ADAM_B1 ** ADAM_STEP)
    v_hat = v2 / (1.0 - ADAM_B2 ** ADAM_STEP)
    delta = -ADAM_LR * (m_hat / (jnp.sqrt(v_hat) + ADAM_EPS) + ADAM_WD * w)
    return delta, m2, v2


def adamw_small(w, g, m, v, name):
    def kern(w_ref, g_ref, m_ref, v_ref, d_ref, m2_ref, v2_ref):
        d, m2, v2 = _adam_update(w_ref[...], g_ref[...], m_ref[...], v_ref[...])
        d_ref[...] = d
        m2_ref[...] = m2
        v2_ref[...] = v2

    return pl.pallas_call(kern, name=name, out_shape=[_sds(w.shape, F32)] * 3)(w, g, m, v)


def adamw_big(parts, w, m, v, name):
    nl, kdim, n = w.shape
    tk = _row_tile(kdim, n)

    def kern(p_ref, w_ref, m_ref, v_ref, g_ref, d_ref, m2_ref, v2_ref):
        g = p_ref[0].astype(F32)
        for q in range(1, 4):
            g = g + p_ref[q].astype(F32)
        d, m2, v2 = _adam_update(w_ref[...], g, m_ref[...], v_ref[...])
        g_ref[...] = g
        d_ref[...] = d
        m2_ref[...] = m2
        v2_ref[...] = v2

    blk = pl.BlockSpec((None, tk, n), lambda l, i: (l, i, 0))
    return pl.pallas_call(
        kern, name=name, grid=(nl, kdim // tk),
        in_specs=[pl.BlockSpec((None, 4, tk, n), lambda l, i: (l, 0, i, 0)), blk, blk, blk],
        out_specs=[blk] * 4, out_shape=[_sds(w.shape, F32)] * 4,
        compiler_params=_params("parallel", "parallel"),
    )(parts, w, m, v)


def cast_bf16(w, name):
    nl, kdim, n = w.shape
    tk = _row_tile(kdim, n)

    def kern(w_ref, o_ref):
        o_ref[...] = w_ref[...].astype(BF16)

    blk = pl.BlockSpec((None, tk, n), lambda l, i: (l, i, 0))
    return pl.pallas_call(
        kern, name=name, grid=(nl, kdim // tk), in_specs=[blk], out_specs=blk,
        out_shape=_sds(w.shape, BF16), compiler_params=_params("parallel", "parallel"),
    )(w)


def pair_add(dw, r1, core, name):
    kdim, n = dw.shape[1], dw.shape[2]
    tk = _row_tile(kdim, n)

    def kern(c_ref, a_ref, b_ref, o_ref):
        o_ref[...] = (a_ref[...].astype(F32) + b_ref[...].astype(F32)).astype(BF16)

    grid_spec = pltpu.PrefetchScalarGridSpec(
        num_scalar_prefetch=1, grid=(4, kdim // tk),
        in_specs=[pl.BlockSpec((None, tk, n), lambda p, i, c: (2 * p + c[0], i, 0)),
                  pl.BlockSpec((None, tk, n), lambda p, i, c: (p, i, 0))],
        out_specs=pl.BlockSpec((None, tk, n), lambda p, i, c: (p, i, 0)))
    return pl.pallas_call(
        kern, name=name, grid_spec=grid_spec, out_shape=_sds((4, kdim, n), BF16),
        compiler_params=_params("parallel", "parallel"),
    )(core, dw, r1)


ANY = pl.BlockSpec(memory_space=pl.ANY)


def _place():
    x, y, c = lax.axis_index("x"), lax.axis_index("y"), lax.axis_index("c")
    chips = [(1 - x, y), (x, 1 - y), (1 - x, 1 - y)]
    return x, y, c, chips


def all_gather(shards, name):
    n = len(shards)

    def kern(*refs):
        ins, outs = refs[:n], refs[n:2 * n]
        send_sems, recv_sems, local_sems = refs[2 * n:]
        x, y, c, chips = _place()
        me, sib = (x, y, c), (x, y, 1 - c)

        def copy(t, k, block, to, src=None):
            px, py, pc = block
            dst = outs[t].at[4 * px + 2 * py + pc]
            return pltpu.make_async_remote_copy(
                src_ref=dst if src is None else src, dst_ref=dst,
                send_sem=send_sems.at[7 * t + k], recv_sem=recv_sems.at[7 * t + k],
                device_id=to, device_id_type=MESH)

        mine = [pltpu.make_async_copy(ins[t], outs[t].at[4 * x + 2 * y + c], local_sems.at[t])
                for t in range(n)]
        for cp in mine:
            cp.start()
        first = []
        for t in range(n):
            first.append(copy(t, 0, me, sib, src=ins[t]))
            first += [copy(t, 1 + j, me, (*chip, c), src=ins[t]) for j, chip in enumerate(chips)]
        for cp in first:
            cp.start()
        passed = []
        for t in range(n):
            for j, chip in enumerate(chips):
                copy(t, 1 + j, (*chip, c), me).wait_recv()
                fwd = copy(t, 4 + j, (*chip, c), sib)
                fwd.start()
                passed.append(fwd)
        for t in range(n):
            copy(t, 0, sib, me).wait_recv()
            for j, chip in enumerate(chips):
                copy(t, 4 + j, (*chip, 1 - c), me).wait_recv()
        for cp in first + passed:
            cp.wait_send()
        for cp in mine:
            cp.wait()

    return pl.pallas_call(
        kern, name=name, in_specs=[ANY] * n, out_specs=[ANY] * n,
        out_shape=[_sds((N_DEV,) + s.shape, s.dtype) for s in shards],
        scratch_shapes=[pltpu.SemaphoreType.DMA((7 * n,)), pltpu.SemaphoreType.DMA((7 * n,)),
                        pltpu.SemaphoreType.DMA((n,))],
    )(*shards)


HBM = pl.BlockSpec(memory_space=pltpu.HBM)
SEM = pl.BlockSpec(memory_space=pltpu.SEMAPHORE)
DATAFLOW = pltpu.SideEffectType.DATAFLOW_SIDE_EFFECTING


def _first_level_targets():
    x, y, c, chips = _place()
    return 4 * x + 2 * y + c, [(x, y, 1 - c)] + [(*chip, c) for chip in chips]


def gather_start(shards, after, name):
    n = len(shards)
    lands = [lax.empty((N_DEV,) + s.shape, s.dtype) for s in shards]

    def kern(*refs):
        ins, lnd = refs[:n], refs[n:2 * n]
        send_sems, recv_sems, local_sems = refs[2 * n + len(after):2 * n + len(after) + 3]
        token = refs[-1]
        me, targets = _first_level_targets()
        for t in range(n):
            pltpu.make_async_copy(ins[t], lnd[t].at[me], local_sems.at[t]).start()
            for k, to in enumerate(targets):
                pltpu.make_async_remote_copy(
                    src_ref=ins[t], dst_ref=lnd[t].at[me], send_sem=send_sems.at[4 * t + k],
                    recv_sem=recv_sems.at[4 * t + k], device_id=to, device_id_type=MESH).start()
        token[...] = jnp.zeros_like(token)

    args = [pltpu.with_memory_space_constraint(a, pltpu.HBM) for a in list(shards) + lands]
    return pl.pallas_call(
        kern, name=name,
        out_shape=(pltpu.SemaphoreType.DMA((4 * n,)), pltpu.SemaphoreType.DMA((4 * n,)),
                   pltpu.SemaphoreType.DMA((n,)),
                   *[pltpu.HBM(a.shape, a.dtype) for a in args], _sds((8, LANES), F32)),
        in_specs=[HBM] * (2 * n) + [ANY] * len(after),
        out_specs=(SEM, SEM, SEM, *[HBM] * (2 * n), pl.BlockSpec(memory_space=pltpu.VMEM)),
        input_output_aliases={i: 3 + i for i in range(2 * n)},
        compiler_params=pltpu.CompilerParams(has_side_effects=DATAFLOW),
    )(*args, *after)


def gather_wait(send_sems, recv_sems, local_sems, shards, lands, after, name):
    n = len(shards)

    def kern(*refs):
        ins, lnd = refs[:n], refs[n:2 * n]
        send_sems, recv_sems, local_sems = refs[2 * n:2 * n + 3]
        me, targets = _first_level_targets()
        for t in range(n):
            pltpu.make_async_copy(ins[t], lnd[t].at[me], local_sems.at[t]).wait()
            for k, to in enumerate(targets):
                cp = pltpu.make_async_remote_copy(
                    src_ref=ins[t], dst_ref=lnd[t].at[me], send_sem=send_sems.at[4 * t + k],
                    recv_sem=recv_sems.at[4 * t + k], device_id=to, device_id_type=MESH)
                cp.wait_send()
                cp.wait_recv()

    bufs = list(shards) + list(lands)
    return pl.pallas_call(
        kern, name=name, out_shape=tuple(pltpu.HBM(a.shape, a.dtype) for a in bufs),
        in_specs=[HBM] * (2 * n) + [SEM, SEM, SEM, ANY], out_specs=[HBM] * (2 * n),
        input_output_aliases={i: i for i in range(2 * n)},
        compiler_params=pltpu.CompilerParams(has_side_effects=DATAFLOW),
    )(*bufs, send_sems, recv_sems, local_sems, after)


def _forward_copies(lnd, send_sems, recv_sems):
    x, y, c, chips = _place()
    passed = []
    for t in range(len(lnd)):
        for j, (qx, qy) in enumerate(chips):
            block = lnd[t].at[4 * qx + 2 * qy + c]
            passed.append(pltpu.make_async_remote_copy(
                src_ref=block, dst_ref=block, send_sem=send_sems.at[3 * t + j],
                recv_sem=recv_sems.at[3 * t + j], device_id=(x, y, 1 - c), device_id_type=MESH))
    return passed


def forward_now(lands, name):
    n = len(lands)

    def kern(*refs):
        copies = _forward_copies(refs[n:2 * n], refs[2 * n], refs[2 * n + 1])
        for cp in copies:
            cp.start()
        for cp in copies:
            cp.wait_recv()
        for cp in copies:
            cp.wait_send()

    return pl.pallas_call(
        kern, name=name, in_specs=[ANY] * n, out_specs=[ANY] * n,
        out_shape=[_sds(a.shape, a.dtype) for a in lands],
        input_output_aliases={i: i for i in range(n)},
        scratch_shapes=[pltpu.SemaphoreType.DMA((3 * n,)), pltpu.SemaphoreType.DMA((3 * n,))],
    )(*lands)


def sibling_start(grads, name):
    n = len(grads)
    lands = [lax.empty((4,) + g.shape[1:], g.dtype) for g in grads]

    def kern(*refs):
        ins, lnd = refs[:n], refs[n:2 * n]
        send_sems, recv_sems = refs[2 * n], refs[2 * n + 1]
        x, y, c, _ = _place()
        for t in range(n):
            for p in range(4):
                pltpu.make_async_remote_copy(
                    src_ref=ins[t].at[2 * p + 1 - c], dst_ref=lnd[t].at[p],
                    send_sem=send_sems.at[4 * t + p], recv_sem=recv_sems.at[4 * t + p],
                    device_id=(x, y, 1 - c), device_id_type=MESH).start()
        refs[-1][...] = jnp.zeros_like(refs[-1])

    args = [pltpu.with_memory_space_constraint(a, pltpu.HBM) for a in list(grads) + lands]
    return pl.pallas_call(
        kern, name=name,
        out_shape=(pltpu.SemaphoreType.DMA((4 * n,)), pltpu.SemaphoreType.DMA((4 * n,)),
                   *[pltpu.HBM(a.shape, a.dtype) for a in args], _sds((8, LANES), F32)),
        in_specs=[HBM] * (2 * n),
        out_specs=(SEM, SEM, *[HBM] * (2 * n), pl.BlockSpec(memory_space=pltpu.VMEM)),
        input_output_aliases={i: 2 + i for i in range(2 * n)},
        compiler_params=pltpu.CompilerParams(has_side_effects=DATAFLOW),
    )(*args)


def sibling_wait(send_sems, recv_sems, grads, lands, after, name):
    n = len(grads)

    def kern(*refs):
        ins, lnd = refs[:n], refs[n:2 * n]
        send_sems, recv_sems = refs[2 * n], refs[2 * n + 1]
        x, y, c, _ = _place()
        for t in range(n):
            for p in range(4):
                cp = pltpu.make_async_remote_copy(
                    src_ref=ins[t].at[2 * p + 1 - c], dst_ref=lnd[t].at[p],
                    send_sem=send_sems.at[4 * t + p], recv_sem=recv_sems.at[4 * t + p],
                    device_id=(x, y, 1 - c), device_id_type=MESH)
                cp.wait_send()
                cp.wait_recv()

    bufs = list(grads) + list(lands)
    outs = pl.pallas_call(
        kern, name=name, out_shape=tuple(pltpu.HBM(a.shape, a.dtype) for a in bufs),
        in_specs=[HBM] * (2 * n) + [SEM, SEM, ANY], out_specs=[HBM] * (2 * n),
        input_output_aliases={i: i for i in range(2 * n)},
        compiler_params=pltpu.CompilerParams(has_side_effects=DATAFLOW),
    )(*bufs, send_sems, recv_sems, after)
    return outs[:n], outs[n:]


def forward_start(lands, name):
    n = len(lands)

    def kern(*refs):
        for cp in _forward_copies(refs[:n], refs[n], refs[n + 1]):
            cp.start()
        refs[-1][...] = jnp.zeros_like(refs[-1])

    return pl.pallas_call(
        kern, name=name,
        out_shape=(pltpu.SemaphoreType.DMA((3 * n,)), pltpu.SemaphoreType.DMA((3 * n,)),
                   *[pltpu.HBM(a.shape, a.dtype) for a in lands], _sds((8, LANES), F32)),
        in_specs=[HBM] * n,
        out_specs=(SEM, SEM, *[HBM] * n, pl.BlockSpec(memory_space=pltpu.VMEM)),
        input_output_aliases={i: 2 + i for i in range(n)},
        compiler_params=pltpu.CompilerParams(has_side_effects=DATAFLOW),
    )(*lands)


def forward_wait(send_sems, recv_sems, lands, after, name):
    n = len(lands)

    def kern(*refs):
        for cp in _forward_copies(refs[:n], refs[n], refs[n + 1]):
            cp.wait_send()
            cp.wait_recv()

    return pl.pallas_call(
        kern, name=name, out_shape=tuple(pltpu.HBM(a.shape, a.dtype) for a in lands),
        in_specs=[HBM] * n + [SEM, SEM, ANY], out_specs=[HBM] * n,
        input_output_aliases={i: i for i in range(n)},
        compiler_params=pltpu.CompilerParams(has_side_effects=DATAFLOW),
    )(*lands, send_sems, recv_sems, after)


def all_reduce_small(vec, name):
    r = vec.shape[0]

    def kern(v_ref, o_ref, buf, send_sems, recv_sems):
        x, y, c, _ = _place()
        me = 4 * x + 2 * y + c
        peers = [(x, y, 1 - c), (1 - x, y, c), (x, 1 - y, c), (1 - x, 1 - y, c),
                 (1 - x, y, 1 - c), (x, 1 - y, 1 - c), (1 - x, 1 - y, 1 - c)]
        buf[me] = v_ref[...]
        copies = []
        for k, peer in enumerate(peers):
            cp = pltpu.make_async_remote_copy(
                src_ref=v_ref, dst_ref=buf.at[me], send_sem=send_sems.at[k],
                recv_sem=recv_sems.at[k], device_id=peer, device_id_type=MESH)
            cp.start()
            copies.append(cp)
        for cp in copies:
            cp.wait_recv()
        for cp in copies:
            cp.wait_send()
        total = buf[0]
        for d in range(1, N_DEV):
            total = total + buf[d]
        o_ref[...] = total

    vm = pl.BlockSpec(memory_space=pltpu.VMEM)
    return pl.pallas_call(
        kern, name=name, in_specs=[vm], out_specs=vm, out_shape=_sds(vec.shape, F32),
        scratch_shapes=[pltpu.VMEM((N_DEV, r, LANES), F32), pltpu.SemaphoreType.DMA((7,)),
                        pltpu.SemaphoreType.DMA((7,))],
    )(vec)


def exchange_with_sibling(grads, name):
    n = len(grads)

    def kern(*refs):
        ins, outs = refs[:n], refs[n:2 * n]
        send_sems, recv_sems = refs[2 * n:]
        x, y, c, _ = _place()
        copies = []
        for t in range(n):
            for p in range(4):
                cp = pltpu.make_async_remote_copy(
                    src_ref=ins[t].at[2 * p + 1 - c], dst_ref=outs[t].at[p],
                    send_sem=send_sems.at[4 * t + p], recv_sem=recv_sems.at[4 * t + p],
                    device_id=(x, y, 1 - c), device_id_type=MESH)
                cp.start()
                copies.append(cp)
        for cp in copies:
            cp.wait_recv()
        for cp in copies:
            cp.wait_send()

    return pl.pallas_call(
        kern, name=name, in_specs=[ANY] * n, out_specs=[ANY] * n,
        out_shape=[_sds((4,) + g.shape[1:], g.dtype) for g in grads],
        scratch_shapes=[pltpu.SemaphoreType.DMA((4 * n,)), pltpu.SemaphoreType.DMA((4 * n,))],
    )(*grads)


def exchange_between_chips(partials, layers, kinds, name):
    n = len(partials)
    n_kind = max(kinds) + 1
    shapes = []
    for kd in range(n_kind):
        idx = [i for i in range(n) if kinds[i] == kd]
        nl = max(layers[i] for i in idx) + 1
        shapes.append(_sds((nl,) + partials[idx[0]].shape, partials[idx[0]].dtype))

    def kern(*refs):
        ins, outs = refs[:n], refs[n:n + n_kind]
        send_sems, recv_sems, local_sems = refs[n + n_kind:]
        x, y, c, chips = _place()
        mine = 2 * x + y
        local = []
        copies = []
        for t in range(n):
            dst = outs[kinds[t]].at[layers[t], mine]
            lc = pltpu.make_async_copy(ins[t].at[mine], dst, local_sems.at[t])
            lc.start()
            local.append(lc)
            for j, (qx, qy) in enumerate(chips):
                cp = pltpu.make_async_remote_copy(
                    src_ref=ins[t].at[2 * qx + qy], dst_ref=dst,
                    send_sem=send_sems.at[3 * t + j], recv_sem=recv_sems.at[3 * t + j],
                    device_id=(qx, qy, c), device_id_type=MESH)
                cp.start()
                copies.append(cp)
        for cp in copies:
            cp.wait_recv()
        for cp in copies:
            cp.wait_send()
        for lc in local:
            lc.wait()

    return pl.pallas_call(
        kern, name=name, in_specs=[ANY] * n, out_specs=[ANY] * n_kind, out_shape=shapes,
        scratch_shapes=[pltpu.SemaphoreType.DMA((3 * n,)), pltpu.SemaphoreType.DMA((3 * n,)),
                        pltpu.SemaphoreType.DMA((n,))],
    )(*partials)


def scatter_start(partials, name):
    n = len(partials)
    lands = [lax.empty(p.shape, p.dtype) for p in partials]

    def kern(*refs):
        ins, lnd = refs[:n], refs[n:2 * n]
        send_sems, recv_sems, local_sems = refs[2 * n:2 * n + 3]
        token = refs[-1]
        x, y, c, chips = _place()
        mine = 2 * x + y
        for t in range(n):
            pltpu.make_async_copy(ins[t].at[mine], lnd[t].at[mine], local_sems.at[t]).start()
            for j, (qx, qy) in enumerate(chips):
                pltpu.make_async_remote_copy(
                    src_ref=ins[t].at[2 * qx + qy], dst_ref=lnd[t].at[mine],
                    send_sem=send_sems.at[3 * t + j], recv_sem=recv_sems.at[3 * t + j],
                    device_id=(qx, qy, c), device_id_type=MESH).start()
        token[...] = jnp.zeros_like(token)

    args = [pltpu.with_memory_space_constraint(a, pltpu.HBM) for a in list(partials) + lands]
    return pl.pallas_call(
        kern, name=name,
        out_shape=(pltpu.SemaphoreType.DMA((3 * n,)), pltpu.SemaphoreType.DMA((3 * n,)),
                   pltpu.SemaphoreType.DMA((n,)),
                   *[pltpu.HBM(a.shape, a.dtype) for a in args], _sds((8, LANES), F32)),
        in_specs=[HBM] * (2 * n),
        out_specs=(SEM, SEM, SEM, *[HBM] * (2 * n), pl.BlockSpec(memory_space=pltpu.VMEM)),
        input_output_aliases={i: 3 + i for i in range(2 * n)},
        compiler_params=pltpu.CompilerParams(has_side_effects=DATAFLOW),
    )(*args)


def scatter_wait(send_sems, recv_sems, local_sems, partials, lands, after, name):
    n = len(partials)

    def kern(*refs):
        ins, lnd = refs[:n], refs[n:2 * n]
        send_sems, recv_sems, local_sems = refs[2 * n:2 * n + 3]
        x, y, c, chips = _place()
        mine = 2 * x + y
        for t in range(n):
            pltpu.make_async_copy(ins[t].at[mine], lnd[t].at[mine], local_sems.at[t]).wait()
            for j, (qx, qy) in enumerate(chips):
                cp = pltpu.make_async_remote_copy(
                    src_ref=ins[t].at[2 * qx + qy], dst_ref=lnd[t].at[mine],
                    send_sem=send_sems.at[3 * t + j], recv_sem=recv_sems.at[3 * t + j],
                    device_id=(qx, qy, c), device_id_type=MESH)
                cp.wait_send()
                cp.wait_recv()

    bufs = list(partials) + list(lands)
    outs = pl.pallas_call(
        kern, name=name, out_shape=tuple(pltpu.HBM(a.shape, a.dtype) for a in bufs),
        in_specs=[HBM] * (2 * n) + [SEM, SEM, SEM, ANY], out_specs=[HBM] * (2 * n),
        input_output_aliases={i: i for i in range(2 * n)},
        compiler_params=pltpu.CompilerParams(has_side_effects=DATAFLOW),
    )(*bufs, send_sems, recv_sems, local_sems, after)
    return outs[n:]


def adamw_layers(parts, w, m, v, name):
    nl, kdim, n = w.shape
    tk = _row_tile(kdim, n)

    def kern(*refs):
        p_refs = refs[:nl]
        w_ref, m_ref, v_ref, g_ref, d_ref, m2_ref, v2_ref = refs[nl:]
        for l in range(nl):
            @pl.when(pl.program_id(0) == l)
            def _():
                g = p_refs[l][0].astype(F32)
                for q in range(1, 4):
                    g = g + p_refs[l][q].astype(F32)
                d, m2, v2 = _adam_update(w_ref[...], g, m_ref[...], v_ref[...])
                g_ref[...] = g
                d_ref[...] = d
                m2_ref[...] = m2
                v2_ref[...] = v2

    def part_spec(l):
        return pl.BlockSpec((4, tk, n), lambda li, i: (0, jnp.where(li == l, i, 0), 0))

    blk = pl.BlockSpec((None, tk, n), lambda li, i: (li, i, 0))
    return pl.pallas_call(
        kern, name=name, grid=(nl, kdim // tk),
        in_specs=[part_spec(l) for l in range(nl)] + [blk, blk, blk],
        out_specs=[blk] * 4, out_shape=[_sds(w.shape, F32)] * 4,
        compiler_params=_params("arbitrary", "arbitrary"),
    )(*parts, w, m, v)


def _pack(arrays):
    flat = jnp.concatenate([a.reshape(-1).astype(F32) for a in arrays])
    pad = (-flat.shape[0]) % (8 * LANES)
    return jnp.pad(flat, (0, pad)).reshape(-1, LANES)


def _unpack(packed, shapes):
    flat = packed.reshape(-1)
    out, off = [], 0
    for s in shapes:
        n = math.prod(s)
        out.append(flat[off:off + n].reshape(s))
        off += n
    return out


def _to_heads(x2d, dil, n_heads, dh):
    t = x2d.shape[0]
    return x2d.reshape(t // dil, dil, n_heads, dh).transpose(2, 1, 0, 3).reshape(n_heads, t, dh)


def _from_heads(xh, dil):
    h, t, w = xh.shape
    return xh.reshape(h, dil, t // dil, w).transpose(2, 1, 0, 3).reshape(t, h * w)


def _unperm(xh, dil):
    h, t, w = xh.shape
    return xh.reshape(h, dil, t // dil, w).transpose(0, 2, 1, 3).reshape(h, t, w)


def _perm(xh, dil):
    h, t, w = xh.shape
    return xh.reshape(h, t // dil, dil, w).transpose(0, 2, 1, 3).reshape(h, t, w)


def local_step(x, target, norm_mix_g, norm_mlp_g, final_norm_g, lbs, hgrn_norm_g, sinks,
               bq_full, bo_full, weights_get, weights_mid, grads_ready):
    t, d = x.shape
    depth = norm_mix_g.shape[0]
    na = d // 2 // A_DIM
    nbh = d // 2 // B_DIM
    nq = d // C_DIM
    nkv = nq // C_GROUP
    a_w = 3 * na * A_DIM
    c_w = (nq + 2 * nkv) * C_DIM
    tabs_a = rope_tables(t, A_DIM)
    tabs_c = rope_tables(t, C_DIM)
    saved = []
    for l in range(depth):
        s = {"x_in": x}
        (win_g, wout_g), token = weights_get(l, x)
        h = rms_fwd(x, norm_mix_g[l] + token, "norm_mix_fwd")
        s["h"] = h
        if l % 2 == 0:
            e = l // 2
            proj = mm_cols_sharded(h, win_g, 0, "even_in_proj")[0]
            qkv_r = rope_call(proj, tabs_a, a_w, 2 * na, False, "rope_a")[0]
            nums, ms, ls, hms = [], [], [], []
            for window, dil in A_BRANCHES:
                hm = _to_heads(qkv_r, dil, 3 * na, A_DIM)
                num, m, lsum = band_fwd(hm, 0, na, 2 * na, na, 1, t // dil // BLK, window // dil,
                                        f"dilated_fwd_{dil}")
                hms.append(hm)
                nums.append(_unperm(num, dil))
                ms.append(_unperm(m, dil))
                ls.append(_unperm(lsum, dil))
            oa, lse = merge_branches(nums, ms, ls, "dilated_merge")
            lb_e = lbs[e].reshape(nbh, 1, B_DIM)
            gn_e = hgrn_norm_g[e].reshape(1, B_DIM)
            ob, opre, states, amats = hgrn_fwd(proj, 3 * na, nbh, lb_e, gn_e, "hgrn_fwd")
            mixed = jnp.concatenate([_from_heads(oa, 1).astype(BF16), ob], axis=1)
            x = mm_rows_sharded(mixed, wout_g, 0, "even_out_proj", [x], ["tile"], _ep_residual)
            s.update(proj=proj, hms=hms, oa=oa, lse=lse, opre=opre, states=states, amats=amats,
                     mixed=mixed, lb=lb_e, gn=gn_e)
        else:
            o = l // 2
            wq = win_g[:, 0].transpose(1, 0, 2).reshape(d, c_w)
            proj = mm_plain(h, wq, "odd_qkv_proj", [bq_full[o].reshape(1, c_w)], ["row"], _ep_bias)
            qkv_r = rope_call(proj, tabs_c, c_w, (nq + nkv) * C_DIM // LANES, False, "rope_c")[0]
            hm = _to_heads(qkv_r, 1, nq + 2 * nkv, C_DIM)
            sink_rows = jnp.repeat(sinks[o].reshape(nkv, C_GROUP), BLK, axis=1).reshape(
                nkv, C_GROUP * BLK, 1)
            o_hm, lse = band_fwd(hm, 0, nq, nq + nkv, nkv, C_GROUP, t // BLK, C_WINDOW - 1,
                                 "swa_fwd", sink_rows=sink_rows, normalise=True)
            attn = _from_heads(o_hm, 1).astype(BF16)
            x = mm_rows_sharded(attn, wout_g, 0, "odd_out_proj", [bo_full[o].reshape(1, d), x],
                                ["row", "tile"], _ep_bias_residual)
            s.update(wq=wq, hm=hm, sink_rows=sink_rows, o_hm=o_hm, lse=lse, attn=attn)
        s["x_mid"] = x
        (w1_g, w2_g), token = weights_mid(l, x)
        s.update(win=win_g, wout=wout_g, w1=w1_g, w2=w2_g)
        h2 = rms_fwd(x, norm_mlp_g[l] + token, "norm_mlp_fwd")
        u, act = mm_cols_sharded(h2, w1_g, 0, "mlp_up", epilogue=_ep_relu2, n_out=2)
        x = mm_rows_sharded(act, w2_g, 0, "mlp_down", [x], ["tile"], _ep_residual)
        s.update(h2=h2, u=u, act=act)
        saved.append(s)

    dx, dxb, dg_final, loss_part = loss_head(x, final_norm_g, target, "loss_head")
    big = []
    small = {"final": dg_final, "loss": loss_part, "mix": [None] * depth, "mlp": [None] * depth,
             "lb": {}, "gn": {}, "sinks": {}, "bq": {}, "bo": {}}
    for l in reversed(range(depth)):
        s = saved[l]
        win_g, wout_g, w1_g, w2_g = s["win"], s["wout"], s["w1"], s["w2"]
        big.append(("w2", l, mm_tn(s["act"], dxb, "mlp_down_dw").reshape(N_DEV, -1, d)))
        du = mm_nt_rows_sharded(dxb, w2_g, 0, "mlp_down_dx", extras=[s["u"]],
                                epilogue=_ep_relu2_bwd, out_dtype=BF16)
        big.append(("w1", l, mm_tn(s["h2"], du, "mlp_up_dw", shard_cols=w1_g.shape[-1])))
        dh2 = mm_nt_cols_sharded(du, w1_g, 0, "mlp_up_dx")
        token = grads_ready(l, big[-2:])
        dx, dxb, dg, col_dx = rms_bwd(s["x_mid"], norm_mlp_g[l] + token, dh2, dx, "norm_mlp_bwd")
        small["mlp"][l] = dg
        if l % 2 == 0:
            e = l // 2
            big.append(("wout", e, mm_tn(s["mixed"], dxb, "even_out_dw").reshape(N_DEV, -1, d)))
            dmixed = mm_nt_rows_sharded(dxb, wout_g, 0, "even_out_dx")
            do_hm = _to_heads(dmixed[:, :na * A_DIM], 1, na, A_DIM)
            delta = head_delta(s["oa"], do_hm, "dilated_delta")
            dsum = None
            for (window, dil), hm in zip(A_BRANCHES, s["hms"]):
                dq, dk, dv = band_bwd(hm, 0, na, 2 * na, _perm(do_hm, dil).astype(BF16),
                                      _perm(s["lse"], dil), _perm(delta, dil), na, 1,
                                      t // dil // BLK, window // dil, f"dilated_bwd_{dil}")
                part = _from_heads(jnp.concatenate([dq, dk, dv], axis=0), dil)
                dsum = part if dsum is None else dsum + part
            dqkv_a = rope_call(dsum, tabs_a, a_w, 2 * na, True, "rope_a_bwd")[0]
            dqb, dfb, dib, dgb, dgn, dlb = hgrn_bwd(s["proj"], 3 * na, nbh, s["lb"], s["gn"],
                                                    s["opre"], s["states"], s["amats"], dmixed, na,
                                                    "hgrn_bwd")
            small["gn"][e] = dgn
            small["lb"][e] = dlb
            dproj = jnp.concatenate([dqkv_a, dqb, dfb, dib, dgb], axis=1)
            big.append(("win", e, mm_tn(s["h"], dproj, "even_in_dw", shard_cols=win_g.shape[-1])))
            dh = mm_nt_cols_sharded(dproj, win_g, 0, "even_in_dx")
        else:
            o = l // 2
            small["bo"][o] = col_dx
            big.append(("wo", o, mm_tn(s["attn"], dxb, "odd_out_dw").reshape(N_DEV, -1, d)))
            dattn = mm_nt_rows_sharded(dxb, wout_g, 0, "odd_out_dx")
            do_hm = _to_heads(dattn, 1, nq, C_DIM)
            dq, dk, dv, dsink = band_bwd(s["hm"], 0, nq, nq + nkv, do_hm, s["lse"], s["o_hm"],
                                         nkv, C_GROUP, t // BLK, C_WINDOW - 1, "swa_bwd",
                                         sink_rows=s["sink_rows"], delta_from_o=True)
            small["sinks"][o] = dsink
            dqkv = _from_heads(jnp.concatenate([dq, dk, dv], axis=0), 1)
            dproj, dbq = rope_call(dqkv, tabs_c, c_w, (nq + nkv) * C_DIM // LANES, True,
                                   "rope_c_bwd", col_sum=True)
            small["bq"][o] = dbq
            dwq = mm_tn(s["h"], dproj, "odd_qkv_dw", tn=512)
            big.append(("wqkv", o, dwq.reshape(d, N_DEV, -1).transpose(1, 0, 2)))
            dh = mm_nt_plain(dproj, s["wq"], "odd_qkv_dx", tk=c_w)
        token = grads_ready(l, big[-2:])
        dx, dxb, dg, _ = rms_bwd(s["x_in"], norm_mix_g[l] + token, dh, dx, "norm_mix_bwd")
        small["mix"][l] = dg
    return dx, small


def kernel(x, norm_mix_g, norm_mlp_g, final_norm_g, even_w_in, even_w_out, hgrn_lb_raw, hgrn_norm_g, odd_w_qkv, odd_b_qkv, odd_sinks, odd_w_o, odd_b_o, mlp_w1, mlp_w2, loss_target, m_norm_mix_g, m_norm_mlp_g, m_final_norm_g, m_even_w_in, m_even_w_out, m_hgrn_lb_raw, m_hgrn_norm_g, m_odd_w_qkv, m_odd_b_qkv, m_odd_sinks, m_odd_w_o, m_odd_b_o, m_mlp_w1, m_mlp_w2, v_norm_mix_g, v_norm_mlp_g, v_final_norm_g, v_even_w_in, v_even_w_out, v_hgrn_lb_raw, v_hgrn_norm_g, v_odd_w_qkv, v_odd_b_qkv, v_odd_sinks, v_odd_w_o, v_odd_b_o, v_mlp_w1, v_mlp_w2):
    d = x.shape[2]
    depth = norm_mix_g.shape[0]
    n_even, n_odd = even_w_in.shape[0], odd_w_qkv.shape[0]
    xi, yi, ci = lax.axis_index("x"), lax.axis_index("y"), lax.axis_index("c")
    dev = 4 * xi + 2 * yi + ci
    core = ci.astype(jnp.int32).reshape(1)

    big_w = {"win": even_w_in, "wout": even_w_out, "wqkv": odd_w_qkv, "wo": odd_w_o,
             "w1": mlp_w1, "w2": mlp_w2}
    big_m = {"win": m_even_w_in, "wout": m_even_w_out, "wqkv": m_odd_w_qkv, "wo": m_odd_w_o,
             "w1": m_mlp_w1, "w2": m_mlp_w2}
    big_v = {"win": v_even_w_in, "wout": v_even_w_out, "wqkv": v_odd_w_qkv, "wo": v_odd_w_o,
             "w1": v_mlp_w1, "w2": v_mlp_w2}
    kinds = list(big_w)
    casts = {k: cast_bf16(big_w[k], f"cast_{k}") for k in kinds}

    def layer_shards(l):
        a, b = ("win", "wout") if l % 2 == 0 else ("wqkv", "wo")
        return [casts[a][l // 2], casts[b][l // 2], casts["w1"][l], casts["w2"][l]]

    bq_w, bo_w = odd_b_qkv.shape[1], odd_b_o.shape[1]
    bq_mine = lax.dynamic_update_slice(jnp.zeros((n_odd, N_DEV * bq_w), F32), odd_b_qkv,
                                       (0, dev * bq_w))
    bo_mine = lax.dynamic_update_slice(jnp.zeros((n_odd, N_DEV * bo_w), F32), odd_b_o,
                                       (0, dev * bo_w))
    biases = all_reduce_small(_pack([bq_mine, bo_mine]), "gather_biases")
    bq_full, bo_full = _unpack(biases, [bq_mine.shape, bo_mine.shape])

    first_level = {}
    second_level = {}
    ready = {}
    zero = jnp.zeros((), F32)

    def start_first_level(key, shards, after):
        started = gather_start(shards, after, f"gather_start_{key}")
        first_level[key] = started[:-1]
        return started[-1]

    def finish_first_level(key, after):
        send_sems, recv_sems, local_sems, *bufs = first_level.pop(key)
        n = len(bufs) // 2
        bufs = gather_wait(send_sems, recv_sems, local_sems, bufs[:n], bufs[n:], after,
                           f"gather_wait_{key}")
        return bufs[n:]

    def weights_get(l, after):
        if l == 0:
            shards = layer_shards(0)
            mixer = all_gather(shards[:2], "gather_layer_0_mixer")
            token = start_first_level("0_mlp", shards[2:], [mixer[0], biases])
            if depth > 1:
                token = start_first_level(1, layer_shards(1), [token])
            return [g[:, None] for g in mixer], token[0, 0]
        send_sems, recv_sems, *lands = second_level.pop(l)
        gathered = forward_wait(send_sems, recv_sems, lands, after, f"gather_forward_wait_{l}")
        ready[l] = gathered[2:]
        return [g[:, None] for g in gathered[:2]], zero

    def weights_mid(l, after):
        if l == 0:
            ready[0] = forward_now(finish_first_level("0_mlp", after), "gather_forward_0_mlp")
        token = zero
        if l + 1 < depth:
            started = forward_start(finish_first_level(l + 1, after),
                                    f"gather_forward_start_{l + 1}")
            second_level[l + 1] = started[:-1]
            token = started[-1][0, 0]
            if l + 2 < depth:
                token = token + start_first_level(l + 2, layer_shards(l + 2),
                                                  [started[-1]])[0, 0]
        return [g[:, None] for g in ready.pop(l)], token

    exchanging = []
    scattering = []

    def finish_exchange(after):
        tag, names, layer_idx, (send_sems, recv_sems, *bufs) = exchanging.pop()
        n = len(names)
        grads, received = sibling_wait(send_sems, recv_sems, bufs[:n], bufs[n:], after,
                                       f"scatter_d2d_wait_{tag}")
        partials = [pair_add(g, r, core, f"pair_add_{k}")
                    for k, g, r in zip(names, grads, received)]
        started = scatter_start(partials, f"scatter_start_{tag}")
        scattering.append((tag, names, layer_idx, started[:-1]))
        return started[-1][0, 0]

    def grads_ready(l, group):
        names = [k for k, _, _ in group]
        grads = [g for _, _, g in group]
        tag = f"{l}_{names[0]}"
        token = finish_exchange(grads[0]) if exchanging else zero
        started = sibling_start(grads, f"scatter_d2d_start_{tag}")
        exchanging.append((tag, names, [li for _, li, _ in group], started[:-1]))
        return token + started[-1][0, 0]

    lbs, soft = lower_bounds_fwd(hgrn_lb_raw, "lower_bounds")

    dx, small = local_step(x[0], loss_target[0], norm_mix_g, norm_mlp_g, final_norm_g, lbs,
                           hgrn_norm_g, odd_sinks, bq_full, bo_full, weights_get, weights_mid,
                           grads_ready)
    finish_exchange(dx)

    parts = ([small["mix"][l] for l in range(depth)] + [small["mlp"][l] for l in range(depth)]
             + [small["final"]] + [small["lb"][e] for e in range(n_even)]
             + [small["gn"][e] for e in range(n_even)] + [small["sinks"][o] for o in range(n_odd)]
             + [small["bq"][o] for o in range(n_odd)] + [small["bo"][o] for o in range(n_odd)]
             + [small["loss"]])
    shapes = ([(depth, d)] * 2 + [(d,), hgrn_lb_raw.shape, hgrn_norm_g.shape, odd_sinks.shape,
              (n_odd, N_DEV * bq_w), (n_odd, N_DEV * bo_w), (1, LANES)])
    g_mix, g_mlp, g_final, d_lbs, g_gn, g_sinks, g_bq_full, g_bo_full, loss_v = _unpack(
        all_reduce_small(_pack(parts), "reduce_small"), shapes)
    g_lb = lower_bounds_bwd(soft, d_lbs, "lower_bounds_bwd")
    g_bq = lax.dynamic_slice(g_bq_full, (0, dev * bq_w), (n_odd, bq_w))
    g_bo = lax.dynamic_slice(g_bo_full, (0, dev * bo_w), (n_odd, bo_w))
    loss = loss_v[0, 0]

    small_names = ["norm_mix_g", "norm_mlp_g", "final_norm_g", "hgrn_lb_raw", "hgrn_norm_g",
                   "odd_b_qkv", "odd_sinks", "odd_b_o"]
    small_w = [norm_mix_g, norm_mlp_g, final_norm_g, hgrn_lb_raw, hgrn_norm_g, odd_b_qkv,
               odd_sinks, odd_b_o]
    small_m = [m_norm_mix_g, m_norm_mlp_g, m_final_norm_g, m_hgrn_lb_raw, m_hgrn_norm_g,
               m_odd_b_qkv, m_odd_sinks, m_odd_b_o]
    small_v = [v_norm_mix_g, v_norm_mlp_g, v_final_norm_g, v_hgrn_lb_raw, v_hgrn_norm_g,
               v_odd_b_qkv, v_odd_sinks, v_odd_b_o]
    small_g = [g_mix, g_mlp, g_final, g_lb, g_gn, g_bq, g_sinks, g_bo]
    sshapes = [w.shape for w in small_w]
    sd, sm, sv = adamw_small(_pack(small_w), _pack(small_g), _pack(small_m), _pack(small_v),
                             "adamw_small")
    res = {}
    for name, g, dl, m2, v2 in zip(small_names, small_g, _unpack(sd, sshapes),
                                   _unpack(sm, sshapes), _unpack(sv, sshapes)):
        res[name] = (g.reshape(dl.shape), dl, m2, v2)

    landed = {k: [None] * big_w[k].shape[0] for k in kinds}
    for tag, names, layer_idx, (send_sems, recv_sems, local_sems, *bufs) in scattering:
        n = len(names)
        lands = scatter_wait(send_sems, recv_sems, local_sems, bufs[:n], bufs[n:], dx,
                             f"scatter_wait_{tag}")
        for k, li, land in zip(names, layer_idx, lands):
            landed[k][li] = land
    long_names = {"win": "even_w_in", "wout": "even_w_out", "wqkv": "odd_w_qkv", "wo": "odd_w_o",
                  "w1": "mlp_w1", "w2": "mlp_w2"}
    for k in kinds:
        res[long_names[k]] = tuple(adamw_layers(landed[k], big_w[k], big_m[k], big_v[k],
                                                f"adamw_{k}"))

    order = ["norm_mix_g", "norm_mlp_g", "final_norm_g", "even_w_in", "even_w_out", "hgrn_lb_raw",
             "hgrn_norm_g", "odd_w_qkv", "odd_b_qkv", "odd_sinks", "odd_w_o", "odd_b_o", "mlp_w1",
             "mlp_w2"]
    outs = [loss, dx[None]]
    for j in range(4):
        outs += [res[n][j] for n in order]
    return tuple(outs)
```

```python
import functools
import math

import jax
import jax.numpy as jnp
from jax import lax
from jax.experimental import pallas as pl
from jax.experimental.pallas import tpu as pltpu

F32 = jnp.float32
BF16 = jnp.bfloat16
MESH = pl.DeviceIdType.MESH

N_DEV = 8
NORM_EPS = 1e-5
ROPE_THETA = 500000.0
BLK = 128
A_DIM = 128
A_BRANCHES = ((128, 1), (512, 4), (2048, 16))
B_DIM = 128
B_CHUNK = 64
C_DIM = 64
C_GROUP = 8
C_WINDOW = 128
LANES = 128

ADAM_LR = 0.001
ADAM_B1 = 0.9
ADAM_B2 = 0.999
ADAM_EPS = 1e-08
ADAM_WD = 0.01
ADAM_STEP = 10

NN = (((1,), (0,)), ((), ()))
NT = (((1,), (1,)), ((), ()))
TN = (((0,), (0,)), ((), ()))


def _params(*sem):
    return pltpu.CompilerParams(dimension_semantics=sem)


def _sigmoid(x):
    return 1.0 / (1.0 + jnp.exp(-x))


def _rows_call(name, body, row_ins, full_ins, row_outs, acc_outs, tm):
    t = row_ins[0].shape[0]
    n_ri, n_fi, n_ro = len(row_ins), len(full_ins), len(row_outs)

    def kern(*refs):
        i = pl.program_id(0)
        body(i, refs[:n_ri], refs[n_ri:n_ri + n_fi],
             refs[n_ri + n_fi:n_ri + n_fi + n_ro], refs[n_ri + n_fi + n_ro:])

    def row_spec(shape):
        return pl.BlockSpec((tm,) + tuple(shape[1:]), lambda i: (i,) + (0,) * (len(shape) - 1))

    def full_spec(shape):
        return pl.BlockSpec(tuple(shape), lambda i: (0,) * len(shape))

    outs = pl.pallas_call(
        kern, name=name, grid=(t // tm,),
        in_specs=[row_spec(a.shape) for a in row_ins] + [full_spec(a.shape) for a in full_ins],
        out_specs=[row_spec(s.shape) for s in row_outs] + [full_spec(s.shape) for s in acc_outs],
        out_shape=list(row_outs) + list(acc_outs),
        compiler_params=_params("arbitrary" if acc_outs else "parallel"),
    )(*row_ins, *full_ins)
    return outs


def _sds(shape, dtype):
    return jax.ShapeDtypeStruct(tuple(shape), dtype)


def rms_fwd(x, g, name):
    t, d = x.shape

    def body(i, ri, fi, ro, ao):
        xv = ri[0][...]
        r = lax.rsqrt(jnp.mean(xv * xv, axis=-1, keepdims=True) + NORM_EPS)
        ro[0][...] = (xv * r * fi[0][...]).astype(BF16)

    return _rows_call(name, body, [x], [g.reshape(1, d)], [_sds((t, d), BF16)], [], 256)[0]


def rms_bwd(x, g, dh, dx_res, name):
    t, d = x.shape

    def body(i, ri, fi, ro, ao):
        xv, dhv, res = ri[0][...], ri[1][...], ri[2][...]
        gv = fi[0][...]
        r = lax.rsqrt(jnp.mean(xv * xv, axis=-1, keepdims=True) + NORM_EPS)
        gd = gv * dhv
        dx = res + r * gd - xv * (r * r * r) * jnp.mean(xv * gd, axis=-1, keepdims=True)
        ro[0][...] = dx
        ro[1][...] = dx.astype(BF16)

        @pl.when(i == 0)
        def _():
            ao[0][...] = jnp.zeros_like(ao[0])
            ao[1][...] = jnp.zeros_like(ao[1])

        ao[0][...] += jnp.sum(dhv * xv * r, axis=0, keepdims=True)
        ao[1][...] += jnp.sum(dx, axis=0, keepdims=True)

    return _rows_call(name, body, [x, dh, dx_res], [g.reshape(1, d)],
                      [_sds((t, d), F32), _sds((t, d), BF16)],
                      [_sds((1, d), F32), _sds((1, d), F32)], 256)


def loss_head(x, g, target, name):
    t, d = x.shape

    def body(i, ri, fi, ro, ao):
        xv, tg = ri[0][...], ri[1][...]
        gv = fi[0][...]
        r = lax.rsqrt(jnp.mean(xv * xv, axis=-1, keepdims=True) + NORM_EPS)
        e = xv * r * gv - tg
        dy = e * (1.0 / d)
        gd = gv * dy
        dx = r * gd - xv * (r * r * r) * jnp.mean(xv * gd, axis=-1, keepdims=True)
        ro[0][...] = dx
        ro[1][...] = dx.astype(BF16)

        @pl.when(i == 0)
        def _():
            ao[0][...] = jnp.zeros_like(ao[0])
            ao[1][...] = jnp.zeros_like(ao[1])

        ao[0][...] += jnp.sum(dy * xv * r, axis=0, keepdims=True)
        part = 0.5 * jnp.sum(jnp.mean(e * e, axis=-1, keepdims=True), axis=0, keepdims=True)
        ao[1][...] += jnp.broadcast_to(part, (1, LANES))

    return _rows_call(name, body, [x, target], [g.reshape(1, d)],
                      [_sds((t, d), F32), _sds((t, d), BF16)],
                      [_sds((1, d), F32), _sds((1, LANES), F32)], 256)


def rope_tables(seq, head_dim):
    rot = head_dim // 4
    half = rot // 2
    inv_freq = 1.0 / (ROPE_THETA ** (jnp.arange(0, rot, 2, dtype=F32) / rot))
    ang = jnp.arange(seq, dtype=F32)[:, None] * inv_freq[None, :]
    cos, sin = jnp.cos(ang), jnp.sin(ang)
    zeros = jnp.zeros((seq, head_dim - rot), F32)
    zh = jnp.zeros((seq, half), F32)
    c = jnp.concatenate([cos, cos, jnp.ones((seq, head_dim - rot), F32)], axis=-1)
    sp = jnp.concatenate([zh, sin, zeros], axis=-1)
    sm = jnp.concatenate([-sin, zh, zeros], axis=-1)
    rep = LANES // head_dim
    return jnp.tile(c, (1, rep)), jnp.tile(sp, (1, rep)), jnp.tile(sm, (1, rep)), half


def rope_call(x, tabs, width, n_rope, inverse, name, col_sum=False):
    c, sp, sm, half = tabs
    t = x.shape[0]
    tm = 256
    n_slab = width // LANES

    def kern(x_ref, c_ref, sp_ref, sm_ref, o_ref, *acc):
        cv, spv, smv = c_ref[...], sp_ref[...], sm_ref[...]
        for j in range(n_slab):
            xs = x_ref[:, j * LANES:(j + 1) * LANES].astype(F32)
            if j < n_rope:
                if inverse:
                    ys = (xs * cv + pltpu.roll(xs * spv, LANES - half, 1)
                          + pltpu.roll(xs * smv, half, 1))
                else:
                    ys = (xs * cv + pltpu.roll(xs, half, 1) * spv
                          + pltpu.roll(xs, LANES - half, 1) * smv)
            else:
                ys = xs
            o_ref[:, j * LANES:(j + 1) * LANES] = ys.astype(BF16)
            if col_sum:
                @pl.when(pl.program_id(0) == 0)
                def _():
                    acc[0][:, j * LANES:(j + 1) * LANES] = jnp.zeros((1, LANES), F32)
                acc[0][:, j * LANES:(j + 1) * LANES] += jnp.sum(ys, axis=0, keepdims=True)

    tab_spec = pl.BlockSpec((tm, LANES), lambda i: (i, 0))
    out_shape = [_sds((t, width), BF16)]
    out_specs = [pl.BlockSpec((tm, width), lambda i: (i, 0))]
    if col_sum:
        out_shape.append(_sds((1, width), F32))
        out_specs.append(pl.BlockSpec((1, width), lambda i: (0, 0)))
    return pl.pallas_call(
        kern, name=name, grid=(t // tm,),
        in_specs=[pl.BlockSpec((tm, width), lambda i: (i, 0)), tab_spec, tab_spec, tab_spec],
        out_specs=out_specs, out_shape=out_shape,
        compiler_params=_params("arbitrary" if col_sum else "parallel"),
    )(x, c, sp, sm)


def _mm_call(name, a, b, extras, out_shapes, grid, a_spec, b_spec, extra_specs, out_specs,
             acc_shape, dims, epilogue):
    n_ex, n_out = len(extras), len(out_shapes)
    nk = grid[2]

    def product(a_ref, b_ref):
        bv = b_ref[...]
        if bv.ndim == 3:
            bv = bv.reshape(bv.shape[0] * bv.shape[1], bv.shape[2])
        return lax.dot_general(a_ref[...].astype(BF16), bv.astype(BF16), dims,
                               preferred_element_type=F32)

    def kern(*refs):
        a_ref, b_ref = refs[0], refs[1]
        ex = refs[2:2 + n_ex]
        outs = refs[2 + n_ex:2 + n_ex + n_out]
        if nk == 1:
            epilogue(product(a_ref, b_ref), ex, outs)
            return
        acc = refs[-1]
        k = pl.program_id(2)

        @pl.when(k == 0)
        def _():
            acc[...] = product(a_ref, b_ref)

        @pl.when(k > 0)
        def _():
            acc[...] += product(a_ref, b_ref)

        @pl.when(k == nk - 1)
        def _():
            epilogue(acc[...], ex, outs)

    return pl.pallas_call(
        kern, name=name, grid=grid,
        in_specs=[a_spec, b_spec, *extra_specs], out_specs=out_specs, out_shape=out_shapes,
        scratch_shapes=[pltpu.VMEM(acc_shape, F32)] if nk > 1 else [],
        compiler_params=_params("parallel", "parallel", "arbitrary"),
    )(a, b, *extras)


def _ep_store(dtype):
    def ep(acc, ex, outs):
        outs[0][...] = acc.astype(dtype)
    return ep


def _ep_residual(acc, ex, outs):
    outs[0][...] = acc + ex[0][...]


def _ep_bias(acc, ex, outs):
    outs[0][...] = acc + ex[0][...]


def _ep_bias_residual(acc, ex, outs):
    outs[0][...] = acc + ex[0][...] + ex[1][...]


def _ep_relu2(acc, ex, outs):
    outs[0][...] = acc
    rl = jnp.maximum(acc, 0.0)
    outs[1][...] = (rl * rl).astype(BF16)


def _ep_relu2_bwd(acc, ex, outs):
    outs[0][...] = (acc * (2.0 * jnp.maximum(ex[0][...], 0.0))).astype(BF16)


MM_TM = 1024
MM_TN = 1024
MM_TK = 2048


def mm_cols_sharded(a, wg, layer, name, epilogue=None, n_out=1):
    m, kdim = a.shape
    n = wg.shape[-1]
    tm, tk = min(m, MM_TM), min(kdim, MM_TK)
    if epilogue is None:
        epilogue, outs = _ep_store(F32), [_sds((m, N_DEV * n), F32)]
    else:
        outs = [_sds((m, N_DEV * n), F32), _sds((m, N_DEV * n), BF16)][:n_out]
    return _mm_call(
        name, a, wg, [], outs, (m // tm, N_DEV, kdim // tk),
        pl.BlockSpec((tm, tk), lambda i, j, k: (i, k)),
        pl.BlockSpec((None, None, tk, n), lambda i, j, k: (j, layer, k, 0)),
        [], [pl.BlockSpec((tm, n), lambda i, j, k: (i, j))] * len(outs),
        (tm, n), NN, epilogue)


def _extra_specs(extra_kinds, tm, tn):
    specs = []
    for kind in extra_kinds:
        if kind == "row":
            specs.append(pl.BlockSpec((1, tn), lambda i, j, k: (0, j)))
        else:
            specs.append(pl.BlockSpec((tm, tn), lambda i, j, k: (i, j)))
    return specs


def mm_rows_sharded(a, wg, layer, name, extras, extra_kinds, epilogue):
    m, kdim = a.shape
    ks, n = wg.shape[-2], wg.shape[-1]
    tm, tn = min(m, MM_TM), min(n, MM_TN)
    gps = max(1, min(kdim, MM_TK) // ks)
    return _mm_call(
        name, a, wg, extras, [_sds((m, n), F32)], (m // tm, n // tn, N_DEV // gps),
        pl.BlockSpec((tm, gps * ks), lambda i, j, k: (i, k)),
        pl.BlockSpec((gps, None, ks, tn), lambda i, j, k: (k, layer, 0, j)),
        _extra_specs(extra_kinds, tm, tn), [pl.BlockSpec((tm, tn), lambda i, j, k: (i, j))],
        (tm, tn), NN, epilogue)[0]


def mm_plain(a, w, name, extras, extra_kinds, epilogue, tn=512):
    m, kdim = a.shape
    n = w.shape[1]
    tm, tk = min(m, MM_TM), min(kdim, MM_TK)
    return _mm_call(
        name, a, w, extras, [_sds((m, n), F32)], (m // tm, n // tn, kdim // tk),
        pl.BlockSpec((tm, tk), lambda i, j, k: (i, k)),
        pl.BlockSpec((tk, tn), lambda i, j, k: (k, j)),
        _extra_specs(extra_kinds, tm, tn), [pl.BlockSpec((tm, tn), lambda i, j, k: (i, j))],
        (tm, tn), NN, epilogue)[0]


def mm_nt_cols_sharded(dy, wg, layer, name):
    m = dy.shape[0]
    kdim, n = wg.shape[-2], wg.shape[-1]
    tm, tn = min(m, MM_TM), min(kdim, MM_TN)
    return _mm_call(
        name, dy, wg, [], [_sds((m, kdim), F32)], (m // tm, kdim // tn, N_DEV),
        pl.BlockSpec((tm, n), lambda i, j, k: (i, k)),
        pl.BlockSpec((None, None, tn, n), lambda i, j, k: (k, layer, j, 0)),
        [], [pl.BlockSpec((tm, tn), lambda i, j, k: (i, j))],
        (tm, tn), NT, _ep_store(F32))[0]


def mm_nt_rows_sharded(dy, wg, layer, name, extras=(), epilogue=None, out_dtype=F32):
    m, n = dy.shape
    ks = wg.shape[-2]
    tm, tk = min(m, MM_TM), min(n, MM_TK)
    gps = max(1, MM_TN // ks)
    tn = gps * ks
    epilogue = _ep_store(out_dtype) if epilogue is None else epilogue
    return _mm_call(
        name, dy, wg, list(extras), [_sds((m, N_DEV * ks), out_dtype)],
        (m // tm, N_DEV // gps, n // tk),
        pl.BlockSpec((tm, tk), lambda i, j, k: (i, k)),
        pl.BlockSpec((gps, None, ks, tk), lambda i, j, k: (j, layer, 0, k)),
        [pl.BlockSpec((tm, tn), lambda i, j, k: (i, j))] * len(extras),
        [pl.BlockSpec((tm, tn), lambda i, j, k: (i, j))],
        (tm, tn), NT, epilogue)[0]


def mm_nt_plain(dy, w, name, tk):
    m, n = dy.shape
    kdim = w.shape[0]
    tm, tn = min(m, MM_TM), min(kdim, MM_TN)
    return _mm_call(
        name, dy, w, [], [_sds((m, kdim), F32)], (m // tm, kdim // tn, n // tk),
        pl.BlockSpec((tm, tk), lambda i, j, k: (i, k)),
        pl.BlockSpec((tn, tk), lambda i, j, k: (j, k)),
        [], [pl.BlockSpec((tm, tn), lambda i, j, k: (i, j))],
        (tm, tn), NT, _ep_store(F32))[0]


def mm_tn(a, dy, name, shard_cols=None, tn=MM_TN):
    t, kdim = a.shape
    n = dy.shape[1]
    tm, tk = min(kdim, MM_TM), min(t, MM_TK)
    if shard_cols is None:
        tn = min(tn, n)
        out = _sds((kdim, n), BF16)
        o_spec = pl.BlockSpec((tm, tn), lambda i, j, k: (i, j))
    else:
        tn = shard_cols
        out = _sds((n // tn, kdim, tn), BF16)
        o_spec = pl.BlockSpec((None, tm, tn), lambda i, j, k: (j, i, 0))
    return _mm_call(
        name, a, dy, [], [out], (kdim // tm, n // tn, t // tk),
        pl.BlockSpec((tk, tm), lambda i, j, k: (k, i)),
        pl.BlockSpec((tk, tn), lambda i, j, k: (k, j)),
        [], [o_spec], (tm, tn), TN, _ep_store(BF16))[0]


BAND_BLOCKS_PER_STEP = 4


def _band_mask(g, nk_prev_valid, max_dist):
    rows = lax.broadcasted_iota(jnp.int32, (g * BLK, 2 * BLK), 0) % BLK
    cols = lax.broadcasted_iota(jnp.int32, (g * BLK, 2 * BLK), 1)
    dist = rows + BLK - cols
    ok = (dist >= 0) & (dist <= max_dist)
    return ok & ((cols >= BLK) | nk_prev_valid)


def band_fwd(qkv, q0, k0, v0, hk, g, seg, max_dist, name, sink_rows=None, normalise=False):
    t, dh = qkv.shape[1], qkv.shape[2]
    nb = t // BLK
    rb = BAND_BLOCKS_PER_STEP // g if g < BAND_BLOCKS_PER_STEP else 1
    rows = rb * BLK
    scale = dh ** -0.5
    has_sink = sink_rows is not None

    def kern(*refs):
        if has_sink:
            q_ref, k_ref, v_ref, s_ref, num_ref, m_ref, *l_ref = refs
            sink = s_ref[...]
        else:
            q_ref, k_ref, v_ref, num_ref, m_ref, *l_ref = refs
        for r in range(rb):
            b = pl.program_id(1) * rb + r
            cur = pl.multiple_of(b * BLK, BLK)
            prev = pl.multiple_of(jnp.maximum(b - 1, 0) * BLK, BLK)
            here = slice(r * BLK, (r + 1) * BLK)
            q = q_ref[:, here, :].reshape(g * BLK, dh)
            kk = jnp.concatenate([k_ref[pl.ds(prev, BLK), :], k_ref[pl.ds(cur, BLK), :]], axis=0)
            vv = jnp.concatenate([v_ref[pl.ds(prev, BLK), :], v_ref[pl.ds(cur, BLK), :]], axis=0)
            s = lax.dot_general(q, kk, NT, preferred_element_type=F32) * scale
            s = jnp.where(_band_mask(g, (b % seg) != 0, max_dist), s, -jnp.inf)
            m = jnp.max(s, axis=-1, keepdims=True)
            if has_sink:
                m = jnp.maximum(m, sink)
            p = jnp.exp(s - m)
            l = jnp.sum(p, axis=-1, keepdims=True)
            if has_sink:
                l = l + jnp.exp(sink - m)
            num = jnp.dot(p.astype(BF16), vv, preferred_element_type=F32)
            if normalise:
                num_ref[:, here, :] = (num * (1.0 / l)).reshape(g, BLK, dh)
                m_ref[:, here, :] = (m + jnp.log(l)).reshape(g, BLK, 1)
            else:
                num_ref[:, here, :] = num.reshape(g, BLK, dh)
                m_ref[:, here, :] = m.reshape(g, BLK, 1)
                l_ref[0][:, here, :] = l.reshape(g, BLK, 1)

    in_specs = [pl.BlockSpec((g, rows, dh), lambda h, b: (q0 // g + h, b, 0)),
                pl.BlockSpec((None, t, dh), lambda h, b: (k0 + h, 0, 0)),
                pl.BlockSpec((None, t, dh), lambda h, b: (v0 + h, 0, 0))]
    args = [qkv, qkv, qkv]
    if has_sink:
        in_specs.append(pl.BlockSpec((None, g * BLK, 1), lambda h, b: (h, 0, 0)))
        args.append(sink_rows)
    hq = hk * g
    n_col = 1 if normalise else 2
    return pl.pallas_call(
        kern, name=name, grid=(hk, nb // rb), in_specs=in_specs,
        out_specs=[pl.BlockSpec((g, rows, dh), lambda h, b: (h, b, 0))]
        + [pl.BlockSpec((g, rows, 1), lambda h, b: (h, b, 0))] * n_col,
        out_shape=[_sds((hq, t, dh), F32)] + [_sds((hq, t, 1), F32)] * n_col,
        compiler_params=_params("parallel", "parallel"),
    )(*args)


def band_bwd(qkv, q0, k0, v0, do, lse, delta, hk, g, seg, max_dist, name, sink_rows=None,
             delta_from_o=False):
    t, dh = qkv.shape[1], qkv.shape[2]
    nb = t // BLK
    rb = BAND_BLOCKS_PER_STEP // g if g < BAND_BLOCKS_PER_STEP else 1
    scale = dh ** -0.5
    has_sink = sink_rows is not None

    def kern(*refs):
        if has_sink:
            (q_ref, k_ref, v_ref, do_ref, lse_ref, dl_ref, s_ref,
             dq_ref, dk_ref, dv_ref, ds_ref, sacc) = refs
            sink = s_ref[...]
        else:
            q_ref, k_ref, v_ref, do_ref, lse_ref, dl_ref, dq_ref, dk_ref, dv_ref = refs
        step = pl.program_id(1)

        @pl.when(step == 0)
        def _():
            dk_ref[...] = jnp.zeros_like(dk_ref)
            dv_ref[...] = jnp.zeros_like(dv_ref)
            if has_sink:
                sacc[...] = jnp.zeros_like(sacc)

        for r in range(rb):
            b = step * rb + r
            cur = pl.multiple_of(b * BLK, BLK)
            prev = pl.multiple_of(jnp.maximum(b - 1, 0) * BLK, BLK)
            here = slice(r * BLK, (r + 1) * BLK)
            q = q_ref[:, here, :].reshape(g * BLK, dh)
            dout = do_ref[:, here, :].reshape(g * BLK, dh)
            lse_b = lse_ref[:, here, :].reshape(g * BLK, 1)
            if delta_from_o:
                dl_b = jnp.sum(dl_ref[:, here, :].reshape(g * BLK, dh) * dout, axis=-1,
                               keepdims=True)
                dout = dout.astype(BF16)
            else:
                dl_b = dl_ref[:, here, :].reshape(g * BLK, 1)
            kk = jnp.concatenate([k_ref[pl.ds(prev, BLK), :], k_ref[pl.ds(cur, BLK), :]], axis=0)
            vv = jnp.concatenate([v_ref[pl.ds(prev, BLK), :], v_ref[pl.ds(cur, BLK), :]], axis=0)
            s = lax.dot_general(q, kk, NT, preferred_element_type=F32) * scale
            s = jnp.where(_band_mask(g, (b % seg) != 0, max_dist), s, -jnp.inf)
            p = jnp.exp(s - lse_b)
            dp = lax.dot_general(dout, vv, NT, preferred_element_type=F32)
            ds = (p * (dp - dl_b) * scale).astype(BF16)
            dq = jnp.dot(ds, kk, preferred_element_type=F32)
            dq_ref[:, here, :] = dq.reshape(g, BLK, dh)
            dkk = lax.dot_general(ds, q, TN, preferred_element_type=F32)
            dvv = lax.dot_general(p.astype(BF16), dout, TN, preferred_element_type=F32)
            dk_ref[pl.ds(prev, BLK), :] += dkk[:BLK]
            dk_ref[pl.ds(cur, BLK), :] += dkk[BLK:]
            dv_ref[pl.ds(prev, BLK), :] += dvv[:BLK]
            dv_ref[pl.ds(cur, BLK), :] += dvv[BLK:]
            if has_sink:
                sacc[...] += -jnp.exp(sink - lse_b) * dl_b

        if has_sink:
            @pl.when(step == nb // rb - 1)
            def _():
                for gi in range(g):
                    ds_ref[gi:gi + 1, :] = jnp.sum(sacc[gi * BLK:(gi + 1) * BLK, :], axis=0,
                                                   keepdims=True)

    rows = rb * BLK
    in_specs = [pl.BlockSpec((g, rows, dh), lambda h, b: (q0 // g + h, b, 0)),
                pl.BlockSpec((None, t, dh), lambda h, b: (k0 + h, 0, 0)),
                pl.BlockSpec((None, t, dh), lambda h, b: (v0 + h, 0, 0)),
                pl.BlockSpec((g, rows, dh), lambda h, b: (h, b, 0)),
                pl.BlockSpec((g, rows, 1), lambda h, b: (h, b, 0)),
                pl.BlockSpec((g, rows, dh if delta_from_o else 1), lambda h, b: (h, b, 0))]
    args = [qkv, qkv, qkv, do, lse, delta]
    hq = hk * g
    out_specs = [pl.BlockSpec((g, rows, dh), lambda h, b: (h, b, 0)),
                 pl.BlockSpec((None, t, dh), lambda h, b: (h, 0, 0)),
                 pl.BlockSpec((None, t, dh), lambda h, b: (h, 0, 0))]
    out_shape = [_sds((hq, t, dh), F32), _sds((hk, t, dh), F32), _sds((hk, t, dh), F32)]
    scratch = []
    if has_sink:
        in_specs.append(pl.BlockSpec((None, g * BLK, 1), lambda h, b: (h, 0, 0)))
        args.append(sink_rows)
        out_specs.append(pl.BlockSpec((None, g, 1), lambda h, b: (h, 0, 0)))
        out_shape.append(_sds((hk, g, 1), F32))
        scratch.append(pltpu.VMEM((g * BLK, 1), F32))
    return pl.pallas_call(
        kern, name=name, grid=(hk, nb // rb), in_specs=in_specs, out_specs=out_specs,
        out_shape=out_shape,
        scratch_shapes=scratch, compiler_params=_params("parallel", "arbitrary"),
    )(*args)


def merge_branches(nums, ms, ls, name):
    h, t, dh = nums[0].shape
    nbr = len(nums)

    def kern(*refs):
        num_refs, m_refs, l_refs = refs[:nbr], refs[nbr:2 * nbr], refs[2 * nbr:3 * nbr]
        o_ref, lse_ref = refs[3 * nbr], refs[3 * nbr + 1]
        mall = m_refs[0][...]
        for i in range(1, nbr):
            mall = jnp.maximum(mall, m_refs[i][...])
        num = jnp.zeros((t, dh), F32)
        den = jnp.zeros((t, 1), F32)
        for i in range(nbr):
            w = jnp.exp(m_refs[i][...] - mall)
            num = num + w * num_refs[i][...]
            den = den + w * l_refs[i][...]
        o_ref[...] = num / den
        lse_ref[...] = mall + jnp.log(den)

    big = pl.BlockSpec((None, t, dh), lambda i: (i, 0, 0))
    col = pl.BlockSpec((None, t, 1), lambda i: (i, 0, 0))
    return pl.pallas_call(
        kern, name=name, grid=(h,), in_specs=[big] * nbr + [col] * (2 * nbr),
        out_specs=[big, col], out_shape=[_sds((h, t, dh), F32), _sds((h, t, 1), F32)],
        compiler_params=_params("parallel"),
    )(*nums, *ms, *ls)


def normalise_heads(num, m, l, name):
    h, t, dh = num.shape

    def kern(num_ref, m_ref, l_ref, o_ref, lse_ref):
        lv = l_ref[...]
        o_ref[...] = num_ref[...] / lv
        lse_ref[...] = m_ref[...] + jnp.log(lv)

    big = pl.BlockSpec((None, t, dh), lambda i: (i, 0, 0))
    col = pl.BlockSpec((None, t, 1), lambda i: (i, 0, 0))
    return pl.pallas_call(
        kern, name=name, grid=(h,), in_specs=[big, col, col], out_specs=[big, col],
        out_shape=[_sds((h, t, dh), F32), _sds((h, t, 1), F32)],
        compiler_params=_params("parallel"),
    )(num, m, l)


def head_delta(o, do, name):
    h, t, dh = o.shape

    def kern(o_ref, do_ref, d_ref):
        d_ref[...] = jnp.sum(o_ref[...] * do_ref[...], axis=-1, keepdims=True)

    big = pl.BlockSpec((None, t, dh), lambda i: (i, 0, 0))
    return pl.pallas_call(
        kern, name=name, grid=(h,), in_specs=[big, big],
        out_specs=pl.BlockSpec((None, t, 1), lambda i: (i, 0, 0)),
        out_shape=_sds((h, t, 1), F32), compiler_params=_params("parallel"),
    )(o, do)


def _dil_rb(nbl):
    return min(BAND_BLOCKS_PER_STEP, nbl)


def dilated_fwd(qkv, na, dil, max_dist, name):
    t, w3 = qkv.shape
    dh = A_DIM
    seq = t // dil
    nbl = seq // BLK
    rb = _dil_rb(nbl)
    rows = rb * BLK
    cb = w3 // dh
    scale = dh ** -0.5
    view = qkv.reshape(seq, dil * w3)

    def kern(q_ref, k_ref, v_ref, num_ref, m_ref, l_ref):
        for r in range(rb):
            b = pl.program_id(1) * rb + r
            cur = pl.multiple_of(b * BLK, BLK)
            prev = pl.multiple_of(jnp.maximum(b - 1, 0) * BLK, BLK)
            here = slice(r * BLK, (r + 1) * BLK)
            kk = jnp.concatenate([k_ref[pl.ds(prev, BLK), :], k_ref[pl.ds(cur, BLK), :]], axis=0)
            vv = jnp.concatenate([v_ref[pl.ds(prev, BLK), :], v_ref[pl.ds(cur, BLK), :]], axis=0)
            s = lax.dot_general(q_ref[here, :], kk, NT, preferred_element_type=F32) * scale
            s = jnp.where(_band_mask(1, b != 0, max_dist), s, -jnp.inf)
            m = jnp.max(s, axis=-1, keepdims=True)
            p = jnp.exp(s - m)
            l = jnp.sum(p, axis=-1, keepdims=True)
            num_ref[here, :] = jnp.dot(p.astype(BF16), vv, preferred_element_type=F32)
            m_ref[here, :] = jnp.broadcast_to(m, (BLK, dh))
            l_ref[here, :] = jnp.broadcast_to(l, (BLK, dh))

    def col(off):
        return lambda p, b: (0, (p // na) * cb + off * na + p % na)

    out_spec = pl.BlockSpec((rows, dh), lambda p, b: (b, p))
    out = _sds((seq, dil * na * dh), F32)
    outs = pl.pallas_call(
        kern, name=name, grid=(dil * na, nbl // rb),
        in_specs=[pl.BlockSpec((rows, dh), lambda p, b: (b, (p // na) * cb + p % na)),
                  pl.BlockSpec((seq, dh), col(1)), pl.BlockSpec((seq, dh), col(2))],
        out_specs=[out_spec] * 3, out_shape=[out] * 3,
        compiler_params=_params("parallel", "parallel"),
    )(view, view, view)
    return [o.reshape(t, na * dh) for o in outs]


def dilated_merge(nums, ms, ls, name):
    t, w = nums[0].shape
    nbr = len(nums)

    def body(i, ri, fi, ro, ao):
        mall = ri[nbr][...]
        for j in range(1, nbr):
            mall = jnp.maximum(mall, ri[nbr + j][...])
        num = jnp.zeros(mall.shape, F32)
        den = jnp.zeros(mall.shape, F32)
        for j in range(nbr):
            wgt = jnp.exp(ri[nbr + j][...] - mall)
            num = num + wgt * ri[j][...]
            den = den + wgt * ri[2 * nbr + j][...]
        o = num / den
        ro[0][...] = o
        ro[1][...] = o.astype(BF16)
        ro[2][...] = mall + jnp.log(den)

    return _rows_call(name, body, list(nums) + list(ms) + list(ls), [],
                      [_sds((t, w), F32), _sds((t, w), BF16), _sds((t, w), F32)], [], 256)


def dilated_delta(o, dmixed, name):
    t, w = o.shape

    def body(i, ri, fi, ro, ao):
        for j in range(w // A_DIM):
            cols = slice(j * A_DIM, (j + 1) * A_DIM)
            d = jnp.sum(ri[0][:, cols] * ri[1][:, cols], axis=-1, keepdims=True)
            ro[0][:, cols] = jnp.broadcast_to(d, (d.shape[0], A_DIM))

    return _rows_call(name, body, [o, dmixed], [], [_sds((t, w), F32)], [], 256)[0]


def dilated_bwd(qkv, dmixed, lse, delta, na, dil, max_dist, name):
    t, w3 = qkv.shape
    dh = A_DIM
    seq = t // dil
    nbl = seq // BLK
    rb = _dil_rb(nbl)
    rows = rb * BLK
    cb = w3 // dh
    db = dmixed.shape[1] // dh
    scale = dh ** -0.5
    view = qkv.reshape(seq, dil * w3)
    do_view = dmixed.reshape(seq, dil * dmixed.shape[1])
    lse_view = lse.reshape(seq, dil * na * dh)
    delta_view = delta.reshape(seq, dil * na * dh)

    def kern(q_ref, k_ref, v_ref, do_ref, lse_ref, dl_ref, dq_ref, dk_ref, dv_ref):
        step = pl.program_id(1)

        @pl.when(step == 0)
        def _():
            dk_ref[...] = jnp.zeros_like(dk_ref)
            dv_ref[...] = jnp.zeros_like(dv_ref)

        for r in range(rb):
            b = step * rb + r
            cur = pl.multiple_of(b * BLK, BLK)
            prev = pl.multiple_of(jnp.maximum(b - 1, 0) * BLK, BLK)
            here = slice(r * BLK, (r + 1) * BLK)
            q = q_ref[here, :]
            dout = do_ref[here, :].astype(BF16)
            kk = jnp.concatenate([k_ref[pl.ds(prev, BLK), :], k_ref[pl.ds(cur, BLK), :]], axis=0)
            vv = jnp.concatenate([v_ref[pl.ds(prev, BLK), :], v_ref[pl.ds(cur, BLK), :]], axis=0)
            s = lax.dot_general(q, kk, NT, preferred_element_type=F32) * scale
            s = jnp.where(_band_mask(1, b != 0, max_dist), s, -jnp.inf)
            p = jnp.exp(s - lse_ref[here, 0:1])
            dp = lax.dot_general(dout, vv, NT, preferred_element_type=F32)
            ds = (p * (dp - dl_ref[here, 0:1]) * scale).astype(BF16)
            dq_ref[here, :] = jnp.dot(ds, kk, preferred_element_type=F32)
            dkk = lax.dot_general(ds, q, TN, preferred_element_type=F32)
            dvv = lax.dot_general(p.astype(BF16), dout, TN, preferred_element_type=F32)
            dk_ref[pl.ds(prev, BLK), :] += dkk[:BLK]
            dk_ref[pl.ds(cur, BLK), :] += dkk[BLK:]
            dv_ref[pl.ds(prev, BLK), :] += dvv[:BLK]
            dv_ref[pl.ds(cur, BLK), :] += dvv[BLK:]

    def col(off):
        return lambda p, b: (0, (p // na) * cb + off * na + p % na)

    blk = pl.BlockSpec((rows, dh), lambda p, b: (b, p))
    whole = pl.BlockSpec((seq, dh), lambda p, b: (0, p))
    out = _sds((seq, dil * na * dh), F32)
    outs = pl.pallas_call(
        kern, name=name, grid=(dil * na, nbl // rb),
        in_specs=[pl.BlockSpec((rows, dh), lambda p, b: (b, (p // na) * cb + p % na)),
                  pl.BlockSpec((seq, dh), col(1)), pl.BlockSpec((seq, dh), col(2)),
                  pl.BlockSpec((rows, dh), lambda p, b: (b, (p // na) * db + p % na)), blk, blk],
        out_specs=[blk, whole, whole], out_shape=[out] * 3,
        compiler_params=_params("parallel", "arbitrary"),
    )(view, view, view, do_view, lse_view, delta_view)
    return [o.reshape(t, na * dh) for o in outs]


def rope_bwd_sum(dqs, dks, dvs, tabs, name):
    c, sp, sm, half = tabs
    t, w = dqs[0].shape
    nbr = len(dqs)
    tm = 256
    n_slab = w // LANES

    def kern(*refs):
        groups = [refs[:nbr], refs[nbr:2 * nbr], refs[2 * nbr:3 * nbr]]
        c_ref, sp_ref, sm_ref, o_ref = refs[3 * nbr:]
        cv, spv, smv = c_ref[...], sp_ref[...], sm_ref[...]
        for gi, group in enumerate(groups):
            for j in range(n_slab):
                cols = slice(j * LANES, (j + 1) * LANES)
                xs = group[0][:, cols]
                for ref in group[1:]:
                    xs = xs + ref[:, cols]
                if gi < 2:
                    xs = (xs * cv + pltpu.roll(xs * spv, LANES - half, 1)
                          + pltpu.roll(xs * smv, half, 1))
                o_ref[:, gi * w + j * LANES:gi * w + (j + 1) * LANES] = xs.astype(BF16)

    big = pl.BlockSpec((tm, w), lambda i: (i, 0))
    tab = pl.BlockSpec((tm, LANES), lambda i: (i, 0))
    return pl.pallas_call(
        kern, name=name, grid=(t // tm,), in_specs=[big] * (3 * nbr) + [tab] * 3,
        out_specs=pl.BlockSpec((tm, 3 * w), lambda i: (i, 0)), out_shape=_sds((t, 3 * w), BF16),
        compiler_params=_params("parallel"),
    )(*dqs, *dks, *dvs, c, sp, sm)


def _cumsum_rows(x, n, reverse=False):
    rows = lax.broadcasted_iota(jnp.int32, x.shape, 0)
    shift = 1
    while shift < n:
        if reverse:
            x = x + jnp.where(rows < n - shift, pltpu.roll(x, n - shift, 0), 0.0)
        else:
            x = x + jnp.where(rows >= shift, pltpu.roll(x, shift, 0), 0.0)
        shift *= 2
    return x


def _hgrn_gates(f, lb):
    sig = _sigmoid(f)
    gate = lb + (1.0 - lb) * sig
    return sig, gate


B_SUB = 16


def _dot3(a, b, dims):
    ah, bh = a.astype(BF16), b.astype(BF16)
    al = (a - ah.astype(F32)).astype(BF16)
    bl = (b - bh.astype(F32)).astype(BF16)
    dot = functools.partial(lax.dot_general, dimension_numbers=dims, preferred_element_type=F32)
    return dot(ah, bh) + dot(al, bh) + dot(ah, bl)


def _sub_scales(b, i):
    r0 = i * B_SUB
    beta = b[r0 - 1:r0, :]
    return jnp.exp(b[r0:r0 + B_SUB, :] - beta), jnp.exp(jnp.minimum(beta - b, 0.0))


def _hgrn_intra_attn(qq, kk, b):
    c = qq.shape[0]
    lane = lax.broadcasted_iota(jnp.int32, (B_SUB, c), 1)
    trow = lax.broadcasted_iota(jnp.int32, (B_SUB, B_DIM), 0)
    blocks = []
    for i in range(c // B_SUB):
        r0 = i * B_SUB
        qi, bi = qq[r0:r0 + B_SUB, :], b[r0:r0 + B_SUB, :]
        if i == 0:
            a_i = jnp.zeros((B_SUB, c), F32)
        else:
            eq, ek = _sub_scales(b, i)
            a_i = jnp.where(lane < r0, _dot3(qi * eq, kk * ek, NT), 0.0)
        for sl in range(B_SUB):
            s = r0 + sl
            e = jnp.exp(jnp.where(trow >= sl, bi - b[s:s + 1, :], -jnp.inf))
            col = jnp.sum(qi * kk[s:s + 1, :] * e, axis=1, keepdims=True)
            a_i = jnp.where(lane == s, col, a_i)
        blocks.append(a_i)
    return jnp.concatenate(blocks, axis=0)


def hgrn_fwd(proj, col0, nh, lb, gn, name):
    t = proj.shape[0]
    c = B_CHUNK
    nc = t // c
    scale = B_DIM ** -0.5

    def kern(q_ref, f_ref, i_ref, g_ref, lb_ref, gn_ref, out_ref, opre_ref, st_ref, a_ref, state):
        lbv = lb_ref[...]
        gnv = gn_ref[...]
        state[...] = jnp.zeros_like(state)

        def chunk(ci, carry):
            rows = pl.ds(pl.multiple_of(ci * c, c), c)
            _, gate = _hgrn_gates(f_ref[rows, :], lbv)
            kk = 1.0 - gate
            qb = q_ref[rows, :]
            qq = qb * _sigmoid(qb) * scale
            v = i_ref[rows, :]
            b = _cumsum_rows(jnp.log(gate), c)
            st = state[...]
            st_ref[ci] = st
            o_inter = lax.dot_general((qq * jnp.exp(b)).astype(BF16), st.astype(BF16), NT,
                                      preferred_element_type=F32)
            amat = _hgrn_intra_attn(qq, kk, b)
            a_ref[ci] = amat
            o = jnp.dot(amat.astype(BF16), v.astype(BF16), preferred_element_type=F32) + o_inter
            opre_ref[rows, :] = o
            bl = b[c - 1:c, :]
            state[...] = st * jnp.exp(bl) + lax.dot_general(
                v.astype(BF16), (kk * jnp.exp(bl - b)).astype(BF16), TN, preferred_element_type=F32)
            r = lax.rsqrt(jnp.mean(o * o, axis=-1, keepdims=True) + NORM_EPS)
            gb = g_ref[rows, :]
            out_ref[rows, :] = (o * r * gnv * (gb * _sigmoid(gb))).astype(BF16)
            return carry

        lax.fori_loop(0, nc, chunk, 0)

    def col(off):
        return pl.BlockSpec((t, B_DIM), lambda h: (0, col0 + off * nh + h))

    return pl.pallas_call(
        kern, name=name, grid=(nh,),
        in_specs=[col(0), col(1), col(2), col(3),
                  pl.BlockSpec((None, 1, B_DIM), lambda h: (h, 0, 0)),
                  pl.BlockSpec((1, B_DIM), lambda h: (0, 0))],
        out_specs=[pl.BlockSpec((t, B_DIM), lambda h: (0, h)),
                   pl.BlockSpec((t, B_DIM), lambda h: (0, h)),
                   pl.BlockSpec((None, nc, B_DIM, B_DIM), lambda h: (h, 0, 0, 0)),
                   pl.BlockSpec((None, nc, c, c), lambda h: (h, 0, 0, 0))],
        out_shape=[_sds((t, nh * B_DIM), BF16), _sds((t, nh * B_DIM), F32),
                   _sds((nh, nc, B_DIM, B_DIM), F32), _sds((nh, nc, c, c), F32)],
        scratch_shapes=[pltpu.VMEM((B_DIM, B_DIM), F32)],
        compiler_params=_params("parallel"),
    )(proj, proj, proj, proj, lb, gn)


def hgrn_bwd(proj, col0, nh, lb, gn, opre, states, amats, dout, dcol0, name):
    t = proj.shape[0]
    c = B_CHUNK
    nc = t // c
    scale = B_DIM ** -0.5
    nsub = c // B_SUB

    def kern(q_ref, f_ref, i_ref, g_ref, lb_ref, gn_ref, opre_ref, st_ref, a_ref, dout_ref,
             dq_ref, df_ref, di_ref, dg_ref, dgn_ref, dlb_ref, dstate, dksc):
        lbv = lb_ref[...]
        gnv = gn_ref[...]
        dstate[...] = jnp.zeros_like(dstate)
        dlb_ref[...] = jnp.zeros_like(dlb_ref)

        @pl.when(pl.program_id(0) == 0)
        def _():
            dgn_ref[...] = jnp.zeros_like(dgn_ref)

        srow = lax.broadcasted_iota(jnp.int32, (c, B_DIM), 0)
        lane = lax.broadcasted_iota(jnp.int32, (B_SUB, c), 1)
        trow = lax.broadcasted_iota(jnp.int32, (B_SUB, B_DIM), 0)
        arow = lax.broadcasted_iota(jnp.int32, (c, c), 0)
        alane = lax.broadcasted_iota(jnp.int32, (c, c), 1)

        def chunk(cj, carry):
            ci = nc - 1 - cj
            rows = pl.ds(pl.multiple_of(ci * c, c), c)
            f = f_ref[rows, :]
            sig, gate = _hgrn_gates(f, lbv)
            kk = 1.0 - gate
            qb = q_ref[rows, :]
            sq = _sigmoid(qb)
            qq = qb * sq * scale
            v = i_ref[rows, :]
            b = _cumsum_rows(jnp.log(gate), c)
            st0 = st_ref[ci]
            dst = dstate[...]
            o = opre_ref[rows, :]
            gb = g_ref[rows, :]
            sg = _sigmoid(gb)
            silu_g = gb * sg
            d_out = dout_ref[rows, :]
            r = lax.rsqrt(jnp.mean(o * o, axis=-1, keepdims=True) + NORM_EPS)
            y = o * r
            dg_ref[rows, :] = (d_out * y * gnv * (sg * (1.0 + gb * (1.0 - sg)))).astype(BF16)
            dyn = d_out * silu_g
            dgn_ref[...] += jnp.sum(dyn * y, axis=0, keepdims=True)
            dy = dyn * gnv
            do = r * (dy - y * jnp.mean(dy * y, axis=-1, keepdims=True))
            eb = jnp.exp(b)
            bl = b[c - 1:c, :]
            ebl = jnp.exp(bl - b)
            ebl_last = jnp.exp(bl)
            do_b = do.astype(BF16)
            dst_b = dst.astype(BF16)
            dq_inter = jnp.dot(do_b, st0.astype(BF16), preferred_element_type=F32) * eb
            dst0 = lax.dot_general(do_b, (qq * eb).astype(BF16), TN,
                                   preferred_element_type=F32) + dst * ebl_last
            dv_inter = lax.dot_general((kk * ebl).astype(BF16), dst_b, NT, preferred_element_type=F32)
            dk_inter = jnp.dot(v.astype(BF16), dst_b, preferred_element_type=F32) * ebl
            amat = a_ref[ci]
            v_b = v.astype(BF16)
            d_a = lax.dot_general(do_b, v_b, NT, preferred_element_type=F32)
            d_a = jnp.where(arow >= alane, d_a, 0.0)
            dv_intra = lax.dot_general(amat.astype(BF16), do_b, TN, preferred_element_type=F32)
            dk_pairs = jnp.zeros((c, B_DIM), F32)
            dq_blocks = []
            for i in range(nsub):
                r0 = i * B_SUB
                qi, bi = qq[r0:r0 + B_SUB, :], b[r0:r0 + B_SUB, :]
                da_i = d_a[r0:r0 + B_SUB, :]
                if i == 0:
                    dq_i = jnp.zeros((B_SUB, B_DIM), F32)
                else:
                    eq, ek = _sub_scales(b, i)
                    da_m = jnp.where(lane < r0, da_i, 0.0)
                    dq_i = _dot3(da_m, kk * ek, NN) * eq
                    dk_pairs = dk_pairs + _dot3(da_m, qi * eq, TN) * ek
                for sl in range(B_SUB):
                    s = r0 + sl
                    e = jnp.exp(jnp.where(trow >= sl, bi - b[s:s + 1, :], -jnp.inf))
                    dacol = jnp.sum(jnp.where(lane == s, da_i, 0.0), axis=1, keepdims=True)
                    w = dacol * e
                    dq_i = dq_i + w * kk[s:s + 1, :]
                    dksc[s:s + 1, :] = jnp.sum(w * qi, axis=0, keepdims=True)
                dq_blocks.append(dq_i)
            dq = jnp.concatenate(dq_blocks, axis=0) + dq_inter
            dk = dk_pairs + dksc[...] + dk_inter
            dv = dv_intra + dv_inter
            db = qq * dq - kk * dk
            extra = (jnp.sum(kk * dk_inter, axis=0, keepdims=True)
                     + ebl_last * jnp.sum(st0 * dst, axis=0, keepdims=True))
            db = db + jnp.where(srow == c - 1, extra, 0.0)
            dlog = _cumsum_rows(db, c, reverse=True)
            dgate = dlog / gate - dk
            df_ref[rows, :] = (dgate * (1.0 - lbv) * sig * (1.0 - sig)).astype(BF16)
            dlb_ref[...] += jnp.sum(dgate * (1.0 - sig), axis=0, keepdims=True)
            dq_ref[rows, :] = (dq * scale * (sq * (1.0 + qb * (1.0 - sq)))).astype(BF16)
            di_ref[rows, :] = dv.astype(BF16)
            dstate[...] = dst0
            return carry

        lax.fori_loop(0, nc, chunk, 0)

    def col(off):
        return pl.BlockSpec((t, B_DIM), lambda h: (0, col0 + off * nh + h))

    hcol = pl.BlockSpec((t, B_DIM), lambda h: (0, h))
    vec = pl.BlockSpec((None, 1, B_DIM), lambda h: (h, 0, 0))
    wide = _sds((t, nh * B_DIM), BF16)
    return pl.pallas_call(
        kern, name=name, grid=(nh,),
        in_specs=[col(0), col(1), col(2), col(3), vec,
                  pl.BlockSpec((1, B_DIM), lambda h: (0, 0)), hcol,
                  pl.BlockSpec((None, nc, B_DIM, B_DIM), lambda h: (h, 0, 0, 0)),
                  pl.BlockSpec((None, nc, c, c), lambda h: (h, 0, 0, 0)),
                  pl.BlockSpec((t, B_DIM), lambda h: (0, dcol0 + h))],
        out_specs=[hcol, hcol, hcol, hcol, pl.BlockSpec((1, B_DIM), lambda h: (0, 0)), vec],
        out_shape=[wide, wide, wide, wide, _sds((1, B_DIM), F32), _sds((nh, 1, B_DIM), F32)],
        scratch_shapes=[pltpu.VMEM((B_DIM, B_DIM), F32), pltpu.VMEM((c, B_DIM), F32)],
        compiler_params=_params("arbitrary"),
    )(proj, proj, proj, proj, lb, gn, opre, states, amats, dout)


def lower_bounds_fwd(raw, name):
    n, w = raw.shape

    def kern(raw_ref, lb_ref, soft_ref):
        r = raw_ref[...]
        mx = r[0:1]
        for i in range(1, n):
            mx = jnp.maximum(mx, r[i:i + 1])
        e = jnp.exp(r - mx)
        den = e[0:1]
        for i in range(1, n):
            den = den + e[i:i + 1]
        soft = e / den
        soft_ref[...] = soft
        run = soft[0:1]
        lb_ref[0:1, :] = run - soft[0:1]
        for i in range(1, n):
            run = run + soft[i:i + 1]
            lb_ref[i:i + 1, :] = run - soft[0:1]

    return pl.pallas_call(kern, name=name, out_shape=[_sds((n, w), F32), _sds((n, w), F32)])(raw)


def lower_bounds_bwd(soft, dlb, name):
    n, w = soft.shape

    def kern(soft_ref, dlb_ref, out_ref):
        s = soft_ref[...]
        d = dlb_ref[...]
        total = d[0:1]
        for i in range(1, n):
            total = total + d[i:i + 1]
        us = []
        tail = total
        for i in range(n):
            us.append(tail - total if i == 0 else tail)
            tail = tail - d[i:i + 1]
        dot = s[0:1] * us[0]
        for i in range(1, n):
            dot = dot + s[i:i + 1] * us[i]
        for i in range(n):
            out_ref[i:i + 1, :] = s[i:i + 1] * (us[i] - dot)

    return pl.pallas_call(kern, name=name, out_shape=_sds((n, w), F32))(soft, dlb)


def _row_tile(kdim, n):
    tk = 512
    while tk > 8 and tk * n > 256 * 1024:
        tk //= 2
    return min(kdim, tk)


def _adam_update(w, g, m, v):
    m2 = ADAM_B1 * m + (1.0 - ADAM_B1) * g
    v2 = ADAM_B2 * v + (1.0 - ADAM_B2) * (g * g)
    m_hat = m2 / (1.0 - ADAM_B1 ** ADAM_STEP)
    v_hat = v2 / (1.0 - ADAM_B2 ** ADAM_STEP)
    delta = -ADAM_LR * (m_hat / (jnp.sqrt(v_hat) + ADAM_EPS) + ADAM_WD * w)
    return delta, m2, v2


def adamw_small(w, g, m, v, name):
    def kern(w_ref, g_ref, m_ref, v_ref, d_ref, m2_ref, v2_ref):
        d, m2, v2 = _adam_update(w_ref[...], g_ref[...], m_ref[...], v_ref[...])
        d_ref[...] = d
        m2_ref[...] = m2
        v2_ref[...] = v2

    return pl.pallas_call(kern, name=name, out_shape=[_sds(w.shape, F32)] * 3)(w, g, m, v)


def adamw_big(parts, w, m, v, name):
    nl, kdim, n = w.shape
    tk = _row_tile(kdim, n)

    def kern(p_ref, w_ref, m_ref, v_ref, g_ref, d_ref, m2_ref, v2_ref):
        g = p_ref[0].astype(F32)
        for q in range(1, 4):
            g = g + p_ref[q].astype(F32)
        d, m2, v2 = _adam_update(w_ref[...], g, m_ref[...], v_ref[...])
        g_ref[...] = g
        d_ref[...] = d
        m2_ref[...] = m2
        v2_ref[...] = v2

    blk = pl.BlockSpec((None, tk, n), lambda l, i: (l, i, 0))
    return pl.pallas_call(
        kern, name=name, grid=(nl, kdim // tk),
        in_specs=[pl.BlockSpec((None, 4, tk, n), lambda l, i: (l, 0, i, 0)), blk, blk, blk],
        out_specs=[blk] * 4, out_shape=[_sds(w.shape, F32)] * 4,
        compiler_params=_params("parallel", "parallel"),
    )(parts, w, m, v)


def cast_bf16(w, name):
    nl, kdim, n = w.shape
    tk = _row_tile(kdim, n)

    def kern(w_ref, o_ref):
        o_ref[...] = w_ref[...].astype(BF16)

    blk = pl.BlockSpec((None, tk, n), lambda l, i: (l, i, 0))
    return pl.pallas_call(
        kern, name=name, grid=(nl, kdim // tk), in_specs=[blk], out_specs=blk,
        out_shape=_sds(w.shape, BF16), compiler_params=_params("parallel", "parallel"),
    )(w)


def pair_add(dw, r1, core, name):
    kdim, n = dw.shape[1], dw.shape[2]
    tk = _row_tile(kdim, n)

    def kern(c_ref, a_ref, b_ref, o_ref):
        o_ref[...] = (a_ref[...].astype(F32) + b_ref[...].astype(F32)).astype(BF16)

    grid_spec = pltpu.PrefetchScalarGridSpec(
        num_scalar_prefetch=1, grid=(4, kdim // tk),
        in_specs=[pl.BlockSpec((None, tk, n), lambda p, i, c: (2 * p + c[0], i, 0)),
                  pl.BlockSpec((None, tk, n), lambda p, i, c: (p, i, 0))],
        out_specs=pl.BlockSpec((None, tk, n), lambda p, i, c: (p, i, 0)))
    return pl.pallas_call(
        kern, name=name, grid_spec=grid_spec, out_shape=_sds((4, kdim, n), BF16),
        compiler_params=_params("parallel", "parallel"),
    )(core, dw, r1)


ANY = pl.BlockSpec(memory_space=pl.ANY)


def _place():
    x, y, c = lax.axis_index("x"), lax.axis_index("y"), lax.axis_index("c")
    chips = [(1 - x, y), (x, 1 - y), (1 - x, 1 - y)]
    return x, y, c, chips


def all_gather(shards, name):
    n = len(shards)

    def kern(*refs):
        ins, outs = refs[:n], refs[n:2 * n]
        send_sems, recv_sems, local_sems = refs[2 * n:]
        x, y, c, chips = _place()
        me, sib = (x, y, c), (x, y, 1 - c)

        def copy(t, k, block, to, src=None):
            px, py, pc = block
            dst = outs[t].at[4 * px + 2 * py + pc]
            return pltpu.make_async_remote_copy(
                src_ref=dst if src is None else src, dst_ref=dst,
                send_sem=send_sems.at[7 * t + k], recv_sem=recv_sems.at[7 * t + k],
                device_id=to, device_id_type=MESH)

        mine = [pltpu.make_async_copy(ins[t], outs[t].at[4 * x + 2 * y + c], local_sems.at[t])
                for t in range(n)]
        for cp in mine:
            cp.start()
        first = []
        for t in range(n):
            first.append(copy(t, 0, me, sib, src=ins[t]))
            first += [copy(t, 1 + j, me, (*chip, c), src=ins[t]) for j, chip in enumerate(chips)]
        for cp in first:
            cp.start()
        passed = []
        for t in range(n):
            for j, chip in enumerate(chips):
                copy(t, 1 + j, (*chip, c), me).wait_recv()
                fwd = copy(t, 4 + j, (*chip, c), sib)
                fwd.start()
                passed.append(fwd)
        for t in range(n):
            copy(t, 0, sib, me).wait_recv()
            for j, chip in enumerate(chips):
                copy(t, 4 + j, (*chip, 1 - c), me).wait_recv()
        for cp in first + passed:
            cp.wait_send()
        for cp in mine:
            cp.wait()

    return pl.pallas_call(
        kern, name=name, in_specs=[ANY] * n, out_specs=[ANY] * n,
        out_shape=[_sds((N_DEV,) + s.shape, s.dtype) for s in shards],
        scratch_shapes=[pltpu.SemaphoreType.DMA((7 * n,)), pltpu.SemaphoreType.DMA((7 * n,)),
                        pltpu.SemaphoreType.DMA((n,))],
    )(*shards)


HBM = pl.BlockSpec(memory_space=pltpu.HBM)
SEM = pl.BlockSpec(memory_space=pltpu.SEMAPHORE)
DATAFLOW = pltpu.SideEffectType.DATAFLOW_SIDE_EFFECTING


def _first_level_targets():
    x, y, c, chips = _place()
    return 4 * x + 2 * y + c, [(x, y, 1 - c)] + [(*chip, c) for chip in chips]


def gather_start(shards, after, name):
    n = len(shards)
    lands = [lax.empty((N_DEV,) + s.shape, s.dtype) for s in shards]

    def kern(*refs):
        ins, lnd = refs[:n], refs[n:2 * n]
        send_sems, recv_sems, local_sems = refs[2 * n + len(after):2 * n + len(after) + 3]
        token = refs[-1]
        me, targets = _first_level_targets()
        for t in range(n):
            pltpu.make_async_copy(ins[t], lnd[t].at[me], local_sems.at[t]).start()
            for k, to in enumerate(targets):
                pltpu.make_async_remote_copy(
                    src_ref=ins[t], dst_ref=lnd[t].at[me], send_sem=send_sems.at[4 * t + k],
                    recv_sem=recv_sems.at[4 * t + k], device_id=to, device_id_type=MESH).start()
        token[...] = jnp.zeros_like(token)

    args = [pltpu.with_memory_space_constraint(a, pltpu.HBM) for a in list(shards) + lands]
    return pl.pallas_call(
        kern, name=name,
        out_shape=(pltpu.SemaphoreType.DMA((4 * n,)), pltpu.SemaphoreType.DMA((4 * n,)),
                   pltpu.SemaphoreType.DMA((n,)),
                   *[pltpu.HBM(a.shape, a.dtype) for a in args], _sds((8, LANES), F32)),
        in_specs=[HBM] * (2 * n) + [ANY] * len(after),
        out_specs=(SEM, SEM, SEM, *[HBM] * (2 * n), pl.BlockSpec(memory_space=pltpu.VMEM)),
        input_output_aliases={i: 3 + i for i in range(2 * n)},
        compiler_params=pltpu.CompilerParams(has_side_effects=DATAFLOW),
    )(*args, *after)


def gather_wait(send_sems, recv_sems, local_sems, shards, lands, after, name):
    n = len(shards)

    def kern(*refs):
        ins, lnd = refs[:n], refs[n:2 * n]
        send_sems, recv_sems, local_sems = refs[2 * n:2 * n + 3]
        me, targets = _first_level_targets()
        for t in range(n):
            pltpu.make_async_copy(ins[t], lnd[t].at[me], local_sems.at[t]).wait()
            for k, to in enumerate(targets):
                cp = pltpu.make_async_remote_copy(
                    src_ref=ins[t], dst_ref=lnd[t].at[me], send_sem=send_sems.at[4 * t + k],
                    recv_sem=recv_sems.at[4 * t + k], device_id=to, device_id_type=MESH)
                cp.wait_send()
                cp.wait_recv()

    bufs = list(shards) + list(lands)
    return pl.pallas_call(
        kern, name=name, out_shape=tuple(pltpu.HBM(a.shape, a.dtype) for a in bufs),
        in_specs=[HBM] * (2 * n) + [SEM, SEM, SEM, ANY], out_specs=[HBM] * (2 * n),
        input_output_aliases={i: i for i in range(2 * n)},
        compiler_params=pltpu.CompilerParams(has_side_effects=DATAFLOW),
    )(*bufs, send_sems, recv_sems, local_sems, after)


def _forward_copies(lnd, send_sems, recv_sems):
    x, y, c, chips = _place()
    passed = []
    for t in range(len(lnd)):
        for j, (qx, qy) in enumerate(chips):
            block = lnd[t].at[4 * qx + 2 * qy + c]
            passed.append(pltpu.make_async_remote_copy(
                src_ref=block, dst_ref=block, send_sem=send_sems.at[3 * t + j],
                recv_sem=recv_sems.at[3 * t + j], device_id=(x, y, 1 - c), device_id_type=MESH))
    return passed


def forward_now(lands, name):
    n = len(lands)

    def kern(*refs):
        copies = _forward_copies(refs[n:2 * n], refs[2 * n], refs[2 * n + 1])
        for cp in copies:
            cp.start()
        for cp in copies:
            cp.wait_recv()
        for cp in copies:
            cp.wait_send()

    return pl.pallas_call(
        kern, name=name, in_specs=[ANY] * n, out_specs=[ANY] * n,
        out_shape=[_sds(a.shape, a.dtype) for a in lands],
        input_output_aliases={i: i for i in range(n)},
        scratch_shapes=[pltpu.SemaphoreType.DMA((3 * n,)), pltpu.SemaphoreType.DMA((3 * n,))],
    )(*lands)


def sibling_start(grads, name):
    n = len(grads)
    lands = [lax.empty((4,) + g.shape[1:], g.dtype) for g in grads]

    def kern(*refs):
        ins, lnd = refs[:n], refs[n:2 * n]
        send_sems, recv_sems = refs[2 * n], refs[2 * n + 1]
        x, y, c, _ = _place()
        for t in range(n):
            for p in range(4):
                pltpu.make_async_remote_copy(
                    src_ref=ins[t].at[2 * p + 1 - c], dst_ref=lnd[t].at[p],
                    send_sem=send_sems.at[4 * t + p], recv_sem=recv_sems.at[4 * t + p],
                    device_id=(x, y, 1 - c), device_id_type=MESH).start()
        refs[-1][...] = jnp.zeros_like(refs[-1])

    args = [pltpu.with_memory_space_constraint(a, pltpu.HBM) for a in list(grads) + lands]
    return pl.pallas_call(
        kern, name=name,
        out_shape=(pltpu.SemaphoreType.DMA((4 * n,)), pltpu.SemaphoreType.DMA((4 * n,)),
                   *[pltpu.HBM(a.shape, a.dtype) for a in args], _sds((8, LANES), F32)),
        in_specs=[HBM] * (2 * n),
        out_specs=(SEM, SEM, *[HBM] * (2 * n), pl.BlockSpec(memory_space=pltpu.VMEM)),
        input_output_aliases={i: 2 + i for i in range(2 * n)},
        compiler_params=pltpu.CompilerParams(has_side_effects=DATAFLOW),
    )(*args)


def sibling_wait(send_sems, recv_sems, grads, lands, after, name):
    n = len(grads)

    def kern(*refs):
        ins, lnd = refs[:n], refs[n:2 * n]
        send_sems, recv_sems = refs[2 * n], refs[2 * n + 1]
        x, y, c, _ = _place()
        for t in range(n):
            for p in range(4):
                cp = pltpu.make_async_remote_copy(
                    src_ref=ins[t].at[2 * p + 1 - c], dst_ref=lnd[t].at[p],
                    send_sem=send_sems.at[4 * t + p], recv_sem=recv_sems.at[4 * t + p],
                    device_id=(x, y, 1 - c), device_id_type=MESH)
                cp.wait_send()
                cp.wait_recv()

    bufs = list(grads) + list(lands)
    outs = pl.pallas_call(
        kern, name=name, out_shape=tuple(pltpu.HBM(a.shape, a.dtype) for a in bufs),
        in_specs=[HBM] * (2 * n) + [SEM, SEM, ANY], out_specs=[HBM] * (2 * n),
        input_output_aliases={i: i for i in range(2 * n)},
        compiler_params=pltpu.CompilerParams(has_side_effects=DATAFLOW),
    )(*bufs, send_sems, recv_sems, after)
    return outs[:n], outs[n:]


def forward_start(lands, name):
    n = len(lands)

    def kern(*refs):
        for cp in _forward_copies(refs[:n], refs[n], refs[n + 1]):
            cp.start()
        refs[-1][...] = jnp.zeros_like(refs[-1])

    return pl.pallas_call(
        kern, name=name,
        out_shape=(pltpu.SemaphoreType.DMA((3 * n,)), pltpu.SemaphoreType.DMA((3 * n,)),
                   *[pltpu.HBM(a.shape, a.dtype) for a in lands], _sds((8, LANES), F32)),
        in_specs=[HBM] * n,
        out_specs=(SEM, SEM, *[HBM] * n, pl.BlockSpec(memory_space=pltpu.VMEM)),
        input_output_aliases={i: 2 + i for i in range(n)},
        compiler_params=pltpu.CompilerParams(has_side_effects=DATAFLOW),
    )(*lands)


def forward_wait(send_sems, recv_sems, lands, after, name):
    n = len(lands)

    def kern(*refs):
        for cp in _forward_copies(refs[:n], refs[n], refs[n + 1]):
            cp.wait_send()
            cp.wait_recv()

    return pl.pallas_call(
        kern, name=name, out_shape=tuple(pltpu.HBM(a.shape, a.dtype) for a in lands),
        in_specs=[HBM] * n + [SEM, SEM, ANY], out_specs=[HBM] * n,
        input_output_aliases={i: i for i in range(n)},
        compiler_params=pltpu.CompilerParams(has_side_effects=DATAFLOW),
    )(*lands, send_sems, recv_sems, after)


def all_reduce_small(vec, name):
    r = vec.shape[0]

    def kern(v_ref, o_ref, buf, send_sems, recv_sems):
        x, y, c, _ = _place()
        me = 4 * x + 2 * y + c
        peers = [(x, y, 1 - c), (1 - x, y, c), (x, 1 - y, c), (1 - x, 1 - y, c),
                 (1 - x, y, 1 - c), (x, 1 - y, 1 - c), (1 - x, 1 - y, 1 - c)]
        buf[me] = v_ref[...]
        copies = []
        for k, peer in enumerate(peers):
            cp = pltpu.make_async_remote_copy(
                src_ref=v_ref, dst_ref=buf.at[me], send_sem=send_sems.at[k],
                recv_sem=recv_sems.at[k], device_id=peer, device_id_type=MESH)
            cp.start()
            copies.append(cp)
        for cp in copies:
            cp.wait_recv()
        for cp in copies:
            cp.wait_send()
        total = buf[0]
        for d in range(1, N_DEV):
            total = total + buf[d]
        o_ref[...] = total

    vm = pl.BlockSpec(memory_space=pltpu.VMEM)
    return pl.pallas_call(
        kern, name=name, in_specs=[vm], out_specs=vm, out_shape=_sds(vec.shape, F32),
        scratch_shapes=[pltpu.VMEM((N_DEV, r, LANES), F32), pltpu.SemaphoreType.DMA((7,)),
                        pltpu.SemaphoreType.DMA((7,))],
    )(vec)


def exchange_with_sibling(grads, name):
    n = len(grads)

    def kern(*refs):
        ins, outs = refs[:n], refs[n:2 * n]
        send_sems, recv_sems = refs[2 * n:]
        x, y, c, _ = _place()
        copies = []
        for t in range(n):
            for p in range(4):
                cp = pltpu.make_async_remote_copy(
                    src_ref=ins[t].at[2 * p + 1 - c], dst_ref=outs[t].at[p],
                    send_sem=send_sems.at[4 * t + p], recv_sem=recv_sems.at[4 * t + p],
                    device_id=(x, y, 1 - c), device_id_type=MESH)
                cp.start()
                copies.append(cp)
        for cp in copies:
            cp.wait_recv()
        for cp in copies:
            cp.wait_send()

    return pl.pallas_call(
        kern, name=name, in_specs=[ANY] * n, out_specs=[ANY] * n,
        out_shape=[_sds((4,) + g.shape[1:], g.dtype) for g in grads],
        scratch_shapes=[pltpu.SemaphoreType.DMA((4 * n,)), pltpu.SemaphoreType.DMA((4 * n,))],
    )(*grads)


def exchange_between_chips(partials, layers, kinds, name):
    n = len(partials)
    n_kind = max(kinds) + 1
    shapes = []
    for kd in range(n_kind):
        idx = [i for i in range(n) if kinds[i] == kd]
        nl = max(layers[i] for i in idx) + 1
        shapes.append(_sds((nl,) + partials[idx[0]].shape, partials[idx[0]].dtype))

    def kern(*refs):
        ins, outs = refs[:n], refs[n:n + n_kind]
        send_sems, recv_sems, local_sems = refs[n + n_kind:]
        x, y, c, chips = _place()
        mine = 2 * x + y
        local = []
        copies = []
        for t in range(n):
            dst = outs[kinds[t]].at[layers[t], mine]
            lc = pltpu.make_async_copy(ins[t].at[mine], dst, local_sems.at[t])
            lc.start()
            local.append(lc)
            for j, (qx, qy) in enumerate(chips):
                cp = pltpu.make_async_remote_copy(
                    src_ref=ins[t].at[2 * qx + qy], dst_ref=dst,
                    send_sem=send_sems.at[3 * t + j], recv_sem=recv_sems.at[3 * t + j],
                    device_id=(qx, qy, c), device_id_type=MESH)
                cp.start()
                copies.append(cp)
        for cp in copies:
            cp.wait_recv()
        for cp in copies:
            cp.wait_send()
        for lc in local:
            lc.wait()

    return pl.pallas_call(
        kern, name=name, in_specs=[ANY] * n, out_specs=[ANY] * n_kind, out_shape=shapes,
        scratch_shapes=[pltpu.SemaphoreType.DMA((3 * n,)), pltpu.SemaphoreType.DMA((3 * n,)),
                        pltpu.SemaphoreType.DMA((n,))],
    )(*partials)


def scatter_start(partials, name):
    n = len(partials)
    lands = [lax.empty(p.shape, p.dtype) for p in partials]

    def kern(*refs):
        ins, lnd = refs[:n], refs[n:2 * n]
        send_sems, recv_sems, local_sems = refs[2 * n:2 * n + 3]
        token = refs[-1]
        x, y, c, chips = _place()
        mine = 2 * x + y
        for t in range(n):
            pltpu.make_async_copy(ins[t].at[mine], lnd[t].at[mine], local_sems.at[t]).start()
            for j, (qx, qy) in enumerate(chips):
                pltpu.make_async_remote_copy(
                    src_ref=ins[t].at[2 * qx + qy], dst_ref=lnd[t].at[mine],
                    send_sem=send_sems.at[3 * t + j], recv_sem=recv_sems.at[3 * t + j],
                    device_id=(qx, qy, c), device_id_type=MESH).start()
        token[...] = jnp.zeros_like(token)

    args = [pltpu.with_memory_space_constraint(a, pltpu.HBM) for a in list(partials) + lands]
    return pl.pallas_call(
        kern, name=name,
        out_shape=(pltpu.SemaphoreType.DMA((3 * n,)), pltpu.SemaphoreType.DMA((3 * n,)),
                   pltpu.SemaphoreType.DMA((n,)),
                   *[pltpu.HBM(a.shape, a.dtype) for a in args], _sds((8, LANES), F32)),
        in_specs=[HBM] * (2 * n),
        out_specs=(SEM, SEM, SEM, *[HBM] * (2 * n), pl.BlockSpec(memory_space=pltpu.VMEM)),
        input_output_aliases={i: 3 + i for i in range(2 * n)},
        compiler_params=pltpu.CompilerParams(has_side_effects=DATAFLOW),
    )(*args)


def scatter_wait(send_sems, recv_sems, local_sems, partials, lands, after, name):
    n = len(partials)

    def kern(*refs):
        ins, lnd = refs[:n], refs[n:2 * n]
        send_sems, recv_sems, local_sems = refs[2 * n:2 * n + 3]
        x, y, c, chips = _place()
        mine = 2 * x + y
        for t in range(n):
            pltpu.make_async_copy(ins[t].at[mine], lnd[t].at[mine], local_sems.at[t]).wait()
            for j, (qx, qy) in enumerate(chips):
                cp = pltpu.make_async_remote_copy(
                    src_ref=ins[t].at[2 * qx + qy], dst_ref=lnd[t].at[mine],
                    send_sem=send_sems.at[3 * t + j], recv_sem=recv_sems.at[3 * t + j],
                    device_id=(qx, qy, c), device_id_type=MESH)
                cp.wait_send()
                cp.wait_recv()

    bufs = list(partials) + list(lands)
    outs = pl.pallas_call(
        kern, name=name, out_shape=tuple(pltpu.HBM(a.shape, a.dtype) for a in bufs),
        in_specs=[HBM] * (2 * n) + [SEM, SEM, SEM, ANY], out_specs=[HBM] * (2 * n),
        input_output_aliases={i: i for i in range(2 * n)},
        compiler_params=pltpu.CompilerParams(has_side_effects=DATAFLOW),
    )(*bufs, send_sems, recv_sems, local_sems, after)
    return outs[n:]


def adamw_layers(parts, w, m, v, name):
    nl, kdim, n = w.shape
    tk = _row_tile(kdim, n)

    def kern(*refs):
        p_refs = refs[:nl]
        w_ref, m_ref, v_ref, g_ref, d_ref, m2_ref, v2_ref = refs[nl:]
        for l in range(nl):
            @pl.when(pl.program_id(0) == l)
            def _():
                g = p_refs[l][0].astype(F32)
                for q in range(1, 4):
                    g = g + p_refs[l][q].astype(F32)
                d, m2, v2 = _adam_update(w_ref[...], g, m_ref[...], v_ref[...])
                g_ref[...] = g
                d_ref[...] = d
                m2_ref[...] = m2
                v2_ref[...] = v2

    def part_spec(l):
        return pl.BlockSpec((4, tk, n), lambda li, i: (0, jnp.where(li == l, i, 0), 0))

    blk = pl.BlockSpec((None, tk, n), lambda li, i: (li, i, 0))
    return pl.pallas_call(
        kern, name=name, grid=(nl, kdim // tk),
        in_specs=[part_spec(l) for l in range(nl)] + [blk, blk, blk],
        out_specs=[blk] * 4, out_shape=[_sds(w.shape, F32)] * 4,
        compiler_params=_params("arbitrary", "arbitrary"),
    )(*parts, w, m, v)


def _pack(arrays):
    flat = jnp.concatenate([a.reshape(-1).astype(F32) for a in arrays])
    pad = (-flat.shape[0]) % (8 * LANES)
    return jnp.pad(flat, (0, pad)).reshape(-1, LANES)


def _unpack(packed, shapes):
    flat = packed.reshape(-1)
    out, off = [], 0
    for s in shapes:
        n = math.prod(s)
        out.append(flat[off:off + n].reshape(s))
        off += n
    return out


def _to_heads(x2d, dil, n_heads, dh):
    t = x2d.shape[0]
    return x2d.reshape(t // dil, dil, n_heads, dh).transpose(2, 1, 0, 3).reshape(n_heads, t, dh)


def _from_heads(xh, dil):
    h, t, w = xh.shape
    return xh.reshape(h, dil, t // dil, w).transpose(2, 1, 0, 3).reshape(t, h * w)


def _unperm(xh, dil):
    h, t, w = xh.shape
    return xh.reshape(h, dil, t // dil, w).transpose(0, 2, 1, 3).reshape(h, t, w)


def _perm(xh, dil):
    h, t, w = xh.shape
    return xh.reshape(h, t // dil, dil, w).transpose(0, 2, 1, 3).reshape(h, t, w)


def local_step(x, target, norm_mix_g, norm_mlp_g, final_norm_g, lbs, hgrn_norm_g, sinks,
               bq_full, bo_full, weights_get, weights_mid, grads_ready):
    t, d = x.shape
    depth = norm_mix_g.shape[0]
    na = d // 2 // A_DIM
    nbh = d // 2 // B_DIM
    nq = d // C_DIM
    nkv = nq // C_GROUP
    a_w = 3 * na * A_DIM
    c_w = (nq + 2 * nkv) * C_DIM
    tabs_a = rope_tables(t, A_DIM)
    tabs_c = rope_tables(t, C_DIM)
    saved = []
    for l in range(depth):
        s = {"x_in": x}
        (win_g, wout_g), token = weights_get(l, x)
        h = rms_fwd(x, norm_mix_g[l] + token, "norm_mix_fwd")
        s["h"] = h
        if l % 2 == 0:
            e = l // 2
            proj = mm_cols_sharded(h, win_g, 0, "even_in_proj")[0]
            qkv_r = rope_call(proj, tabs_a, a_w, 2 * na, False, "rope_a")[0]
            nums, ms, ls = [], [], []
            for window, dil in A_BRANCHES:
                num, m, lsum = dilated_fwd(qkv_r, na, dil, window // dil, f"dilated_fwd_{dil}")
                nums.append(num)
                ms.append(m)
                ls.append(lsum)
            oa, oa_b, lse = dilated_merge(nums, ms, ls, "dilated_merge")
            lb_e = lbs[e].reshape(nbh, 1, B_DIM)
            gn_e = hgrn_norm_g[e].reshape(1, B_DIM)
            ob, opre, states, amats = hgrn_fwd(proj, 3 * na, nbh, lb_e, gn_e, "hgrn_fwd")
            mixed = jnp.concatenate([oa_b, ob], axis=1)
            x = mm_rows_sharded(mixed, wout_g, 0, "even_out_proj", [x], ["tile"], _ep_residual)
            s.update(proj=proj, qkv_r=qkv_r, oa=oa, lse=lse, opre=opre, states=states,
                     amats=amats, mixed=mixed, lb=lb_e, gn=gn_e)
        else:
            o = l // 2
            wq = win_g[:, 0].transpose(1, 0, 2).reshape(d, c_w)
            proj = mm_plain(h, wq, "odd_qkv_proj", [bq_full[o].reshape(1, c_w)], ["row"], _ep_bias)
            qkv_r = rope_call(proj, tabs_c, c_w, (nq + nkv) * C_DIM // LANES, False, "rope_c")[0]
            hm = _to_heads(qkv_r, 1, nq + 2 * nkv, C_DIM)
            sink_rows = jnp.repeat(sinks[o].reshape(nkv, C_GROUP), BLK, axis=1).reshape(
                nkv, C_GROUP * BLK, 1)
            o_hm, lse = band_fwd(hm, 0, nq, nq + nkv, nkv, C_GROUP, t // BLK, C_WINDOW - 1,
                                 "swa_fwd", sink_rows=sink_rows, normalise=True)
            attn = _from_heads(o_hm, 1).astype(BF16)
            x = mm_rows_sharded(attn, wout_g, 0, "odd_out_proj", [bo_full[o].reshape(1, d), x],
                                ["row", "tile"], _ep_bias_residual)
            s.update(wq=wq, hm=hm, sink_rows=sink_rows, o_hm=o_hm, lse=lse, attn=attn)
        s["x_mid"] = x
        (w1_g, w2_g), token = weights_mid(l, x)
        s.update(win=win_g, wout=wout_g, w1=w1_g, w2=w2_g)
        h2 = rms_fwd(x, norm_mlp_g[l] + token, "norm_mlp_fwd")
        u, act = mm_cols_sharded(h2, w1_g, 0, "mlp_up", epilogue=_ep_relu2, n_out=2)
        x = mm_rows_sharded(act, w2_g, 0, "mlp_down", [x], ["tile"], _ep_residual)
        s.update(h2=h2, u=u, act=act)
        saved.append(s)

    dx, dxb, dg_final, loss_part = loss_head(x, final_norm_g, target, "loss_head")
    big = []
    small = {"final": dg_final, "loss": loss_part, "mix": [None] * depth, "mlp": [None] * depth,
             "lb": {}, "gn": {}, "sinks": {}, "bq": {}, "bo": {}}
    for l in reversed(range(depth)):
        s = saved[l]
        win_g, wout_g, w1_g, w2_g = s["win"], s["wout"], s["w1"], s["w2"]
        big.append(("w2", l, mm_tn(s["act"], dxb, "mlp_down_dw").reshape(N_DEV, -1, d)))
        du = mm_nt_rows_sharded(dxb, w2_g, 0, "mlp_down_dx", extras=[s["u"]],
                                epilogue=_ep_relu2_bwd, out_dtype=BF16)
        big.append(("w1", l, mm_tn(s["h2"], du, "mlp_up_dw", shard_cols=w1_g.shape[-1])))
        dh2 = mm_nt_cols_sharded(du, w1_g, 0, "mlp_up_dx")
        token = grads_ready(l, big[-2:])
        dx, dxb, dg, col_dx = rms_bwd(s["x_mid"], norm_mlp_g[l] + token, dh2, dx, "norm_mlp_bwd")
        small["mlp"][l] = dg
        if l % 2 == 0:
            e = l // 2
            big.append(("wout", e, mm_tn(s["mixed"], dxb, "even_out_dw").reshape(N_DEV, -1, d)))
            dmixed = mm_nt_rows_sharded(dxb, wout_g, 0, "even_out_dx")
            delta = dilated_delta(s["oa"], dmixed, "dilated_delta")
            dqs, dks, dvs = [], [], []
            for window, dil in A_BRANCHES:
                dq, dk, dv = dilated_bwd(s["qkv_r"], dmixed, s["lse"], delta, na, dil,
                                         window // dil, f"dilated_bwd_{dil}")
                dqs.append(dq)
                dks.append(dk)
                dvs.append(dv)
            dqkv_a = rope_bwd_sum(dqs, dks, dvs, tabs_a, "rope_a_bwd")
            dqb, dfb, dib, dgb, dgn, dlb = hgrn_bwd(s["proj"], 3 * na, nbh, s["lb"], s["gn"],
                                                    s["opre"], s["states"], s["amats"], dmixed, na,
                                                    "hgrn_bwd")
            small["gn"][e] = dgn
            small["lb"][e] = dlb
            dproj = jnp.concatenate([dqkv_a, dqb, dfb, dib, dgb], axis=1)
            big.append(("win", e, mm_tn(s["h"], dproj, "even_in_dw", shard_cols=win_g.shape[-1])))
            dh = mm_nt_cols_sharded(dproj, win_g, 0, "even_in_dx")
        else:
            o = l // 2
            small["bo"][o] = col_dx
            big.append(("wo", o, mm_tn(s["attn"], dxb, "odd_out_dw").reshape(N_DEV, -1, d)))
            dattn = mm_nt_rows_sharded(dxb, wout_g, 0, "odd_out_dx")
            do_hm = _to_heads(dattn, 1, nq, C_DIM)
            dq, dk, dv, dsink = band_bwd(s["hm"], 0, nq, nq + nkv, do_hm, s["lse"], s["o_hm"],
                                         nkv, C_GROUP, t // BLK, C_WINDOW - 1, "swa_bwd",
                                         sink_rows=s["sink_rows"], delta_from_o=True)
            small["sinks"][o] = dsink
            dqkv = _from_heads(jnp.concatenate([dq, dk, dv], axis=0), 1)
            dproj, dbq = rope_call(dqkv, tabs_c, c_w, (nq + nkv) * C_DIM // LANES, True,
                                   "rope_c_bwd", col_sum=True)
            small["bq"][o] = dbq
            dwq = mm_tn(s["h"], dproj, "odd_qkv_dw", tn=512)
            big.append(("wqkv", o, dwq.reshape(d, N_DEV, -1).transpose(1, 0, 2)))
            dh = mm_nt_plain(dproj, s["wq"], "odd_qkv_dx", tk=c_w)
        token = grads_ready(l, big[-2:])
        dx, dxb, dg, _ = rms_bwd(s["x_in"], norm_mix_g[l] + token, dh, dx, "norm_mix_bwd")
        small["mix"][l] = dg
    return dx, small


def kernel(x, norm_mix_g, norm_mlp_g, final_norm_g, even_w_in, even_w_out, hgrn_lb_raw, hgrn_norm_g, odd_w_qkv, odd_b_qkv, odd_sinks, odd_w_o, odd_b_o, mlp_w1, mlp_w2, loss_target, m_norm_mix_g, m_norm_mlp_g, m_final_norm_g, m_even_w_in, m_even_w_out, m_hgrn_lb_raw, m_hgrn_norm_g, m_odd_w_qkv, m_odd_b_qkv, m_odd_sinks, m_odd_w_o, m_odd_b_o, m_mlp_w1, m_mlp_w2, v_norm_mix_g, v_norm_mlp_g, v_final_norm_g, v_even_w_in, v_even_w_out, v_hgrn_lb_raw, v_hgrn_norm_g, v_odd_w_qkv, v_odd_b_qkv, v_odd_sinks, v_odd_w_o, v_odd_b_o, v_mlp_w1, v_mlp_w2):
    d = x.shape[2]
    depth = norm_mix_g.shape[0]
    n_even, n_odd = even_w_in.shape[0], odd_w_qkv.shape[0]
    xi, yi, ci = lax.axis_index("x"), lax.axis_index("y"), lax.axis_index("c")
    dev = 4 * xi + 2 * yi + ci
    core = ci.astype(jnp.int32).reshape(1)

    big_w = {"win": even_w_in, "wout": even_w_out, "wqkv": odd_w_qkv, "wo": odd_w_o,
             "w1": mlp_w1, "w2": mlp_w2}
    big_m = {"win": m_even_w_in, "wout": m_even_w_out, "wqkv": m_odd_w_qkv, "wo": m_odd_w_o,
             "w1": m_mlp_w1, "w2": m_mlp_w2}
    big_v = {"win": v_even_w_in, "wout": v_even_w_out, "wqkv": v_odd_w_qkv, "wo": v_odd_w_o,
             "w1": v_mlp_w1, "w2": v_mlp_w2}
    kinds = list(big_w)
    casts = {k: cast_bf16(big_w[k], f"cast_{k}") for k in kinds}

    def layer_shards(l):
        a, b = ("win", "wout") if l % 2 == 0 else ("wqkv", "wo")
        return [casts[a][l // 2], casts[b][l // 2], casts["w1"][l], casts["w2"][l]]

    bq_w, bo_w = odd_b_qkv.shape[1], odd_b_o.shape[1]
    bq_mine = lax.dynamic_update_slice(jnp.zeros((n_odd, N_DEV * bq_w), F32), odd_b_qkv,
                                       (0, dev * bq_w))
    bo_mine = lax.dynamic_update_slice(jnp.zeros((n_odd, N_DEV * bo_w), F32), odd_b_o,
                                       (0, dev * bo_w))
    biases = all_reduce_small(_pack([bq_mine, bo_mine]), "gather_biases")
    bq_full, bo_full = _unpack(biases, [bq_mine.shape, bo_mine.shape])

    first_level = {}
    second_level = {}
    ready = {}
    zero = jnp.zeros((), F32)

    def start_first_level(key, shards, after):
        started = gather_start(shards, after, f"gather_start_{key}")
        first_level[key] = started[:-1]
        return started[-1]

    def finish_first_level(key, after):
        send_sems, recv_sems, local_sems, *bufs = first_level.pop(key)
        n = len(bufs) // 2
        bufs = gather_wait(send_sems, recv_sems, local_sems, bufs[:n], bufs[n:], after,
                           f"gather_wait_{key}")
        return bufs[n:]

    def weights_get(l, after):
        if l == 0:
            shards = layer_shards(0)
            mixer = all_gather(shards[:2], "gather_layer_0_mixer")
            token = start_first_level("0_mlp", shards[2:], [mixer[0], biases])
            if depth > 1:
                token = start_first_level(1, layer_shards(1), [token])
            return [g[:, None] for g in mixer], token[0, 0]
        send_sems, recv_sems, *lands = second_level.pop(l)
        gathered = forward_wait(send_sems, recv_sems, lands, after, f"gather_forward_wait_{l}")
        ready[l] = gathered[2:]
        return [g[:, None] for g in gathered[:2]], zero

    def weights_mid(l, after):
        if l == 0:
            ready[0] = forward_now(finish_first_level("0_mlp", after), "gather_forward_0_mlp")
        token = zero
        if l + 1 < depth:
            started = forward_start(finish_first_level(l + 1, after),
                                    f"gather_forward_start_{l + 1}")
            second_level[l + 1] = started[:-1]
            token = started[-1][0, 0]
            if l + 2 < depth:
                token = token + start_first_level(l + 2, layer_shards(l + 2),
                                                  [started[-1]])[0, 0]
        return [g[:, None] for g in ready.pop(l)], token

    exchanging = []
    scattering = []

    def finish_exchange(after):
        tag, names, layer_idx, (send_sems, recv_sems, *bufs) = exchanging.pop()
        n = len(names)
        grads, received = sibling_wait(send_sems, recv_sems, bufs[:n], bufs[n:], after,
                                       f"scatter_d2d_wait_{tag}")
        partials = [pair_add(g, r, core, f"pair_add_{k}")
                    for k, g, r in zip(names, grads, received)]
        started = scatter_start(partials, f"scatter_start_{tag}")
        scattering.append((tag, names, layer_idx, started[:-1]))
        return started[-1][0, 0]

    def grads_ready(l, group):
        names = [k for k, _, _ in group]
        grads = [g for _, _, g in group]
        tag = f"{l}_{names[0]}"
        token = finish_exchange(grads[0]) if exchanging else zero
        started = sibling_start(grads, f"scatter_d2d_start_{tag}")
        exchanging.append((tag, names, [li for _, li, _ in group], started[:-1]))
        return token + started[-1][0, 0]

    lbs, soft = lower_bounds_fwd(hgrn_lb_raw, "lower_bounds")

    dx, small = local_step(x[0], loss_target[0], norm_mix_g, norm_mlp_g, final_norm_g, lbs,
                           hgrn_norm_g, odd_sinks, bq_full, bo_full, weights_get, weights_mid,
                           grads_ready)
    finish_exchange(dx)

    parts = ([small["mix"][l] for l in range(depth)] + [small["mlp"][l] for l in range(depth)]
             + [small["final"]] + [small["lb"][e] for e in range(n_even)]
             + [small["gn"][e] for e in range(n_even)] + [small["sinks"][o] for o in range(n_odd)]
             + [small["bq"][o] for o in range(n_odd)] + [small["bo"][o] for o in range(n_odd)]
             + [small["loss"]])
    shapes = ([(depth, d)] * 2 + [(d,), hgrn_lb_raw.shape, hgrn_norm_g.shape, odd_sinks.shape,
              (n_odd, N_DEV * bq_w), (n_odd, N_DEV * bo_w), (1, LANES)])
    g_mix, g_mlp, g_final, d_lbs, g_gn, g_sinks, g_bq_full, g_bo_full, loss_v = _unpack(
        all_reduce_small(_pack(parts), "reduce_small"), shapes)
    g_lb = lower_bounds_bwd(soft, d_lbs, "lower_bounds_bwd")
    g_bq = lax.dynamic_slice(g_bq_full, (0, dev * bq_w), (n_odd, bq_w))
    g_bo = lax.dynamic_slice(g_bo_full, (0, dev * bo_w), (n_odd, bo_w))
    loss = loss_v[0, 0]

    small_names = ["norm_mix_g", "norm_mlp_g", "final_norm_g", "hgrn_lb_raw", "hgrn_norm_g",
                   "odd_b_qkv", "odd_sinks", "odd_b_o"]
    small_w = [norm_mix_g, norm_mlp_g, final_norm_g, hgrn_lb_raw, hgrn_norm_g, odd_b_qkv,
               odd_sinks, odd_b_o]
    small_m = [m_norm_mix_g, m_norm_mlp_g, m_final_norm_g, m_hgrn_lb_raw, m_hgrn_norm_g,
               m_odd_b_qkv, m_odd_sinks, m_odd_b_o]
    small_v = [v_norm_mix_g, v_norm_mlp_g, v_final_norm_g, v_hgrn_lb_raw, v_hgrn_norm_g,
               v_odd_b_qkv, v_odd_sinks, v_odd_b_o]
    small_g = [g_mix, g_mlp, g_final, g_lb, g_gn, g_bq, g_sinks, g_bo]
    sshapes = [w.shape for w in small_w]
    sd, sm, sv = adamw_small(_pack(small_w), _pack(small_g), _pack(small_m), _pack(small_v),
                             "adamw_small")
    res = {}
    for name, g, dl, m2, v2 in zip(small_names, small_g, _unpack(sd, sshapes),
                                   _unpack(sm, sshapes), _unpack(sv, sshapes)):
        res[name] = (g.reshape(dl.shape), dl, m2, v2)

    landed = {k: [None] * big_w[k].shape[0] for k in kinds}
    for tag, names, layer_idx, (send_sems, recv_sems, local_sems, *bufs) in scattering:
        n = len(names)
        lands = scatter_wait(send_sems, recv_sems, local_sems, bufs[:n], bufs[n:], dx,
                             f"scatter_wait_{tag}")
        for k, li, land in zip(names, layer_idx, lands):
            landed[k][li] = land
    long_names = {"win": "even_w_in", "wout": "even_w_out", "wqkv": "odd_w_qkv", "wo": "odd_w_o",
                  "w1": "mlp_w1", "w2": "mlp_w2"}
    for k in kinds:
        res[long_names[k]] = tuple(adamw_layers(landed[k], big_w[k], big_m[k], big_v[k],
                                                f"adamw_{k}"))

    order = ["norm_mix_g", "norm_mlp_g", "final_norm_g", "even_w_in", "even_w_out", "hgrn_lb_raw",
             "hgrn_norm_g", "odd_w_qkv", "odd_b_qkv", "odd_sinks", "odd_w_o", "odd_b_o", "mlp_w1",
             "mlp_w2"]
    outs = [loss, dx[None]]
    for j in range(4):
        outs += [res[n][j] for n in order]
    return tuple(outs)
```

```python
import functools
import math

import jax
import jax.numpy as jnp
from jax import lax
from jax.experimental import pallas as pl
from jax.experimental.pallas import tpu as pltpu

F32 = jnp.float32
BF16 = jnp.bfloat16
MESH = pl.DeviceIdType.MESH

N_DEV = 8
NORM_EPS = 1e-5
ROPE_THETA = 500000.0
BLK = 128
A_DIM = 128
A_BRANCHES = ((128, 1), (512, 4), (2048, 16))
B_DIM = 128
B_CHUNK = 64
C_DIM = 64
C_GROUP = 8
C_WINDOW = 128
LANES = 128

ADAM_LR = 0.001
ADAM_B1 = 0.9
ADAM_B2 = 0.999
ADAM_EPS = 1e-08
ADAM_WD = 0.01
ADAM_STEP = 10

NN = (((1,), (0,)), ((), ()))
NT = (((1,), (1,)), ((), ()))
TN = (((0,), (0,)), ((), ()))


def _params(*sem):
    return pltpu.CompilerParams(dimension_semantics=sem)


def _sigmoid(x):
    return 1.0 / (1.0 + jnp.exp(-x))


def _rows_call(name, body, row_ins, full_ins, row_outs, acc_outs, tm):
    t = row_ins[0].shape[0]
    n_ri, n_fi, n_ro = len(row_ins), len(full_ins), len(row_outs)

    def kern(*refs):
        i = pl.program_id(0)
        body(i, refs[:n_ri], refs[n_ri:n_ri + n_fi],
             refs[n_ri + n_fi:n_ri + n_fi + n_ro], refs[n_ri + n_fi + n_ro:])

    def row_spec(shape):
        return pl.BlockSpec((tm,) + tuple(shape[1:]), lambda i: (i,) + (0,) * (len(shape) - 1))

    def full_spec(shape):
        return pl.BlockSpec(tuple(shape), lambda i: (0,) * len(shape))

    outs = pl.pallas_call(
        kern, name=name, grid=(t // tm,),
        in_specs=[row_spec(a.shape) for a in row_ins] + [full_spec(a.shape) for a in full_ins],
        out_specs=[row_spec(s.shape) for s in row_outs] + [full_spec(s.shape) for s in acc_outs],
        out_shape=list(row_outs) + list(acc_outs),
        compiler_params=_params("arbitrary" if acc_outs else "parallel"),
    )(*row_ins, *full_ins)
    return outs


def _sds(shape, dtype):
    return jax.ShapeDtypeStruct(tuple(shape), dtype)


def rms_fwd(x, g, name):
    t, d = x.shape

    def body(i, ri, fi, ro, ao):
        xv = ri[0][...]
        r = lax.rsqrt(jnp.mean(xv * xv, axis=-1, keepdims=True) + NORM_EPS)
        ro[0][...] = (xv * r * fi[0][...]).astype(BF16)

    return _rows_call(name, body, [x], [g.reshape(1, d)], [_sds((t, d), BF16)], [], 256)[0]


def rms_bwd(x, g, dh, dx_res, name):
    t, d = x.shape

    def body(i, ri, fi, ro, ao):
        xv, dhv, res = ri[0][...], ri[1][...], ri[2][...]
        gv = fi[0][...]
        r = lax.rsqrt(jnp.mean(xv * xv, axis=-1, keepdims=True) + NORM_EPS)
        gd = gv * dhv
        dx = res + r * gd - xv * (r * r * r) * jnp.mean(xv * gd, axis=-1, keepdims=True)
        ro[0][...] = dx
        ro[1][...] = dx.astype(BF16)

        @pl.when(i == 0)
        def _():
            ao[0][...] = jnp.zeros_like(ao[0])
            ao[1][...] = jnp.zeros_like(ao[1])

        ao[0][...] += jnp.sum(dhv * xv * r, axis=0, keepdims=True)
        ao[1][...] += jnp.sum(dx, axis=0, keepdims=True)

    return _rows_call(name, body, [x, dh, dx_res], [g.reshape(1, d)],
                      [_sds((t, d), F32), _sds((t, d), BF16)],
                      [_sds((1, d), F32), _sds((1, d), F32)], 256)


def loss_head(x, g, target, name):
    t, d = x.shape

    def body(i, ri, fi, ro, ao):
        xv, tg = ri[0][...], ri[1][...]
        gv = fi[0][...]
        r = lax.rsqrt(jnp.mean(xv * xv, axis=-1, keepdims=True) + NORM_EPS)
        e = xv * r * gv - tg
        dy = e * (1.0 / d)
        gd = gv * dy
        dx = r * gd - xv * (r * r * r) * jnp.mean(xv * gd, axis=-1, keepdims=True)
        ro[0][...] = dx
        ro[1][...] = dx.astype(BF16)

        @pl.when(i == 0)
        def _():
            ao[0][...] = jnp.zeros_like(ao[0])
            ao[1][...] = jnp.zeros_like(ao[1])

        ao[0][...] += jnp.sum(dy * xv * r, axis=0, keepdims=True)
        part = 0.5 * jnp.sum(jnp.mean(e * e, axis=-1, keepdims=True), axis=0, keepdims=True)
        ao[1][...] += jnp.broadcast_to(part, (1, LANES))

    return _rows_call(name, body, [x, target], [g.reshape(1, d)],
                      [_sds((t, d), F32), _sds((t, d), BF16)],
                      [_sds((1, d), F32), _sds((1, LANES), F32)], 256)


def rope_tables(seq, head_dim):
    rot = head_dim // 4
    half = rot // 2
    inv_freq = 1.0 / (ROPE_THETA ** (jnp.arange(0, rot, 2, dtype=F32) / rot))
    ang = jnp.arange(seq, dtype=F32)[:, None] * inv_freq[None, :]
    cos, sin = jnp.cos(ang), jnp.sin(ang)
    zeros = jnp.zeros((seq, head_dim - rot), F32)
    zh = jnp.zeros((seq, half), F32)
    c = jnp.concatenate([cos, cos, jnp.ones((seq, head_dim - rot), F32)], axis=-1)
    sp = jnp.concatenate([zh, sin, zeros], axis=-1)
    sm = jnp.concatenate([-sin, zh, zeros], axis=-1)
    rep = LANES // head_dim
    return jnp.tile(c, (1, rep)), jnp.tile(sp, (1, rep)), jnp.tile(sm, (1, rep)), half


def rope_call(x, tabs, width, n_rope, inverse, name, col_sum=False):
    c, sp, sm, half = tabs
    t = x.shape[0]
    tm = 256
    n_slab = width // LANES

    def kern(x_ref, c_ref, sp_ref, sm_ref, o_ref, *acc):
        cv, spv, smv = c_ref[...], sp_ref[...], sm_ref[...]
        for j in range(n_slab):
            xs = x_ref[:, j * LANES:(j + 1) * LANES].astype(F32)
            if j < n_rope:
                if inverse:
                    ys = (xs * cv + pltpu.roll(xs * spv, LANES - half, 1)
                          + pltpu.roll(xs * smv, half, 1))
                else:
                    ys = (xs * cv + pltpu.roll(xs, half, 1) * spv
                          + pltpu.roll(xs, LANES - half, 1) * smv)
            else:
                ys = xs
            o_ref[:, j * LANES:(j + 1) * LANES] = ys.astype(BF16)
            if col_sum:
                @pl.when(pl.program_id(0) == 0)
                def _():
                    acc[0][:, j * LANES:(j + 1) * LANES] = jnp.zeros((1, LANES), F32)
                acc[0][:, j * LANES:(j + 1) * LANES] += jnp.sum(ys, axis=0, keepdims=True)

    tab_spec = pl.BlockSpec((tm, LANES), lambda i: (i, 0))
    out_shape = [_sds((t, width), BF16)]
    out_specs = [pl.BlockSpec((tm, width), lambda i: (i, 0))]
    if col_sum:
        out_shape.append(_sds((1, width), F32))
        out_specs.append(pl.BlockSpec((1, width), lambda i: (0, 0)))
    return pl.pallas_call(
        kern, name=name, grid=(t // tm,),
        in_specs=[pl.BlockSpec((tm, width), lambda i: (i, 0)), tab_spec, tab_spec, tab_spec],
        out_specs=out_specs, out_shape=out_shape,
        compiler_params=_params("arbitrary" if col_sum else "parallel"),
    )(x, c, sp, sm)


def _mm_call(name, a, b, extras, out_shapes, grid, a_spec, b_spec, extra_specs, out_specs,
             acc_shape, dims, epilogue):
    n_ex, n_out = len(extras), len(out_shapes)
    nk = grid[2]

    def product(a_ref, b_ref):
        bv = b_ref[...]
        if bv.ndim == 3:
            bv = bv.reshape(bv.shape[0] * bv.shape[1], bv.shape[2])
        return lax.dot_general(a_ref[...].astype(BF16), bv.astype(BF16), dims,
                               preferred_element_type=F32)

    def kern(*refs):
        a_ref, b_ref = refs[0], refs[1]
        ex = refs[2:2 + n_ex]
        outs = refs[2 + n_ex:2 + n_ex + n_out]
        if nk == 1:
            epilogue(product(a_ref, b_ref), ex, outs)
            return
        acc = refs[-1]
        k = pl.program_id(2)

        @pl.when(k == 0)
        def _():
            acc[...] = product(a_ref, b_ref)

        @pl.when(k > 0)
        def _():
            acc[...] += product(a_ref, b_ref)

        @pl.when(k == nk - 1)
        def _():
            epilogue(acc[...], ex, outs)

    return pl.pallas_call(
        kern, name=name, grid=grid,
        in_specs=[a_spec, b_spec, *extra_specs], out_specs=out_specs, out_shape=out_shapes,
        scratch_shapes=[pltpu.VMEM(acc_shape, F32)] if nk > 1 else [],
        compiler_params=_params("parallel", "parallel", "arbitrary"),
    )(a, b, *extras)


def _ep_store(dtype):
    def ep(acc, ex, outs):
        outs[0][...] = acc.astype(dtype)
    return ep


def _ep_residual(acc, ex, outs):
    outs[0][...] = acc + ex[0][...]


def _ep_bias(acc, ex, outs):
    outs[0][...] = acc + ex[0][...]


def _ep_bias_residual(acc, ex, outs):
    outs[0][...] = acc + ex[0][...] + ex[1][...]


def _ep_relu2(acc, ex, outs):
    outs[0][...] = acc
    rl = jnp.maximum(acc, 0.0)
    outs[1][...] = (rl * rl).astype(BF16)


def _ep_relu2_bwd(acc, ex, outs):
    outs[0][...] = (acc * (2.0 * jnp.maximum(ex[0][...], 0.0))).astype(BF16)


MM_TM = 1024
MM_TN = 1024
MM_TK = 2048


def mm_cols_sharded(a, wg, layer, name, epilogue=None, n_out=1):
    m, kdim = a.shape
    n = wg.shape[-1]
    tm, tk = min(m, MM_TM), min(kdim, MM_TK)
    if epilogue is None:
        epilogue, outs = _ep_store(F32), [_sds((m, N_DEV * n), F32)]
    else:
        outs = [_sds((m, N_DEV * n), F32), _sds((m, N_DEV * n), BF16)][:n_out]
    return _mm_call(
        name, a, wg, [], outs, (m // tm, N_DEV, kdim // tk),
        pl.BlockSpec((tm, tk), lambda i, j, k: (i, k)),
        pl.BlockSpec((None, None, tk, n), lambda i, j, k: (j, layer, k, 0)),
        [], [pl.BlockSpec((tm, n), lambda i, j, k: (i, j))] * len(outs),
        (tm, n), NN, epilogue)


def _extra_specs(extra_kinds, tm, tn):
    specs = []
    for kind in extra_kinds:
        if kind == "row":
            specs.append(pl.BlockSpec((1, tn), lambda i, j, k: (0, j)))
        else:
            specs.append(pl.BlockSpec((tm, tn), lambda i, j, k: (i, j)))
    return specs


def mm_rows_sharded(a, wg, layer, name, extras, extra_kinds, epilogue):
    m, kdim = a.shape
    ks, n = wg.shape[-2], wg.shape[-1]
    tm, tn = min(m, MM_TM), min(n, MM_TN)
    gps = max(1, min(kdim, MM_TK) // ks)
    return _mm_call(
        name, a, wg, extras, [_sds((m, n), F32)], (m // tm, n // tn, N_DEV // gps),
        pl.BlockSpec((tm, gps * ks), lambda i, j, k: (i, k)),
        pl.BlockSpec((gps, None, ks, tn), lambda i, j, k: (k, layer, 0, j)),
        _extra_specs(extra_kinds, tm, tn), [pl.BlockSpec((tm, tn), lambda i, j, k: (i, j))],
        (tm, tn), NN, epilogue)[0]


def mm_plain(a, w, name, extras, extra_kinds, epilogue, tn=512):
    m, kdim = a.shape
    n = w.shape[1]
    tm, tk = min(m, MM_TM), min(kdim, MM_TK)
    return _mm_call(
        name, a, w, extras, [_sds((m, n), F32)], (m // tm, n // tn, kdim // tk),
        pl.BlockSpec((tm, tk), lambda i, j, k: (i, k)),
        pl.BlockSpec((tk, tn), lambda i, j, k: (k, j)),
        _extra_specs(extra_kinds, tm, tn), [pl.BlockSpec((tm, tn), lambda i, j, k: (i, j))],
        (tm, tn), NN, epilogue)[0]


def mm_nt_cols_sharded(dy, wg, layer, name):
    m = dy.shape[0]
    kdim, n = wg.shape[-2], wg.shape[-1]
    tm, tn = min(m, MM_TM), min(kdim, MM_TN)
    return _mm_call(
        name, dy, wg, [], [_sds((m, kdim), F32)], (m // tm, kdim // tn, N_DEV),
        pl.BlockSpec((tm, n), lambda i, j, k: (i, k)),
        pl.BlockSpec((None, None, tn, n), lambda i, j, k: (k, layer, j, 0)),
        [], [pl.BlockSpec((tm, tn), lambda i, j, k: (i, j))],
        (tm, tn), NT, _ep_store(F32))[0]


def mm_nt_rows_sharded(dy, wg, layer, name, extras=(), epilogue=None, out_dtype=F32):
    m, n = dy.shape
    ks = wg.shape[-2]
    tm, tk = min(m, MM_TM), min(n, MM_TK)
    gps = max(1, MM_TN // ks)
    tn = gps * ks
    epilogue = _ep_store(out_dtype) if epilogue is None else epilogue
    return _mm_call(
        name, dy, wg, list(extras), [_sds((m, N_DEV * ks), out_dtype)],
        (m // tm, N_DEV // gps, n // tk),
        pl.BlockSpec((tm, tk), lambda i, j, k: (i, k)),
        pl.BlockSpec((gps, None, ks, tk), lambda i, j, k: (j, layer, 0, k)),
        [pl.BlockSpec((tm, tn), lambda i, j, k: (i, j))] * len(extras),
        [pl.BlockSpec((tm, tn), lambda i, j, k: (i, j))],
        (tm, tn), NT, epilogue)[0]


def mm_nt_plain(dy, w, name, tk):
    m, n = dy.shape
    kdim = w.shape[0]
    tm, tn = min(m, MM_TM), min(kdim, MM_TN)
    return _mm_call(
        name, dy, w, [], [_sds((m, kdim), F32)], (m // tm, kdim // tn, n // tk),
        pl.BlockSpec((tm, tk), lambda i, j, k: (i, k)),
        pl.BlockSpec((tn, tk), lambda i, j, k: (j, k)),
        [], [pl.BlockSpec((tm, tn), lambda i, j, k: (i, j))],
        (tm, tn), NT, _ep_store(F32))[0]


def mm_tn(a, dy, name, shard_cols=None, tn=MM_TN):
    t, kdim = a.shape
    n = dy.shape[1]
    tm, tk = min(kdim, MM_TM), min(t, MM_TK)
    if shard_cols is None:
        tn = min(tn, n)
        out = _sds((kdim, n), BF16)
        o_spec = pl.BlockSpec((tm, tn), lambda i, j, k: (i, j))
    else:
        tn = shard_cols
        out = _sds((n // tn, kdim, tn), BF16)
        o_spec = pl.BlockSpec((None, tm, tn), lambda i, j, k: (j, i, 0))
    return _mm_call(
        name, a, dy, [], [out], (kdim // tm, n // tn, t // tk),
        pl.BlockSpec((tk, tm), lambda i, j, k: (k, i)),
        pl.BlockSpec((tk, tn), lambda i, j, k: (k, j)),
        [], [o_spec], (tm, tn), TN, _ep_store(BF16))[0]


BAND_BLOCKS_PER_STEP = 4


def _band_mask(g, nk_prev_valid, max_dist):
    rows = lax.broadcasted_iota(jnp.int32, (g * BLK, 2 * BLK), 0) % BLK
    cols = lax.broadcasted_iota(jnp.int32, (g * BLK, 2 * BLK), 1)
    dist = rows + BLK - cols
    ok = (dist >= 0) & (dist <= max_dist)
    return ok & ((cols >= BLK) | nk_prev_valid)


def band_fwd(qkv, q0, k0, v0, hk, g, seg, max_dist, name, sink_rows=None, normalise=False):
    t, dh = qkv.shape[1], qkv.shape[2]
    nb = t // BLK
    rb = BAND_BLOCKS_PER_STEP // g if g < BAND_BLOCKS_PER_STEP else 1
    rows = rb * BLK
    scale = dh ** -0.5
    has_sink = sink_rows is not None

    def kern(*refs):
        if has_sink:
            q_ref, k_ref, v_ref, s_ref, num_ref, m_ref, *l_ref = refs
            sink = s_ref[...]
        else:
            q_ref, k_ref, v_ref, num_ref, m_ref, *l_ref = refs
        for r in range(rb):
            b = pl.program_id(1) * rb + r
            cur = pl.multiple_of(b * BLK, BLK)
            prev = pl.multiple_of(jnp.maximum(b - 1, 0) * BLK, BLK)
            here = slice(r * BLK, (r + 1) * BLK)
            q = q_ref[:, here, :].reshape(g * BLK, dh)
            kk = jnp.concatenate([k_ref[pl.ds(prev, BLK), :], k_ref[pl.ds(cur, BLK), :]], axis=0)
            vv = jnp.concatenate([v_ref[pl.ds(prev, BLK), :], v_ref[pl.ds(cur, BLK), :]], axis=0)
            s = lax.dot_general(q, kk, NT, preferred_element_type=F32) * scale
            s = jnp.where(_band_mask(g, (b % seg) != 0, max_dist), s, -jnp.inf)
            m = jnp.max(s, axis=-1, keepdims=True)
            if has_sink:
                m = jnp.maximum(m, sink)
            p = jnp.exp(s - m)
            l = jnp.sum(p, axis=-1, keepdims=True)
            if has_sink:
                l = l + jnp.exp(sink - m)
            num = jnp.dot(p.astype(BF16), vv, preferred_element_type=F32)
            if normalise:
                num_ref[:, here, :] = (num * (1.0 / l)).reshape(g, BLK, dh)
                m_ref[:, here, :] = (m + jnp.log(l)).reshape(g, BLK, 1)
            else:
                num_ref[:, here, :] = num.reshape(g, BLK, dh)
                m_ref[:, here, :] = m.reshape(g, BLK, 1)
                l_ref[0][:, here, :] = l.reshape(g, BLK, 1)

    in_specs = [pl.BlockSpec((g, rows, dh), lambda h, b: (q0 // g + h, b, 0)),
                pl.BlockSpec((None, t, dh), lambda h, b: (k0 + h, 0, 0)),
                pl.BlockSpec((None, t, dh), lambda h, b: (v0 + h, 0, 0))]
    args = [qkv, qkv, qkv]
    if has_sink:
        in_specs.append(pl.BlockSpec((None, g * BLK, 1), lambda h, b: (h, 0, 0)))
        args.append(sink_rows)
    hq = hk * g
    n_col = 1 if normalise else 2
    return pl.pallas_call(
        kern, name=name, grid=(hk, nb // rb), in_specs=in_specs,
        out_specs=[pl.BlockSpec((g, rows, dh), lambda h, b: (h, b, 0))]
        + [pl.BlockSpec((g, rows, 1), lambda h, b: (h, b, 0))] * n_col,
        out_shape=[_sds((hq, t, dh), F32)] + [_sds((hq, t, 1), F32)] * n_col,
        compiler_params=_params("parallel", "parallel"),
    )(*args)


def band_bwd(qkv, q0, k0, v0, do, lse, delta, hk, g, seg, max_dist, name, sink_rows=None,
             delta_from_o=False):
    t, dh = qkv.shape[1], qkv.shape[2]
    nb = t // BLK
    rb = BAND_BLOCKS_PER_STEP // g if g < BAND_BLOCKS_PER_STEP else 1
    scale = dh ** -0.5
    has_sink = sink_rows is not None

    def kern(*refs):
        if has_sink:
            (q_ref, k_ref, v_ref, do_ref, lse_ref, dl_ref, s_ref,
             dq_ref, dk_ref, dv_ref, ds_ref, sacc) = refs
            sink = s_ref[...]
        else:
            q_ref, k_ref, v_ref, do_ref, lse_ref, dl_ref, dq_ref, dk_ref, dv_ref = refs
        step = pl.program_id(1)

        @pl.when(step == 0)
        def _():
            dk_ref[...] = jnp.zeros_like(dk_ref)
            dv_ref[...] = jnp.zeros_like(dv_ref)
            if has_sink:
                sacc[...] = jnp.zeros_like(sacc)

        for r in range(rb):
            b = step * rb + r
            cur = pl.multiple_of(b * BLK, BLK)
            prev = pl.multiple_of(jnp.maximum(b - 1, 0) * BLK, BLK)
            here = slice(r * BLK, (r + 1) * BLK)
            q = q_ref[:, here, :].reshape(g * BLK, dh)
            dout = do_ref[:, here, :].reshape(g * BLK, dh)
            lse_b = lse_ref[:, here, :].reshape(g * BLK, 1)
            if delta_from_o:
                dl_b = jnp.sum(dl_ref[:, here, :].reshape(g * BLK, dh) * dout, axis=-1,
                               keepdims=True)
                dout = dout.astype(BF16)
            else:
                dl_b = dl_ref[:, here, :].reshape(g * BLK, 1)
            kk = jnp.concatenate([k_ref[pl.ds(prev, BLK), :], k_ref[pl.ds(cur, BLK), :]], axis=0)
            vv = jnp.concatenate([v_ref[pl.ds(prev, BLK), :], v_ref[pl.ds(cur, BLK), :]], axis=0)
            s = lax.dot_general(q, kk, NT, preferred_element_type=F32) * scale
            s = jnp.where(_band_mask(g, (b % seg) != 0, max_dist), s, -jnp.inf)
            p = jnp.exp(s - lse_b)
            dp = lax.dot_general(dout, vv, NT, preferred_element_type=F32)
            ds = (p * (dp - dl_b) * scale).astype(BF16)
            dq = jnp.dot(ds, kk, preferred_element_type=F32)
            dq_ref[:, here, :] = dq.reshape(g, BLK, dh)
            dkk = lax.dot_general(ds, q, TN, preferred_element_type=F32)
            dvv = lax.dot_general(p.astype(BF16), dout, TN, preferred_element_type=F32)
            dk_ref[pl.ds(prev, BLK), :] += dkk[:BLK]
            dk_ref[pl.ds(cur, BLK), :] += dkk[BLK:]
            dv_ref[pl.ds(prev, BLK), :] += dvv[:BLK]
            dv_ref[pl.ds(cur, BLK), :] += dvv[BLK:]
            if has_sink:
                sacc[...] += -jnp.exp(sink - lse_b) * dl_b

        if has_sink:
            @pl.when(step == nb // rb - 1)
            def _():
                for gi in range(g):
                    ds_ref[gi:gi + 1, :] = jnp.sum(sacc[gi * BLK:(gi + 1) * BLK, :], axis=0,
                                                   keepdims=True)

    rows = rb * BLK
    in_specs = [pl.BlockSpec((g, rows, dh), lambda h, b: (q0 // g + h, b, 0)),
                pl.BlockSpec((None, t, dh), lambda h, b: (k0 + h, 0, 0)),
                pl.BlockSpec((None, t, dh), lambda h, b: (v0 + h, 0, 0)),
                pl.BlockSpec((g, rows, dh), lambda h, b: (h, b, 0)),
                pl.BlockSpec((g, rows, 1), lambda h, b: (h, b, 0)),
                pl.BlockSpec((g, rows, dh if delta_from_o else 1), lambda h, b: (h, b, 0))]
    args = [qkv, qkv, qkv, do, lse, delta]
    hq = hk * g
    out_specs = [pl.BlockSpec((g, rows, dh), lambda h, b: (h, b, 0)),
                 pl.BlockSpec((None, t, dh), lambda h, b: (h, 0, 0)),
                 pl.BlockSpec((None, t, dh), lambda h, b: (h, 0, 0))]
    out_shape = [_sds((hq, t, dh), F32), _sds((hk, t, dh), F32), _sds((hk, t, dh), F32)]
    scratch = []
    if has_sink:
        in_specs.append(pl.BlockSpec((None, g * BLK, 1), lambda h, b: (h, 0, 0)))
        args.append(sink_rows)
        out_specs.append(pl.BlockSpec((None, g, 1), lambda h, b: (h, 0, 0)))
        out_shape.append(_sds((hk, g, 1), F32))
        scratch.append(pltpu.VMEM((g * BLK, 1), F32))
    return pl.pallas_call(
        kern, name=name, grid=(hk, nb // rb), in_specs=in_specs, out_specs=out_specs,
        out_shape=out_shape,
        scratch_shapes=scratch, compiler_params=_params("parallel", "arbitrary"),
    )(*args)


def merge_branches(nums, ms, ls, name):
    h, t, dh = nums[0].shape
    nbr = len(nums)

    def kern(*refs):
        num_refs, m_refs, l_refs = refs[:nbr], refs[nbr:2 * nbr], refs[2 * nbr:3 * nbr]
        o_ref, lse_ref = refs[3 * nbr], refs[3 * nbr + 1]
        mall = m_refs[0][...]
        for i in range(1, nbr):
            mall = jnp.maximum(mall, m_refs[i][...])
        num = jnp.zeros((t, dh), F32)
        den = jnp.zeros((t, 1), F32)
        for i in range(nbr):
            w = jnp.exp(m_refs[i][...] - mall)
            num = num + w * num_refs[i][...]
            den = den + w * l_refs[i][...]
        o_ref[...] = num / den
        lse_ref[...] = mall + jnp.log(den)

    big = pl.BlockSpec((None, t, dh), lambda i: (i, 0, 0))
    col = pl.BlockSpec((None, t, 1), lambda i: (i, 0, 0))
    return pl.pallas_call(
        kern, name=name, grid=(h,), in_specs=[big] * nbr + [col] * (2 * nbr),
        out_specs=[big, col], out_shape=[_sds((h, t, dh), F32), _sds((h, t, 1), F32)],
        compiler_params=_params("parallel"),
    )(*nums, *ms, *ls)


def normalise_heads(num, m, l, name):
    h, t, dh = num.shape

    def kern(num_ref, m_ref, l_ref, o_ref, lse_ref):
        lv = l_ref[...]
        o_ref[...] = num_ref[...] / lv
        lse_ref[...] = m_ref[...] + jnp.log(lv)

    big = pl.BlockSpec((None, t, dh), lambda i: (i, 0, 0))
    col = pl.BlockSpec((None, t, 1), lambda i: (i, 0, 0))
    return pl.pallas_call(
        kern, name=name, grid=(h,), in_specs=[big, col, col], out_specs=[big, col],
        out_shape=[_sds((h, t, dh), F32), _sds((h, t, 1), F32)],
        compiler_params=_params("parallel"),
    )(num, m, l)


def head_delta(o, do, name):
    h, t, dh = o.shape

    def kern(o_ref, do_ref, d_ref):
        d_ref[...] = jnp.sum(o_ref[...] * do_ref[...], axis=-1, keepdims=True)

    big = pl.BlockSpec((None, t, dh), lambda i: (i, 0, 0))
    return pl.pallas_call(
        kern, name=name, grid=(h,), in_specs=[big, big],
        out_specs=pl.BlockSpec((None, t, 1), lambda i: (i, 0, 0)),
        out_shape=_sds((h, t, 1), F32), compiler_params=_params("parallel"),
    )(o, do)


def _dil_rb(nbl):
    return min(BAND_BLOCKS_PER_STEP, nbl)


def dilated_fwd(qkv, na, dil, max_dist, name):
    t, w3 = qkv.shape
    dh = A_DIM
    seq = t // dil
    nbl = seq // BLK
    rb = _dil_rb(nbl)
    rows = rb * BLK
    cb = w3 // dh
    scale = dh ** -0.5
    view = qkv.reshape(seq, dil * w3)

    def kern(q_ref, k_ref, v_ref, num_ref, m_ref, l_ref):
        for r in range(rb):
            b = pl.program_id(1) * rb + r
            cur = pl.multiple_of(b * BLK, BLK)
            prev = pl.multiple_of(jnp.maximum(b - 1, 0) * BLK, BLK)
            here = slice(r * BLK, (r + 1) * BLK)
            kk = jnp.concatenate([k_ref[pl.ds(prev, BLK), :], k_ref[pl.ds(cur, BLK), :]], axis=0)
            vv = jnp.concatenate([v_ref[pl.ds(prev, BLK), :], v_ref[pl.ds(cur, BLK), :]], axis=0)
            s = lax.dot_general(q_ref[here, :], kk, NT, preferred_element_type=F32) * scale
            s = jnp.where(_band_mask(1, b != 0, max_dist), s, -jnp.inf)
            m = jnp.max(s, axis=-1, keepdims=True)
            p = jnp.exp(s - m)
            l = jnp.sum(p, axis=-1, keepdims=True)
            num_ref[here, :] = jnp.dot(p.astype(BF16), vv, preferred_element_type=F32)
            m_ref[here, :] = jnp.broadcast_to(m, (BLK, dh))
            l_ref[here, :] = jnp.broadcast_to(l, (BLK, dh))

    def col(off):
        return lambda p, b: (0, (p // na) * cb + off * na + p % na)

    out_spec = pl.BlockSpec((rows, dh), lambda p, b: (b, p))
    out = _sds((seq, dil * na * dh), F32)
    outs = pl.pallas_call(
        kern, name=name, grid=(dil * na, nbl // rb),
        in_specs=[pl.BlockSpec((rows, dh), lambda p, b: (b, (p // na) * cb + p % na)),
                  pl.BlockSpec((seq, dh), col(1)), pl.BlockSpec((seq, dh), col(2))],
        out_specs=[out_spec] * 3, out_shape=[out] * 3,
        compiler_params=_params("parallel", "parallel"),
    )(view, view, view)
    return [o.reshape(t, na * dh) for o in outs]


def dilated_merge(nums, ms, ls, name):
    t, w = nums[0].shape
    nbr = len(nums)

    def body(i, ri, fi, ro, ao):
        mall = ri[nbr][...]
        for j in range(1, nbr):
            mall = jnp.maximum(mall, ri[nbr + j][...])
        num = jnp.zeros(mall.shape, F32)
        den = jnp.zeros(mall.shape, F32)
        for j in range(nbr):
            wgt = jnp.exp(ri[nbr + j][...] - mall)
            num = num + wgt * ri[j][...]
            den = den + wgt * ri[2 * nbr + j][...]
        o = num / den
        ro[0][...] = o
        ro[1][...] = o.astype(BF16)
        ro[2][...] = mall + jnp.log(den)

    return _rows_call(name, body, list(nums) + list(ms) + list(ls), [],
                      [_sds((t, w), F32), _sds((t, w), BF16), _sds((t, w), F32)], [], 256)


def dilated_delta(o, dmixed, name):
    t, w = o.shape

    def body(i, ri, fi, ro, ao):
        for j in range(w // A_DIM):
            cols = slice(j * A_DIM, (j + 1) * A_DIM)
            d = jnp.sum(ri[0][:, cols] * ri[1][:, cols], axis=-1, keepdims=True)
            ro[0][:, cols] = jnp.broadcast_to(d, (d.shape[0], A_DIM))

    return _rows_call(name, body, [o, dmixed], [], [_sds((t, w), F32)], [], 256)[0]


def dilated_bwd(qkv, dmixed, lse, delta, na, dil, max_dist, name):
    t, w3 = qkv.shape
    dh = A_DIM
    seq = t // dil
    nbl = seq // BLK
    rb = _dil_rb(nbl)
    rows = rb * BLK
    cb = w3 // dh
    db = dmixed.shape[1] // dh
    scale = dh ** -0.5
    view = qkv.reshape(seq, dil * w3)
    do_view = dmixed.reshape(seq, dil * dmixed.shape[1])
    lse_view = lse.reshape(seq, dil * na * dh)
    delta_view = delta.reshape(seq, dil * na * dh)

    def kern(q_ref, k_ref, v_ref, do_ref, lse_ref, dl_ref, dq_ref, dk_ref, dv_ref):
        step = pl.program_id(1)

        @pl.when(step == 0)
        def _():
            dk_ref[...] = jnp.zeros_like(dk_ref)
            dv_ref[...] = jnp.zeros_like(dv_ref)

        for r in range(rb):
            b = step * rb + r
            cur = pl.multiple_of(b * BLK, BLK)
            prev = pl.multiple_of(jnp.maximum(b - 1, 0) * BLK, BLK)
            here = slice(r * BLK, (r + 1) * BLK)
            q = q_ref[here, :]
            dout = do_ref[here, :].astype(BF16)
            kk = jnp.concatenate([k_ref[pl.ds(prev, BLK), :], k_ref[pl.ds(cur, BLK), :]], axis=0)
            vv = jnp.concatenate([v_ref[pl.ds(prev, BLK), :], v_ref[pl.ds(cur, BLK), :]], axis=0)
            s = lax.dot_general(q, kk, NT, preferred_element_type=F32) * scale
            s = jnp.where(_band_mask(1, b != 0, max_dist), s, -jnp.inf)
            p = jnp.exp(s - lse_ref[here, 0:1])
            dp = lax.dot_general(dout, vv, NT, preferred_element_type=F32)
            ds = (p * (dp - dl_ref[here, 0:1]) * scale).astype(BF16)
            dq_ref[here, :] = jnp.dot(ds, kk, preferred_element_type=F32)
            dkk = lax.dot_general(ds, q, TN, preferred_element_type=F32)
            dvv = lax.dot_general(p.astype(BF16), dout, TN, preferred_element_type=F32)
            dk_ref[pl.ds(prev, BLK), :] += dkk[:BLK]
            dk_ref[pl.ds(cur, BLK), :] += dkk[BLK:]
            dv_ref[pl.ds(prev, BLK), :] += dvv[:BLK]
            dv_ref[pl.ds(cur, BLK), :] += dvv[BLK:]

    def col(off):
        return lambda p, b: (0, (p // na) * cb + off * na + p % na)

    blk = pl.BlockSpec((rows, dh), lambda p, b: (b, p))
    whole = pl.BlockSpec((seq, dh), lambda p, b: (0, p))
    out = _sds((seq, dil * na * dh), F32)
    outs = pl.pallas_call(
        kern, name=name, grid=(dil * na, nbl // rb),
        in_specs=[pl.BlockSpec((rows, dh), lambda p, b: (b, (p // na) * cb + p % na)),
                  pl.BlockSpec((seq, dh), col(1)), pl.BlockSpec((seq, dh), col(2)),
                  pl.BlockSpec((rows, dh), lambda p, b: (b, (p // na) * db + p % na)), blk, blk],
        out_specs=[blk, whole, whole], out_shape=[out] * 3,
        compiler_params=_params("parallel", "arbitrary"),
    )(view, view, view, do_view, lse_view, delta_view)
    return [o.reshape(t, na * dh) for o in outs]


def rope_bwd_sum(dqs, dks, dvs, tabs, name):
    c, sp, sm, half = tabs
    t, w = dqs[0].shape
    nbr = len(dqs)
    tm = 256
    n_slab = w // LANES

    def kern(*refs):
        groups = [refs[:nbr], refs[nbr:2 * nbr], refs[2 * nbr:3 * nbr]]
        c_ref, sp_ref, sm_ref, o_ref = refs[3 * nbr:]
        cv, spv, smv = c_ref[...], sp_ref[...], sm_ref[...]
        for gi, group in enumerate(groups):
            for j in range(n_slab):
                cols = slice(j * LANES, (j + 1) * LANES)
                xs = group[0][:, cols]
                for ref in group[1:]:
                    xs = xs + ref[:, cols]
                if gi < 2:
                    xs = (xs * cv + pltpu.roll(xs * spv, LANES - half, 1)
                          + pltpu.roll(xs * smv, half, 1))
                o_ref[:, gi * w + j * LANES:gi * w + (j + 1) * LANES] = xs.astype(BF16)

    big = pl.BlockSpec((tm, w), lambda i: (i, 0))
    tab = pl.BlockSpec((tm, LANES), lambda i: (i, 0))
    return pl.pallas_call(
        kern, name=name, grid=(t // tm,), in_specs=[big] * (3 * nbr) + [tab] * 3,
        out_specs=pl.BlockSpec((tm, 3 * w), lambda i: (i, 0)), out_shape=_sds((t, 3 * w), BF16),
        compiler_params=_params("parallel"),
    )(*dqs, *dks, *dvs, c, sp, sm)


def _cumsum_rows(x, n, reverse=False):
    rows = lax.broadcasted_iota(jnp.int32, x.shape, 0)
    shift = 1
    while shift < n:
        if reverse:
            x = x + jnp.where(rows < n - shift, pltpu.roll(x, n - shift, 0), 0.0)
        else:
            x = x + jnp.where(rows >= shift, pltpu.roll(x, shift, 0), 0.0)
        shift *= 2
    return x


def _hgrn_gates(f, lb):
    sig = _sigmoid(f)
    gate = lb + (1.0 - lb) * sig
    return sig, gate


B_SUB = 16


def _dot3(a, b, dims):
    ah, bh = a.astype(BF16), b.astype(BF16)
    al = (a - ah.astype(F32)).astype(BF16)
    bl = (b - bh.astype(F32)).astype(BF16)
    dot = functools.partial(lax.dot_general, dimension_numbers=dims, preferred_element_type=F32)
    return dot(ah, bh) + dot(al, bh) + dot(ah, bl)


def _sub_scales(b, i):
    r0 = i * B_SUB
    beta = b[r0 - 1:r0, :]
    return jnp.exp(b[r0:r0 + B_SUB, :] - beta), jnp.exp(jnp.minimum(beta - b, 0.0))


def _hgrn_intra_attn(qq, kk, b):
    c = qq.shape[0]
    lane = lax.broadcasted_iota(jnp.int32, (B_SUB, c), 1)
    trow = lax.broadcasted_iota(jnp.int32, (B_SUB, B_DIM), 0)
    blocks = []
    for i in range(c // B_SUB):
        r0 = i * B_SUB
        qi, bi = qq[r0:r0 + B_SUB, :], b[r0:r0 + B_SUB, :]
        if i == 0:
            a_i = jnp.zeros((B_SUB, c), F32)
        else:
            eq, ek = _sub_scales(b, i)
            a_i = jnp.where(lane < r0, _dot3(qi * eq, kk * ek, NT), 0.0)
        for sl in range(B_SUB):
            s = r0 + sl
            e = jnp.exp(jnp.where(trow >= sl, bi - b[s:s + 1, :], -jnp.inf))
            col = jnp.sum(qi * kk[s:s + 1, :] * e, axis=1, keepdims=True)
            a_i = jnp.where(lane == s, col, a_i)
        blocks.append(a_i)
    return jnp.concatenate(blocks, axis=0)


def hgrn_fwd(proj, col0, nh, lb, gn, name):
    t = proj.shape[0]
    c = B_CHUNK
    nc = t // c
    scale = B_DIM ** -0.5

    def kern(q_ref, f_ref, i_ref, g_ref, lb_ref, gn_ref, out_ref, opre_ref, st_ref, a_ref, state):
        lbv = lb_ref[...]
        gnv = gn_ref[...]
        state[...] = jnp.zeros_like(state)

        def chunk(ci, carry):
            rows = pl.ds(pl.multiple_of(ci * c, c), c)
            _, gate = _hgrn_gates(f_ref[rows, :], lbv)
            kk = 1.0 - gate
            qb = q_ref[rows, :]
            qq = qb * _sigmoid(qb) * scale
            v = i_ref[rows, :]
            b = _cumsum_rows(jnp.log(gate), c)
            st = state[...]
            st_ref[ci] = st
            o_inter = lax.dot_general((qq * jnp.exp(b)).astype(BF16), st.astype(BF16), NT,
                                      preferred_element_type=F32)
            amat = _hgrn_intra_attn(qq, kk, b)
            a_ref[ci] = amat
            o = jnp.dot(amat.astype(BF16), v.astype(BF16), preferred_element_type=F32) + o_inter
            opre_ref[rows, :] = o
            bl = b[c - 1:c, :]
            state[...] = st * jnp.exp(bl) + lax.dot_general(
                v.astype(BF16), (kk * jnp.exp(bl - b)).astype(BF16), TN, preferred_element_type=F32)
            r = lax.rsqrt(jnp.mean(o * o, axis=-1, keepdims=True) + NORM_EPS)
            gb = g_ref[rows, :]
            out_ref[rows, :] = (o * r * gnv * (gb * _sigmoid(gb))).astype(BF16)
            return carry

        lax.fori_loop(0, nc, chunk, 0)

    def col(off):
        return pl.BlockSpec((t, B_DIM), lambda h: (0, col0 + off * nh + h))

    return pl.pallas_call(
        kern, name=name, grid=(nh,),
        in_specs=[col(0), col(1), col(2), col(3),
                  pl.BlockSpec((None, 1, B_DIM), lambda h: (h, 0, 0)),
                  pl.BlockSpec((1, B_DIM), lambda h: (0, 0))],
        out_specs=[pl.BlockSpec((t, B_DIM), lambda h: (0, h)),
                   pl.BlockSpec((t, B_DIM), lambda h: (0, h)),
                   pl.BlockSpec((None, nc, B_DIM, B_DIM), lambda h: (h, 0, 0, 0)),
                   pl.BlockSpec((None, nc, c, c), lambda h: (h, 0, 0, 0))],
        out_shape=[_sds((t, nh * B_DIM), BF16), _sds((t, nh * B_DIM), F32),
                   _sds((nh, nc, B_DIM, B_DIM), F32), _sds((nh, nc, c, c), F32)],
        scratch_shapes=[pltpu.VMEM((B_DIM, B_DIM), F32)],
        compiler_params=_params("parallel"),
    )(proj, proj, proj, proj, lb, gn)


def hgrn_bwd(proj, col0, nh, lb, gn, opre, states, amats, dout, dcol0, name):
    t = proj.shape[0]
    c = B_CHUNK
    nc = t // c
    scale = B_DIM ** -0.5
    nsub = c // B_SUB

    def kern(q_ref, f_ref, i_ref, g_ref, lb_ref, gn_ref, opre_ref, st_ref, a_ref, dout_ref,
             dq_ref, df_ref, di_ref, dg_ref, dgn_ref, dlb_ref, dstate, dksc):
        lbv = lb_ref[...]
        gnv = gn_ref[...]
        dstate[...] = jnp.zeros_like(dstate)
        dlb_ref[...] = jnp.zeros_like(dlb_ref)

        @pl.when(pl.program_id(0) == 0)
        def _():
            dgn_ref[...] = jnp.zeros_like(dgn_ref)

        srow = lax.broadcasted_iota(jnp.int32, (c, B_DIM), 0)
        lane = lax.broadcasted_iota(jnp.int32, (B_SUB, c), 1)
        trow = lax.broadcasted_iota(jnp.int32, (B_SUB, B_DIM), 0)
        arow = lax.broadcasted_iota(jnp.int32, (c, c), 0)
        alane = lax.broadcasted_iota(jnp.int32, (c, c), 1)

        def chunk(cj, carry):
            ci = nc - 1 - cj
            rows = pl.ds(pl.multiple_of(ci * c, c), c)
            f = f_ref[rows, :]
            sig, gate = _hgrn_gates(f, lbv)
            kk = 1.0 - gate
            qb = q_ref[rows, :]
            sq = _sigmoid(qb)
            qq = qb * sq * scale
            v = i_ref[rows, :]
            b = _cumsum_rows(jnp.log(gate), c)
            st0 = st_ref[ci]
            dst = dstate[...]
            o = opre_ref[rows, :]
            gb = g_ref[rows, :]
            sg = _sigmoid(gb)
            silu_g = gb * sg
            d_out = dout_ref[rows, :]
            r = lax.rsqrt(jnp.mean(o * o, axis=-1, keepdims=True) + NORM_EPS)
            y = o * r
            dg_ref[rows, :] = (d_out * y * gnv * (sg * (1.0 + gb * (1.0 - sg)))).astype(BF16)
            dyn = d_out * silu_g
            dgn_ref[...] += jnp.sum(dyn * y, axis=0, keepdims=True)
            dy = dyn * gnv
            do = r * (dy - y * jnp.mean(dy * y, axis=-1, keepdims=True))
            eb = jnp.exp(b)
            bl = b[c - 1:c, :]
            ebl = jnp.exp(bl - b)
            ebl_last = jnp.exp(bl)
            do_b = do.astype(BF16)
            dst_b = dst.astype(BF16)
            dq_inter = jnp.dot(do_b, st0.astype(BF16), preferred_element_type=F32) * eb
            dst0 = lax.dot_general(do_b, (qq * eb).astype(BF16), TN,
                                   preferred_element_type=F32) + dst * ebl_last
            dv_inter = lax.dot_general((kk * ebl).astype(BF16), dst_b, NT, preferred_element_type=F32)
            dk_inter = jnp.dot(v.astype(BF16), dst_b, preferred_element_type=F32) * ebl
            amat = a_ref[ci]
            v_b = v.astype(BF16)
            d_a = lax.dot_general(do_b, v_b, NT, preferred_element_type=F32)
            d_a = jnp.where(arow >= alane, d_a, 0.0)
            dv_intra = lax.dot_general(amat.astype(BF16), do_b, TN, preferred_element_type=F32)
            dk_pairs = jnp.zeros((c, B_DIM), F32)
            dq_blocks = []
            for i in range(nsub):
                r0 = i * B_SUB
                qi, bi = qq[r0:r0 + B_SUB, :], b[r0:r0 + B_SUB, :]
                da_i = d_a[r0:r0 + B_SUB, :]
                if i == 0:
                    dq_i = jnp.zeros((B_SUB, B_DIM), F32)
                else:
                    eq, ek = _sub_scales(b, i)
                    da_m = jnp.where(lane < r0, da_i, 0.0)
                    dq_i = _dot3(da_m, kk * ek, NN) * eq
                    dk_pairs = dk_pairs + _dot3(da_m, qi * eq, TN) * ek
                for sl in range(B_SUB):
                    s = r0 + sl
                    e = jnp.exp(jnp.where(trow >= sl, bi - b[s:s + 1, :], -jnp.inf))
                    dacol = jnp.sum(jnp.where(lane == s, da_i, 0.0), axis=1, keepdims=True)
                    w = dacol * e
                    dq_i = dq_i + w * kk[s:s + 1, :]
                    dksc[s:s + 1, :] = jnp.sum(w * qi, axis=0, keepdims=True)
                dq_blocks.append(dq_i)
            dq = jnp.concatenate(dq_blocks, axis=0) + dq_inter
            dk = dk_pairs + dksc[...] + dk_inter
            dv = dv_intra + dv_inter
            db = qq * dq - kk * dk
            extra = (jnp.sum(kk * dk_inter, axis=0, keepdims=True)
                     + ebl_last * jnp.sum(st0 * dst, axis=0, keepdims=True))
            db = db + jnp.where(srow == c - 1, extra, 0.0)
            dlog = _cumsum_rows(db, c, reverse=True)
            dgate = dlog / gate - dk
            df_ref[rows, :] = (dgate * (1.0 - lbv) * sig * (1.0 - sig)).astype(BF16)
            dlb_ref[...] += jnp.sum(dgate * (1.0 - sig), axis=0, keepdims=True)
            dq_ref[rows, :] = (dq * scale * (sq * (1.0 + qb * (1.0 - sq)))).astype(BF16)
            di_ref[rows, :] = dv.astype(BF16)
            dstate[...] = dst0
            return carry

        lax.fori_loop(0, nc, chunk, 0)

    def col(off):
        return pl.BlockSpec((t, B_DIM), lambda h: (0, col0 + off * nh + h))

    hcol = pl.BlockSpec((t, B_DIM), lambda h: (0, h))
    vec = pl.BlockSpec((None, 1, B_DIM), lambda h: (h, 0, 0))
    wide = _sds((t, nh * B_DIM), BF16)
    return pl.pallas_call(
        kern, name=name, grid=(nh,),
        in_specs=[col(0), col(1), col(2), col(3), vec,
                  pl.BlockSpec((1, B_DIM), lambda h: (0, 0)), hcol,
                  pl.BlockSpec((None, nc, B_DIM, B_DIM), lambda h: (h, 0, 0, 0)),
                  pl.BlockSpec((None, nc, c, c), lambda h: (h, 0, 0, 0)),
                  pl.BlockSpec((t, B_DIM), lambda h: (0, dcol0 + h))],
        out_specs=[hcol, hcol, hcol, hcol, pl.BlockSpec((1, B_DIM), lambda h: (0, 0)), vec],
        out_shape=[wide, wide, wide, wide, _sds((1, B_DIM), F32), _sds((nh, 1, B_DIM), F32)],
        scratch_shapes=[pltpu.VMEM((B_DIM, B_DIM), F32), pltpu.VMEM((c, B_DIM), F32)],
        compiler_params=_params("arbitrary"),
    )(proj, proj, proj, proj, lb, gn, opre, states, amats, dout)


def lower_bounds_fwd(raw, name):
    n, w = raw.shape

    def kern(raw_ref, lb_ref, soft_ref):
        r = raw_ref[...]
        mx = r[0:1]
        for i in range(1, n):
            mx = jnp.maximum(mx, r[i:i + 1])
        e = jnp.exp(r - mx)
        den = e[0:1]
        for i in range(1, n):
            den = den + e[i:i + 1]
        soft = e / den
        soft_ref[...] = soft
        run = soft[0:1]
        lb_ref[0:1, :] = run - soft[0:1]
        for i in range(1, n):
            run = run + soft[i:i + 1]
            lb_ref[i:i + 1, :] = run - soft[0:1]

    return pl.pallas_call(kern, name=name, out_shape=[_sds((n, w), F32), _sds((n, w), F32)])(raw)


def lower_bounds_bwd(soft, dlb, name):
    n, w = soft.shape

    def kern(soft_ref, dlb_ref, out_ref):
        s = soft_ref[...]
        d = dlb_ref[...]
        total = d[0:1]
        for i in range(1, n):
            total = total + d[i:i + 1]
        us = []
        tail = total
        for i in range(n):
            us.append(tail - total if i == 0 else tail)
            tail = tail - d[i:i + 1]
        dot = s[0:1] * us[0]
        for i in range(1, n):
            dot = dot + s[i:i + 1] * us[i]
        for i in range(n):
            out_ref[i:i + 1, :] = s[i:i + 1] * (us[i] - dot)

    return pl.pallas_call(kern, name=name, out_shape=_sds((n, w), F32))(soft, dlb)


def _row_tile(kdim, n):
    tk = 512
    while tk > 8 and tk * n > 256 * 1024:
        tk //= 2
    return min(kdim, tk)


def _adam_update(w, g, m, v):
    m2 = ADAM_B1 * m + (1.0 - ADAM_B1) * g
    v2 = ADAM_B2 * v + (1.0 - ADAM_B2) * (g * g)
    m_hat = m2 / (1.0 - ADAM_B1 ** ADAM_STEP)
    v_hat = v2 / (1.0 - ADAM_B2 ** ADAM_STEP)
    delta = -ADAM_LR * (m_hat / (jnp.sqrt(v_hat) + ADAM_EPS) + ADAM_WD * w)
    return delta, m2, v2


def adamw_small(w, g, m, v, name):
    def kern(w_ref, g_ref, m_ref, v_ref, d_ref, m2_ref, v2_ref):
        d, m2, v2 = _adam_update(w_ref[...], g_ref[...], m_ref[...], v_ref[...])
        d_ref[...] = d
        m2_ref[...] = m2
        v2_ref[...] = v2

    return pl.pallas_call(kern, name=name, out_shape=[_sds(w.shape, F32)] * 3)(w, g, m, v)


def adamw_big(parts, w, m, v, name):
    nl, kdim, n = w.shape
    tk = _row_tile(kdim, n)

    def kern(p_ref, w_ref, m_ref, v_ref, g_ref, d_ref, m2_ref, v2_ref):
        g = p_ref[0].astype(F32)
        for q in range(1, 4):
            g = g + p_ref[q].astype(F32)
        d, m2, v2 = _adam_update(w_ref[...], g, m_ref[...], v_ref[...])
        g_ref[...] = g
        d_ref[...] = d
        m2_ref[...] = m2
        v2_ref[...] = v2

    blk = pl.BlockSpec((None, tk, n), lambda l, i: (l, i, 0))
    return pl.pallas_call(
        kern, name=name, grid=(nl, kdim // tk),
        in_specs=[pl.BlockSpec((None, 4, tk, n), lambda l, i: (l, 0, i, 0)), blk, blk, blk],
        out_specs=[blk] * 4, out_shape=[_sds(w.shape, F32)] * 4,
        compiler_params=_params("parallel", "parallel"),
    )(parts, w, m, v)


def cast_bf16(w, name):
    nl, kdim, n = w.shape
    tk = _row_tile(kdim, n)

    def kern(w_ref, o_ref):
        o_ref[...] = w_ref[...].astype(BF16)

    blk = pl.BlockSpec((None, tk, n), lambda l, i: (l, i, 0))
    return pl.pallas_call(
        kern, name=name, grid=(nl, kdim // tk), in_specs=[blk], out_specs=blk,
        out_shape=_sds(w.shape, BF16), compiler_params=_params("parallel", "parallel"),
    )(w)


def pair_add(dw, r1, core, name):
    kdim, n = dw.shape[1], dw.shape[2]
    tk = _row_tile(kdim, n)

    def kern(c_ref, a_ref, b_ref, o_ref):
        o_ref[...] = (a_ref[...].astype(F32) + b_ref[...].astype(F32)).astype(BF16)

    grid_spec = pltpu.PrefetchScalarGridSpec(
        num_scalar_prefetch=1, grid=(4, kdim // tk),
        in_specs=[pl.BlockSpec((None, tk, n), lambda p, i, c: (2 * p + c[0], i, 0)),
                  pl.BlockSpec((None, tk, n), lambda p, i, c: (p, i, 0))],
        out_specs=pl.BlockSpec((None, tk, n), lambda p, i, c: (p, i, 0)))
    return pl.pallas_call(
        kern, name=name, grid_spec=grid_spec, out_shape=_sds((4, kdim, n), BF16),
        compiler_params=_params("parallel", "parallel"),
    )(core, dw, r1)


ANY = pl.BlockSpec(memory_space=pl.ANY)


def _place():
    x, y, c = lax.axis_index("x"), lax.axis_index("y"), lax.axis_index("c")
    chips = [(1 - x, y), (x, 1 - y), (1 - x, 1 - y)]
    return x, y, c, chips


def all_gather(shards, name):
    n = len(shards)

    def kern(*refs):
        ins, outs = refs[:n], refs[n:2 * n]
        send_sems, recv_sems, local_sems = refs[2 * n:]
        x, y, c, chips = _place()
        me, sib = (x, y, c), (x, y, 1 - c)

        def copy(t, k, block, to, src=None):
            px, py, pc = block
            dst = outs[t].at[4 * px + 2 * py + pc]
            return pltpu.make_async_remote_copy(
                src_ref=dst if src is None else src, dst_ref=dst,
                send_sem=send_sems.at[7 * t + k], recv_sem=recv_sems.at[7 * t + k],
                device_id=to, device_id_type=MESH)

        mine = [pltpu.make_async_copy(ins[t], outs[t].at[4 * x + 2 * y + c], local_sems.at[t])
                for t in range(n)]
        for cp in mine:
            cp.start()
        first = []
        for t in range(n):
            first.append(copy(t, 0, me, sib, src=ins[t]))
            first += [copy(t, 1 + j, me, (*chip, c), src=ins[t]) for j, chip in enumerate(chips)]
        for cp in first:
            cp.start()
        passed = []
        for t in range(n):
            for j, chip in enumerate(chips):
                copy(t, 1 + j, (*chip, c), me).wait_recv()
                fwd = copy(t, 4 + j, (*chip, c), sib)
                fwd.start()
                passed.append(fwd)
        for t in range(n):
            copy(t, 0, sib, me).wait_recv()
            for j, chip in enumerate(chips):
                copy(t, 4 + j, (*chip, 1 - c), me).wait_recv()
        for cp in first + passed:
            cp.wait_send()
        for cp in mine:
            cp.wait()

    return pl.pallas_call(
        kern, name=name, in_specs=[ANY] * n, out_specs=[ANY] * n,
        out_shape=[_sds((N_DEV,) + s.shape, s.dtype) for s in shards],
        scratch_shapes=[pltpu.SemaphoreType.DMA((7 * n,)), pltpu.SemaphoreType.DMA((7 * n,)),
                        pltpu.SemaphoreType.DMA((n,))],
    )(*shards)


HBM = pl.BlockSpec(memory_space=pltpu.HBM)
SEM = pl.BlockSpec(memory_space=pltpu.SEMAPHORE)
DATAFLOW = pltpu.SideEffectType.DATAFLOW_SIDE_EFFECTING


def _first_level_targets():
    x, y, c, chips = _place()
    return 4 * x + 2 * y + c, [(x, y, 1 - c)] + [(*chip, c) for chip in chips]


def gather_start(shards, after, name):
    n = len(shards)
    lands = [lax.empty((N_DEV,) + s.shape, s.dtype) for s in shards]

    def kern(*refs):
        ins, lnd = refs[:n], refs[n:2 * n]
        send_sems, recv_sems, local_sems = refs[2 * n + len(after):2 * n + len(after) + 3]
        token = refs[-1]
        me, targets = _first_level_targets()
        for t in range(n):
            pltpu.make_async_copy(ins[t], lnd[t].at[me], local_sems.at[t]).start()
            for k, to in enumerate(targets):
                pltpu.make_async_remote_copy(
                    src_ref=ins[t], dst_ref=lnd[t].at[me], send_sem=send_sems.at[4 * t + k],
                    recv_sem=recv_sems.at[4 * t + k], device_id=to, device_id_type=MESH).start()
        token[...] = jnp.zeros_like(token)

    args = [pltpu.with_memory_space_constraint(a, pltpu.HBM) for a in list(shards) + lands]
    return pl.pallas_call(
        kern, name=name,
        out_shape=(pltpu.SemaphoreType.DMA((4 * n,)), pltpu.SemaphoreType.DMA((4 * n,)),
                   pltpu.SemaphoreType.DMA((n,)),
                   *[pltpu.HBM(a.shape, a.dtype) for a in args], _sds((8, LANES), F32)),
        in_specs=[HBM] * (2 * n) + [ANY] * len(after),
        out_specs=(SEM, SEM, SEM, *[HBM] * (2 * n), pl.BlockSpec(memory_space=pltpu.VMEM)),
        input_output_aliases={i: 3 + i for i in range(2 * n)},
        compiler_params=pltpu.CompilerParams(has_side_effects=DATAFLOW),
    )(*args, *after)


def gather_wait(send_sems, recv_sems, local_sems, shards, lands, after, name):
    n = len(shards)

    def kern(*refs):
        ins, lnd = refs[:n], refs[n:2 * n]
        send_sems, recv_sems, local_sems = refs[2 * n:2 * n + 3]
        me, targets = _first_level_targets()
        for t in range(n):
            pltpu.make_async_copy(ins[t], lnd[t].at[me], local_sems.at[t]).wait()
            for k, to in enumerate(targets):
                cp = pltpu.make_async_remote_copy(
                    src_ref=ins[t], dst_ref=lnd[t].at[me], send_sem=send_sems.at[4 * t + k],
                    recv_sem=recv_sems.at[4 * t + k], device_id=to, device_id_type=MESH)
                cp.wait_send()
                cp.wait_recv()

    bufs = list(shards) + list(lands)
    return pl.pallas_call(
        kern, name=name, out_shape=tuple(pltpu.HBM(a.shape, a.dtype) for a in bufs),
        in_specs=[HBM] * (2 * n) + [SEM, SEM, SEM, ANY], out_specs=[HBM] * (2 * n),
        input_output_aliases={i: i for i in range(2 * n)},
        compiler_params=pltpu.CompilerParams(has_side_effects=DATAFLOW),
    )(*bufs, send_sems, recv_sems, local_sems, after)


def _forward_copies(lnd, send_sems, recv_sems):
    x, y, c, chips = _place()
    passed = []
    for t in range(len(lnd)):
        for j, (qx, qy) in enumerate(chips):
            block = lnd[t].at[4 * qx + 2 * qy + c]
            passed.append(pltpu.make_async_remote_copy(
                src_ref=block, dst_ref=block, send_sem=send_sems.at[3 * t + j],
                recv_sem=recv_sems.at[3 * t + j], device_id=(x, y, 1 - c), device_id_type=MESH))
    return passed


def forward_now(lands, name):
    n = len(lands)

    def kern(*refs):
        copies = _forward_copies(refs[n:2 * n], refs[2 * n], refs[2 * n + 1])
        for cp in copies:
            cp.start()
        for cp in copies:
            cp.wait_recv()
        for cp in copies:
            cp.wait_send()

    return pl.pallas_call(
        kern, name=name, in_specs=[ANY] * n, out_specs=[ANY] * n,
        out_shape=[_sds(a.shape, a.dtype) for a in lands],
        input_output_aliases={i: i for i in range(n)},
        scratch_shapes=[pltpu.SemaphoreType.DMA((3 * n,)), pltpu.SemaphoreType.DMA((3 * n,))],
    )(*lands)


def sibling_start(grads, name):
    n = len(grads)
    lands = [lax.empty((4,) + g.shape[1:], g.dtype) for g in grads]

    def kern(*refs):
        ins, lnd = refs[:n], refs[n:2 * n]
        send_sems, recv_sems = refs[2 * n], refs[2 * n + 1]
        x, y, c, _ = _place()
        for t in range(n):
            for p in range(4):
                pltpu.make_async_remote_copy(
                    src_ref=ins[t].at[2 * p + 1 - c], dst_ref=lnd[t].at[p],
                    send_sem=send_sems.at[4 * t + p], recv_sem=recv_sems.at[4 * t + p],
                    device_id=(x, y, 1 - c), device_id_type=MESH).start()
        refs[-1][...] = jnp.zeros_like(refs[-1])

    args = [pltpu.with_memory_space_constraint(a, pltpu.HBM) for a in list(grads) + lands]
    return pl.pallas_call(
        kern, name=name,
        out_shape=(pltpu.SemaphoreType.DMA((4 * n,)), pltpu.SemaphoreType.DMA((4 * n,)),
                   *[pltpu.HBM(a.shape, a.dtype) for a in args], _sds((8, LANES), F32)),
        in_specs=[HBM] * (2 * n),
        out_specs=(SEM, SEM, *[HBM] * (2 * n), pl.BlockSpec(memory_space=pltpu.VMEM)),
        input_output_aliases={i: 2 + i for i in range(2 * n)},
        compiler_params=pltpu.CompilerParams(has_side_effects=DATAFLOW),
    )(*args)


def sibling_wait(send_sems, recv_sems, grads, lands, after, name):
    n = len(grads)

    def kern(*refs):
        ins, lnd = refs[:n], refs[n:2 * n]
        send_sems, recv_sems = refs[2 * n], refs[2 * n + 1]
        x, y, c, _ = _place()
        for t in range(n):
            for p in range(4):
                cp = pltpu.make_async_remote_copy(
                    src_ref=ins[t].at[2 * p + 1 - c], dst_ref=lnd[t].at[p],
                    send_sem=send_sems.at[4 * t + p], recv_sem=recv_sems.at[4 * t + p],
                    device_id=(x, y, 1 - c), device_id_type=MESH)
                cp.wait_send()
                cp.wait_recv()

    bufs = list(grads) + list(lands)
    outs = pl.pallas_call(
        kern, name=name, out_shape=tuple(pltpu.HBM(a.shape, a.dtype) for a in bufs),
        in_specs=[HBM] * (2 * n) + [SEM, SEM, ANY], out_specs=[HBM] * (2 * n),
        input_output_aliases={i: i for i in range(2 * n)},
        compiler_params=pltpu.CompilerParams(has_side_effects=DATAFLOW),
    )(*bufs, send_sems, recv_sems, after)
    return outs[:n], outs[n:]


def forward_start(lands, name):
    n = len(lands)

    def kern(*refs):
        for cp in _forward_copies(refs[:n], refs[n], refs[n + 1]):
            cp.start()
        refs[-1][...] = jnp.zeros_like(refs[-1])

    return pl.pallas_call(
        kern, name=name,
        out_shape=(pltpu.SemaphoreType.DMA((3 * n,)), pltpu.SemaphoreType.DMA((3 * n,)),
                   *[pltpu.HBM(a.shape, a.dtype) for a in lands], _sds((8, LANES), F32)),
        in_specs=[HBM] * n,
        out_specs=(SEM, SEM, *[HBM] * n, pl.BlockSpec(memory_space=pltpu.VMEM)),
        input_output_aliases={i: 2 + i for i in range(n)},
        compiler_params=pltpu.CompilerParams(has_side_effects=DATAFLOW),
    )(*lands)


def forward_wait(send_sems, recv_sems, lands, after, name):
    n = len(lands)

    def kern(*refs):
        for cp in _forward_copies(refs[:n], refs[n], refs[n + 1]):
            cp.wait_send()
            cp.wait_recv()

    return pl.pallas_call(
        kern, name=name, out_shape=tuple(pltpu.HBM(a.shape, a.dtype) for a in lands),
        in_specs=[HBM] * n + [SEM, SEM, ANY], out_specs=[HBM] * n,
        input_output_aliases={i: i for i in range(n)},
        compiler_params=pltpu.CompilerParams(has_side_effects=DATAFLOW),
    )(*lands, send_sems, recv_sems, after)


def all_reduce_small(vec, name):
    r = vec.shape[0]

    def kern(v_ref, o_ref, buf, send_sems, recv_sems):
        x, y, c, _ = _place()
        me = 4 * x + 2 * y + c
        peers = [(x, y, 1 - c), (1 - x, y, c), (x, 1 - y, c), (1 - x, 1 - y, c),
                 (1 - x, y, 1 - c), (x, 1 - y, 1 - c), (1 - x, 1 - y, 1 - c)]
        buf[me] = v_ref[...]
        copies = []
        for k, peer in enumerate(peers):
            cp = pltpu.make_async_remote_copy(
                src_ref=v_ref, dst_ref=buf.at[me], send_sem=send_sems.at[k],
                recv_sem=recv_sems.at[k], device_id=peer, device_id_type=MESH)
            cp.start()
            copies.append(cp)
        for cp in copies:
            cp.wait_recv()
        for cp in copies:
            cp.wait_send()
        total = buf[0]
        for d in range(1, N_DEV):
            total = total + buf[d]
        o_ref[...] = total

    vm = pl.BlockSpec(memory_space=pltpu.VMEM)
    return pl.pallas_call(
        kern, name=name, in_specs=[vm], out_specs=vm, out_shape=_sds(vec.shape, F32),
        scratch_shapes=[pltpu.VMEM((N_DEV, r, LANES), F32), pltpu.SemaphoreType.DMA((7,)),
                        pltpu.SemaphoreType.DMA((7,))],
    )(vec)


def exchange_with_sibling(grads, name):
    n = len(grads)

    def kern(*refs):
        ins, outs = refs[:n], refs[n:2 * n]
        send_sems, recv_sems = refs[2 * n:]
        x, y, c, _ = _place()
        copies = []
        for t in range(n):
            for p in range(4):
                cp = pltpu.make_async_remote_copy(
                    src_ref=ins[t].at[2 * p + 1 - c], dst_ref=outs[t].at[p],
                    send_sem=send_sems.at[4 * t + p], recv_sem=recv_sems.at[4 * t + p],
                    device_id=(x, y, 1 - c), device_id_type=MESH)
                cp.start()
                copies.append(cp)
        for cp in copies:
            cp.wait_recv()
        for cp in copies:
            cp.wait_send()

    return pl.pallas_call(
        kern, name=name, in_specs=[ANY] * n, out_specs=[ANY] * n,
        out_shape=[_sds((4,) + g.shape[1:], g.dtype) for g in grads],
        scratch_shapes=[pltpu.SemaphoreType.DMA((4 * n,)), pltpu.SemaphoreType.DMA((4 * n,))],
    )(*grads)


def exchange_between_chips(partials, layers, kinds, name):
    n = len(partials)
    n_kind = max(kinds) + 1
    shapes = []
    for kd in range(n_kind):
        idx = [i for i in range(n) if kinds[i] == kd]
        nl = max(layers[i] for i in idx) + 1
        shapes.append(_sds((nl,) + partials[idx[0]].shape, partials[idx[0]].dtype))

    def kern(*refs):
        ins, outs = refs[:n], refs[n:n + n_kind]
        send_sems, recv_sems, local_sems = refs[n + n_kind:]
        x, y, c, chips = _place()
        mine = 2 * x + y
        local = []
        copies = []
        for t in range(n):
            dst = outs[kinds[t]].at[layers[t], mine]
            lc = pltpu.make_async_copy(ins[t].at[mine], dst, local_sems.at[t])
            lc.start()
            local.append(lc)
            for j, (qx, qy) in enumerate(chips):
                cp = pltpu.make_async_remote_copy(
                    src_ref=ins[t].at[2 * qx + qy], dst_ref=dst,
                    send_sem=send_sems.at[3 * t + j], recv_sem=recv_sems.at[3 * t + j],
                    device_id=(qx, qy, c), device_id_type=MESH)
                cp.start()
                copies.append(cp)
        for cp in copies:
            cp.wait_recv()
        for cp in copies:
            cp.wait_send()
        for lc in local:
            lc.wait()

    return pl.pallas_call(
        kern, name=name, in_specs=[ANY] * n, out_specs=[ANY] * n_kind, out_shape=shapes,
        scratch_shapes=[pltpu.SemaphoreType.DMA((3 * n,)), pltpu.SemaphoreType.DMA((3 * n,)),
                        pltpu.SemaphoreType.DMA((n,))],
    )(*partials)


def scatter_start(partials, name):
    n = len(partials)
    lands = [lax.empty(p.shape, p.dtype) for p in partials]

    def kern(*refs):
        ins, lnd = refs[:n], refs[n:2 * n]
        send_sems, recv_sems, local_sems = refs[2 * n:2 * n + 3]
        token = refs[-1]
        x, y, c, chips = _place()
        mine = 2 * x + y
        for t in range(n):
            pltpu.make_async_copy(ins[t].at[mine], lnd[t].at[mine], local_sems.at[t]).start()
            for j, (qx, qy) in enumerate(chips):
                pltpu.make_async_remote_copy(
                    src_ref=ins[t].at[2 * qx + qy], dst_ref=lnd[t].at[mine],
                    send_sem=send_sems.at[3 * t + j], recv_sem=recv_sems.at[3 * t + j],
                    device_id=(qx, qy, c), device_id_type=MESH).start()
        token[...] = jnp.zeros_like(token)

    args = [pltpu.with_memory_space_constraint(a, pltpu.HBM) for a in list(partials) + lands]
    return pl.pallas_call(
        kern, name=name,
        out_shape=(pltpu.SemaphoreType.DMA((3 * n,)), pltpu.SemaphoreType.DMA((3 * n,)),
                   pltpu.SemaphoreType.DMA((n,)),
                   *[pltpu.HBM(a.shape, a.dtype) for a in args], _sds((8, LANES), F32)),
        in_specs=[HBM] * (2 * n),
        out_specs=(SEM, SEM, SEM, *[HBM] * (2 * n), pl.BlockSpec(memory_space=pltpu.VMEM)),
        input_output_aliases={i: 3 + i for i in range(2 * n)},
        compiler_params=pltpu.CompilerParams(has_side_effects=DATAFLOW),
    )(*args)


def scatter_wait(send_sems, recv_sems, local_sems, partials, lands, after, name):
    n = len(partials)

    def kern(*refs):
        ins, lnd = refs[:n], refs[n:2 * n]
        send_sems, recv_sems, local_sems = refs[2 * n:2 * n + 3]
        x, y, c, chips = _place()
        mine = 2 * x + y
        for t in range(n):
            pltpu.make_async_copy(ins[t].at[mine], lnd[t].at[mine], local_sems.at[t]).wait()
            for j, (qx, qy) in enumerate(chips):
                cp = pltpu.make_async_remote_copy(
                    src_ref=ins[t].at[2 * qx + qy], dst_ref=lnd[t].at[mine],
                    send_sem=send_sems.at[3 * t + j], recv_sem=recv_sems.at[3 * t + j],
                    device_id=(qx, qy, c), device_id_type=MESH)
                cp.wait_send()
                cp.wait_recv()

    bufs = list(partials) + list(lands)
    outs = pl.pallas_call(
        kern, name=name, out_shape=tuple(pltpu.HBM(a.shape, a.dtype) for a in bufs),
        in_specs=[HBM] * (2 * n) + [SEM, SEM, SEM, ANY], out_specs=[HBM] * (2 * n),
        input_output_aliases={i: i for i in range(2 * n)},
        compiler_params=pltpu.CompilerParams(has_side_effects=DATAFLOW),
    )(*bufs, send_sems, recv_sems, local_sems, after)
    return outs[n:]


def adamw_layers(parts, w, m, v, name):
    nl, kdim, n = w.shape
    tk = _row_tile(kdim, n)

    def kern(*refs):
        p_refs = refs[:nl]
        w_ref, m_ref, v_ref, g_ref, d_ref, m2_ref, v2_ref = refs[nl:]
        for l in range(nl):
            @pl.when(pl.program_id(0) == l)
            def _():
                g = p_refs[l][0].astype(F32)
                for q in range(1, 4):
                    g = g + p_refs[l][q].astype(F32)
                d, m2, v2 = _adam_update(w_ref[...], g, m_ref[...], v_ref[...])
                g_ref[...] = g
                d_ref[...] = d
                m2_ref[...] = m2
                v2_ref[...] = v2

    def part_spec(l):
        return pl.BlockSpec((4, tk, n), lambda li, i: (0, jnp.where(li == l, i, 0), 0))

    blk = pl.BlockSpec((None, tk, n), lambda li, i: (li, i, 0))
    return pl.pallas_call(
        kern, name=name, grid=(nl, kdim // tk),
        in_specs=[part_spec(l) for l in range(nl)] + [blk, blk, blk],
        out_specs=[blk] * 4, out_shape=[_sds(w.shape, F32)] * 4,
        compiler_params=_params("arbitrary", "arbitrary"),
    )(*parts, w, m, v)


def _pack(arrays):
    flat = jnp.concatenate([a.reshape(-1).astype(F32) for a in arrays])
    pad = (-flat.shape[0]) % (8 * LANES)
    return jnp.pad(flat, (0, pad)).reshape(-1, LANES)


def _unpack(packed, shapes):
    flat = packed.reshape(-1)
    out, off = [], 0
    for s in shapes:
        n = math.prod(s)
        out.append(flat[off:off + n].reshape(s))
        off += n
    return out


def _to_heads(x2d, dil, n_heads, dh):
    t = x2d.shape[0]
    return x2d.reshape(t // dil, dil, n_heads, dh).transpose(2, 1, 0, 3).reshape(n_heads, t, dh)


def _from_heads(xh, dil):
    h, t, w = xh.shape
    return xh.reshape(h, dil, t // dil, w).transpose(2, 1, 0, 3).reshape(t, h * w)


def _unperm(xh, dil):
    h, t, w = xh.shape
    return xh.reshape(h, dil, t // dil, w).transpose(0, 2, 1, 3).reshape(h, t, w)


def _perm(xh, dil):
    h, t, w = xh.shape
    return xh.reshape(h, t // dil, dil, w).transpose(0, 2, 1, 3).reshape(h, t, w)


def local_step(x, target, norm_mix_g, norm_mlp_g, final_norm_g, lbs, hgrn_norm_g, sinks,
               bq_full, bo_full, weights_get, weights_mid, grads_ready):
    t, d = x.shape
    depth = norm_mix_g.shape[0]
    na = d // 2 // A_DIM
    nbh = d // 2 // B_DIM
    nq = d // C_DIM
    nkv = nq // C_GROUP
    a_w = 3 * na * A_DIM
    c_w = (nq + 2 * nkv) * C_DIM
    tabs_a = rope_tables(t, A_DIM)
    tabs_c = rope_tables(t, C_DIM)
    saved = []
    for l in range(depth):
        s = {"x_in": x}
        (win_g, wout_g), token = weights_get(l, x)
        h = rms_fwd(x, norm_mix_g[l] + token, "norm_mix_fwd")
        s["h"] = h
        if l % 2 == 0:
            e = l // 2
            proj = mm_cols_sharded(h, win_g, 0, "even_in_proj")[0]
            qkv_r = rope_call(proj, tabs_a, a_w, 2 * na, False, "rope_a")[0]
            nums, ms, ls, hms = [], [], [], []
            for window, dil in A_BRANCHES:
                hm = _to_heads(qkv_r, dil, 3 * na, A_DIM)
                num, m, lsum = band_fwd(hm, 0, na, 2 * na, na, 1, t // dil // BLK, window // dil,
                                        f"dilated_fwd_{dil}")
                hms.append(hm)
                nums.append(_unperm(num, dil))
                ms.append(_unperm(m, dil))
                ls.append(_unperm(lsum, dil))
            oa, lse = merge_branches(nums, ms, ls, "dilated_merge")
            lb_e = lbs[e].reshape(nbh, 1, B_DIM)
            gn_e = hgrn_norm_g[e].reshape(1, B_DIM)
            ob, opre, states, amats = hgrn_fwd(proj, 3 * na, nbh, lb_e, gn_e, "hgrn_fwd")
            mixed = jnp.concatenate([_from_heads(oa, 1).astype(BF16), ob], axis=1)
            x = mm_rows_sharded(mixed, wout_g, 0, "even_out_proj", [x], ["tile"], _ep_residual)
            s.update(proj=proj, hms=hms, oa=oa, lse=lse, opre=opre, states=states, amats=amats,
                     mixed=mixed, lb=lb_e, gn=gn_e)
        else:
            o = l // 2
            wq = win_g[:, 0].transpose(1, 0, 2).reshape(d, c_w)
            proj = mm_plain(h, wq, "odd_qkv_proj", [bq_full[o].reshape(1, c_w)], ["row"], _ep_bias)
            qkv_r = rope_call(proj, tabs_c, c_w, (nq + nkv) * C_DIM // LANES, False, "rope_c")[0]
            hm = _to_heads(qkv_r, 1, nq + 2 * nkv, C_DIM)
            sink_rows = jnp.repeat(sinks[o].reshape(nkv, C_GROUP), BLK, axis=1).reshape(
                nkv, C_GROUP * BLK, 1)
            o_hm, lse = band_fwd(hm, 0, nq, nq + nkv, nkv, C_GROUP, t // BLK, C_WINDOW - 1,
                                 "swa_fwd", sink_rows=sink_rows, normalise=True)
            attn = _from_heads(o_hm, 1).astype(BF16)
            x = mm_rows_sharded(attn, wout_g, 0, "odd_out_proj", [bo_full[o].reshape(1, d), x],
                                ["row", "tile"], _ep_bias_residual)
            s.update(wq=wq, hm=hm, sink_rows=sink_rows, o_hm=o_hm, lse=lse, attn=attn)
        s["x_mid"] = x
        (w1_g, w2_g), token = weights_mid(l, x)
        s.update(win=win_g, wout=wout_g, w1=w1_g, w2=w2_g)
        h2 = rms_fwd(x, norm_mlp_g[l] + token, "norm_mlp_fwd")
        u, act = mm_cols_sharded(h2, w1_g, 0, "mlp_up", epilogue=_ep_relu2, n_out=2)
        x = mm_rows_sharded(act, w2_g, 0, "mlp_down", [x], ["tile"], _ep_residual)
        s.update(h2=h2, u=u, act=act)
        saved.append(s)

    dx, dxb, dg_final, loss_part = loss_head(x, final_norm_g, target, "loss_head")
    big = []
    small = {"final": dg_final, "loss": loss_part, "mix": [None] * depth, "mlp": [None] * depth,
             "lb": {}, "gn": {}, "sinks": {}, "bq": {}, "bo": {}}
    for l in reversed(range(depth)):
        s = saved[l]
        win_g, wout_g, w1_g, w2_g = s["win"], s["wout"], s["w1"], s["w2"]
        big.append(("w2", l, mm_tn(s["act"], dxb, "mlp_down_dw").reshape(N_DEV, -1, d)))
        du = mm_nt_rows_sharded(dxb, w2_g, 0, "mlp_down_dx", extras=[s["u"]],
                                epilogue=_ep_relu2_bwd, out_dtype=BF16)
        big.append(("w1", l, mm_tn(s["h2"], du, "mlp_up_dw", shard_cols=w1_g.shape[-1])))
        dh2 = mm_nt_cols_sharded(du, w1_g, 0, "mlp_up_dx")
        token = grads_ready(l, big[-2:])
        dx, dxb, dg, col_dx = rms_bwd(s["x_mid"], norm_mlp_g[l] + token, dh2, dx, "norm_mlp_bwd")
        small["mlp"][l] = dg
        if l % 2 == 0:
            e = l // 2
            big.append(("wout", e, mm_tn(s["mixed"], dxb, "even_out_dw").reshape(N_DEV, -1, d)))
            dmixed = mm_nt_rows_sharded(dxb, wout_g, 0, "even_out_dx")
            do_hm = _to_heads(dmixed[:, :na * A_DIM], 1, na, A_DIM)
            delta = head_delta(s["oa"], do_hm, "dilated_delta")
            dsum = None
            for (window, dil), hm in zip(A_BRANCHES, s["hms"]):
                dq, dk, dv = band_bwd(hm, 0, na, 2 * na, _perm(do_hm, dil).astype(BF16),
                                      _perm(s["lse"], dil), _perm(delta, dil), na, 1,
                                      t // dil // BLK, window // dil, f"dilated_bwd_{dil}")
                part = _from_heads(jnp.concatenate([dq, dk, dv], axis=0), dil)
                dsum = part if dsum is None else dsum + part
            dqkv_a = rope_call(dsum, tabs_a, a_w, 2 * na, True, "rope_a_bwd")[0]
            dqb, dfb, dib, dgb, dgn, dlb = hgrn_bwd(s["proj"], 3 * na, nbh, s["lb"], s["gn"],
                                                    s["opre"], s["states"], s["amats"], dmixed, na,
                                                    "hgrn_bwd")
            small["gn"][e] = dgn
            small["lb"][e] = dlb
            dproj = jnp.concatenate([dqkv_a, dqb, dfb, dib, dgb], axis=1)
            big.append(("win", e, mm_tn(s["h"], dproj, "even_in_dw", shard_cols=win_g.shape[-1])))
            dh = mm_nt_cols_sharded(dproj, win_g, 0, "even_in_dx")
        else:
            o = l // 2
            small["bo"][o] = col_dx
            big.append(("wo", o, mm_tn(s["attn"], dxb, "odd_out_dw").reshape(N_DEV, -1, d)))
            dattn = mm_nt_rows_sharded(dxb, wout_g, 0, "odd_out_dx")
            do_hm = _to_heads(dattn, 1, nq, C_DIM)
            dq, dk, dv, dsink = band_bwd(s["hm"], 0, nq, nq + nkv, do_hm, s["lse"], s["o_hm"],
                                         nkv, C_GROUP, t // BLK, C_WINDOW - 1, "swa_bwd",
                                         sink_rows=s["sink_rows"], delta_from_o=True)
            small["sinks"][o] = dsink
            dqkv = _from_heads(jnp.concatenate([dq, dk, dv], axis=0), 1)
            dproj, dbq = rope_call(dqkv, tabs_c, c_w, (nq + nkv) * C_DIM // LANES, True,
                                   "rope_c_bwd", col_sum=True)
            small["bq"][o] = dbq
            dwq = mm_tn(s["h"], dproj, "odd_qkv_dw", tn=512)
            big.append(("wqkv", o, dwq.reshape(d, N_DEV, -1).transpose(1, 0, 2)))
            dh = mm_nt_plain(dproj, s["wq"], "odd_qkv_dx", tk=c_w)
        token = grads_ready(l, big[-2:])
        dx, dxb, dg, _ = rms_bwd(s["x_in"], norm_mix_g[l] + token, dh, dx, "norm_mix_bwd")
        small["mix"][l] = dg
    return dx, small


def kernel(x, norm_mix_g, norm_mlp_g, final_norm_g, even_w_in, even_w_out, hgrn_lb_raw, hgrn_norm_g, odd_w_qkv, odd_b_qkv, odd_sinks, odd_w_o, odd_b_o, mlp_w1, mlp_w2, loss_target, m_norm_mix_g, m_norm_mlp_g, m_final_norm_g, m_even_w_in, m_even_w_out, m_hgrn_lb_raw, m_hgrn_norm_g, m_odd_w_qkv, m_odd_b_qkv, m_odd_sinks, m_odd_w_o, m_odd_b_o, m_mlp_w1, m_mlp_w2, v_norm_mix_g, v_norm_mlp_g, v_final_norm_g, v_even_w_in, v_even_w_out, v_hgrn_lb_raw, v_hgrn_norm_g, v_odd_w_qkv, v_odd_b_qkv, v_odd_sinks, v_odd_w_o, v_odd_b_o, v_mlp_w1, v_mlp_w2):
    d = x.shape[2]
    depth = norm_mix_g.shape[0]
    n_even, n_odd = even_w_in.shape[0], odd_w_qkv.shape[0]
    xi, yi, ci = lax.axis_index("x"), lax.axis_index("y"), lax.axis_index("c")
    dev = 4 * xi + 2 * yi + ci
    core = ci.astype(jnp.int32).reshape(1)

    big_w = {"win": even_w_in, "wout": even_w_out, "wqkv": odd_w_qkv, "wo": odd_w_o,
             "w1": mlp_w1, "w2": mlp_w2}
    big_m = {"win": m_even_w_in, "wout": m_even_w_out, "wqkv": m_odd_w_qkv, "wo": m_odd_w_o,
             "w1": m_mlp_w1, "w2": m_mlp_w2}
    big_v = {"win": v_even_w_in, "wout": v_even_w_out, "wqkv": v_odd_w_qkv, "wo": v_odd_w_o,
             "w1": v_mlp_w1, "w2": v_mlp_w2}
    kinds = list(big_w)
    casts = {k: cast_bf16(big_w[k], f"cast_{k}") for k in kinds}

    def layer_shards(l):
        a, b = ("win", "wout") if l % 2 == 0 else ("wqkv", "wo")
        return [casts[a][l // 2], casts[b][l // 2], casts["w1"][l], casts["w2"][l]]

    bq_w, bo_w = odd_b_qkv.shape[1], odd_b_o.shape[1]
    bq_mine = lax.dynamic_update_slice(jnp.zeros((n_odd, N_DEV * bq_w), F32), odd_b_qkv,
                                       (0, dev * bq_w))
    bo_mine = lax.dynamic_update_slice(jnp.zeros((n_odd, N_DEV * bo_w), F32), odd_b_o,
                                       (0, dev * bo_w))
    biases = all_reduce_small(_pack([bq_mine, bo_mine]), "gather_biases")
    bq_full, bo_full = _unpack(biases, [bq_mine.shape, bo_mine.shape])

    first_level = {}
    second_level = {}
    ready = {}
    zero = jnp.zeros((), F32)

    def start_first_level(key, shards, after):
        started = gather_start(shards, after, f"gather_start_{key}")
        first_level[key] = started[:-1]
        return started[-1]

    def finish_first_level(key, after):
        send_sems, recv_sems, local_sems, *bufs = first_level.pop(key)
        n = len(bufs) // 2
        bufs = gather_wait(send_sems, recv_sems, local_sems, bufs[:n], bufs[n:], after,
                           f"gather_wait_{key}")
        return bufs[n:]

    def weights_get(l, after):
        if l == 0:
            shards = layer_shards(0)
            mixer = all_gather(shards[:2], "gather_layer_0_mixer")
            token = start_first_level("0_mlp", shards[2:], [mixer[0], biases])
            if depth > 1:
                token = start_first_level(1, layer_shards(1), [token])
            return [g[:, None] for g in mixer], token[0, 0]
        if l == 1:
            gathered = forward_now(finish_first_level(1, after), "gather_forward_1")
        else:
            send_sems, recv_sems, *lands = second_level.pop(l)
            gathered = forward_wait(send_sems, recv_sems, lands, after,
                                    f"gather_forward_wait_{l}")
        ready[l] = gathered[2:]
        return [g[:, None] for g in gathered[:2]], zero

    def weights_mid(l, after):
        token = zero
        if l == 0:
            ready[0] = forward_now(finish_first_level("0_mlp", after), "gather_forward_0_mlp")
            if depth > 2:
                token = start_first_level(2, layer_shards(2), [ready[0][0]])[0, 0]
        elif l + 1 < depth:
            started = forward_start(finish_first_level(l + 1, after),
                                    f"gather_forward_start_{l + 1}")
            second_level[l + 1] = started[:-1]
            token = started[-1][0, 0]
            if l + 2 < depth:
                token = token + start_first_level(l + 2, layer_shards(l + 2),
                                                  [started[-1]])[0, 0]
        return [g[:, None] for g in ready.pop(l)], token

    exchanging = []
    scattering = []

    def finish_exchange(after):
        tag, names, layer_idx, (send_sems, recv_sems, *bufs) = exchanging.pop()
        n = len(names)
        grads, received = sibling_wait(send_sems, recv_sems, bufs[:n], bufs[n:], after,
                                       f"scatter_d2d_wait_{tag}")
        partials = [pair_add(g, r, core, f"pair_add_{k}")
                    for k, g, r in zip(names, grads, received)]
        started = scatter_start(partials, f"scatter_start_{tag}")
        scattering.append((tag, names, layer_idx, started[:-1]))
        return started[-1][0, 0]

    def grads_ready(l, group):
        names = [k for k, _, _ in group]
        grads = [g for _, _, g in group]
        tag = f"{l}_{names[0]}"
        token = finish_exchange(grads[0]) if exchanging else zero
        started = sibling_start(grads, f"scatter_d2d_start_{tag}")
        exchanging.append((tag, names, [li for _, li, _ in group], started[:-1]))
        return token + started[-1][0, 0]

    lbs, soft = lower_bounds_fwd(hgrn_lb_raw, "lower_bounds")

    dx, small = local_step(x[0], loss_target[0], norm_mix_g, norm_mlp_g, final_norm_g, lbs,
                           hgrn_norm_g, odd_sinks, bq_full, bo_full, weights_get, weights_mid,
                           grads_ready)
    finish_exchange(dx)

    parts = ([small["mix"][l] for l in range(depth)] + [small["mlp"][l] for l in range(depth)]
             + [small["final"]] + [small["lb"][e] for e in range(n_even)]
             + [small["gn"][e] for e in range(n_even)] + [small["sinks"][o] for o in range(n_odd)]
             + [small["bq"][o] for o in range(n_odd)] + [small["bo"][o] for o in range(n_odd)]
             + [small["loss"]])
    shapes = ([(depth, d)] * 2 + [(d,), hgrn_lb_raw.shape, hgrn_norm_g.shape, odd_sinks.shape,
              (n_odd, N_DEV * bq_w), (n_odd, N_DEV * bo_w), (1, LANES)])
    g_mix, g_mlp, g_final, d_lbs, g_gn, g_sinks, g_bq_full, g_bo_full, loss_v = _unpack(
        all_reduce_small(_pack(parts), "reduce_small"), shapes)
    g_lb = lower_bounds_bwd(soft, d_lbs, "lower_bounds_bwd")
    g_bq = lax.dynamic_slice(g_bq_full, (0, dev * bq_w), (n_odd, bq_w))
    g_bo = lax.dynamic_slice(g_bo_full, (0, dev * bo_w), (n_odd, bo_w))
    loss = loss_v[0, 0]

    small_names = ["norm_mix_g", "norm_mlp_g", "final_norm_g", "hgrn_lb_raw", "hgrn_norm_g",
                   "odd_b_qkv", "odd_sinks", "odd_b_o"]
    small_w = [norm_mix_g, norm_mlp_g, final_norm_g, hgrn_lb_raw, hgrn_norm_g, odd_b_qkv,
               odd_sinks, odd_b_o]
    small_m = [m_norm_mix_g, m_norm_mlp_g, m_final_norm_g, m_hgrn_lb_raw, m_hgrn_norm_g,
               m_odd_b_qkv, m_odd_sinks, m_odd_b_o]
    small_v = [v_norm_mix_g, v_norm_mlp_g, v_final_norm_g, v_hgrn_lb_raw, v_hgrn_norm_g,
               v_odd_b_qkv, v_odd_sinks, v_odd_b_o]
    small_g = [g_mix, g_mlp, g_final, g_lb, g_gn, g_bq, g_sinks, g_bo]
    sshapes = [w.shape for w in small_w]
    sd, sm, sv = adamw_small(_pack(small_w), _pack(small_g), _pack(small_m), _pack(small_v),
                             "adamw_small")
    res = {}
    for name, g, dl, m2, v2 in zip(small_names, small_g, _unpack(sd, sshapes),
                                   _unpack(sm, sshapes), _unpack(sv, sshapes)):
        res[name] = (g.reshape(dl.shape), dl, m2, v2)

    landed = {k: [None] * big_w[k].shape[0] for k in kinds}
    for tag, names, layer_idx, (send_sems, recv_sems, local_sems, *bufs) in scattering:
        n = len(names)
        lands = scatter_wait(send_sems, recv_sems, local_sems, bufs[:n], bufs[n:], dx,
                             f"scatter_wait_{tag}")
        for k, li, land in zip(names, layer_idx, lands):
            landed[k][li] = land
    long_names = {"win": "even_w_in", "wout": "even_w_out", "wqkv": "odd_w_qkv", "wo": "odd_w_o",
                  "w1": "mlp_w1", "w2": "mlp_w2"}
    for k in kinds:
        res[long_names[k]] = tuple(adamw_layers(landed[k], big_w[k], big_m[k], big_v[k],
                                                f"adamw_{k}"))

    order = ["norm_mix_g", "norm_mlp_g", "final_norm_g", "even_w_in", "even_w_out", "hgrn_lb_raw",
             "hgrn_norm_g", "odd_w_qkv", "odd_b_qkv", "odd_sinks", "odd_w_o", "odd_b_o", "mlp_w1",
             "mlp_w2"]
    outs = [loss, dx[None]]
    for j in range(4):
        outs += [res[n][j] for n in order]
    return tuple(outs)
```

```python
import functools
import math

import jax
import jax.numpy as jnp
from jax import lax
from jax.experimental import pallas as pl
from jax.experimental.pallas import tpu as pltpu

F32 = jnp.float32
BF16 = jnp.bfloat16
MESH = pl.DeviceIdType.MESH

N_DEV = 8
NORM_EPS = 1e-5
ROPE_THETA = 500000.0
BLK = 128
A_DIM = 128
A_BRANCHES = ((128, 1), (512, 4), (2048, 16))
B_DIM = 128
B_CHUNK = 64
C_DIM = 64
C_GROUP = 8
C_WINDOW = 128
LANES = 128

ADAM_LR = 0.001
ADAM_B1 = 0.9
ADAM_B2 = 0.999
ADAM_EPS = 1e-08
ADAM_WD = 0.01
ADAM_STEP = 10

NN = (((1,), (0,)), ((), ()))
NT = (((1,), (1,)), ((), ()))
TN = (((0,), (0,)), ((), ()))


def _params(*sem):
    return pltpu.CompilerParams(dimension_semantics=sem)


def _sigmoid(x):
    return 1.0 / (1.0 + jnp.exp(-x))


def _rows_call(name, body, row_ins, full_ins, row_outs, acc_outs, tm):
    t = row_ins[0].shape[0]
    n_ri, n_fi, n_ro = len(row_ins), len(full_ins), len(row_outs)

    def kern(*refs):
        i = pl.program_id(0)
        body(i, refs[:n_ri], refs[n_ri:n_ri + n_fi],
             refs[n_ri + n_fi:n_ri + n_fi + n_ro], refs[n_ri + n_fi + n_ro:])

    def row_spec(shape):
        return pl.BlockSpec((tm,) + tuple(shape[1:]), lambda i: (i,) + (0,) * (len(shape) - 1))

    def full_spec(shape):
        return pl.BlockSpec(tuple(shape), lambda i: (0,) * len(shape))

    outs = pl.pallas_call(
        kern, name=name, grid=(t // tm,),
        in_specs=[row_spec(a.shape) for a in row_ins] + [full_spec(a.shape) for a in full_ins],
        out_specs=[row_spec(s.shape) for s in row_outs] + [full_spec(s.shape) for s in acc_outs],
        out_shape=list(row_outs) + list(acc_outs),
        compiler_params=_params("arbitrary" if acc_outs else "parallel"),
    )(*row_ins, *full_ins)
    return outs


def _sds(shape, dtype):
    return jax.ShapeDtypeStruct(tuple(shape), dtype)


def rms_fwd(x, g, name):
    t, d = x.shape

    def body(i, ri, fi, ro, ao):
        xv = ri[0][...]
        r = lax.rsqrt(jnp.mean(xv * xv, axis=-1, keepdims=True) + NORM_EPS)
        ro[0][...] = (xv * r * fi[0][...]).astype(BF16)

    return _rows_call(name, body, [x], [g.reshape(1, d)], [_sds((t, d), BF16)], [], 256)[0]


def rms_bwd(x, g, dh, dx_res, name):
    t, d = x.shape

    def body(i, ri, fi, ro, ao):
        xv, dhv, res = ri[0][...], ri[1][...], ri[2][...]
        gv = fi[0][...]
        r = lax.rsqrt(jnp.mean(xv * xv, axis=-1, keepdims=True) + NORM_EPS)
        gd = gv * dhv
        dx = res + r * gd - xv * (r * r * r) * jnp.mean(xv * gd, axis=-1, keepdims=True)
        ro[0][...] = dx
        ro[1][...] = dx.astype(BF16)

        @pl.when(i == 0)
        def _():
            ao[0][...] = jnp.zeros_like(ao[0])
            ao[1][...] = jnp.zeros_like(ao[1])

        ao[0][...] += jnp.sum(dhv * xv * r, axis=0, keepdims=True)
        ao[1][...] += jnp.sum(dx, axis=0, keepdims=True)

    return _rows_call(name, body, [x, dh, dx_res], [g.reshape(1, d)],
                      [_sds((t, d), F32), _sds((t, d), BF16)],
                      [_sds((1, d), F32), _sds((1, d), F32)], 256)


def loss_head(x, g, target, name):
    t, d = x.shape

    def body(i, ri, fi, ro, ao):
        xv, tg = ri[0][...], ri[1][...]
        gv = fi[0][...]
        r = lax.rsqrt(jnp.mean(xv * xv, axis=-1, keepdims=True) + NORM_EPS)
        e = xv * r * gv - tg
        dy = e * (1.0 / d)
        gd = gv * dy
        dx = r * gd - xv * (r * r * r) * jnp.mean(xv * gd, axis=-1, keepdims=True)
        ro[0][...] = dx
        ro[1][...] = dx.astype(BF16)

        @pl.when(i == 0)
        def _():
            ao[0][...] = jnp.zeros_like(ao[0])
            ao[1][...] = jnp.zeros_like(ao[1])

        ao[0][...] += jnp.sum(dy * xv * r, axis=0, keepdims=True)
        part = 0.5 * jnp.sum(jnp.mean(e * e, axis=-1, keepdims=True), axis=0, keepdims=True)
        ao[1][...] += jnp.broadcast_to(part, (1, LANES))

    return _rows_call(name, body, [x, target], [g.reshape(1, d)],
                      [_sds((t, d), F32), _sds((t, d), BF16)],
                      [_sds((1, d), F32), _sds((1, LANES), F32)], 256)


def rope_tables(seq, head_dim):
    rot = head_dim // 4
    half = rot // 2
    inv_freq = 1.0 / (ROPE_THETA ** (jnp.arange(0, rot, 2, dtype=F32) / rot))
    ang = jnp.arange(seq, dtype=F32)[:, None] * inv_freq[None, :]
    cos, sin = jnp.cos(ang), jnp.sin(ang)
    zeros = jnp.zeros((seq, head_dim - rot), F32)
    zh = jnp.zeros((seq, half), F32)
    c = jnp.concatenate([cos, cos, jnp.ones((seq, head_dim - rot), F32)], axis=-1)
    sp = jnp.concatenate([zh, sin, zeros], axis=-1)
    sm = jnp.concatenate([-sin, zh, zeros], axis=-1)
    rep = LANES // head_dim
    return jnp.tile(c, (1, rep)), jnp.tile(sp, (1, rep)), jnp.tile(sm, (1, rep)), half


def rope_call(x, tabs, width, n_rope, inverse, name, col_sum=False):
    c, sp, sm, half = tabs
    t = x.shape[0]
    tm = 256
    n_slab = width // LANES

    def kern(x_ref, c_ref, sp_ref, sm_ref, o_ref, *acc):
        cv, spv, smv = c_ref[...], sp_ref[...], sm_ref[...]
        for j in range(n_slab):
            xs = x_ref[:, j * LANES:(j + 1) * LANES].astype(F32)
            if j < n_rope:
                if inverse:
                    ys = (xs * cv + pltpu.roll(xs * spv, LANES - half, 1)
                          + pltpu.roll(xs * smv, half, 1))
                else:
                    ys = (xs * cv + pltpu.roll(xs, half, 1) * spv
                          + pltpu.roll(xs, LANES - half, 1) * smv)
            else:
                ys = xs
            o_ref[:, j * LANES:(j + 1) * LANES] = ys.astype(BF16)
            if col_sum:
                @pl.when(pl.program_id(0) == 0)
                def _():
                    acc[0][:, j * LANES:(j + 1) * LANES] = jnp.zeros((1, LANES), F32)
                acc[0][:, j * LANES:(j + 1) * LANES] += jnp.sum(ys, axis=0, keepdims=True)

    tab_spec = pl.BlockSpec((tm, LANES), lambda i: (i, 0))
    out_shape = [_sds((t, width), BF16)]
    out_specs = [pl.BlockSpec((tm, width), lambda i: (i, 0))]
    if col_sum:
        out_shape.append(_sds((1, width), F32))
        out_specs.append(pl.BlockSpec((1, width), lambda i: (0, 0)))
    return pl.pallas_call(
        kern, name=name, grid=(t // tm,),
        in_specs=[pl.BlockSpec((tm, width), lambda i: (i, 0)), tab_spec, tab_spec, tab_spec],
        out_specs=out_specs, out_shape=out_shape,
        compiler_params=_params("arbitrary" if col_sum else "parallel"),
    )(x, c, sp, sm)


def _mm_call(name, a, b, extras, out_shapes, grid, a_spec, b_spec, extra_specs, out_specs,
             acc_shape, dims, epilogue):
    n_ex, n_out = len(extras), len(out_shapes)
    nk = grid[2]

    def product(a_ref, b_ref):
        bv = b_ref[...]
        if bv.ndim == 3:
            bv = bv.reshape(bv.shape[0] * bv.shape[1], bv.shape[2])
        return lax.dot_general(a_ref[...].astype(BF16), bv.astype(BF16), dims,
                               preferred_element_type=F32)

    def kern(*refs):
        a_ref, b_ref = refs[0], refs[1]
        ex = refs[2:2 + n_ex]
        outs = refs[2 + n_ex:2 + n_ex + n_out]
        if nk == 1:
            epilogue(product(a_ref, b_ref), ex, outs)
            return
        acc = refs[-1]
        k = pl.program_id(2)

        @pl.when(k == 0)
        def _():
            acc[...] = product(a_ref, b_ref)

        @pl.when(k > 0)
        def _():
            acc[...] += product(a_ref, b_ref)

        @pl.when(k == nk - 1)
        def _():
            epilogue(acc[...], ex, outs)

    return pl.pallas_call(
        kern, name=name, grid=grid,
        in_specs=[a_spec, b_spec, *extra_specs], out_specs=out_specs, out_shape=out_shapes,
        scratch_shapes=[pltpu.VMEM(acc_shape, F32)] if nk > 1 else [],
        compiler_params=_params("parallel", "parallel", "arbitrary"),
    )(a, b, *extras)


def _ep_store(dtype):
    def ep(acc, ex, outs):
        outs[0][...] = acc.astype(dtype)
    return ep


def _ep_residual(acc, ex, outs):
    outs[0][...] = acc + ex[0][...]


def _ep_bias(acc, ex, outs):
    outs[0][...] = acc + ex[0][...]


def _ep_bias_residual(acc, ex, outs):
    outs[0][...] = acc + ex[0][...] + ex[1][...]


def _ep_relu2(acc, ex, outs):
    outs[0][...] = acc
    rl = jnp.maximum(acc, 0.0)
    outs[1][...] = (rl * rl).astype(BF16)


def _ep_relu2_bwd(acc, ex, outs):
    outs[0][...] = (acc * (2.0 * jnp.maximum(ex[0][...], 0.0))).astype(BF16)


MM_TM = 1024
MM_TN = 1024
MM_TK = 2048


def mm_cols_sharded(a, wg, layer, name, epilogue=None, n_out=1):
    m, kdim = a.shape
    n = wg.shape[-1]
    tm, tk = min(m, MM_TM), min(kdim, MM_TK)
    if epilogue is None:
        epilogue, outs = _ep_store(F32), [_sds((m, N_DEV * n), F32)]
    else:
        outs = [_sds((m, N_DEV * n), F32), _sds((m, N_DEV * n), BF16)][:n_out]
    return _mm_call(
        name, a, wg, [], outs, (m // tm, N_DEV, kdim // tk),
        pl.BlockSpec((tm, tk), lambda i, j, k: (i, k)),
        pl.BlockSpec((None, None, tk, n), lambda i, j, k: (j, layer, k, 0)),
        [], [pl.BlockSpec((tm, n), lambda i, j, k: (i, j))] * len(outs),
        (tm, n), NN, epilogue)


def _extra_specs(extra_kinds, tm, tn):
    specs = []
    for kind in extra_kinds:
        if kind == "row":
            specs.append(pl.BlockSpec((1, tn), lambda i, j, k: (0, j)))
        else:
            specs.append(pl.BlockSpec((tm, tn), lambda i, j, k: (i, j)))
    return specs


def mm_rows_sharded(a, wg, layer, name, extras, extra_kinds, epilogue):
    m, kdim = a.shape
    ks, n = wg.shape[-2], wg.shape[-1]
    tm, tn = min(m, MM_TM), min(n, MM_TN)
    gps = max(1, min(kdim, MM_TK) // ks)
    return _mm_call(
        name, a, wg, extras, [_sds((m, n), F32)], (m // tm, n // tn, N_DEV // gps),
        pl.BlockSpec((tm, gps * ks), lambda i, j, k: (i, k)),
        pl.BlockSpec((gps, None, ks, tn), lambda i, j, k: (k, layer, 0, j)),
        _extra_specs(extra_kinds, tm, tn), [pl.BlockSpec((tm, tn), lambda i, j, k: (i, j))],
        (tm, tn), NN, epilogue)[0]


def mm_plain(a, w, name, extras, extra_kinds, epilogue, tn=512):
    m, kdim = a.shape
    n = w.shape[1]
    tm, tk = min(m, MM_TM), min(kdim, MM_TK)
    return _mm_call(
        name, a, w, extras, [_sds((m, n), F32)], (m // tm, n // tn, kdim // tk),
        pl.BlockSpec((tm, tk), lambda i, j, k: (i, k)),
        pl.BlockSpec((tk, tn), lambda i, j, k: (k, j)),
        _extra_specs(extra_kinds, tm, tn), [pl.BlockSpec((tm, tn), lambda i, j, k: (i, j))],
        (tm, tn), NN, epilogue)[0]


def mm_nt_cols_sharded(dy, wg, layer, name):
    m = dy.shape[0]
    kdim, n = wg.shape[-2], wg.shape[-1]
    tm, tn = min(m, MM_TM), min(kdim, MM_TN)
    return _mm_call(
        name, dy, wg, [], [_sds((m, kdim), F32)], (m // tm, kdim // tn, N_DEV),
        pl.BlockSpec((tm, n), lambda i, j, k: (i, k)),
        pl.BlockSpec((None, None, tn, n), lambda i, j, k: (k, layer, j, 0)),
        [], [pl.BlockSpec((tm, tn), lambda i, j, k: (i, j))],
        (tm, tn), NT, _ep_store(F32))[0]


def mm_nt_rows_sharded(dy, wg, layer, name, extras=(), epilogue=None, out_dtype=F32):
    m, n = dy.shape
    ks = wg.shape[-2]
    tm, tk = min(m, MM_TM), min(n, MM_TK)
    gps = max(1, MM_TN // ks)
    tn = gps * ks
    epilogue = _ep_store(out_dtype) if epilogue is None else epilogue
    return _mm_call(
        name, dy, wg, list(extras), [_sds((m, N_DEV * ks), out_dtype)],
        (m // tm, N_DEV // gps, n // tk),
        pl.BlockSpec((tm, tk), lambda i, j, k: (i, k)),
        pl.BlockSpec((gps, None, ks, tk), lambda i, j, k: (j, layer, 0, k)),
        [pl.BlockSpec((tm, tn), lambda i, j, k: (i, j))] * len(extras),
        [pl.BlockSpec((tm, tn), lambda i, j, k: (i, j))],
        (tm, tn), NT, epilogue)[0]


def mm_nt_plain(dy, w, name, tk):
    m, n = dy.shape
    kdim = w.shape[0]
    tm, tn = min(m, MM_TM), min(kdim, MM_TN)
    return _mm_call(
        name, dy, w, [], [_sds((m, kdim), F32)], (m // tm, kdim // tn, n // tk),
        pl.BlockSpec((tm, tk), lambda i, j, k: (i, k)),
        pl.BlockSpec((tn, tk), lambda i, j, k: (j, k)),
        [], [pl.BlockSpec((tm, tn), lambda i, j, k: (i, j))],
        (tm, tn), NT, _ep_store(F32))[0]


def mm_tn(a, dy, name, shard_cols=None, tn=MM_TN):
    t, kdim = a.shape
    n = dy.shape[1]
    tm, tk = min(kdim, MM_TM), min(t, MM_TK)
    if shard_cols is None:
        tn = min(tn, n)
        out = _sds((kdim, n), BF16)
        o_spec = pl.BlockSpec((tm, tn), lambda i, j, k: (i, j))
    else:
        tn = shard_cols
        out = _sds((n // tn, kdim, tn), BF16)
        o_spec = pl.BlockSpec((None, tm, tn), lambda i, j, k: (j, i, 0))
    return _mm_call(
        name, a, dy, [], [out], (kdim // tm, n // tn, t // tk),
        pl.BlockSpec((tk, tm), lambda i, j, k: (k, i)),
        pl.BlockSpec((tk, tn), lambda i, j, k: (k, j)),
        [], [o_spec], (tm, tn), TN, _ep_store(BF16))[0]


BAND_BLOCKS_PER_STEP = 4


def _band_mask(g, nk_prev_valid, max_dist):
    rows = lax.broadcasted_iota(jnp.int32, (g * BLK, 2 * BLK), 0) % BLK
    cols = lax.broadcasted_iota(jnp.int32, (g * BLK, 2 * BLK), 1)
    dist = rows + BLK - cols
    ok = (dist >= 0) & (dist <= max_dist)
    return ok & ((cols >= BLK) | nk_prev_valid)


def band_fwd(qkv, q0, k0, v0, hk, g, seg, max_dist, name, sink_rows=None, normalise=False):
    t, dh = qkv.shape[1], qkv.shape[2]
    nb = t // BLK
    rb = BAND_BLOCKS_PER_STEP // g if g < BAND_BLOCKS_PER_STEP else 1
    rows = rb * BLK
    scale = dh ** -0.5
    has_sink = sink_rows is not None

    def kern(*refs):
        if has_sink:
            q_ref, k_ref, v_ref, s_ref, num_ref, m_ref, *l_ref = refs
            sink = s_ref[...]
        else:
            q_ref, k_ref, v_ref, num_ref, m_ref, *l_ref = refs
        for r in range(rb):
            b = pl.program_id(1) * rb + r
            cur = pl.multiple_of(b * BLK, BLK)
            prev = pl.multiple_of(jnp.maximum(b - 1, 0) * BLK, BLK)
            here = slice(r * BLK, (r + 1) * BLK)
            q = q_ref[:, here, :].reshape(g * BLK, dh)
            kk = jnp.concatenate([k_ref[pl.ds(prev, BLK), :], k_ref[pl.ds(cur, BLK), :]], axis=0)
            vv = jnp.concatenate([v_ref[pl.ds(prev, BLK), :], v_ref[pl.ds(cur, BLK), :]], axis=0)
            s = lax.dot_general(q, kk, NT, preferred_element_type=F32) * scale
            s = jnp.where(_band_mask(g, (b % seg) != 0, max_dist), s, -jnp.inf)
            m = jnp.max(s, axis=-1, keepdims=True)
            if has_sink:
                m = jnp.maximum(m, sink)
            p = jnp.exp(s - m)
            l = jnp.sum(p, axis=-1, keepdims=True)
            if has_sink:
                l = l + jnp.exp(sink - m)
            num = jnp.dot(p.astype(BF16), vv, preferred_element_type=F32)
            if normalise:
                num_ref[:, here, :] = (num * (1.0 / l)).reshape(g, BLK, dh)
                m_ref[:, here, :] = (m + jnp.log(l)).reshape(g, BLK, 1)
            else:
                num_ref[:, here, :] = num.reshape(g, BLK, dh)
                m_ref[:, here, :] = m.reshape(g, BLK, 1)
                l_ref[0][:, here, :] = l.reshape(g, BLK, 1)

    in_specs = [pl.BlockSpec((g, rows, dh), lambda h, b: (q0 // g + h, b, 0)),
                pl.BlockSpec((None, t, dh), lambda h, b: (k0 + h, 0, 0)),
                pl.BlockSpec((None, t, dh), lambda h, b: (v0 + h, 0, 0))]
    args = [qkv, qkv, qkv]
    if has_sink:
        in_specs.append(pl.BlockSpec((None, g * BLK, 1), lambda h, b: (h, 0, 0)))
        args.append(sink_rows)
    hq = hk * g
    n_col = 1 if normalise else 2
    return pl.pallas_call(
        kern, name=name, grid=(hk, nb // rb), in_specs=in_specs,
        out_specs=[pl.BlockSpec((g, rows, dh), lambda h, b: (h, b, 0))]
        + [pl.BlockSpec((g, rows, 1), lambda h, b: (h, b, 0))] * n_col,
        out_shape=[_sds((hq, t, dh), F32)] + [_sds((hq, t, 1), F32)] * n_col,
        compiler_params=_params("parallel", "parallel"),
    )(*args)


def band_bwd(qkv, q0, k0, v0, do, lse, delta, hk, g, seg, max_dist, name, sink_rows=None,
             delta_from_o=False):
    t, dh = qkv.shape[1], qkv.shape[2]
    nb = t // BLK
    rb = BAND_BLOCKS_PER_STEP // g if g < BAND_BLOCKS_PER_STEP else 1
    scale = dh ** -0.5
    has_sink = sink_rows is not None

    def kern(*refs):
        if has_sink:
            (q_ref, k_ref, v_ref, do_ref, lse_ref, dl_ref, s_ref,
             dq_ref, dk_ref, dv_ref, ds_ref, sacc) = refs
            sink = s_ref[...]
        else:
            q_ref, k_ref, v_ref, do_ref, lse_ref, dl_ref, dq_ref, dk_ref, dv_ref = refs
        step = pl.program_id(1)

        @pl.when(step == 0)
        def _():
            dk_ref[...] = jnp.zeros_like(dk_ref)
            dv_ref[...] = jnp.zeros_like(dv_ref)
            if has_sink:
                sacc[...] = jnp.zeros_like(sacc)

        for r in range(rb):
            b = step * rb + r
            cur = pl.multiple_of(b * BLK, BLK)
            prev = pl.multiple_of(jnp.maximum(b - 1, 0) * BLK, BLK)
            here = slice(r * BLK, (r + 1) * BLK)
            q = q_ref[:, here, :].reshape(g * BLK, dh)
            dout = do_ref[:, here, :].reshape(g * BLK, dh)
            lse_b = lse_ref[:, here, :].reshape(g * BLK, 1)
            if delta_from_o:
                dl_b = jnp.sum(dl_ref[:, here, :].reshape(g * BLK, dh) * dout, axis=-1,
                               keepdims=True)
                dout = dout.astype(BF16)
            else:
                dl_b = dl_ref[:, here, :].reshape(g * BLK, 1)
            kk = jnp.concatenate([k_ref[pl.ds(prev, BLK), :], k_ref[pl.ds(cur, BLK), :]], axis=0)
            vv = jnp.concatenate([v_ref[pl.ds(prev, BLK), :], v_ref[pl.ds(cur, BLK), :]], axis=0)
            s = lax.dot_general(q, kk, NT, preferred_element_type=F32) * scale
            s = jnp.where(_band_mask(g, (b % seg) != 0, max_dist), s, -jnp.inf)
            p = jnp.exp(s - lse_b)
            dp = lax.dot_general(dout, vv, NT, preferred_element_type=F32)
            ds = (p * (dp - dl_b) * scale).astype(BF16)
            dq = jnp.dot(ds, kk, preferred_element_type=F32)
            dq_ref[:, here, :] = dq.reshape(g, BLK, dh)
            dkk = lax.dot_general(ds, q, TN, preferred_element_type=F32)
            dvv = lax.dot_general(p.astype(BF16), dout, TN, preferred_element_type=F32)
            dk_ref[pl.ds(prev, BLK), :] += dkk[:BLK]
            dk_ref[pl.ds(cur, BLK), :] += dkk[BLK:]
            dv_ref[pl.ds(prev, BLK), :] += dvv[:BLK]
            dv_ref[pl.ds(cur, BLK), :] += dvv[BLK:]
            if has_sink:
                sacc[...] += -jnp.exp(sink - lse_b) * dl_b

        if has_sink:
            @pl.when(step == nb // rb - 1)
            def _():
                for gi in range(g):
                    ds_ref[gi:gi + 1, :] = jnp.sum(sacc[gi * BLK:(gi + 1) * BLK, :], axis=0,
                                                   keepdims=True)

    rows = rb * BLK
    in_specs = [pl.BlockSpec((g, rows, dh), lambda h, b: (q0 // g + h, b, 0)),
                pl.BlockSpec((None, t, dh), lambda h, b: (k0 + h, 0, 0)),
                pl.BlockSpec((None, t, dh), lambda h, b: (v0 + h, 0, 0)),
                pl.BlockSpec((g, rows, dh), lambda h, b: (h, b, 0)),
                pl.BlockSpec((g, rows, 1), lambda h, b: (h, b, 0)),
                pl.BlockSpec((g, rows, dh if delta_from_o else 1), lambda h, b: (h, b, 0))]
    args = [qkv, qkv, qkv, do, lse, delta]
    hq = hk * g
    out_specs = [pl.BlockSpec((g, rows, dh), lambda h, b: (h, b, 0)),
                 pl.BlockSpec((None, t, dh), lambda h, b: (h, 0, 0)),
                 pl.BlockSpec((None, t, dh), lambda h, b: (h, 0, 0))]
    out_shape = [_sds((hq, t, dh), F32), _sds((hk, t, dh), F32), _sds((hk, t, dh), F32)]
    scratch = []
    if has_sink:
        in_specs.append(pl.BlockSpec((None, g * BLK, 1), lambda h, b: (h, 0, 0)))
        args.append(sink_rows)
        out_specs.append(pl.BlockSpec((None, g, 1), lambda h, b: (h, 0, 0)))
        out_shape.append(_sds((hk, g, 1), F32))
        scratch.append(pltpu.VMEM((g * BLK, 1), F32))
    return pl.pallas_call(
        kern, name=name, grid=(hk, nb // rb), in_specs=in_specs, out_specs=out_specs,
        out_shape=out_shape,
        scratch_shapes=scratch, compiler_params=_params("parallel", "arbitrary"),
    )(*args)


def merge_branches(nums, ms, ls, name):
    h, t, dh = nums[0].shape
    nbr = len(nums)

    def kern(*refs):
        num_refs, m_refs, l_refs = refs[:nbr], refs[nbr:2 * nbr], refs[2 * nbr:3 * nbr]
        o_ref, lse_ref = refs[3 * nbr], refs[3 * nbr + 1]
        mall = m_refs[0][...]
        for i in range(1, nbr):
            mall = jnp.maximum(mall, m_refs[i][...])
        num = jnp.zeros((t, dh), F32)
        den = jnp.zeros((t, 1), F32)
        for i in range(nbr):
            w = jnp.exp(m_refs[i][...] - mall)
            num = num + w * num_refs[i][...]
            den = den + w * l_refs[i][...]
        o_ref[...] = num / den
        lse_ref[...] = mall + jnp.log(den)

    big = pl.BlockSpec((None, t, dh), lambda i: (i, 0, 0))
    col = pl.BlockSpec((None, t, 1), lambda i: (i, 0, 0))
    return pl.pallas_call(
        kern, name=name, grid=(h,), in_specs=[big] * nbr + [col] * (2 * nbr),
        out_specs=[big, col], out_shape=[_sds((h, t, dh), F32), _sds((h, t, 1), F32)],
        compiler_params=_params("parallel"),
    )(*nums, *ms, *ls)


def normalise_heads(num, m, l, name):
    h, t, dh = num.shape

    def kern(num_ref, m_ref, l_ref, o_ref, lse_ref):
        lv = l_ref[...]
        o_ref[...] = num_ref[...] / lv
        lse_ref[...] = m_ref[...] + jnp.log(lv)

    big = pl.BlockSpec((None, t, dh), lambda i: (i, 0, 0))
    col = pl.BlockSpec((None, t, 1), lambda i: (i, 0, 0))
    return pl.pallas_call(
        kern, name=name, grid=(h,), in_specs=[big, col, col], out_specs=[big, col],
        out_shape=[_sds((h, t, dh), F32), _sds((h, t, 1), F32)],
        compiler_params=_params("parallel"),
    )(num, m, l)


def head_delta(o, do, name):
    h, t, dh = o.shape

    def kern(o_ref, do_ref, d_ref):
        d_ref[...] = jnp.sum(o_ref[...] * do_ref[...], axis=-1, keepdims=True)

    big = pl.BlockSpec((None, t, dh), lambda i: (i, 0, 0))
    return pl.pallas_call(
        kern, name=name, grid=(h,), in_specs=[big, big],
        out_specs=pl.BlockSpec((None, t, 1), lambda i: (i, 0, 0)),
        out_shape=_sds((h, t, 1), F32), compiler_params=_params("parallel"),
    )(o, do)


def _dil_rb(nbl):
    return min(BAND_BLOCKS_PER_STEP, nbl)


def dilated_fwd(qkv, na, dil, max_dist, name):
    t, w3 = qkv.shape
    dh = A_DIM
    seq = t // dil
    nbl = seq // BLK
    rb = _dil_rb(nbl)
    rows = rb * BLK
    cb = w3 // dh
    scale = dh ** -0.5
    view = qkv.reshape(seq, dil * w3)

    def kern(q_ref, k_ref, v_ref, num_ref, m_ref, l_ref):
        for r in range(rb):
            b = pl.program_id(1) * rb + r
            cur = pl.multiple_of(b * BLK, BLK)
            prev = pl.multiple_of(jnp.maximum(b - 1, 0) * BLK, BLK)
            here = slice(r * BLK, (r + 1) * BLK)
            kk = jnp.concatenate([k_ref[pl.ds(prev, BLK), :], k_ref[pl.ds(cur, BLK), :]], axis=0)
            vv = jnp.concatenate([v_ref[pl.ds(prev, BLK), :], v_ref[pl.ds(cur, BLK), :]], axis=0)
            s = lax.dot_general(q_ref[here, :], kk, NT, preferred_element_type=F32) * scale
            s = jnp.where(_band_mask(1, b != 0, max_dist), s, -jnp.inf)
            m = jnp.max(s, axis=-1, keepdims=True)
            p = jnp.exp(s - m)
            l = jnp.sum(p, axis=-1, keepdims=True)
            num_ref[here, :] = jnp.dot(p.astype(BF16), vv, preferred_element_type=F32)
            m_ref[here, :] = jnp.broadcast_to(m, (BLK, dh))
            l_ref[here, :] = jnp.broadcast_to(l, (BLK, dh))

    def col(off):
        return lambda p, b: (0, (p // na) * cb + off * na + p % na)

    out_spec = pl.BlockSpec((rows, dh), lambda p, b: (b, p))
    out = _sds((seq, dil * na * dh), F32)
    outs = pl.pallas_call(
        kern, name=name, grid=(dil * na, nbl // rb),
        in_specs=[pl.BlockSpec((rows, dh), lambda p, b: (b, (p // na) * cb + p % na)),
                  pl.BlockSpec((seq, dh), col(1)), pl.BlockSpec((seq, dh), col(2))],
        out_specs=[out_spec] * 3, out_shape=[out] * 3,
        compiler_params=_params("parallel", "parallel"),
    )(view, view, view)
    return [o.reshape(t, na * dh) for o in outs]


def dilated_merge(nums, ms, ls, name):
    t, w = nums[0].shape
    nbr = len(nums)

    def body(i, ri, fi, ro, ao):
        mall = ri[nbr][...]
        for j in range(1, nbr):
            mall = jnp.maximum(mall, ri[nbr + j][...])
        num = jnp.zeros(mall.shape, F32)
        den = jnp.zeros(mall.shape, F32)
        for j in range(nbr):
            wgt = jnp.exp(ri[nbr + j][...] - mall)
            num = num + wgt * ri[j][...]
            den = den + wgt * ri[2 * nbr + j][...]
        o = num / den
        ro[0][...] = o
        ro[1][...] = o.astype(BF16)
        ro[2][...] = mall + jnp.log(den)

    return _rows_call(name, body, list(nums) + list(ms) + list(ls), [],
                      [_sds((t, w), F32), _sds((t, w), BF16), _sds((t, w), F32)], [], 256)


def dilated_delta(o, dmixed, name):
    t, w = o.shape

    def body(i, ri, fi, ro, ao):
        for j in range(w // A_DIM):
            cols = slice(j * A_DIM, (j + 1) * A_DIM)
            d = jnp.sum(ri[0][:, cols] * ri[1][:, cols], axis=-1, keepdims=True)
            ro[0][:, cols] = jnp.broadcast_to(d, (d.shape[0], A_DIM))

    return _rows_call(name, body, [o, dmixed], [], [_sds((t, w), F32)], [], 256)[0]


def dilated_bwd(qkv, dmixed, lse, delta, na, dil, max_dist, name):
    t, w3 = qkv.shape
    dh = A_DIM
    seq = t // dil
    nbl = seq // BLK
    rb = _dil_rb(nbl)
    rows = rb * BLK
    cb = w3 // dh
    db = dmixed.shape[1] // dh
    scale = dh ** -0.5
    view = qkv.reshape(seq, dil * w3)
    do_view = dmixed.reshape(seq, dil * dmixed.shape[1])
    lse_view = lse.reshape(seq, dil * na * dh)
    delta_view = delta.reshape(seq, dil * na * dh)

    def kern(q_ref, k_ref, v_ref, do_ref, lse_ref, dl_ref, dq_ref, dk_ref, dv_ref):
        step = pl.program_id(1)

        @pl.when(step == 0)
        def _():
            dk_ref[...] = jnp.zeros_like(dk_ref)
            dv_ref[...] = jnp.zeros_like(dv_ref)

        for r in range(rb):
            b = step * rb + r
            cur = pl.multiple_of(b * BLK, BLK)
            prev = pl.multiple_of(jnp.maximum(b - 1, 0) * BLK, BLK)
            here = slice(r * BLK, (r + 1) * BLK)
            q = q_ref[here, :]
            dout = do_ref[here, :].astype(BF16)
            kk = jnp.concatenate([k_ref[pl.ds(prev, BLK), :], k_ref[pl.ds(cur, BLK), :]], axis=0)
            vv = jnp.concatenate([v_ref[pl.ds(prev, BLK), :], v_ref[pl.ds(cur, BLK), :]], axis=0)
            s = lax.dot_general(q, kk, NT, preferred_element_type=F32) * scale
            s = jnp.where(_band_mask(1, b != 0, max_dist), s, -jnp.inf)
            p = jnp.exp(s - lse_ref[here, 0:1])
            dp = lax.dot_general(dout, vv, NT, preferred_element_type=F32)
            ds = (p * (dp - dl_ref[here, 0:1]) * scale).astype(BF16)
            dq_ref[here, :] = jnp.dot(ds, kk, preferred_element_type=F32)
            dkk = lax.dot_general(ds, q, TN, preferred_element_type=F32)
            dvv = lax.dot_general(p.astype(BF16), dout, TN, preferred_element_type=F32)
            dk_ref[pl.ds(prev, BLK), :] += dkk[:BLK]
            dk_ref[pl.ds(cur, BLK), :] += dkk[BLK:]
            dv_ref[pl.ds(prev, BLK), :] += dvv[:BLK]
            dv_ref[pl.ds(cur, BLK), :] += dvv[BLK:]

    def col(off):
        return lambda p, b: (0, (p // na) * cb + off * na + p % na)

    blk = pl.BlockSpec((rows, dh), lambda p, b: (b, p))
    whole = pl.BlockSpec((seq, dh), lambda p, b: (0, p))
    out = _sds((seq, dil * na * dh), F32)
    outs = pl.pallas_call(
        kern, name=name, grid=(dil * na, nbl // rb),
        in_specs=[pl.BlockSpec((rows, dh), lambda p, b: (b, (p // na) * cb + p % na)),
                  pl.BlockSpec((seq, dh), col(1)), pl.BlockSpec((seq, dh), col(2)),
                  pl.BlockSpec((rows, dh), lambda p, b: (b, (p // na) * db + p % na)), blk, blk],
        out_specs=[blk, whole, whole], out_shape=[out] * 3,
        compiler_params=_params("parallel", "arbitrary"),
    )(view, view, view, do_view, lse_view, delta_view)
    return [o.reshape(t, na * dh) for o in outs]


def rope_bwd_sum(dqs, dks, dvs, tabs, name):
    c, sp, sm, half = tabs
    t, w = dqs[0].shape
    nbr = len(dqs)
    tm = 256
    n_slab = w // LANES

    def kern(*refs):
        groups = [refs[:nbr], refs[nbr:2 * nbr], refs[2 * nbr:3 * nbr]]
        c_ref, sp_ref, sm_ref, o_ref = refs[3 * nbr:]
        cv, spv, smv = c_ref[...], sp_ref[...], sm_ref[...]
        for gi, group in enumerate(groups):
            for j in range(n_slab):
                cols = slice(j * LANES, (j + 1) * LANES)
                xs = group[0][:, cols]
                for ref in group[1:]:
                    xs = xs + ref[:, cols]
                if gi < 2:
                    xs = (xs * cv + pltpu.roll(xs * spv, LANES - half, 1)
                          + pltpu.roll(xs * smv, half, 1))
                o_ref[:, gi * w + j * LANES:gi * w + (j + 1) * LANES] = xs.astype(BF16)

    big = pl.BlockSpec((tm, w), lambda i: (i, 0))
    tab = pl.BlockSpec((tm, LANES), lambda i: (i, 0))
    return pl.pallas_call(
        kern, name=name, grid=(t // tm,), in_specs=[big] * (3 * nbr) + [tab] * 3,
        out_specs=pl.BlockSpec((tm, 3 * w), lambda i: (i, 0)), out_shape=_sds((t, 3 * w), BF16),
        compiler_params=_params("parallel"),
    )(*dqs, *dks, *dvs, c, sp, sm)


def _cumsum_rows(x, n, reverse=False):
    rows = lax.broadcasted_iota(jnp.int32, x.shape, 0)
    shift = 1
    while shift < n:
        if reverse:
            x = x + jnp.where(rows < n - shift, pltpu.roll(x, n - shift, 0), 0.0)
        else:
            x = x + jnp.where(rows >= shift, pltpu.roll(x, shift, 0), 0.0)
        shift *= 2
    return x


def _hgrn_gates(f, lb):
    sig = _sigmoid(f)
    gate = lb + (1.0 - lb) * sig
    return sig, gate


B_SUB = 16


def _dot3(a, b, dims):
    ah, bh = a.astype(BF16), b.astype(BF16)
    al = (a - ah.astype(F32)).astype(BF16)
    bl = (b - bh.astype(F32)).astype(BF16)
    dot = functools.partial(lax.dot_general, dimension_numbers=dims, preferred_element_type=F32)
    return dot(ah, bh) + dot(al, bh) + dot(ah, bl)


def _sub_scales(b, i):
    r0 = i * B_SUB
    beta = b[r0 - 1:r0, :]
    return jnp.exp(b[r0:r0 + B_SUB, :] - beta), jnp.exp(jnp.minimum(beta - b, 0.0))


def _hgrn_intra_attn(qq, kk, b):
    c = qq.shape[0]
    lane = lax.broadcasted_iota(jnp.int32, (B_SUB, c), 1)
    trow = lax.broadcasted_iota(jnp.int32, (B_SUB, B_DIM), 0)
    blocks = []
    for i in range(c // B_SUB):
        r0 = i * B_SUB
        qi, bi = qq[r0:r0 + B_SUB, :], b[r0:r0 + B_SUB, :]
        if i == 0:
            a_i = jnp.zeros((B_SUB, c), F32)
        else:
            eq, ek = _sub_scales(b, i)
            a_i = jnp.where(lane < r0, _dot3(qi * eq, kk * ek, NT), 0.0)
        for sl in range(B_SUB):
            s = r0 + sl
            e = jnp.exp(jnp.where(trow >= sl, bi - b[s:s + 1, :], -jnp.inf))
            col = jnp.sum(qi * kk[s:s + 1, :] * e, axis=1, keepdims=True)
            a_i = jnp.where(lane == s, col, a_i)
        blocks.append(a_i)
    return jnp.concatenate(blocks, axis=0)


def hgrn_fwd(proj, col0, nh, lb, gn, name):
    t = proj.shape[0]
    c = B_CHUNK
    nc = t // c
    scale = B_DIM ** -0.5

    def kern(q_ref, f_ref, i_ref, g_ref, lb_ref, gn_ref, out_ref, opre_ref, st_ref, a_ref, state):
        lbv = lb_ref[...]
        gnv = gn_ref[...]
        state[...] = jnp.zeros_like(state)

        def chunk(ci, carry):
            rows = pl.ds(pl.multiple_of(ci * c, c), c)
            _, gate = _hgrn_gates(f_ref[rows, :], lbv)
            kk = 1.0 - gate
            qb = q_ref[rows, :]
            qq = qb * _sigmoid(qb) * scale
            v = i_ref[rows, :]
            b = _cumsum_rows(jnp.log(gate), c)
            st = state[...]
            st_ref[ci] = st
            o_inter = lax.dot_general((qq * jnp.exp(b)).astype(BF16), st.astype(BF16), NT,
                                      preferred_element_type=F32)
            amat = _hgrn_intra_attn(qq, kk, b)
            a_ref[ci] = amat
            o = jnp.dot(amat.astype(BF16), v.astype(BF16), preferred_element_type=F32) + o_inter
            opre_ref[rows, :] = o
            bl = b[c - 1:c, :]
            state[...] = st * jnp.exp(bl) + lax.dot_general(
                v.astype(BF16), (kk * jnp.exp(bl - b)).astype(BF16), TN, preferred_element_type=F32)
            r = lax.rsqrt(jnp.mean(o * o, axis=-1, keepdims=True) + NORM_EPS)
            gb = g_ref[rows, :]
            out_ref[rows, :] = (o * r * gnv * (gb * _sigmoid(gb))).astype(BF16)
            return carry

        lax.fori_loop(0, nc, chunk, 0)

    def col(off):
        return pl.BlockSpec((t, B_DIM), lambda h: (0, col0 + off * nh + h))

    return pl.pallas_call(
        kern, name=name, grid=(nh,),
        in_specs=[col(0), col(1), col(2), col(3),
                  pl.BlockSpec((None, 1, B_DIM), lambda h: (h, 0, 0)),
                  pl.BlockSpec((1, B_DIM), lambda h: (0, 0))],
        out_specs=[pl.BlockSpec((t, B_DIM), lambda h: (0, h)),
                   pl.BlockSpec((t, B_DIM), lambda h: (0, h)),
                   pl.BlockSpec((None, nc, B_DIM, B_DIM), lambda h: (h, 0, 0, 0)),
                   pl.BlockSpec((None, nc, c, c), lambda h: (h, 0, 0, 0))],
        out_shape=[_sds((t, nh * B_DIM), BF16), _sds((t, nh * B_DIM), F32),
                   _sds((nh, nc, B_DIM, B_DIM), F32), _sds((nh, nc, c, c), F32)],
        scratch_shapes=[pltpu.VMEM((B_DIM, B_DIM), F32)],
        compiler_params=_params("parallel"),
    )(proj, proj, proj, proj, lb, gn)


def hgrn_bwd(proj, col0, nh, lb, gn, opre, states, amats, dout, dcol0, name):
    t = proj.shape[0]
    c = B_CHUNK
    nc = t // c
    scale = B_DIM ** -0.5
    nsub = c // B_SUB

    def kern(q_ref, f_ref, i_ref, g_ref, lb_ref, gn_ref, opre_ref, st_ref, a_ref, dout_ref,
             dq_ref, df_ref, di_ref, dg_ref, dgn_ref, dlb_ref, dstate, dksc):
        lbv = lb_ref[...]
        gnv = gn_ref[...]
        dstate[...] = jnp.zeros_like(dstate)
        dlb_ref[...] = jnp.zeros_like(dlb_ref)

        @pl.when(pl.program_id(0) == 0)
        def _():
            dgn_ref[...] = jnp.zeros_like(dgn_ref)

        srow = lax.broadcasted_iota(jnp.int32, (c, B_DIM), 0)
        lane = lax.broadcasted_iota(jnp.int32, (B_SUB, c), 1)
        trow = lax.broadcasted_iota(jnp.int32, (B_SUB, B_DIM), 0)
        arow = lax.broadcasted_iota(jnp.int32, (c, c), 0)
        alane = lax.broadcasted_iota(jnp.int32, (c, c), 1)

        def chunk(cj, carry):
            ci = nc - 1 - cj
            rows = pl.ds(pl.multiple_of(ci * c, c), c)
            f = f_ref[rows, :]
            sig, gate = _hgrn_gates(f, lbv)
            kk = 1.0 - gate
            qb = q_ref[rows, :]
            sq = _sigmoid(qb)
            qq = qb * sq * scale
            v = i_ref[rows, :]
            b = _cumsum_rows(jnp.log(gate), c)
            st0 = st_ref[ci]
            dst = dstate[...]
            o = opre_ref[rows, :]
            gb = g_ref[rows, :]
            sg = _sigmoid(gb)
            silu_g = gb * sg
            d_out = dout_ref[rows, :]
            r = lax.rsqrt(jnp.mean(o * o, axis=-1, keepdims=True) + NORM_EPS)
            y = o * r
            dg_ref[rows, :] = (d_out * y * gnv * (sg * (1.0 + gb * (1.0 - sg)))).astype(BF16)
            dyn = d_out * silu_g
            dgn_ref[...] += jnp.sum(dyn * y, axis=0, keepdims=True)
            dy = dyn * gnv
            do = r * (dy - y * jnp.mean(dy * y, axis=-1, keepdims=True))
            eb = jnp.exp(b)
            bl = b[c - 1:c, :]
            ebl = jnp.exp(bl - b)
            ebl_last = jnp.exp(bl)
            do_b = do.astype(BF16)
            dst_b = dst.astype(BF16)
            dq_inter = jnp.dot(do_b, st0.astype(BF16), preferred_element_type=F32) * eb
            dst0 = lax.dot_general(do_b, (qq * eb).astype(BF16), TN,
                                   preferred_element_type=F32) + dst * ebl_last
            dv_inter = lax.dot_general((kk * ebl).astype(BF16), dst_b, NT, preferred_element_type=F32)
            dk_inter = jnp.dot(v.astype(BF16), dst_b, preferred_element_type=F32) * ebl
            amat = a_ref[ci]
            v_b = v.astype(BF16)
            d_a = lax.dot_general(do_b, v_b, NT, preferred_element_type=F32)
            d_a = jnp.where(arow >= alane, d_a, 0.0)
            dv_intra = lax.dot_general(amat.astype(BF16), do_b, TN, preferred_element_type=F32)
            dk_pairs = jnp.zeros((c, B_DIM), F32)
            dq_blocks = []
            for i in range(nsub):
                r0 = i * B_SUB
                qi, bi = qq[r0:r0 + B_SUB, :], b[r0:r0 + B_SUB, :]
                da_i = d_a[r0:r0 + B_SUB, :]
                if i == 0:
                    dq_i = jnp.zeros((B_SUB, B_DIM), F32)
                else:
                    eq, ek = _sub_scales(b, i)
                    da_m = jnp.where(lane < r0, da_i, 0.0)
                    dq_i = _dot3(da_m, kk * ek, NN) * eq
                    dk_pairs = dk_pairs + _dot3(da_m, qi * eq, TN) * ek
                for sl in range(B_SUB):
                    s = r0 + sl
                    e = jnp.exp(jnp.where(trow >= sl, bi - b[s:s + 1, :], -jnp.inf))
                    dacol = jnp.sum(jnp.where(lane == s, da_i, 0.0), axis=1, keepdims=True)
                    w = dacol * e
                    dq_i = dq_i + w * kk[s:s + 1, :]
                    dksc[s:s + 1, :] = jnp.sum(w * qi, axis=0, keepdims=True)
                dq_blocks.append(dq_i)
            dq = jnp.concatenate(dq_blocks, axis=0) + dq_inter
            dk = dk_pairs + dksc[...] + dk_inter
            dv = dv_intra + dv_inter
            db = qq * dq - kk * dk
            extra = (jnp.sum(kk * dk_inter, axis=0, keepdims=True)
                     + ebl_last * jnp.sum(st0 * dst, axis=0, keepdims=True))
            db = db + jnp.where(srow == c - 1, extra, 0.0)
            dlog = _cumsum_rows(db, c, reverse=True)
            dgate = dlog / gate - dk
            df_ref[rows, :] = (dgate * (1.0 - lbv) * sig * (1.0 - sig)).astype(BF16)
            dlb_ref[...] += jnp.sum(dgate * (1.0 - sig), axis=0, keepdims=True)
            dq_ref[rows, :] = (dq * scale * (sq * (1.0 + qb * (1.0 - sq)))).astype(BF16)
            di_ref[rows, :] = dv.astype(BF16)
            dstate[...] = dst0
            return carry

        lax.fori_loop(0, nc, chunk, 0)

    def col(off):
        return pl.BlockSpec((t, B_DIM), lambda h: (0, col0 + off * nh + h))

    hcol = pl.BlockSpec((t, B_DIM), lambda h: (0, h))
    vec = pl.BlockSpec((None, 1, B_DIM), lambda h: (h, 0, 0))
    wide = _sds((t, nh * B_DIM), BF16)
    return pl.pallas_call(
        kern, name=name, grid=(nh,),
        in_specs=[col(0), col(1), col(2), col(3), vec,
                  pl.BlockSpec((1, B_DIM), lambda h: (0, 0)), hcol,
                  pl.BlockSpec((None, nc, B_DIM, B_DIM), lambda h: (h, 0, 0, 0)),
                  pl.BlockSpec((None, nc, c, c), lambda h: (h, 0, 0, 0)),
                  pl.BlockSpec((t, B_DIM), lambda h: (0, dcol0 + h))],
        out_specs=[hcol, hcol, hcol, hcol, pl.BlockSpec((1, B_DIM), lambda h: (0, 0)), vec],
        out_shape=[wide, wide, wide, wide, _sds((1, B_DIM), F32), _sds((nh, 1, B_DIM), F32)],
        scratch_shapes=[pltpu.VMEM((B_DIM, B_DIM), F32), pltpu.VMEM((c, B_DIM), F32)],
        compiler_params=_params("arbitrary"),
    )(proj, proj, proj, proj, lb, gn, opre, states, amats, dout)


def lower_bounds_fwd(raw, name):
    n, w = raw.shape

    def kern(raw_ref, lb_ref, soft_ref):
        r = raw_ref[...]
        mx = r[0:1]
        for i in range(1, n):
            mx = jnp.maximum(mx, r[i:i + 1])
        e = jnp.exp(r - mx)
        den = e[0:1]
        for i in range(1, n):
            den = den + e[i:i + 1]
        soft = e / den
        soft_ref[...] = soft
        run = soft[0:1]
        lb_ref[0:1, :] = run - soft[0:1]
        for i in range(1, n):
            run = run + soft[i:i + 1]
            lb_ref[i:i + 1, :] = run - soft[0:1]

    return pl.pallas_call(kern, name=name, out_shape=[_sds((n, w), F32), _sds((n, w), F32)])(raw)


def lower_bounds_bwd(soft, dlb, name):
    n, w = soft.shape

    def kern(soft_ref, dlb_ref, out_ref):
        s = soft_ref[...]
        d = dlb_ref[...]
        total = d[0:1]
        for i in range(1, n):
            total = total + d[i:i + 1]
        us = []
        tail = total
        for i in range(n):
            us.append(tail - total if i == 0 else tail)
            tail = tail - d[i:i + 1]
        dot = s[0:1] * us[0]
        for i in range(1, n):
            dot = dot + s[i:i + 1] * us[i]
        for i in range(n):
            out_ref[i:i + 1, :] = s[i:i + 1] * (us[i] - dot)

    return pl.pallas_call(kern, name=name, out_shape=_sds((n, w), F32))(soft, dlb)


def _row_tile(kdim, n):
    tk = 512
    while tk > 8 and tk * n > 256 * 1024:
        tk //= 2
    return min(kdim, tk)


def _adam_update(w, g, m, v):
    m2 = ADAM_B1 * m + (1.0 - ADAM_B1) * g
    v2 = ADAM_B2 * v + (1.0 - ADAM_B2) * (g * g)
    m_hat = m2 / (1.0 - ADAM_B1 ** ADAM_STEP)
    v_hat = v2 / (1.0 - ADAM_B2 ** ADAM_STEP)
    delta = -ADAM_LR * (m_hat / (jnp.sqrt(v_hat) + ADAM_EPS) + ADAM_WD * w)
    return delta, m2, v2


def adamw_small(w, g, m, v, name):
    def kern(w_ref, g_ref, m_ref, v_ref, d_ref, m2_ref, v2_ref):
        d, m2, v2 = _adam_update(w_ref[...], g_ref[...], m_ref[...], v_ref[...])
        d_ref[...] = d
        m2_ref[...] = m2
        v2_ref[...] = v2

    return pl.pallas_call(kern, name=name, out_shape=[_sds(w.shape, F32)] * 3)(w, g, m, v)


def adamw_big(parts, w, m, v, name):
    nl, kdim, n = w.shape
    tk = _row_tile(kdim, n)

    def kern(p_ref, w_ref, m_ref, v_ref, g_ref, d_ref, m2_ref, v2_ref):
        g = p_ref[0].astype(F32)
        for q in range(1, 4):
            g = g + p_ref[q].astype(F32)
        d, m2, v2 = _adam_update(w_ref[...], g, m_ref[...], v_ref[...])
        g_ref[...] = g
        d_ref[...] = d
        m2_ref[...] = m2
        v2_ref[...] = v2

    blk = pl.BlockSpec((None, tk, n), lambda l, i: (l, i, 0))
    return pl.pallas_call(
        kern, name=name, grid=(nl, kdim // tk),
        in_specs=[pl.BlockSpec((None, 4, tk, n), lambda l, i: (l, 0, i, 0)), blk, blk, blk],
        out_specs=[blk] * 4, out_shape=[_sds(w.shape, F32)] * 4,
        compiler_params=_params("parallel", "parallel"),
    )(parts, w, m, v)


def cast_bf16(w, name):
    nl, kdim, n = w.shape
    tk = _row_tile(kdim, n)

    def kern(w_ref, o_ref):
        o_ref[...] = w_ref[...].astype(BF16)

    blk = pl.BlockSpec((None, tk, n), lambda l, i: (l, i, 0))
    return pl.pallas_call(
        kern, name=name, grid=(nl, kdim // tk), in_specs=[blk], out_specs=blk,
        out_shape=_sds(w.shape, BF16), compiler_params=_params("parallel", "parallel"),
    )(w)


def pair_add(dw, r1, core, name):
    kdim, n = dw.shape[1], dw.shape[2]
    tk = _row_tile(kdim, n)

    def kern(c_ref, a_ref, b_ref, o_ref):
        o_ref[...] = (a_ref[...].astype(F32) + b_ref[...].astype(F32)).astype(BF16)

    grid_spec = pltpu.PrefetchScalarGridSpec(
        num_scalar_prefetch=1, grid=(4, kdim // tk),
        in_specs=[pl.BlockSpec((None, tk, n), lambda p, i, c: (2 * p + c[0], i, 0)),
                  pl.BlockSpec((None, tk, n), lambda p, i, c: (p, i, 0))],
        out_specs=pl.BlockSpec((None, tk, n), lambda p, i, c: (p, i, 0)))
    return pl.pallas_call(
        kern, name=name, grid_spec=grid_spec, out_shape=_sds((4, kdim, n), BF16),
        compiler_params=_params("parallel", "parallel"),
    )(core, dw, r1)


ANY = pl.BlockSpec(memory_space=pl.ANY)


def _place():
    x, y, c = lax.axis_index("x"), lax.axis_index("y"), lax.axis_index("c")
    chips = [(1 - x, y), (x, 1 - y), (1 - x, 1 - y)]
    return x, y, c, chips


def all_gather(shards, name):
    n = len(shards)

    def kern(*refs):
        ins, outs = refs[:n], refs[n:2 * n]
        send_sems, recv_sems, local_sems = refs[2 * n:]
        x, y, c, chips = _place()
        me, sib = (x, y, c), (x, y, 1 - c)

        def copy(t, k, block, to, src=None):
            px, py, pc = block
            dst = outs[t].at[4 * px + 2 * py + pc]
            return pltpu.make_async_remote_copy(
                src_ref=dst if src is None else src, dst_ref=dst,
                send_sem=send_sems.at[7 * t + k], recv_sem=recv_sems.at[7 * t + k],
                device_id=to, device_id_type=MESH)

        mine = [pltpu.make_async_copy(ins[t], outs[t].at[4 * x + 2 * y + c], local_sems.at[t])
                for t in range(n)]
        for cp in mine:
            cp.start()
        first = []
        for t in range(n):
            first.append(copy(t, 0, me, sib, src=ins[t]))
            first += [copy(t, 1 + j, me, (*chip, c), src=ins[t]) for j, chip in enumerate(chips)]
        for cp in first:
            cp.start()
        passed = []
        for t in range(n):
            for j, chip in enumerate(chips):
                copy(t, 1 + j, (*chip, c), me).wait_recv()
                fwd = copy(t, 4 + j, (*chip, c), sib)
                fwd.start()
                passed.append(fwd)
        for t in range(n):
            copy(t, 0, sib, me).wait_recv()
            for j, chip in enumerate(chips):
                copy(t, 4 + j, (*chip, 1 - c), me).wait_recv()
        for cp in first + passed:
            cp.wait_send()
        for cp in mine:
            cp.wait()

    return pl.pallas_call(
        kern, name=name, in_specs=[ANY] * n, out_specs=[ANY] * n,
        out_shape=[_sds((N_DEV,) + s.shape, s.dtype) for s in shards],
        scratch_shapes=[pltpu.SemaphoreType.DMA((7 * n,)), pltpu.SemaphoreType.DMA((7 * n,)),
                        pltpu.SemaphoreType.DMA((n,))],
    )(*shards)


HBM = pl.BlockSpec(memory_space=pltpu.HBM)
SEM = pl.BlockSpec(memory_space=pltpu.SEMAPHORE)
DATAFLOW = pltpu.SideEffectType.DATAFLOW_SIDE_EFFECTING


def _first_level_targets():
    x, y, c, chips = _place()
    return 4 * x + 2 * y + c, [(x, y, 1 - c)] + [(*chip, c) for chip in chips]


def gather_start(shards, after, name):
    n = len(shards)
    lands = [lax.empty((N_DEV,) + s.shape, s.dtype) for s in shards]

    def kern(*refs):
        ins, lnd = refs[:n], refs[n:2 * n]
        send_sems, recv_sems, local_sems = refs[2 * n + len(after):2 * n + len(after) + 3]
        token = refs[-1]
        me, targets = _first_level_targets()
        for t in range(n):
            pltpu.make_async_copy(ins[t], lnd[t].at[me], local_sems.at[t]).start()
            for k, to in enumerate(targets):
                pltpu.make_async_remote_copy(
                    src_ref=ins[t], dst_ref=lnd[t].at[me], send_sem=send_sems.at[4 * t + k],
                    recv_sem=recv_sems.at[4 * t + k], device_id=to, device_id_type=MESH).start()
        token[...] = jnp.zeros_like(token)

    args = [pltpu.with_memory_space_constraint(a, pltpu.HBM) for a in list(shards) + lands]
    return pl.pallas_call(
        kern, name=name,
        out_shape=(pltpu.SemaphoreType.DMA((4 * n,)), pltpu.SemaphoreType.DMA((4 * n,)),
                   pltpu.SemaphoreType.DMA((n,)),
                   *[pltpu.HBM(a.shape, a.dtype) for a in args], _sds((8, LANES), F32)),
        in_specs=[HBM] * (2 * n) + [ANY] * len(after),
        out_specs=(SEM, SEM, SEM, *[HBM] * (2 * n), pl.BlockSpec(memory_space=pltpu.VMEM)),
        input_output_aliases={i: 3 + i for i in range(2 * n)},
        compiler_params=pltpu.CompilerParams(has_side_effects=DATAFLOW),
    )(*args, *after)


def gather_wait(send_sems, recv_sems, local_sems, shards, lands, after, name):
    n = len(shards)

    def kern(*refs):
        ins, lnd = refs[:n], refs[n:2 * n]
        send_sems, recv_sems, local_sems = refs[2 * n:2 * n + 3]
        me, targets = _first_level_targets()
        for t in range(n):
            pltpu.make_async_copy(ins[t], lnd[t].at[me], local_sems.at[t]).wait()
            for k, to in enumerate(targets):
                cp = pltpu.make_async_remote_copy(
                    src_ref=ins[t], dst_ref=lnd[t].at[me], send_sem=send_sems.at[4 * t + k],
                    recv_sem=recv_sems.at[4 * t + k], device_id=to, device_id_type=MESH)
                cp.wait_send()
                cp.wait_recv()

    bufs = list(shards) + list(lands)
    return pl.pallas_call(
        kern, name=name, out_shape=tuple(pltpu.HBM(a.shape, a.dtype) for a in bufs),
        in_specs=[HBM] * (2 * n) + [SEM, SEM, SEM, ANY], out_specs=[HBM] * (2 * n),
        input_output_aliases={i: i for i in range(2 * n)},
        compiler_params=pltpu.CompilerParams(has_side_effects=DATAFLOW),
    )(*bufs, send_sems, recv_sems, local_sems, after)


def _forward_copies(lnd, send_sems, recv_sems):
    x, y, c, chips = _place()
    passed = []
    for t in range(len(lnd)):
        for j, (qx, qy) in enumerate(chips):
            block = lnd[t].at[4 * qx + 2 * qy + c]
            passed.append(pltpu.make_async_remote_copy(
                src_ref=block, dst_ref=block, send_sem=send_sems.at[3 * t + j],
                recv_sem=recv_sems.at[3 * t + j], device_id=(x, y, 1 - c), device_id_type=MESH))
    return passed


def forward_now(lands, name):
    n = len(lands)

    def kern(*refs):
        copies = _forward_copies(refs[n:2 * n], refs[2 * n], refs[2 * n + 1])
        for cp in copies:
            cp.start()
        for cp in copies:
            cp.wait_recv()
        for cp in copies:
            cp.wait_send()

    return pl.pallas_call(
        kern, name=name, in_specs=[ANY] * n, out_specs=[ANY] * n,
        out_shape=[_sds(a.shape, a.dtype) for a in lands],
        input_output_aliases={i: i for i in range(n)},
        scratch_shapes=[pltpu.SemaphoreType.DMA((3 * n,)), pltpu.SemaphoreType.DMA((3 * n,))],
    )(*lands)


def sibling_start(grads, name):
    n = len(grads)
    lands = [lax.empty((4,) + g.shape[1:], g.dtype) for g in grads]

    def kern(*refs):
        ins, lnd = refs[:n], refs[n:2 * n]
        send_sems, recv_sems = refs[2 * n], refs[2 * n + 1]
        x, y, c, _ = _place()
        for t in range(n):
            for p in range(4):
                pltpu.make_async_remote_copy(
                    src_ref=ins[t].at[2 * p + 1 - c], dst_ref=lnd[t].at[p],
                    send_sem=send_sems.at[4 * t + p], recv_sem=recv_sems.at[4 * t + p],
                    device_id=(x, y, 1 - c), device_id_type=MESH).start()
        refs[-1][...] = jnp.zeros_like(refs[-1])

    args = [pltpu.with_memory_space_constraint(a, pltpu.HBM) for a in list(grads) + lands]
    return pl.pallas_call(
        kern, name=name,
        out_shape=(pltpu.SemaphoreType.DMA((4 * n,)), pltpu.SemaphoreType.DMA((4 * n,)),
                   *[pltpu.HBM(a.shape, a.dtype) for a in args], _sds((8, LANES), F32)),
        in_specs=[HBM] * (2 * n),
        out_specs=(SEM, SEM, *[HBM] * (2 * n), pl.BlockSpec(memory_space=pltpu.VMEM)),
        input_output_aliases={i: 2 + i for i in range(2 * n)},
        compiler_params=pltpu.CompilerParams(has_side_effects=DATAFLOW),
    )(*args)


def sibling_wait(send_sems, recv_sems, grads, lands, after, name):
    n = len(grads)

    def kern(*refs):
        ins, lnd = refs[:n], refs[n:2 * n]
        send_sems, recv_sems = refs[2 * n], refs[2 * n + 1]
        x, y, c, _ = _place()
        for t in range(n):
            for p in range(4):
                cp = pltpu.make_async_remote_copy(
                    src_ref=ins[t].at[2 * p + 1 - c], dst_ref=lnd[t].at[p],
                    send_sem=send_sems.at[4 * t + p], recv_sem=recv_sems.at[4 * t + p],
                    device_id=(x, y, 1 - c), device_id_type=MESH)
                cp.wait_send()
                cp.wait_recv()

    bufs = list(grads) + list(lands)
    outs = pl.pallas_call(
        kern, name=name, out_shape=tuple(pltpu.HBM(a.shape, a.dtype) for a in bufs),
        in_specs=[HBM] * (2 * n) + [SEM, SEM, ANY], out_specs=[HBM] * (2 * n),
        input_output_aliases={i: i for i in range(2 * n)},
        compiler_params=pltpu.CompilerParams(has_side_effects=DATAFLOW),
    )(*bufs, send_sems, recv_sems, after)
    return outs[:n], outs[n:]


def forward_start(lands, name):
    n = len(lands)

    def kern(*refs):
        for cp in _forward_copies(refs[:n], refs[n], refs[n + 1]):
            cp.start()
        refs[-1][...] = jnp.zeros_like(refs[-1])

    return pl.pallas_call(
        kern, name=name,
        out_shape=(pltpu.SemaphoreType.DMA((3 * n,)), pltpu.SemaphoreType.DMA((3 * n,)),
                   *[pltpu.HBM(a.shape, a.dtype) for a in lands], _sds((8, LANES), F32)),
        in_specs=[HBM] * n,
        out_specs=(SEM, SEM, *[HBM] * n, pl.BlockSpec(memory_space=pltpu.VMEM)),
        input_output_aliases={i: 2 + i for i in range(n)},
        compiler_params=pltpu.CompilerParams(has_side_effects=DATAFLOW),
    )(*lands)


def forward_wait(send_sems, recv_sems, lands, after, name):
    n = len(lands)

    def kern(*refs):
        for cp in _forward_copies(refs[:n], refs[n], refs[n + 1]):
            cp.wait_send()
            cp.wait_recv()

    return pl.pallas_call(
        kern, name=name, out_shape=tuple(pltpu.HBM(a.shape, a.dtype) for a in lands),
        in_specs=[HBM] * n + [SEM, SEM, ANY], out_specs=[HBM] * n,
        input_output_aliases={i: i for i in range(n)},
        compiler_params=pltpu.CompilerParams(has_side_effects=DATAFLOW),
    )(*lands, send_sems, recv_sems, after)


def all_reduce_small(vec, name):
    r = vec.shape[0]

    def kern(v_ref, o_ref, buf, send_sems, recv_sems):
        x, y, c, _ = _place()
        me = 4 * x + 2 * y + c
        peers = [(x, y, 1 - c), (1 - x, y, c), (x, 1 - y, c), (1 - x, 1 - y, c),
                 (1 - x, y, 1 - c), (x, 1 - y, 1 - c), (1 - x, 1 - y, 1 - c)]
        buf[me] = v_ref[...]
        copies = []
        for k, peer in enumerate(peers):
            cp = pltpu.make_async_remote_copy(
                src_ref=v_ref, dst_ref=buf.at[me], send_sem=send_sems.at[k],
                recv_sem=recv_sems.at[k], device_id=peer, device_id_type=MESH)
            cp.start()
            copies.append(cp)
        for cp in copies:
            cp.wait_recv()
        for cp in copies:
            cp.wait_send()
        total = buf[0]
        for d in range(1, N_DEV):
            total = total + buf[d]
        o_ref[...] = total

    vm = pl.BlockSpec(memory_space=pltpu.VMEM)
    return pl.pallas_call(
        kern, name=name, in_specs=[vm], out_specs=vm, out_shape=_sds(vec.shape, F32),
        scratch_shapes=[pltpu.VMEM((N_DEV, r, LANES), F32), pltpu.SemaphoreType.DMA((7,)),
                        pltpu.SemaphoreType.DMA((7,))],
    )(vec)


def exchange_with_sibling(grads, name):
    n = len(grads)

    def kern(*refs):
        ins, outs = refs[:n], refs[n:2 * n]
        send_sems, recv_sems = refs[2 * n:]
        x, y, c, _ = _place()
        copies = []
        for t in range(n):
            for p in range(4):
                cp = pltpu.make_async_remote_copy(
                    src_ref=ins[t].at[2 * p + 1 - c], dst_ref=outs[t].at[p],
                    send_sem=send_sems.at[4 * t + p], recv_sem=recv_sems.at[4 * t + p],
                    device_id=(x, y, 1 - c), device_id_type=MESH)
                cp.start()
                copies.append(cp)
        for cp in copies:
            cp.wait_recv()
        for cp in copies:
            cp.wait_send()

    return pl.pallas_call(
        kern, name=name, in_specs=[ANY] * n, out_specs=[ANY] * n,
        out_shape=[_sds((4,) + g.shape[1:], g.dtype) for g in grads],
        scratch_shapes=[pltpu.SemaphoreType.DMA((4 * n,)), pltpu.SemaphoreType.DMA((4 * n,))],
    )(*grads)


def exchange_between_chips(partials, layers, kinds, name):
    n = len(partials)
    n_kind = max(kinds) + 1
    shapes = []
    for kd in range(n_kind):
        idx = [i for i in range(n) if kinds[i] == kd]
        nl = max(layers[i] for i in idx) + 1
        shapes.append(_sds((nl,) + partials[idx[0]].shape, partials[idx[0]].dtype))

    def kern(*refs):
        ins, outs = refs[:n], refs[n:n + n_kind]
        send_sems, recv_sems, local_sems = refs[n + n_kind:]
        x, y, c, chips = _place()
        mine = 2 * x + y
        local = []
        copies = []
        for t in range(n):
            dst = outs[kinds[t]].at[layers[t], mine]
            lc = pltpu.make_async_copy(ins[t].at[mine], dst, local_sems.at[t])
            lc.start()
            local.append(lc)
            for j, (qx, qy) in enumerate(chips):
                cp = pltpu.make_async_remote_copy(
                    src_ref=ins[t].at[2 * qx + qy], dst_ref=dst,
                    send_sem=send_sems.at[3 * t + j], recv_sem=recv_sems.at[3 * t + j],
                    device_id=(qx, qy, c), device_id_type=MESH)
                cp.start()
                copies.append(cp)
        for cp in copies:
            cp.wait_recv()
        for cp in copies:
            cp.wait_send()
        for lc in local:
            lc.wait()

    return pl.pallas_call(
        kern, name=name, in_specs=[ANY] * n, out_specs=[ANY] * n_kind, out_shape=shapes,
        scratch_shapes=[pltpu.SemaphoreType.DMA((3 * n,)), pltpu.SemaphoreType.DMA((3 * n,)),
                        pltpu.SemaphoreType.DMA((n,))],
    )(*partials)


def scatter_start(partials, name):
    n = len(partials)
    lands = [lax.empty(p.shape, p.dtype) for p in partials]

    def kern(*refs):
        ins, lnd = refs[:n], refs[n:2 * n]
        send_sems, recv_sems, local_sems = refs[2 * n:2 * n + 3]
        token = refs[-1]
        x, y, c, chips = _place()
        mine = 2 * x + y
        for t in range(n):
            pltpu.make_async_copy(ins[t].at[mine], lnd[t].at[mine], local_sems.at[t]).start()
            for j, (qx, qy) in enumerate(chips):
                pltpu.make_async_remote_copy(
                    src_ref=ins[t].at[2 * qx + qy], dst_ref=lnd[t].at[mine],
                    send_sem=send_sems.at[3 * t + j], recv_sem=recv_sems.at[3 * t + j],
                    device_id=(qx, qy, c), device_id_type=MESH).start()
        token[...] = jnp.zeros_like(token)

    args = [pltpu.with_memory_space_constraint(a, pltpu.HBM) for a in list(partials) + lands]
    return pl.pallas_call(
        kern, name=name,
        out_shape=(pltpu.SemaphoreType.DMA((3 * n,)), pltpu.SemaphoreType.DMA((3 * n,)),
                   pltpu.SemaphoreType.DMA((n,)),
                   *[pltpu.HBM(a.shape, a.dtype) for a in args], _sds((8, LANES), F32)),
        in_specs=[HBM] * (2 * n),
        out_specs=(SEM, SEM, SEM, *[HBM] * (2 * n), pl.BlockSpec(memory_space=pltpu.VMEM)),
        input_output_aliases={i: 3 + i for i in range(2 * n)},
        compiler_params=pltpu.CompilerParams(has_side_effects=DATAFLOW),
    )(*args)


def scatter_wait(send_sems, recv_sems, local_sems, partials, lands, after, name):
    n = len(partials)

    def kern(*refs):
        ins, lnd = refs[:n], refs[n:2 * n]
        send_sems, recv_sems, local_sems = refs[2 * n:2 * n + 3]
        x, y, c, chips = _place()
        mine = 2 * x + y
        for t in range(n):
            pltpu.make_async_copy(ins[t].at[mine], lnd[t].at[mine], local_sems.at[t]).wait()
            for j, (qx, qy) in enumerate(chips):
                cp = pltpu.make_async_remote_copy(
                    src_ref=ins[t].at[2 * qx + qy], dst_ref=lnd[t].at[mine],
                    send_sem=send_sems.at[3 * t + j], recv_sem=recv_sems.at[3 * t + j],
                    device_id=(qx, qy, c), device_id_type=MESH)
                cp.wait_send()
                cp.wait_recv()

    bufs = list(partials) + list(lands)
    outs = pl.pallas_call(
        kern, name=name, out_shape=tuple(pltpu.HBM(a.shape, a.dtype) for a in bufs),
        in_specs=[HBM] * (2 * n) + [SEM, SEM, SEM, ANY], out_specs=[HBM] * (2 * n),
        input_output_aliases={i: i for i in range(2 * n)},
        compiler_params=pltpu.CompilerParams(has_side_effects=DATAFLOW),
    )(*bufs, send_sems, recv_sems, local_sems, after)
    return outs[n:]


def adamw_layers(parts, w, m, v, name):
    nl, kdim, n = w.shape
    tk = _row_tile(kdim, n)

    def kern(*refs):
        p_refs = refs[:nl]
        w_ref, m_ref, v_ref, g_ref, d_ref, m2_ref, v2_ref = refs[nl:]
        for l in range(nl):
            @pl.when(pl.program_id(0) == l)
            def _():
                g = p_refs[l][0].astype(F32)
                for q in range(1, 4):
                    g = g + p_refs[l][q].astype(F32)
                d, m2, v2 = _adam_update(w_ref[...], g, m_ref[...], v_ref[...])
                g_ref[...] = g
                d_ref[...] = d
                m2_ref[...] = m2
                v2_ref[...] = v2

    def part_spec(l):
        return pl.BlockSpec((4, tk, n), lambda li, i: (0, jnp.where(li == l, i, 0), 0))

    blk = pl.BlockSpec((None, tk, n), lambda li, i: (li, i, 0))
    return pl.pallas_call(
        kern, name=name, grid=(nl, kdim // tk),
        in_specs=[part_spec(l) for l in range(nl)] + [blk, blk, blk],
        out_specs=[blk] * 4, out_shape=[_sds(w.shape, F32)] * 4,
        compiler_params=_params("arbitrary", "arbitrary"),
    )(*parts, w, m, v)


def _pack(arrays):
    flat = jnp.concatenate([a.reshape(-1).astype(F32) for a in arrays])
    pad = (-flat.shape[0]) % (8 * LANES)
    return jnp.pad(flat, (0, pad)).reshape(-1, LANES)


def _unpack(packed, shapes):
    flat = packed.reshape(-1)
    out, off = [], 0
    for s in shapes:
        n = math.prod(s)
        out.append(flat[off:off + n].reshape(s))
        off += n
    return out


def _to_heads(x2d, dil, n_heads, dh):
    t = x2d.shape[0]
    return x2d.reshape(t // dil, dil, n_heads, dh).transpose(2, 1, 0, 3).reshape(n_heads, t, dh)


def _from_heads(xh, dil):
    h, t, w = xh.shape
    return xh.reshape(h, dil, t // dil, w).transpose(2, 1, 0, 3).reshape(t, h * w)


def _unperm(xh, dil):
    h, t, w = xh.shape
    return xh.reshape(h, dil, t // dil, w).transpose(0, 2, 1, 3).reshape(h, t, w)


def _perm(xh, dil):
    h, t, w = xh.shape
    return xh.reshape(h, t // dil, dil, w).transpose(0, 2, 1, 3).reshape(h, t, w)


def local_step(x, target, norm_mix_g, norm_mlp_g, final_norm_g, lbs, hgrn_norm_g, sinks,
               bq_full, bo_full, weights_get, weights_mid, grads_ready):
    t, d = x.shape
    depth = norm_mix_g.shape[0]
    na = d // 2 // A_DIM
    nbh = d // 2 // B_DIM
    nq = d // C_DIM
    nkv = nq // C_GROUP
    a_w = 3 * na * A_DIM
    c_w = (nq + 2 * nkv) * C_DIM
    tabs_a = rope_tables(t, A_DIM)
    tabs_c = rope_tables(t, C_DIM)
    saved = []
    for l in range(depth):
        s = {"x_in": x}
        (win_g, wout_g), token = weights_get(l, x)
        h = rms_fwd(x, norm_mix_g[l] + token, "norm_mix_fwd")
        s["h"] = h
        if l % 2 == 0:
            e = l // 2
            proj = mm_cols_sharded(h, win_g, 0, "even_in_proj")[0]
            qkv_r = rope_call(proj, tabs_a, a_w, 2 * na, False, "rope_a")[0]
            nums, ms, ls, hms = [], [], [], []
            for window, dil in A_BRANCHES:
                hm = _to_heads(qkv_r, dil, 3 * na, A_DIM)
                num, m, lsum = band_fwd(hm, 0, na, 2 * na, na, 1, t // dil // BLK, window // dil,
                                        f"dilated_fwd_{dil}")
                hms.append(hm)
                nums.append(_unperm(num, dil))
                ms.append(_unperm(m, dil))
                ls.append(_unperm(lsum, dil))
            oa, lse = merge_branches(nums, ms, ls, "dilated_merge")
            lb_e = lbs[e].reshape(nbh, 1, B_DIM)
            gn_e = hgrn_norm_g[e].reshape(1, B_DIM)
            ob, opre, states, amats = hgrn_fwd(proj, 3 * na, nbh, lb_e, gn_e, "hgrn_fwd")
            mixed = jnp.concatenate([_from_heads(oa, 1).astype(BF16), ob], axis=1)
            x = mm_rows_sharded(mixed, wout_g, 0, "even_out_proj", [x], ["tile"], _ep_residual)
            s.update(proj=proj, hms=hms, oa=oa, lse=lse, opre=opre, states=states, amats=amats,
                     mixed=mixed, lb=lb_e, gn=gn_e)
        else:
            o = l // 2
            wq = win_g[:, 0].transpose(1, 0, 2).reshape(d, c_w)
            proj = mm_plain(h, wq, "odd_qkv_proj", [bq_full[o].reshape(1, c_w)], ["row"], _ep_bias)
            qkv_r = rope_call(proj, tabs_c, c_w, (nq + nkv) * C_DIM // LANES, False, "rope_c")[0]
            hm = _to_heads(qkv_r, 1, nq + 2 * nkv, C_DIM)
            sink_rows = jnp.repeat(sinks[o].reshape(nkv, C_GROUP), BLK, axis=1).reshape(
                nkv, C_GROUP * BLK, 1)
            o_hm, lse = band_fwd(hm, 0, nq, nq + nkv, nkv, C_GROUP, t // BLK, C_WINDOW - 1,
                                 "swa_fwd", sink_rows=sink_rows, normalise=True)
            attn = _from_heads(o_hm, 1).astype(BF16)
            x = mm_rows_sharded(attn, wout_g, 0, "odd_out_proj", [bo_full[o].reshape(1, d), x],
                                ["row", "tile"], _ep_bias_residual)
            s.update(wq=wq, hm=hm, sink_rows=sink_rows, o_hm=o_hm, lse=lse, attn=attn)
        s["x_mid"] = x
        (w1_g, w2_g), token = weights_mid(l, x)
        s.update(win=win_g, wout=wout_g, w1=w1_g, w2=w2_g)
        h2 = rms_fwd(x, norm_mlp_g[l] + token, "norm_mlp_fwd")
        u, act = mm_cols_sharded(h2, w1_g, 0, "mlp_up", epilogue=_ep_relu2, n_out=2)
        x = mm_rows_sharded(act, w2_g, 0, "mlp_down", [x], ["tile"], _ep_residual)
        s.update(h2=h2, u=u, act=act)
        saved.append(s)

    dx, dxb, dg_final, loss_part = loss_head(x, final_norm_g, target, "loss_head")
    big = []
    small = {"final": dg_final, "loss": loss_part, "mix": [None] * depth, "mlp": [None] * depth,
             "lb": {}, "gn": {}, "sinks": {}, "bq": {}, "bo": {}}
    for l in reversed(range(depth)):
        s = saved[l]
        win_g, wout_g, w1_g, w2_g = s["win"], s["wout"], s["w1"], s["w2"]
        big.append(("w2", l, mm_tn(s["act"], dxb, "mlp_down_dw").reshape(N_DEV, -1, d)))
        du = mm_nt_rows_sharded(dxb, w2_g, 0, "mlp_down_dx", extras=[s["u"]],
                                epilogue=_ep_relu2_bwd, out_dtype=BF16)
        big.append(("w1", l, mm_tn(s["h2"], du, "mlp_up_dw", shard_cols=w1_g.shape[-1])))
        dh2 = mm_nt_cols_sharded(du, w1_g, 0, "mlp_up_dx")
        token = grads_ready(l, big[-2:])
        dx, dxb, dg, col_dx = rms_bwd(s["x_mid"], norm_mlp_g[l] + token, dh2, dx, "norm_mlp_bwd")
        small["mlp"][l] = dg
        if l % 2 == 0:
            e = l // 2
            big.append(("wout", e, mm_tn(s["mixed"], dxb, "even_out_dw").reshape(N_DEV, -1, d)))
            dmixed = mm_nt_rows_sharded(dxb, wout_g, 0, "even_out_dx")
            do_hm = _to_heads(dmixed[:, :na * A_DIM], 1, na, A_DIM)
            delta = head_delta(s["oa"], do_hm, "dilated_delta")
            dsum = None
            for (window, dil), hm in zip(A_BRANCHES, s["hms"]):
                dq, dk, dv = band_bwd(hm, 0, na, 2 * na, _perm(do_hm, dil).astype(BF16),
                                      _perm(s["lse"], dil), _perm(delta, dil), na, 1,
                                      t // dil // BLK, window // dil, f"dilated_bwd_{dil}")
                part = _from_heads(jnp.concatenate([dq, dk, dv], axis=0), dil)
                dsum = part if dsum is None else dsum + part
            dqkv_a = rope_call(dsum, tabs_a, a_w, 2 * na, True, "rope_a_bwd")[0]
            dqb, dfb, dib, dgb, dgn, dlb = hgrn_bwd(s["proj"], 3 * na, nbh, s["lb"], s["gn"],
                                                    s["opre"], s["states"], s["amats"], dmixed, na,
                                                    "hgrn_bwd")
            small["gn"][e] = dgn
            small["lb"][e] = dlb
            dproj = jnp.concatenate([dqkv_a, dqb, dfb, dib, dgb], axis=1)
            big.append(("win", e, mm_tn(s["h"], dproj, "even_in_dw", shard_cols=win_g.shape[-1])))
            dh = mm_nt_cols_sharded(dproj, win_g, 0, "even_in_dx")
        else:
            o = l // 2
            small["bo"][o] = col_dx
            big.append(("wo", o, mm_tn(s["attn"], dxb, "odd_out_dw").reshape(N_DEV, -1, d)))
            dattn = mm_nt_rows_sharded(dxb, wout_g, 0, "odd_out_dx")
            do_hm = _to_heads(dattn, 1, nq, C_DIM)
            dq, dk, dv, dsink = band_bwd(s["hm"], 0, nq, nq + nkv, do_hm, s["lse"], s["o_hm"],
                                         nkv, C_GROUP, t // BLK, C_WINDOW - 1, "swa_bwd",
                                         sink_rows=s["sink_rows"], delta_from_o=True)
            small["sinks"][o] = dsink
            dqkv = _from_heads(jnp.concatenate([dq, dk, dv], axis=0), 1)
            dproj, dbq = rope_call(dqkv, tabs_c, c_w, (nq + nkv) * C_DIM // LANES, True,
                                   "rope_c_bwd", col_sum=True)
            small["bq"][o] = dbq
            dwq = mm_tn(s["h"], dproj, "odd_qkv_dw", tn=512)
            big.append(("wqkv", o, dwq.reshape(d, N_DEV, -1).transpose(1, 0, 2)))
            dh = mm_nt_plain(dproj, s["wq"], "odd_qkv_dx", tk=c_w)
        token = grads_ready(l, big[-2:])
        dx, dxb, dg, _ = rms_bwd(s["x_in"], norm_mix_g[l] + token, dh, dx, "norm_mix_bwd")
        small["mix"][l] = dg
    return dx, small


def kernel(x, norm_mix_g, norm_mlp_g, final_norm_g, even_w_in, even_w_out, hgrn_lb_raw, hgrn_norm_g, odd_w_qkv, odd_b_qkv, odd_sinks, odd_w_o, odd_b_o, mlp_w1, mlp_w2, loss_target, m_norm_mix_g, m_norm_mlp_g, m_final_norm_g, m_even_w_in, m_even_w_out, m_hgrn_lb_raw, m_hgrn_norm_g, m_odd_w_qkv, m_odd_b_qkv, m_odd_sinks, m_odd_w_o, m_odd_b_o, m_mlp_w1, m_mlp_w2, v_norm_mix_g, v_norm_mlp_g, v_final_norm_g, v_even_w_in, v_even_w_out, v_hgrn_lb_raw, v_hgrn_norm_g, v_odd_w_qkv, v_odd_b_qkv, v_odd_sinks, v_odd_w_o, v_odd_b_o, v_mlp_w1, v_mlp_w2):
    d = x.shape[2]
    depth = norm_mix_g.shape[0]
    n_even, n_odd = even_w_in.shape[0], odd_w_qkv.shape[0]
    xi, yi, ci = lax.axis_index("x"), lax.axis_index("y"), lax.axis_index("c")
    dev = 4 * xi + 2 * yi + ci
    core = ci.astype(jnp.int32).reshape(1)

    big_w = {"win": even_w_in, "wout": even_w_out, "wqkv": odd_w_qkv, "wo": odd_w_o,
             "w1": mlp_w1, "w2": mlp_w2}
    big_m = {"win": m_even_w_in, "wout": m_even_w_out, "wqkv": m_odd_w_qkv, "wo": m_odd_w_o,
             "w1": m_mlp_w1, "w2": m_mlp_w2}
    big_v = {"win": v_even_w_in, "wout": v_even_w_out, "wqkv": v_odd_w_qkv, "wo": v_odd_w_o,
             "w1": v_mlp_w1, "w2": v_mlp_w2}
    kinds = list(big_w)
    casts = {k: cast_bf16(big_w[k], f"cast_{k}") for k in kinds}

    def layer_shards(l):
        a, b = ("win", "wout") if l % 2 == 0 else ("wqkv", "wo")
        return [casts[a][l // 2], casts[b][l // 2], casts["w1"][l], casts["w2"][l]]

    bq_w, bo_w = odd_b_qkv.shape[1], odd_b_o.shape[1]
    bq_mine = lax.dynamic_update_slice(jnp.zeros((n_odd, N_DEV * bq_w), F32), odd_b_qkv,
                                       (0, dev * bq_w))
    bo_mine = lax.dynamic_update_slice(jnp.zeros((n_odd, N_DEV * bo_w), F32), odd_b_o,
                                       (0, dev * bo_w))
    biases = all_reduce_small(_pack([bq_mine, bo_mine]), "gather_biases")
    bq_full, bo_full = _unpack(biases, [bq_mine.shape, bo_mine.shape])

    first_level = {}
    second_level = {}
    ready = {}
    zero = jnp.zeros((), F32)

    def start_first_level(key, shards, after):
        started = gather_start(shards, after, f"gather_start_{key}")
        first_level[key] = started[:-1]
        return started[-1]

    def finish_first_level(key, after):
        send_sems, recv_sems, local_sems, *bufs = first_level.pop(key)
        n = len(bufs) // 2
        bufs = gather_wait(send_sems, recv_sems, local_sems, bufs[:n], bufs[n:], after,
                           f"gather_wait_{key}")
        return bufs[n:]

    def weights_get(l, after):
        if l == 0:
            shards = layer_shards(0)
            mixer = all_gather(shards[:2], "gather_layer_0_mixer")
            token = start_first_level("0_mlp", shards[2:], [mixer[0], biases])
            for ahead in range(1, min(depth, 3)):
                token = start_first_level(ahead, layer_shards(ahead), [token])
            return [g[:, None] for g in mixer], token[0, 0]
        if l == 1:
            gathered = forward_now(finish_first_level(1, after), "gather_forward_1")
        else:
            send_sems, recv_sems, *lands = second_level.pop(l)
            gathered = forward_wait(send_sems, recv_sems, lands, after,
                                    f"gather_forward_wait_{l}")
        ready[l] = gathered[2:]
        return [g[:, None] for g in gathered[:2]], zero

    def weights_mid(l, after):
        token = zero
        if l == 0:
            ready[0] = forward_now(finish_first_level("0_mlp", after), "gather_forward_0_mlp")
            order_after = ready[0][0]
        elif l + 1 < depth:
            started = forward_start(finish_first_level(l + 1, after),
                                    f"gather_forward_start_{l + 1}")
            second_level[l + 1] = started[:-1]
            token = started[-1][0, 0]
            order_after = started[-1]
        if l + 3 < depth:
            token = token + start_first_level(l + 3, layer_shards(l + 3), [order_after])[0, 0]
        return [g[:, None] for g in ready.pop(l)], token

    exchanging = []
    scattering = []

    def finish_exchange(after):
        tag, names, layer_idx, (send_sems, recv_sems, *bufs) = exchanging.pop()
        n = len(names)
        grads, received = sibling_wait(send_sems, recv_sems, bufs[:n], bufs[n:], after,
                                       f"scatter_d2d_wait_{tag}")
        partials = [pair_add(g, r, core, f"pair_add_{k}")
                    for k, g, r in zip(names, grads, received)]
        started = scatter_start(partials, f"scatter_start_{tag}")
        scattering.append((tag, names, layer_idx, started[:-1]))
        return started[-1][0, 0]

    def grads_ready(l, group):
        names = [k for k, _, _ in group]
        grads = [g for _, _, g in group]
        tag = f"{l}_{names[0]}"
        token = finish_exchange(grads[0]) if exchanging else zero
        started = sibling_start(grads, f"scatter_d2d_start_{tag}")
        exchanging.append((tag, names, [li for _, li, _ in group], started[:-1]))
        return token + started[-1][0, 0]

    lbs, soft = lower_bounds_fwd(hgrn_lb_raw, "lower_bounds")

    dx, small = local_step(x[0], loss_target[0], norm_mix_g, norm_mlp_g, final_norm_g, lbs,
                           hgrn_norm_g, odd_sinks, bq_full, bo_full, weights_get, weights_mid,
                           grads_ready)
    finish_exchange(dx)

    parts = ([small["mix"][l] for l in range(depth)] + [small["mlp"][l] for l in range(depth)]
             + [small["final"]] + [small["lb"][e] for e in range(n_even)]
             + [small["gn"][e] for e in range(n_even)] + [small["sinks"][o] for o in range(n_odd)]
             + [small["bq"][o] for o in range(n_odd)] + [small["bo"][o] for o in range(n_odd)]
             + [small["loss"]])
    shapes = ([(depth, d)] * 2 + [(d,), hgrn_lb_raw.shape, hgrn_norm_g.shape, odd_sinks.shape,
              (n_odd, N_DEV * bq_w), (n_odd, N_DEV * bo_w), (1, LANES)])
    g_mix, g_mlp, g_final, d_lbs, g_gn, g_sinks, g_bq_full, g_bo_full, loss_v = _unpack(
        all_reduce_small(_pack(parts), "reduce_small"), shapes)
    g_lb = lower_bounds_bwd(soft, d_lbs, "lower_bounds_bwd")
    g_bq = lax.dynamic_slice(g_bq_full, (0, dev * bq_w), (n_odd, bq_w))
    g_bo = lax.dynamic_slice(g_bo_full, (0, dev * bo_w), (n_odd, bo_w))
    loss = loss_v[0, 0]

    small_names = ["norm_mix_g", "norm_mlp_g", "final_norm_g", "hgrn_lb_raw", "hgrn_norm_g",
                   "odd_b_qkv", "odd_sinks", "odd_b_o"]
    small_w = [norm_mix_g, norm_mlp_g, final_norm_g, hgrn_lb_raw, hgrn_norm_g, odd_b_qkv,
               odd_sinks, odd_b_o]
    small_m = [m_norm_mix_g, m_norm_mlp_g, m_final_norm_g, m_hgrn_lb_raw, m_hgrn_norm_g,
               m_odd_b_qkv, m_odd_sinks, m_odd_b_o]
    small_v = [v_norm_mix_g, v_norm_mlp_g, v_final_norm_g, v_hgrn_lb_raw, v_hgrn_norm_g,
               v_odd_b_qkv, v_odd_sinks, v_odd_b_o]
    small_g = [g_mix, g_mlp, g_final, g_lb, g_gn, g_bq, g_sinks, g_bo]
    sshapes = [w.shape for w in small_w]
    sd, sm, sv = adamw_small(_pack(small_w), _pack(small_g), _pack(small_m), _pack(small_v),
                             "adamw_small")
    res = {}
    for name, g, dl, m2, v2 in zip(small_names, small_g, _unpack(sd, sshapes),
                                   _unpack(sm, sshapes), _unpack(sv, sshapes)):
        res[name] = (g.reshape(dl.shape), dl, m2, v2)

    landed = {k: [None] * big_w[k].shape[0] for k in kinds}
    for tag, names, layer_idx, (send_sems, recv_sems, local_sems, *bufs) in scattering:
        n = len(names)
        lands = scatter_wait(send_sems, recv_sems, local_sems, bufs[:n], bufs[n:], dx,
                             f"scatter_wait_{tag}")
        for k, li, land in zip(names, layer_idx, lands):
            landed[k][li] = land
    long_names = {"win": "even_w_in", "wout": "even_w_out", "wqkv": "odd_w_qkv", "wo": "odd_w_o",
                  "w1": "mlp_w1", "w2": "mlp_w2"}
    for k in kinds:
        res[long_names[k]] = tuple(adamw_layers(landed[k], big_w[k], big_m[k], big_v[k],
                                                f"adamw_{k}"))

    order = ["norm_mix_g", "norm_mlp_g", "final_norm_g", "even_w_in", "even_w_out", "hgrn_lb_raw",
             "hgrn_norm_g", "odd_w_qkv", "odd_b_qkv", "odd_sinks", "odd_w_o", "odd_b_o", "mlp_w1",
             "mlp_w2"]
    outs = [loss, dx[None]]
    for j in range(4):
        outs += [res[n][j] for n in order]
    return tuple(outs)
```

```python
import functools
import math

import jax
import jax.numpy as jnp
from jax import lax
from jax.experimental import pallas as pl
from jax.experimental.pallas import tpu as pltpu

F32 = jnp.float32
BF16 = jnp.bfloat16
MESH = pl.DeviceIdType.MESH

N_DEV = 8
NORM_EPS = 1e-5
ROPE_THETA = 500000.0
BLK = 128
A_DIM = 128
A_BRANCHES = ((128, 1), (512, 4), (2048, 16))
B_DIM = 128
B_CHUNK = 64
C_DIM = 64
C_GROUP = 8
C_WINDOW = 128
LANES = 128

ADAM_LR = 0.001
ADAM_B1 = 0.9
ADAM_B2 = 0.999
ADAM_EPS = 1e-08
ADAM_WD = 0.01
ADAM_STEP = 10

NN = (((1,), (0,)), ((), ()))
NT = (((1,), (1,)), ((), ()))
TN = (((0,), (0,)), ((), ()))


def _params(*sem):
    return pltpu.CompilerParams(dimension_semantics=sem)


def _sigmoid(x):
    return 1.0 / (1.0 + jnp.exp(-x))


def _rows_call(name, body, row_ins, full_ins, row_outs, acc_outs, tm):
    t = row_ins[0].shape[0]
    n_ri, n_fi, n_ro = len(row_ins), len(full_ins), len(row_outs)

    def kern(*refs):
        i = pl.program_id(0)
        body(i, refs[:n_ri], refs[n_ri:n_ri + n_fi],
             refs[n_ri + n_fi:n_ri + n_fi + n_ro], refs[n_ri + n_fi + n_ro:])

    def row_spec(shape):
        return pl.BlockSpec((tm,) + tuple(shape[1:]), lambda i: (i,) + (0,) * (len(shape) - 1))

    def full_spec(shape):
        return pl.BlockSpec(tuple(shape), lambda i: (0,) * len(shape))

    outs = pl.pallas_call(
        kern, name=name, grid=(t // tm,),
        in_specs=[row_spec(a.shape) for a in row_ins] + [full_spec(a.shape) for a in full_ins],
        out_specs=[row_spec(s.shape) for s in row_outs] + [full_spec(s.shape) for s in acc_outs],
        out_shape=list(row_outs) + list(acc_outs),
        compiler_params=_params("arbitrary" if acc_outs else "parallel"),
    )(*row_ins, *full_ins)
    return outs


def _sds(shape, dtype):
    return jax.ShapeDtypeStruct(tuple(shape), dtype)


def rms_fwd(x, g, name):
    t, d = x.shape

    def body(i, ri, fi, ro, ao):
        xv = ri[0][...]
        r = lax.rsqrt(jnp.mean(xv * xv, axis=-1, keepdims=True) + NORM_EPS)
        ro[0][...] = (xv * r * fi[0][...]).astype(BF16)

    return _rows_call(name, body, [x], [g.reshape(1, d)], [_sds((t, d), BF16)], [], 256)[0]


def rms_bwd(x, g, dh, dx_res, name):
    t, d = x.shape

    def body(i, ri, fi, ro, ao):
        xv, dhv, res = ri[0][...], ri[1][...], ri[2][...]
        gv = fi[0][...]
        r = lax.rsqrt(jnp.mean(xv * xv, axis=-1, keepdims=True) + NORM_EPS)
        gd = gv * dhv
        dx = res + r * gd - xv * (r * r * r) * jnp.mean(xv * gd, axis=-1, keepdims=True)
        ro[0][...] = dx
        ro[1][...] = dx.astype(BF16)

        @pl.when(i == 0)
        def _():
            ao[0][...] = jnp.zeros_like(ao[0])
            ao[1][...] = jnp.zeros_like(ao[1])

        ao[0][...] += jnp.sum(dhv * xv * r, axis=0, keepdims=True)
        ao[1][...] += jnp.sum(dx, axis=0, keepdims=True)

    return _rows_call(name, body, [x, dh, dx_res], [g.reshape(1, d)],
                      [_sds((t, d), F32), _sds((t, d), BF16)],
                      [_sds((1, d), F32), _sds((1, d), F32)], 256)


def loss_head(x, g, target, name):
    t, d = x.shape

    def body(i, ri, fi, ro, ao):
        xv, tg = ri[0][...], ri[1][...]
        gv = fi[0][...]
        r = lax.rsqrt(jnp.mean(xv * xv, axis=-1, keepdims=True) + NORM_EPS)
        e = xv * r * gv - tg
        dy = e * (1.0 / d)
        gd = gv * dy
        dx = r * gd - xv * (r * r * r) * jnp.mean(xv * gd, axis=-1, keepdims=True)
        ro[0][...] = dx
        ro[1][...] = dx.astype(BF16)

        @pl.when(i == 0)
        def _():
            ao[0][...] = jnp.zeros_like(ao[0])
            ao[1][...] = jnp.zeros_like(ao[1])

        ao[0][...] += jnp.sum(dy * xv * r, axis=0, keepdims=True)
        part = 0.5 * jnp.sum(jnp.mean(e * e, axis=-1, keepdims=True), axis=0, keepdims=True)
        ao[1][...] += jnp.broadcast_to(part, (1, LANES))

    return _rows_call(name, body, [x, target], [g.reshape(1, d)],
                      [_sds((t, d), F32), _sds((t, d), BF16)],
                      [_sds((1, d), F32), _sds((1, LANES), F32)], 256)


def rope_tables(seq, head_dim):
    rot = head_dim // 4
    half = rot // 2
    inv_freq = 1.0 / (ROPE_THETA ** (jnp.arange(0, rot, 2, dtype=F32) / rot))
    ang = jnp.arange(seq, dtype=F32)[:, None] * inv_freq[None, :]
    cos, sin = jnp.cos(ang), jnp.sin(ang)
    zeros = jnp.zeros((seq, head_dim - rot), F32)
    zh = jnp.zeros((seq, half), F32)
    c = jnp.concatenate([cos, cos, jnp.ones((seq, head_dim - rot), F32)], axis=-1)
    sp = jnp.concatenate([zh, sin, zeros], axis=-1)
    sm = jnp.concatenate([-sin, zh, zeros], axis=-1)
    rep = LANES // head_dim
    return jnp.tile(c, (1, rep)), jnp.tile(sp, (1, rep)), jnp.tile(sm, (1, rep)), half


def rope_call(x, tabs, width, n_rope, inverse, name, col_sum=False):
    c, sp, sm, half = tabs
    t = x.shape[0]
    tm = 256
    n_slab = width // LANES

    def kern(x_ref, c_ref, sp_ref, sm_ref, o_ref, *acc):
        cv, spv, smv = c_ref[...], sp_ref[...], sm_ref[...]
        for j in range(n_slab):
            xs = x_ref[:, j * LANES:(j + 1) * LANES].astype(F32)
            if j < n_rope:
                if inverse:
                    ys = (xs * cv + pltpu.roll(xs * spv, LANES - half, 1)
                          + pltpu.roll(xs * smv, half, 1))
                else:
                    ys = (xs * cv + pltpu.roll(xs, half, 1) * spv
                          + pltpu.roll(xs, LANES - half, 1) * smv)
            else:
                ys = xs
            o_ref[:, j * LANES:(j + 1) * LANES] = ys.astype(BF16)
            if col_sum:
                @pl.when(pl.program_id(0) == 0)
                def _():
                    acc[0][:, j * LANES:(j + 1) * LANES] = jnp.zeros((1, LANES), F32)
                acc[0][:, j * LANES:(j + 1) * LANES] += jnp.sum(ys, axis=0, keepdims=True)

    tab_spec = pl.BlockSpec((tm, LANES), lambda i: (i, 0))
    out_shape = [_sds((t, width), BF16)]
    out_specs = [pl.BlockSpec((tm, width), lambda i: (i, 0))]
    if col_sum:
        out_shape.append(_sds((1, width), F32))
        out_specs.append(pl.BlockSpec((1, width), lambda i: (0, 0)))
    return pl.pallas_call(
        kern, name=name, grid=(t // tm,),
        in_specs=[pl.BlockSpec((tm, width), lambda i: (i, 0)), tab_spec, tab_spec, tab_spec],
        out_specs=out_specs, out_shape=out_shape,
        compiler_params=_params("arbitrary" if col_sum else "parallel"),
    )(x, c, sp, sm)


def _mm_call(name, a, b, extras, out_shapes, grid, a_spec, b_spec, extra_specs, out_specs,
             acc_shape, dims, epilogue):
    n_ex, n_out = len(extras), len(out_shapes)
    nk = grid[2]

    def product(a_ref, b_ref):
        bv = b_ref[...]
        if bv.ndim == 3:
            bv = bv.reshape(bv.shape[0] * bv.shape[1], bv.shape[2])
        return lax.dot_general(a_ref[...].astype(BF16), bv.astype(BF16), dims,
                               preferred_element_type=F32)

    def kern(*refs):
        a_ref, b_ref = refs[0], refs[1]
        ex = refs[2:2 + n_ex]
        outs = refs[2 + n_ex:2 + n_ex + n_out]
        if nk == 1:
            epilogue(product(a_ref, b_ref), ex, outs)
            return
        acc = refs[-1]
        k = pl.program_id(2)

        @pl.when(k == 0)
        def _():
            acc[...] = product(a_ref, b_ref)

        @pl.when(k > 0)
        def _():
            acc[...] += product(a_ref, b_ref)

        @pl.when(k == nk - 1)
        def _():
            epilogue(acc[...], ex, outs)

    return pl.pallas_call(
        kern, name=name, grid=grid,
        in_specs=[a_spec, b_spec, *extra_specs], out_specs=out_specs, out_shape=out_shapes,
        scratch_shapes=[pltpu.VMEM(acc_shape, F32)] if nk > 1 else [],
        compiler_params=_params("parallel", "parallel", "arbitrary"),
    )(a, b, *extras)


def _ep_store(dtype):
    def ep(acc, ex, outs):
        outs[0][...] = acc.astype(dtype)
    return ep


def _ep_residual(acc, ex, outs):
    outs[0][...] = acc + ex[0][...]


def _ep_bias(acc, ex, outs):
    outs[0][...] = acc + ex[0][...]


def _ep_bias_residual(acc, ex, outs):
    outs[0][...] = acc + ex[0][...] + ex[1][...]


def _ep_relu2(acc, ex, outs):
    outs[0][...] = acc
    rl = jnp.maximum(acc, 0.0)
    outs[1][...] = (rl * rl).astype(BF16)


def _ep_relu2_bwd(acc, ex, outs):
    outs[0][...] = (acc * (2.0 * jnp.maximum(ex[0][...], 0.0))).astype(BF16)


MM_TM = 1024
MM_TN = 1024
MM_TK = 2048


def mm_cols_sharded(a, wg, layer, name, epilogue=None, n_out=1):
    m, kdim = a.shape
    n = wg.shape[-1]
    tm, tk = min(m, MM_TM), min(kdim, MM_TK)
    if epilogue is None:
        epilogue, outs = _ep_store(F32), [_sds((m, N_DEV * n), F32)]
    else:
        outs = [_sds((m, N_DEV * n), F32), _sds((m, N_DEV * n), BF16)][:n_out]
    return _mm_call(
        name, a, wg, [], outs, (m // tm, N_DEV, kdim // tk),
        pl.BlockSpec((tm, tk), lambda i, j, k: (i, k)),
        pl.BlockSpec((None, None, tk, n), lambda i, j, k: (j, layer, k, 0)),
        [], [pl.BlockSpec((tm, n), lambda i, j, k: (i, j))] * len(outs),
        (tm, n), NN, epilogue)


def _extra_specs(extra_kinds, tm, tn):
    specs = []
    for kind in extra_kinds:
        if kind == "row":
            specs.append(pl.BlockSpec((1, tn), lambda i, j, k: (0, j)))
        else:
            specs.append(pl.BlockSpec((tm, tn), lambda i, j, k: (i, j)))
    return specs


def mm_rows_sharded(a, wg, layer, name, extras, extra_kinds, epilogue):
    m, kdim = a.shape
    ks, n = wg.shape[-2], wg.shape[-1]
    tm, tn = min(m, MM_TM), min(n, MM_TN)
    gps = max(1, min(kdim, MM_TK) // ks)
    return _mm_call(
        name, a, wg, extras, [_sds((m, n), F32)], (m // tm, n // tn, N_DEV // gps),
        pl.BlockSpec((tm, gps * ks), lambda i, j, k: (i, k)),
        pl.BlockSpec((gps, None, ks, tn), lambda i, j, k: (k, layer, 0, j)),
        _extra_specs(extra_kinds, tm, tn), [pl.BlockSpec((tm, tn), lambda i, j, k: (i, j))],
        (tm, tn), NN, epilogue)[0]


def mm_plain(a, w, name, extras, extra_kinds, epilogue, tn=512):
    m, kdim = a.shape
    n = w.shape[1]
    tm, tk = min(m, MM_TM), min(kdim, MM_TK)
    return _mm_call(
        name, a, w, extras, [_sds((m, n), F32)], (m // tm, n // tn, kdim // tk),
        pl.BlockSpec((tm, tk), lambda i, j, k: (i, k)),
        pl.BlockSpec((tk, tn), lambda i, j, k: (k, j)),
        _extra_specs(extra_kinds, tm, tn), [pl.BlockSpec((tm, tn), lambda i, j, k: (i, j))],
        (tm, tn), NN, epilogue)[0]


def mm_nt_cols_sharded(dy, wg, layer, name):
    m = dy.shape[0]
    kdim, n = wg.shape[-2], wg.shape[-1]
    tm, tn = min(m, MM_TM), min(kdim, MM_TN)
    return _mm_call(
        name, dy, wg, [], [_sds((m, kdim), F32)], (m // tm, kdim // tn, N_DEV),
        pl.BlockSpec((tm, n), lambda i, j, k: (i, k)),
        pl.BlockSpec((None, None, tn, n), lambda i, j, k: (k, layer, j, 0)),
        [], [pl.BlockSpec((tm, tn), lambda i, j, k: (i, j))],
        (tm, tn), NT, _ep_store(F32))[0]


def mm_nt_rows_sharded(dy, wg, layer, name, extras=(), epilogue=None, out_dtype=F32):
    m, n = dy.shape
    ks = wg.shape[-2]
    tm, tk = min(m, MM_TM), min(n, MM_TK)
    gps = max(1, MM_TN // ks)
    tn = gps * ks
    epilogue = _ep_store(out_dtype) if epilogue is None else epilogue
    return _mm_call(
        name, dy, wg, list(extras), [_sds((m, N_DEV * ks), out_dtype)],
        (m // tm, N_DEV // gps, n // tk),
        pl.BlockSpec((tm, tk), lambda i, j, k: (i, k)),
        pl.BlockSpec((gps, None, ks, tk), lambda i, j, k: (j, layer, 0, k)),
        [pl.BlockSpec((tm, tn), lambda i, j, k: (i, j))] * len(extras),
        [pl.BlockSpec((tm, tn), lambda i, j, k: (i, j))],
        (tm, tn), NT, epilogue)[0]


def mm_nt_plain(dy, w, name, tk):
    m, n = dy.shape
    kdim = w.shape[0]
    tm, tn = min(m, MM_TM), min(kdim, MM_TN)
    return _mm_call(
        name, dy, w, [], [_sds((m, kdim), F32)], (m // tm, kdim // tn, n // tk),
        pl.BlockSpec((tm, tk), lambda i, j, k: (i, k)),
        pl.BlockSpec((tn, tk), lambda i, j, k: (j, k)),
        [], [pl.BlockSpec((tm, tn), lambda i, j, k: (i, j))],
        (tm, tn), NT, _ep_store(F32))[0]


def mm_tn(a, dy, name, shard_cols=None, tn=MM_TN):
    t, kdim = a.shape
    n = dy.shape[1]
    tm, tk = min(kdim, MM_TM), min(t, MM_TK)
    if shard_cols is None:
        tn = min(tn, n)
        out = _sds((kdim, n), BF16)
        o_spec = pl.BlockSpec((tm, tn), lambda i, j, k: (i, j))
    else:
        tn = shard_cols
        out = _sds((n // tn, kdim, tn), BF16)
        o_spec = pl.BlockSpec((None, tm, tn), lambda i, j, k: (j, i, 0))
    return _mm_call(
        name, a, dy, [], [out], (kdim // tm, n // tn, t // tk),
        pl.BlockSpec((tk, tm), lambda i, j, k: (k, i)),
        pl.BlockSpec((tk, tn), lambda i, j, k: (k, j)),
        [], [o_spec], (tm, tn), TN, _ep_store(BF16))[0]


BAND_BLOCKS_PER_STEP = 4
BAND_BLOCKS_PER_STEP_GROUPED = 2


def _band_mask(g, nk_prev_valid, max_dist):
    rows = lax.broadcasted_iota(jnp.int32, (g * BLK, 2 * BLK), 0) % BLK
    cols = lax.broadcasted_iota(jnp.int32, (g * BLK, 2 * BLK), 1)
    dist = rows + BLK - cols
    ok = (dist >= 0) & (dist <= max_dist)
    return ok & ((cols >= BLK) | nk_prev_valid)


def band_fwd(qkv, q0, k0, v0, hk, g, seg, max_dist, name, sink_rows=None, normalise=False):
    t, dh = qkv.shape[1], qkv.shape[2]
    nb = t // BLK
    rb = BAND_BLOCKS_PER_STEP if g == 1 else BAND_BLOCKS_PER_STEP_GROUPED
    rows = rb * BLK
    scale = dh ** -0.5
    has_sink = sink_rows is not None

    def kern(*refs):
        if has_sink:
            q_ref, k_ref, v_ref, s_ref, num_ref, m_ref, *l_ref = refs
            sink = s_ref[...]
        else:
            q_ref, k_ref, v_ref, num_ref, m_ref, *l_ref = refs
        for r in range(rb):
            b = pl.program_id(1) * rb + r
            cur = pl.multiple_of(b * BLK, BLK)
            prev = pl.multiple_of(jnp.maximum(b - 1, 0) * BLK, BLK)
            here = slice(r * BLK, (r + 1) * BLK)
            q = q_ref[:, here, :].reshape(g * BLK, dh)
            kk = jnp.concatenate([k_ref[pl.ds(prev, BLK), :], k_ref[pl.ds(cur, BLK), :]], axis=0)
            vv = jnp.concatenate([v_ref[pl.ds(prev, BLK), :], v_ref[pl.ds(cur, BLK), :]], axis=0)
            s = lax.dot_general(q, kk, NT, preferred_element_type=F32) * scale
            s = jnp.where(_band_mask(g, (b % seg) != 0, max_dist), s, -jnp.inf)
            m = jnp.max(s, axis=-1, keepdims=True)
            if has_sink:
                m = jnp.maximum(m, sink)
            p = jnp.exp(s - m)
            l = jnp.sum(p, axis=-1, keepdims=True)
            if has_sink:
                l = l + jnp.exp(sink - m)
            num = jnp.dot(p.astype(BF16), vv, preferred_element_type=F32)
            if normalise:
                num_ref[:, here, :] = (num * (1.0 / l)).reshape(g, BLK, dh)
                m_ref[:, here, :] = (m + jnp.log(l)).reshape(g, BLK, 1)
            else:
                num_ref[:, here, :] = num.reshape(g, BLK, dh)
                m_ref[:, here, :] = m.reshape(g, BLK, 1)
                l_ref[0][:, here, :] = l.reshape(g, BLK, 1)

    in_specs = [pl.BlockSpec((g, rows, dh), lambda h, b: (q0 // g + h, b, 0)),
                pl.BlockSpec((None, t, dh), lambda h, b: (k0 + h, 0, 0)),
                pl.BlockSpec((None, t, dh), lambda h, b: (v0 + h, 0, 0))]
    args = [qkv, qkv, qkv]
    if has_sink:
        in_specs.append(pl.BlockSpec((None, g * BLK, 1), lambda h, b: (h, 0, 0)))
        args.append(sink_rows)
    hq = hk * g
    n_col = 1 if normalise else 2
    return pl.pallas_call(
        kern, name=name, grid=(hk, nb // rb), in_specs=in_specs,
        out_specs=[pl.BlockSpec((g, rows, dh), lambda h, b: (h, b, 0))]
        + [pl.BlockSpec((g, rows, 1), lambda h, b: (h, b, 0))] * n_col,
        out_shape=[_sds((hq, t, dh), F32)] + [_sds((hq, t, 1), F32)] * n_col,
        compiler_params=_params("parallel", "parallel"),
    )(*args)


def band_bwd(qkv, q0, k0, v0, do, lse, delta, hk, g, seg, max_dist, name, sink_rows=None,
             delta_from_o=False):
    t, dh = qkv.shape[1], qkv.shape[2]
    nb = t // BLK
    rb = BAND_BLOCKS_PER_STEP if g == 1 else BAND_BLOCKS_PER_STEP_GROUPED
    scale = dh ** -0.5
    has_sink = sink_rows is not None

    def kern(*refs):
        if has_sink:
            (q_ref, k_ref, v_ref, do_ref, lse_ref, dl_ref, s_ref,
             dq_ref, dk_ref, dv_ref, ds_ref, sacc) = refs
            sink = s_ref[...]
        else:
            q_ref, k_ref, v_ref, do_ref, lse_ref, dl_ref, dq_ref, dk_ref, dv_ref = refs
        step = pl.program_id(1)

        @pl.when(step == 0)
        def _():
            dk_ref[...] = jnp.zeros_like(dk_ref)
            dv_ref[...] = jnp.zeros_like(dv_ref)
            if has_sink:
                sacc[...] = jnp.zeros_like(sacc)

        for r in range(rb):
            b = step * rb + r
            cur = pl.multiple_of(b * BLK, BLK)
            prev = pl.multiple_of(jnp.maximum(b - 1, 0) * BLK, BLK)
            here = slice(r * BLK, (r + 1) * BLK)
            q = q_ref[:, here, :].reshape(g * BLK, dh)
            dout = do_ref[:, here, :].reshape(g * BLK, dh)
            lse_b = lse_ref[:, here, :].reshape(g * BLK, 1)
            if delta_from_o:
                dl_b = jnp.sum(dl_ref[:, here, :].reshape(g * BLK, dh) * dout, axis=-1,
                               keepdims=True)
                dout = dout.astype(BF16)
            else:
                dl_b = dl_ref[:, here, :].reshape(g * BLK, 1)
            kk = jnp.concatenate([k_ref[pl.ds(prev, BLK), :], k_ref[pl.ds(cur, BLK), :]], axis=0)
            vv = jnp.concatenate([v_ref[pl.ds(prev, BLK), :], v_ref[pl.ds(cur, BLK), :]], axis=0)
            s = lax.dot_general(q, kk, NT, preferred_element_type=F32) * scale
            s = jnp.where(_band_mask(g, (b % seg) != 0, max_dist), s, -jnp.inf)
            p = jnp.exp(s - lse_b)
            dp = lax.dot_general(dout, vv, NT, preferred_element_type=F32)
            ds = (p * (dp - dl_b) * scale).astype(BF16)
            dq = jnp.dot(ds, kk, preferred_element_type=F32)
            dq_ref[:, here, :] = dq.reshape(g, BLK, dh)
            dkk = lax.dot_general(ds, q, TN, preferred_element_type=F32)
            dvv = lax.dot_general(p.astype(BF16), dout, TN, preferred_element_type=F32)
            dk_ref[pl.ds(prev, BLK), :] += dkk[:BLK]
            dk_ref[pl.ds(cur, BLK), :] += dkk[BLK:]
            dv_ref[pl.ds(prev, BLK), :] += dvv[:BLK]
            dv_ref[pl.ds(cur, BLK), :] += dvv[BLK:]
            if has_sink:
                sacc[...] += -jnp.exp(sink - lse_b) * dl_b

        if has_sink:
            @pl.when(step == nb // rb - 1)
            def _():
                for gi in range(g):
                    ds_ref[gi:gi + 1, :] = jnp.sum(sacc[gi * BLK:(gi + 1) * BLK, :], axis=0,
                                                   keepdims=True)

    rows = rb * BLK
    in_specs = [pl.BlockSpec((g, rows, dh), lambda h, b: (q0 // g + h, b, 0)),
                pl.BlockSpec((None, t, dh), lambda h, b: (k0 + h, 0, 0)),
                pl.BlockSpec((None, t, dh), lambda h, b: (v0 + h, 0, 0)),
                pl.BlockSpec((g, rows, dh), lambda h, b: (h, b, 0)),
                pl.BlockSpec((g, rows, 1), lambda h, b: (h, b, 0)),
                pl.BlockSpec((g, rows, dh if delta_from_o else 1), lambda h, b: (h, b, 0))]
    args = [qkv, qkv, qkv, do, lse, delta]
    hq = hk * g
    out_specs = [pl.BlockSpec((g, rows, dh), lambda h, b: (h, b, 0)),
                 pl.BlockSpec((None, t, dh), lambda h, b: (h, 0, 0)),
                 pl.BlockSpec((None, t, dh), lambda h, b: (h, 0, 0))]
    out_shape = [_sds((hq, t, dh), F32), _sds((hk, t, dh), F32), _sds((hk, t, dh), F32)]
    scratch = []
    if has_sink:
        in_specs.append(pl.BlockSpec((None, g * BLK, 1), lambda h, b: (h, 0, 0)))
        args.append(sink_rows)
        out_specs.append(pl.BlockSpec((None, g, 1), lambda h, b: (h, 0, 0)))
        out_shape.append(_sds((hk, g, 1), F32))
        scratch.append(pltpu.VMEM((g * BLK, 1), F32))
    return pl.pallas_call(
        kern, name=name, grid=(hk, nb // rb), in_specs=in_specs, out_specs=out_specs,
        out_shape=out_shape,
        scratch_shapes=scratch, compiler_params=_params("parallel", "arbitrary"),
    )(*args)


def merge_branches(nums, ms, ls, name):
    h, t, dh = nums[0].shape
    nbr = len(nums)

    def kern(*refs):
        num_refs, m_refs, l_refs = refs[:nbr], refs[nbr:2 * nbr], refs[2 * nbr:3 * nbr]
        o_ref, lse_ref = refs[3 * nbr], refs[3 * nbr + 1]
        mall = m_refs[0][...]
        for i in range(1, nbr):
            mall = jnp.maximum(mall, m_refs[i][...])
        num = jnp.zeros((t, dh), F32)
        den = jnp.zeros((t, 1), F32)
        for i in range(nbr):
            w = jnp.exp(m_refs[i][...] - mall)
            num = num + w * num_refs[i][...]
            den = den + w * l_refs[i][...]
        o_ref[...] = num / den
        lse_ref[...] = mall + jnp.log(den)

    big = pl.BlockSpec((None, t, dh), lambda i: (i, 0, 0))
    col = pl.BlockSpec((None, t, 1), lambda i: (i, 0, 0))
    return pl.pallas_call(
        kern, name=name, grid=(h,), in_specs=[big] * nbr + [col] * (2 * nbr),
        out_specs=[big, col], out_shape=[_sds((h, t, dh), F32), _sds((h, t, 1), F32)],
        compiler_params=_params("parallel"),
    )(*nums, *ms, *ls)


def normalise_heads(num, m, l, name):
    h, t, dh = num.shape

    def kern(num_ref, m_ref, l_ref, o_ref, lse_ref):
        lv = l_ref[...]
        o_ref[...] = num_ref[...] / lv
        lse_ref[...] = m_ref[...] + jnp.log(lv)

    big = pl.BlockSpec((None, t, dh), lambda i: (i, 0, 0))
    col = pl.BlockSpec((None, t, 1), lambda i: (i, 0, 0))
    return pl.pallas_call(
        kern, name=name, grid=(h,), in_specs=[big, col, col], out_specs=[big, col],
        out_shape=[_sds((h, t, dh), F32), _sds((h, t, 1), F32)],
        compiler_params=_params("parallel"),
    )(num, m, l)


def head_delta(o, do, name):
    h, t, dh = o.shape

    def kern(o_ref, do_ref, d_ref):
        d_ref[...] = jnp.sum(o_ref[...] * do_ref[...], axis=-1, keepdims=True)

    big = pl.BlockSpec((None, t, dh), lambda i: (i, 0, 0))
    return pl.pallas_call(
        kern, name=name, grid=(h,), in_specs=[big, big],
        out_specs=pl.BlockSpec((None, t, 1), lambda i: (i, 0, 0)),
        out_shape=_sds((h, t, 1), F32), compiler_params=_params("parallel"),
    )(o, do)


def _dil_rb(nbl):
    return min(BAND_BLOCKS_PER_STEP, nbl)


def dilated_fwd(qkv, na, dil, max_dist, name):
    t, w3 = qkv.shape
    dh = A_DIM
    seq = t // dil
    nbl = seq // BLK
    rb = _dil_rb(nbl)
    rows = rb * BLK
    cb = w3 // dh
    scale = dh ** -0.5
    view = qkv.reshape(seq, dil * w3)

    def kern(q_ref, k_ref, v_ref, num_ref, m_ref, l_ref):
        for r in range(rb):
            b = pl.program_id(1) * rb + r
            cur = pl.multiple_of(b * BLK, BLK)
            prev = pl.multiple_of(jnp.maximum(b - 1, 0) * BLK, BLK)
            here = slice(r * BLK, (r + 1) * BLK)
            kk = jnp.concatenate([k_ref[pl.ds(prev, BLK), :], k_ref[pl.ds(cur, BLK), :]], axis=0)
            vv = jnp.concatenate([v_ref[pl.ds(prev, BLK), :], v_ref[pl.ds(cur, BLK), :]], axis=0)
            s = lax.dot_general(q_ref[here, :], kk, NT, preferred_element_type=F32) * scale
            s = jnp.where(_band_mask(1, b != 0, max_dist), s, -jnp.inf)
            m = jnp.max(s, axis=-1, keepdims=True)
            p = jnp.exp(s - m)
            l = jnp.sum(p, axis=-1, keepdims=True)
            num_ref[here, :] = jnp.dot(p.astype(BF16), vv, preferred_element_type=F32)
            m_ref[here, :] = jnp.broadcast_to(m, (BLK, dh))
            l_ref[here, :] = jnp.broadcast_to(l, (BLK, dh))

    def col(off):
        return lambda p, b: (0, (p // na) * cb + off * na + p % na)

    out_spec = pl.BlockSpec((rows, dh), lambda p, b: (b, p))
    out = _sds((seq, dil * na * dh), F32)
    outs = pl.pallas_call(
        kern, name=name, grid=(dil * na, nbl // rb),
        in_specs=[pl.BlockSpec((rows, dh), lambda p, b: (b, (p // na) * cb + p % na)),
                  pl.BlockSpec((seq, dh), col(1)), pl.BlockSpec((seq, dh), col(2))],
        out_specs=[out_spec] * 3, out_shape=[out] * 3,
        compiler_params=_params("parallel", "parallel"),
    )(view, view, view)
    return [o.reshape(t, na * dh) for o in outs]


def dilated_merge(nums, ms, ls, name):
    t, w = nums[0].shape
    nbr = len(nums)

    def body(i, ri, fi, ro, ao):
        mall = ri[nbr][...]
        for j in range(1, nbr):
            mall = jnp.maximum(mall, ri[nbr + j][...])
        num = jnp.zeros(mall.shape, F32)
        den = jnp.zeros(mall.shape, F32)
        for j in range(nbr):
            wgt = jnp.exp(ri[nbr + j][...] - mall)
            num = num + wgt * ri[j][...]
            den = den + wgt * ri[2 * nbr + j][...]
        o = num / den
        ro[0][...] = o
        ro[1][...] = o.astype(BF16)
        ro[2][...] = mall + jnp.log(den)

    return _rows_call(name, body, list(nums) + list(ms) + list(ls), [],
                      [_sds((t, w), F32), _sds((t, w), BF16), _sds((t, w), F32)], [], 256)


def dilated_delta(o, dmixed, name):
    t, w = o.shape

    def body(i, ri, fi, ro, ao):
        for j in range(w // A_DIM):
            cols = slice(j * A_DIM, (j + 1) * A_DIM)
            d = jnp.sum(ri[0][:, cols] * ri[1][:, cols], axis=-1, keepdims=True)
            ro[0][:, cols] = jnp.broadcast_to(d, (d.shape[0], A_DIM))

    return _rows_call(name, body, [o, dmixed], [], [_sds((t, w), F32)], [], 256)[0]


def dilated_bwd(qkv, dmixed, lse, delta, na, dil, max_dist, name):
    t, w3 = qkv.shape
    dh = A_DIM
    seq = t // dil
    nbl = seq // BLK
    rb = _dil_rb(nbl)
    rows = rb * BLK
    cb = w3 // dh
    db = dmixed.shape[1] // dh
    scale = dh ** -0.5
    view = qkv.reshape(seq, dil * w3)
    do_view = dmixed.reshape(seq, dil * dmixed.shape[1])
    lse_view = lse.reshape(seq, dil * na * dh)
    delta_view = delta.reshape(seq, dil * na * dh)

    def kern(q_ref, k_ref, v_ref, do_ref, lse_ref, dl_ref, dq_ref, dk_ref, dv_ref):
        step = pl.program_id(1)

        @pl.when(step == 0)
        def _():
            dk_ref[...] = jnp.zeros_like(dk_ref)
            dv_ref[...] = jnp.zeros_like(dv_ref)

        for r in range(rb):
            b = step * rb + r
            cur = pl.multiple_of(b * BLK, BLK)
            prev = pl.multiple_of(jnp.maximum(b - 1, 0) * BLK, BLK)
            here = slice(r * BLK, (r + 1) * BLK)
            q = q_ref[here, :]
            dout = do_ref[here, :].astype(BF16)
            kk = jnp.concatenate([k_ref[pl.ds(prev, BLK), :], k_ref[pl.ds(cur, BLK), :]], axis=0)
            vv = jnp.concatenate([v_ref[pl.ds(prev, BLK), :], v_ref[pl.ds(cur, BLK), :]], axis=0)
            s = lax.dot_general(q, kk, NT, preferred_element_type=F32) * scale
            s = jnp.where(_band_mask(1, b != 0, max_dist), s, -jnp.inf)
            p = jnp.exp(s - lse_ref[here, 0:1])
            dp = lax.dot_general(dout, vv, NT, preferred_element_type=F32)
            ds = (p * (dp - dl_ref[here, 0:1]) * scale).astype(BF16)
            dq_ref[here, :] = jnp.dot(ds, kk, preferred_element_type=F32)
            dkk = lax.dot_general(ds, q, TN, preferred_element_type=F32)
            dvv = lax.dot_general(p.astype(BF16), dout, TN, preferred_element_type=F32)
            dk_ref[pl.ds(prev, BLK), :] += dkk[:BLK]
            dk_ref[pl.ds(cur, BLK), :] += dkk[BLK:]
            dv_ref[pl.ds(prev, BLK), :] += dvv[:BLK]
            dv_ref[pl.ds(cur, BLK), :] += dvv[BLK:]

    def col(off):
        return lambda p, b: (0, (p // na) * cb + off * na + p % na)

    blk = pl.BlockSpec((rows, dh), lambda p, b: (b, p))
    whole = pl.BlockSpec((seq, dh), lambda p, b: (0, p))
    out = _sds((seq, dil * na * dh), F32)
    outs = pl.pallas_call(
        kern, name=name, grid=(dil * na, nbl // rb),
        in_specs=[pl.BlockSpec((rows, dh), lambda p, b: (b, (p // na) * cb + p % na)),
                  pl.BlockSpec((seq, dh), col(1)), pl.BlockSpec((seq, dh), col(2)),
                  pl.BlockSpec((rows, dh), lambda p, b: (b, (p // na) * db + p % na)), blk, blk],
        out_specs=[blk, whole, whole], out_shape=[out] * 3,
        compiler_params=_params("parallel", "arbitrary"),
    )(view, view, view, do_view, lse_view, delta_view)
    return [o.reshape(t, na * dh) for o in outs]


def rope_bwd_sum(dqs, dks, dvs, tabs, name):
    c, sp, sm, half = tabs
    t, w = dqs[0].shape
    nbr = len(dqs)
    tm = 256
    n_slab = w // LANES

    def kern(*refs):
        groups = [refs[:nbr], refs[nbr:2 * nbr], refs[2 * nbr:3 * nbr]]
        c_ref, sp_ref, sm_ref, o_ref = refs[3 * nbr:]
        cv, spv, smv = c_ref[...], sp_ref[...], sm_ref[...]
        for gi, group in enumerate(groups):
            for j in range(n_slab):
                cols = slice(j * LANES, (j + 1) * LANES)
                xs = group[0][:, cols]
                for ref in group[1:]:
                    xs = xs + ref[:, cols]
                if gi < 2:
                    xs = (xs * cv + pltpu.roll(xs * spv, LANES - half, 1)
                          + pltpu.roll(xs * smv, half, 1))
                o_ref[:, gi * w + j * LANES:gi * w + (j + 1) * LANES] = xs.astype(BF16)

    big = pl.BlockSpec((tm, w), lambda i: (i, 0))
    tab = pl.BlockSpec((tm, LANES), lambda i: (i, 0))
    return pl.pallas_call(
        kern, name=name, grid=(t // tm,), in_specs=[big] * (3 * nbr) + [tab] * 3,
        out_specs=pl.BlockSpec((tm, 3 * w), lambda i: (i, 0)), out_shape=_sds((t, 3 * w), BF16),
        compiler_params=_params("parallel"),
    )(*dqs, *dks, *dvs, c, sp, sm)


def _cumsum_rows(x, n, reverse=False):
    rows = lax.broadcasted_iota(jnp.int32, x.shape, 0)
    shift = 1
    while shift < n:
        if reverse:
            x = x + jnp.where(rows < n - shift, pltpu.roll(x, n - shift, 0), 0.0)
        else:
            x = x + jnp.where(rows >= shift, pltpu.roll(x, shift, 0), 0.0)
        shift *= 2
    return x


def _hgrn_gates(f, lb):
    sig = _sigmoid(f)
    gate = lb + (1.0 - lb) * sig
    return sig, gate


B_SUB = 16


def _dot3(a, b, dims):
    ah, bh = a.astype(BF16), b.astype(BF16)
    al = (a - ah.astype(F32)).astype(BF16)
    bl = (b - bh.astype(F32)).astype(BF16)
    dot = functools.partial(lax.dot_general, dimension_numbers=dims, preferred_element_type=F32)
    return dot(ah, bh) + dot(al, bh) + dot(ah, bl)


def _sub_scales(b, i):
    r0 = i * B_SUB
    beta = b[r0 - 1:r0, :]
    return jnp.exp(b[r0:r0 + B_SUB, :] - beta), jnp.exp(jnp.minimum(beta - b, 0.0))


def _hgrn_intra_attn(qq, kk, b):
    c = qq.shape[0]
    lane = lax.broadcasted_iota(jnp.int32, (B_SUB, c), 1)
    trow = lax.broadcasted_iota(jnp.int32, (B_SUB, B_DIM), 0)
    blocks = []
    for i in range(c // B_SUB):
        r0 = i * B_SUB
        qi, bi = qq[r0:r0 + B_SUB, :], b[r0:r0 + B_SUB, :]
        if i == 0:
            a_i = jnp.zeros((B_SUB, c), F32)
        else:
            eq, ek = _sub_scales(b, i)
            a_i = jnp.where(lane < r0, _dot3(qi * eq, kk * ek, NT), 0.0)
        for sl in range(B_SUB):
            s = r0 + sl
            e = jnp.exp(jnp.where(trow >= sl, bi - b[s:s + 1, :], -jnp.inf))
            col = jnp.sum(qi * kk[s:s + 1, :] * e, axis=1, keepdims=True)
            a_i = jnp.where(lane == s, col, a_i)
        blocks.append(a_i)
    return jnp.concatenate(blocks, axis=0)


def hgrn_fwd(proj, col0, nh, lb, gn, name):
    t = proj.shape[0]
    c = B_CHUNK
    nc = t // c
    scale = B_DIM ** -0.5

    def kern(q_ref, f_ref, i_ref, g_ref, lb_ref, gn_ref, out_ref, opre_ref, st_ref, a_ref, state):
        lbv = lb_ref[...]
        gnv = gn_ref[...]
        state[...] = jnp.zeros_like(state)

        def chunk(ci, carry):
            rows = pl.ds(pl.multiple_of(ci * c, c), c)
            _, gate = _hgrn_gates(f_ref[rows, :], lbv)
            kk = 1.0 - gate
            qb = q_ref[rows, :]
            qq = qb * _sigmoid(qb) * scale
            v = i_ref[rows, :]
            b = _cumsum_rows(jnp.log(gate), c)
            st = state[...]
            st_ref[ci] = st
            o_inter = lax.dot_general((qq * jnp.exp(b)).astype(BF16), st.astype(BF16), NT,
                                      preferred_element_type=F32)
            amat = _hgrn_intra_attn(qq, kk, b)
            a_ref[ci] = amat
            o = jnp.dot(amat.astype(BF16), v.astype(BF16), preferred_element_type=F32) + o_inter
            opre_ref[rows, :] = o
            bl = b[c - 1:c, :]
            state[...] = st * jnp.exp(bl) + lax.dot_general(
                v.astype(BF16), (kk * jnp.exp(bl - b)).astype(BF16), TN, preferred_element_type=F32)
            r = lax.rsqrt(jnp.mean(o * o, axis=-1, keepdims=True) + NORM_EPS)
            gb = g_ref[rows, :]
            out_ref[rows, :] = (o * r * gnv * (gb * _sigmoid(gb))).astype(BF16)
            return carry

        lax.fori_loop(0, nc, chunk, 0)

    def col(off):
        return pl.BlockSpec((t, B_DIM), lambda h: (0, col0 + off * nh + h))

    return pl.pallas_call(
        kern, name=name, grid=(nh,),
        in_specs=[col(0), col(1), col(2), col(3),
                  pl.BlockSpec((None, 1, B_DIM), lambda h: (h, 0, 0)),
                  pl.BlockSpec((1, B_DIM), lambda h: (0, 0))],
        out_specs=[pl.BlockSpec((t, B_DIM), lambda h: (0, h)),
                   pl.BlockSpec((t, B_DIM), lambda h: (0, h)),
                   pl.BlockSpec((None, nc, B_DIM, B_DIM), lambda h: (h, 0, 0, 0)),
                   pl.BlockSpec((None, nc, c, c), lambda h: (h, 0, 0, 0))],
        out_shape=[_sds((t, nh * B_DIM), BF16), _sds((t, nh * B_DIM), F32),
                   _sds((nh, nc, B_DIM, B_DIM), F32), _sds((nh, nc, c, c), F32)],
        scratch_shapes=[pltpu.VMEM((B_DIM, B_DIM), F32)],
        compiler_params=_params("parallel"),
    )(proj, proj, proj, proj, lb, gn)


def hgrn_bwd(proj, col0, nh, lb, gn, opre, states, amats, dout, dcol0, name):
    t = proj.shape[0]
    c = B_CHUNK
    nc = t // c
    scale = B_DIM ** -0.5
    nsub = c // B_SUB

    def kern(q_ref, f_ref, i_ref, g_ref, lb_ref, gn_ref, opre_ref, st_ref, a_ref, dout_ref,
             dq_ref, df_ref, di_ref, dg_ref, dgn_ref, dlb_ref, dstate, dksc):
        lbv = lb_ref[...]
        gnv = gn_ref[...]
        dstate[...] = jnp.zeros_like(dstate)
        dlb_ref[...] = jnp.zeros_like(dlb_ref)

        @pl.when(pl.program_id(0) == 0)
        def _():
            dgn_ref[...] = jnp.zeros_like(dgn_ref)

        srow = lax.broadcasted_iota(jnp.int32, (c, B_DIM), 0)
        lane = lax.broadcasted_iota(jnp.int32, (B_SUB, c), 1)
        trow = lax.broadcasted_iota(jnp.int32, (B_SUB, B_DIM), 0)
        arow = lax.broadcasted_iota(jnp.int32, (c, c), 0)
        alane = lax.broadcasted_iota(jnp.int32, (c, c), 1)

        def chunk(cj, carry):
            ci = nc - 1 - cj
            rows = pl.ds(pl.multiple_of(ci * c, c), c)
            f = f_ref[rows, :]
            sig, gate = _hgrn_gates(f, lbv)
            kk = 1.0 - gate
            qb = q_ref[rows, :]
            sq = _sigmoid(qb)
            qq = qb * sq * scale
            v = i_ref[rows, :]
            b = _cumsum_rows(jnp.log(gate), c)
            st0 = st_ref[ci]
            dst = dstate[...]
            o = opre_ref[rows, :]
            gb = g_ref[rows, :]
            sg = _sigmoid(gb)
            silu_g = gb * sg
            d_out = dout_ref[rows, :]
            r = lax.rsqrt(jnp.mean(o * o, axis=-1, keepdims=True) + NORM_EPS)
            y = o * r
            dg_ref[rows, :] = (d_out * y * gnv * (sg * (1.0 + gb * (1.0 - sg)))).astype(BF16)
            dyn = d_out * silu_g
            dgn_ref[...] += jnp.sum(dyn * y, axis=0, keepdims=True)
            dy = dyn * gnv
            do = r * (dy - y * jnp.mean(dy * y, axis=-1, keepdims=True))
            eb = jnp.exp(b)
            bl = b[c - 1:c, :]
            ebl = jnp.exp(bl - b)
            ebl_last = jnp.exp(bl)
            do_b = do.astype(BF16)
            dst_b = dst.astype(BF16)
            dq_inter = jnp.dot(do_b, st0.astype(BF16), preferred_element_type=F32) * eb
            dst0 = lax.dot_general(do_b, (qq * eb).astype(BF16), TN,
                                   preferred_element_type=F32) + dst * ebl_last
            dv_inter = lax.dot_general((kk * ebl).astype(BF16), dst_b, NT, preferred_element_type=F32)
            dk_inter = jnp.dot(v.astype(BF16), dst_b, preferred_element_type=F32) * ebl
            amat = a_ref[ci]
            v_b = v.astype(BF16)
            d_a = lax.dot_general(do_b, v_b, NT, preferred_element_type=F32)
            d_a = jnp.where(arow >= alane, d_a, 0.0)
            dv_intra = lax.dot_general(amat.astype(BF16), do_b, TN, preferred_element_type=F32)
            dk_pairs = jnp.zeros((c, B_DIM), F32)
            dq_blocks = []
            for i in range(nsub):
                r0 = i * B_SUB
                qi, bi = qq[r0:r0 + B_SUB, :], b[r0:r0 + B_SUB, :]
                da_i = d_a[r0:r0 + B_SUB, :]
                if i == 0:
                    dq_i = jnp.zeros((B_SUB, B_DIM), F32)
                else:
                    eq, ek = _sub_scales(b, i)
                    da_m = jnp.where(lane < r0, da_i, 0.0)
                    dq_i = _dot3(da_m, kk * ek, NN) * eq
                    dk_pairs = dk_pairs + _dot3(da_m, qi * eq, TN) * ek
                for sl in range(B_SUB):
                    s = r0 + sl
                    e = jnp.exp(jnp.where(trow >= sl, bi - b[s:s + 1, :], -jnp.inf))
                    dacol = jnp.sum(jnp.where(lane == s, da_i, 0.0), axis=1, keepdims=True)
                    w = dacol * e
                    dq_i = dq_i + w * kk[s:s + 1, :]
                    dksc[s:s + 1, :] = jnp.sum(w * qi, axis=0, keepdims=True)
                dq_blocks.append(dq_i)
            dq = jnp.concatenate(dq_blocks, axis=0) + dq_inter
            dk = dk_pairs + dksc[...] + dk_inter
            dv = dv_intra + dv_inter
            db = qq * dq - kk * dk
            extra = (jnp.sum(kk * dk_inter, axis=0, keepdims=True)
                     + ebl_last * jnp.sum(st0 * dst, axis=0, keepdims=True))
            db = db + jnp.where(srow == c - 1, extra, 0.0)
            dlog = _cumsum_rows(db, c, reverse=True)
            dgate = dlog / gate - dk
            df_ref[rows, :] = (dgate * (1.0 - lbv) * sig * (1.0 - sig)).astype(BF16)
            dlb_ref[...] += jnp.sum(dgate * (1.0 - sig), axis=0, keepdims=True)
            dq_ref[rows, :] = (dq * scale * (sq * (1.0 + qb * (1.0 - sq)))).astype(BF16)
            di_ref[rows, :] = dv.astype(BF16)
            dstate[...] = dst0
            return carry

        lax.fori_loop(0, nc, chunk, 0)

    def col(off):
        return pl.BlockSpec((t, B_DIM), lambda h: (0, col0 + off * nh + h))

    hcol = pl.BlockSpec((t, B_DIM), lambda h: (0, h))
    vec = pl.BlockSpec((None, 1, B_DIM), lambda h: (h, 0, 0))
    wide = _sds((t, nh * B_DIM), BF16)
    return pl.pallas_call(
        kern, name=name, grid=(nh,),
        in_specs=[col(0), col(1), col(2), col(3), vec,
                  pl.BlockSpec((1, B_DIM), lambda h: (0, 0)), hcol,
                  pl.BlockSpec((None, nc, B_DIM, B_DIM), lambda h: (h, 0, 0, 0)),
                  pl.BlockSpec((None, nc, c, c), lambda h: (h, 0, 0, 0)),
                  pl.BlockSpec((t, B_DIM), lambda h: (0, dcol0 + h))],
        out_specs=[hcol, hcol, hcol, hcol, pl.BlockSpec((1, B_DIM), lambda h: (0, 0)), vec],
        out_shape=[wide, wide, wide, wide, _sds((1, B_DIM), F32), _sds((nh, 1, B_DIM), F32)],
        scratch_shapes=[pltpu.VMEM((B_DIM, B_DIM), F32), pltpu.VMEM((c, B_DIM), F32)],
        compiler_params=_params("arbitrary"),
    )(proj, proj, proj, proj, lb, gn, opre, states, amats, dout)


def lower_bounds_fwd(raw, name):
    n, w = raw.shape

    def kern(raw_ref, lb_ref, soft_ref):
        r = raw_ref[...]
        mx = r[0:1]
        for i in range(1, n):
            mx = jnp.maximum(mx, r[i:i + 1])
        e = jnp.exp(r - mx)
        den = e[0:1]
        for i in range(1, n):
            den = den + e[i:i + 1]
        soft = e / den
        soft_ref[...] = soft
        run = soft[0:1]
        lb_ref[0:1, :] = run - soft[0:1]
        for i in range(1, n):
            run = run + soft[i:i + 1]
            lb_ref[i:i + 1, :] = run - soft[0:1]

    return pl.pallas_call(kern, name=name, out_shape=[_sds((n, w), F32), _sds((n, w), F32)])(raw)


def lower_bounds_bwd(soft, dlb, name):
    n, w = soft.shape

    def kern(soft_ref, dlb_ref, out_ref):
        s = soft_ref[...]
        d = dlb_ref[...]
        total = d[0:1]
        for i in range(1, n):
            total = total + d[i:i + 1]
        us = []
        tail = total
        for i in range(n):
            us.append(tail - total if i == 0 else tail)
            tail = tail - d[i:i + 1]
        dot = s[0:1] * us[0]
        for i in range(1, n):
            dot = dot + s[i:i + 1] * us[i]
        for i in range(n):
            out_ref[i:i + 1, :] = s[i:i + 1] * (us[i] - dot)

    return pl.pallas_call(kern, name=name, out_shape=_sds((n, w), F32))(soft, dlb)


def _row_tile(kdim, n):
    tk = 512
    while tk > 8 and tk * n > 256 * 1024:
        tk //= 2
    return min(kdim, tk)


def _adam_update(w, g, m, v):
    m2 = ADAM_B1 * m + (1.0 - ADAM_B1) * g
    v2 = ADAM_B2 * v + (1.0 - ADAM_B2) * (g * g)
    m_hat = m2 / (1.0 - ADAM_B1 ** ADAM_STEP)
    v_hat = v2 / (1.0 - ADAM_B2 ** ADAM_STEP)
    delta = -ADAM_LR * (m_hat / (jnp.sqrt(v_hat) + ADAM_EPS) + ADAM_WD * w)
    return delta, m2, v2


def adamw_small(w, g, m, v, name):
    def kern(w_ref, g_ref, m_ref, v_ref, d_ref, m2_ref, v2_ref):
        d, m2, v2 = _adam_update(w_ref[...], g_ref[...], m_ref[...], v_ref[...])
        d_ref[...] = d
        m2_ref[...] = m2
        v2_ref[...] = v2

    return pl.pallas_call(kern, name=name, out_shape=[_sds(w.shape, F32)] * 3)(w, g, m, v)


def adamw_big(parts, w, m, v, name):
    nl, kdim, n = w.shape
    tk = _row_tile(kdim, n)

    def kern(p_ref, w_ref, m_ref, v_ref, g_ref, d_ref, m2_ref, v2_ref):
        g = p_ref[0].astype(F32)
        for q in range(1, 4):
            g = g + p_ref[q].astype(F32)
        d, m2, v2 = _adam_update(w_ref[...], g, m_ref[...], v_ref[...])
        g_ref[...] = g
        d_ref[...] = d
        m2_ref[...] = m2
        v2_ref[...] = v2

    blk = pl.BlockSpec((None, tk, n), lambda l, i: (l, i, 0))
    return pl.pallas_call(
        kern, name=name, grid=(nl, kdim // tk),
        in_specs=[pl.BlockSpec((None, 4, tk, n), lambda l, i: (l, 0, i, 0)), blk, blk, blk],
        out_specs=[blk] * 4, out_shape=[_sds(w.shape, F32)] * 4,
        compiler_params=_params("parallel", "parallel"),
    )(parts, w, m, v)


def cast_bf16(w, name):
    nl, kdim, n = w.shape
    tk = _row_tile(kdim, n)

    def kern(w_ref, o_ref):
        o_ref[...] = w_ref[...].astype(BF16)

    blk = pl.BlockSpec((None, tk, n), lambda l, i: (l, i, 0))
    return pl.pallas_call(
        kern, name=name, grid=(nl, kdim // tk), in_specs=[blk], out_specs=blk,
        out_shape=_sds(w.shape, BF16), compiler_params=_params("parallel", "parallel"),
    )(w)


def pair_add(dw, r1, core, name):
    kdim, n = dw.shape[1], dw.shape[2]
    tk = _row_tile(kdim, n)

    def kern(c_ref, a_ref, b_ref, o_ref):
        o_ref[...] = (a_ref[...].astype(F32) + b_ref[...].astype(F32)).astype(BF16)

    grid_spec = pltpu.PrefetchScalarGridSpec(
        num_scalar_prefetch=1, grid=(4, kdim // tk),
        in_specs=[pl.BlockSpec((None, tk, n), lambda p, i, c: (2 * p + c[0], i, 0)),
                  pl.BlockSpec((None, tk, n), lambda p, i, c: (p, i, 0))],
        out_specs=pl.BlockSpec((None, tk, n), lambda p, i, c: (p, i, 0)))
    return pl.pallas_call(
        kern, name=name, grid_spec=grid_spec, out_shape=_sds((4, kdim, n), BF16),
        compiler_params=_params("parallel", "parallel"),
    )(core, dw, r1)


ANY = pl.BlockSpec(memory_space=pl.ANY)


def _place():
    x, y, c = lax.axis_index("x"), lax.axis_index("y"), lax.axis_index("c")
    chips = [(1 - x, y), (x, 1 - y), (1 - x, 1 - y)]
    return x, y, c, chips


def all_gather(shards, name):
    n = len(shards)

    def kern(*refs):
        ins, outs = refs[:n], refs[n:2 * n]
        send_sems, recv_sems, local_sems = refs[2 * n:]
        x, y, c, chips = _place()
        me, sib = (x, y, c), (x, y, 1 - c)

        def copy(t, k, block, to, src=None):
            px, py, pc = block
            dst = outs[t].at[4 * px + 2 * py + pc]
            return pltpu.make_async_remote_copy(
                src_ref=dst if src is None else src, dst_ref=dst,
                send_sem=send_sems.at[7 * t + k], recv_sem=recv_sems.at[7 * t + k],
                device_id=to, device_id_type=MESH)

        mine = [pltpu.make_async_copy(ins[t], outs[t].at[4 * x + 2 * y + c], local_sems.at[t])
                for t in range(n)]
        for cp in mine:
            cp.start()
        first = []
        for t in range(n):
            first.append(copy(t, 0, me, sib, src=ins[t]))
            first += [copy(t, 1 + j, me, (*chip, c), src=ins[t]) for j, chip in enumerate(chips)]
        for cp in first:
            cp.start()
        passed = []
        for t in range(n):
            for j, chip in enumerate(chips):
                copy(t, 1 + j, (*chip, c), me).wait_recv()
                fwd = copy(t, 4 + j, (*chip, c), sib)
                fwd.start()
                passed.append(fwd)
        for t in range(n):
            copy(t, 0, sib, me).wait_recv()
            for j, chip in enumerate(chips):
                copy(t, 4 + j, (*chip, 1 - c), me).wait_recv()
        for cp in first + passed:
            cp.wait_send()
        for cp in mine:
            cp.wait()

    return pl.pallas_call(
        kern, name=name, in_specs=[ANY] * n, out_specs=[ANY] * n,
        out_shape=[_sds((N_DEV,) + s.shape, s.dtype) for s in shards],
        scratch_shapes=[pltpu.SemaphoreType.DMA((7 * n,)), pltpu.SemaphoreType.DMA((7 * n,)),
                        pltpu.SemaphoreType.DMA((n,))],
    )(*shards)


HBM = pl.BlockSpec(memory_space=pltpu.HBM)
SEM = pl.BlockSpec(memory_space=pltpu.SEMAPHORE)
DATAFLOW = pltpu.SideEffectType.DATAFLOW_SIDE_EFFECTING


def _first_level_targets():
    x, y, c, chips = _place()
    return 4 * x + 2 * y + c, [(x, y, 1 - c)] + [(*chip, c) for chip in chips]


def gather_start(shards, after, name):
    n = len(shards)
    lands = [lax.empty((N_DEV,) + s.shape, s.dtype) for s in shards]

    def kern(*refs):
        ins, lnd = refs[:n], refs[n:2 * n]
        send_sems, recv_sems, local_sems = refs[2 * n + len(after):2 * n + len(after) + 3]
        token = refs[-1]
        me, targets = _first_level_targets()
        for t in range(n):
            pltpu.make_async_copy(ins[t], lnd[t].at[me], local_sems.at[t]).start()
            for k, to in enumerate(targets):
                pltpu.make_async_remote_copy(
                    src_ref=ins[t], dst_ref=lnd[t].at[me], send_sem=send_sems.at[4 * t + k],
                    recv_sem=recv_sems.at[4 * t + k], device_id=to, device_id_type=MESH).start()
        token[...] = jnp.zeros_like(token)

    args = [pltpu.with_memory_space_constraint(a, pltpu.HBM) for a in list(shards) + lands]
    return pl.pallas_call(
        kern, name=name,
        out_shape=(pltpu.SemaphoreType.DMA((4 * n,)), pltpu.SemaphoreType.DMA((4 * n,)),
                   pltpu.SemaphoreType.DMA((n,)),
                   *[pltpu.HBM(a.shape, a.dtype) for a in args], _sds((8, LANES), F32)),
        in_specs=[HBM] * (2 * n) + [ANY] * len(after),
        out_specs=(SEM, SEM, SEM, *[HBM] * (2 * n), pl.BlockSpec(memory_space=pltpu.VMEM)),
        input_output_aliases={i: 3 + i for i in range(2 * n)},
        compiler_params=pltpu.CompilerParams(has_side_effects=DATAFLOW),
    )(*args, *after)


def gather_wait(send_sems, recv_sems, local_sems, shards, lands, after, name):
    n = len(shards)

    def kern(*refs):
        ins, lnd = refs[:n], refs[n:2 * n]
        send_sems, recv_sems, local_sems = refs[2 * n:2 * n + 3]
        me, targets = _first_level_targets()
        for t in range(n):
            pltpu.make_async_copy(ins[t], lnd[t].at[me], local_sems.at[t]).wait()
            for k, to in enumerate(targets):
                cp = pltpu.make_async_remote_copy(
                    src_ref=ins[t], dst_ref=lnd[t].at[me], send_sem=send_sems.at[4 * t + k],
                    recv_sem=recv_sems.at[4 * t + k], device_id=to, device_id_type=MESH)
                cp.wait_send()
                cp.wait_recv()

    bufs = list(shards) + list(lands)
    return pl.pallas_call(
        kern, name=name, out_shape=tuple(pltpu.HBM(a.shape, a.dtype) for a in bufs),
        in_specs=[HBM] * (2 * n) + [SEM, SEM, SEM, ANY], out_specs=[HBM] * (2 * n),
        input_output_aliases={i: i for i in range(2 * n)},
        compiler_params=pltpu.CompilerParams(has_side_effects=DATAFLOW),
    )(*bufs, send_sems, recv_sems, local_sems, after)


def _forward_copies(lnd, send_sems, recv_sems):
    x, y, c, chips = _place()
    passed = []
    for t in range(len(lnd)):
        for j, (qx, qy) in enumerate(chips):
            block = lnd[t].at[4 * qx + 2 * qy + c]
            passed.append(pltpu.make_async_remote_copy(
                src_ref=block, dst_ref=block, send_sem=send_sems.at[3 * t + j],
                recv_sem=recv_sems.at[3 * t + j], device_id=(x, y, 1 - c), device_id_type=MESH))
    return passed


def forward_now(lands, name):
    n = len(lands)

    def kern(*refs):
        copies = _forward_copies(refs[n:2 * n], refs[2 * n], refs[2 * n + 1])
        for cp in copies:
            cp.start()
        for cp in copies:
            cp.wait_recv()
        for cp in copies:
            cp.wait_send()

    return pl.pallas_call(
        kern, name=name, in_specs=[ANY] * n, out_specs=[ANY] * n,
        out_shape=[_sds(a.shape, a.dtype) for a in lands],
        input_output_aliases={i: i for i in range(n)},
        scratch_shapes=[pltpu.SemaphoreType.DMA((3 * n,)), pltpu.SemaphoreType.DMA((3 * n,))],
    )(*lands)


def sibling_start(grads, name):
    n = len(grads)
    lands = [lax.empty((4,) + g.shape[1:], g.dtype) for g in grads]

    def kern(*refs):
        ins, lnd = refs[:n], refs[n:2 * n]
        send_sems, recv_sems = refs[2 * n], refs[2 * n + 1]
        x, y, c, _ = _place()
        for t in range(n):
            for p in range(4):
                pltpu.make_async_remote_copy(
                    src_ref=ins[t].at[2 * p + 1 - c], dst_ref=lnd[t].at[p],
                    send_sem=send_sems.at[4 * t + p], recv_sem=recv_sems.at[4 * t + p],
                    device_id=(x, y, 1 - c), device_id_type=MESH).start()
        refs[-1][...] = jnp.zeros_like(refs[-1])

    args = [pltpu.with_memory_space_constraint(a, pltpu.HBM) for a in list(grads) + lands]
    return pl.pallas_call(
        kern, name=name,
        out_shape=(pltpu.SemaphoreType.DMA((4 * n,)), pltpu.SemaphoreType.DMA((4 * n,)),
                   *[pltpu.HBM(a.shape, a.dtype) for a in args], _sds((8, LANES), F32)),
        in_specs=[HBM] * (2 * n),
        out_specs=(SEM, SEM, *[HBM] * (2 * n), pl.BlockSpec(memory_space=pltpu.VMEM)),
        input_output_aliases={i: 2 + i for i in range(2 * n)},
        compiler_params=pltpu.CompilerParams(has_side_effects=DATAFLOW),
    )(*args)


def sibling_wait(send_sems, recv_sems, grads, lands, after, name):
    n = len(grads)

    def kern(*refs):
        ins, lnd = refs[:n], refs[n:2 * n]
        send_sems, recv_sems = refs[2 * n], refs[2 * n + 1]
        x, y, c, _ = _place()
        for t in range(n):
            for p in range(4):
                cp = pltpu.make_async_remote_copy(
                    src_ref=ins[t].at[2 * p + 1 - c], dst_ref=lnd[t].at[p],
                    send_sem=send_sems.at[4 * t + p], recv_sem=recv_sems.at[4 * t + p],
                    device_id=(x, y, 1 - c), device_id_type=MESH)
                cp.wait_send()
                cp.wait_recv()

    bufs = list(grads) + list(lands)
    outs = pl.pallas_call(
        kern, name=name, out_shape=tuple(pltpu.HBM(a.shape, a.dtype) for a in bufs),
        in_specs=[HBM] * (2 * n) + [SEM, SEM, ANY], out_specs=[HBM] * (2 * n),
        input_output_aliases={i: i for i in range(2 * n)},
        compiler_params=pltpu.CompilerParams(has_side_effects=DATAFLOW),
    )(*bufs, send_sems, recv_sems, after)
    return outs[:n], outs[n:]


def forward_start(lands, name):
    n = len(lands)

    def kern(*refs):
        for cp in _forward_copies(refs[:n], refs[n], refs[n + 1]):
            cp.start()
        refs[-1][...] = jnp.zeros_like(refs[-1])

    return pl.pallas_call(
        kern, name=name,
        out_shape=(pltpu.SemaphoreType.DMA((3 * n,)), pltpu.SemaphoreType.DMA((3 * n,)),
                   *[pltpu.HBM(a.shape, a.dtype) for a in lands], _sds((8, LANES), F32)),
        in_specs=[HBM] * n,
        out_specs=(SEM, SEM, *[HBM] * n, pl.BlockSpec(memory_space=pltpu.VMEM)),
        input_output_aliases={i: 2 + i for i in range(n)},
        compiler_params=pltpu.CompilerParams(has_side_effects=DATAFLOW),
    )(*lands)


def forward_wait(send_sems, recv_sems, lands, after, name):
    n = len(lands)

    def kern(*refs):
        for cp in _forward_copies(refs[:n], refs[n], refs[n + 1]):
            cp.wait_send()
            cp.wait_recv()

    return pl.pallas_call(
        kern, name=name, out_shape=tuple(pltpu.HBM(a.shape, a.dtype) for a in lands),
        in_specs=[HBM] * n + [SEM, SEM, ANY], out_specs=[HBM] * n,
        input_output_aliases={i: i for i in range(n)},
        compiler_params=pltpu.CompilerParams(has_side_effects=DATAFLOW),
    )(*lands, send_sems, recv_sems, after)


def all_reduce_small(vec, name):
    r = vec.shape[0]

    def kern(v_ref, o_ref, buf, send_sems, recv_sems):
        x, y, c, _ = _place()
        me = 4 * x + 2 * y + c
        peers = [(x, y, 1 - c), (1 - x, y, c), (x, 1 - y, c), (1 - x, 1 - y, c),
                 (1 - x, y, 1 - c), (x, 1 - y, 1 - c), (1 - x, 1 - y, 1 - c)]
        buf[me] = v_ref[...]
        copies = []
        for k, peer in enumerate(peers):
            cp = pltpu.make_async_remote_copy(
                src_ref=v_ref, dst_ref=buf.at[me], send_sem=send_sems.at[k],
                recv_sem=recv_sems.at[k], device_id=peer, device_id_type=MESH)
            cp.start()
            copies.append(cp)
        for cp in copies:
            cp.wait_recv()
        for cp in copies:
            cp.wait_send()
        total = buf[0]
        for d in range(1, N_DEV):
            total = total + buf[d]
        o_ref[...] = total

    vm = pl.BlockSpec(memory_space=pltpu.VMEM)
    return pl.pallas_call(
        kern, name=name, in_specs=[vm], out_specs=vm, out_shape=_sds(vec.shape, F32),
        scratch_shapes=[pltpu.VMEM((N_DEV, r, LANES), F32), pltpu.SemaphoreType.DMA((7,)),
                        pltpu.SemaphoreType.DMA((7,))],
    )(vec)


def exchange_with_sibling(grads, name):
    n = len(grads)

    def kern(*refs):
        ins, outs = refs[:n], refs[n:2 * n]
        send_sems, recv_sems = refs[2 * n:]
        x, y, c, _ = _place()
        copies = []
        for t in range(n):
            for p in range(4):
                cp = pltpu.make_async_remote_copy(
                    src_ref=ins[t].at[2 * p + 1 - c], dst_ref=outs[t].at[p],
                    send_sem=send_sems.at[4 * t + p], recv_sem=recv_sems.at[4 * t + p],
                    device_id=(x, y, 1 - c), device_id_type=MESH)
                cp.start()
                copies.append(cp)
        for cp in copies:
            cp.wait_recv()
        for cp in copies:
            cp.wait_send()

    return pl.pallas_call(
        kern, name=name, in_specs=[ANY] * n, out_specs=[ANY] * n,
        out_shape=[_sds((4,) + g.shape[1:], g.dtype) for g in grads],
        scratch_shapes=[pltpu.SemaphoreType.DMA((4 * n,)), pltpu.SemaphoreType.DMA((4 * n,))],
    )(*grads)


def exchange_between_chips(partials, layers, kinds, name):
    n = len(partials)
    n_kind = max(kinds) + 1
    shapes = []
    for kd in range(n_kind):
        idx = [i for i in range(n) if kinds[i] == kd]
        nl = max(layers[i] for i in idx) + 1
        shapes.append(_sds((nl,) + partials[idx[0]].shape, partials[idx[0]].dtype))

    def kern(*refs):
        ins, outs = refs[:n], refs[n:n + n_kind]
        send_sems, recv_sems, local_sems = refs[n + n_kind:]
        x, y, c, chips = _place()
        mine = 2 * x + y
        local = []
        copies = []
        for t in range(n):
            dst = outs[kinds[t]].at[layers[t], mine]
            lc = pltpu.make_async_copy(ins[t].at[mine], dst, local_sems.at[t])
            lc.start()
            local.append(lc)
            for j, (qx, qy) in enumerate(chips):
                cp = pltpu.make_async_remote_copy(
                    src_ref=ins[t].at[2 * qx + qy], dst_ref=dst,
                    send_sem=send_sems.at[3 * t + j], recv_sem=recv_sems.at[3 * t + j],
                    device_id=(qx, qy, c), device_id_type=MESH)
                cp.start()
                copies.append(cp)
        for cp in copies:
            cp.wait_recv()
        for cp in copies:
            cp.wait_send()
        for lc in local:
            lc.wait()

    return pl.pallas_call(
        kern, name=name, in_specs=[ANY] * n, out_specs=[ANY] * n_kind, out_shape=shapes,
        scratch_shapes=[pltpu.SemaphoreType.DMA((3 * n,)), pltpu.SemaphoreType.DMA((3 * n,)),
                        pltpu.SemaphoreType.DMA((n,))],
    )(*partials)


def scatter_start(partials, name):
    n = len(partials)
    lands = [lax.empty(p.shape, p.dtype) for p in partials]

    def kern(*refs):
        ins, lnd = refs[:n], refs[n:2 * n]
        send_sems, recv_sems, local_sems = refs[2 * n:2 * n + 3]
        token = refs[-1]
        x, y, c, chips = _place()
        mine = 2 * x + y
        for t in range(n):
            pltpu.make_async_copy(ins[t].at[mine], lnd[t].at[mine], local_sems.at[t]).start()
            for j, (qx, qy) in enumerate(chips):
                pltpu.make_async_remote_copy(
                    src_ref=ins[t].at[2 * qx + qy], dst_ref=lnd[t].at[mine],
                    send_sem=send_sems.at[3 * t + j], recv_sem=recv_sems.at[3 * t + j],
                    device_id=(qx, qy, c), device_id_type=MESH).start()
        token[...] = jnp.zeros_like(token)

    args = [pltpu.with_memory_space_constraint(a, pltpu.HBM) for a in list(partials) + lands]
    return pl.pallas_call(
        kern, name=name,
        out_shape=(pltpu.SemaphoreType.DMA((3 * n,)), pltpu.SemaphoreType.DMA((3 * n,)),
                   pltpu.SemaphoreType.DMA((n,)),
                   *[pltpu.HBM(a.shape, a.dtype) for a in args], _sds((8, LANES), F32)),
        in_specs=[HBM] * (2 * n),
        out_specs=(SEM, SEM, SEM, *[HBM] * (2 * n), pl.BlockSpec(memory_space=pltpu.VMEM)),
        input_output_aliases={i: 3 + i for i in range(2 * n)},
        compiler_params=pltpu.CompilerParams(has_side_effects=DATAFLOW),
    )(*args)


def scatter_wait(send_sems, recv_sems, local_sems, partials, lands, after, name):
    n = len(partials)

    def kern(*refs):
        ins, lnd = refs[:n], refs[n:2 * n]
        send_sems, recv_sems, local_sems = refs[2 * n:2 * n + 3]
        x, y, c, chips = _place()
        mine = 2 * x + y
        for t in range(n):
            pltpu.make_async_copy(ins[t].at[mine], lnd[t].at[mine], local_sems.at[t]).wait()
            for j, (qx, qy) in enumerate(chips):
                cp = pltpu.make_async_remote_copy(
                    src_ref=ins[t].at[2 * qx + qy], dst_ref=lnd[t].at[mine],
                    send_sem=send_sems.at[3 * t + j], recv_sem=recv_sems.at[3 * t + j],
                    device_id=(qx, qy, c), device_id_type=MESH)
                cp.wait_send()
                cp.wait_recv()

    bufs = list(partials) + list(lands)
    outs = pl.pallas_call(
        kern, name=name, out_shape=tuple(pltpu.HBM(a.shape, a.dtype) for a in bufs),
        in_specs=[HBM] * (2 * n) + [SEM, SEM, SEM, ANY], out_specs=[HBM] * (2 * n),
        input_output_aliases={i: i for i in range(2 * n)},
        compiler_params=pltpu.CompilerParams(has_side_effects=DATAFLOW),
    )(*bufs, send_sems, recv_sems, local_sems, after)
    return outs[n:]


def adamw_layers(parts, w, m, v, name):
    nl, kdim, n = w.shape
    tk = _row_tile(kdim, n)

    def kern(*refs):
        p_refs = refs[:nl]
        w_ref, m_ref, v_ref, g_ref, d_ref, m2_ref, v2_ref = refs[nl:]
        for l in range(nl):
            @pl.when(pl.program_id(0) == l)
            def _():
                g = p_refs[l][0].astype(F32)
                for q in range(1, 4):
                    g = g + p_refs[l][q].astype(F32)
                d, m2, v2 = _adam_update(w_ref[...], g, m_ref[...], v_ref[...])
                g_ref[...] = g
                d_ref[...] = d
                m2_ref[...] = m2
                v2_ref[...] = v2

    def part_spec(l):
        return pl.BlockSpec((4, tk, n), lambda li, i: (0, jnp.where(li == l, i, 0), 0))

    blk = pl.BlockSpec((None, tk, n), lambda li, i: (li, i, 0))
    return pl.pallas_call(
        kern, name=name, grid=(nl, kdim // tk),
        in_specs=[part_spec(l) for l in range(nl)] + [blk, blk, blk],
        out_specs=[blk] * 4, out_shape=[_sds(w.shape, F32)] * 4,
        compiler_params=_params("arbitrary", "arbitrary"),
    )(*parts, w, m, v)


def _pack(arrays):
    flat = jnp.concatenate([a.reshape(-1).astype(F32) for a in arrays])
    pad = (-flat.shape[0]) % (8 * LANES)
    return jnp.pad(flat, (0, pad)).reshape(-1, LANES)


def _unpack(packed, shapes):
    flat = packed.reshape(-1)
    out, off = [], 0
    for s in shapes:
        n = math.prod(s)
        out.append(flat[off:off + n].reshape(s))
        off += n
    return out


def _to_heads(x2d, dil, n_heads, dh):
    t = x2d.shape[0]
    return x2d.reshape(t // dil, dil, n_heads, dh).transpose(2, 1, 0, 3).reshape(n_heads, t, dh)


def _from_heads(xh, dil):
    h, t, w = xh.shape
    return xh.reshape(h, dil, t // dil, w).transpose(2, 1, 0, 3).reshape(t, h * w)


def _unperm(xh, dil):
    h, t, w = xh.shape
    return xh.reshape(h, dil, t // dil, w).transpose(0, 2, 1, 3).reshape(h, t, w)


def _perm(xh, dil):
    h, t, w = xh.shape
    return xh.reshape(h, t // dil, dil, w).transpose(0, 2, 1, 3).reshape(h, t, w)


def local_step(x, target, norm_mix_g, norm_mlp_g, final_norm_g, lbs, hgrn_norm_g, sinks,
               bq_full, bo_full, weights_get, weights_mid, grads_ready):
    t, d = x.shape
    depth = norm_mix_g.shape[0]
    na = d // 2 // A_DIM
    nbh = d // 2 // B_DIM
    nq = d // C_DIM
    nkv = nq // C_GROUP
    a_w = 3 * na * A_DIM
    c_w = (nq + 2 * nkv) * C_DIM
    tabs_a = rope_tables(t, A_DIM)
    tabs_c = rope_tables(t, C_DIM)
    saved = []
    for l in range(depth):
        s = {"x_in": x}
        (win_g, wout_g), token = weights_get(l, x)
        h = rms_fwd(x, norm_mix_g[l] + token, "norm_mix_fwd")
        s["h"] = h
        if l % 2 == 0:
            e = l // 2
            proj = mm_cols_sharded(h, win_g, 0, "even_in_proj")[0]
            qkv_r = rope_call(proj, tabs_a, a_w, 2 * na, False, "rope_a")[0]
            nums, ms, ls, hms = [], [], [], []
            for window, dil in A_BRANCHES:
                hm = _to_heads(qkv_r, dil, 3 * na, A_DIM)
                num, m, lsum = band_fwd(hm, 0, na, 2 * na, na, 1, t // dil // BLK, window // dil,
                                        f"dilated_fwd_{dil}")
                hms.append(hm)
                nums.append(_unperm(num, dil))
                ms.append(_unperm(m, dil))
                ls.append(_unperm(lsum, dil))
            oa, lse = merge_branches(nums, ms, ls, "dilated_merge")
            lb_e = lbs[e].reshape(nbh, 1, B_DIM)
            gn_e = hgrn_norm_g[e].reshape(1, B_DIM)
            ob, opre, states, amats = hgrn_fwd(proj, 3 * na, nbh, lb_e, gn_e, "hgrn_fwd")
            mixed = jnp.concatenate([_from_heads(oa, 1).astype(BF16), ob], axis=1)
            x = mm_rows_sharded(mixed, wout_g, 0, "even_out_proj", [x], ["tile"], _ep_residual)
            s.update(proj=proj, hms=hms, oa=oa, lse=lse, opre=opre, states=states, amats=amats,
                     mixed=mixed, lb=lb_e, gn=gn_e)
        else:
            o = l // 2
            wq = win_g[:, 0].transpose(1, 0, 2).reshape(d, c_w)
            proj = mm_plain(h, wq, "odd_qkv_proj", [bq_full[o].reshape(1, c_w)], ["row"], _ep_bias)
            qkv_r = rope_call(proj, tabs_c, c_w, (nq + nkv) * C_DIM // LANES, False, "rope_c")[0]
            hm = _to_heads(qkv_r, 1, nq + 2 * nkv, C_DIM)
            sink_rows = jnp.repeat(sinks[o].reshape(nkv, C_GROUP), BLK, axis=1).reshape(
                nkv, C_GROUP * BLK, 1)
            o_hm, lse = band_fwd(hm, 0, nq, nq + nkv, nkv, C_GROUP, t // BLK, C_WINDOW - 1,
                                 "swa_fwd", sink_rows=sink_rows, normalise=True)
            attn = _from_heads(o_hm, 1).astype(BF16)
            x = mm_rows_sharded(attn, wout_g, 0, "odd_out_proj", [bo_full[o].reshape(1, d), x],
                                ["row", "tile"], _ep_bias_residual)
            s.update(wq=wq, hm=hm, sink_rows=sink_rows, o_hm=o_hm, lse=lse, attn=attn)
        s["x_mid"] = x
        (w1_g, w2_g), token = weights_mid(l, x)
        s.update(win=win_g, wout=wout_g, w1=w1_g, w2=w2_g)
        h2 = rms_fwd(x, norm_mlp_g[l] + token, "norm_mlp_fwd")
        u, act = mm_cols_sharded(h2, w1_g, 0, "mlp_up", epilogue=_ep_relu2, n_out=2)
        x = mm_rows_sharded(act, w2_g, 0, "mlp_down", [x], ["tile"], _ep_residual)
        s.update(h2=h2, u=u, act=act)
        saved.append(s)

    dx, dxb, dg_final, loss_part = loss_head(x, final_norm_g, target, "loss_head")
    big = []
    small = {"final": dg_final, "loss": loss_part, "mix": [None] * depth, "mlp": [None] * depth,
             "lb": {}, "gn": {}, "sinks": {}, "bq": {}, "bo": {}}
    for l in reversed(range(depth)):
        s = saved[l]
        win_g, wout_g, w1_g, w2_g = s["win"], s["wout"], s["w1"], s["w2"]
        big.append(("w2", l, mm_tn(s["act"], dxb, "mlp_down_dw").reshape(N_DEV, -1, d)))
        du = mm_nt_rows_sharded(dxb, w2_g, 0, "mlp_down_dx", extras=[s["u"]],
                                epilogue=_ep_relu2_bwd, out_dtype=BF16)
        big.append(("w1", l, mm_tn(s["h2"], du, "mlp_up_dw", shard_cols=w1_g.shape[-1])))
        dh2 = mm_nt_cols_sharded(du, w1_g, 0, "mlp_up_dx")
        token = grads_ready(l, big[-2:])
        dx, dxb, dg, col_dx = rms_bwd(s["x_mid"], norm_mlp_g[l] + token, dh2, dx, "norm_mlp_bwd")
        small["mlp"][l] = dg
        if l % 2 == 0:
            e = l // 2
            big.append(("wout", e, mm_tn(s["mixed"], dxb, "even_out_dw").reshape(N_DEV, -1, d)))
            dmixed = mm_nt_rows_sharded(dxb, wout_g, 0, "even_out_dx")
            do_hm = _to_heads(dmixed[:, :na * A_DIM], 1, na, A_DIM)
            delta = head_delta(s["oa"], do_hm, "dilated_delta")
            dsum = None
            for (window, dil), hm in zip(A_BRANCHES, s["hms"]):
                dq, dk, dv = band_bwd(hm, 0, na, 2 * na, _perm(do_hm, dil).astype(BF16),
                                      _perm(s["lse"], dil), _perm(delta, dil), na, 1,
                                      t // dil // BLK, window // dil, f"dilated_bwd_{dil}")
                part = _from_heads(jnp.concatenate([dq, dk, dv], axis=0), dil)
                dsum = part if dsum is None else dsum + part
            dqkv_a = rope_call(dsum, tabs_a, a_w, 2 * na, True, "rope_a_bwd")[0]
            dqb, dfb, dib, dgb, dgn, dlb = hgrn_bwd(s["proj"], 3 * na, nbh, s["lb"], s["gn"],
                                                    s["opre"], s["states"], s["amats"], dmixed, na,
                                                    "hgrn_bwd")
            small["gn"][e] = dgn
            small["lb"][e] = dlb
            dproj = jnp.concatenate([dqkv_a, dqb, dfb, dib, dgb], axis=1)
            big.append(("win", e, mm_tn(s["h"], dproj, "even_in_dw", shard_cols=win_g.shape[-1])))
            dh = mm_nt_cols_sharded(dproj, win_g, 0, "even_in_dx")
        else:
            o = l // 2
            small["bo"][o] = col_dx
            big.append(("wo", o, mm_tn(s["attn"], dxb, "odd_out_dw").reshape(N_DEV, -1, d)))
            dattn = mm_nt_rows_sharded(dxb, wout_g, 0, "odd_out_dx")
            do_hm = _to_heads(dattn, 1, nq, C_DIM)
            dq, dk, dv, dsink = band_bwd(s["hm"], 0, nq, nq + nkv, do_hm, s["lse"], s["o_hm"],
                                         nkv, C_GROUP, t // BLK, C_WINDOW - 1, "swa_bwd",
                                         sink_rows=s["sink_rows"], delta_from_o=True)
            small["sinks"][o] = dsink
            dqkv = _from_heads(jnp.concatenate([dq, dk, dv], axis=0), 1)
            dproj, dbq = rope_call(dqkv, tabs_c, c_w, (nq + nkv) * C_DIM // LANES, True,
                                   "rope_c_bwd", col_sum=True)
            small["bq"][o] = dbq
            dwq = mm_tn(s["h"], dproj, "odd_qkv_dw", tn=512)
            big.append(("wqkv", o, dwq.reshape(d, N_DEV, -1).transpose(1, 0, 2)))
            dh = mm_nt_plain(dproj, s["wq"], "odd_qkv_dx", tk=c_w)
        token = grads_ready(l, big[-2:])
        dx, dxb, dg, _ = rms_bwd(s["x_in"], norm_mix_g[l] + token, dh, dx, "norm_mix_bwd")
        small["mix"][l] = dg
    return dx, small


def kernel(x, norm_mix_g, norm_mlp_g, final_norm_g, even_w_in, even_w_out, hgrn_lb_raw, hgrn_norm_g, odd_w_qkv, odd_b_qkv, odd_sinks, odd_w_o, odd_b_o, mlp_w1, mlp_w2, loss_target, m_norm_mix_g, m_norm_mlp_g, m_final_norm_g, m_even_w_in, m_even_w_out, m_hgrn_lb_raw, m_hgrn_norm_g, m_odd_w_qkv, m_odd_b_qkv, m_odd_sinks, m_odd_w_o, m_odd_b_o, m_mlp_w1, m_mlp_w2, v_norm_mix_g, v_norm_mlp_g, v_final_norm_g, v_even_w_in, v_even_w_out, v_hgrn_lb_raw, v_hgrn_norm_g, v_odd_w_qkv, v_odd_b_qkv, v_odd_sinks, v_odd_w_o, v_odd_b_o, v_mlp_w1, v_mlp_w2):
    d = x.shape[2]
    depth = norm_mix_g.shape[0]
    n_even, n_odd = even_w_in.shape[0], odd_w_qkv.shape[0]
    xi, yi, ci = lax.axis_index("x"), lax.axis_index("y"), lax.axis_index("c")
    dev = 4 * xi + 2 * yi + ci
    core = ci.astype(jnp.int32).reshape(1)

    big_w = {"win": even_w_in, "wout": even_w_out, "wqkv": odd_w_qkv, "wo": odd_w_o,
             "w1": mlp_w1, "w2": mlp_w2}
    big_m = {"win": m_even_w_in, "wout": m_even_w_out, "wqkv": m_odd_w_qkv, "wo": m_odd_w_o,
             "w1": m_mlp_w1, "w2": m_mlp_w2}
    big_v = {"win": v_even_w_in, "wout": v_even_w_out, "wqkv": v_odd_w_qkv, "wo": v_odd_w_o,
             "w1": v_mlp_w1, "w2": v_mlp_w2}
    kinds = list(big_w)
    casts = {k: cast_bf16(big_w[k], f"cast_{k}") for k in kinds}

    def layer_shards(l):
        a, b = ("win", "wout") if l % 2 == 0 else ("wqkv", "wo")
        return [casts[a][l // 2], casts[b][l // 2], casts["w1"][l], casts["w2"][l]]

    bq_w, bo_w = odd_b_qkv.shape[1], odd_b_o.shape[1]
    bq_mine = lax.dynamic_update_slice(jnp.zeros((n_odd, N_DEV * bq_w), F32), odd_b_qkv,
                                       (0, dev * bq_w))
    bo_mine = lax.dynamic_update_slice(jnp.zeros((n_odd, N_DEV * bo_w), F32), odd_b_o,
                                       (0, dev * bo_w))
    biases = all_reduce_small(_pack([bq_mine, bo_mine]), "gather_biases")
    bq_full, bo_full = _unpack(biases, [bq_mine.shape, bo_mine.shape])

    first_level = {}
    second_level = {}
    ready = {}
    zero = jnp.zeros((), F32)

    def start_first_level(key, shards, after):
        started = gather_start(shards, after, f"gather_start_{key}")
        first_level[key] = started[:-1]
        return started[-1]

    def finish_first_level(key, after):
        send_sems, recv_sems, local_sems, *bufs = first_level.pop(key)
        n = len(bufs) // 2
        bufs = gather_wait(send_sems, recv_sems, local_sems, bufs[:n], bufs[n:], after,
                           f"gather_wait_{key}")
        return bufs[n:]

    def weights_get(l, after):
        if l == 0:
            shards = layer_shards(0)
            mixer = all_gather(shards[:2], "gather_layer_0_mixer")
            token = start_first_level("0_mlp", shards[2:], [mixer[0], biases])
            for ahead in range(1, min(depth, 3)):
                token = start_first_level(ahead, layer_shards(ahead), [token])
            return [g[:, None] for g in mixer], token[0, 0]
        if l == 1:
            gathered = forward_now(finish_first_level(1, after), "gather_forward_1")
        else:
            send_sems, recv_sems, *lands = second_level.pop(l)
            gathered = forward_wait(send_sems, recv_sems, lands, after,
                                    f"gather_forward_wait_{l}")
        ready[l] = gathered[2:]
        return [g[:, None] for g in gathered[:2]], zero

    def weights_mid(l, after):
        token = zero
        if l == 0:
            ready[0] = forward_now(finish_first_level("0_mlp", after), "gather_forward_0_mlp")
            order_after = ready[0][0]
        elif l + 1 < depth:
            started = forward_start(finish_first_level(l + 1, after),
                                    f"gather_forward_start_{l + 1}")
            second_level[l + 1] = started[:-1]
            token = started[-1][0, 0]
            order_after = started[-1]
        if l + 3 < depth:
            token = token + start_first_level(l + 3, layer_shards(l + 3), [order_after])[0, 0]
        return [g[:, None] for g in ready.pop(l)], token

    exchanging = []
    scattering = []

    def finish_exchange(after):
        tag, names, layer_idx, (send_sems, recv_sems, *bufs) = exchanging.pop()
        n = len(names)
        grads, received = sibling_wait(send_sems, recv_sems, bufs[:n], bufs[n:], after,
                                       f"scatter_d2d_wait_{tag}")
        partials = [pair_add(g, r, core, f"pair_add_{k}")
                    for k, g, r in zip(names, grads, received)]
        started = scatter_start(partials, f"scatter_start_{tag}")
        scattering.append((tag, names, layer_idx, started[:-1]))
        return started[-1][0, 0]

    def grads_ready(l, group):
        names = [k for k, _, _ in group]
        grads = [g for _, _, g in group]
        tag = f"{l}_{names[0]}"
        token = finish_exchange(grads[0]) if exchanging else zero
        started = sibling_start(grads, f"scatter_d2d_start_{tag}")
        exchanging.append((tag, names, [li for _, li, _ in group], started[:-1]))
        return token + started[-1][0, 0]

    lbs, soft = lower_bounds_fwd(hgrn_lb_raw, "lower_bounds")

    dx, small = local_step(x[0], loss_target[0], norm_mix_g, norm_mlp_g, final_norm_g, lbs,
                           hgrn_norm_g, odd_sinks, bq_full, bo_full, weights_get, weights_mid,
                           grads_ready)
    finish_exchange(dx)

    parts = ([small["mix"][l] for l in range(depth)] + [small["mlp"][l] for l in range(depth)]
             + [small["final"]] + [small["lb"][e] for e in range(n_even)]
             + [small["gn"][e] for e in range(n_even)] + [small["sinks"][o] for o in range(n_odd)]
             + [small["bq"][o] for o in range(n_odd)] + [small["bo"][o] for o in range(n_odd)]
             + [small["loss"]])
    shapes = ([(depth, d)] * 2 + [(d,), hgrn_lb_raw.shape, hgrn_norm_g.shape, odd_sinks.shape,
              (n_odd, N_DEV * bq_w), (n_odd, N_DEV * bo_w), (1, LANES)])
    g_mix, g_mlp, g_final, d_lbs, g_gn, g_sinks, g_bq_full, g_bo_full, loss_v = _unpack(
        all_reduce_small(_pack(parts), "reduce_small"), shapes)
    g_lb = lower_bounds_bwd(soft, d_lbs, "lower_bounds_bwd")
    g_bq = lax.dynamic_slice(g_bq_full, (0, dev * bq_w), (n_odd, bq_w))
    g_bo = lax.dynamic_slice(g_bo_full, (0, dev * bo_w), (n_odd, bo_w))
    loss = loss_v[0, 0]

    small_names = ["norm_mix_g", "norm_mlp_g", "final_norm_g", "hgrn_lb_raw", "hgrn_norm_g",
                   "odd_b_qkv", "odd_sinks", "odd_b_o"]
    small_w = [norm_mix_g, norm_mlp_g, final_norm_g, hgrn_lb_raw, hgrn_norm_g, odd_b_qkv,
               odd_sinks, odd_b_o]
    small_m = [m_norm_mix_g, m_norm_mlp_g, m_final_norm_g, m_hgrn_lb_raw, m_hgrn_norm_g,
               m_odd_b_qkv, m_odd_sinks, m_odd_b_o]
    small_v = [v_norm_mix_g, v_norm_mlp_g, v_final_norm_g, v_hgrn_lb_raw, v_hgrn_norm_g,
               v_odd_b_qkv, v_odd_sinks, v_odd_b_o]
    small_g = [g_mix, g_mlp, g_final, g_lb, g_gn, g_bq, g_sinks, g_bo]
    sshapes = [w.shape for w in small_w]
    sd, sm, sv = adamw_small(_pack(small_w), _pack(small_g), _pack(small_m), _pack(small_v),
                             "adamw_small")
    res = {}
    for name, g, dl, m2, v2 in zip(small_names, small_g, _unpack(sd, sshapes),
                                   _unpack(sm, sshapes), _unpack(sv, sshapes)):
        res[name] = (g.reshape(dl.shape), dl, m2, v2)

    landed = {k: [None] * big_w[k].shape[0] for k in kinds}
    for tag, names, layer_idx, (send_sems, recv_sems, local_sems, *bufs) in scattering:
        n = len(names)
        lands = scatter_wait(send_sems, recv_sems, local_sems, bufs[:n], bufs[n:], dx,
                             f"scatter_wait_{tag}")
        for k, li, land in zip(names, layer_idx, lands):
            landed[k][li] = land
    long_names = {"win": "even_w_in", "wout": "even_w_out", "wqkv": "odd_w_qkv", "wo": "odd_w_o",
                  "w1": "mlp_w1", "w2": "mlp_w2"}
    for k in kinds:
        res[long_names[k]] = tuple(adamw_layers(landed[k], big_w[k], big_m[k], big_v[k],
                                                f"adamw_{k}"))

    order = ["norm_mix_g", "norm_mlp_g", "final_norm_g", "even_w_in", "even_w_out", "hgrn_lb_raw",
             "hgrn_norm_g", "odd_w_qkv", "odd_b_qkv", "odd_sinks", "odd_w_o", "odd_b_o", "mlp_w1",
             "mlp_w2"]
    outs = [loss, dx[None]]
    for j in range(4):
        outs += [res[n][j] for n in order]
    return tuple(outs)
```

```python
import functools
import math

import jax
import jax.numpy as jnp
from jax import lax
from jax.experimental import pallas as pl
from jax.experimental.pallas import tpu as pltpu

F32 = jnp.float32
BF16 = jnp.bfloat16
MESH = pl.DeviceIdType.MESH

N_DEV = 8
NORM_EPS = 1e-5
ROPE_THETA = 500000.0
BLK = 128
A_DIM = 128
A_BRANCHES = ((128, 1), (512, 4), (2048, 16))
B_DIM = 128
B_CHUNK = 64
C_DIM = 64
C_GROUP = 8
C_WINDOW = 128
LANES = 128

ADAM_LR = 0.001
ADAM_B1 = 0.9
ADAM_B2 = 0.999
ADAM_EPS = 1e-08
ADAM_WD = 0.01
ADAM_STEP = 10

NN = (((1,), (0,)), ((), ()))
NT = (((1,), (1,)), ((), ()))
TN = (((0,), (0,)), ((), ()))


def _params(*sem):
    return pltpu.CompilerParams(dimension_semantics=sem)


def _sigmoid(x):
    return 1.0 / (1.0 + jnp.exp(-x))


def _rows_call(name, body, row_ins, full_ins, row_outs, acc_outs, tm):
    t = row_ins[0].shape[0]
    n_ri, n_fi, n_ro = len(row_ins), len(full_ins), len(row_outs)

    def kern(*refs):
        i = pl.program_id(0)
        body(i, refs[:n_ri], refs[n_ri:n_ri + n_fi],
             refs[n_ri + n_fi:n_ri + n_fi + n_ro], refs[n_ri + n_fi + n_ro:])

    def row_spec(shape):
        return pl.BlockSpec((tm,) + tuple(shape[1:]), lambda i: (i,) + (0,) * (len(shape) - 1))

    def full_spec(shape):
        return pl.BlockSpec(tuple(shape), lambda i: (0,) * len(shape))

    outs = pl.pallas_call(
        kern, name=name, grid=(t // tm,),
        in_specs=[row_spec(a.shape) for a in row_ins] + [full_spec(a.shape) for a in full_ins],
        out_specs=[row_spec(s.shape) for s in row_outs] + [full_spec(s.shape) for s in acc_outs],
        out_shape=list(row_outs) + list(acc_outs),
        compiler_params=_params("arbitrary" if acc_outs else "parallel"),
    )(*row_ins, *full_ins)
    return outs


def _sds(shape, dtype):
    return jax.ShapeDtypeStruct(tuple(shape), dtype)


def rms_fwd(x, g, name):
    t, d = x.shape

    def body(i, ri, fi, ro, ao):
        xv = ri[0][...]
        r = lax.rsqrt(jnp.mean(xv * xv, axis=-1, keepdims=True) + NORM_EPS)
        ro[0][...] = (xv * r * fi[0][...]).astype(BF16)

    return _rows_call(name, body, [x], [g.reshape(1, d)], [_sds((t, d), BF16)], [], 256)[0]


def rms_bwd(x, g, dh, dx_res, name):
    t, d = x.shape

    def body(i, ri, fi, ro, ao):
        xv, dhv, res = ri[0][...], ri[1][...], ri[2][...]
        gv = fi[0][...]
        r = lax.rsqrt(jnp.mean(xv * xv, axis=-1, keepdims=True) + NORM_EPS)
        gd = gv * dhv
        dx = res + r * gd - xv * (r * r * r) * jnp.mean(xv * gd, axis=-1, keepdims=True)
        ro[0][...] = dx
        ro[1][...] = dx.astype(BF16)

        @pl.when(i == 0)
        def _():
            ao[0][...] = jnp.zeros_like(ao[0])
            ao[1][...] = jnp.zeros_like(ao[1])

        ao[0][...] += jnp.sum(dhv * xv * r, axis=0, keepdims=True)
        ao[1][...] += jnp.sum(dx, axis=0, keepdims=True)

    return _rows_call(name, body, [x, dh, dx_res], [g.reshape(1, d)],
                      [_sds((t, d), F32), _sds((t, d), BF16)],
                      [_sds((1, d), F32), _sds((1, d), F32)], 256)


def loss_head(x, g, target, name):
    t, d = x.shape

    def body(i, ri, fi, ro, ao):
        xv, tg = ri[0][...], ri[1][...]
        gv = fi[0][...]
        r = lax.rsqrt(jnp.mean(xv * xv, axis=-1, keepdims=True) + NORM_EPS)
        e = xv * r * gv - tg
        dy = e * (1.0 / d)
        gd = gv * dy
        dx = r * gd - xv * (r * r * r) * jnp.mean(xv * gd, axis=-1, keepdims=True)
        ro[0][...] = dx
        ro[1][...] = dx.astype(BF16)

        @pl.when(i == 0)
        def _():
            ao[0][...] = jnp.zeros_like(ao[0])
            ao[1][...] = jnp.zeros_like(ao[1])

        ao[0][...] += jnp.sum(dy * xv * r, axis=0, keepdims=True)
        part = 0.5 * jnp.sum(jnp.mean(e * e, axis=-1, keepdims=True), axis=0, keepdims=True)
        ao[1][...] += jnp.broadcast_to(part, (1, LANES))

    return _rows_call(name, body, [x, target], [g.reshape(1, d)],
                      [_sds((t, d), F32), _sds((t, d), BF16)],
                      [_sds((1, d), F32), _sds((1, LANES), F32)], 256)


def rope_tables(seq, head_dim):
    rot = head_dim // 4
    half = rot // 2
    inv_freq = 1.0 / (ROPE_THETA ** (jnp.arange(0, rot, 2, dtype=F32) / rot))
    ang = jnp.arange(seq, dtype=F32)[:, None] * inv_freq[None, :]
    cos, sin = jnp.cos(ang), jnp.sin(ang)
    zeros = jnp.zeros((seq, head_dim - rot), F32)
    zh = jnp.zeros((seq, half), F32)
    c = jnp.concatenate([cos, cos, jnp.ones((seq, head_dim - rot), F32)], axis=-1)
    sp = jnp.concatenate([zh, sin, zeros], axis=-1)
    sm = jnp.concatenate([-sin, zh, zeros], axis=-1)
    rep = LANES // head_dim
    return jnp.tile(c, (1, rep)), jnp.tile(sp, (1, rep)), jnp.tile(sm, (1, rep)), half


def rope_call(x, tabs, width, n_rope, inverse, name, col_sum=False):
    c, sp, sm, half = tabs
    t = x.shape[0]
    tm = 256
    n_slab = width // LANES

    def kern(x_ref, c_ref, sp_ref, sm_ref, o_ref, *acc):
        cv, spv, smv = c_ref[...], sp_ref[...], sm_ref[...]
        for j in range(n_slab):
            xs = x_ref[:, j * LANES:(j + 1) * LANES].astype(F32)
            if j < n_rope:
                if inverse:
                    ys = (xs * cv + pltpu.roll(xs * spv, LANES - half, 1)
                          + pltpu.roll(xs * smv, half, 1))
                else:
                    ys = (xs * cv + pltpu.roll(xs, half, 1) * spv
                          + pltpu.roll(xs, LANES - half, 1) * smv)
            else:
                ys = xs
            o_ref[:, j * LANES:(j + 1) * LANES] = ys.astype(BF16)
            if col_sum:
                @pl.when(pl.program_id(0) == 0)
                def _():
                    acc[0][:, j * LANES:(j + 1) * LANES] = jnp.zeros((1, LANES), F32)
                acc[0][:, j * LANES:(j + 1) * LANES] += jnp.sum(ys, axis=0, keepdims=True)

    tab_spec = pl.BlockSpec((tm, LANES), lambda i: (i, 0))
    out_shape = [_sds((t, width), BF16)]
    out_specs = [pl.BlockSpec((tm, width), lambda i: (i, 0))]
    if col_sum:
        out_shape.append(_sds((1, width), F32))
        out_specs.append(pl.BlockSpec((1, width), lambda i: (0, 0)))
    return pl.pallas_call(
        kern, name=name, grid=(t // tm,),
        in_specs=[pl.BlockSpec((tm, width), lambda i: (i, 0)), tab_spec, tab_spec, tab_spec],
        out_specs=out_specs, out_shape=out_shape,
        compiler_params=_params("arbitrary" if col_sum else "parallel"),
    )(x, c, sp, sm)


def _mm_call(name, a, b, extras, out_shapes, grid, a_spec, b_spec, extra_specs, out_specs,
             acc_shape, dims, epilogue):
    n_ex, n_out = len(extras), len(out_shapes)
    nk = grid[2]

    def product(a_ref, b_ref):
        bv = b_ref[...]
        if bv.ndim == 3:
            bv = bv.reshape(bv.shape[0] * bv.shape[1], bv.shape[2])
        return lax.dot_general(a_ref[...].astype(BF16), bv.astype(BF16), dims,
                               preferred_element_type=F32)

    def kern(*refs):
        a_ref, b_ref = refs[0], refs[1]
        ex = refs[2:2 + n_ex]
        outs = refs[2 + n_ex:2 + n_ex + n_out]
        if nk == 1:
            epilogue(product(a_ref, b_ref), ex, outs)
            return
        acc = refs[-1]
        k = pl.program_id(2)

        @pl.when(k == 0)
        def _():
            acc[...] = product(a_ref, b_ref)

        @pl.when(k > 0)
        def _():
            acc[...] += product(a_ref, b_ref)

        @pl.when(k == nk - 1)
        def _():
            epilogue(acc[...], ex, outs)

    return pl.pallas_call(
        kern, name=name, grid=grid,
        in_specs=[a_spec, b_spec, *extra_specs], out_specs=out_specs, out_shape=out_shapes,
        scratch_shapes=[pltpu.VMEM(acc_shape, F32)] if nk > 1 else [],
        compiler_params=_params("parallel", "parallel", "arbitrary"),
    )(a, b, *extras)


def _ep_store(dtype):
    def ep(acc, ex, outs):
        outs[0][...] = acc.astype(dtype)
    return ep


def _ep_residual(acc, ex, outs):
    outs[0][...] = acc + ex[0][...]


def _ep_bias(acc, ex, outs):
    outs[0][...] = acc + ex[0][...]


def _ep_bias_residual(acc, ex, outs):
    outs[0][...] = acc + ex[0][...] + ex[1][...]


def _ep_relu2(acc, ex, outs):
    outs[0][...] = acc
    rl = jnp.maximum(acc, 0.0)
    outs[1][...] = (rl * rl).astype(BF16)


def _ep_relu2_bwd(acc, ex, outs):
    outs[0][...] = (acc * (2.0 * jnp.maximum(ex[0][...], 0.0))).astype(BF16)


MM_TM = 1024
MM_TN = 1024
MM_TK = 2048


def mm_cols_sharded(a, wg, layer, name, epilogue=None, n_out=1):
    m, kdim = a.shape
    n = wg.shape[-1]
    tm, tk = min(m, MM_TM), min(kdim, MM_TK)
    if epilogue is None:
        epilogue, outs = _ep_store(F32), [_sds((m, N_DEV * n), F32)]
    else:
        outs = [_sds((m, N_DEV * n), F32), _sds((m, N_DEV * n), BF16)][:n_out]
    return _mm_call(
        name, a, wg, [], outs, (m // tm, N_DEV, kdim // tk),
        pl.BlockSpec((tm, tk), lambda i, j, k: (i, k)),
        pl.BlockSpec((None, None, tk, n), lambda i, j, k: (j, layer, k, 0)),
        [], [pl.BlockSpec((tm, n), lambda i, j, k: (i, j))] * len(outs),
        (tm, n), NN, epilogue)


def _extra_specs(extra_kinds, tm, tn):
    specs = []
    for kind in extra_kinds:
        if kind == "row":
            specs.append(pl.BlockSpec((1, tn), lambda i, j, k: (0, j)))
        else:
            specs.append(pl.BlockSpec((tm, tn), lambda i, j, k: (i, j)))
    return specs


def mm_rows_sharded(a, wg, layer, name, extras, extra_kinds, epilogue):
    m, kdim = a.shape
    ks, n = wg.shape[-2], wg.shape[-1]
    tm, tn = min(m, MM_TM), min(n, MM_TN)
    gps = max(1, min(kdim, MM_TK) // ks)
    return _mm_call(
        name, a, wg, extras, [_sds((m, n), F32)], (m // tm, n // tn, N_DEV // gps),
        pl.BlockSpec((tm, gps * ks), lambda i, j, k: (i, k)),
        pl.BlockSpec((gps, None, ks, tn), lambda i, j, k: (k, layer, 0, j)),
        _extra_specs(extra_kinds, tm, tn), [pl.BlockSpec((tm, tn), lambda i, j, k: (i, j))],
        (tm, tn), NN, epilogue)[0]


def mm_plain(a, w, name, extras, extra_kinds, epilogue, tn=512):
    m, kdim = a.shape
    n = w.shape[1]
    tm, tk = min(m, MM_TM), min(kdim, MM_TK)
    return _mm_call(
        name, a, w, extras, [_sds((m, n), F32)], (m // tm, n // tn, kdim // tk),
        pl.BlockSpec((tm, tk), lambda i, j, k: (i, k)),
        pl.BlockSpec((tk, tn), lambda i, j, k: (k, j)),
        _extra_specs(extra_kinds, tm, tn), [pl.BlockSpec((tm, tn), lambda i, j, k: (i, j))],
        (tm, tn), NN, epilogue)[0]


def mm_nt_cols_sharded(dy, wg, layer, name):
    m = dy.shape[0]
    kdim, n = wg.shape[-2], wg.shape[-1]
    tm, tn = min(m, MM_TM), min(kdim, MM_TN)
    return _mm_call(
        name, dy, wg, [], [_sds((m, kdim), F32)], (m // tm, kdim // tn, N_DEV),
        pl.BlockSpec((tm, n), lambda i, j, k: (i, k)),
        pl.BlockSpec((None, None, tn, n), lambda i, j, k: (k, layer, j, 0)),
        [], [pl.BlockSpec((tm, tn), lambda i, j, k: (i, j))],
        (tm, tn), NT, _ep_store(F32))[0]


def mm_nt_rows_sharded(dy, wg, layer, name, extras=(), epilogue=None, out_dtype=F32):
    m, n = dy.shape
    ks = wg.shape[-2]
    tm, tk = min(m, MM_TM), min(n, MM_TK)
    gps = max(1, MM_TN // ks)
    tn = gps * ks
    epilogue = _ep_store(out_dtype) if epilogue is None else epilogue
    return _mm_call(
        name, dy, wg, list(extras), [_sds((m, N_DEV * ks), out_dtype)],
        (m // tm, N_DEV // gps, n // tk),
        pl.BlockSpec((tm, tk), lambda i, j, k: (i, k)),
        pl.BlockSpec((gps, None, ks, tk), lambda i, j, k: (j, layer, 0, k)),
        [pl.BlockSpec((tm, tn), lambda i, j, k: (i, j))] * len(extras),
        [pl.BlockSpec((tm, tn), lambda i, j, k: (i, j))],
        (tm, tn), NT, epilogue)[0]


def mm_nt_plain(dy, w, name, tk):
    m, n = dy.shape
    kdim = w.shape[0]
    tm, tn = min(m, MM_TM), min(kdim, MM_TN)
    return _mm_call(
        name, dy, w, [], [_sds((m, kdim), F32)], (m // tm, kdim // tn, n // tk),
        pl.BlockSpec((tm, tk), lambda i, j, k: (i, k)),
        pl.BlockSpec((tn, tk), lambda i, j, k: (j, k)),
        [], [pl.BlockSpec((tm, tn), lambda i, j, k: (i, j))],
        (tm, tn), NT, _ep_store(F32))[0]


def mm_tn(a, dy, name, shard_cols=None, tn=MM_TN):
    t, kdim = a.shape
    n = dy.shape[1]
    tm, tk = min(kdim, MM_TM), min(t, MM_TK)
    if shard_cols is None:
        tn = min(tn, n)
        out = _sds((kdim, n), BF16)
        o_spec = pl.BlockSpec((tm, tn), lambda i, j, k: (i, j))
    else:
        tn = shard_cols
        out = _sds((n // tn, kdim, tn), BF16)
        o_spec = pl.BlockSpec((None, tm, tn), lambda i, j, k: (j, i, 0))
    return _mm_call(
        name, a, dy, [], [out], (kdim // tm, n // tn, t // tk),
        pl.BlockSpec((tk, tm), lambda i, j, k: (k, i)),
        pl.BlockSpec((tk, tn), lambda i, j, k: (k, j)),
        [], [o_spec], (tm, tn), TN, _ep_store(BF16))[0]


BAND_BLOCKS_PER_STEP = 4
BAND_BLOCKS_PER_STEP_GROUPED = 2


def _band_mask(g, nk_prev_valid, max_dist):
    rows = lax.broadcasted_iota(jnp.int32, (g * BLK, 2 * BLK), 0) % BLK
    cols = lax.broadcasted_iota(jnp.int32, (g * BLK, 2 * BLK), 1)
    dist = rows + BLK - cols
    ok = (dist >= 0) & (dist <= max_dist)
    return ok & ((cols >= BLK) | nk_prev_valid)


def band_fwd(qkv, q0, k0, v0, hk, g, seg, max_dist, name, sink_rows=None, normalise=False):
    t, dh = qkv.shape[1], qkv.shape[2]
    nb = t // BLK
    rb = BAND_BLOCKS_PER_STEP if g == 1 else BAND_BLOCKS_PER_STEP_GROUPED
    rows = rb * BLK
    scale = dh ** -0.5
    has_sink = sink_rows is not None

    def kern(*refs):
        if has_sink:
            q_ref, k_ref, v_ref, s_ref, num_ref, m_ref, *l_ref = refs
            sink = s_ref[...]
        else:
            q_ref, k_ref, v_ref, num_ref, m_ref, *l_ref = refs
        for r in range(rb):
            b = pl.program_id(1) * rb + r
            cur = pl.multiple_of(b * BLK, BLK)
            prev = pl.multiple_of(jnp.maximum(b - 1, 0) * BLK, BLK)
            here = slice(r * BLK, (r + 1) * BLK)
            q = q_ref[:, here, :].reshape(g * BLK, dh)
            kk = jnp.concatenate([k_ref[pl.ds(prev, BLK), :], k_ref[pl.ds(cur, BLK), :]], axis=0)
            vv = jnp.concatenate([v_ref[pl.ds(prev, BLK), :], v_ref[pl.ds(cur, BLK), :]], axis=0)
            s = lax.dot_general(q, kk, NT, preferred_element_type=F32) * scale
            s = jnp.where(_band_mask(g, (b % seg) != 0, max_dist), s, -jnp.inf)
            m = jnp.max(s, axis=-1, keepdims=True)
            if has_sink:
                m = jnp.maximum(m, sink)
            p = jnp.exp(s - m)
            l = jnp.sum(p, axis=-1, keepdims=True)
            if has_sink:
                l = l + jnp.exp(sink - m)
            num = jnp.dot(p.astype(BF16), vv, preferred_element_type=F32)
            if normalise:
                num_ref[:, here, :] = (num * (1.0 / l)).reshape(g, BLK, dh)
                m_ref[:, here, :] = (m + jnp.log(l)).reshape(g, BLK, 1)
            else:
                num_ref[:, here, :] = num.reshape(g, BLK, dh)
                m_ref[:, here, :] = m.reshape(g, BLK, 1)
                l_ref[0][:, here, :] = l.reshape(g, BLK, 1)

    in_specs = [pl.BlockSpec((g, rows, dh), lambda h, b: (q0 // g + h, b, 0)),
                pl.BlockSpec((None, t, dh), lambda h, b: (k0 + h, 0, 0)),
                pl.BlockSpec((None, t, dh), lambda h, b: (v0 + h, 0, 0))]
    args = [qkv, qkv, qkv]
    if has_sink:
        in_specs.append(pl.BlockSpec((None, g * BLK, 1), lambda h, b: (h, 0, 0)))
        args.append(sink_rows)
    hq = hk * g
    n_col = 1 if normalise else 2
    return pl.pallas_call(
        kern, name=name, grid=(hk, nb // rb), in_specs=in_specs,
        out_specs=[pl.BlockSpec((g, rows, dh), lambda h, b: (h, b, 0))]
        + [pl.BlockSpec((g, rows, 1), lambda h, b: (h, b, 0))] * n_col,
        out_shape=[_sds((hq, t, dh), F32)] + [_sds((hq, t, 1), F32)] * n_col,
        compiler_params=_params("parallel", "parallel"),
    )(*args)


def band_bwd(qkv, q0, k0, v0, do, lse, delta, hk, g, seg, max_dist, name, sink_rows=None,
             delta_from_o=False):
    t, dh = qkv.shape[1], qkv.shape[2]
    nb = t // BLK
    rb = BAND_BLOCKS_PER_STEP if g == 1 else BAND_BLOCKS_PER_STEP_GROUPED
    scale = dh ** -0.5
    has_sink = sink_rows is not None

    def kern(*refs):
        if has_sink:
            (q_ref, k_ref, v_ref, do_ref, lse_ref, dl_ref, s_ref,
             dq_ref, dk_ref, dv_ref, ds_ref, sacc) = refs
            sink = s_ref[...]
        else:
            q_ref, k_ref, v_ref, do_ref, lse_ref, dl_ref, dq_ref, dk_ref, dv_ref = refs
        step = pl.program_id(1)

        @pl.when(step == 0)
        def _():
            dk_ref[...] = jnp.zeros_like(dk_ref)
            dv_ref[...] = jnp.zeros_like(dv_ref)
            if has_sink:
                sacc[...] = jnp.zeros_like(sacc)

        for r in range(rb):
            b = step * rb + r
            cur = pl.multiple_of(b * BLK, BLK)
            prev = pl.multiple_of(jnp.maximum(b - 1, 0) * BLK, BLK)
            here = slice(r * BLK, (r + 1) * BLK)
            q = q_ref[:, here, :].reshape(g * BLK, dh)
            dout = do_ref[:, here, :].reshape(g * BLK, dh)
            lse_b = lse_ref[:, here, :].reshape(g * BLK, 1)
            if delta_from_o:
                dl_b = jnp.sum(dl_ref[:, here, :].reshape(g * BLK, dh) * dout, axis=-1,
                               keepdims=True)
                dout = dout.astype(BF16)
            else:
                dl_b = dl_ref[:, here, :].reshape(g * BLK, 1)
            kk = jnp.concatenate([k_ref[pl.ds(prev, BLK), :], k_ref[pl.ds(cur, BLK), :]], axis=0)
            vv = jnp.concatenate([v_ref[pl.ds(prev, BLK), :], v_ref[pl.ds(cur, BLK), :]], axis=0)
            s = lax.dot_general(q, kk, NT, preferred_element_type=F32) * scale
            s = jnp.where(_band_mask(g, (b % seg) != 0, max_dist), s, -jnp.inf)
            p = jnp.exp(s - lse_b)
            dp = lax.dot_general(dout, vv, NT, preferred_element_type=F32)
            ds = (p * (dp - dl_b) * scale).astype(BF16)
            dq = jnp.dot(ds, kk, preferred_element_type=F32)
            dq_ref[:, here, :] = dq.reshape(g, BLK, dh)
            dkk = lax.dot_general(ds, q, TN, preferred_element_type=F32)
            dvv = lax.dot_general(p.astype(BF16), dout, TN, preferred_element_type=F32)
            dk_ref[pl.ds(prev, BLK), :] += dkk[:BLK]
            dk_ref[pl.ds(cur, BLK), :] += dkk[BLK:]
            dv_ref[pl.ds(prev, BLK), :] += dvv[:BLK]
            dv_ref[pl.ds(cur, BLK), :] += dvv[BLK:]
            if has_sink:
                sacc[...] += -jnp.exp(sink - lse_b) * dl_b

        if has_sink:
            @pl.when(step == nb // rb - 1)
            def _():
                for gi in range(g):
                    ds_ref[gi:gi + 1, :] = jnp.sum(sacc[gi * BLK:(gi + 1) * BLK, :], axis=0,
                                                   keepdims=True)

    rows = rb * BLK
    in_specs = [pl.BlockSpec((g, rows, dh), lambda h, b: (q0 // g + h, b, 0)),
                pl.BlockSpec((None, t, dh), lambda h, b: (k0 + h, 0, 0)),
                pl.BlockSpec((None, t, dh), lambda h, b: (v0 + h, 0, 0)),
                pl.BlockSpec((g, rows, dh), lambda h, b: (h, b, 0)),
                pl.BlockSpec((g, rows, 1), lambda h, b: (h, b, 0)),
                pl.BlockSpec((g, rows, dh if delta_from_o else 1), lambda h, b: (h, b, 0))]
    args = [qkv, qkv, qkv, do, lse, delta]
    hq = hk * g
    out_specs = [pl.BlockSpec((g, rows, dh), lambda h, b: (h, b, 0)),
                 pl.BlockSpec((None, t, dh), lambda h, b: (h, 0, 0)),
                 pl.BlockSpec((None, t, dh), lambda h, b: (h, 0, 0))]
    out_shape = [_sds((hq, t, dh), F32), _sds((hk, t, dh), F32), _sds((hk, t, dh), F32)]
    scratch = []
    if has_sink:
        in_specs.append(pl.BlockSpec((None, g * BLK, 1), lambda h, b: (h, 0, 0)))
        args.append(sink_rows)
        out_specs.append(pl.BlockSpec((None, g, 1), lambda h, b: (h, 0, 0)))
        out_shape.append(_sds((hk, g, 1), F32))
        scratch.append(pltpu.VMEM((g * BLK, 1), F32))
    return pl.pallas_call(
        kern, name=name, grid=(hk, nb // rb), in_specs=in_specs, out_specs=out_specs,
        out_shape=out_shape,
        scratch_shapes=scratch, compiler_params=_params("parallel", "arbitrary"),
    )(*args)


def merge_branches(nums, ms, ls, name):
    h, t, dh = nums[0].shape
    nbr = len(nums)

    def kern(*refs):
        num_refs, m_refs, l_refs = refs[:nbr], refs[nbr:2 * nbr], refs[2 * nbr:3 * nbr]
        o_ref, lse_ref = refs[3 * nbr], refs[3 * nbr + 1]
        mall = m_refs[0][...]
        for i in range(1, nbr):
            mall = jnp.maximum(mall, m_refs[i][...])
        num = jnp.zeros((t, dh), F32)
        den = jnp.zeros((t, 1), F32)
        for i in range(nbr):
            w = jnp.exp(m_refs[i][...] - mall)
            num = num + w * num_refs[i][...]
            den = den + w * l_refs[i][...]
        o_ref[...] = num / den
        lse_ref[...] = mall + jnp.log(den)

    big = pl.BlockSpec((None, t, dh), lambda i: (i, 0, 0))
    col = pl.BlockSpec((None, t, 1), lambda i: (i, 0, 0))
    return pl.pallas_call(
        kern, name=name, grid=(h,), in_specs=[big] * nbr + [col] * (2 * nbr),
        out_specs=[big, col], out_shape=[_sds((h, t, dh), F32), _sds((h, t, 1), F32)],
        compiler_params=_params("parallel"),
    )(*nums, *ms, *ls)


def normalise_heads(num, m, l, name):
    h, t, dh = num.shape

    def kern(num_ref, m_ref, l_ref, o_ref, lse_ref):
        lv = l_ref[...]
        o_ref[...] = num_ref[...] / lv
        lse_ref[...] = m_ref[...] + jnp.log(lv)

    big = pl.BlockSpec((None, t, dh), lambda i: (i, 0, 0))
    col = pl.BlockSpec((None, t, 1), lambda i: (i, 0, 0))
    return pl.pallas_call(
        kern, name=name, grid=(h,), in_specs=[big, col, col], out_specs=[big, col],
        out_shape=[_sds((h, t, dh), F32), _sds((h, t, 1), F32)],
        compiler_params=_params("parallel"),
    )(num, m, l)


def head_delta(o, do, name):
    h, t, dh = o.shape

    def kern(o_ref, do_ref, d_ref):
        d_ref[...] = jnp.sum(o_ref[...] * do_ref[...], axis=-1, keepdims=True)

    big = pl.BlockSpec((None, t, dh), lambda i: (i, 0, 0))
    return pl.pallas_call(
        kern, name=name, grid=(h,), in_specs=[big, big],
        out_specs=pl.BlockSpec((None, t, 1), lambda i: (i, 0, 0)),
        out_shape=_sds((h, t, 1), F32), compiler_params=_params("parallel"),
    )(o, do)


def _dil_rb(nbl):
    return min(BAND_BLOCKS_PER_STEP, nbl)


def dilated_fwd(qkv, na, dil, max_dist, name):
    t, w3 = qkv.shape
    dh = A_DIM
    seq = t // dil
    nbl = seq // BLK
    rb = _dil_rb(nbl)
    rows = rb * BLK
    cb = w3 // dh
    scale = dh ** -0.5
    view = qkv.reshape(seq, dil * w3)

    def kern(q_ref, k_ref, v_ref, num_ref, m_ref, l_ref):
        for r in range(rb):
            b = pl.program_id(1) * rb + r
            cur = pl.multiple_of(b * BLK, BLK)
            prev = pl.multiple_of(jnp.maximum(b - 1, 0) * BLK, BLK)
            here = slice(r * BLK, (r + 1) * BLK)
            kk = jnp.concatenate([k_ref[pl.ds(prev, BLK), :], k_ref[pl.ds(cur, BLK), :]], axis=0)
            vv = jnp.concatenate([v_ref[pl.ds(prev, BLK), :], v_ref[pl.ds(cur, BLK), :]], axis=0)
            s = lax.dot_general(q_ref[here, :], kk, NT, preferred_element_type=F32) * scale
            s = jnp.where(_band_mask(1, b != 0, max_dist), s, -jnp.inf)
            m = jnp.max(s, axis=-1, keepdims=True)
            p = jnp.exp(s - m)
            l = jnp.sum(p, axis=-1, keepdims=True)
            num_ref[here, :] = jnp.dot(p.astype(BF16), vv, preferred_element_type=F32)
            m_ref[here, :] = jnp.broadcast_to(m, (BLK, dh))
            l_ref[here, :] = jnp.broadcast_to(l, (BLK, dh))

    def col(off):
        return lambda p, b: (0, (p // na) * cb + off * na + p % na)

    out_spec = pl.BlockSpec((rows, dh), lambda p, b: (b, p))
    out = _sds((seq, dil * na * dh), F32)
    outs = pl.pallas_call(
        kern, name=name, grid=(dil * na, nbl // rb),
        in_specs=[pl.BlockSpec((rows, dh), lambda p, b: (b, (p // na) * cb + p % na)),
                  pl.BlockSpec((seq, dh), col(1)), pl.BlockSpec((seq, dh), col(2))],
        out_specs=[out_spec] * 3, out_shape=[out] * 3,
        compiler_params=_params("parallel", "parallel"),
    )(view, view, view)
    return [o.reshape(t, na * dh) for o in outs]


def dilated_merge(nums, ms, ls, name):
    t, w = nums[0].shape
    nbr = len(nums)

    def body(i, ri, fi, ro, ao):
        mall = ri[nbr][...]
        for j in range(1, nbr):
            mall = jnp.maximum(mall, ri[nbr + j][...])
        num = jnp.zeros(mall.shape, F32)
        den = jnp.zeros(mall.shape, F32)
        for j in range(nbr):
            wgt = jnp.exp(ri[nbr + j][...] - mall)
            num = num + wgt * ri[j][...]
            den = den + wgt * ri[2 * nbr + j][...]
        o = num / den
        ro[0][...] = o
        ro[1][...] = o.astype(BF16)
        ro[2][...] = mall + jnp.log(den)

    return _rows_call(name, body, list(nums) + list(ms) + list(ls), [],
                      [_sds((t, w), F32), _sds((t, w), BF16), _sds((t, w), F32)], [], 256)


def dilated_delta(o, dmixed, name):
    t, w = o.shape

    def body(i, ri, fi, ro, ao):
        for j in range(w // A_DIM):
            cols = slice(j * A_DIM, (j + 1) * A_DIM)
            d = jnp.sum(ri[0][:, cols] * ri[1][:, cols], axis=-1, keepdims=True)
            ro[0][:, cols] = jnp.broadcast_to(d, (d.shape[0], A_DIM))

    return _rows_call(name, body, [o, dmixed], [], [_sds((t, w), F32)], [], 256)[0]


def dilated_bwd(qkv, dmixed, lse, delta, na, dil, max_dist, name):
    t, w3 = qkv.shape
    dh = A_DIM
    seq = t // dil
    nbl = seq // BLK
    rb = _dil_rb(nbl)
    rows = rb * BLK
    cb = w3 // dh
    db = dmixed.shape[1] // dh
    scale = dh ** -0.5
    view = qkv.reshape(seq, dil * w3)
    do_view = dmixed.reshape(seq, dil * dmixed.shape[1])
    lse_view = lse.reshape(seq, dil * na * dh)
    delta_view = delta.reshape(seq, dil * na * dh)

    def kern(q_ref, k_ref, v_ref, do_ref, lse_ref, dl_ref, dq_ref, dk_ref, dv_ref):
        step = pl.program_id(1)

        @pl.when(step == 0)
        def _():
            dk_ref[...] = jnp.zeros_like(dk_ref)
            dv_ref[...] = jnp.zeros_like(dv_ref)

        for r in range(rb):
            b = step * rb + r
            cur = pl.multiple_of(b * BLK, BLK)
            prev = pl.multiple_of(jnp.maximum(b - 1, 0) * BLK, BLK)
            here = slice(r * BLK, (r + 1) * BLK)
            q = q_ref[here, :]
            dout = do_ref[here, :].astype(BF16)
            kk = jnp.concatenate([k_ref[pl.ds(prev, BLK), :], k_ref[pl.ds(cur, BLK), :]], axis=0)
            vv = jnp.concatenate([v_ref[pl.ds(prev, BLK), :], v_ref[pl.ds(cur, BLK), :]], axis=0)
            s = lax.dot_general(q, kk, NT, preferred_element_type=F32) * scale
            s = jnp.where(_band_mask(1, b != 0, max_dist), s, -jnp.inf)
            p = jnp.exp(s - lse_ref[here, 0:1])
            dp = lax.dot_general(dout, vv, NT, preferred_element_type=F32)
            ds = (p * (dp - dl_ref[here, 0:1]) * scale).astype(BF16)
            dq_ref[here, :] = jnp.dot(ds, kk, preferred_element_type=F32)
            dkk = lax.dot_general(ds, q, TN, preferred_element_type=F32)
            dvv = lax.dot_general(p.astype(BF16), dout, TN, preferred_element_type=F32)
            dk_ref[pl.ds(prev, BLK), :] += dkk[:BLK]
            dk_ref[pl.ds(cur, BLK), :] += dkk[BLK:]
            dv_ref[pl.ds(prev, BLK), :] += dvv[:BLK]
            dv_ref[pl.ds(cur, BLK), :] += dvv[BLK:]

    def col(off):
        return lambda p, b: (0, (p // na) * cb + off * na + p % na)

    blk = pl.BlockSpec((rows, dh), lambda p, b: (b, p))
    whole = pl.BlockSpec((seq, dh), lambda p, b: (0, p))
    out = _sds((seq, dil * na * dh), F32)
    outs = pl.pallas_call(
        kern, name=name, grid=(dil * na, nbl // rb),
        in_specs=[pl.BlockSpec((rows, dh), lambda p, b: (b, (p // na) * cb + p % na)),
                  pl.BlockSpec((seq, dh), col(1)), pl.BlockSpec((seq, dh), col(2)),
                  pl.BlockSpec((rows, dh), lambda p, b: (b, (p // na) * db + p % na)), blk, blk],
        out_specs=[blk, whole, whole], out_shape=[out] * 3,
        compiler_params=_params("parallel", "arbitrary"),
    )(view, view, view, do_view, lse_view, delta_view)
    return [o.reshape(t, na * dh) for o in outs]


def rope_bwd_sum(dqs, dks, dvs, tabs, name):
    c, sp, sm, half = tabs
    t, w = dqs[0].shape
    nbr = len(dqs)
    tm = 256
    n_slab = w // LANES

    def kern(*refs):
        groups = [refs[:nbr], refs[nbr:2 * nbr], refs[2 * nbr:3 * nbr]]
        c_ref, sp_ref, sm_ref, o_ref = refs[3 * nbr:]
        cv, spv, smv = c_ref[...], sp_ref[...], sm_ref[...]
        for gi, group in enumerate(groups):
            for j in range(n_slab):
                cols = slice(j * LANES, (j + 1) * LANES)
                xs = group[0][:, cols]
                for ref in group[1:]:
                    xs = xs + ref[:, cols]
                if gi < 2:
                    xs = (xs * cv + pltpu.roll(xs * spv, LANES - half, 1)
                          + pltpu.roll(xs * smv, half, 1))
                o_ref[:, gi * w + j * LANES:gi * w + (j + 1) * LANES] = xs.astype(BF16)

    big = pl.BlockSpec((tm, w), lambda i: (i, 0))
    tab = pl.BlockSpec((tm, LANES), lambda i: (i, 0))
    return pl.pallas_call(
        kern, name=name, grid=(t // tm,), in_specs=[big] * (3 * nbr) + [tab] * 3,
        out_specs=pl.BlockSpec((tm, 3 * w), lambda i: (i, 0)), out_shape=_sds((t, 3 * w), BF16),
        compiler_params=_params("parallel"),
    )(*dqs, *dks, *dvs, c, sp, sm)


def _cumsum_rows(x, n, reverse=False):
    rows = lax.broadcasted_iota(jnp.int32, x.shape, 0)
    shift = 1
    while shift < n:
        if reverse:
            x = x + jnp.where(rows < n - shift, pltpu.roll(x, n - shift, 0), 0.0)
        else:
            x = x + jnp.where(rows >= shift, pltpu.roll(x, shift, 0), 0.0)
        shift *= 2
    return x


def _hgrn_gates(f, lb):
    sig = _sigmoid(f)
    gate = lb + (1.0 - lb) * sig
    return sig, gate


B_SUB = 16


def _dot3(a, b, dims):
    ah, bh = a.astype(BF16), b.astype(BF16)
    al = (a - ah.astype(F32)).astype(BF16)
    bl = (b - bh.astype(F32)).astype(BF16)
    dot = functools.partial(lax.dot_general, dimension_numbers=dims, preferred_element_type=F32)
    return dot(ah, bh) + dot(al, bh) + dot(ah, bl)


def _sub_scales(b, i):
    r0 = i * B_SUB
    beta = b[r0 - 1:r0, :]
    return jnp.exp(b[r0:r0 + B_SUB, :] - beta), jnp.exp(jnp.minimum(beta - b, 0.0))


def _hgrn_intra_attn(qq, kk, b):
    c = qq.shape[0]
    lane = lax.broadcasted_iota(jnp.int32, (B_SUB, c), 1)
    trow = lax.broadcasted_iota(jnp.int32, (B_SUB, B_DIM), 0)
    blocks = []
    for i in range(c // B_SUB):
        r0 = i * B_SUB
        qi, bi = qq[r0:r0 + B_SUB, :], b[r0:r0 + B_SUB, :]
        if i == 0:
            a_i = jnp.zeros((B_SUB, c), F32)
        else:
            eq, ek = _sub_scales(b, i)
            a_i = jnp.where(lane < r0, _dot3(qi * eq, kk * ek, NT), 0.0)
        for sl in range(B_SUB):
            s = r0 + sl
            e = jnp.exp(jnp.where(trow >= sl, bi - b[s:s + 1, :], -jnp.inf))
            col = jnp.sum(qi * kk[s:s + 1, :] * e, axis=1, keepdims=True)
            a_i = jnp.where(lane == s, col, a_i)
        blocks.append(a_i)
    return jnp.concatenate(blocks, axis=0)


def hgrn_fwd(proj, col0, nh, lb, gn, name):
    t = proj.shape[0]
    c = B_CHUNK
    nc = t // c
    scale = B_DIM ** -0.5

    def kern(q_ref, f_ref, i_ref, g_ref, lb_ref, gn_ref, out_ref, opre_ref, st_ref, a_ref, state):
        lbv = lb_ref[...]
        gnv = gn_ref[...]
        state[...] = jnp.zeros_like(state)

        def chunk(ci, carry):
            rows = pl.ds(pl.multiple_of(ci * c, c), c)
            _, gate = _hgrn_gates(f_ref[rows, :], lbv)
            kk = 1.0 - gate
            qb = q_ref[rows, :]
            qq = qb * _sigmoid(qb) * scale
            v = i_ref[rows, :]
            b = _cumsum_rows(jnp.log(gate), c)
            st = state[...]
            st_ref[ci] = st
            o_inter = lax.dot_general((qq * jnp.exp(b)).astype(BF16), st.astype(BF16), NT,
                                      preferred_element_type=F32)
            amat = _hgrn_intra_attn(qq, kk, b)
            a_ref[ci] = amat
            o = jnp.dot(amat.astype(BF16), v.astype(BF16), preferred_element_type=F32) + o_inter
            opre_ref[rows, :] = o
            bl = b[c - 1:c, :]
            state[...] = st * jnp.exp(bl) + lax.dot_general(
                v.astype(BF16), (kk * jnp.exp(bl - b)).astype(BF16), TN, preferred_element_type=F32)
            r = lax.rsqrt(jnp.mean(o * o, axis=-1, keepdims=True) + NORM_EPS)
            gb = g_ref[rows, :]
            out_ref[rows, :] = (o * r * gnv * (gb * _sigmoid(gb))).astype(BF16)
            return carry

        lax.fori_loop(0, nc, chunk, 0)

    def col(off):
        return pl.BlockSpec((t, B_DIM), lambda h: (0, col0 + off * nh + h))

    return pl.pallas_call(
        kern, name=name, grid=(nh,),
        in_specs=[col(0), col(1), col(2), col(3),
                  pl.BlockSpec((None, 1, B_DIM), lambda h: (h, 0, 0)),
                  pl.BlockSpec((1, B_DIM), lambda h: (0, 0))],
        out_specs=[pl.BlockSpec((t, B_DIM), lambda h: (0, h)),
                   pl.BlockSpec((t, B_DIM), lambda h: (0, h)),
                   pl.BlockSpec((None, nc, B_DIM, B_DIM), lambda h: (h, 0, 0, 0)),
                   pl.BlockSpec((None, nc, c, c), lambda h: (h, 0, 0, 0))],
        out_shape=[_sds((t, nh * B_DIM), BF16), _sds((t, nh * B_DIM), F32),
                   _sds((nh, nc, B_DIM, B_DIM), F32), _sds((nh, nc, c, c), F32)],
        scratch_shapes=[pltpu.VMEM((B_DIM, B_DIM), F32)],
        compiler_params=_params("parallel"),
    )(proj, proj, proj, proj, lb, gn)


def hgrn_bwd(proj, col0, nh, lb, gn, opre, states, amats, dout, dcol0, name):
    t = proj.shape[0]
    c = B_CHUNK
    nc = t // c
    scale = B_DIM ** -0.5
    nsub = c // B_SUB

    def kern(q_ref, f_ref, i_ref, g_ref, lb_ref, gn_ref, opre_ref, st_ref, a_ref, dout_ref,
             dq_ref, df_ref, di_ref, dg_ref, dgn_ref, dlb_ref, dstate, dksc):
        lbv = lb_ref[...]
        gnv = gn_ref[...]
        dstate[...] = jnp.zeros_like(dstate)
        dlb_ref[...] = jnp.zeros_like(dlb_ref)

        @pl.when(pl.program_id(0) == 0)
        def _():
            dgn_ref[...] = jnp.zeros_like(dgn_ref)

        srow = lax.broadcasted_iota(jnp.int32, (c, B_DIM), 0)
        lane = lax.broadcasted_iota(jnp.int32, (B_SUB, c), 1)
        trow = lax.broadcasted_iota(jnp.int32, (B_SUB, B_DIM), 0)
        arow = lax.broadcasted_iota(jnp.int32, (c, c), 0)
        alane = lax.broadcasted_iota(jnp.int32, (c, c), 1)

        def chunk(cj, carry):
            ci = nc - 1 - cj
            rows = pl.ds(pl.multiple_of(ci * c, c), c)
            f = f_ref[rows, :]
            sig, gate = _hgrn_gates(f, lbv)
            kk = 1.0 - gate
            qb = q_ref[rows, :]
            sq = _sigmoid(qb)
            qq = qb * sq * scale
            v = i_ref[rows, :]
            b = _cumsum_rows(jnp.log(gate), c)
            st0 = st_ref[ci]
            dst = dstate[...]
            o = opre_ref[rows, :]
            gb = g_ref[rows, :]
            sg = _sigmoid(gb)
            silu_g = gb * sg
            d_out = dout_ref[rows, :]
            r = lax.rsqrt(jnp.mean(o * o, axis=-1, keepdims=True) + NORM_EPS)
            y = o * r
            dg_ref[rows, :] = (d_out * y * gnv * (sg * (1.0 + gb * (1.0 - sg)))).astype(BF16)
            dyn = d_out * silu_g
            dgn_ref[...] += jnp.sum(dyn * y, axis=0, keepdims=True)
            dy = dyn * gnv
            do = r * (dy - y * jnp.mean(dy * y, axis=-1, keepdims=True))
            eb = jnp.exp(b)
            bl = b[c - 1:c, :]
            ebl = jnp.exp(bl - b)
            ebl_last = jnp.exp(bl)
            do_b = do.astype(BF16)
            dst_b = dst.astype(BF16)
            dq_inter = jnp.dot(do_b, st0.astype(BF16), preferred_element_type=F32) * eb
            dst0 = lax.dot_general(do_b, (qq * eb).astype(BF16), TN,
                                   preferred_element_type=F32) + dst * ebl_last
            dv_inter = lax.dot_general((kk * ebl).astype(BF16), dst_b, NT, preferred_element_type=F32)
            dk_inter = jnp.dot(v.astype(BF16), dst_b, preferred_element_type=F32) * ebl
            amat = a_ref[ci]
            v_b = v.astype(BF16)
            d_a = lax.dot_general(do_b, v_b, NT, preferred_element_type=F32)
            d_a = jnp.where(arow >= alane, d_a, 0.0)
            dv_intra = lax.dot_general(amat.astype(BF16), do_b, TN, preferred_element_type=F32)
            dk_pairs = jnp.zeros((c, B_DIM), F32)
            dq_blocks = []
            for i in range(nsub):
                r0 = i * B_SUB
                qi, bi = qq[r0:r0 + B_SUB, :], b[r0:r0 + B_SUB, :]
                da_i = d_a[r0:r0 + B_SUB, :]
                if i == 0:
                    dq_i = jnp.zeros((B_SUB, B_DIM), F32)
                else:
                    eq, ek = _sub_scales(b, i)
                    da_m = jnp.where(lane < r0, da_i, 0.0)
                    dq_i = _dot3(da_m, kk * ek, NN) * eq
                    dk_pairs = dk_pairs + _dot3(da_m, qi * eq, TN) * ek
                for sl in range(B_SUB):
                    s = r0 + sl
                    e = jnp.exp(jnp.where(trow >= sl, bi - b[s:s + 1, :], -jnp.inf))
                    dacol = jnp.sum(jnp.where(lane == s, da_i, 0.0), axis=1, keepdims=True)
                    w = dacol * e
                    dq_i = dq_i + w * kk[s:s + 1, :]
                    dksc[s:s + 1, :] = jnp.sum(w * qi, axis=0, keepdims=True)
                dq_blocks.append(dq_i)
            dq = jnp.concatenate(dq_blocks, axis=0) + dq_inter
            dk = dk_pairs + dksc[...] + dk_inter
            dv = dv_intra + dv_inter
            db = qq * dq - kk * dk
            extra = (jnp.sum(kk * dk_inter, axis=0, keepdims=True)
                     + ebl_last * jnp.sum(st0 * dst, axis=0, keepdims=True))
            db = db + jnp.where(srow == c - 1, extra, 0.0)
            dlog = _cumsum_rows(db, c, reverse=True)
            dgate = dlog / gate - dk
            df_ref[rows, :] = (dgate * (1.0 - lbv) * sig * (1.0 - sig)).astype(BF16)
            dlb_ref[...] += jnp.sum(dgate * (1.0 - sig), axis=0, keepdims=True)
            dq_ref[rows, :] = (dq * scale * (sq * (1.0 + qb * (1.0 - sq)))).astype(BF16)
            di_ref[rows, :] = dv.astype(BF16)
            dstate[...] = dst0
            return carry

        lax.fori_loop(0, nc, chunk, 0)

    def col(off):
        return pl.BlockSpec((t, B_DIM), lambda h: (0, col0 + off * nh + h))

    hcol = pl.BlockSpec((t, B_DIM), lambda h: (0, h))
    vec = pl.BlockSpec((None, 1, B_DIM), lambda h: (h, 0, 0))
    wide = _sds((t, nh * B_DIM), BF16)
    return pl.pallas_call(
        kern, name=name, grid=(nh,),
        in_specs=[col(0), col(1), col(2), col(3), vec,
                  pl.BlockSpec((1, B_DIM), lambda h: (0, 0)), hcol,
                  pl.BlockSpec((None, nc, B_DIM, B_DIM), lambda h: (h, 0, 0, 0)),
                  pl.BlockSpec((None, nc, c, c), lambda h: (h, 0, 0, 0)),
                  pl.BlockSpec((t, B_DIM), lambda h: (0, dcol0 + h))],
        out_specs=[hcol, hcol, hcol, hcol, pl.BlockSpec((1, B_DIM), lambda h: (0, 0)), vec],
        out_shape=[wide, wide, wide, wide, _sds((1, B_DIM), F32), _sds((nh, 1, B_DIM), F32)],
        scratch_shapes=[pltpu.VMEM((B_DIM, B_DIM), F32), pltpu.VMEM((c, B_DIM), F32)],
        compiler_params=_params("arbitrary"),
    )(proj, proj, proj, proj, lb, gn, opre, states, amats, dout)


def lower_bounds_fwd(raw, name):
    n, w = raw.shape

    def kern(raw_ref, lb_ref, soft_ref):
        r = raw_ref[...]
        mx = r[0:1]
        for i in range(1, n):
            mx = jnp.maximum(mx, r[i:i + 1])
        e = jnp.exp(r - mx)
        den = e[0:1]
        for i in range(1, n):
            den = den + e[i:i + 1]
        soft = e / den
        soft_ref[...] = soft
        run = soft[0:1]
        lb_ref[0:1, :] = run - soft[0:1]
        for i in range(1, n):
            run = run + soft[i:i + 1]
            lb_ref[i:i + 1, :] = run - soft[0:1]

    return pl.pallas_call(kern, name=name, out_shape=[_sds((n, w), F32), _sds((n, w), F32)])(raw)


def lower_bounds_bwd(soft, dlb, name):
    n, w = soft.shape

    def kern(soft_ref, dlb_ref, out_ref):
        s = soft_ref[...]
        d = dlb_ref[...]
        total = d[0:1]
        for i in range(1, n):
            total = total + d[i:i + 1]
        us = []
        tail = total
        for i in range(n):
            us.append(tail - total if i == 0 else tail)
            tail = tail - d[i:i + 1]
        dot = s[0:1] * us[0]
        for i in range(1, n):
            dot = dot + s[i:i + 1] * us[i]
        for i in range(n):
            out_ref[i:i + 1, :] = s[i:i + 1] * (us[i] - dot)

    return pl.pallas_call(kern, name=name, out_shape=_sds((n, w), F32))(soft, dlb)


def _row_tile(kdim, n):
    tk = 512
    while tk > 8 and tk * n > 256 * 1024:
        tk //= 2
    return min(kdim, tk)


def _adam_update(w, g, m, v):
    m2 = ADAM_B1 * m + (1.0 - ADAM_B1) * g
    v2 = ADAM_B2 * v + (1.0 - ADAM_B2) * (g * g)
    m_hat = m2 / (1.0 - ADAM_B1 ** ADAM_STEP)
    v_hat = v2 / (1.0 - ADAM_B2 ** ADAM_STEP)
    delta = -ADAM_LR * (m_hat / (jnp.sqrt(v_hat) + ADAM_EPS) + ADAM_WD * w)
    return delta, m2, v2


def adamw_small(w, g, m, v, name):
    def kern(w_ref, g_ref, m_ref, v_ref, d_ref, m2_ref, v2_ref):
        d, m2, v2 = _adam_update(w_ref[...], g_ref[...], m_ref[...], v_ref[...])
        d_ref[...] = d
        m2_ref[...] = m2
        v2_ref[...] = v2

    return pl.pallas_call(kern, name=name, out_shape=[_sds(w.shape, F32)] * 3)(w, g, m, v)


def adamw_big(parts, w, m, v, name):
    nl, kdim, n = w.shape
    tk = _row_tile(kdim, n)

    def kern(p_ref, w_ref, m_ref, v_ref, g_ref, d_ref, m2_ref, v2_ref):
        g = p_ref[0].astype(F32)
        for q in range(1, 4):
            g = g + p_ref[q].astype(F32)
        d, m2, v2 = _adam_update(w_ref[...], g, m_ref[...], v_ref[...])
        g_ref[...] = g
        d_ref[...] = d
        m2_ref[...] = m2
        v2_ref[...] = v2

    blk = pl.BlockSpec((None, tk, n), lambda l, i: (l, i, 0))
    return pl.pallas_call(
        kern, name=name, grid=(nl, kdim // tk),
        in_specs=[pl.BlockSpec((None, 4, tk, n), lambda l, i: (l, 0, i, 0)), blk, blk, blk],
        out_specs=[blk] * 4, out_shape=[_sds(w.shape, F32)] * 4,
        compiler_params=_params("parallel", "parallel"),
    )(parts, w, m, v)


def cast_bf16(w, name):
    nl, kdim, n = w.shape
    tk = _row_tile(kdim, n)

    def kern(w_ref, o_ref):
        o_ref[...] = w_ref[...].astype(BF16)

    blk = pl.BlockSpec((None, tk, n), lambda l, i: (l, i, 0))
    return pl.pallas_call(
        kern, name=name, grid=(nl, kdim // tk), in_specs=[blk], out_specs=blk,
        out_shape=_sds(w.shape, BF16), compiler_params=_params("parallel", "parallel"),
    )(w)


def pair_add(dw, r1, core, name):
    kdim, n = dw.shape[1], dw.shape[2]
    tk = min(kdim, 4 * _row_tile(kdim, n))

    def kern(c_ref, a_ref, b_ref, o_ref):
        o_ref[...] = (a_ref[...].astype(F32) + b_ref[...].astype(F32)).astype(BF16)

    grid_spec = pltpu.PrefetchScalarGridSpec(
        num_scalar_prefetch=1, grid=(4, kdim // tk),
        in_specs=[pl.BlockSpec((None, tk, n), lambda p, i, c: (2 * p + c[0], i, 0)),
                  pl.BlockSpec((None, tk, n), lambda p, i, c: (p, i, 0))],
        out_specs=pl.BlockSpec((None, tk, n), lambda p, i, c: (p, i, 0)))
    return pl.pallas_call(
        kern, name=name, grid_spec=grid_spec, out_shape=_sds((4, kdim, n), BF16),
        compiler_params=_params("parallel", "parallel"),
    )(core, dw, r1)


ANY = pl.BlockSpec(memory_space=pl.ANY)


def _place():
    x, y, c = lax.axis_index("x"), lax.axis_index("y"), lax.axis_index("c")
    chips = [(1 - x, y), (x, 1 - y), (1 - x, 1 - y)]
    return x, y, c, chips


def all_gather(shards, name):
    n = len(shards)

    def kern(*refs):
        ins, outs = refs[:n], refs[n:2 * n]
        send_sems, recv_sems, local_sems = refs[2 * n:]
        x, y, c, chips = _place()
        me, sib = (x, y, c), (x, y, 1 - c)

        def copy(t, k, block, to, src=None):
            px, py, pc = block
            dst = outs[t].at[4 * px + 2 * py + pc]
            return pltpu.make_async_remote_copy(
                src_ref=dst if src is None else src, dst_ref=dst,
                send_sem=send_sems.at[7 * t + k], recv_sem=recv_sems.at[7 * t + k],
                device_id=to, device_id_type=MESH)

        mine = [pltpu.make_async_copy(ins[t], outs[t].at[4 * x + 2 * y + c], local_sems.at[t])
                for t in range(n)]
        for cp in mine:
            cp.start()
        first = []
        for t in range(n):
            first.append(copy(t, 0, me, sib, src=ins[t]))
            first += [copy(t, 1 + j, me, (*chip, c), src=ins[t]) for j, chip in enumerate(chips)]
        for cp in first:
            cp.start()
        passed = []
        for t in range(n):
            for j, chip in enumerate(chips):
                copy(t, 1 + j, (*chip, c), me).wait_recv()
                fwd = copy(t, 4 + j, (*chip, c), sib)
                fwd.start()
                passed.append(fwd)
        for t in range(n):
            copy(t, 0, sib, me).wait_recv()
            for j, chip in enumerate(chips):
                copy(t, 4 + j, (*chip, 1 - c), me).wait_recv()
        for cp in first + passed:
            cp.wait_send()
        for cp in mine:
            cp.wait()

    return pl.pallas_call(
        kern, name=name, in_specs=[ANY] * n, out_specs=[ANY] * n,
        out_shape=[_sds((N_DEV,) + s.shape, s.dtype) for s in shards],
        scratch_shapes=[pltpu.SemaphoreType.DMA((7 * n,)), pltpu.SemaphoreType.DMA((7 * n,)),
                        pltpu.SemaphoreType.DMA((n,))],
    )(*shards)


HBM = pl.BlockSpec(memory_space=pltpu.HBM)
SEM = pl.BlockSpec(memory_space=pltpu.SEMAPHORE)
DATAFLOW = pltpu.SideEffectType.DATAFLOW_SIDE_EFFECTING


def _first_level_targets():
    x, y, c, chips = _place()
    return 4 * x + 2 * y + c, [(x, y, 1 - c)] + [(*chip, c) for chip in chips]


def gather_start(shards, after, name):
    n = len(shards)
    lands = [lax.empty((N_DEV,) + s.shape, s.dtype) for s in shards]

    def kern(*refs):
        ins, lnd = refs[:n], refs[n:2 * n]
        send_sems, recv_sems, local_sems = refs[2 * n + len(after):2 * n + len(after) + 3]
        token = refs[-1]
        me, targets = _first_level_targets()
        for t in range(n):
            pltpu.make_async_copy(ins[t], lnd[t].at[me], local_sems.at[t]).start()
            for k, to in enumerate(targets):
                pltpu.make_async_remote_copy(
                    src_ref=ins[t], dst_ref=lnd[t].at[me], send_sem=send_sems.at[4 * t + k],
                    recv_sem=recv_sems.at[4 * t + k], device_id=to, device_id_type=MESH).start()
        token[...] = jnp.zeros_like(token)

    args = [pltpu.with_memory_space_constraint(a, pltpu.HBM) for a in list(shards) + lands]
    return pl.pallas_call(
        kern, name=name,
        out_shape=(pltpu.SemaphoreType.DMA((4 * n,)), pltpu.SemaphoreType.DMA((4 * n,)),
                   pltpu.SemaphoreType.DMA((n,)),
                   *[pltpu.HBM(a.shape, a.dtype) for a in args], _sds((8, LANES), F32)),
        in_specs=[HBM] * (2 * n) + [ANY] * len(after),
        out_specs=(SEM, SEM, SEM, *[HBM] * (2 * n), pl.BlockSpec(memory_space=pltpu.VMEM)),
        input_output_aliases={i: 3 + i for i in range(2 * n)},
        compiler_params=pltpu.CompilerParams(has_side_effects=DATAFLOW),
    )(*args, *after)


def gather_wait(send_sems, recv_sems, local_sems, shards, lands, after, name):
    n = len(shards)

    def kern(*refs):
        ins, lnd = refs[:n], refs[n:2 * n]
        send_sems, recv_sems, local_sems = refs[2 * n:2 * n + 3]
        me, targets = _first_level_targets()
        for t in range(n):
            pltpu.make_async_copy(ins[t], lnd[t].at[me], local_sems.at[t]).wait()
            for k, to in enumerate(targets):
                cp = pltpu.make_async_remote_copy(
                    src_ref=ins[t], dst_ref=lnd[t].at[me], send_sem=send_sems.at[4 * t + k],
                    recv_sem=recv_sems.at[4 * t + k], device_id=to, device_id_type=MESH)
                cp.wait_send()
                cp.wait_recv()

    bufs = list(shards) + list(lands)
    return pl.pallas_call(
        kern, name=name, out_shape=tuple(pltpu.HBM(a.shape, a.dtype) for a in bufs),
        in_specs=[HBM] * (2 * n) + [SEM, SEM, SEM, ANY], out_specs=[HBM] * (2 * n),
        input_output_aliases={i: i for i in range(2 * n)},
        compiler_params=pltpu.CompilerParams(has_side_effects=DATAFLOW),
    )(*bufs, send_sems, recv_sems, local_sems, after)


def _forward_copies(lnd, send_sems, recv_sems):
    x, y, c, chips = _place()
    passed = []
    for t in range(len(lnd)):
        for j, (qx, qy) in enumerate(chips):
            block = lnd[t].at[4 * qx + 2 * qy + c]
            passed.append(pltpu.make_async_remote_copy(
                src_ref=block, dst_ref=block, send_sem=send_sems.at[3 * t + j],
                recv_sem=recv_sems.at[3 * t + j], device_id=(x, y, 1 - c), device_id_type=MESH))
    return passed


def forward_now(lands, name):
    n = len(lands)

    def kern(*refs):
        copies = _forward_copies(refs[n:2 * n], refs[2 * n], refs[2 * n + 1])
        for cp in copies:
            cp.start()
        for cp in copies:
            cp.wait_recv()
        for cp in copies:
            cp.wait_send()

    return pl.pallas_call(
        kern, name=name, in_specs=[ANY] * n, out_specs=[ANY] * n,
        out_shape=[_sds(a.shape, a.dtype) for a in lands],
        input_output_aliases={i: i for i in range(n)},
        scratch_shapes=[pltpu.SemaphoreType.DMA((3 * n,)), pltpu.SemaphoreType.DMA((3 * n,))],
    )(*lands)


def sibling_start(grads, name):
    n = len(grads)
    lands = [lax.empty((4,) + g.shape[1:], g.dtype) for g in grads]

    def kern(*refs):
        ins, lnd = refs[:n], refs[n:2 * n]
        send_sems, recv_sems = refs[2 * n], refs[2 * n + 1]
        x, y, c, _ = _place()
        for t in range(n):
            for p in range(4):
                pltpu.make_async_remote_copy(
                    src_ref=ins[t].at[2 * p + 1 - c], dst_ref=lnd[t].at[p],
                    send_sem=send_sems.at[4 * t + p], recv_sem=recv_sems.at[4 * t + p],
                    device_id=(x, y, 1 - c), device_id_type=MESH).start()
        refs[-1][...] = jnp.zeros_like(refs[-1])

    args = [pltpu.with_memory_space_constraint(a, pltpu.HBM) for a in list(grads) + lands]
    return pl.pallas_call(
        kern, name=name,
        out_shape=(pltpu.SemaphoreType.DMA((4 * n,)), pltpu.SemaphoreType.DMA((4 * n,)),
                   *[pltpu.HBM(a.shape, a.dtype) for a in args], _sds((8, LANES), F32)),
        in_specs=[HBM] * (2 * n),
        out_specs=(SEM, SEM, *[HBM] * (2 * n), pl.BlockSpec(memory_space=pltpu.VMEM)),
        input_output_aliases={i: 2 + i for i in range(2 * n)},
        compiler_params=pltpu.CompilerParams(has_side_effects=DATAFLOW),
    )(*args)


def sibling_wait(send_sems, recv_sems, grads, lands, after, name):
    n = len(grads)

    def kern(*refs):
        ins, lnd = refs[:n], refs[n:2 * n]
        send_sems, recv_sems = refs[2 * n], refs[2 * n + 1]
        x, y, c, _ = _place()
        for t in range(n):
            for p in range(4):
                cp = pltpu.make_async_remote_copy(
                    src_ref=ins[t].at[2 * p + 1 - c], dst_ref=lnd[t].at[p],
                    send_sem=send_sems.at[4 * t + p], recv_sem=recv_sems.at[4 * t + p],
                    device_id=(x, y, 1 - c), device_id_type=MESH)
                cp.wait_send()
                cp.wait_recv()

    bufs = list(grads) + list(lands)
    outs = pl.pallas_call(
        kern, name=name, out_shape=tuple(pltpu.HBM(a.shape, a.dtype) for a in bufs),
        in_specs=[HBM] * (2 * n) + [SEM, SEM, ANY], out_specs=[HBM] * (2 * n),
        input_output_aliases={i: i for i in range(2 * n)},
        compiler_params=pltpu.CompilerParams(has_side_effects=DATAFLOW),
    )(*bufs, send_sems, recv_sems, after)
    return outs[:n], outs[n:]


def forward_start(lands, name):
    n = len(lands)

    def kern(*refs):
        for cp in _forward_copies(refs[:n], refs[n], refs[n + 1]):
            cp.start()
        refs[-1][...] = jnp.zeros_like(refs[-1])

    return pl.pallas_call(
        kern, name=name,
        out_shape=(pltpu.SemaphoreType.DMA((3 * n,)), pltpu.SemaphoreType.DMA((3 * n,)),
                   *[pltpu.HBM(a.shape, a.dtype) for a in lands], _sds((8, LANES), F32)),
        in_specs=[HBM] * n,
        out_specs=(SEM, SEM, *[HBM] * n, pl.BlockSpec(memory_space=pltpu.VMEM)),
        input_output_aliases={i: 2 + i for i in range(n)},
        compiler_params=pltpu.CompilerParams(has_side_effects=DATAFLOW),
    )(*lands)


def forward_wait(send_sems, recv_sems, lands, after, name):
    n = len(lands)

    def kern(*refs):
        for cp in _forward_copies(refs[:n], refs[n], refs[n + 1]):
            cp.wait_send()
            cp.wait_recv()

    return pl.pallas_call(
        kern, name=name, out_shape=tuple(pltpu.HBM(a.shape, a.dtype) for a in lands),
        in_specs=[HBM] * n + [SEM, SEM, ANY], out_specs=[HBM] * n,
        input_output_aliases={i: i for i in range(n)},
        compiler_params=pltpu.CompilerParams(has_side_effects=DATAFLOW),
    )(*lands, send_sems, recv_sems, after)


def all_reduce_small(vec, name):
    r = vec.shape[0]

    def kern(v_ref, o_ref, buf, send_sems, recv_sems):
        x, y, c, _ = _place()
        me = 4 * x + 2 * y + c
        peers = [(x, y, 1 - c), (1 - x, y, c), (x, 1 - y, c), (1 - x, 1 - y, c),
                 (1 - x, y, 1 - c), (x, 1 - y, 1 - c), (1 - x, 1 - y, 1 - c)]
        buf[me] = v_ref[...]
        copies = []
        for k, peer in enumerate(peers):
            cp = pltpu.make_async_remote_copy(
                src_ref=v_ref, dst_ref=buf.at[me], send_sem=send_sems.at[k],
                recv_sem=recv_sems.at[k], device_id=peer, device_id_type=MESH)
            cp.start()
            copies.append(cp)
        for cp in copies:
            cp.wait_recv()
        for cp in copies:
            cp.wait_send()
        total = buf[0]
        for d in range(1, N_DEV):
            total = total + buf[d]
        o_ref[...] = total

    vm = pl.BlockSpec(memory_space=pltpu.VMEM)
    return pl.pallas_call(
        kern, name=name, in_specs=[vm], out_specs=vm, out_shape=_sds(vec.shape, F32),
        scratch_shapes=[pltpu.VMEM((N_DEV, r, LANES), F32), pltpu.SemaphoreType.DMA((7,)),
                        pltpu.SemaphoreType.DMA((7,))],
    )(vec)


def exchange_with_sibling(grads, name):
    n = len(grads)

    def kern(*refs):
        ins, outs = refs[:n], refs[n:2 * n]
        send_sems, recv_sems = refs[2 * n:]
        x, y, c, _ = _place()
        copies = []
        for t in range(n):
            for p in range(4):
                cp = pltpu.make_async_remote_copy(
                    src_ref=ins[t].at[2 * p + 1 - c], dst_ref=outs[t].at[p],
                    send_sem=send_sems.at[4 * t + p], recv_sem=recv_sems.at[4 * t + p],
                    device_id=(x, y, 1 - c), device_id_type=MESH)
                cp.start()
                copies.append(cp)
        for cp in copies:
            cp.wait_recv()
        for cp in copies:
            cp.wait_send()

    return pl.pallas_call(
        kern, name=name, in_specs=[ANY] * n, out_specs=[ANY] * n,
        out_shape=[_sds((4,) + g.shape[1:], g.dtype) for g in grads],
        scratch_shapes=[pltpu.SemaphoreType.DMA((4 * n,)), pltpu.SemaphoreType.DMA((4 * n,))],
    )(*grads)


def exchange_between_chips(partials, layers, kinds, name):
    n = len(partials)
    n_kind = max(kinds) + 1
    shapes = []
    for kd in range(n_kind):
        idx = [i for i in range(n) if kinds[i] == kd]
        nl = max(layers[i] for i in idx) + 1
        shapes.append(_sds((nl,) + partials[idx[0]].shape, partials[idx[0]].dtype))

    def kern(*refs):
        ins, outs = refs[:n], refs[n:n + n_kind]
        send_sems, recv_sems, local_sems = refs[n + n_kind:]
        x, y, c, chips = _place()
        mine = 2 * x + y
        local = []
        copies = []
        for t in range(n):
            dst = outs[kinds[t]].at[layers[t], mine]
            lc = pltpu.make_async_copy(ins[t].at[mine], dst, local_sems.at[t])
            lc.start()
            local.append(lc)
            for j, (qx, qy) in enumerate(chips):
                cp = pltpu.make_async_remote_copy(
                    src_ref=ins[t].at[2 * qx + qy], dst_ref=dst,
                    send_sem=send_sems.at[3 * t + j], recv_sem=recv_sems.at[3 * t + j],
                    device_id=(qx, qy, c), device_id_type=MESH)
                cp.start()
                copies.append(cp)
        for cp in copies:
            cp.wait_recv()
        for cp in copies:
            cp.wait_send()
        for lc in local:
            lc.wait()

    return pl.pallas_call(
        kern, name=name, in_specs=[ANY] * n, out_specs=[ANY] * n_kind, out_shape=shapes,
        scratch_shapes=[pltpu.SemaphoreType.DMA((3 * n,)), pltpu.SemaphoreType.DMA((3 * n,)),
                        pltpu.SemaphoreType.DMA((n,))],
    )(*partials)


def scatter_start(partials, name):
    n = len(partials)
    lands = [lax.empty(p.shape, p.dtype) for p in partials]

    def kern(*refs):
        ins, lnd = refs[:n], refs[n:2 * n]
        send_sems, recv_sems, local_sems = refs[2 * n:2 * n + 3]
        token = refs[-1]
        x, y, c, chips = _place()
        mine = 2 * x + y
        for t in range(n):
            pltpu.make_async_copy(ins[t].at[mine], lnd[t].at[mine], local_sems.at[t]).start()
            for j, (qx, qy) in enumerate(chips):
                pltpu.make_async_remote_copy(
                    src_ref=ins[t].at[2 * qx + qy], dst_ref=lnd[t].at[mine],
                    send_sem=send_sems.at[3 * t + j], recv_sem=recv_sems.at[3 * t + j],
                    device_id=(qx, qy, c), device_id_type=MESH).start()
        token[...] = jnp.zeros_like(token)

    args = [pltpu.with_memory_space_constraint(a, pltpu.HBM) for a in list(partials) + lands]
    return pl.pallas_call(
        kern, name=name,
        out_shape=(pltpu.SemaphoreType.DMA((3 * n,)), pltpu.SemaphoreType.DMA((3 * n,)),
                   pltpu.SemaphoreType.DMA((n,)),
                   *[pltpu.HBM(a.shape, a.dtype) for a in args], _sds((8, LANES), F32)),
        in_specs=[HBM] * (2 * n),
        out_specs=(SEM, SEM, SEM, *[HBM] * (2 * n), pl.BlockSpec(memory_space=pltpu.VMEM)),
        input_output_aliases={i: 3 + i for i in range(2 * n)},
        compiler_params=pltpu.CompilerParams(has_side_effects=DATAFLOW),
    )(*args)


def scatter_wait(send_sems, recv_sems, local_sems, partials, lands, after, name):
    n = len(partials)

    def kern(*refs):
        ins, lnd = refs[:n], refs[n:2 * n]
        send_sems, recv_sems, local_sems = refs[2 * n:2 * n + 3]
        x, y, c, chips = _place()
        mine = 2 * x + y
        for t in range(n):
            pltpu.make_async_copy(ins[t].at[mine], lnd[t].at[mine], local_sems.at[t]).wait()
            for j, (qx, qy) in enumerate(chips):
                cp = pltpu.make_async_remote_copy(
                    src_ref=ins[t].at[2 * qx + qy], dst_ref=lnd[t].at[mine],
                    send_sem=send_sems.at[3 * t + j], recv_sem=recv_sems.at[3 * t + j],
                    device_id=(qx, qy, c), device_id_type=MESH)
                cp.wait_send()
                cp.wait_recv()

    bufs = list(partials) + list(lands)
    outs = pl.pallas_call(
        kern, name=name, out_shape=tuple(pltpu.HBM(a.shape, a.dtype) for a in bufs),
        in_specs=[HBM] * (2 * n) + [SEM, SEM, SEM, ANY], out_specs=[HBM] * (2 * n),
        input_output_aliases={i: i for i in range(2 * n)},
        compiler_params=pltpu.CompilerParams(has_side_effects=DATAFLOW),
    )(*bufs, send_sems, recv_sems, local_sems, after)
    return outs[n:]


def adamw_layers(parts, w, m, v, name):
    nl, kdim, n = w.shape
    tk = _row_tile(kdim, n)

    def kern(*refs):
        p_refs = refs[:nl]
        w_ref, m_ref, v_ref, g_ref, d_ref, m2_ref, v2_ref = refs[nl:]
        for l in range(nl):
            @pl.when(pl.program_id(0) == l)
            def _():
                g = p_refs[l][0].astype(F32)
                for q in range(1, 4):
                    g = g + p_refs[l][q].astype(F32)
                d, m2, v2 = _adam_update(w_ref[...], g, m_ref[...], v_ref[...])
                g_ref[...] = g
                d_ref[...] = d
                m2_ref[...] = m2
                v2_ref[...] = v2

    def part_spec(l):
        return pl.BlockSpec((4, tk, n), lambda li, i: (0, jnp.where(li == l, i, 0), 0))

    blk = pl.BlockSpec((None, tk, n), lambda li, i: (li, i, 0))
    return pl.pallas_call(
        kern, name=name, grid=(nl, kdim // tk),
        in_specs=[part_spec(l) for l in range(nl)] + [blk, blk, blk],
        out_specs=[blk] * 4, out_shape=[_sds(w.shape, F32)] * 4,
        compiler_params=_params("arbitrary", "arbitrary"),
    )(*parts, w, m, v)


def _pack(arrays):
    flat = jnp.concatenate([a.reshape(-1).astype(F32) for a in arrays])
    pad = (-flat.shape[0]) % (8 * LANES)
    return jnp.pad(flat, (0, pad)).reshape(-1, LANES)


def _unpack(packed, shapes):
    flat = packed.reshape(-1)
    out, off = [], 0
    for s in shapes:
        n = math.prod(s)
        out.append(flat[off:off + n].reshape(s))
        off += n
    return out


def _to_heads(x2d, dil, n_heads, dh):
    t = x2d.shape[0]
    return x2d.reshape(t // dil, dil, n_heads, dh).transpose(2, 1, 0, 3).reshape(n_heads, t, dh)


def _from_heads(xh, dil):
    h, t, w = xh.shape
    return xh.reshape(h, dil, t // dil, w).transpose(2, 1, 0, 3).reshape(t, h * w)


def _unperm(xh, dil):
    h, t, w = xh.shape
    return xh.reshape(h, dil, t // dil, w).transpose(0, 2, 1, 3).reshape(h, t, w)


def _perm(xh, dil):
    h, t, w = xh.shape
    return xh.reshape(h, t // dil, dil, w).transpose(0, 2, 1, 3).reshape(h, t, w)


def local_step(x, target, norm_mix_g, norm_mlp_g, final_norm_g, lbs, hgrn_norm_g, sinks,
               bq_full, bo_full, weights_get, weights_mid, grads_ready):
    t, d = x.shape
    depth = norm_mix_g.shape[0]
    na = d // 2 // A_DIM
    nbh = d // 2 // B_DIM
    nq = d // C_DIM
    nkv = nq // C_GROUP
    a_w = 3 * na * A_DIM
    c_w = (nq + 2 * nkv) * C_DIM
    tabs_a = rope_tables(t, A_DIM)
    tabs_c = rope_tables(t, C_DIM)
    saved = []
    for l in range(depth):
        s = {"x_in": x}
        (win_g, wout_g), token = weights_get(l, x)
        h = rms_fwd(x, norm_mix_g[l] + token, "norm_mix_fwd")
        s["h"] = h
        if l % 2 == 0:
            e = l // 2
            proj = mm_cols_sharded(h, win_g, 0, "even_in_proj")[0]
            qkv_r = rope_call(proj, tabs_a, a_w, 2 * na, False, "rope_a")[0]
            nums, ms, ls, hms = [], [], [], []
            for window, dil in A_BRANCHES:
                hm = _to_heads(qkv_r, dil, 3 * na, A_DIM)
                num, m, lsum = band_fwd(hm, 0, na, 2 * na, na, 1, t // dil // BLK, window // dil,
                                        f"dilated_fwd_{dil}")
                hms.append(hm)
                nums.append(_unperm(num, dil))
                ms.append(_unperm(m, dil))
                ls.append(_unperm(lsum, dil))
            oa, lse = merge_branches(nums, ms, ls, "dilated_merge")
            lb_e = lbs[e].reshape(nbh, 1, B_DIM)
            gn_e = hgrn_norm_g[e].reshape(1, B_DIM)
            ob, opre, states, amats = hgrn_fwd(proj, 3 * na, nbh, lb_e, gn_e, "hgrn_fwd")
            mixed = jnp.concatenate([_from_heads(oa, 1).astype(BF16), ob], axis=1)
            x = mm_rows_sharded(mixed, wout_g, 0, "even_out_proj", [x], ["tile"], _ep_residual)
            s.update(proj=proj, hms=hms, oa=oa, lse=lse, opre=opre, states=states, amats=amats,
                     mixed=mixed, lb=lb_e, gn=gn_e)
        else:
            o = l // 2
            wq = win_g[:, 0].transpose(1, 0, 2).reshape(d, c_w)
            proj = mm_plain(h, wq, "odd_qkv_proj", [bq_full[o].reshape(1, c_w)], ["row"], _ep_bias)
            qkv_r = rope_call(proj, tabs_c, c_w, (nq + nkv) * C_DIM // LANES, False, "rope_c")[0]
            hm = _to_heads(qkv_r, 1, nq + 2 * nkv, C_DIM)
            sink_rows = jnp.repeat(sinks[o].reshape(nkv, C_GROUP), BLK, axis=1).reshape(
                nkv, C_GROUP * BLK, 1)
            o_hm, lse = band_fwd(hm, 0, nq, nq + nkv, nkv, C_GROUP, t // BLK, C_WINDOW - 1,
                                 "swa_fwd", sink_rows=sink_rows, normalise=True)
            attn = _from_heads(o_hm, 1).astype(BF16)
            x = mm_rows_sharded(attn, wout_g, 0, "odd_out_proj", [bo_full[o].reshape(1, d), x],
                                ["row", "tile"], _ep_bias_residual)
            s.update(wq=wq, hm=hm, sink_rows=sink_rows, o_hm=o_hm, lse=lse, attn=attn)
        s["x_mid"] = x
        (w1_g, w2_g), token = weights_mid(l, x)
        s.update(win=win_g, wout=wout_g, w1=w1_g, w2=w2_g)
        h2 = rms_fwd(x, norm_mlp_g[l] + token, "norm_mlp_fwd")
        u, act = mm_cols_sharded(h2, w1_g, 0, "mlp_up", epilogue=_ep_relu2, n_out=2)
        x = mm_rows_sharded(act, w2_g, 0, "mlp_down", [x], ["tile"], _ep_residual)
        s.update(h2=h2, u=u, act=act)
        saved.append(s)

    dx, dxb, dg_final, loss_part = loss_head(x, final_norm_g, target, "loss_head")
    big = []
    small = {"final": dg_final, "loss": loss_part, "mix": [None] * depth, "mlp": [None] * depth,
             "lb": {}, "gn": {}, "sinks": {}, "bq": {}, "bo": {}}
    for l in reversed(range(depth)):
        s = saved[l]
        win_g, wout_g, w1_g, w2_g = s["win"], s["wout"], s["w1"], s["w2"]
        big.append(("w2", l, mm_tn(s["act"], dxb, "mlp_down_dw").reshape(N_DEV, -1, d)))
        du = mm_nt_rows_sharded(dxb, w2_g, 0, "mlp_down_dx", extras=[s["u"]],
                                epilogue=_ep_relu2_bwd, out_dtype=BF16)
        big.append(("w1", l, mm_tn(s["h2"], du, "mlp_up_dw", shard_cols=w1_g.shape[-1])))
        dh2 = mm_nt_cols_sharded(du, w1_g, 0, "mlp_up_dx")
        token = grads_ready(l, big[-2:])
        dx, dxb, dg, col_dx = rms_bwd(s["x_mid"], norm_mlp_g[l] + token, dh2, dx, "norm_mlp_bwd")
        small["mlp"][l] = dg
        if l % 2 == 0:
            e = l // 2
            big.append(("wout", e, mm_tn(s["mixed"], dxb, "even_out_dw").reshape(N_DEV, -1, d)))
            dmixed = mm_nt_rows_sharded(dxb, wout_g, 0, "even_out_dx")
            do_hm = _to_heads(dmixed[:, :na * A_DIM], 1, na, A_DIM)
            delta = head_delta(s["oa"], do_hm, "dilated_delta")
            dsum = None
            for (window, dil), hm in zip(A_BRANCHES, s["hms"]):
                dq, dk, dv = band_bwd(hm, 0, na, 2 * na, _perm(do_hm, dil).astype(BF16),
                                      _perm(s["lse"], dil), _perm(delta, dil), na, 1,
                                      t // dil // BLK, window // dil, f"dilated_bwd_{dil}")
                part = _from_heads(jnp.concatenate([dq, dk, dv], axis=0), dil)
                dsum = part if dsum is None else dsum + part
            dqkv_a = rope_call(dsum, tabs_a, a_w, 2 * na, True, "rope_a_bwd")[0]
            dqb, dfb, dib, dgb, dgn, dlb = hgrn_bwd(s["proj"], 3 * na, nbh, s["lb"], s["gn"],
                                                    s["opre"], s["states"], s["amats"], dmixed, na,
                                                    "hgrn_bwd")
            small["gn"][e] = dgn
            small["lb"][e] = dlb
            dproj = jnp.concatenate([dqkv_a, dqb, dfb, dib, dgb], axis=1)
            big.append(("win", e, mm_tn(s["h"], dproj, "even_in_dw", shard_cols=win_g.shape[-1])))
            dh = mm_nt_cols_sharded(dproj, win_g, 0, "even_in_dx")
        else:
            o = l // 2
            small["bo"][o] = col_dx
            big.append(("wo", o, mm_tn(s["attn"], dxb, "odd_out_dw").reshape(N_DEV, -1, d)))
            dattn = mm_nt_rows_sharded(dxb, wout_g, 0, "odd_out_dx")
            do_hm = _to_heads(dattn, 1, nq, C_DIM)
            dq, dk, dv, dsink = band_bwd(s["hm"], 0, nq, nq + nkv, do_hm, s["lse"], s["o_hm"],
                                         nkv, C_GROUP, t // BLK, C_WINDOW - 1, "swa_bwd",
                                         sink_rows=s["sink_rows"], delta_from_o=True)
            small["sinks"][o] = dsink
            dqkv = _from_heads(jnp.concatenate([dq, dk, dv], axis=0), 1)
            dproj, dbq = rope_call(dqkv, tabs_c, c_w, (nq + nkv) * C_DIM // LANES, True,
                                   "rope_c_bwd", col_sum=True)
            small["bq"][o] = dbq
            dwq = mm_tn(s["h"], dproj, "odd_qkv_dw", tn=512)
            big.append(("wqkv", o, dwq.reshape(d, N_DEV, -1).transpose(1, 0, 2)))
            dh = mm_nt_plain(dproj, s["wq"], "odd_qkv_dx", tk=c_w)
        token = grads_ready(l, big[-2:])
        dx, dxb, dg, _ = rms_bwd(s["x_in"], norm_mix_g[l] + token, dh, dx, "norm_mix_bwd")
        small["mix"][l] = dg
    return dx, small


def kernel(x, norm_mix_g, norm_mlp_g, final_norm_g, even_w_in, even_w_out, hgrn_lb_raw, hgrn_norm_g, odd_w_qkv, odd_b_qkv, odd_sinks, odd_w_o, odd_b_o, mlp_w1, mlp_w2, loss_target, m_norm_mix_g, m_norm_mlp_g, m_final_norm_g, m_even_w_in, m_even_w_out, m_hgrn_lb_raw, m_hgrn_norm_g, m_odd_w_qkv, m_odd_b_qkv, m_odd_sinks, m_odd_w_o, m_odd_b_o, m_mlp_w1, m_mlp_w2, v_norm_mix_g, v_norm_mlp_g, v_final_norm_g, v_even_w_in, v_even_w_out, v_hgrn_lb_raw, v_hgrn_norm_g, v_odd_w_qkv, v_odd_b_qkv, v_odd_sinks, v_odd_w_o, v_odd_b_o, v_mlp_w1, v_mlp_w2):
    d = x.shape[2]
    depth = norm_mix_g.shape[0]
    n_even, n_odd = even_w_in.shape[0], odd_w_qkv.shape[0]
    xi, yi, ci = lax.axis_index("x"), lax.axis_index("y"), lax.axis_index("c")
    dev = 4 * xi + 2 * yi + ci
    core = ci.astype(jnp.int32).reshape(1)

    big_w = {"win": even_w_in, "wout": even_w_out, "wqkv": odd_w_qkv, "wo": odd_w_o,
             "w1": mlp_w1, "w2": mlp_w2}
    big_m = {"win": m_even_w_in, "wout": m_even_w_out, "wqkv": m_odd_w_qkv, "wo": m_odd_w_o,
             "w1": m_mlp_w1, "w2": m_mlp_w2}
    big_v = {"win": v_even_w_in, "wout": v_even_w_out, "wqkv": v_odd_w_qkv, "wo": v_odd_w_o,
             "w1": v_mlp_w1, "w2": v_mlp_w2}
    kinds = list(big_w)
    casts = {k: cast_bf16(big_w[k], f"cast_{k}") for k in kinds}

    def layer_shards(l):
        a, b = ("win", "wout") if l % 2 == 0 else ("wqkv", "wo")
        return [casts[a][l // 2], casts[b][l // 2], casts["w1"][l], casts["w2"][l]]

    bq_w, bo_w = odd_b_qkv.shape[1], odd_b_o.shape[1]
    bq_mine = lax.dynamic_update_slice(jnp.zeros((n_odd, N_DEV * bq_w), F32), odd_b_qkv,
                                       (0, dev * bq_w))
    bo_mine = lax.dynamic_update_slice(jnp.zeros((n_odd, N_DEV * bo_w), F32), odd_b_o,
                                       (0, dev * bo_w))
    biases = all_reduce_small(_pack([bq_mine, bo_mine]), "gather_biases")
    bq_full, bo_full = _unpack(biases, [bq_mine.shape, bo_mine.shape])

    first_level = {}
    second_level = {}
    ready = {}
    zero = jnp.zeros((), F32)

    def start_first_level(key, shards, after):
        started = gather_start(shards, after, f"gather_start_{key}")
        first_level[key] = started[:-1]
        return started[-1]

    def finish_first_level(key, after):
        send_sems, recv_sems, local_sems, *bufs = first_level.pop(key)
        n = len(bufs) // 2
        bufs = gather_wait(send_sems, recv_sems, local_sems, bufs[:n], bufs[n:], after,
                           f"gather_wait_{key}")
        return bufs[n:]

    def weights_get(l, after):
        if l == 0:
            shards = layer_shards(0)
            mixer = all_gather(shards[:2], "gather_layer_0_mixer")
            token = start_first_level("0_mlp", shards[2:], [mixer[0], biases])
            for ahead in range(1, min(depth, 3)):
                token = start_first_level(ahead, layer_shards(ahead), [token])
            return [g[:, None] for g in mixer], token[0, 0]
        if l == 1:
            gathered = forward_now(finish_first_level(1, after), "gather_forward_1")
        else:
            send_sems, recv_sems, *lands = second_level.pop(l)
            gathered = forward_wait(send_sems, recv_sems, lands, after,
                                    f"gather_forward_wait_{l}")
        ready[l] = gathered[2:]
        return [g[:, None] for g in gathered[:2]], zero

    def weights_mid(l, after):
        token = zero
        if l == 0:
            ready[0] = forward_now(finish_first_level("0_mlp", after), "gather_forward_0_mlp")
            order_after = ready[0][0]
        elif l + 1 < depth:
            started = forward_start(finish_first_level(l + 1, after),
                                    f"gather_forward_start_{l + 1}")
            second_level[l + 1] = started[:-1]
            token = started[-1][0, 0]
            order_after = started[-1]
        if l + 3 < depth:
            token = token + start_first_level(l + 3, layer_shards(l + 3), [order_after])[0, 0]
        return [g[:, None] for g in ready.pop(l)], token

    exchanging = []
    scattering = []

    def finish_exchange(after):
        tag, names, layer_idx, (send_sems, recv_sems, *bufs) = exchanging.pop()
        n = len(names)
        grads, received = sibling_wait(send_sems, recv_sems, bufs[:n], bufs[n:], after,
                                       f"scatter_d2d_wait_{tag}")
        partials = [pair_add(g, r, core, f"pair_add_{k}")
                    for k, g, r in zip(names, grads, received)]
        started = scatter_start(partials, f"scatter_start_{tag}")
        scattering.append((tag, names, layer_idx, started[:-1]))
        return started[-1][0, 0]

    def grads_ready(l, group):
        names = [k for k, _, _ in group]
        grads = [g for _, _, g in group]
        tag = f"{l}_{names[0]}"
        token = finish_exchange(grads[0]) if exchanging else zero
        started = sibling_start(grads, f"scatter_d2d_start_{tag}")
        exchanging.append((tag, names, [li for _, li, _ in group], started[:-1]))
        return token + started[-1][0, 0]

    lbs, soft = lower_bounds_fwd(hgrn_lb_raw, "lower_bounds")

    dx, small = local_step(x[0], loss_target[0], norm_mix_g, norm_mlp_g, final_norm_g, lbs,
                           hgrn_norm_g, odd_sinks, bq_full, bo_full, weights_get, weights_mid,
                           grads_ready)
    finish_exchange(dx)

    parts = ([small["mix"][l] for l in range(depth)] + [small["mlp"][l] for l in range(depth)]
             + [small["final"]] + [small["lb"][e] for e in range(n_even)]
             + [small["gn"][e] for e in range(n_even)] + [small["sinks"][o] for o in range(n_odd)]
             + [small["bq"][o] for o in range(n_odd)] + [small["bo"][o] for o in range(n_odd)]
             + [small["loss"]])
    shapes = ([(depth, d)] * 2 + [(d,), hgrn_lb_raw.shape, hgrn_norm_g.shape, odd_sinks.shape,
              (n_odd, N_DEV * bq_w), (n_odd, N_DEV * bo_w), (1, LANES)])
    g_mix, g_mlp, g_final, d_lbs, g_gn, g_sinks, g_bq_full, g_bo_full, loss_v = _unpack(
        all_reduce_small(_pack(parts), "reduce_small"), shapes)
    g_lb = lower_bounds_bwd(soft, d_lbs, "lower_bounds_bwd")
    g_bq = lax.dynamic_slice(g_bq_full, (0, dev * bq_w), (n_odd, bq_w))
    g_bo = lax.dynamic_slice(g_bo_full, (0, dev * bo_w), (n_odd, bo_w))
    loss = loss_v[0, 0]

    small_names = ["norm_mix_g", "norm_mlp_g", "final_norm_g", "hgrn_lb_raw", "hgrn_norm_g",
                   "odd_b_qkv", "odd_sinks", "odd_b_o"]
    small_w = [norm_mix_g, norm_mlp_g, final_norm_g, hgrn_lb_raw, hgrn_norm_g, odd_b_qkv,
               odd_sinks, odd_b_o]
    small_m = [m_norm_mix_g, m_norm_mlp_g, m_final_norm_g, m_hgrn_lb_raw, m_hgrn_norm_g,
               m_odd_b_qkv, m_odd_sinks, m_odd_b_o]
    small_v = [v_norm_mix_g, v_norm_mlp_g, v_final_norm_g, v_hgrn_lb_raw, v_hgrn_norm_g,
               v_odd_b_qkv, v_odd_sinks, v_odd_b_o]
    small_g = [g_mix, g_mlp, g_final, g_lb, g_gn, g_bq, g_sinks, g_bo]
    sshapes = [w.shape for w in small_w]
    sd, sm, sv = adamw_small(_pack(small_w), _pack(small_g), _pack(small_m), _pack(small_v),
                             "adamw_small")
    res = {}
    for name, g, dl, m2, v2 in zip(small_names, small_g, _unpack(sd, sshapes),
                                   _unpack(sm, sshapes), _unpack(sv, sshapes)):
        res[name] = (g.reshape(dl.shape), dl, m2, v2)

    landed = {k: [None] * big_w[k].shape[0] for k in kinds}
    for tag, names, layer_idx, (send_sems, recv_sems, local_sems, *bufs) in scattering:
        n = len(names)
        lands = scatter_wait(send_sems, recv_sems, local_sems, bufs[:n], bufs[n:], dx,
                             f"scatter_wait_{tag}")
        for k, li, land in zip(names, layer_idx, lands):
            landed[k][li] = land
    long_names = {"win": "even_w_in", "wout": "even_w_out", "wqkv": "odd_w_qkv", "wo": "odd_w_o",
                  "w1": "mlp_w1", "w2": "mlp_w2"}
    for k in kinds:
        res[long_names[k]] = tuple(adamw_layers(landed[k], big_w[k], big_m[k], big_v[k],
                                                f"adamw_{k}"))

    order = ["norm_mix_g", "norm_mlp_g", "final_norm_g", "even_w_in", "even_w_out", "hgrn_lb_raw",
             "hgrn_norm_g", "odd_w_qkv", "odd_b_qkv", "odd_sinks", "odd_w_o", "odd_b_o", "mlp_w1",
             "mlp_w2"]
    outs = [loss, dx[None]]
    for j in range(4):
        outs += [res[n][j] for n in order]
    return tuple(outs)
```

```python
import functools
import math

import jax
import jax.numpy as jnp
from jax import lax
from jax.experimental import pallas as pl
from jax.experimental.pallas import tpu as pltpu

F32 = jnp.float32
BF16 = jnp.bfloat16
MESH = pl.DeviceIdType.MESH

N_DEV = 8
NORM_EPS = 1e-5
ROPE_THETA = 500000.0
BLK = 128
A_DIM = 128
A_BRANCHES = ((128, 1), (512, 4), (2048, 16))
B_DIM = 128
B_CHUNK = 64
C_DIM = 64
C_GROUP = 8
C_WINDOW = 128
LANES = 128

ADAM_LR = 0.001
ADAM_B1 = 0.9
ADAM_B2 = 0.999
ADAM_EPS = 1e-08
ADAM_WD = 0.01
ADAM_STEP = 10

NN = (((1,), (0,)), ((), ()))
NT = (((1,), (1,)), ((), ()))
TN = (((0,), (0,)), ((), ()))


def _params(*sem):
    return pltpu.CompilerParams(dimension_semantics=sem)


def _sigmoid(x):
    return 1.0 / (1.0 + jnp.exp(-x))


def _rows_call(name, body, row_ins, full_ins, row_outs, acc_outs, tm):
    t = row_ins[0].shape[0]
    n_ri, n_fi, n_ro = len(row_ins), len(full_ins), len(row_outs)

    def kern(*refs):
        i = pl.program_id(0)
        body(i, refs[:n_ri], refs[n_ri:n_ri + n_fi],
             refs[n_ri + n_fi:n_ri + n_fi + n_ro], refs[n_ri + n_fi + n_ro:])

    def row_spec(shape):
        return pl.BlockSpec((tm,) + tuple(shape[1:]), lambda i: (i,) + (0,) * (len(shape) - 1))

    def full_spec(shape):
        return pl.BlockSpec(tuple(shape), lambda i: (0,) * len(shape))

    outs = pl.pallas_call(
        kern, name=name, grid=(t // tm,),
        in_specs=[row_spec(a.shape) for a in row_ins] + [full_spec(a.shape) for a in full_ins],
        out_specs=[row_spec(s.shape) for s in row_outs] + [full_spec(s.shape) for s in acc_outs],
        out_shape=list(row_outs) + list(acc_outs),
        compiler_params=_params("arbitrary" if acc_outs else "parallel"),
    )(*row_ins, *full_ins)
    return outs


def _sds(shape, dtype):
    return jax.ShapeDtypeStruct(tuple(shape), dtype)


def rms_fwd(x, g, name):
    t, d = x.shape

    def body(i, ri, fi, ro, ao):
        xv = ri[0][...]
        r = lax.rsqrt(jnp.mean(xv * xv, axis=-1, keepdims=True) + NORM_EPS)
        ro[0][...] = (xv * r * fi[0][...]).astype(BF16)

    return _rows_call(name, body, [x], [g.reshape(1, d)], [_sds((t, d), BF16)], [], 256)[0]


def rms_bwd(x, g, dh, dx_res, name):
    t, d = x.shape

    def body(i, ri, fi, ro, ao):
        xv, dhv, res = ri[0][...], ri[1][...], ri[2][...]
        gv = fi[0][...]
        r = lax.rsqrt(jnp.mean(xv * xv, axis=-1, keepdims=True) + NORM_EPS)
        gd = gv * dhv
        dx = res + r * gd - xv * (r * r * r) * jnp.mean(xv * gd, axis=-1, keepdims=True)
        ro[0][...] = dx
        ro[1][...] = dx.astype(BF16)

        @pl.when(i == 0)
        def _():
            ao[0][...] = jnp.zeros_like(ao[0])
            ao[1][...] = jnp.zeros_like(ao[1])

        ao[0][...] += jnp.sum(dhv * xv * r, axis=0, keepdims=True)
        ao[1][...] += jnp.sum(dx, axis=0, keepdims=True)

    return _rows_call(name, body, [x, dh, dx_res], [g.reshape(1, d)],
                      [_sds((t, d), F32), _sds((t, d), BF16)],
                      [_sds((1, d), F32), _sds((1, d), F32)], 256)


def loss_head(x, g, target, name):
    t, d = x.shape

    def body(i, ri, fi, ro, ao):
        xv, tg = ri[0][...], ri[1][...]
        gv = fi[0][...]
        r = lax.rsqrt(jnp.mean(xv * xv, axis=-1, keepdims=True) + NORM_EPS)
        e = xv * r * gv - tg
        dy = e * (1.0 / d)
        gd = gv * dy
        dx = r * gd - xv * (r * r * r) * jnp.mean(xv * gd, axis=-1, keepdims=True)
        ro[0][...] = dx
        ro[1][...] = dx.astype(BF16)

        @pl.when(i == 0)
        def _():
            ao[0][...] = jnp.zeros_like(ao[0])
            ao[1][...] = jnp.zeros_like(ao[1])

        ao[0][...] += jnp.sum(dy * xv * r, axis=0, keepdims=True)
        part = 0.5 * jnp.sum(jnp.mean(e * e, axis=-1, keepdims=True), axis=0, keepdims=True)
        ao[1][...] += jnp.broadcast_to(part, (1, LANES))

    return _rows_call(name, body, [x, target], [g.reshape(1, d)],
                      [_sds((t, d), F32), _sds((t, d), BF16)],
                      [_sds((1, d), F32), _sds((1, LANES), F32)], 256)


def rope_tables(seq, head_dim):
    rot = head_dim // 4
    half = rot // 2
    inv_freq = 1.0 / (ROPE_THETA ** (jnp.arange(0, rot, 2, dtype=F32) / rot))
    ang = jnp.arange(seq, dtype=F32)[:, None] * inv_freq[None, :]
    cos, sin = jnp.cos(ang), jnp.sin(ang)
    zeros = jnp.zeros((seq, head_dim - rot), F32)
    zh = jnp.zeros((seq, half), F32)
    c = jnp.concatenate([cos, cos, jnp.ones((seq, head_dim - rot), F32)], axis=-1)
    sp = jnp.concatenate([zh, sin, zeros], axis=-1)
    sm = jnp.concatenate([-sin, zh, zeros], axis=-1)
    rep = LANES // head_dim
    return jnp.tile(c, (1, rep)), jnp.tile(sp, (1, rep)), jnp.tile(sm, (1, rep)), half


def rope_call(x, tabs, width, n_rope, inverse, name, col_sum=False):
    c, sp, sm, half = tabs
    t = x.shape[0]
    tm = 256
    n_slab = width // LANES

    def kern(x_ref, c_ref, sp_ref, sm_ref, o_ref, *acc):
        cv, spv, smv = c_ref[...], sp_ref[...], sm_ref[...]
        for j in range(n_slab):
            xs = x_ref[:, j * LANES:(j + 1) * LANES].astype(F32)
            if j < n_rope:
                if inverse:
                    ys = (xs * cv + pltpu.roll(xs * spv, LANES - half, 1)
                          + pltpu.roll(xs * smv, half, 1))
                else:
                    ys = (xs * cv + pltpu.roll(xs, half, 1) * spv
                          + pltpu.roll(xs, LANES - half, 1) * smv)
            else:
                ys = xs
            o_ref[:, j * LANES:(j + 1) * LANES] = ys.astype(BF16)
            if col_sum:
                @pl.when(pl.program_id(0) == 0)
                def _():
                    acc[0][:, j * LANES:(j + 1) * LANES] = jnp.zeros((1, LANES), F32)
                acc[0][:, j * LANES:(j + 1) * LANES] += jnp.sum(ys, axis=0, keepdims=True)

    tab_spec = pl.BlockSpec((tm, LANES), lambda i: (i, 0))
    out_shape = [_sds((t, width), BF16)]
    out_specs = [pl.BlockSpec((tm, width), lambda i: (i, 0))]
    if col_sum:
        out_shape.append(_sds((1, width), F32))
        out_specs.append(pl.BlockSpec((1, width), lambda i: (0, 0)))
    return pl.pallas_call(
        kern, name=name, grid=(t // tm,),
        in_specs=[pl.BlockSpec((tm, width), lambda i: (i, 0)), tab_spec, tab_spec, tab_spec],
        out_specs=out_specs, out_shape=out_shape,
        compiler_params=_params("arbitrary" if col_sum else "parallel"),
    )(x, c, sp, sm)


def _mm_call(name, a, b, extras, out_shapes, grid, a_spec, b_spec, extra_specs, out_specs,
             acc_shape, dims, epilogue):
    n_ex, n_out = len(extras), len(out_shapes)
    nk = grid[2]

    def product(a_ref, b_ref):
        bv = b_ref[...]
        if bv.ndim == 3:
            bv = bv.reshape(bv.shape[0] * bv.shape[1], bv.shape[2])
        return lax.dot_general(a_ref[...].astype(BF16), bv.astype(BF16), dims,
                               preferred_element_type=F32)

    def kern(*refs):
        a_ref, b_ref = refs[0], refs[1]
        ex = refs[2:2 + n_ex]
        outs = refs[2 + n_ex:2 + n_ex + n_out]
        if nk == 1:
            epilogue(product(a_ref, b_ref), ex, outs)
            return
        acc = refs[-1]
        k = pl.program_id(2)

        @pl.when(k == 0)
        def _():
            acc[...] = product(a_ref, b_ref)

        @pl.when(k > 0)
        def _():
            acc[...] += product(a_ref, b_ref)

        @pl.when(k == nk - 1)
        def _():
            epilogue(acc[...], ex, outs)

    return pl.pallas_call(
        kern, name=name, grid=grid,
        in_specs=[a_spec, b_spec, *extra_specs], out_specs=out_specs, out_shape=out_shapes,
        scratch_shapes=[pltpu.VMEM(acc_shape, F32)] if nk > 1 else [],
        compiler_params=_params("parallel", "parallel", "arbitrary"),
    )(a, b, *extras)


def _ep_store(dtype):
    def ep(acc, ex, outs):
        outs[0][...] = acc.astype(dtype)
    return ep


def _ep_residual(acc, ex, outs):
    outs[0][...] = acc + ex[0][...]


def _ep_bias(acc, ex, outs):
    outs[0][...] = acc + ex[0][...]


def _ep_bias_residual(acc, ex, outs):
    outs[0][...] = acc + ex[0][...] + ex[1][...]


def _ep_relu2(acc, ex, outs):
    outs[0][...] = acc
    rl = jnp.maximum(acc, 0.0)
    outs[1][...] = (rl * rl).astype(BF16)


def _ep_relu2_bwd(acc, ex, outs):
    outs[0][...] = (acc * (2.0 * jnp.maximum(ex[0][...], 0.0))).astype(BF16)


MM_TM = 1024
MM_TN = 1024
MM_TK = 2048


def mm_cols_sharded(a, wg, layer, name, epilogue=None, n_out=1):
    m, kdim = a.shape
    n = wg.shape[-1]
    tm, tk = min(m, MM_TM), min(kdim, MM_TK)
    if epilogue is None:
        epilogue, outs = _ep_store(F32), [_sds((m, N_DEV * n), F32)]
    else:
        outs = [_sds((m, N_DEV * n), F32), _sds((m, N_DEV * n), BF16)][:n_out]
    return _mm_call(
        name, a, wg, [], outs, (m // tm, N_DEV, kdim // tk),
        pl.BlockSpec((tm, tk), lambda i, j, k: (i, k)),
        pl.BlockSpec((None, None, tk, n), lambda i, j, k: (j, layer, k, 0)),
        [], [pl.BlockSpec((tm, n), lambda i, j, k: (i, j))] * len(outs),
        (tm, n), NN, epilogue)


def _extra_specs(extra_kinds, tm, tn):
    specs = []
    for kind in extra_kinds:
        if kind == "row":
            specs.append(pl.BlockSpec((1, tn), lambda i, j, k: (0, j)))
        else:
            specs.append(pl.BlockSpec((tm, tn), lambda i, j, k: (i, j)))
    return specs


def mm_rows_sharded(a, wg, layer, name, extras, extra_kinds, epilogue):
    m, kdim = a.shape
    ks, n = wg.shape[-2], wg.shape[-1]
    tm, tn = min(m, MM_TM), min(n, MM_TN)
    gps = max(1, min(kdim, MM_TK) // ks)
    return _mm_call(
        name, a, wg, extras, [_sds((m, n), F32)], (m // tm, n // tn, N_DEV // gps),
        pl.BlockSpec((tm, gps * ks), lambda i, j, k: (i, k)),
        pl.BlockSpec((gps, None, ks, tn), lambda i, j, k: (k, layer, 0, j)),
        _extra_specs(extra_kinds, tm, tn), [pl.BlockSpec((tm, tn), lambda i, j, k: (i, j))],
        (tm, tn), NN, epilogue)[0]


def mm_plain(a, w, name, extras, extra_kinds, epilogue, tn=512):
    m, kdim = a.shape
    n = w.shape[1]
    tm, tk = min(m, MM_TM), min(kdim, MM_TK)
    return _mm_call(
        name, a, w, extras, [_sds((m, n), F32)], (m // tm, n // tn, kdim // tk),
        pl.BlockSpec((tm, tk), lambda i, j, k: (i, k)),
        pl.BlockSpec((tk, tn), lambda i, j, k: (k, j)),
        _extra_specs(extra_kinds, tm, tn), [pl.BlockSpec((tm, tn), lambda i, j, k: (i, j))],
        (tm, tn), NN, epilogue)[0]


def mm_nt_cols_sharded(dy, wg, layer, name):
    m = dy.shape[0]
    kdim, n = wg.shape[-2], wg.shape[-1]
    tm, tn = min(m, MM_TM), min(kdim, MM_TN)
    return _mm_call(
        name, dy, wg, [], [_sds((m, kdim), F32)], (m // tm, kdim // tn, N_DEV),
        pl.BlockSpec((tm, n), lambda i, j, k: (i, k)),
        pl.BlockSpec((None, None, tn, n), lambda i, j, k: (k, layer, j, 0)),
        [], [pl.BlockSpec((tm, tn), lambda i, j, k: (i, j))],
        (tm, tn), NT, _ep_store(F32))[0]


def mm_nt_rows_sharded(dy, wg, layer, name, extras=(), epilogue=None, out_dtype=F32):
    m, n = dy.shape
    ks = wg.shape[-2]
    tm, tk = min(m, MM_TM), min(n, MM_TK)
    gps = max(1, MM_TN // ks)
    tn = gps * ks
    epilogue = _ep_store(out_dtype) if epilogue is None else epilogue
    return _mm_call(
        name, dy, wg, list(extras), [_sds((m, N_DEV * ks), out_dtype)],
        (m // tm, N_DEV // gps, n // tk),
        pl.BlockSpec((tm, tk), lambda i, j, k: (i, k)),
        pl.BlockSpec((gps, None, ks, tk), lambda i, j, k: (j, layer, 0, k)),
        [pl.BlockSpec((tm, tn), lambda i, j, k: (i, j))] * len(extras),
        [pl.BlockSpec((tm, tn), lambda i, j, k: (i, j))],
        (tm, tn), NT, epilogue)[0]


def mm_nt_plain(dy, w, name, tk):
    m, n = dy.shape
    kdim = w.shape[0]
    tm, tn = min(m, MM_TM), min(kdim, MM_TN)
    return _mm_call(
        name, dy, w, [], [_sds((m, kdim), F32)], (m // tm, kdim // tn, n // tk),
        pl.BlockSpec((tm, tk), lambda i, j, k: (i, k)),
        pl.BlockSpec((tn, tk), lambda i, j, k: (j, k)),
        [], [pl.BlockSpec((tm, tn), lambda i, j, k: (i, j))],
        (tm, tn), NT, _ep_store(F32))[0]


def mm_tn(a, dy, name, shard_cols=None, tn=MM_TN):
    t, kdim = a.shape
    n = dy.shape[1]
    tm, tk = min(kdim, MM_TM), min(t, MM_TK)
    if shard_cols is None:
        tn = min(tn, n)
        out = _sds((kdim, n), BF16)
        o_spec = pl.BlockSpec((tm, tn), lambda i, j, k: (i, j))
    else:
        tn = shard_cols
        out = _sds((n // tn, kdim, tn), BF16)
        o_spec = pl.BlockSpec((None, tm, tn), lambda i, j, k: (j, i, 0))
    return _mm_call(
        name, a, dy, [], [out], (kdim // tm, n // tn, t // tk),
        pl.BlockSpec((tk, tm), lambda i, j, k: (k, i)),
        pl.BlockSpec((tk, tn), lambda i, j, k: (k, j)),
        [], [o_spec], (tm, tn), TN, _ep_store(BF16))[0]


BAND_BLOCKS_PER_STEP = 4
BAND_BLOCKS_PER_STEP_GROUPED = 2


def _band_mask(g, nk_prev_valid, max_dist):
    rows = lax.broadcasted_iota(jnp.int32, (g * BLK, 2 * BLK), 0) % BLK
    cols = lax.broadcasted_iota(jnp.int32, (g * BLK, 2 * BLK), 1)
    dist = rows + BLK - cols
    ok = (dist >= 0) & (dist <= max_dist)
    return ok & ((cols >= BLK) | nk_prev_valid)


def band_fwd(qkv, q0, k0, v0, hk, g, seg, max_dist, name, sink_rows=None, normalise=False):
    t, dh = qkv.shape[1], qkv.shape[2]
    nb = t // BLK
    rb = BAND_BLOCKS_PER_STEP if g == 1 else BAND_BLOCKS_PER_STEP_GROUPED
    rows = rb * BLK
    scale = dh ** -0.5
    has_sink = sink_rows is not None

    def kern(*refs):
        if has_sink:
            q_ref, k_ref, v_ref, s_ref, num_ref, m_ref, *l_ref = refs
            sink = s_ref[...]
        else:
            q_ref, k_ref, v_ref, num_ref, m_ref, *l_ref = refs
        for r in range(rb):
            b = pl.program_id(1) * rb + r
            cur = pl.multiple_of(b * BLK, BLK)
            prev = pl.multiple_of(jnp.maximum(b - 1, 0) * BLK, BLK)
            here = slice(r * BLK, (r + 1) * BLK)
            q = q_ref[:, here, :].reshape(g * BLK, dh)
            kk = jnp.concatenate([k_ref[pl.ds(prev, BLK), :], k_ref[pl.ds(cur, BLK), :]], axis=0)
            vv = jnp.concatenate([v_ref[pl.ds(prev, BLK), :], v_ref[pl.ds(cur, BLK), :]], axis=0)
            s = lax.dot_general(q, kk, NT, preferred_element_type=F32) * scale
            s = jnp.where(_band_mask(g, (b % seg) != 0, max_dist), s, -jnp.inf)
            m = jnp.max(s, axis=-1, keepdims=True)
            if has_sink:
                m = jnp.maximum(m, sink)
            p = jnp.exp(s - m)
            l = jnp.sum(p, axis=-1, keepdims=True)
            if has_sink:
                l = l + jnp.exp(sink - m)
            num = jnp.dot(p.astype(BF16), vv, preferred_element_type=F32)
            if normalise:
                num_ref[:, here, :] = (num * (1.0 / l)).reshape(g, BLK, dh)
                m_ref[:, here, :] = (m + jnp.log(l)).reshape(g, BLK, 1)
            else:
                num_ref[:, here, :] = num.reshape(g, BLK, dh)
                m_ref[:, here, :] = m.reshape(g, BLK, 1)
                l_ref[0][:, here, :] = l.reshape(g, BLK, 1)

    in_specs = [pl.BlockSpec((g, rows, dh), lambda h, b: (q0 // g + h, b, 0)),
                pl.BlockSpec((None, t, dh), lambda h, b: (k0 + h, 0, 0)),
                pl.BlockSpec((None, t, dh), lambda h, b: (v0 + h, 0, 0))]
    args = [qkv, qkv, qkv]
    if has_sink:
        in_specs.append(pl.BlockSpec((None, g * BLK, 1), lambda h, b: (h, 0, 0)))
        args.append(sink_rows)
    hq = hk * g
    n_col = 1 if normalise else 2
    return pl.pallas_call(
        kern, name=name, grid=(hk, nb // rb), in_specs=in_specs,
        out_specs=[pl.BlockSpec((g, rows, dh), lambda h, b: (h, b, 0))]
        + [pl.BlockSpec((g, rows, 1), lambda h, b: (h, b, 0))] * n_col,
        out_shape=[_sds((hq, t, dh), F32)] + [_sds((hq, t, 1), F32)] * n_col,
        compiler_params=_params("parallel", "parallel"),
    )(*args)


def band_bwd(qkv, q0, k0, v0, do, lse, delta, hk, g, seg, max_dist, name, sink_rows=None,
             delta_from_o=False):
    t, dh = qkv.shape[1], qkv.shape[2]
    nb = t // BLK
    rb = BAND_BLOCKS_PER_STEP if g == 1 else BAND_BLOCKS_PER_STEP_GROUPED
    scale = dh ** -0.5
    has_sink = sink_rows is not None

    def kern(*refs):
        if has_sink:
            (q_ref, k_ref, v_ref, do_ref, lse_ref, dl_ref, s_ref,
             dq_ref, dk_ref, dv_ref, ds_ref, sacc) = refs
            sink = s_ref[...]
        else:
            q_ref, k_ref, v_ref, do_ref, lse_ref, dl_ref, dq_ref, dk_ref, dv_ref = refs
        step = pl.program_id(1)

        @pl.when(step == 0)
        def _():
            dk_ref[...] = jnp.zeros_like(dk_ref)
            dv_ref[...] = jnp.zeros_like(dv_ref)
            if has_sink:
                sacc[...] = jnp.zeros_like(sacc)

        for r in range(rb):
            b = step * rb + r
            cur = pl.multiple_of(b * BLK, BLK)
            prev = pl.multiple_of(jnp.maximum(b - 1, 0) * BLK, BLK)
            here = slice(r * BLK, (r + 1) * BLK)
            q = q_ref[:, here, :].reshape(g * BLK, dh)
            dout = do_ref[:, here, :].reshape(g * BLK, dh)
            lse_b = lse_ref[:, here, :].reshape(g * BLK, 1)
            if delta_from_o:
                dl_b = jnp.sum(dl_ref[:, here, :].reshape(g * BLK, dh) * dout, axis=-1,
                               keepdims=True)
                dout = dout.astype(BF16)
            else:
                dl_b = dl_ref[:, here, :].reshape(g * BLK, 1)
            kk = jnp.concatenate([k_ref[pl.ds(prev, BLK), :], k_ref[pl.ds(cur, BLK), :]], axis=0)
            vv = jnp.concatenate([v_ref[pl.ds(prev, BLK), :], v_ref[pl.ds(cur, BLK), :]], axis=0)
            s = lax.dot_general(q, kk, NT, preferred_element_type=F32) * scale
            s = jnp.where(_band_mask(g, (b % seg) != 0, max_dist), s, -jnp.inf)
            p = jnp.exp(s - lse_b)
            dp = lax.dot_general(dout, vv, NT, preferred_element_type=F32)
            ds = (p * (dp - dl_b) * scale).astype(BF16)
            dq = jnp.dot(ds, kk, preferred_element_type=F32)
            dq_ref[:, here, :] = dq.reshape(g, BLK, dh)
            dkk = lax.dot_general(ds, q, TN, preferred_element_type=F32)
            dvv = lax.dot_general(p.astype(BF16), dout, TN, preferred_element_type=F32)
            dk_ref[pl.ds(prev, BLK), :] += dkk[:BLK]
            dk_ref[pl.ds(cur, BLK), :] += dkk[BLK:]
            dv_ref[pl.ds(prev, BLK), :] += dvv[:BLK]
            dv_ref[pl.ds(cur, BLK), :] += dvv[BLK:]
            if has_sink:
                sacc[...] += -jnp.exp(sink - lse_b) * dl_b

        if has_sink:
            @pl.when(step == nb // rb - 1)
            def _():
                for gi in range(g):
                    ds_ref[gi:gi + 1, :] = jnp.sum(sacc[gi * BLK:(gi + 1) * BLK, :], axis=0,
                                                   keepdims=True)

    rows = rb * BLK
    in_specs = [pl.BlockSpec((g, rows, dh), lambda h, b: (q0 // g + h, b, 0)),
                pl.BlockSpec((None, t, dh), lambda h, b: (k0 + h, 0, 0)),
                pl.BlockSpec((None, t, dh), lambda h, b: (v0 + h, 0, 0)),
                pl.BlockSpec((g, rows, dh), lambda h, b: (h, b, 0)),
                pl.BlockSpec((g, rows, 1), lambda h, b: (h, b, 0)),
                pl.BlockSpec((g, rows, dh if delta_from_o else 1), lambda h, b: (h, b, 0))]
    args = [qkv, qkv, qkv, do, lse, delta]
    hq = hk * g
    out_specs = [pl.BlockSpec((g, rows, dh), lambda h, b: (h, b, 0)),
                 pl.BlockSpec((None, t, dh), lambda h, b: (h, 0, 0)),
                 pl.BlockSpec((None, t, dh), lambda h, b: (h, 0, 0))]
    out_shape = [_sds((hq, t, dh), F32), _sds((hk, t, dh), F32), _sds((hk, t, dh), F32)]
    scratch = []
    if has_sink:
        in_specs.append(pl.BlockSpec((None, g * BLK, 1), lambda h, b: (h, 0, 0)))
        args.append(sink_rows)
        out_specs.append(pl.BlockSpec((None, g, 1), lambda h, b: (h, 0, 0)))
        out_shape.append(_sds((hk, g, 1), F32))
        scratch.append(pltpu.VMEM((g * BLK, 1), F32))
    return pl.pallas_call(
        kern, name=name, grid=(hk, nb // rb), in_specs=in_specs, out_specs=out_specs,
        out_shape=out_shape,
        scratch_shapes=scratch, compiler_params=_params("parallel", "arbitrary"),
    )(*args)


def merge_branches(nums, ms, ls, name):
    h, t, dh = nums[0].shape
    nbr = len(nums)

    def kern(*refs):
        num_refs, m_refs, l_refs = refs[:nbr], refs[nbr:2 * nbr], refs[2 * nbr:3 * nbr]
        o_ref, lse_ref = refs[3 * nbr], refs[3 * nbr + 1]
        mall = m_refs[0][...]
        for i in range(1, nbr):
            mall = jnp.maximum(mall, m_refs[i][...])
        num = jnp.zeros((t, dh), F32)
        den = jnp.zeros((t, 1), F32)
        for i in range(nbr):
            w = jnp.exp(m_refs[i][...] - mall)
            num = num + w * num_refs[i][...]
            den = den + w * l_refs[i][...]
        o_ref[...] = num / den
        lse_ref[...] = mall + jnp.log(den)

    big = pl.BlockSpec((None, t, dh), lambda i: (i, 0, 0))
    col = pl.BlockSpec((None, t, 1), lambda i: (i, 0, 0))
    return pl.pallas_call(
        kern, name=name, grid=(h,), in_specs=[big] * nbr + [col] * (2 * nbr),
        out_specs=[big, col], out_shape=[_sds((h, t, dh), F32), _sds((h, t, 1), F32)],
        compiler_params=_params("parallel"),
    )(*nums, *ms, *ls)


def normalise_heads(num, m, l, name):
    h, t, dh = num.shape

    def kern(num_ref, m_ref, l_ref, o_ref, lse_ref):
        lv = l_ref[...]
        o_ref[...] = num_ref[...] / lv
        lse_ref[...] = m_ref[...] + jnp.log(lv)

    big = pl.BlockSpec((None, t, dh), lambda i: (i, 0, 0))
    col = pl.BlockSpec((None, t, 1), lambda i: (i, 0, 0))
    return pl.pallas_call(
        kern, name=name, grid=(h,), in_specs=[big, col, col], out_specs=[big, col],
        out_shape=[_sds((h, t, dh), F32), _sds((h, t, 1), F32)],
        compiler_params=_params("parallel"),
    )(num, m, l)


def head_delta(o, do, name):
    h, t, dh = o.shape

    def kern(o_ref, do_ref, d_ref):
        d_ref[...] = jnp.sum(o_ref[...] * do_ref[...], axis=-1, keepdims=True)

    big = pl.BlockSpec((None, t, dh), lambda i: (i, 0, 0))
    return pl.pallas_call(
        kern, name=name, grid=(h,), in_specs=[big, big],
        out_specs=pl.BlockSpec((None, t, 1), lambda i: (i, 0, 0)),
        out_shape=_sds((h, t, 1), F32), compiler_params=_params("parallel"),
    )(o, do)


def _dil_rb(nbl):
    return min(BAND_BLOCKS_PER_STEP, nbl)


def dilated_fwd(qkv, na, dil, max_dist, name):
    t, w3 = qkv.shape
    dh = A_DIM
    seq = t // dil
    nbl = seq // BLK
    rb = _dil_rb(nbl)
    rows = rb * BLK
    cb = w3 // dh
    scale = dh ** -0.5
    view = qkv.reshape(seq, dil * w3)

    def kern(q_ref, k_ref, v_ref, num_ref, m_ref, l_ref):
        for r in range(rb):
            b = pl.program_id(1) * rb + r
            cur = pl.multiple_of(b * BLK, BLK)
            prev = pl.multiple_of(jnp.maximum(b - 1, 0) * BLK, BLK)
            here = slice(r * BLK, (r + 1) * BLK)
            kk = jnp.concatenate([k_ref[pl.ds(prev, BLK), :], k_ref[pl.ds(cur, BLK), :]], axis=0)
            vv = jnp.concatenate([v_ref[pl.ds(prev, BLK), :], v_ref[pl.ds(cur, BLK), :]], axis=0)
            s = lax.dot_general(q_ref[here, :], kk, NT, preferred_element_type=F32) * scale
            s = jnp.where(_band_mask(1, b != 0, max_dist), s, -jnp.inf)
            m = jnp.max(s, axis=-1, keepdims=True)
            p = jnp.exp(s - m)
            l = jnp.sum(p, axis=-1, keepdims=True)
            num_ref[here, :] = jnp.dot(p.astype(BF16), vv, preferred_element_type=F32)
            m_ref[here, :] = jnp.broadcast_to(m, (BLK, dh))
            l_ref[here, :] = jnp.broadcast_to(l, (BLK, dh))

    def col(off):
        return lambda p, b: (0, (p // na) * cb + off * na + p % na)

    out_spec = pl.BlockSpec((rows, dh), lambda p, b: (b, p))
    out = _sds((seq, dil * na * dh), F32)
    outs = pl.pallas_call(
        kern, name=name, grid=(dil * na, nbl // rb),
        in_specs=[pl.BlockSpec((rows, dh), lambda p, b: (b, (p // na) * cb + p % na)),
                  pl.BlockSpec((seq, dh), col(1)), pl.BlockSpec((seq, dh), col(2))],
        out_specs=[out_spec] * 3, out_shape=[out] * 3,
        compiler_params=_params("parallel", "parallel"),
    )(view, view, view)
    return [o.reshape(t, na * dh) for o in outs]


def dilated_merge(nums, ms, ls, name):
    t, w = nums[0].shape
    nbr = len(nums)

    def body(i, ri, fi, ro, ao):
        mall = ri[nbr][...]
        for j in range(1, nbr):
            mall = jnp.maximum(mall, ri[nbr + j][...])
        num = jnp.zeros(mall.shape, F32)
        den = jnp.zeros(mall.shape, F32)
        for j in range(nbr):
            wgt = jnp.exp(ri[nbr + j][...] - mall)
            num = num + wgt * ri[j][...]
            den = den + wgt * ri[2 * nbr + j][...]
        o = num / den
        ro[0][...] = o
        ro[1][...] = o.astype(BF16)
        ro[2][...] = mall + jnp.log(den)

    return _rows_call(name, body, list(nums) + list(ms) + list(ls), [],
                      [_sds((t, w), F32), _sds((t, w), BF16), _sds((t, w), F32)], [], 256)


def dilated_delta(o, dmixed, name):
    t, w = o.shape

    def body(i, ri, fi, ro, ao):
        for j in range(w // A_DIM):
            cols = slice(j * A_DIM, (j + 1) * A_DIM)
            d = jnp.sum(ri[0][:, cols] * ri[1][:, cols], axis=-1, keepdims=True)
            ro[0][:, cols] = jnp.broadcast_to(d, (d.shape[0], A_DIM))

    return _rows_call(name, body, [o, dmixed], [], [_sds((t, w), F32)], [], 256)[0]


def dilated_bwd(qkv, dmixed, lse, delta, na, dil, max_dist, name):
    t, w3 = qkv.shape
    dh = A_DIM
    seq = t // dil
    nbl = seq // BLK
    rb = _dil_rb(nbl)
    rows = rb * BLK
    cb = w3 // dh
    db = dmixed.shape[1] // dh
    scale = dh ** -0.5
    view = qkv.reshape(seq, dil * w3)
    do_view = dmixed.reshape(seq, dil * dmixed.shape[1])
    lse_view = lse.reshape(seq, dil * na * dh)
    delta_view = delta.reshape(seq, dil * na * dh)

    def kern(q_ref, k_ref, v_ref, do_ref, lse_ref, dl_ref, dq_ref, dk_ref, dv_ref):
        step = pl.program_id(1)

        @pl.when(step == 0)
        def _():
            dk_ref[...] = jnp.zeros_like(dk_ref)
            dv_ref[...] = jnp.zeros_like(dv_ref)

        for r in range(rb):
            b = step * rb + r
            cur = pl.multiple_of(b * BLK, BLK)
            prev = pl.multiple_of(jnp.maximum(b - 1, 0) * BLK, BLK)
            here = slice(r * BLK, (r + 1) * BLK)
            q = q_ref[here, :]
            dout = do_ref[here, :].astype(BF16)
            kk = jnp.concatenate([k_ref[pl.ds(prev, BLK), :], k_ref[pl.ds(cur, BLK), :]], axis=0)
            vv = jnp.concatenate([v_ref[pl.ds(prev, BLK), :], v_ref[pl.ds(cur, BLK), :]], axis=0)
            s = lax.dot_general(q, kk, NT, preferred_element_type=F32) * scale
            s = jnp.where(_band_mask(1, b != 0, max_dist), s, -jnp.inf)
            p = jnp.exp(s - lse_ref[here, 0:1])
            dp = lax.dot_general(dout, vv, NT, preferred_element_type=F32)
            ds = (p * (dp - dl_ref[here, 0:1]) * scale).astype(BF16)
            dq_ref[here, :] = jnp.dot(ds, kk, preferred_element_type=F32)
            dkk = lax.dot_general(ds, q, TN, preferred_element_type=F32)
            dvv = lax.dot_general(p.astype(BF16), dout, TN, preferred_element_type=F32)
            dk_ref[pl.ds(prev, BLK), :] += dkk[:BLK]
            dk_ref[pl.ds(cur, BLK), :] += dkk[BLK:]
            dv_ref[pl.ds(prev, BLK), :] += dvv[:BLK]
            dv_ref[pl.ds(cur, BLK), :] += dvv[BLK:]

    def col(off):
        return lambda p, b: (0, (p // na) * cb + off * na + p % na)

    blk = pl.BlockSpec((rows, dh), lambda p, b: (b, p))
    whole = pl.BlockSpec((seq, dh), lambda p, b: (0, p))
    out = _sds((seq, dil * na * dh), F32)
    outs = pl.pallas_call(
        kern, name=name, grid=(dil * na, nbl // rb),
        in_specs=[pl.BlockSpec((rows, dh), lambda p, b: (b, (p // na) * cb + p % na)),
                  pl.BlockSpec((seq, dh), col(1)), pl.BlockSpec((seq, dh), col(2)),
                  pl.BlockSpec((rows, dh), lambda p, b: (b, (p // na) * db + p % na)), blk, blk],
        out_specs=[blk, whole, whole], out_shape=[out] * 3,
        compiler_params=_params("parallel", "arbitrary"),
    )(view, view, view, do_view, lse_view, delta_view)
    return [o.reshape(t, na * dh) for o in outs]


def rope_bwd_sum(dqs, dks, dvs, tabs, name):
    c, sp, sm, half = tabs
    t, w = dqs[0].shape
    nbr = len(dqs)
    tm = 256
    n_slab = w // LANES

    def kern(*refs):
        groups = [refs[:nbr], refs[nbr:2 * nbr], refs[2 * nbr:3 * nbr]]
        c_ref, sp_ref, sm_ref, o_ref = refs[3 * nbr:]
        cv, spv, smv = c_ref[...], sp_ref[...], sm_ref[...]
        for gi, group in enumerate(groups):
            for j in range(n_slab):
                cols = slice(j * LANES, (j + 1) * LANES)
                xs = group[0][:, cols]
                for ref in group[1:]:
                    xs = xs + ref[:, cols]
                if gi < 2:
                    xs = (xs * cv + pltpu.roll(xs * spv, LANES - half, 1)
                          + pltpu.roll(xs * smv, half, 1))
                o_ref[:, gi * w + j * LANES:gi * w + (j + 1) * LANES] = xs.astype(BF16)

    big = pl.BlockSpec((tm, w), lambda i: (i, 0))
    tab = pl.BlockSpec((tm, LANES), lambda i: (i, 0))
    return pl.pallas_call(
        kern, name=name, grid=(t // tm,), in_specs=[big] * (3 * nbr) + [tab] * 3,
        out_specs=pl.BlockSpec((tm, 3 * w), lambda i: (i, 0)), out_shape=_sds((t, 3 * w), BF16),
        compiler_params=_params("parallel"),
    )(*dqs, *dks, *dvs, c, sp, sm)


def _cumsum_rows(x, n, reverse=False):
    rows = lax.broadcasted_iota(jnp.int32, x.shape, 0)
    shift = 1
    while shift < n:
        if reverse:
            x = x + jnp.where(rows < n - shift, pltpu.roll(x, n - shift, 0), 0.0)
        else:
            x = x + jnp.where(rows >= shift, pltpu.roll(x, shift, 0), 0.0)
        shift *= 2
    return x


def _hgrn_gates(f, lb):
    sig = _sigmoid(f)
    gate = lb + (1.0 - lb) * sig
    return sig, gate


B_SUB = 16


def _dot3(a, b, dims):
    ah, bh = a.astype(BF16), b.astype(BF16)
    al = (a - ah.astype(F32)).astype(BF16)
    bl = (b - bh.astype(F32)).astype(BF16)
    dot = functools.partial(lax.dot_general, dimension_numbers=dims, preferred_element_type=F32)
    return dot(ah, bh) + dot(al, bh) + dot(ah, bl)


def _sub_scales(b, i):
    r0 = i * B_SUB
    beta = b[r0 - 1:r0, :]
    return jnp.exp(b[r0:r0 + B_SUB, :] - beta), jnp.exp(jnp.minimum(beta - b, 0.0))


def _hgrn_intra_attn(qq, kk, b):
    c = qq.shape[0]
    lane = lax.broadcasted_iota(jnp.int32, (B_SUB, c), 1)
    trow = lax.broadcasted_iota(jnp.int32, (B_SUB, B_DIM), 0)
    blocks = []
    for i in range(c // B_SUB):
        r0 = i * B_SUB
        qi, bi = qq[r0:r0 + B_SUB, :], b[r0:r0 + B_SUB, :]
        if i == 0:
            a_i = jnp.zeros((B_SUB, c), F32)
        else:
            eq, ek = _sub_scales(b, i)
            a_i = jnp.where(lane < r0, _dot3(qi * eq, kk * ek, NT), 0.0)
        for sl in range(B_SUB):
            s = r0 + sl
            e = jnp.exp(jnp.where(trow >= sl, bi - b[s:s + 1, :], -jnp.inf))
            col = jnp.sum(qi * kk[s:s + 1, :] * e, axis=1, keepdims=True)
            a_i = jnp.where(lane == s, col, a_i)
        blocks.append(a_i)
    return jnp.concatenate(blocks, axis=0)


def hgrn_fwd(proj, col0, nh, lb, gn, name):
    t = proj.shape[0]
    c = B_CHUNK
    nc = t // c
    scale = B_DIM ** -0.5

    def kern(q_ref, f_ref, i_ref, g_ref, lb_ref, gn_ref, out_ref, opre_ref, st_ref, a_ref, state):
        lbv = lb_ref[...]
        gnv = gn_ref[...]
        state[...] = jnp.zeros_like(state)

        def chunk(ci, carry):
            rows = pl.ds(pl.multiple_of(ci * c, c), c)
            _, gate = _hgrn_gates(f_ref[rows, :], lbv)
            kk = 1.0 - gate
            qb = q_ref[rows, :]
            qq = qb * _sigmoid(qb) * scale
            v = i_ref[rows, :]
            b = _cumsum_rows(jnp.log(gate), c)
            st = state[...]
            st_ref[ci] = st
            o_inter = lax.dot_general((qq * jnp.exp(b)).astype(BF16), st.astype(BF16), NT,
                                      preferred_element_type=F32)
            amat = _hgrn_intra_attn(qq, kk, b)
            a_ref[ci] = amat
            o = jnp.dot(amat.astype(BF16), v.astype(BF16), preferred_element_type=F32) + o_inter
            opre_ref[rows, :] = o
            bl = b[c - 1:c, :]
            state[...] = st * jnp.exp(bl) + lax.dot_general(
                v.astype(BF16), (kk * jnp.exp(bl - b)).astype(BF16), TN, preferred_element_type=F32)
            r = lax.rsqrt(jnp.mean(o * o, axis=-1, keepdims=True) + NORM_EPS)
            gb = g_ref[rows, :]
            out_ref[rows, :] = (o * r * gnv * (gb * _sigmoid(gb))).astype(BF16)
            return carry

        lax.fori_loop(0, nc, chunk, 0)

    def col(off):
        return pl.BlockSpec((t, B_DIM), lambda h: (0, col0 + off * nh + h))

    return pl.pallas_call(
        kern, name=name, grid=(nh,),
        in_specs=[col(0), col(1), col(2), col(3),
                  pl.BlockSpec((None, 1, B_DIM), lambda h: (h, 0, 0)),
                  pl.BlockSpec((1, B_DIM), lambda h: (0, 0))],
        out_specs=[pl.BlockSpec((t, B_DIM), lambda h: (0, h)),
                   pl.BlockSpec((t, B_DIM), lambda h: (0, h)),
                   pl.BlockSpec((None, nc, B_DIM, B_DIM), lambda h: (h, 0, 0, 0)),
                   pl.BlockSpec((None, nc, c, c), lambda h: (h, 0, 0, 0))],
        out_shape=[_sds((t, nh * B_DIM), BF16), _sds((t, nh * B_DIM), F32),
                   _sds((nh, nc, B_DIM, B_DIM), F32), _sds((nh, nc, c, c), F32)],
        scratch_shapes=[pltpu.VMEM((B_DIM, B_DIM), F32)],
        compiler_params=_params("parallel"),
    )(proj, proj, proj, proj, lb, gn)


def hgrn_bwd(proj, col0, nh, lb, gn, opre, states, amats, dout, dcol0, name):
    t = proj.shape[0]
    c = B_CHUNK
    nc = t // c
    scale = B_DIM ** -0.5
    nsub = c // B_SUB

    def kern(q_ref, f_ref, i_ref, g_ref, lb_ref, gn_ref, opre_ref, st_ref, a_ref, dout_ref,
             dq_ref, df_ref, di_ref, dg_ref, dgn_ref, dlb_ref, dstate, dksc):
        lbv = lb_ref[...]
        gnv = gn_ref[...]
        dstate[...] = jnp.zeros_like(dstate)
        dlb_ref[...] = jnp.zeros_like(dlb_ref)

        @pl.when(pl.program_id(0) == 0)
        def _():
            dgn_ref[...] = jnp.zeros_like(dgn_ref)

        srow = lax.broadcasted_iota(jnp.int32, (c, B_DIM), 0)
        lane = lax.broadcasted_iota(jnp.int32, (B_SUB, c), 1)
        trow = lax.broadcasted_iota(jnp.int32, (B_SUB, B_DIM), 0)
        arow = lax.broadcasted_iota(jnp.int32, (c, c), 0)
        alane = lax.broadcasted_iota(jnp.int32, (c, c), 1)

        def chunk(cj, carry):
            ci = nc - 1 - cj
            rows = pl.ds(pl.multiple_of(ci * c, c), c)
            f = f_ref[rows, :]
            sig, gate = _hgrn_gates(f, lbv)
            kk = 1.0 - gate
            qb = q_ref[rows, :]
            sq = _sigmoid(qb)
            qq = qb * sq * scale
            v = i_ref[rows, :]
            b = _cumsum_rows(jnp.log(gate), c)
            st0 = st_ref[ci]
            dst = dstate[...]
            o = opre_ref[rows, :]
            gb = g_ref[rows, :]
            sg = _sigmoid(gb)
            silu_g = gb * sg
            d_out = dout_ref[rows, :]
            r = lax.rsqrt(jnp.mean(o * o, axis=-1, keepdims=True) + NORM_EPS)
            y = o * r
            dg_ref[rows, :] = (d_out * y * gnv * (sg * (1.0 + gb * (1.0 - sg)))).astype(BF16)
            dyn = d_out * silu_g
            dgn_ref[...] += jnp.sum(dyn * y, axis=0, keepdims=True)
            dy = dyn * gnv
            do = r * (dy - y * jnp.mean(dy * y, axis=-1, keepdims=True))
            eb = jnp.exp(b)
            bl = b[c - 1:c, :]
            ebl = jnp.exp(bl - b)
            ebl_last = jnp.exp(bl)
            do_b = do.astype(BF16)
            dst_b = dst.astype(BF16)
            dq_inter = jnp.dot(do_b, st0.astype(BF16), preferred_element_type=F32) * eb
            dst0 = lax.dot_general(do_b, (qq * eb).astype(BF16), TN,
                                   preferred_element_type=F32) + dst * ebl_last
            dv_inter = lax.dot_general((kk * ebl).astype(BF16), dst_b, NT, preferred_element_type=F32)
            dk_inter = jnp.dot(v.astype(BF16), dst_b, preferred_element_type=F32) * ebl
            amat = a_ref[ci]
            v_b = v.astype(BF16)
            d_a = lax.dot_general(do_b, v_b, NT, preferred_element_type=F32)
            d_a = jnp.where(arow >= alane, d_a, 0.0)
            dv_intra = lax.dot_general(amat.astype(BF16), do_b, TN, preferred_element_type=F32)
            dk_pairs = jnp.zeros((c, B_DIM), F32)
            dq_blocks = []
            for i in range(nsub):
                r0 = i * B_SUB
                qi, bi = qq[r0:r0 + B_SUB, :], b[r0:r0 + B_SUB, :]
                da_i = d_a[r0:r0 + B_SUB, :]
                if i == 0:
                    dq_i = jnp.zeros((B_SUB, B_DIM), F32)
                else:
                    eq, ek = _sub_scales(b, i)
                    da_m = jnp.where(lane < r0, da_i, 0.0)
                    dq_i = _dot3(da_m, kk * ek, NN) * eq
                    dk_pairs = dk_pairs + _dot3(da_m, qi * eq, TN) * ek
                for sl in range(B_SUB):
                    s = r0 + sl
                    e = jnp.exp(jnp.where(trow >= sl, bi - b[s:s + 1, :], -jnp.inf))
                    dacol = jnp.sum(jnp.where(lane == s, da_i, 0.0), axis=1, keepdims=True)
                    w = dacol * e
                    dq_i = dq_i + w * kk[s:s + 1, :]
                    dksc[s:s + 1, :] = jnp.sum(w * qi, axis=0, keepdims=True)
                dq_blocks.append(dq_i)
            dq = jnp.concatenate(dq_blocks, axis=0) + dq_inter
            dk = dk_pairs + dksc[...] + dk_inter
            dv = dv_intra + dv_inter
            db = qq * dq - kk * dk
            extra = (jnp.sum(kk * dk_inter, axis=0, keepdims=True)
                     + ebl_last * jnp.sum(st0 * dst, axis=0, keepdims=True))
            db = db + jnp.where(srow == c - 1, extra, 0.0)
            dlog = _cumsum_rows(db, c, reverse=True)
            dgate = dlog / gate - dk
            df_ref[rows, :] = (dgate * (1.0 - lbv) * sig * (1.0 - sig)).astype(BF16)
            dlb_ref[...] += jnp.sum(dgate * (1.0 - sig), axis=0, keepdims=True)
            dq_ref[rows, :] = (dq * scale * (sq * (1.0 + qb * (1.0 - sq)))).astype(BF16)
            di_ref[rows, :] = dv.astype(BF16)
            dstate[...] = dst0
            return carry

        lax.fori_loop(0, nc, chunk, 0)

    def col(off):
        return pl.BlockSpec((t, B_DIM), lambda h: (0, col0 + off * nh + h))

    hcol = pl.BlockSpec((t, B_DIM), lambda h: (0, h))
    vec = pl.BlockSpec((None, 1, B_DIM), lambda h: (h, 0, 0))
    wide = _sds((t, nh * B_DIM), BF16)
    return pl.pallas_call(
        kern, name=name, grid=(nh,),
        in_specs=[col(0), col(1), col(2), col(3), vec,
                  pl.BlockSpec((1, B_DIM), lambda h: (0, 0)), hcol,
                  pl.BlockSpec((None, nc, B_DIM, B_DIM), lambda h: (h, 0, 0, 0)),
                  pl.BlockSpec((None, nc, c, c), lambda h: (h, 0, 0, 0)),
                  pl.BlockSpec((t, B_DIM), lambda h: (0, dcol0 + h))],
        out_specs=[hcol, hcol, hcol, hcol, pl.BlockSpec((1, B_DIM), lambda h: (0, 0)), vec],
        out_shape=[wide, wide, wide, wide, _sds((1, B_DIM), F32), _sds((nh, 1, B_DIM), F32)],
        scratch_shapes=[pltpu.VMEM((B_DIM, B_DIM), F32), pltpu.VMEM((c, B_DIM), F32)],
        compiler_params=_params("arbitrary"),
    )(proj, proj, proj, proj, lb, gn, opre, states, amats, dout)


def lower_bounds_fwd(raw, name):
    n, w = raw.shape

    def kern(raw_ref, lb_ref, soft_ref):
        r = raw_ref[...]
        mx = r[0:1]
        for i in range(1, n):
            mx = jnp.maximum(mx, r[i:i + 1])
        e = jnp.exp(r - mx)
        den = e[0:1]
        for i in range(1, n):
            den = den + e[i:i + 1]
        soft = e / den
        soft_ref[...] = soft
        run = soft[0:1]
        lb_ref[0:1, :] = run - soft[0:1]
        for i in range(1, n):
            run = run + soft[i:i + 1]
            lb_ref[i:i + 1, :] = run - soft[0:1]

    return pl.pallas_call(kern, name=name, out_shape=[_sds((n, w), F32), _sds((n, w), F32)])(raw)


def lower_bounds_bwd(soft, dlb, name):
    n, w = soft.shape

    def kern(soft_ref, dlb_ref, out_ref):
        s = soft_ref[...]
        d = dlb_ref[...]
        total = d[0:1]
        for i in range(1, n):
            total = total + d[i:i + 1]
        us = []
        tail = total
        for i in range(n):
            us.append(tail - total if i == 0 else tail)
            tail = tail - d[i:i + 1]
        dot = s[0:1] * us[0]
        for i in range(1, n):
            dot = dot + s[i:i + 1] * us[i]
        for i in range(n):
            out_ref[i:i + 1, :] = s[i:i + 1] * (us[i] - dot)

    return pl.pallas_call(kern, name=name, out_shape=_sds((n, w), F32))(soft, dlb)


def _row_tile(kdim, n):
    tk = 512
    while tk > 8 and tk * n > 256 * 1024:
        tk //= 2
    return min(kdim, tk)


def _adam_update(w, g, m, v):
    m2 = ADAM_B1 * m + (1.0 - ADAM_B1) * g
    v2 = ADAM_B2 * v + (1.0 - ADAM_B2) * (g * g)
    m_hat = m2 / (1.0 - ADAM_B1 ** ADAM_STEP)
    v_hat = v2 / (1.0 - ADAM_B2 ** ADAM_STEP)
    delta = -ADAM_LR * (m_hat / (jnp.sqrt(v_hat) + ADAM_EPS) + ADAM_WD * w)
    return delta, m2, v2


def adamw_small(w, g, m, v, name):
    def kern(w_ref, g_ref, m_ref, v_ref, d_ref, m2_ref, v2_ref):
        d, m2, v2 = _adam_update(w_ref[...], g_ref[...], m_ref[...], v_ref[...])
        d_ref[...] = d
        m2_ref[...] = m2
        v2_ref[...] = v2

    return pl.pallas_call(kern, name=name, out_shape=[_sds(w.shape, F32)] * 3)(w, g, m, v)


def adamw_big(parts, w, m, v, name):
    nl, kdim, n = w.shape
    tk = _row_tile(kdim, n)

    def kern(p_ref, w_ref, m_ref, v_ref, g_ref, d_ref, m2_ref, v2_ref):
        g = p_ref[0].astype(F32)
        for q in range(1, 4):
            g = g + p_ref[q].astype(F32)
        d, m2, v2 = _adam_update(w_ref[...], g, m_ref[...], v_ref[...])
        g_ref[...] = g
        d_ref[...] = d
        m2_ref[...] = m2
        v2_ref[...] = v2

    blk = pl.BlockSpec((None, tk, n), lambda l, i: (l, i, 0))
    return pl.pallas_call(
        kern, name=name, grid=(nl, kdim // tk),
        in_specs=[pl.BlockSpec((None, 4, tk, n), lambda l, i: (l, 0, i, 0)), blk, blk, blk],
        out_specs=[blk] * 4, out_shape=[_sds(w.shape, F32)] * 4,
        compiler_params=_params("parallel", "parallel"),
    )(parts, w, m, v)


def cast_bf16(w, name):
    nl, kdim, n = w.shape
    tk = min(kdim, 4 * _row_tile(kdim, n))

    def kern(w_ref, o_ref):
        o_ref[...] = w_ref[...].astype(BF16)

    blk = pl.BlockSpec((None, tk, n), lambda l, i: (l, i, 0))
    return pl.pallas_call(
        kern, name=name, grid=(nl, kdim // tk), in_specs=[blk], out_specs=blk,
        out_shape=_sds(w.shape, BF16), compiler_params=_params("parallel", "parallel"),
    )(w)


def pair_add(dw, r1, core, name):
    kdim, n = dw.shape[1], dw.shape[2]
    tk = min(kdim, 4 * _row_tile(kdim, n))

    def kern(c_ref, a_ref, b_ref, o_ref):
        o_ref[...] = (a_ref[...].astype(F32) + b_ref[...].astype(F32)).astype(BF16)

    grid_spec = pltpu.PrefetchScalarGridSpec(
        num_scalar_prefetch=1, grid=(4, kdim // tk),
        in_specs=[pl.BlockSpec((None, tk, n), lambda p, i, c: (2 * p + c[0], i, 0)),
                  pl.BlockSpec((None, tk, n), lambda p, i, c: (p, i, 0))],
        out_specs=pl.BlockSpec((None, tk, n), lambda p, i, c: (p, i, 0)))
    return pl.pallas_call(
        kern, name=name, grid_spec=grid_spec, out_shape=_sds((4, kdim, n), BF16),
        compiler_params=_params("parallel", "parallel"),
    )(core, dw, r1)


ANY = pl.BlockSpec(memory_space=pl.ANY)


def _place():
    x, y, c = lax.axis_index("x"), lax.axis_index("y"), lax.axis_index("c")
    chips = [(1 - x, y), (x, 1 - y), (1 - x, 1 - y)]
    return x, y, c, chips


def all_gather(shards, name):
    n = len(shards)

    def kern(*refs):
        ins, outs = refs[:n], refs[n:2 * n]
        send_sems, recv_sems, local_sems = refs[2 * n:]
        x, y, c, chips = _place()
        me, sib = (x, y, c), (x, y, 1 - c)

        def copy(t, k, block, to, src=None):
            px, py, pc = block
            dst = outs[t].at[4 * px + 2 * py + pc]
            return pltpu.make_async_remote_copy(
                src_ref=dst if src is None else src, dst_ref=dst,
                send_sem=send_sems.at[7 * t + k], recv_sem=recv_sems.at[7 * t + k],
                device_id=to, device_id_type=MESH)

        mine = [pltpu.make_async_copy(ins[t], outs[t].at[4 * x + 2 * y + c], local_sems.at[t])
                for t in range(n)]
        for cp in mine:
            cp.start()
        first = []
        for t in range(n):
            first.append(copy(t, 0, me, sib, src=ins[t]))
            first += [copy(t, 1 + j, me, (*chip, c), src=ins[t]) for j, chip in enumerate(chips)]
        for cp in first:
            cp.start()
        passed = []
        for t in range(n):
            for j, chip in enumerate(chips):
                copy(t, 1 + j, (*chip, c), me).wait_recv()
                fwd = copy(t, 4 + j, (*chip, c), sib)
                fwd.start()
                passed.append(fwd)
        for t in range(n):
            copy(t, 0, sib, me).wait_recv()
            for j, chip in enumerate(chips):
                copy(t, 4 + j, (*chip, 1 - c), me).wait_recv()
        for cp in first + passed:
            cp.wait_send()
        for cp in mine:
            cp.wait()

    return pl.pallas_call(
        kern, name=name, in_specs=[ANY] * n, out_specs=[ANY] * n,
        out_shape=[_sds((N_DEV,) + s.shape, s.dtype) for s in shards],
        scratch_shapes=[pltpu.SemaphoreType.DMA((7 * n,)), pltpu.SemaphoreType.DMA((7 * n,)),
                        pltpu.SemaphoreType.DMA((n,))],
    )(*shards)


HBM = pl.BlockSpec(memory_space=pltpu.HBM)
SEM = pl.BlockSpec(memory_space=pltpu.SEMAPHORE)
DATAFLOW = pltpu.SideEffectType.DATAFLOW_SIDE_EFFECTING


def _first_level_targets():
    x, y, c, chips = _place()
    return 4 * x + 2 * y + c, [(x, y, 1 - c)] + [(*chip, c) for chip in chips]


def gather_start(shards, after, name):
    n = len(shards)
    lands = [lax.empty((N_DEV,) + s.shape, s.dtype) for s in shards]

    def kern(*refs):
        ins, lnd = refs[:n], refs[n:2 * n]
        send_sems, recv_sems, local_sems = refs[2 * n + len(after):2 * n + len(after) + 3]
        token = refs[-1]
        me, targets = _first_level_targets()
        for t in range(n):
            pltpu.make_async_copy(ins[t], lnd[t].at[me], local_sems.at[t]).start()
            for k, to in enumerate(targets):
                pltpu.make_async_remote_copy(
                    src_ref=ins[t], dst_ref=lnd[t].at[me], send_sem=send_sems.at[4 * t + k],
                    recv_sem=recv_sems.at[4 * t + k], device_id=to, device_id_type=MESH).start()
        token[...] = jnp.zeros_like(token)

    args = [pltpu.with_memory_space_constraint(a, pltpu.HBM) for a in list(shards) + lands]
    return pl.pallas_call(
        kern, name=name,
        out_shape=(pltpu.SemaphoreType.DMA((4 * n,)), pltpu.SemaphoreType.DMA((4 * n,)),
                   pltpu.SemaphoreType.DMA((n,)),
                   *[pltpu.HBM(a.shape, a.dtype) for a in args], _sds((8, LANES), F32)),
        in_specs=[HBM] * (2 * n) + [ANY] * len(after),
        out_specs=(SEM, SEM, SEM, *[HBM] * (2 * n), pl.BlockSpec(memory_space=pltpu.VMEM)),
        input_output_aliases={i: 3 + i for i in range(2 * n)},
        compiler_params=pltpu.CompilerParams(has_side_effects=DATAFLOW),
    )(*args, *after)


def gather_wait(send_sems, recv_sems, local_sems, shards, lands, after, name):
    n = len(shards)

    def kern(*refs):
        ins, lnd = refs[:n], refs[n:2 * n]
        send_sems, recv_sems, local_sems = refs[2 * n:2 * n + 3]
        me, targets = _first_level_targets()
        for t in range(n):
            pltpu.make_async_copy(ins[t], lnd[t].at[me], local_sems.at[t]).wait()
            for k, to in enumerate(targets):
                cp = pltpu.make_async_remote_copy(
                    src_ref=ins[t], dst_ref=lnd[t].at[me], send_sem=send_sems.at[4 * t + k],
                    recv_sem=recv_sems.at[4 * t + k], device_id=to, device_id_type=MESH)
                cp.wait_send()
                cp.wait_recv()

    bufs = list(shards) + list(lands)
    return pl.pallas_call(
        kern, name=name, out_shape=tuple(pltpu.HBM(a.shape, a.dtype) for a in bufs),
        in_specs=[HBM] * (2 * n) + [SEM, SEM, SEM, ANY], out_specs=[HBM] * (2 * n),
        input_output_aliases={i: i for i in range(2 * n)},
        compiler_params=pltpu.CompilerParams(has_side_effects=DATAFLOW),
    )(*bufs, send_sems, recv_sems, local_sems, after)


def _forward_copies(lnd, send_sems, recv_sems):
    x, y, c, chips = _place()
    passed = []
    for t in range(len(lnd)):
        for j, (qx, qy) in enumerate(chips):
            block = lnd[t].at[4 * qx + 2 * qy + c]
            passed.append(pltpu.make_async_remote_copy(
                src_ref=block, dst_ref=block, send_sem=send_sems.at[3 * t + j],
                recv_sem=recv_sems.at[3 * t + j], device_id=(x, y, 1 - c), device_id_type=MESH))
    return passed


def forward_now(lands, name):
    n = len(lands)

    def kern(*refs):
        copies = _forward_copies(refs[n:2 * n], refs[2 * n], refs[2 * n + 1])
        for cp in copies:
            cp.start()
        for cp in copies:
            cp.wait_recv()
        for cp in copies:
            cp.wait_send()

    return pl.pallas_call(
        kern, name=name, in_specs=[ANY] * n, out_specs=[ANY] * n,
        out_shape=[_sds(a.shape, a.dtype) for a in lands],
        input_output_aliases={i: i for i in range(n)},
        scratch_shapes=[pltpu.SemaphoreType.DMA((3 * n,)), pltpu.SemaphoreType.DMA((3 * n,))],
    )(*lands)


def sibling_start(grads, name):
    n = len(grads)
    lands = [lax.empty((4,) + g.shape[1:], g.dtype) for g in grads]

    def kern(*refs):
        ins, lnd = refs[:n], refs[n:2 * n]
        send_sems, recv_sems = refs[2 * n], refs[2 * n + 1]
        x, y, c, _ = _place()
        for t in range(n):
            for p in range(4):
                pltpu.make_async_remote_copy(
                    src_ref=ins[t].at[2 * p + 1 - c], dst_ref=lnd[t].at[p],
                    send_sem=send_sems.at[4 * t + p], recv_sem=recv_sems.at[4 * t + p],
                    device_id=(x, y, 1 - c), device_id_type=MESH).start()
        refs[-1][...] = jnp.zeros_like(refs[-1])

    args = [pltpu.with_memory_space_constraint(a, pltpu.HBM) for a in list(grads) + lands]
    return pl.pallas_call(
        kern, name=name,
        out_shape=(pltpu.SemaphoreType.DMA((4 * n,)), pltpu.SemaphoreType.DMA((4 * n,)),
                   *[pltpu.HBM(a.shape, a.dtype) for a in args], _sds((8, LANES), F32)),
        in_specs=[HBM] * (2 * n),
        out_specs=(SEM, SEM, *[HBM] * (2 * n), pl.BlockSpec(memory_space=pltpu.VMEM)),
        input_output_aliases={i: 2 + i for i in range(2 * n)},
        compiler_params=pltpu.CompilerParams(has_side_effects=DATAFLOW),
    )(*args)


def sibling_wait(send_sems, recv_sems, grads, lands, after, name):
    n = len(grads)

    def kern(*refs):
        ins, lnd = refs[:n], refs[n:2 * n]
        send_sems, recv_sems = refs[2 * n], refs[2 * n + 1]
        x, y, c, _ = _place()
        for t in range(n):
            for p in range(4):
                cp = pltpu.make_async_remote_copy(
                    src_ref=ins[t].at[2 * p + 1 - c], dst_ref=lnd[t].at[p],
                    send_sem=send_sems.at[4 * t + p], recv_sem=recv_sems.at[4 * t + p],
                    device_id=(x, y, 1 - c), device_id_type=MESH)
                cp.wait_send()
                cp.wait_recv()

    bufs = list(grads) + list(lands)
    outs = pl.pallas_call(
        kern, name=name, out_shape=tuple(pltpu.HBM(a.shape, a.dtype) for a in bufs),
        in_specs=[HBM] * (2 * n) + [SEM, SEM, ANY], out_specs=[HBM] * (2 * n),
        input_output_aliases={i: i for i in range(2 * n)},
        compiler_params=pltpu.CompilerParams(has_side_effects=DATAFLOW),
    )(*bufs, send_sems, recv_sems, after)
    return outs[:n], outs[n:]


def forward_start(lands, name):
    n = len(lands)

    def kern(*refs):
        for cp in _forward_copies(refs[:n], refs[n], refs[n + 1]):
            cp.start()
        refs[-1][...] = jnp.zeros_like(refs[-1])

    return pl.pallas_call(
        kern, name=name,
        out_shape=(pltpu.SemaphoreType.DMA((3 * n,)), pltpu.SemaphoreType.DMA((3 * n,)),
                   *[pltpu.HBM(a.shape, a.dtype) for a in lands], _sds((8, LANES), F32)),
        in_specs=[HBM] * n,
        out_specs=(SEM, SEM, *[HBM] * n, pl.BlockSpec(memory_space=pltpu.VMEM)),
        input_output_aliases={i: 2 + i for i in range(n)},
        compiler_params=pltpu.CompilerParams(has_side_effects=DATAFLOW),
    )(*lands)


def forward_wait(send_sems, recv_sems, lands, after, name):
    n = len(lands)

    def kern(*refs):
        for cp in _forward_copies(refs[:n], refs[n], refs[n + 1]):
            cp.wait_send()
            cp.wait_recv()

    return pl.pallas_call(
        kern, name=name, out_shape=tuple(pltpu.HBM(a.shape, a.dtype) for a in lands),
        in_specs=[HBM] * n + [SEM, SEM, ANY], out_specs=[HBM] * n,
        input_output_aliases={i: i for i in range(n)},
        compiler_params=pltpu.CompilerParams(has_side_effects=DATAFLOW),
    )(*lands, send_sems, recv_sems, after)


def all_reduce_small(vec, name):
    r = vec.shape[0]

    def kern(v_ref, o_ref, buf, send_sems, recv_sems):
        x, y, c, _ = _place()
        me = 4 * x + 2 * y + c
        peers = [(x, y, 1 - c), (1 - x, y, c), (x, 1 - y, c), (1 - x, 1 - y, c),
                 (1 - x, y, 1 - c), (x, 1 - y, 1 - c), (1 - x, 1 - y, 1 - c)]
        buf[me] = v_ref[...]
        copies = []
        for k, peer in enumerate(peers):
            cp = pltpu.make_async_remote_copy(
                src_ref=v_ref, dst_ref=buf.at[me], send_sem=send_sems.at[k],
                recv_sem=recv_sems.at[k], device_id=peer, device_id_type=MESH)
            cp.start()
            copies.append(cp)
        for cp in copies:
            cp.wait_recv()
        for cp in copies:
            cp.wait_send()
        total = buf[0]
        for d in range(1, N_DEV):
            total = total + buf[d]
        o_ref[...] = total

    vm = pl.BlockSpec(memory_space=pltpu.VMEM)
    return pl.pallas_call(
        kern, name=name, in_specs=[vm], out_specs=vm, out_shape=_sds(vec.shape, F32),
        scratch_shapes=[pltpu.VMEM((N_DEV, r, LANES), F32), pltpu.SemaphoreType.DMA((7,)),
                        pltpu.SemaphoreType.DMA((7,))],
    )(vec)


def exchange_with_sibling(grads, name):
    n = len(grads)

    def kern(*refs):
        ins, outs = refs[:n], refs[n:2 * n]
        send_sems, recv_sems = refs[2 * n:]
        x, y, c, _ = _place()
        copies = []
        for t in range(n):
            for p in range(4):
                cp = pltpu.make_async_remote_copy(
                    src_ref=ins[t].at[2 * p + 1 - c], dst_ref=outs[t].at[p],
                    send_sem=send_sems.at[4 * t + p], recv_sem=recv_sems.at[4 * t + p],
                    device_id=(x, y, 1 - c), device_id_type=MESH)
                cp.start()
                copies.append(cp)
        for cp in copies:
            cp.wait_recv()
        for cp in copies:
            cp.wait_send()

    return pl.pallas_call(
        kern, name=name, in_specs=[ANY] * n, out_specs=[ANY] * n,
        out_shape=[_sds((4,) + g.shape[1:], g.dtype) for g in grads],
        scratch_shapes=[pltpu.SemaphoreType.DMA((4 * n,)), pltpu.SemaphoreType.DMA((4 * n,))],
    )(*grads)


def exchange_between_chips(partials, layers, kinds, name):
    n = len(partials)
    n_kind = max(kinds) + 1
    shapes = []
    for kd in range(n_kind):
        idx = [i for i in range(n) if kinds[i] == kd]
        nl = max(layers[i] for i in idx) + 1
        shapes.append(_sds((nl,) + partials[idx[0]].shape, partials[idx[0]].dtype))

    def kern(*refs):
        ins, outs = refs[:n], refs[n:n + n_kind]
        send_sems, recv_sems, local_sems = refs[n + n_kind:]
        x, y, c, chips = _place()
        mine = 2 * x + y
        local = []
        copies = []
        for t in range(n):
            dst = outs[kinds[t]].at[layers[t], mine]
            lc = pltpu.make_async_copy(ins[t].at[mine], dst, local_sems.at[t])
            lc.start()
            local.append(lc)
            for j, (qx, qy) in enumerate(chips):
                cp = pltpu.make_async_remote_copy(
                    src_ref=ins[t].at[2 * qx + qy], dst_ref=dst,
                    send_sem=send_sems.at[3 * t + j], recv_sem=recv_sems.at[3 * t + j],
                    device_id=(qx, qy, c), device_id_type=MESH)
                cp.start()
                copies.append(cp)
        for cp in copies:
            cp.wait_recv()
        for cp in copies:
            cp.wait_send()
        for lc in local:
            lc.wait()

    return pl.pallas_call(
        kern, name=name, in_specs=[ANY] * n, out_specs=[ANY] * n_kind, out_shape=shapes,
        scratch_shapes=[pltpu.SemaphoreType.DMA((3 * n,)), pltpu.SemaphoreType.DMA((3 * n,)),
                        pltpu.SemaphoreType.DMA((n,))],
    )(*partials)


def scatter_start(partials, name):
    n = len(partials)
    lands = [lax.empty(p.shape, p.dtype) for p in partials]

    def kern(*refs):
        ins, lnd = refs[:n], refs[n:2 * n]
        send_sems, recv_sems, local_sems = refs[2 * n:2 * n + 3]
        token = refs[-1]
        x, y, c, chips = _place()
        mine = 2 * x + y
        for t in range(n):
            pltpu.make_async_copy(ins[t].at[mine], lnd[t].at[mine], local_sems.at[t]).start()
            for j, (qx, qy) in enumerate(chips):
                pltpu.make_async_remote_copy(
                    src_ref=ins[t].at[2 * qx + qy], dst_ref=lnd[t].at[mine],
                    send_sem=send_sems.at[3 * t + j], recv_sem=recv_sems.at[3 * t + j],
                    device_id=(qx, qy, c), device_id_type=MESH).start()
        token[...] = jnp.zeros_like(token)

    args = [pltpu.with_memory_space_constraint(a, pltpu.HBM) for a in list(partials) + lands]
    return pl.pallas_call(
        kern, name=name,
        out_shape=(pltpu.SemaphoreType.DMA((3 * n,)), pltpu.SemaphoreType.DMA((3 * n,)),
                   pltpu.SemaphoreType.DMA((n,)),
                   *[pltpu.HBM(a.shape, a.dtype) for a in args], _sds((8, LANES), F32)),
        in_specs=[HBM] * (2 * n),
        out_specs=(SEM, SEM, SEM, *[HBM] * (2 * n), pl.BlockSpec(memory_space=pltpu.VMEM)),
        input_output_aliases={i: 3 + i for i in range(2 * n)},
        compiler_params=pltpu.CompilerParams(has_side_effects=DATAFLOW),
    )(*args)


def scatter_wait(send_sems, recv_sems, local_sems, partials, lands, after, name):
    n = len(partials)

    def kern(*refs):
        ins, lnd = refs[:n], refs[n:2 * n]
        send_sems, recv_sems, local_sems = refs[2 * n:2 * n + 3]
        x, y, c, chips = _place()
        mine = 2 * x + y
        for t in range(n):
            pltpu.make_async_copy(ins[t].at[mine], lnd[t].at[mine], local_sems.at[t]).wait()
            for j, (qx, qy) in enumerate(chips):
                cp = pltpu.make_async_remote_copy(
                    src_ref=ins[t].at[2 * qx + qy], dst_ref=lnd[t].at[mine],
                    send_sem=send_sems.at[3 * t + j], recv_sem=recv_sems.at[3 * t + j],
                    device_id=(qx, qy, c), device_id_type=MESH)
                cp.wait_send()
                cp.wait_recv()

    bufs = list(partials) + list(lands)
    outs = pl.pallas_call(
        kern, name=name, out_shape=tuple(pltpu.HBM(a.shape, a.dtype) for a in bufs),
        in_specs=[HBM] * (2 * n) + [SEM, SEM, SEM, ANY], out_specs=[HBM] * (2 * n),
        input_output_aliases={i: i for i in range(2 * n)},
        compiler_params=pltpu.CompilerParams(has_side_effects=DATAFLOW),
    )(*bufs, send_sems, recv_sems, local_sems, after)
    return outs[n:]


def adamw_layers(parts, w, m, v, name):
    nl, kdim, n = w.shape
    tk = _row_tile(kdim, n)

    def kern(*refs):
        p_refs = refs[:nl]
        w_ref, m_ref, v_ref, g_ref, d_ref, m2_ref, v2_ref = refs[nl:]
        for l in range(nl):
            @pl.when(pl.program_id(0) == l)
            def _():
                g = p_refs[l][0].astype(F32)
                for q in range(1, 4):
                    g = g + p_refs[l][q].astype(F32)
                d, m2, v2 = _adam_update(w_ref[...], g, m_ref[...], v_ref[...])
                g_ref[...] = g
                d_ref[...] = d
                m2_ref[...] = m2
                v2_ref[...] = v2

    def part_spec(l):
        return pl.BlockSpec((4, tk, n), lambda li, i: (0, jnp.where(li == l, i, 0), 0))

    blk = pl.BlockSpec((None, tk, n), lambda li, i: (li, i, 0))
    return pl.pallas_call(
        kern, name=name, grid=(nl, kdim // tk),
        in_specs=[part_spec(l) for l in range(nl)] + [blk, blk, blk],
        out_specs=[blk] * 4, out_shape=[_sds(w.shape, F32)] * 4,
        compiler_params=_params("arbitrary", "arbitrary"),
    )(*parts, w, m, v)


def _pack(arrays):
    flat = jnp.concatenate([a.reshape(-1).astype(F32) for a in arrays])
    pad = (-flat.shape[0]) % (8 * LANES)
    return jnp.pad(flat, (0, pad)).reshape(-1, LANES)


def _unpack(packed, shapes):
    flat = packed.reshape(-1)
    out, off = [], 0
    for s in shapes:
        n = math.prod(s)
        out.append(flat[off:off + n].reshape(s))
        off += n
    return out


def _to_heads(x2d, dil, n_heads, dh):
    t = x2d.shape[0]
    return x2d.reshape(t // dil, dil, n_heads, dh).transpose(2, 1, 0, 3).reshape(n_heads, t, dh)


def _from_heads(xh, dil):
    h, t, w = xh.shape
    return xh.reshape(h, dil, t // dil, w).transpose(2, 1, 0, 3).reshape(t, h * w)


def _unperm(xh, dil):
    h, t, w = xh.shape
    return xh.reshape(h, dil, t // dil, w).transpose(0, 2, 1, 3).reshape(h, t, w)


def _perm(xh, dil):
    h, t, w = xh.shape
    return xh.reshape(h, t // dil, dil, w).transpose(0, 2, 1, 3).reshape(h, t, w)


def local_step(x, target, norm_mix_g, norm_mlp_g, final_norm_g, lbs, hgrn_norm_g, sinks,
               bq_full, bo_full, weights_get, weights_mid, grads_ready):
    t, d = x.shape
    depth = norm_mix_g.shape[0]
    na = d // 2 // A_DIM
    nbh = d // 2 // B_DIM
    nq = d // C_DIM
    nkv = nq // C_GROUP
    a_w = 3 * na * A_DIM
    c_w = (nq + 2 * nkv) * C_DIM
    tabs_a = rope_tables(t, A_DIM)
    tabs_c = rope_tables(t, C_DIM)
    saved = []
    for l in range(depth):
        s = {"x_in": x}
        (win_g, wout_g), token = weights_get(l, x)
        h = rms_fwd(x, norm_mix_g[l] + token, "norm_mix_fwd")
        s["h"] = h
        if l % 2 == 0:
            e = l // 2
            proj = mm_cols_sharded(h, win_g, 0, "even_in_proj")[0]
            qkv_r = rope_call(proj, tabs_a, a_w, 2 * na, False, "rope_a")[0]
            nums, ms, ls, hms = [], [], [], []
            for window, dil in A_BRANCHES:
                hm = _to_heads(qkv_r, dil, 3 * na, A_DIM)
                num, m, lsum = band_fwd(hm, 0, na, 2 * na, na, 1, t // dil // BLK, window // dil,
                                        f"dilated_fwd_{dil}")
                hms.append(hm)
                nums.append(_unperm(num, dil))
                ms.append(_unperm(m, dil))
                ls.append(_unperm(lsum, dil))
            oa, lse = merge_branches(nums, ms, ls, "dilated_merge")
            lb_e = lbs[e].reshape(nbh, 1, B_DIM)
            gn_e = hgrn_norm_g[e].reshape(1, B_DIM)
            ob, opre, states, amats = hgrn_fwd(proj, 3 * na, nbh, lb_e, gn_e, "hgrn_fwd")
            mixed = jnp.concatenate([_from_heads(oa, 1).astype(BF16), ob], axis=1)
            x = mm_rows_sharded(mixed, wout_g, 0, "even_out_proj", [x], ["tile"], _ep_residual)
            s.update(proj=proj, hms=hms, oa=oa, lse=lse, opre=opre, states=states, amats=amats,
                     mixed=mixed, lb=lb_e, gn=gn_e)
        else:
            o = l // 2
            wq = win_g[:, 0].transpose(1, 0, 2).reshape(d, c_w)
            proj = mm_plain(h, wq, "odd_qkv_proj", [bq_full[o].reshape(1, c_w)], ["row"], _ep_bias)
            qkv_r = rope_call(proj, tabs_c, c_w, (nq + nkv) * C_DIM // LANES, False, "rope_c")[0]
            hm = _to_heads(qkv_r, 1, nq + 2 * nkv, C_DIM)
            sink_rows = jnp.repeat(sinks[o].reshape(nkv, C_GROUP), BLK, axis=1).reshape(
                nkv, C_GROUP * BLK, 1)
            o_hm, lse = band_fwd(hm, 0, nq, nq + nkv, nkv, C_GROUP, t // BLK, C_WINDOW - 1,
                                 "swa_fwd", sink_rows=sink_rows, normalise=True)
            attn = _from_heads(o_hm, 1).astype(BF16)
            x = mm_rows_sharded(attn, wout_g, 0, "odd_out_proj", [bo_full[o].reshape(1, d), x],
                                ["row", "tile"], _ep_bias_residual)
            s.update(wq=wq, hm=hm, sink_rows=sink_rows, o_hm=o_hm, lse=lse, attn=attn)
        s["x_mid"] = x
        (w1_g, w2_g), token = weights_mid(l, x)
        s.update(win=win_g, wout=wout_g, w1=w1_g, w2=w2_g)
        h2 = rms_fwd(x, norm_mlp_g[l] + token, "norm_mlp_fwd")
        u, act = mm_cols_sharded(h2, w1_g, 0, "mlp_up", epilogue=_ep_relu2, n_out=2)
        x = mm_rows_sharded(act, w2_g, 0, "mlp_down", [x], ["tile"], _ep_residual)
        s.update(h2=h2, u=u, act=act)
        saved.append(s)

    dx, dxb, dg_final, loss_part = loss_head(x, final_norm_g, target, "loss_head")
    big = []
    small = {"final": dg_final, "loss": loss_part, "mix": [None] * depth, "mlp": [None] * depth,
             "lb": {}, "gn": {}, "sinks": {}, "bq": {}, "bo": {}}
    for l in reversed(range(depth)):
        s = saved[l]
        win_g, wout_g, w1_g, w2_g = s["win"], s["wout"], s["w1"], s["w2"]
        big.append(("w2", l, mm_tn(s["act"], dxb, "mlp_down_dw").reshape(N_DEV, -1, d)))
        du = mm_nt_rows_sharded(dxb, w2_g, 0, "mlp_down_dx", extras=[s["u"]],
                                epilogue=_ep_relu2_bwd, out_dtype=BF16)
        big.append(("w1", l, mm_tn(s["h2"], du, "mlp_up_dw", shard_cols=w1_g.shape[-1])))
        dh2 = mm_nt_cols_sharded(du, w1_g, 0, "mlp_up_dx")
        token = grads_ready(l, big[-2:])
        dx, dxb, dg, col_dx = rms_bwd(s["x_mid"], norm_mlp_g[l] + token, dh2, dx, "norm_mlp_bwd")
        small["mlp"][l] = dg
        if l % 2 == 0:
            e = l // 2
            big.append(("wout", e, mm_tn(s["mixed"], dxb, "even_out_dw").reshape(N_DEV, -1, d)))
            dmixed = mm_nt_rows_sharded(dxb, wout_g, 0, "even_out_dx")
            do_hm = _to_heads(dmixed[:, :na * A_DIM], 1, na, A_DIM)
            delta = head_delta(s["oa"], do_hm, "dilated_delta")
            dsum = None
            for (window, dil), hm in zip(A_BRANCHES, s["hms"]):
                dq, dk, dv = band_bwd(hm, 0, na, 2 * na, _perm(do_hm, dil).astype(BF16),
                                      _perm(s["lse"], dil), _perm(delta, dil), na, 1,
                                      t // dil // BLK, window // dil, f"dilated_bwd_{dil}")
                part = _from_heads(jnp.concatenate([dq, dk, dv], axis=0), dil)
                dsum = part if dsum is None else dsum + part
            dqkv_a = rope_call(dsum, tabs_a, a_w, 2 * na, True, "rope_a_bwd")[0]
            dqb, dfb, dib, dgb, dgn, dlb = hgrn_bwd(s["proj"], 3 * na, nbh, s["lb"], s["gn"],
                                                    s["opre"], s["states"], s["amats"], dmixed, na,
                                                    "hgrn_bwd")
            small["gn"][e] = dgn
            small["lb"][e] = dlb
            dproj = jnp.concatenate([dqkv_a, dqb, dfb, dib, dgb], axis=1)
            big.append(("win", e, mm_tn(s["h"], dproj, "even_in_dw", shard_cols=win_g.shape[-1])))
            dh = mm_nt_cols_sharded(dproj, win_g, 0, "even_in_dx")
        else:
            o = l // 2
            small["bo"][o] = col_dx
            big.append(("wo", o, mm_tn(s["attn"], dxb, "odd_out_dw").reshape(N_DEV, -1, d)))
            dattn = mm_nt_rows_sharded(dxb, wout_g, 0, "odd_out_dx")
            do_hm = _to_heads(dattn, 1, nq, C_DIM)
            dq, dk, dv, dsink = band_bwd(s["hm"], 0, nq, nq + nkv, do_hm, s["lse"], s["o_hm"],
                                         nkv, C_GROUP, t // BLK, C_WINDOW - 1, "swa_bwd",
                                         sink_rows=s["sink_rows"], delta_from_o=True)
            small["sinks"][o] = dsink
            dqkv = _from_heads(jnp.concatenate([dq, dk, dv], axis=0), 1)
            dproj, dbq = rope_call(dqkv, tabs_c, c_w, (nq + nkv) * C_DIM // LANES, True,
                                   "rope_c_bwd", col_sum=True)
            small["bq"][o] = dbq
            dwq = mm_tn(s["h"], dproj, "odd_qkv_dw", tn=512)
            big.append(("wqkv", o, dwq.reshape(d, N_DEV, -1).transpose(1, 0, 2)))
            dh = mm_nt_plain(dproj, s["wq"], "odd_qkv_dx", tk=c_w)
        token = grads_ready(l, big[-2:])
        dx, dxb, dg, _ = rms_bwd(s["x_in"], norm_mix_g[l] + token, dh, dx, "norm_mix_bwd")
        small["mix"][l] = dg
    return dx, small


def kernel(x, norm_mix_g, norm_mlp_g, final_norm_g, even_w_in, even_w_out, hgrn_lb_raw, hgrn_norm_g, odd_w_qkv, odd_b_qkv, odd_sinks, odd_w_o, odd_b_o, mlp_w1, mlp_w2, loss_target, m_norm_mix_g, m_norm_mlp_g, m_final_norm_g, m_even_w_in, m_even_w_out, m_hgrn_lb_raw, m_hgrn_norm_g, m_odd_w_qkv, m_odd_b_qkv, m_odd_sinks, m_odd_w_o, m_odd_b_o, m_mlp_w1, m_mlp_w2, v_norm_mix_g, v_norm_mlp_g, v_final_norm_g, v_even_w_in, v_even_w_out, v_hgrn_lb_raw, v_hgrn_norm_g, v_odd_w_qkv, v_odd_b_qkv, v_odd_sinks, v_odd_w_o, v_odd_b_o, v_mlp_w1, v_mlp_w2):
    d = x.shape[2]
    depth = norm_mix_g.shape[0]
    n_even, n_odd = even_w_in.shape[0], odd_w_qkv.shape[0]
    xi, yi, ci = lax.axis_index("x"), lax.axis_index("y"), lax.axis_index("c")
    dev = 4 * xi + 2 * yi + ci
    core = ci.astype(jnp.int32).reshape(1)

    big_w = {"win": even_w_in, "wout": even_w_out, "wqkv": odd_w_qkv, "wo": odd_w_o,
             "w1": mlp_w1, "w2": mlp_w2}
    big_m = {"win": m_even_w_in, "wout": m_even_w_out, "wqkv": m_odd_w_qkv, "wo": m_odd_w_o,
             "w1": m_mlp_w1, "w2": m_mlp_w2}
    big_v = {"win": v_even_w_in, "wout": v_even_w_out, "wqkv": v_odd_w_qkv, "wo": v_odd_w_o,
             "w1": v_mlp_w1, "w2": v_mlp_w2}
    kinds = list(big_w)
    casts = {k: cast_bf16(big_w[k], f"cast_{k}") for k in kinds}

    def layer_shards(l):
        a, b = ("win", "wout") if l % 2 == 0 else ("wqkv", "wo")
        return [casts[a][l // 2], casts[b][l // 2], casts["w1"][l], casts["w2"][l]]

    bq_w, bo_w = odd_b_qkv.shape[1], odd_b_o.shape[1]
    bq_mine = lax.dynamic_update_slice(jnp.zeros((n_odd, N_DEV * bq_w), F32), odd_b_qkv,
                                       (0, dev * bq_w))
    bo_mine = lax.dynamic_update_slice(jnp.zeros((n_odd, N_DEV * bo_w), F32), odd_b_o,
                                       (0, dev * bo_w))
    biases = all_reduce_small(_pack([bq_mine, bo_mine]), "gather_biases")
    bq_full, bo_full = _unpack(biases, [bq_mine.shape, bo_mine.shape])

    first_level = {}
    second_level = {}
    ready = {}
    zero = jnp.zeros((), F32)

    def start_first_level(key, shards, after):
        started = gather_start(shards, after, f"gather_start_{key}")
        first_level[key] = started[:-1]
        return started[-1]

    def finish_first_level(key, after):
        send_sems, recv_sems, local_sems, *bufs = first_level.pop(key)
        n = len(bufs) // 2
        bufs = gather_wait(send_sems, recv_sems, local_sems, bufs[:n], bufs[n:], after,
                           f"gather_wait_{key}")
        return bufs[n:]

    def weights_get(l, after):
        if l == 0:
            shards = layer_shards(0)
            mixer = all_gather(shards[:2], "gather_layer_0_mixer")
            token = start_first_level("0_mlp", shards[2:], [mixer[0], biases])
            for ahead in range(1, min(depth, 3)):
                token = start_first_level(ahead, layer_shards(ahead), [token])
            return [g[:, None] for g in mixer], token[0, 0]
        if l == 1:
            gathered = forward_now(finish_first_level(1, after), "gather_forward_1")
        else:
            send_sems, recv_sems, *lands = second_level.pop(l)
            gathered = forward_wait(send_sems, recv_sems, lands, after,
                                    f"gather_forward_wait_{l}")
        ready[l] = gathered[2:]
        return [g[:, None] for g in gathered[:2]], zero

    def weights_mid(l, after):
        token = zero
        if l == 0:
            ready[0] = forward_now(finish_first_level("0_mlp", after), "gather_forward_0_mlp")
            order_after = ready[0][0]
        elif l + 1 < depth:
            started = forward_start(finish_first_level(l + 1, after),
                                    f"gather_forward_start_{l + 1}")
            second_level[l + 1] = started[:-1]
            token = started[-1][0, 0]
            order_after = started[-1]
        if l + 3 < depth:
            token = token + start_first_level(l + 3, layer_shards(l + 3), [order_after])[0, 0]
        return [g[:, None] for g in ready.pop(l)], token

    exchanging = []
    scattering = []

    def finish_exchange(after):
        tag, names, layer_idx, (send_sems, recv_sems, *bufs) = exchanging.pop()
        n = len(names)
        grads, received = sibling_wait(send_sems, recv_sems, bufs[:n], bufs[n:], after,
                                       f"scatter_d2d_wait_{tag}")
        partials = [pair_add(g, r, core, f"pair_add_{k}")
                    for k, g, r in zip(names, grads, received)]
        started = scatter_start(partials, f"scatter_start_{tag}")
        scattering.append((tag, names, layer_idx, started[:-1]))
        return started[-1][0, 0]

    def grads_ready(l, group):
        names = [k for k, _, _ in group]
        grads = [g for _, _, g in group]
        tag = f"{l}_{names[0]}"
        token = finish_exchange(grads[0]) if exchanging else zero
        started = sibling_start(grads, f"scatter_d2d_start_{tag}")
        exchanging.append((tag, names, [li for _, li, _ in group], started[:-1]))
        return token + started[-1][0, 0]

    lbs, soft = lower_bounds_fwd(hgrn_lb_raw, "lower_bounds")

    dx, small = local_step(x[0], loss_target[0], norm_mix_g, norm_mlp_g, final_norm_g, lbs,
                           hgrn_norm_g, odd_sinks, bq_full, bo_full, weights_get, weights_mid,
                           grads_ready)
    finish_exchange(dx)

    parts = ([small["mix"][l] for l in range(depth)] + [small["mlp"][l] for l in range(depth)]
             + [small["final"]] + [small["lb"][e] for e in range(n_even)]
             + [small["gn"][e] for e in range(n_even)] + [small["sinks"][o] for o in range(n_odd)]
             + [small["bq"][o] for o in range(n_odd)] + [small["bo"][o] for o in range(n_odd)]
             + [small["loss"]])
    shapes = ([(depth, d)] * 2 + [(d,), hgrn_lb_raw.shape, hgrn_norm_g.shape, odd_sinks.shape,
              (n_odd, N_DEV * bq_w), (n_odd, N_DEV * bo_w), (1, LANES)])
    g_mix, g_mlp, g_final, d_lbs, g_gn, g_sinks, g_bq_full, g_bo_full, loss_v = _unpack(
        all_reduce_small(_pack(parts), "reduce_small"), shapes)
    g_lb = lower_bounds_bwd(soft, d_lbs, "lower_bounds_bwd")
    g_bq = lax.dynamic_slice(g_bq_full, (0, dev * bq_w), (n_odd, bq_w))
    g_bo = lax.dynamic_slice(g_bo_full, (0, dev * bo_w), (n_odd, bo_w))
    loss = loss_v[0, 0]

    small_names = ["norm_mix_g", "norm_mlp_g", "final_norm_g", "hgrn_lb_raw", "hgrn_norm_g",
                   "odd_b_qkv", "odd_sinks", "odd_b_o"]
    small_w = [norm_mix_g, norm_mlp_g, final_norm_g, hgrn_lb_raw, hgrn_norm_g, odd_b_qkv,
               odd_sinks, odd_b_o]
    small_m = [m_norm_mix_g, m_norm_mlp_g, m_final_norm_g, m_hgrn_lb_raw, m_hgrn_norm_g,
               m_odd_b_qkv, m_odd_sinks, m_odd_b_o]
    small_v = [v_norm_mix_g, v_norm_mlp_g, v_final_norm_g, v_hgrn_lb_raw, v_hgrn_norm_g,
               v_odd_b_qkv, v_odd_sinks, v_odd_b_o]
    small_g = [g_mix, g_mlp, g_final, g_lb, g_gn, g_bq, g_sinks, g_bo]
    sshapes = [w.shape for w in small_w]
    sd, sm, sv = adamw_small(_pack(small_w), _pack(small_g), _pack(small_m), _pack(small_v),
                             "adamw_small")
    res = {}
    for name, g, dl, m2, v2 in zip(small_names, small_g, _unpack(sd, sshapes),
                                   _unpack(sm, sshapes), _unpack(sv, sshapes)):
        res[name] = (g.reshape(dl.shape), dl, m2, v2)

    landed = {k: [None] * big_w[k].shape[0] for k in kinds}
    for tag, names, layer_idx, (send_sems, recv_sems, local_sems, *bufs) in scattering:
        n = len(names)
        lands = scatter_wait(send_sems, recv_sems, local_sems, bufs[:n], bufs[n:], dx,
                             f"scatter_wait_{tag}")
        for k, li, land in zip(names, layer_idx, lands):
            landed[k][li] = land
    long_names = {"win": "even_w_in", "wout": "even_w_out", "wqkv": "odd_w_qkv", "wo": "odd_w_o",
                  "w1": "mlp_w1", "w2": "mlp_w2"}
    for k in kinds:
        res[long_names[k]] = tuple(adamw_layers(landed[k], big_w[k], big_m[k], big_v[k],
                                                f"adamw_{k}"))

    order = ["norm_mix_g", "norm_mlp_g", "final_norm_g", "even_w_in", "even_w_out", "hgrn_lb_raw",
             "hgrn_norm_g", "odd_w_qkv", "odd_b_qkv", "odd_sinks", "odd_w_o", "odd_b_o", "mlp_w1",
             "mlp_w2"]
    outs = [loss, dx[None]]
    for j in range(4):
        outs += [res[n][j] for n in order]
    return tuple(outs)
```
